```python
import math
import jax
import jax.numpy as jnp
from jax import lax
import numpy as np

D_MODEL = 4096
BATCH = 4
SEQ = 4096
DEPTH = 4

N_MIXERS = 4
D_PLE = 256
NORM_EPS = 1e-6
ROPE_THETA = 10000.0

MB_EXPAND = 2
MB_D_INNER = MB_EXPAND * D_MODEL
MB_HEAD_DIM = 64
MB_N_HEADS = MB_D_INNER // MB_HEAD_DIM
MB_N_GROUPS = 8
MB_D_STATE = 128
MB_CONV = 4
MB_CHUNK = 128
MB_DT_MIN = 1e-3
MB_DT_MAX = 1e-1

NSA_HEAD_DIM = 128
NSA_N_HEADS = D_MODEL // NSA_HEAD_DIM
NSA_N_KV = 4
NSA_CMP_BLOCK = 32
NSA_CMP_STRIDE = 16
NSA_CMP_HIDDEN = 512
NSA_SEL_BLOCK = 64
NSA_TOPK = 16
NSA_WINDOW = 512
NSA_Q_BLOCK = 64
NSA_FORCED_SCORE = 1e9

HG_HEAD_DIM = 128
HG_N_HEADS = D_MODEL // HG_HEAD_DIM
HG_CHUNK = 32

RW_HEAD_DIM = 64
RW_N_HEADS = D_MODEL // RW_HEAD_DIM
RW_DECAY_LORA = max(32, int(round(1.8 * math.sqrt(D_MODEL) / 32)) * 32)
RW_AAA_LORA = max(32, int(round(1.8 * math.sqrt(D_MODEL) / 32)) * 32)
RW_GATE_LORA = max(32, int(round(0.6 * D_MODEL ** 0.8 / 32)) * 32)
RW_LN_EPS = 64e-5

FFN_DENSE = 2 * D_MODEL
MOE_EXPERTS = 8
MOE_TOPK = 2
MOE_D_EXPERT = 3 * D_MODEL // 8

F32 = jnp.float32

kernel_name = 'hybrid_ssd_nsa_hgrn2_rwkv7_moe_trunk'


def rmsnorm(x, gain):
    xf = x.astype(F32)
    y = xf * lax.rsqrt(jnp.mean(xf * xf, axis=-1, keepdims=True) + NORM_EPS)
    return (y * gain.astype(F32)).astype(x.dtype)


def rope_cos_sin(pos, dim):
    inv = ROPE_THETA ** (-(jnp.arange(0, dim, 2, dtype=F32) / dim))
    ang = pos.astype(F32)[:, None] * inv[None, :]
    return jnp.cos(ang), jnp.sin(ang)


def apply_rope(x, cos, sin):
    xf = x.astype(F32)
    x1, x2 = jnp.split(xf, 2, axis=-1)
    c = cos[:, None, :]
    s = sin[:, None, :]
    return jnp.concatenate([x1 * c - x2 * s, x1 * s + x2 * c], axis=-1).astype(x.dtype)


def masked_softmax(s, mask):
    s = jnp.where(mask, s.astype(F32), -jnp.inf)
    m = jnp.max(s, axis=-1, keepdims=True)
    m = jnp.where(jnp.isfinite(m), m, 0.0)
    e = jnp.exp(s - m)
    d = jnp.sum(e, axis=-1, keepdims=True)
    return e / jnp.where(d > 0, d, 1.0)


def causal_depthwise_conv(x, w, b):
    k_width = w.shape[0]
    s_len = x.shape[1]
    xp = jnp.pad(x, ((0, 0), (k_width - 1, 0), (0, 0)))
    y = b
    for j in range(k_width):
        y = y + w[j] * xp[:, j:j + s_len]
    return y


def swiglu(x, w_in, w_out):
    gate, up = jnp.split(x @ w_in, 2, axis=-1)
    return (jax.nn.silu(gate) * up) @ w_out


def moe_swiglu(x, router, w_in, w_out):
    logits = (x @ router).astype(F32)
    top_val, top_idx = lax.top_k(logits, MOE_TOPK)
    top_w = jax.nn.softmax(top_val, axis=-1)
    combine = jnp.sum(jax.nn.one_hot(top_idx, MOE_EXPERTS, dtype=F32) * top_w[..., None], axis=-2)
    out = jnp.zeros(x.shape, F32)
    for e in range(MOE_EXPERTS):
        out = out + combine[..., e:e + 1] * swiglu(x, w_in[e], w_out[e]).astype(F32)
    return out.astype(x.dtype)


def mamba2_mixer(u, w_in, conv_w, conv_b, dt_bias, a_log, d_skip, norm_w, w_out):
    bsz, s_len, _ = u.shape
    G, E, P, N, L = MB_N_GROUPS, MB_N_HEADS // MB_N_GROUPS, MB_HEAD_DIM, MB_D_STATE, MB_CHUNK
    nc = s_len // L
    z, xbc, dt = jnp.split(u @ w_in, [MB_D_INNER, 2 * MB_D_INNER + 2 * G * N], axis=-1)
    xbc = jax.nn.silu(causal_depthwise_conv(xbc, conv_w, conv_b))
    xs, b_in, c_out = jnp.split(xbc, [MB_D_INNER, MB_D_INNER + G * N], axis=-1)
    dt = jax.nn.softplus(dt.astype(F32) + dt_bias.astype(F32))
    a = (dt * -jnp.exp(a_log.astype(F32))).reshape(bsz, nc, L, G, E)
    xs_h = xs.astype(F32).reshape(bsz, s_len, MB_N_HEADS, P)
    xdt = xs_h * dt[..., None]
    xdt_c = jnp.moveaxis(xdt.reshape(bsz, nc, L, G, E, P), 1, 0)
    a_c = jnp.transpose(a, (1, 0, 3, 4, 2))
    b_c = jnp.moveaxis(b_in.astype(F32).reshape(bsz, nc, L, G, N), 1, 0)
    c_c = jnp.moveaxis(c_out.astype(F32).reshape(bsz, nc, L, G, N), 1, 0)
    causal = jnp.tril(jnp.ones((L, L), dtype=bool))

    def chunk_step(state, inp):
        x_k, a_k, b_k, c_k = inp
        a_cum = jnp.cumsum(a_k, axis=-1)
        seg = a_cum[..., :, None] - a_cum[..., None, :]
        decay = jnp.exp(jnp.where(causal, seg, -jnp.inf))
        cb = jnp.einsum('blgn,bsgn->bgls', c_k, b_k)
        y_diag = jnp.einsum('bgls,bgels,bsgep->blgep', cb, decay, x_k)
        y_off = jnp.einsum('blgn,bgepn,bgel->blgep', c_k, state, jnp.exp(a_cum))
        to_end = jnp.exp(a_cum[..., -1:] - a_cum)
        new_state = state * jnp.exp(a_cum[..., -1])[..., None, None] + jnp.einsum('bsgn,bges,bsgep->bgepn', b_k, to_end, x_k)
        return new_state, y_diag + y_off

    state0 = jnp.zeros((bsz, G, E, P, N), F32)
    _, y = lax.scan(chunk_step, state0, (xdt_c, a_c, b_c, c_c))
    y = jnp.moveaxis(y, 0, 1).reshape(bsz, s_len, MB_N_HEADS, P)
    y = y + xs_h * d_skip.astype(F32)[:, None]
    y = y.reshape(bsz, s_len, MB_D_INNER) * jax.nn.silu(z.astype(F32))
    y = y.reshape(bsz, s_len, G, MB_D_INNER // G)
    y = y * lax.rsqrt(jnp.mean(y * y, axis=-1, keepdims=True) + NORM_EPS)
    y = y.reshape(bsz, s_len, MB_D_INNER) * norm_w.astype(F32)
    return y.astype(u.dtype) @ w_out


def nsa_compress(x, pos_emb, w1, w2):
    s_len = x.shape[1]
    n_cmp = (s_len - NSA_CMP_BLOCK) // NSA_CMP_STRIDE + 1
    idx = jnp.arange(n_cmp)[:, None] * NSA_CMP_STRIDE + jnp.arange(NSA_CMP_BLOCK)[None, :]
    blocks = x[:, idx] + pos_emb[None, None, :, None, :]
    hid = jax.nn.silu(jnp.einsum('bnlgd,lde->bnge', blocks, w1))
    return jnp.einsum('bnge,ed->bngd', hid, w2)


def nsa_mixer(u, w_in, cmp_pos_k, cmp_pos_v, cmp_k_w1, cmp_k_w2, cmp_v_w1, cmp_v_w2, w_out):
    bsz, s_len, _ = u.shape
    H, G, dh = NSA_N_HEADS, NSA_N_KV, NSA_HEAD_DIM
    R = H // G
    QB, SB, W = NSA_Q_BLOCK, NSA_SEL_BLOCK, NSA_WINDOW
    n_qb, n_sel = s_len // QB, s_len // SB
    topn = min(NSA_TOPK, n_sel)
    scale = dh ** -0.5
    kvw = G * dh
    points = [H * dh + j * kvw for j in range(7)]
    q, k_cmp, v_cmp, k_slc, v_slc, k_win, v_win, gates = jnp.split(u @ w_in, points, axis=-1)
    q = q.reshape(bsz, s_len, H, dh)
    k_cmp, v_cmp, k_slc, v_slc, k_win, v_win = [t.reshape(bsz, s_len, G, dh) for t in (k_cmp, v_cmp, k_slc, v_slc, k_win, v_win)]
    gates = jax.nn.sigmoid(gates.astype(F32)).reshape(bsz, s_len, G, R, 3)
    cos, sin = rope_cos_sin(jnp.arange(s_len), dh)
    q = apply_rope(q, cos, sin)
    k_slc = apply_rope(k_slc, cos, sin)
    k_win = apply_rope(k_win, cos, sin)
    kc = nsa_compress(k_cmp, cmp_pos_k, cmp_k_w1, cmp_k_w2)
    vc = nsa_compress(v_cmp, cmp_pos_v, cmp_v_w1, cmp_v_w2)
    n_cmp = kc.shape[1]
    cmp_start = jnp.arange(n_cmp) * NSA_CMP_STRIDE
    cmp_end = cmp_start + NSA_CMP_BLOCK - 1
    cos_c, sin_c = rope_cos_sin(cmp_end, dh)
    kc = apply_rope(kc, cos_c, sin_c)
    sel_start = jnp.arange(n_sel) * SB
    overlap = jnp.clip(jnp.minimum(cmp_start[:, None] + NSA_CMP_BLOCK, sel_start[None, :] + SB) - jnp.maximum(cmp_start[:, None], sel_start[None, :]), 0, None).astype(F32) / NSA_CMP_BLOCK
    k_sel_blk = jnp.transpose(k_slc.reshape(bsz, n_sel, SB, G, dh), (0, 3, 1, 2, 4))
    v_sel_blk = jnp.transpose(v_slc.reshape(bsz, n_sel, SB, G, dh), (0, 3, 1, 2, 4))
    k_win_pad = jnp.pad(k_win, ((0, 0), (W, 0), (0, 0), (0, 0)))
    v_win_pad = jnp.pad(v_win, ((0, 0), (W, 0), (0, 0), (0, 0)))
    bi = jnp.arange(bsz)[:, None, None, None]
    gi = jnp.arange(G)[None, :, None, None]
    blk_ids = jnp.arange(n_sel)
    q_blocks = jnp.moveaxis(q.reshape(bsz, n_qb, QB, G, R, dh), 1, 0)
    g_blocks = jnp.moveaxis(gates.reshape(bsz, n_qb, QB, G, R, 3), 1, 0)

    def block_fn(args):
        qb, q_blk, g_blk = args
        t = qb * QB + jnp.arange(QB)
        p_c = masked_softmax(jnp.einsum('bqgrd,bngd->bgrqn', q_blk, kc) * scale, cmp_end[None, :] <= t[:, None])
        o_c = jnp.einsum('bgrqn,bngd->bqgrd', p_c, vc)
        cur = t // SB
        imp = jnp.einsum('bgrqn,ns->bgqs', p_c, overlap)
        forced = (blk_ids[None, :] == 0) | (blk_ids[None, :] == cur[:, None]) | (blk_ids[None, :] == cur[:, None] - 1)
        imp = jnp.where(forced, NSA_FORCED_SCORE, imp)
        imp = jnp.where(blk_ids[None, :] > cur[:, None], -jnp.inf, imp)
        _, idx = lax.top_k(imp, topn)
        ks = k_sel_blk[bi, gi, idx]
        vs = v_sel_blk[bi, gi, idx]
        key_pos = idx[..., None] * SB + jnp.arange(SB)
        valid = (key_pos <= t[None, None, :, None, None])[:, :, None]
        s_s = jnp.einsum('bqgrd,bgqknd->bgrqkn', q_blk, ks) * scale
        p_s = masked_softmax(s_s.reshape(bsz, G, R, QB, topn * SB), valid.reshape(bsz, G, 1, QB, topn * SB)).reshape(s_s.shape)
        o_s = jnp.einsum('bgrqkn,bgqknd->bqgrd', p_s, vs)
        kw = lax.dynamic_slice_in_dim(k_win_pad, qb * QB, QB + W, axis=1)
        vw = lax.dynamic_slice_in_dim(v_win_pad, qb * QB, QB + W, axis=1)
        kpos = qb * QB - W + jnp.arange(QB + W)
        valid_w = (kpos[None, :] <= t[:, None]) & (kpos[None, :] > t[:, None] - W) & (kpos[None, :] >= 0)
        p_w = masked_softmax(jnp.einsum('bqgrd,bkgd->bgrqk', q_blk, kw) * scale, valid_w)
        o_w = jnp.einsum('bgrqk,bkgd->bqgrd', p_w, vw)
        o = g_blk[..., 0:1] * o_c + g_blk[..., 1:2] * o_s + g_blk[..., 2:3] * o_w
        return o.astype(q_blk.dtype)

    o = lax.map(block_fn, (jnp.arange(n_qb), q_blocks, g_blocks))
    o = jnp.moveaxis(o, 0, 1).reshape(bsz, s_len, H * dh)
    return o @ w_out


def hgrn2_mixer(u, w_in, lower_bound, norm_w, w_out):
    bsz, s_len, d = u.shape
    H, dk, C = HG_N_HEADS, HG_HEAD_DIM, HG_CHUNK
    nc = s_len // C
    q, f, i_in, g = jnp.split(u @ w_in, 4, axis=-1)
    lb = lower_bound.astype(F32)
    f = lb + (1.0 - lb) * jax.nn.sigmoid(f.astype(F32))
    log_f = jnp.log(f)
    k = 1.0 - f
    q = jax.nn.silu(q.astype(F32))

    def to_chunks(t):
        return jnp.transpose(t.reshape(bsz, nc, C, H, dk), (1, 0, 3, 2, 4))

    causal = jnp.tril(jnp.ones((C, C), dtype=bool))

    def chunk_step(state, inp):
        q_k, k_k, v_k, lf_k = inp
        b = jnp.cumsum(lf_k, axis=2)
        b_end = b[:, :, -1:, :]
        q_dec = q_k * jnp.exp(b)
        k_dec = k_k * jnp.exp(-b)
        scores = jnp.where(causal, jnp.einsum('bhtd,bhsd->bhts', q_dec, k_dec), 0.0)
        o = jnp.einsum('bhts,bhse->bhte', scores, v_k) + jnp.einsum('bhtd,bhde->bhte', q_dec, state)
        new_state = jnp.exp(b_end[:, :, 0, :])[..., None] * state + jnp.einsum('bhsd,bhse->bhde', k_k * jnp.exp(b_end - b), v_k)
        return new_state, o

    state0 = jnp.zeros((bsz, H, dk, dk), F32)
    _, o = lax.scan(chunk_step, state0, (to_chunks(q), to_chunks(k), to_chunks(i_in.astype(F32)), to_chunks(log_f)))
    o = jnp.transpose(o, (1, 0, 3, 2, 4)).reshape(bsz, s_len, H, dk)
    o = o * lax.rsqrt(jnp.mean(o * o, axis=-1, keepdims=True) + NORM_EPS) * norm_w.astype(F32)
    o = o.reshape(bsz, s_len, d) * jax.nn.silu(g.astype(F32))
    return o.astype(u.dtype) @ w_out


def rwkv7_mixer(u, mu, w_rkv, w0, w1, w2, a0, a1, a2, g1, g2, k_k, k_a, r_k, ln_w, ln_b, w_out):
    bsz, s_len, d = u.shape
    H, N = RW_N_HEADS, RW_HEAD_DIM
    dx = jnp.pad(u, ((0, 0), (1, 0), (0, 0)))[:, :-1] - u
    xr = u + dx * mu[0]
    xw = u + dx * mu[1]
    xk = u + dx * mu[2]
    xv = u + dx * mu[3]
    xa = u + dx * mu[4]
    xg = u + dx * mu[5]
    r = (xr @ w_rkv[0]).astype(F32)
    k = (xk @ w_rkv[1]).astype(F32)
    v = (xv @ w_rkv[2]).astype(F32)
    w = -jax.nn.softplus(-(w0 + jnp.tanh(xw @ w1) @ w2).astype(F32)) - 0.5
    decay = jnp.exp(-jnp.exp(w))
    a = jax.nn.sigmoid((a0 + (xa @ a1) @ a2).astype(F32))
    g = (jax.nn.sigmoid(xg @ g1) @ g2).astype(F32)
    kk = (k * k_k.astype(F32)).reshape(bsz, s_len, H, N)
    kk = kk / jnp.maximum(jnp.sqrt(jnp.sum(kk * kk, axis=-1, keepdims=True)), 1e-12)
    k = k * (1.0 + (a - 1.0) * k_a.astype(F32))
    r_h = r.reshape(bsz, s_len, H, N)
    k_h = k.reshape(bsz, s_len, H, N)
    v_h = v.reshape(bsz, s_len, H, N)
    a_h = a.reshape(bsz, s_len, H, N)
    w_h = decay.reshape(bsz, s_len, H, N)

    def step(state, inp):
        r_t, w_t, k_t, v_t, kk_t, a_t = inp
        removed = jnp.einsum('bhvk,bhk->bhv', state, kk_t)
        state = state * w_t[:, :, None, :] - removed[..., None] * (kk_t * a_t)[:, :, None, :] + v_t[..., None] * k_t[:, :, None, :]
        return state, jnp.einsum('bhvk,bhk->bhv', state, r_t)

    seq_first = lambda t: jnp.moveaxis(t, 1, 0)
    state0 = jnp.zeros((bsz, H, N, N), F32)
    _, y = lax.scan(step, state0, (seq_first(r_h), seq_first(w_h), seq_first(k_h), seq_first(v_h), seq_first(kk), seq_first(a_h)))
    y = jnp.moveaxis(y, 0, 1)
    mean = jnp.mean(y, axis=-1, keepdims=True)
    var = jnp.mean(jnp.square(y - mean), axis=-1, keepdims=True)
    y = ((y - mean) * lax.rsqrt(var + RW_LN_EPS)).reshape(bsz, s_len, d) * ln_w.astype(F32) + ln_b.astype(F32)
    bonus = jnp.sum(r_h * k_h * r_k.astype(F32), axis=-1, keepdims=True) * v_h
    y = y + bonus.reshape(bsz, s_len, d)
    return (y * g).astype(u.dtype) @ w_out


def _normal(key, shape, scale):
    return jax.random.normal(key, shape, F32) * scale


def setup_inputs(seed: int = 0) -> dict:
    key = jax.random.key(seed)
    kit = iter(list(jax.random.split(key, 80)))
    nk = lambda: next(kit)
    D = D_MODEL
    inv = lambda n: n ** -0.5
    mb_in_cols = 2 * MB_D_INNER + 2 * MB_N_GROUPS * MB_D_STATE + MB_N_HEADS
    mb_conv_ch = MB_D_INNER + 2 * MB_N_GROUPS * MB_D_STATE
    nsa_in_cols = NSA_N_HEADS * NSA_HEAD_DIM + 6 * NSA_N_KV * NSA_HEAD_DIM + 3 * NSA_N_HEADS
    u01 = jax.random.uniform(nk(), (MB_N_HEADS,), F32)
    dt0 = jnp.exp(u01 * (math.log(MB_DT_MAX) - math.log(MB_DT_MIN)) + math.log(MB_DT_MIN))
    return {
        'x': _normal(nk(), (BATCH, SEQ, D), 1.0),
        'p': _normal(nk(), (DEPTH, BATCH, SEQ, D_PLE), 1.0),
        'norm_mix': 1.0 + _normal(nk(), (DEPTH, D), 0.02),
        'norm_ffn': 1.0 + _normal(nk(), (DEPTH, D), 0.02),
        'norm_pl': 1.0 + _normal(nk(), (DEPTH, D), 0.02),
        'pl_proj': _normal(nk(), (DEPTH, D_PLE, D), inv(D_PLE)),
        'pl_gate': _normal(nk(), (DEPTH, D, D), inv(D)),
        'norm_final': 1.0 + _normal(nk(), (D,), 0.02),
        'mb_w_in': _normal(nk(), (D, mb_in_cols), inv(D)),
        'mb_conv_w': _normal(nk(), (MB_CONV, mb_conv_ch), inv(MB_CONV)),
        'mb_conv_b': _normal(nk(), (mb_conv_ch,), 0.02),
        'mb_dt_bias': dt0 + jnp.log(-jnp.expm1(-dt0)),
        'mb_a_log': jnp.log(jax.random.uniform(nk(), (MB_N_HEADS,), F32, 1.0, 16.0)),
        'mb_d_skip': 1.0 + _normal(nk(), (MB_N_HEADS,), 0.1),
        'mb_norm_w': 1.0 + _normal(nk(), (MB_D_INNER,), 0.02),
        'mb_w_out': _normal(nk(), (MB_D_INNER, D), inv(MB_D_INNER)),
        'nsa_w_in': _normal(nk(), (D, nsa_in_cols), inv(D)),
        'nsa_cmp_pos_k': _normal(nk(), (NSA_CMP_BLOCK, NSA_HEAD_DIM), 0.02),
        'nsa_cmp_pos_v': _normal(nk(), (NSA_CMP_BLOCK, NSA_HEAD_DIM), 0.02),
        'nsa_cmp_k_w1': _normal(nk(), (NSA_CMP_BLOCK, NSA_HEAD_DIM, NSA_CMP_HIDDEN), inv(NSA_CMP_BLOCK * NSA_HEAD_DIM)),
        'nsa_cmp_k_w2': _normal(nk(), (NSA_CMP_HIDDEN, NSA_HEAD_DIM), inv(NSA_CMP_HIDDEN)),
        'nsa_cmp_v_w1': _normal(nk(), (NSA_CMP_BLOCK, NSA_HEAD_DIM, NSA_CMP_HIDDEN), inv(NSA_CMP_BLOCK * NSA_HEAD_DIM)),
        'nsa_cmp_v_w2': _normal(nk(), (NSA_CMP_HIDDEN, NSA_HEAD_DIM), inv(NSA_CMP_HIDDEN)),
        'nsa_w_out': _normal(nk(), (NSA_N_HEADS * NSA_HEAD_DIM, D), inv(D)),
        'hg_w_in': _normal(nk(), (D, 4 * D), inv(D)),
        'hg_lb_logits': 1.0 + _normal(nk(), (DEPTH, D), 0.02),
        'hg_norm_w': 1.0 + _normal(nk(), (HG_HEAD_DIM,), 0.02),
        'hg_w_out': _normal(nk(), (D, D), inv(D)),
        'rw_mu': jax.random.uniform(nk(), (6, D), F32),
        'rw_w_rkv': _normal(nk(), (3, D, D), inv(D)),
        'rw_w0': jax.random.uniform(nk(), (D,), F32, -6.0, -0.5),
        'rw_w1': _normal(nk(), (D, RW_DECAY_LORA), inv(D)),
        'rw_w2': _normal(nk(), (RW_DECAY_LORA, D), 0.1 * inv(RW_DECAY_LORA)),
        'rw_a0': _normal(nk(), (D,), 0.1),
        'rw_a1': _normal(nk(), (D, RW_AAA_LORA), inv(D)),
        'rw_a2': _normal(nk(), (RW_AAA_LORA, D), 0.1 * inv(RW_AAA_LORA)),
        'rw_g1': _normal(nk(), (D, RW_GATE_LORA), inv(D)),
        'rw_g2': _normal(nk(), (RW_GATE_LORA, D), inv(RW_GATE_LORA)),
        'rw_k_k': 0.85 + _normal(nk(), (D,), 0.02),
        'rw_k_a': 1.0 + _normal(nk(), (D,), 0.02),
        'rw_r_k': _normal(nk(), (RW_N_HEADS, RW_HEAD_DIM), 0.1),
        'rw_ln_w': 1.0 + _normal(nk(), (D,), 0.02),
        'rw_ln_b': _normal(nk(), (D,), 0.02),
        'rw_w_out': _normal(nk(), (D, D), inv(D)),
        'ffn0_w_in': _normal(nk(), (D, 2 * FFN_DENSE), inv(D)),
        'ffn0_w_out': _normal(nk(), (FFN_DENSE, D), inv(FFN_DENSE)),
        'moe1_router': _normal(nk(), (D, MOE_EXPERTS), inv(D)),
        'moe1_w_in': _normal(nk(), (MOE_EXPERTS, D, 2 * MOE_D_EXPERT), inv(D)),
        'moe1_w_out': _normal(nk(), (MOE_EXPERTS, MOE_D_EXPERT, D), inv(MOE_D_EXPERT)),
        'ffn2_w_in': _normal(nk(), (D, 2 * FFN_DENSE), inv(D)),
        'ffn2_w_out': _normal(nk(), (FFN_DENSE, D), inv(FFN_DENSE)),
        'moe3_router': _normal(nk(), (D, MOE_EXPERTS), inv(D)),
        'moe3_w_in': _normal(nk(), (MOE_EXPERTS, D, 2 * MOE_D_EXPERT), inv(D)),
        'moe3_w_out': _normal(nk(), (MOE_EXPERTS, MOE_D_EXPERT, D), inv(MOE_D_EXPERT)),
    }


def reference(x, p, norm_mix, norm_ffn, norm_pl, pl_proj, pl_gate, norm_final,
              mb_w_in, mb_conv_w, mb_conv_b, mb_dt_bias, mb_a_log, mb_d_skip, mb_norm_w, mb_w_out,
              nsa_w_in, nsa_cmp_pos_k, nsa_cmp_pos_v, nsa_cmp_k_w1, nsa_cmp_k_w2, nsa_cmp_v_w1, nsa_cmp_v_w2, nsa_w_out,
              hg_w_in, hg_lb_logits, hg_norm_w, hg_w_out,
              rw_mu, rw_w_rkv, rw_w0, rw_w1, rw_w2, rw_a0, rw_a1, rw_a2, rw_g1, rw_g2, rw_k_k, rw_k_a, rw_r_k, rw_ln_w, rw_ln_b, rw_w_out,
              ffn0_w_in, ffn0_w_out, moe1_router, moe1_w_in, moe1_w_out,
              ffn2_w_in, ffn2_w_out, moe3_router, moe3_w_in, moe3_w_out):
    lb_all = jax.nn.softmax(hg_lb_logits.astype(F32), axis=0)
    lb_all = jnp.cumsum(lb_all, axis=0) - lb_all[0]
    dense = [(ffn0_w_in, ffn0_w_out), (ffn2_w_in, ffn2_w_out)]
    moe = [(moe1_router, moe1_w_in, moe1_w_out), (moe3_router, moe3_w_in, moe3_w_out)]
    h = x
    for i in range(DEPTH):
        u = rmsnorm(h, norm_mix[i])
        kind = i % N_MIXERS
        if kind == 0:
            m = mamba2_mixer(u, mb_w_in, mb_conv_w, mb_conv_b, mb_dt_bias, mb_a_log, mb_d_skip, mb_norm_w, mb_w_out)
        elif kind == 1:
            m = nsa_mixer(u, nsa_w_in, nsa_cmp_pos_k, nsa_cmp_pos_v, nsa_cmp_k_w1, nsa_cmp_k_w2, nsa_cmp_v_w1, nsa_cmp_v_w2, nsa_w_out)
        elif kind == 2:
            m = hgrn2_mixer(u, hg_w_in, lb_all[i], hg_norm_w, hg_w_out)
        else:
            m = rwkv7_mixer(u, rw_mu, rw_w_rkv, rw_w0, rw_w1, rw_w2, rw_a0, rw_a1, rw_a2, rw_g1, rw_g2, rw_k_k, rw_k_a, rw_r_k, rw_ln_w, rw_ln_b, rw_w_out)
        h = h + m.astype(h.dtype)
        v = rmsnorm(h, norm_ffn[i])
        if i % 2 == 0:
            f = swiglu(v, dense[i // 2][0], dense[i // 2][1])
        else:
            f = moe_swiglu(v, moe[i // 2][0], moe[i // 2][1], moe[i // 2][2])
        h = h + f.astype(h.dtype)
        gate = jax.nn.sigmoid((rmsnorm(h, norm_pl[i]) @ pl_gate[i]).astype(F32))
        h = h + ((p[i] @ pl_proj[i]).astype(F32) * gate).astype(h.dtype)
    return rmsnorm(h, norm_final)
```

```python
import functools
import math

import jax
import jax.numpy as jnp
from jax import lax
from jax.experimental import pallas as pl
from jax.experimental.pallas import tpu as pltpu

F32 = jnp.float32
BF16 = jnp.bfloat16

NORM_EPS = 1e-6
ROPE_THETA = 10000.0

V7X_VMEM_BYTES = 64 * 1024 * 1024
VMEM_LIMIT_BYTES = V7X_VMEM_BYTES - 8 * 1024 * 1024
LANES = 128

MB_HEAD_DIM = 64
MB_N_GROUPS = 8
MB_D_STATE = 128
MB_CONV = 4
MB_CHUNK = 128

NSA_HEAD_DIM = 128
NSA_N_KV = 4
NSA_CMP_BLOCK = 32
NSA_CMP_STRIDE = 16
NSA_SEL_BLOCK = 64
NSA_TOPK = 16
NSA_WINDOW = 512
NSA_FORCED_SCORE = 1e9

HG_HEAD_DIM = 128
HG_CHUNK = 32

RW_HEAD_DIM = 64
RW_LN_EPS = 64e-5
RW_CHUNK = 64

MOE_TOPK = 2


def _params(*semantics):
    return pltpu.CompilerParams(dimension_semantics=semantics, vmem_limit_bytes=VMEM_LIMIT_BYTES)


def _pick(n, target):
    if n <= target:
        return n
    for c in range(target, 0, -1):
        if n % c == 0:
            return c
    return n


def _silu(x):
    return x * jax.nn.sigmoid(x)


def _rmsnorm_kernel(x_ref, g_ref, o_ref):
    x = x_ref[...]
    ms = jnp.mean(x * x, axis=-1, keepdims=True)
    o_ref[...] = (x * lax.rsqrt(ms + NORM_EPS) * g_ref[...]).astype(o_ref.dtype)


def rmsnorm(x, gain, out_dtype=BF16, name="rmsnorm"):
    m, d = x.shape
    bm = _pick(m, 256)
    return pl.pallas_call(
        _rmsnorm_kernel,
        grid=(m // bm,),
        in_specs=[pl.BlockSpec((bm, d), lambda i: (i, 0)), pl.BlockSpec((1, d), lambda i: (0, 0))],
        out_specs=pl.BlockSpec((bm, d), lambda i: (i, 0)),
        out_shape=jax.ShapeDtypeStruct((m, d), out_dtype),
        compiler_params=_params("parallel"),
        name=name,
    )(x, gain.reshape(1, d).astype(F32))


def _mm_kernel(*refs, n_w, n_extra, nk, epilogue):
    x_ref = refs[0]
    w_refs = refs[1:1 + n_w]
    e_refs = refs[1 + n_w:1 + n_w + n_extra]
    o_ref = refs[1 + n_w + n_extra]
    acc_refs = refs[2 + n_w + n_extra:]
    x = x_ref[...]
    if nk == 1:
        accs = [jnp.dot(x, w[...], preferred_element_type=F32) for w in w_refs]
        o_ref[...] = epilogue(accs, [e[...] for e in e_refs]).astype(o_ref.dtype)
        return
    k = pl.program_id(2)

    @pl.when(k == 0)
    def _():
        for a in acc_refs:
            a[...] = jnp.zeros_like(a)

    for a, w in zip(acc_refs, w_refs):
        a[...] += jnp.dot(x, w[...], preferred_element_type=F32)

    @pl.when(k == nk - 1)
    def _():
        o_ref[...] = epilogue([a[...] for a in acc_refs], [e[...] for e in e_refs]).astype(o_ref.dtype)


def _first(accs, extras):
    return accs[0]


def matmul(x, ws, n_out, *, epilogue=_first, extras=(), out_dtype=F32, bm=1024, bn=512, bk=None, name="matmul"):
    m, kdim = x.shape
    bm = _pick(m, bm)
    bn = _pick(n_out, bn)
    if bk is None:
        bk = kdim if kdim <= 4096 else _pick(kdim, 4096)
    nk = kdim // bk
    assert kdim % bk == 0 and m % bm == 0 and n_out % bn == 0
    in_specs = [pl.BlockSpec((bm, bk), lambda i, j, k: (i, k))]
    args = [x]
    for w, off in ws:
        assert off % bn == 0 and w.shape[0] == kdim
        in_specs.append(pl.BlockSpec((bk, bn), functools.partial(lambda i, j, k, o: (k, j + o), o=off // bn)))
        args.append(w)
    for arr, kind in extras:
        if kind == "mn":
            in_specs.append(pl.BlockSpec((bm, bn), lambda i, j, k: (i, j)))
        elif kind == "m":
            in_specs.append(pl.BlockSpec((bm, arr.shape[1]), lambda i, j, k: (i, 0)))
        else:
            in_specs.append(pl.BlockSpec((1, bn), lambda i, j, k: (0, j)))
        args.append(arr)
    scratch = [pltpu.VMEM((bm, bn), F32) for _ in ws] if nk > 1 else []
    kern = functools.partial(_mm_kernel, n_w=len(ws), n_extra=len(extras), nk=nk, epilogue=epilogue)
    return pl.pallas_call(
        kern,
        grid=(m // bm, n_out // bn, nk),
        in_specs=in_specs,
        out_specs=pl.BlockSpec((bm, bn), lambda i, j, k: (i, j)),
        out_shape=jax.ShapeDtypeStruct((m, n_out), out_dtype),
        scratch_shapes=scratch,
        compiler_params=_params("parallel", "parallel", "arbitrary"),
        name=name,
    )(*args)


def _ep_residual(accs, extras):
    return extras[0] + accs[0]


def _ep_swiglu(accs, extras):
    return _silu(accs[0]) * accs[1]


def _ep_bias(accs, extras):
    return accs[0] + extras[0]


def _ep_tanh(accs, extras):
    return jnp.tanh(accs[0])


def _ep_sigmoid(accs, extras):
    return jax.nn.sigmoid(accs[0])


def _ep_bias_sigmoid(accs, extras):
    return jax.nn.sigmoid(accs[0] + extras[0])


def _ep_rw_logdecay(accs, extras):
    w = -jax.nn.softplus(-(accs[0] + extras[0])) - 0.5
    return -jnp.exp(w)


def _ep_ple_gate(accs, extras):
    return extras[0] + extras[1] * jax.nn.sigmoid(accs[0])


def _conv_silu_kernel(x_ref, w_ref, b_ref, o_ref, *, k_width):
    x = x_ref[0]
    row = lax.broadcasted_iota(jnp.int32, x.shape, 0)
    y = b_ref[...] + w_ref[k_width - 1:k_width, :] * x
    for j in range(k_width - 1):
        shift = k_width - 1 - j
        xs = jnp.where(row >= shift, pltpu.roll(x, shift, 0), 0.0)
        y = y + w_ref[j:j + 1, :] * xs
    o_ref[0] = _silu(y)


def conv_silu(x, w, b):
    bsz, s_len, c = x.shape
    cb = _pick(c, 256)
    k_width = w.shape[0]
    return pl.pallas_call(
        functools.partial(_conv_silu_kernel, k_width=k_width),
        grid=(bsz, c // cb),
        in_specs=[pl.BlockSpec((1, s_len, cb), lambda b_, j: (b_, 0, j)),
                  pl.BlockSpec((k_width, cb), lambda b_, j: (0, j)),
                  pl.BlockSpec((1, cb), lambda b_, j: (0, j))],
        out_specs=pl.BlockSpec((1, s_len, cb), lambda b_, j: (b_, 0, j)),
        out_shape=jax.ShapeDtypeStruct(x.shape, F32),
        compiler_params=_params("parallel", "parallel"),
        name="mamba_conv_silu",
    )(x, w, b.reshape(1, c))


def _cumsum_rows(x, n):
    row = lax.broadcasted_iota(jnp.int32, x.shape, 0)
    s = 1
    while s < n:
        x = x + jnp.where(row >= s, pltpu.roll(x, s, 0), 0.0)
        s *= 2
    return x


def _cumsum_lanes(x, n):
    col = lax.broadcasted_iota(jnp.int32, x.shape, 1)
    s = 1
    while s < n:
        x = x + jnp.where(col >= s, pltpu.roll(x, s, 1), 0.0)
        s *= 2
    return x


def _dot_nt(a, b):
    return lax.dot_general(a, b, (((1,), (1,)), ((), ())), preferred_element_type=F32)


def _dot_tn(a, b):
    return lax.dot_general(a, b, (((0,), (0,)), ((), ())), preferred_element_type=F32)


def _ssd_kernel(xs_ref, b_ref, c_ref, z_ref, dt_ref, dtt_ref, bias_r_ref, bias_c_ref, alog_r_ref, alog_c_ref,
                dskip_ref, normw_ref, o_ref, state_ref, y_ref, *, chunk, heads, p_dim):
    @pl.when(pl.program_id(2) == 0)
    def _():
        state_ref[...] = jnp.zeros_like(state_ref)

    dt = jax.nn.softplus(dt_ref[0, 0] + bias_r_ref[0])
    dtt = jax.nn.softplus(dtt_ref[0, 0] + bias_c_ref[0])
    a_cum = _cumsum_rows(dt * -jnp.exp(alog_r_ref[0]), chunk)
    a_cum_t = _cumsum_lanes(dtt * -jnp.exp(alog_c_ref[0]), chunk)
    xs = xs_ref[0]
    bmat = b_ref[0]
    cmat = c_ref[0].astype(BF16)
    cb = _dot_nt(cmat, bmat.astype(BF16))
    b_t = bmat.T.astype(BF16)
    li = lax.broadcasted_iota(jnp.int32, (chunk, chunk), 0)
    si = lax.broadcasted_iota(jnp.int32, (chunk, chunk), 1)
    causal = li >= si
    for e in range(heads):
        col = a_cum[:, e:e + 1]
        rowv = a_cum_t[e:e + 1, :]
        a_last = a_cum_t[e:e + 1, chunk - 1:chunk]
        decay = jnp.exp(jnp.where(causal, col - rowv, -jnp.inf))
        xdt = xs[:, e * p_dim:(e + 1) * p_dim] * dt[:, e:e + 1]
        st = state_ref[e]
        y = jnp.dot((cb * decay).astype(BF16), xdt.astype(BF16), preferred_element_type=F32)
        y = y + jnp.dot(cmat, st.astype(BF16), preferred_element_type=F32) * jnp.exp(col)
        to_end = jnp.exp(a_last - col)
        state_ref[e] = st * jnp.exp(a_last) + jnp.dot(b_t, (xdt * to_end).astype(BF16), preferred_element_type=F32)
        y_ref[:, e * p_dim:(e + 1) * p_dim] = y
    y = y_ref[...] + xs * dskip_ref[...]
    y = y * _silu(z_ref[0])
    ms = jnp.mean(y * y, axis=-1, keepdims=True)
    o_ref[0] = (y * lax.rsqrt(ms + NORM_EPS) * normw_ref[...]).astype(o_ref.dtype)


def ssd_scan(xbc, z, dt, dt_bias, a_log, d_skip, norm_w, *, chunk=MB_CHUNK):
    bsz, s_len, d_inner = z.shape
    n_heads = dt.shape[-1]
    n_state = MB_D_STATE
    groups = (xbc.shape[-1] - d_inner) // (2 * n_state)
    heads = n_heads // groups
    p_dim = d_inner // n_heads
    gw = heads * p_dim
    assert gw % LANES == 0 and d_inner % n_state == 0
    chunk = min(chunk, s_len)
    nc = s_len // chunk
    b_off = d_inner // n_state
    c_off = b_off + groups
    dt_g = jnp.transpose(dt.reshape(bsz, s_len, groups, heads), (0, 2, 1, 3))
    dt_gt = jnp.transpose(dt_g, (0, 1, 3, 2))
    kern = functools.partial(_ssd_kernel, chunk=chunk, heads=heads, p_dim=p_dim)
    per_group = lambda b_, g, c: (g, 0, 0)
    return pl.pallas_call(
        kern,
        grid=(bsz, groups, nc),
        in_specs=[pl.BlockSpec((1, chunk, gw), lambda b_, g, c: (b_, c, g)),
                  pl.BlockSpec((1, chunk, n_state), lambda b_, g, c: (b_, c, b_off + g)),
                  pl.BlockSpec((1, chunk, n_state), lambda b_, g, c: (b_, c, c_off + g)),
                  pl.BlockSpec((1, chunk, gw), lambda b_, g, c: (b_, c, g)),
                  pl.BlockSpec((1, 1, chunk, heads), lambda b_, g, c: (b_, g, c, 0)),
                  pl.BlockSpec((1, 1, heads, chunk), lambda b_, g, c: (b_, g, 0, c)),
                  pl.BlockSpec((1, 1, heads), per_group),
                  pl.BlockSpec((1, heads, 1), per_group),
                  pl.BlockSpec((1, 1, heads), per_group),
                  pl.BlockSpec((1, heads, 1), per_group),
                  pl.BlockSpec((1, gw), lambda b_, g, c: (0, g)),
                  pl.BlockSpec((1, gw), lambda b_, g, c: (0, g))],
        out_specs=pl.BlockSpec((1, chunk, gw), lambda b_, g, c: (b_, c, g)),
        out_shape=jax.ShapeDtypeStruct(z.shape, BF16),
        scratch_shapes=[pltpu.VMEM((heads, n_state, p_dim), F32), pltpu.VMEM((chunk, gw), F32)],
        compiler_params=_params("parallel", "parallel", "arbitrary"),
        name="mamba_ssd",
    )(xbc, xbc, xbc, z, dt_g, dt_gt,
      dt_bias.reshape(groups, 1, heads), dt_bias.reshape(groups, heads, 1),
      a_log.reshape(groups, 1, heads), a_log.reshape(groups, heads, 1),
      jnp.repeat(d_skip, p_dim).reshape(1, d_inner), norm_w.reshape(1, d_inner))


def mamba2_mixer(u, h, w, bsz, s_len):
    d_inner = w["mb_w_out"].shape[0]
    n_heads = w["mb_dt_bias"].shape[0]
    w_in = w["mb_w_in"]
    xbc_w = w_in.shape[1] - d_inner - n_heads
    z = matmul(u, [(w_in[:, :d_inner], 0)], d_inner, name="mb_in_z")
    xbc = matmul(u, [(w_in[:, d_inner:d_inner + xbc_w], 0)], xbc_w, name="mb_in_xbc")
    dt = matmul(u, [(w_in[:, d_inner + xbc_w:], 0)], n_heads, name="mb_in_dt")
    xbc = conv_silu(xbc.reshape(bsz, s_len, xbc_w), w["mb_conv_w"], w["mb_conv_b"])
    y = ssd_scan(xbc, z.reshape(bsz, s_len, d_inner), dt.reshape(bsz, s_len, n_heads),
                 w["mb_dt_bias"], w["mb_a_log"], w["mb_d_skip"], w["mb_norm_w"])
    return matmul(y.reshape(bsz * s_len, d_inner), [(w["mb_w_out"], 0)], h.shape[1],
                  epilogue=_ep_residual, extras=[(h, "mn")], bm=512, name="mb_out")


def _hgrn_kernel(q_ref, f_ref, i_ref, g_ref, lb_ref, nw_ref, o_ref, state_ref, *, sub, n_sub):
    @pl.when(pl.program_id(2) == 0)
    def _():
        state_ref[...] = jnp.zeros_like(state_ref)

    lb = lb_ref[...]
    nw = nw_ref[...]
    ti = lax.broadcasted_iota(jnp.int32, (sub, sub), 0)
    si = lax.broadcasted_iota(jnp.int32, (sub, sub), 1)
    causal = ti >= si

    def body(c, carry):
        rows = pl.ds(pl.multiple_of(c * sub, sub), sub)
        f = lb + (1.0 - lb) * jax.nn.sigmoid(f_ref[0, rows, :])
        k = 1.0 - f
        q = _silu(q_ref[0, rows, :])
        v = i_ref[0, rows, :].astype(BF16)
        b = _cumsum_rows(jnp.log(f), sub)
        b_end = b[sub - 1:sub, :]
        q_dec = (q * jnp.exp(b)).astype(BF16)
        k_dec = (k * jnp.exp(-b)).astype(BF16)
        scores = jnp.where(causal, _dot_nt(q_dec, k_dec), 0.0)
        st = state_ref[...]
        o = jnp.dot(scores.astype(BF16), v, preferred_element_type=F32) + _dot_nt(q_dec, st.astype(BF16))
        state_ref[...] = st * jnp.exp(b_end) + _dot_tn(v, (k * jnp.exp(b_end - b)).astype(BF16))
        o = o * lax.rsqrt(jnp.mean(o * o, axis=-1, keepdims=True) + NORM_EPS) * nw
        o_ref[0, rows, :] = (o * _silu(g_ref[0, rows, :])).astype(o_ref.dtype)
        return carry

    lax.fori_loop(0, n_sub, body, 0)


def hgrn2_scan(proj, lower_bound, norm_w, *, dk=HG_HEAD_DIM, sub=HG_CHUNK, tb=256):
    bsz, s_len, d4 = proj.shape
    d = d4 // 4
    n_heads = d // dk
    tb = min(tb, s_len)
    kern = functools.partial(_hgrn_kernel, sub=sub, n_sub=tb // sub)
    spec = lambda part: pl.BlockSpec((1, tb, dk), lambda b_, h_, t: (b_, t, part * n_heads + h_))
    return pl.pallas_call(
        kern,
        grid=(bsz, n_heads, s_len // tb),
        in_specs=[spec(0), spec(1), spec(2), spec(3),
                  pl.BlockSpec((1, dk), lambda b_, h_, t: (0, h_)),
                  pl.BlockSpec((1, dk), lambda b_, h_, t: (0, 0))],
        out_specs=pl.BlockSpec((1, tb, dk), lambda b_, h_, t: (b_, t, h_)),
        out_shape=jax.ShapeDtypeStruct((bsz, s_len, d), BF16),
        scratch_shapes=[pltpu.VMEM((dk, dk), F32)],
        compiler_params=_params("parallel", "parallel", "arbitrary"),
        name="hgrn2_scan",
    )(proj, proj, proj, proj, lower_bound.reshape(1, d), norm_w.reshape(1, dk))


def hgrn2_mixer(u, h, w, lower_bound, bsz, s_len):
    d = h.shape[1]
    proj = matmul(u, [(w["hg_w_in"], 0)], 4 * d, name="hg_in")
    o = hgrn2_scan(proj.reshape(bsz, s_len, 4 * d), lower_bound, w["hg_norm_w"])
    return matmul(o.reshape(bsz * s_len, d), [(w["hg_w_out"], 0)], d,
                  epilogue=_ep_residual, extras=[(h, "mn")], name="hg_out")


def dense_ffn(v, h, w_in, w_out):
    f = w_out.shape[0]
    hid = matmul(v, [(w_in, 0), (w_in, f)], f, epilogue=_ep_swiglu, out_dtype=BF16, name="ffn_in")
    return matmul(hid, [(w_out, 0)], h.shape[1], epilogue=_ep_residual, extras=[(h, "mn")], bm=512, name="ffn_out")


def _router_kernel(x_ref, r_ref, o_ref, *, n_experts):
    logits = jnp.dot(x_ref[...], r_ref[...], preferred_element_type=F32)
    lane = lax.broadcasted_iota(jnp.int32, logits.shape, 1)
    logits = jnp.where(lane < n_experts, logits, -jnp.inf)
    m1 = jnp.max(logits, axis=-1, keepdims=True)
    i1 = jnp.min(jnp.where(logits == m1, lane, LANES), axis=-1, keepdims=True)
    rest = jnp.where(lane == i1, -jnp.inf, logits)
    m2 = jnp.max(rest, axis=-1, keepdims=True)
    i2 = jnp.min(jnp.where(rest == m2, lane, LANES), axis=-1, keepdims=True)
    e2 = jnp.exp(m2 - m1)
    w1 = 1.0 / (1.0 + e2)
    o_ref[...] = jnp.where(lane == i1, w1, 0.0) + jnp.where(lane == i2, e2 * w1, 0.0)


def moe_router(v, router):
    m, d = v.shape
    n_experts = router.shape[1]
    r_pad = jnp.zeros((d, LANES), BF16).at[:, :n_experts].set(router.astype(BF16))
    bm = _pick(m, 512)
    return pl.pallas_call(
        functools.partial(_router_kernel, n_experts=n_experts),
        grid=(m // bm,),
        in_specs=[pl.BlockSpec((bm, d), lambda i: (i, 0)), pl.BlockSpec((d, LANES), lambda i: (0, 0))],
        out_specs=pl.BlockSpec((bm, LANES), lambda i: (i, 0)),
        out_shape=jax.ShapeDtypeStruct((m, LANES), F32),
        compiler_params=_params("parallel"),
        name="moe_router",
    )(v, r_pad)


def _moe_in_kernel(x_ref, wg_ref, wu_ref, c_ref, o_ref, *, blocks_per_expert):
    x = x_ref[...]
    g = jnp.dot(x, wg_ref[0], preferred_element_type=F32)
    u = jnp.dot(x, wu_ref[0], preferred_element_type=F32)
    e = pl.program_id(1) // blocks_per_expert
    comb = c_ref[...]
    lane = lax.broadcasted_iota(jnp.int32, comb.shape, 1)
    scale = jnp.sum(jnp.where(lane == e, comb, 0.0), axis=-1, keepdims=True)
    o_ref[...] = (_silu(g) * u * scale).astype(o_ref.dtype)


def moe_ffn(v, h, router, w_in, w_out, *, bm=1024, bn=512):
    m, d = v.shape
    n_experts, _, two_de = w_in.shape
    de = two_de // 2
    bm = _pick(m, bm)
    bn = _pick(de, bn)
    bpe = de // bn
    comb = moe_router(v, router)
    hid = pl.pallas_call(
        functools.partial(_moe_in_kernel, blocks_per_expert=bpe),
        grid=(m // bm, n_experts * bpe),
        in_specs=[pl.BlockSpec((bm, d), lambda i, j: (i, 0)),
                  pl.BlockSpec((1, d, bn), lambda i, j: (j // bpe, 0, j % bpe)),
                  pl.BlockSpec((1, d, bn), lambda i, j: (j // bpe, 0, j % bpe + bpe)),
                  pl.BlockSpec((bm, LANES), lambda i, j: (i, 0))],
        out_specs=pl.BlockSpec((bm, bn), lambda i, j: (i, j)),
        out_shape=jax.ShapeDtypeStruct((m, n_experts * de), BF16),
        compiler_params=_params("parallel", "parallel"),
        name="moe_in",
    )(v, w_in, w_in, comb)
    return matmul(hid, [(w_out.reshape(n_experts * de, d), 0)], d, epilogue=_ep_residual, extras=[(h, "mn")],
                  name="moe_out")


def ple_gate(h, p_i, norm_pl, pl_proj, pl_gate):
    d = h.shape[1]
    pp = matmul(p_i, [(pl_proj, 0)], d, name="ple_proj")
    n = rmsnorm(h, norm_pl, name="rmsnorm_ple")
    return matmul(n, [(pl_gate, 0)], d, epilogue=_ep_ple_gate, extras=[(h, "mn"), (pp, "mn")], name="ple_gate")


def _rw_mix_kernel(u_ref, mu_ref, *o_refs):
    u = u_ref[0]
    row = lax.broadcasted_iota(jnp.int32, u.shape, 0)
    dx = jnp.where(row >= 1, pltpu.roll(u, 1, 0), 0.0) - u
    for j, o_ref in enumerate(o_refs):
        o_ref[0] = (u + dx * mu_ref[j:j + 1, :]).astype(o_ref.dtype)


def rw_token_mix(u, mu):
    bsz, s_len, d = u.shape
    cb = _pick(d, LANES)
    n_mix = mu.shape[0]
    spec = pl.BlockSpec((1, s_len, cb), lambda b_, j: (b_, 0, j))
    return pl.pallas_call(
        _rw_mix_kernel,
        grid=(bsz, d // cb),
        in_specs=[spec, pl.BlockSpec((n_mix, cb), lambda b_, j: (0, j))],
        out_specs=[spec] * n_mix,
        out_shape=[jax.ShapeDtypeStruct(u.shape, BF16)] * n_mix,
        compiler_params=_params("parallel", "parallel"),
        name="rwkv_token_mix",
    )(u, mu)


def _dot_hi(a, b):
    return jnp.dot(a, b, preferred_element_type=F32, precision=lax.Precision.HIGHEST)


def _rw_head(r, k, v, a, lw, kkw, kaw, rkw, st, chunk):
    kk = k * kkw
    kk = kk / jnp.maximum(jnp.sqrt(jnp.sum(kk * kk, axis=-1, keepdims=True)), 1e-12)
    kmod = k * (1.0 + (a - 1.0) * kaw)
    cum = _cumsum_rows(lw, chunk)
    cum_end = cum[chunk - 1:chunk, :]
    e_neg = jnp.exp(-cum)
    am = (kk * jnp.exp(cum - lw)).astype(BF16)
    kka = kk * a
    bm = (kka * e_neg).astype(BF16)
    km = (kmod * e_neg).astype(BF16)
    rm = (r * jnp.exp(cum)).astype(BF16)
    vb = v.astype(BF16)
    stb = st.astype(BF16)
    ti = lax.broadcasted_iota(jnp.int32, (chunk, chunk), 0)
    si = lax.broadcasted_iota(jnp.int32, (chunk, chunk), 1)
    strict = ti > si
    incl = ti >= si
    nmat = -jnp.where(strict, _dot_nt(am, bm), 0.0)
    lk = jnp.where(strict, _dot_nt(am, km), 0.0).astype(BF16)
    mk = jnp.where(incl, _dot_nt(rm, km), 0.0).astype(BF16)
    mb = jnp.where(incl, _dot_nt(rm, bm), 0.0).astype(BF16)
    x = _dot_nt(am, stb) + jnp.dot(lk, vb, preferred_element_type=F32)
    x = x + _dot_hi(nmat, x)
    p = 2
    while p < chunk:
        nmat = _dot_hi(nmat, nmat)
        x = x + _dot_hi(nmat, x)
        p *= 2
    pb = x.astype(BF16)
    y = _dot_nt(rm, stb) + jnp.dot(mk, vb, preferred_element_type=F32) - jnp.dot(mb, pb, preferred_element_type=F32)
    to_end = jnp.exp(cum_end - cum)
    new_st = (st * jnp.exp(cum_end) + _dot_tn(vb, (kmod * to_end).astype(BF16))
              - _dot_tn(pb, (kka * to_end).astype(BF16)))
    bonus = jnp.sum(r * kmod * rkw, axis=-1, keepdims=True) * v
    return y, bonus, new_st


def _rw_scan_kernel(r_ref, k_ref, v_ref, a_ref, lw_ref, g_ref, kk_ref, ka_ref, rk_ref, lnw_ref, lnb_ref,
                    o_ref, state_ref, *, chunk, heads, n):
    @pl.when(pl.program_id(2) == 0)
    def _():
        state_ref[...] = jnp.zeros_like(state_ref)

    for j in range(heads):
        sl = slice(j * n, (j + 1) * n)
        y, bonus, new_st = _rw_head(r_ref[0, :, sl], k_ref[0, :, sl], v_ref[0, :, sl], a_ref[0, :, sl],
                                    lw_ref[0, :, sl], kk_ref[:, sl], ka_ref[:, sl], rk_ref[:, sl],
                                    state_ref[j], chunk)
        state_ref[j] = new_st
        mean = jnp.mean(y, axis=-1, keepdims=True)
        yc = y - mean
        var = jnp.mean(yc * yc, axis=-1, keepdims=True)
        yn = yc * lax.rsqrt(var + RW_LN_EPS) * lnw_ref[:, sl] + lnb_ref[:, sl]
        o_ref[0, :, sl] = ((yn + bonus) * g_ref[0, :, sl]).astype(o_ref.dtype)


def rw_scan(r, k, v, a, lw, g, k_k, k_a, r_k, ln_w, ln_b, *, n=RW_HEAD_DIM, chunk=RW_CHUNK, heads=4):
    bsz, s_len, d = r.shape
    chunk = min(chunk, s_len)
    heads = min(heads, d // n)
    hw = heads * n
    seq = pl.BlockSpec((1, chunk, hw), lambda b_, h_, c: (b_, c, h_))
    par = pl.BlockSpec((1, hw), lambda b_, h_, c: (0, h_))
    row = lambda t: t.reshape(1, d)
    kern = functools.partial(_rw_scan_kernel, chunk=chunk, heads=heads, n=n)
    return pl.pallas_call(
        kern,
        grid=(bsz, d // hw, s_len // chunk),
        in_specs=[seq] * 6 + [par] * 5,
        out_specs=seq,
        out_shape=jax.ShapeDtypeStruct(r.shape, BF16),
        scratch_shapes=[pltpu.VMEM((heads, n, n), F32)],
        compiler_params=_params("parallel", "parallel", "arbitrary"),
        name="rwkv7_scan",
    )(r, k, v, a, lw, g, row(k_k), row(k_a), row(r_k), row(ln_w), row(ln_b))


def rwkv7_mixer(u, h, w, bsz, s_len):
    t, d = u.shape
    xr, xw, xk, xv, xa, xg = [x.reshape(t, d) for x in rw_token_mix(u.reshape(bsz, s_len, d), w["rw_mu"])]
    r = matmul(xr, [(w["rw_w_rkv"][0], 0)], d, name="rw_r")
    k = matmul(xk, [(w["rw_w_rkv"][1], 0)], d, name="rw_k")
    v = matmul(xv, [(w["rw_w_rkv"][2], 0)], d, name="rw_v")
    row = lambda x: x.reshape(1, d)
    w_lo = matmul(xw, [(w["rw_w1"], 0)], w["rw_w1"].shape[1], epilogue=_ep_tanh, out_dtype=BF16, name="rw_w1")
    lw = matmul(w_lo, [(w["rw_w2"], 0)], d, epilogue=_ep_rw_logdecay, extras=[(row(w["rw_w0"]), "n")], name="rw_w2")
    a_lo = matmul(xa, [(w["rw_a1"], 0)], w["rw_a1"].shape[1], out_dtype=BF16, name="rw_a1")
    a = matmul(a_lo, [(w["rw_a2"], 0)], d, epilogue=_ep_bias_sigmoid, extras=[(row(w["rw_a0"]), "n")], name="rw_a2")
    g_lo = matmul(xg, [(w["rw_g1"], 0)], w["rw_g1"].shape[1], epilogue=_ep_sigmoid, out_dtype=BF16, name="rw_g1")
    g = matmul(g_lo, [(w["rw_g2"], 0)], d, name="rw_g2")
    shp = (bsz, s_len, d)
    y = rw_scan(r.reshape(shp), k.reshape(shp), v.reshape(shp), a.reshape(shp), lw.reshape(shp), g.reshape(shp),
                w["rw_k_k"], w["rw_k_a"], w["rw_r_k"], w["rw_ln_w"], w["rw_ln_b"])
    return matmul(y.reshape(t, d), [(w["rw_w_out"], 0)], d, epilogue=_ep_residual, extras=[(h, "mn")], name="rw_out")


NEG_BIG = -1e30


def _rope_kernel(x_ref, cc_ref, ss_ref, o_ref, *, n_q_slots, scale):
    x = x_ref[0]
    out = x * cc_ref[...] + pltpu.roll(x, x.shape[-1] // 2, 1) * ss_ref[...]
    out = out * jnp.where(pl.program_id(2) < n_q_slots, scale, 1.0)
    o_ref[0] = out.astype(o_ref.dtype)


def _rope_tables(pos, dim):
    inv = ROPE_THETA ** (-(jnp.arange(0, dim, 2, dtype=F32) / dim))
    ang = pos.astype(F32)[:, None] * inv[None, :]
    cos, sin = jnp.cos(ang), jnp.sin(ang)
    return jnp.concatenate([cos, cos], axis=-1), jnp.concatenate([-sin, sin], axis=-1)


def nsa_rope(proj, n_q_slots, k_slots, dh, scale, tb=512):
    bsz, s_len, _ = proj.shape
    tb = min(tb, s_len)
    cc, ss = _rope_tables(jnp.arange(s_len), dh)
    n_out = n_q_slots + len(k_slots)

    def in_slot(j):
        slot = j
        for idx, ks in enumerate(k_slots):
            slot = jnp.where(j == n_q_slots + idx, ks, slot)
        return slot

    return pl.pallas_call(
        functools.partial(_rope_kernel, n_q_slots=n_q_slots, scale=scale),
        grid=(bsz, s_len // tb, n_out),
        in_specs=[pl.BlockSpec((1, tb, dh), lambda b_, t, j: (b_, t, in_slot(j))),
                  pl.BlockSpec((tb, dh), lambda b_, t, j: (t, 0)),
                  pl.BlockSpec((tb, dh), lambda b_, t, j: (t, 0))],
        out_specs=pl.BlockSpec((1, tb, dh), lambda b_, t, j: (b_, t, j)),
        out_shape=jax.ShapeDtypeStruct((bsz, s_len, n_out * dh), BF16),
        compiler_params=_params("parallel", "parallel", "arbitrary"),
        name="nsa_rope",
    )(proj, cc, ss)


def _cmp_finish_kernel(z_ref, bias_ref, w2_ref, cc_ref, ss_ref, o_ref, *, hidden, rope):
    z = z_ref[0]
    nc = z.shape[0]
    nxt = pltpu.roll(z[:, hidden:], nc - 1, 0)
    hid = _silu(z[:, :hidden] + nxt + bias_ref[...])
    out = jnp.dot(hid.astype(BF16), w2_ref[...], preferred_element_type=F32)
    if rope:
        out = out * cc_ref[...] + pltpu.roll(out, out.shape[-1] // 2, 1) * ss_ref[...]
    o_ref[0] = out.astype(o_ref.dtype)


def nsa_compress(x, pos_emb, w1, w2, bsz, s_len, groups, dh, rope):
    stride, blk = NSA_CMP_STRIDE, NSA_CMP_BLOCK
    nc = s_len // stride
    hidden = w1.shape[-1]
    half = stride * dh
    x16 = jnp.transpose(x.reshape(bsz, nc, stride, groups, dh), (0, 3, 1, 2, 4)).reshape(bsz * groups * nc, half)
    w1f = w1.reshape(blk * dh, hidden)
    wcat = jnp.concatenate([w1f[:half], w1f[half:]], axis=1).astype(BF16)
    z = matmul(x16.astype(BF16), [(wcat, 0)], 2 * hidden, name="nsa_cmp_w1")
    bias = matmul(pos_emb.reshape(1, blk * dh).astype(BF16), [(w1f.astype(BF16), 0)], hidden, name="nsa_cmp_pos")
    cc, ss = _rope_tables(jnp.arange(nc) * stride + blk - 1, dh)
    return pl.pallas_call(
        functools.partial(_cmp_finish_kernel, hidden=hidden, rope=rope),
        grid=(bsz * groups,),
        in_specs=[pl.BlockSpec((1, nc, 2 * hidden), lambda i: (i, 0, 0)),
                  pl.BlockSpec((1, hidden), lambda i: (0, 0)),
                  pl.BlockSpec((hidden, dh), lambda i: (0, 0)),
                  pl.BlockSpec((nc, dh), lambda i: (0, 0)),
                  pl.BlockSpec((nc, dh), lambda i: (0, 0))],
        out_specs=pl.BlockSpec((1, nc, dh), lambda i: (i, 0, 0)),
        out_shape=jax.ShapeDtypeStruct((bsz * groups, nc, dh), BF16),
        compiler_params=_params("parallel"),
        name="nsa_cmp_finish",
    )(z.reshape(bsz * groups, nc, 2 * hidden), bias, w2.astype(BF16), cc, ss)


def _nsa_cmp_select_kernel(q_ref, kc_ref, vc_ref, ov_ref, oc_ref, sel_ref, *, tq, rep, dh, topn):
    qi = pl.program_id(2)
    kc = kc_ref[0]
    vc = vc_ref[0]
    nc = kc.shape[0]
    n_sel = sel_ref.shape[-1]
    t = qi * tq + lax.broadcasted_iota(jnp.int32, (tq, nc), 0)
    cmp_end = lax.broadcasted_iota(jnp.int32, (tq, nc), 1) * NSA_CMP_STRIDE + (NSA_CMP_BLOCK - 1)
    visible = cmp_end <= t
    psum = jnp.zeros((tq, nc), F32)
    for r in range(rep):
        s = jnp.where(visible, _dot_nt(q_ref[0, :, r * dh:(r + 1) * dh], kc), NEG_BIG)
        m = jnp.max(s, axis=-1, keepdims=True)
        e = jnp.where(visible, jnp.exp(s - m), 0.0)
        den = jnp.sum(e, axis=-1, keepdims=True)
        p = e / jnp.where(den > 0, den, 1.0)
        oc_ref[0, :, r * dh:(r + 1) * dh] = jnp.dot(p.astype(BF16), vc, preferred_element_type=F32)
        psum = psum + p
    imp = _dot_hi(psum, ov_ref[...])
    blk = lax.broadcasted_iota(jnp.int32, (tq, n_sel), 1)
    cur = (qi * tq + lax.broadcasted_iota(jnp.int32, (tq, n_sel), 0)) // NSA_SEL_BLOCK
    forced = (blk == 0) | (blk == cur) | (blk == cur - 1)
    imp = jnp.where(forced, NSA_FORCED_SCORE, imp)
    imp = jnp.where(blk > cur, -jnp.inf, imp)
    sel = jnp.zeros((tq, n_sel), F32)
    for _ in range(topn):
        m = jnp.max(imp, axis=-1, keepdims=True)
        first = jnp.min(jnp.where(imp == m, blk, n_sel), axis=-1, keepdims=True)
        hit = blk == first
        sel = jnp.where(hit, 1.0, sel)
        imp = jnp.where(hit, -jnp.inf, imp)
    sel_ref[0, 0] = sel


def _flash_step(q_scr, k, v, mask, m_ref, l_ref, acc_ref, rep, tq):
    kb = k.shape[0]
    s = _dot_nt(q_scr[...], k).reshape(rep, tq, kb)
    s = jnp.where(mask[None], s, NEG_BIG)
    m_old = m_ref[...].reshape(rep, tq, -1)[:, :, :1]
    m_new = jnp.maximum(m_old, jnp.max(s, axis=-1, keepdims=True))
    p = jnp.where(mask[None], jnp.exp(s - m_new), 0.0)
    alpha = jnp.exp(m_old - m_new)
    l_old = l_ref[...].reshape(rep, tq, -1)[:, :, :1]
    l_new = alpha * l_old + jnp.sum(p, axis=-1, keepdims=True)
    pv = jnp.dot(p.reshape(rep * tq, kb).astype(BF16), v, preferred_element_type=F32)
    acc_ref[...] = (alpha * acc_ref[...].reshape(rep, tq, -1)).reshape(rep * tq, -1) + pv
    m_ref[...] = jnp.broadcast_to(m_new, (rep, tq, m_ref.shape[-1])).reshape(m_ref.shape)
    l_ref[...] = jnp.broadcast_to(l_new, (rep, tq, l_ref.shape[-1])).reshape(l_ref.shape)


def _flash_init(q_ref, q_scr, m_ref, l_ref, acc_ref, rep, tq, dh):
    for r in range(rep):
        q_scr[r * tq:(r + 1) * tq, :] = q_ref[0, :, r * dh:(r + 1) * dh]
    m_ref[...] = jnp.full_like(m_ref, NEG_BIG)
    l_ref[...] = jnp.zeros_like(l_ref)
    acc_ref[...] = jnp.zeros_like(acc_ref)


def _flash_result(l_ref, acc_ref):
    l = l_ref[...][:, :1]
    return acc_ref[...] / jnp.where(l > 0, l, 1.0)


def _nsa_select_kernel(q_ref, k_ref, v_ref, sel_ref, o_ref, q_scr, m_ref, l_ref, acc_ref, *, tq, kb, rep, dh):
    qi = pl.program_id(2)
    kj = pl.program_id(3)

    @pl.when(kj == 0)
    def _():
        _flash_init(q_ref, q_scr, m_ref, l_ref, acc_ref, rep, tq, dh)

    @pl.when(kj * kb <= qi * tq + tq - 1)
    def _():
        sel = sel_ref[0, 0]
        blk = lax.broadcasted_iota(jnp.int32, sel.shape, 1)
        kpos = kj * kb + lax.broadcasted_iota(jnp.int32, (tq, kb), 1)
        t = qi * tq + lax.broadcasted_iota(jnp.int32, (tq, kb), 0)
        chosen = jnp.zeros((tq, kb), F32)
        for i in range(kb // NSA_SEL_BLOCK):
            col = jnp.sum(jnp.where(blk == kj * (kb // NSA_SEL_BLOCK) + i, sel, 0.0), axis=-1, keepdims=True)
            in_blk = (kpos - kj * kb) // NSA_SEL_BLOCK == i
            chosen = jnp.where(in_blk, col, chosen)
        mask = (chosen > 0) & (kpos <= t)
        _flash_step(q_scr, k_ref[0], v_ref[0].astype(BF16), mask, m_ref, l_ref, acc_ref, rep, tq)

    @pl.when(kj == pl.num_programs(3) - 1)
    def _():
        out = _flash_result(l_ref, acc_ref)
        for r in range(rep):
            o_ref[0, :, r * dh:(r + 1) * dh] = out[r * tq:(r + 1) * tq, :]


def _nsa_window_kernel(q_ref, k_ref, v_ref, oc_ref, os_ref, g_ref, o_ref, q_scr, m_ref, l_ref, acc_ref,
                       *, tq, kb, rep, dh, window, n_steps):
    qi = pl.program_id(2)
    w = pl.program_id(3)
    kblk = qi * (tq // kb) - (n_steps - tq // kb) + w

    @pl.when(w == 0)
    def _():
        _flash_init(q_ref, q_scr, m_ref, l_ref, acc_ref, rep, tq, dh)

    @pl.when(kblk >= 0)
    def _():
        kpos = kblk * kb + lax.broadcasted_iota(jnp.int32, (tq, kb), 1)
        t = qi * tq + lax.broadcasted_iota(jnp.int32, (tq, kb), 0)
        mask = (kpos <= t) & (kpos > t - window)
        _flash_step(q_scr, k_ref[0], v_ref[0].astype(BF16), mask, m_ref, l_ref, acc_ref, rep, tq)

    @pl.when(w == n_steps - 1)
    def _():
        out = _flash_result(l_ref, acc_ref)
        gates = g_ref[0, 0]
        for r in range(rep):
            sl = slice(r * dh, (r + 1) * dh)
            o = (gates[:, 3 * r:3 * r + 1] * oc_ref[0, :, sl] + gates[:, 3 * r + 1:3 * r + 2] * os_ref[0, :, sl]
                 + gates[:, 3 * r + 2:3 * r + 3] * out[r * tq:(r + 1) * tq, :])
            o_ref[0, :, sl] = o.astype(o_ref.dtype)


def nsa_mixer(u, h, w, bsz, s_len):
    t, d = u.shape
    dh, groups = NSA_HEAD_DIM, NSA_N_KV
    n_heads = d // dh
    rep = n_heads // groups
    kvw = groups * dh
    qw = n_heads * dh
    main_w = qw + 6 * kvw
    scale = dh ** -0.5
    tq = kb = min(128, s_len)
    nq = s_len // tq
    n_sel = s_len // NSA_SEL_BLOCK
    topn = min(NSA_TOPK, n_sel)
    w_in = w["nsa_w_in"]
    proj = matmul(u, [(w_in[:, :main_w], 0)], main_w, name="nsa_in").reshape(bsz, s_len, main_w)
    gates = matmul(u, [(w_in[:, main_w:], 0)], w_in.shape[1] - main_w, epilogue=_ep_sigmoid, name="nsa_gates")
    gates = jnp.transpose(gates.reshape(bsz, s_len, groups, rep * 3), (0, 2, 1, 3))
    slot = lambda j: (qw + j * kvw) // dh
    roped = nsa_rope(proj, n_heads, [slot(2) + g for g in range(groups)] + [slot(4) + g for g in range(groups)],
                     dh, scale)
    kc = nsa_compress(proj[..., qw:qw + kvw], w["nsa_cmp_pos_k"], w["nsa_cmp_k_w1"], w["nsa_cmp_k_w2"],
                      bsz, s_len, groups, dh, True)
    vc = nsa_compress(proj[..., qw + kvw:qw + 2 * kvw], w["nsa_cmp_pos_v"], w["nsa_cmp_v_w1"], w["nsa_cmp_v_w2"],
                      bsz, s_len, groups, dh, False)
    nc = kc.shape[1]
    cs = jnp.arange(nc)[:, None] * NSA_CMP_STRIDE
    ss = jnp.arange(n_sel)[None, :] * NSA_SEL_BLOCK
    overlap = jnp.clip(jnp.minimum(cs + NSA_CMP_BLOCK, ss + NSA_SEL_BLOCK) - jnp.maximum(cs, ss), 0, None)
    overlap = overlap.astype(F32) / NSA_CMP_BLOCK

    q_spec3 = pl.BlockSpec((1, tq, rep * dh), lambda b_, g, i: (b_, i, g))
    o_c, sel = pl.pallas_call(
        functools.partial(_nsa_cmp_select_kernel, tq=tq, rep=rep, dh=dh, topn=topn),
        grid=(bsz, groups, nq),
        in_specs=[q_spec3,
                  pl.BlockSpec((1, nc, dh), lambda b_, g, i: (b_ * groups + g, 0, 0)),
                  pl.BlockSpec((1, nc, dh), lambda b_, g, i: (b_ * groups + g, 0, 0)),
                  pl.BlockSpec((nc, n_sel), lambda b_, g, i: (0, 0))],
        out_specs=[q_spec3, pl.BlockSpec((1, 1, tq, n_sel), lambda b_, g, i: (b_, g, i, 0))],
        out_shape=[jax.ShapeDtypeStruct((bsz, s_len, qw), F32),
                   jax.ShapeDtypeStruct((bsz, groups, s_len, n_sel), F32)],
        compiler_params=_params("parallel", "parallel", "parallel"),
        name="nsa_cmp_select",
    )(roped, kc, vc, overlap)

    q_spec = pl.BlockSpec((1, tq, rep * dh), lambda b_, g, i, j: (b_, i, g))
    flash_scratch = [pltpu.VMEM((rep * tq, dh), BF16), pltpu.VMEM((rep * tq, LANES), F32),
                     pltpu.VMEM((rep * tq, LANES), F32), pltpu.VMEM((rep * tq, dh), F32)]
    last_kb = lambda i: (i * tq + tq - 1) // kb
    o_s = pl.pallas_call(
        functools.partial(_nsa_select_kernel, tq=tq, kb=kb, rep=rep, dh=dh),
        grid=(bsz, groups, nq, s_len // kb),
        in_specs=[q_spec,
                  pl.BlockSpec((1, kb, dh), lambda b_, g, i, j: (b_, jnp.minimum(j, last_kb(i)), n_heads + g)),
                  pl.BlockSpec((1, kb, dh), lambda b_, g, i, j: (b_, jnp.minimum(j, last_kb(i)), slot(3) + g)),
                  pl.BlockSpec((1, 1, tq, n_sel), lambda b_, g, i, j: (b_, g, i, 0))],
        out_specs=q_spec,
        out_shape=jax.ShapeDtypeStruct((bsz, s_len, qw), F32),
        scratch_shapes=flash_scratch,
        compiler_params=_params("parallel", "parallel", "parallel", "arbitrary"),
        name="nsa_select_attn",
    )(roped, roped, proj, sel)

    n_steps = NSA_WINDOW // kb + tq // kb
    win_blk = lambda i, j: jnp.maximum(i * (tq // kb) - (n_steps - tq // kb) + j, 0)
    o = pl.pallas_call(
        functools.partial(_nsa_window_kernel, tq=tq, kb=kb, rep=rep, dh=dh, window=NSA_WINDOW, n_steps=n_steps),
        grid=(bsz, groups, nq, n_steps),
        in_specs=[q_spec,
                  pl.BlockSpec((1, kb, dh), lambda b_, g, i, j: (b_, win_blk(i, j), n_heads + groups + g)),
                  pl.BlockSpec((1, kb, dh), lambda b_, g, i, j: (b_, win_blk(i, j), slot(5) + g)),
                  q_spec, q_spec,
                  pl.BlockSpec((1, 1, tq, rep * 3), lambda b_, g, i, j: (b_, g, i, 0))],
        out_specs=q_spec,
        out_shape=jax.ShapeDtypeStruct((bsz, s_len, qw), BF16),
        scratch_shapes=flash_scratch,
        compiler_params=_params("parallel", "parallel", "parallel", "arbitrary"),
        name="nsa_window_attn",
    )(roped, roped, proj, o_c, o_s, gates)
    return matmul(o.reshape(t, qw), [(w["nsa_w_out"], 0)], d, epilogue=_ep_residual, extras=[(h, "mn")], name="nsa_out")


_MATMUL_WEIGHTS = ("pl_proj", "pl_gate", "mb_w_in", "mb_w_out", "nsa_w_in", "nsa_w_out", "hg_w_in", "hg_w_out",
                   "rw_w_rkv", "rw_w1", "rw_w2", "rw_a1", "rw_a2", "rw_g1", "rw_g2", "rw_w_out",
                   "ffn0_w_in", "ffn0_w_out", "moe1_w_in", "moe1_w_out", "ffn2_w_in", "ffn2_w_out",
                   "moe3_w_in", "moe3_w_out")


def kernel(x, p, norm_mix, norm_ffn, norm_pl, pl_proj, pl_gate, norm_final, mb_w_in, mb_conv_w, mb_conv_b, mb_dt_bias, mb_a_log, mb_d_skip, mb_norm_w, mb_w_out, nsa_w_in, nsa_cmp_pos_k, nsa_cmp_pos_v, nsa_cmp_k_w1, nsa_cmp_k_w2, nsa_cmp_v_w1, nsa_cmp_v_w2, nsa_w_out, hg_w_in, hg_lb_logits, hg_norm_w, hg_w_out, rw_mu, rw_w_rkv, rw_w0, rw_w1, rw_w2, rw_a0, rw_a1, rw_a2, rw_g1, rw_g2, rw_k_k, rw_k_a, rw_r_k, rw_ln_w, rw_ln_b, rw_w_out, ffn0_w_in, ffn0_w_out, moe1_router, moe1_w_in, moe1_w_out, ffn2_w_in, ffn2_w_out, moe3_router, moe3_w_in, moe3_w_out):
    w = dict(locals())
    for name in _MATMUL_WEIGHTS:
        w[name] = w[name].astype(BF16)
    bsz, s_len, d = x.shape
    depth = p.shape[0]
    t = bsz * s_len
    lb_all = jax.nn.softmax(hg_lb_logits.astype(F32), axis=0)
    lb_all = jnp.cumsum(lb_all, axis=0) - lb_all[0]
    dense = [(w["ffn0_w_in"], w["ffn0_w_out"]), (w["ffn2_w_in"], w["ffn2_w_out"])]
    moe = [(moe1_router, w["moe1_w_in"], w["moe1_w_out"]), (moe3_router, w["moe3_w_in"], w["moe3_w_out"])]
    p_bf = p.reshape(depth, t, p.shape[-1]).astype(BF16)
    h = x.reshape(t, d)
    for i in range(depth):
        kind = i % 4
        if kind == 0:
            h = mamba2_mixer(rmsnorm(h, norm_mix[i]), h, w, bsz, s_len)
        elif kind == 1:
            h = nsa_mixer(rmsnorm(h, norm_mix[i]), h, w, bsz, s_len)
        elif kind == 2:
            h = hgrn2_mixer(rmsnorm(h, norm_mix[i]), h, w, lb_all[i], bsz, s_len)
        else:
            h = rwkv7_mixer(rmsnorm(h, norm_mix[i], out_dtype=F32), h, w, bsz, s_len)
        v = rmsnorm(h, norm_ffn[i])
        if i % 2 == 0:
            h = dense_ffn(v, h, *dense[i // 2])
        else:
            h = moe_ffn(v, h, *moe[i // 2])
        h = ple_gate(h, p_bf[i], norm_pl[i], w["pl_proj"][i], w["pl_gate"][i])
    return rmsnorm(h, norm_final, out_dtype=F32).reshape(bsz, s_len, d)
```

```python
import functools
import math

import jax
import jax.numpy as jnp
from jax import lax
from jax.experimental import pallas as pl
from jax.experimental.pallas import tpu as pltpu

F32 = jnp.float32
BF16 = jnp.bfloat16

NORM_EPS = 1e-6
ROPE_THETA = 10000.0

V7X_VMEM_BYTES = 64 * 1024 * 1024
VMEM_LIMIT_BYTES = V7X_VMEM_BYTES - 8 * 1024 * 1024
LANES = 128

MB_HEAD_DIM = 64
MB_N_GROUPS = 8
MB_D_STATE = 128
MB_CONV = 4
MB_CHUNK = 128

NSA_HEAD_DIM = 128
NSA_N_KV = 4
NSA_CMP_BLOCK = 32
NSA_CMP_STRIDE = 16
NSA_SEL_BLOCK = 64
NSA_TOPK = 16
NSA_WINDOW = 512
NSA_FORCED_SCORE = 1e9

HG_HEAD_DIM = 128
HG_CHUNK = 32

RW_HEAD_DIM = 64
RW_LN_EPS = 64e-5
RW_CHUNK = 128

MOE_TOPK = 2


def _params(*semantics):
    return pltpu.CompilerParams(dimension_semantics=semantics, vmem_limit_bytes=VMEM_LIMIT_BYTES)


def _pick(n, target):
    if n <= target:
        return n
    for c in range(target, 0, -1):
        if n % c == 0:
            return c
    return n


def _silu(x):
    return x * jax.nn.sigmoid(x)


def _rmsnorm_kernel(x_ref, g_ref, o_ref):
    x = x_ref[...]
    ms = jnp.mean(x * x, axis=-1, keepdims=True)
    o_ref[...] = (x * lax.rsqrt(ms + NORM_EPS) * g_ref[...]).astype(o_ref.dtype)


def rmsnorm(x, gain, out_dtype=BF16, name="rmsnorm"):
    m, d = x.shape
    bm = _pick(m, 256)
    return pl.pallas_call(
        _rmsnorm_kernel,
        grid=(m // bm,),
        in_specs=[pl.BlockSpec((bm, d), lambda i: (i, 0)), pl.BlockSpec((1, d), lambda i: (0, 0))],
        out_specs=pl.BlockSpec((bm, d), lambda i: (i, 0)),
        out_shape=jax.ShapeDtypeStruct((m, d), out_dtype),
        compiler_params=_params("parallel"),
        name=name,
    )(x, gain.reshape(1, d).astype(F32))


def _mm_kernel(*refs, n_w, n_extra, nk, epilogue):
    x_ref = refs[0]
    w_refs = refs[1:1 + n_w]
    e_refs = refs[1 + n_w:1 + n_w + n_extra]
    o_ref = refs[1 + n_w + n_extra]
    acc_refs = refs[2 + n_w + n_extra:]
    x = x_ref[...]
    if nk == 1:
        accs = [jnp.dot(x, w[...], preferred_element_type=F32) for w in w_refs]
        o_ref[...] = epilogue(accs, [e[...] for e in e_refs]).astype(o_ref.dtype)
        return
    k = pl.program_id(2)

    @pl.when(k == 0)
    def _():
        for a in acc_refs:
            a[...] = jnp.zeros_like(a)

    for a, w in zip(acc_refs, w_refs):
        a[...] += jnp.dot(x, w[...], preferred_element_type=F32)

    @pl.when(k == nk - 1)
    def _():
        o_ref[...] = epilogue([a[...] for a in acc_refs], [e[...] for e in e_refs]).astype(o_ref.dtype)


def _first(accs, extras):
    return accs[0]


def matmul(x, ws, n_out, *, epilogue=_first, extras=(), out_dtype=F32, bm=1024, bn=512, bk=None, name="matmul"):
    m, kdim = x.shape
    bm = _pick(m, bm)
    bn = _pick(n_out, bn)
    if bk is None:
        bk = kdim if kdim <= 4096 else _pick(kdim, 4096)
    nk = kdim // bk
    assert kdim % bk == 0 and m % bm == 0 and n_out % bn == 0
    in_specs = [pl.BlockSpec((bm, bk), lambda i, j, k: (i, k))]
    args = [x]
    for w, off in ws:
        assert off % bn == 0 and w.shape[0] == kdim
        in_specs.append(pl.BlockSpec((bk, bn), functools.partial(lambda i, j, k, o: (k, j + o), o=off // bn)))
        args.append(w)
    for arr, kind in extras:
        if kind == "mn":
            in_specs.append(pl.BlockSpec((bm, bn), lambda i, j, k: (i, j)))
        elif kind == "m":
            in_specs.append(pl.BlockSpec((bm, arr.shape[1]), lambda i, j, k: (i, 0)))
        else:
            in_specs.append(pl.BlockSpec((1, bn), lambda i, j, k: (0, j)))
        args.append(arr)
    scratch = [pltpu.VMEM((bm, bn), F32) for _ in ws] if nk > 1 else []
    kern = functools.partial(_mm_kernel, n_w=len(ws), n_extra=len(extras), nk=nk, epilogue=epilogue)
    return pl.pallas_call(
        kern,
        grid=(m // bm, n_out // bn, nk),
        in_specs=in_specs,
        out_specs=pl.BlockSpec((bm, bn), lambda i, j, k: (i, j)),
        out_shape=jax.ShapeDtypeStruct((m, n_out), out_dtype),
        scratch_shapes=scratch,
        compiler_params=_params("parallel", "parallel", "arbitrary"),
        name=name,
    )(*args)


def _ep_residual(accs, extras):
    return extras[0] + accs[0]


def _ep_swiglu(accs, extras):
    return _silu(accs[0]) * accs[1]


def _ep_bias(accs, extras):
    return accs[0] + extras[0]


def _ep_tanh(accs, extras):
    return jnp.tanh(accs[0])


def _ep_sigmoid(accs, extras):
    return jax.nn.sigmoid(accs[0])


def _ep_bias_sigmoid(accs, extras):
    return jax.nn.sigmoid(accs[0] + extras[0])


def _ep_rw_logdecay(accs, extras):
    w = -jax.nn.softplus(-(accs[0] + extras[0])) - 0.5
    return -jnp.exp(w)


def _ep_ple_gate(accs, extras):
    return extras[0] + extras[1] * jax.nn.sigmoid(accs[0])


def _conv_silu_kernel(x_ref, w_ref, b_ref, o_ref, *, k_width):
    x = x_ref[0]
    row = lax.broadcasted_iota(jnp.int32, x.shape, 0)
    y = b_ref[...] + w_ref[k_width - 1:k_width, :] * x
    for j in range(k_width - 1):
        shift = k_width - 1 - j
        xs = jnp.where(row >= shift, pltpu.roll(x, shift, 0), 0.0)
        y = y + w_ref[j:j + 1, :] * xs
    o_ref[0] = _silu(y)


def conv_silu(x, w, b):
    bsz, s_len, c = x.shape
    cb = _pick(c, 256)
    k_width = w.shape[0]
    return pl.pallas_call(
        functools.partial(_conv_silu_kernel, k_width=k_width),
        grid=(bsz, c // cb),
        in_specs=[pl.BlockSpec((1, s_len, cb), lambda b_, j: (b_, 0, j)),
                  pl.BlockSpec((k_width, cb), lambda b_, j: (0, j)),
                  pl.BlockSpec((1, cb), lambda b_, j: (0, j))],
        out_specs=pl.BlockSpec((1, s_len, cb), lambda b_, j: (b_, 0, j)),
        out_shape=jax.ShapeDtypeStruct(x.shape, F32),
        compiler_params=_params("parallel", "parallel"),
        name="mamba_conv_silu",
    )(x, w, b.reshape(1, c))


def _cumsum_rows(x, n):
    row = lax.broadcasted_iota(jnp.int32, x.shape, 0)
    s = 1
    while s < n:
        x = x + jnp.where(row >= s, pltpu.roll(x, s, 0), 0.0)
        s *= 2
    return x


def _cumsum_lanes(x, n):
    col = lax.broadcasted_iota(jnp.int32, x.shape, 1)
    s = 1
    while s < n:
        x = x + jnp.where(col >= s, pltpu.roll(x, s, 1), 0.0)
        s *= 2
    return x


def _dot_nt(a, b):
    return lax.dot_general(a, b, (((1,), (1,)), ((), ())), preferred_element_type=F32)


def _dot_tn(a, b):
    return lax.dot_general(a, b, (((0,), (0,)), ((), ())), preferred_element_type=F32)


def _ssd_kernel(xs_ref, b_ref, c_ref, z_ref, dt_ref, dtt_ref, bias_r_ref, bias_c_ref, alog_r_ref, alog_c_ref,
                dskip_ref, normw_ref, o_ref, state_ref, y_ref, *, chunk, heads, p_dim):
    @pl.when(pl.program_id(2) == 0)
    def _():
        state_ref[...] = jnp.zeros_like(state_ref)

    dt = jax.nn.softplus(dt_ref[0, 0] + bias_r_ref[0])
    dtt = jax.nn.softplus(dtt_ref[0, 0] + bias_c_ref[0])
    a_cum = _cumsum_rows(dt * -jnp.exp(alog_r_ref[0]), chunk)
    a_cum_t = _cumsum_lanes(dtt * -jnp.exp(alog_c_ref[0]), chunk)
    xs = xs_ref[0]
    bmat = b_ref[0]
    cmat = c_ref[0].astype(BF16)
    cb = _dot_nt(cmat, bmat.astype(BF16))
    b_t = bmat.T.astype(BF16)
    li = lax.broadcasted_iota(jnp.int32, (chunk, chunk), 0)
    si = lax.broadcasted_iota(jnp.int32, (chunk, chunk), 1)
    causal = li >= si
    per = LANES // p_dim
    lane_seg = lax.broadcasted_iota(jnp.int32, (1, LANES), 1) // p_dim

    def pick(vals):
        out = vals[-1]
        for i in range(per - 2, -1, -1):
            out = jnp.where(lane_seg == i, vals[i], out)
        return out

    dot = functools.partial(jnp.dot, preferred_element_type=F32)
    es = range(heads)
    tiles = range(heads // per)
    col = [a_cum[:, e:e + 1] for e in es]
    a_last = [a_cum_t[e:e + 1, chunk - 1:chunk] for e in es]
    m = [(cb * jnp.exp(jnp.where(causal, col[e] - a_cum_t[e:e + 1, :], -jnp.inf))).astype(BF16) for e in es]
    of = lambda vals, i: [vals[i * per + j] for j in range(per)]
    xdt = [xs[:, i * LANES:(i + 1) * LANES] * pick([dt[:, e:e + 1] for e in of(es, i)]) for i in tiles]
    xdt_b = [x.astype(BF16) for x in xdt]
    st = [state_ref[i] for i in tiles]
    y_in = [pick([dot(m[e], xdt_b[i]) for e in of(es, i)]) for i in tiles]
    y_st = [dot(cmat, st[i].astype(BF16)) * pick([jnp.exp(c) for c in of(col, i)]) for i in tiles]
    to_end = [pick([jnp.exp(a_last[e] - col[e]) for e in of(es, i)]) for i in tiles]
    for i in tiles:
        state_ref[i] = (st[i] * pick([jnp.exp(x) for x in of(a_last, i)])
                        + dot(b_t, (xdt[i] * to_end[i]).astype(BF16)))
        y_ref[:, i * LANES:(i + 1) * LANES] = y_in[i] + y_st[i]
    y = y_ref[...] + xs * dskip_ref[...]
    y = y * _silu(z_ref[0])
    ms = jnp.mean(y * y, axis=-1, keepdims=True)
    o_ref[0] = (y * lax.rsqrt(ms + NORM_EPS) * normw_ref[...]).astype(o_ref.dtype)


def ssd_scan(xbc, z, dt, dt_bias, a_log, d_skip, norm_w, *, chunk=MB_CHUNK):
    bsz, s_len, d_inner = z.shape
    n_heads = dt.shape[-1]
    n_state = MB_D_STATE
    groups = (xbc.shape[-1] - d_inner) // (2 * n_state)
    heads = n_heads // groups
    p_dim = d_inner // n_heads
    gw = heads * p_dim
    assert gw % LANES == 0 and d_inner % n_state == 0
    chunk = min(chunk, s_len)
    nc = s_len // chunk
    b_off = d_inner // n_state
    c_off = b_off + groups
    dt_g = jnp.transpose(dt.reshape(bsz, s_len, groups, heads), (0, 2, 1, 3))
    dt_gt = jnp.transpose(dt_g, (0, 1, 3, 2))
    kern = functools.partial(_ssd_kernel, chunk=chunk, heads=heads, p_dim=p_dim)
    per_group = lambda b_, g, c: (g, 0, 0)
    return pl.pallas_call(
        kern,
        grid=(bsz, groups, nc),
        in_specs=[pl.BlockSpec((1, chunk, gw), lambda b_, g, c: (b_, c, g)),
                  pl.BlockSpec((1, chunk, n_state), lambda b_, g, c: (b_, c, b_off + g)),
                  pl.BlockSpec((1, chunk, n_state), lambda b_, g, c: (b_, c, c_off + g)),
                  pl.BlockSpec((1, chunk, gw), lambda b_, g, c: (b_, c, g)),
                  pl.BlockSpec((1, 1, chunk, heads), lambda b_, g, c: (b_, g, c, 0)),
                  pl.BlockSpec((1, 1, heads, chunk), lambda b_, g, c: (b_, g, 0, c)),
                  pl.BlockSpec((1, 1, heads), per_group),
                  pl.BlockSpec((1, heads, 1), per_group),
                  pl.BlockSpec((1, 1, heads), per_group),
                  pl.BlockSpec((1, heads, 1), per_group),
                  pl.BlockSpec((1, gw), lambda b_, g, c: (0, g)),
                  pl.BlockSpec((1, gw), lambda b_, g, c: (0, g))],
        out_specs=pl.BlockSpec((1, chunk, gw), lambda b_, g, c: (b_, c, g)),
        out_shape=jax.ShapeDtypeStruct(z.shape, BF16),
        scratch_shapes=[pltpu.VMEM((gw // LANES, n_state, LANES), F32), pltpu.VMEM((chunk, gw), F32)],
        compiler_params=_params("parallel", "parallel", "arbitrary"),
        name="mamba_ssd",
    )(xbc, xbc, xbc, z, dt_g, dt_gt,
      dt_bias.reshape(groups, 1, heads), dt_bias.reshape(groups, heads, 1),
      a_log.reshape(groups, 1, heads), a_log.reshape(groups, heads, 1),
      jnp.repeat(d_skip, p_dim).reshape(1, d_inner), norm_w.reshape(1, d_inner))


def mamba2_mixer(u, h, w, bsz, s_len):
    d_inner = w["mb_w_out"].shape[0]
    n_heads = w["mb_dt_bias"].shape[0]
    w_in = w["mb_w_in"]
    xbc_w = w_in.shape[1] - d_inner - n_heads
    z = matmul(u, [(w_in[:, :d_inner], 0)], d_inner, name="mb_in_z")
    xbc = matmul(u, [(w_in[:, d_inner:d_inner + xbc_w], 0)], xbc_w, name="mb_in_xbc")
    dt = matmul(u, [(w_in[:, d_inner + xbc_w:], 0)], n_heads, name="mb_in_dt")
    xbc = conv_silu(xbc.reshape(bsz, s_len, xbc_w), w["mb_conv_w"], w["mb_conv_b"])
    y = ssd_scan(xbc, z.reshape(bsz, s_len, d_inner), dt.reshape(bsz, s_len, n_heads),
                 w["mb_dt_bias"], w["mb_a_log"], w["mb_d_skip"], w["mb_norm_w"])
    return matmul(y.reshape(bsz * s_len, d_inner), [(w["mb_w_out"], 0)], h.shape[1],
                  epilogue=_ep_residual, extras=[(h, "mn")], bm=512, name="mb_out")


def _seg_cumsum_rows(x, seg, reverse=False):
    n = x.shape[0]
    pos = lax.broadcasted_iota(jnp.int32, x.shape, 0) % seg
    s = 1
    while s < seg:
        if reverse:
            x = x + jnp.where(pos < seg - s, pltpu.roll(x, n - s, 0), 0.0)
        else:
            x = x + jnp.where(pos >= s, pltpu.roll(x, s, 0), 0.0)
        s *= 2
    return x


def _hgrn_kernel(q_ref, f_ref, i_ref, g_ref, lb_ref, nw_ref, o_ref, state_ref, *, sub, n_sub, heads, dk):
    @pl.when(pl.program_id(2) == 0)
    def _():
        state_ref[...] = jnp.zeros_like(state_ref)

    lb = lb_ref[...]
    nw = nw_ref[...]
    ti = lax.broadcasted_iota(jnp.int32, (sub, sub), 0)
    si = lax.broadcasted_iota(jnp.int32, (sub, sub), 1)
    causal = ti >= si
    f = lb + (1.0 - lb) * jax.nn.sigmoid(f_ref[0])
    lf = jnp.log(f)
    k = 1.0 - f
    b = _seg_cumsum_rows(lf, sub)
    to_end = _seg_cumsum_rows(lf, sub, reverse=True) - lf
    q_dec = (_silu(q_ref[0]) * jnp.exp(b)).astype(BF16)
    k_dec = (k * jnp.exp(-b)).astype(BF16)
    k_end = (k * jnp.exp(to_end)).astype(BF16)
    v = i_ref[0].astype(BF16)
    cs = range(n_sub)
    hs = range(heads)
    blk = lambda x, c, h: x[c * sub:(c + 1) * sub, h * dk:(h + 1) * dk]
    scores = [[jnp.where(causal, _dot_nt(blk(q_dec, c, h), blk(k_dec, c, h)), 0.0).astype(BF16) for h in hs]
              for c in cs]
    upd = [[_dot_tn(blk(v, c, h), blk(k_end, c, h)) for h in hs] for c in cs]
    states = []
    st = [state_ref[h] for h in hs]
    for c in cs:
        states.append(st)
        decay = jnp.exp(b[(c + 1) * sub - 1:(c + 1) * sub, :])
        st = [st[h] * decay[:, h * dk:(h + 1) * dk] + upd[c][h] for h in hs]
    for h in hs:
        state_ref[h] = st[h]
    for c in cs:
        rows = slice(c * sub, (c + 1) * sub)
        for h in hs:
            o = (jnp.dot(scores[c][h], blk(v, c, h), preferred_element_type=F32)
                 + _dot_nt(blk(q_dec, c, h), states[c][h].astype(BF16)))
            o = o * lax.rsqrt(jnp.mean(o * o, axis=-1, keepdims=True) + NORM_EPS) * nw
            cols = slice(h * dk, (h + 1) * dk)
            o_ref[0, rows, cols] = (o * _silu(g_ref[0, rows, cols])).astype(o_ref.dtype)


def hgrn2_scan(proj, lower_bound, norm_w, *, dk=HG_HEAD_DIM, sub=HG_CHUNK, tb=256, heads=4):
    bsz, s_len, d4 = proj.shape
    d = d4 // 4
    n_heads = d // dk
    tb = min(tb, s_len)
    heads = min(heads, n_heads)
    hw = heads * dk
    n_hb = n_heads // heads
    kern = functools.partial(_hgrn_kernel, sub=sub, n_sub=tb // sub, heads=heads, dk=dk)
    spec = lambda part: pl.BlockSpec((1, tb, hw), lambda b_, h_, t: (b_, t, part * n_hb + h_))
    return pl.pallas_call(
        kern,
        grid=(bsz, n_hb, s_len // tb),
        in_specs=[spec(0), spec(1), spec(2), spec(3),
                  pl.BlockSpec((1, hw), lambda b_, h_, t: (0, h_)),
                  pl.BlockSpec((1, dk), lambda b_, h_, t: (0, 0))],
        out_specs=pl.BlockSpec((1, tb, hw), lambda b_, h_, t: (b_, t, h_)),
        out_shape=jax.ShapeDtypeStruct((bsz, s_len, d), BF16),
        scratch_shapes=[pltpu.VMEM((heads, dk, dk), F32)],
        compiler_params=_params("parallel", "parallel", "arbitrary"),
        name="hgrn2_scan",
    )(proj, proj, proj, proj, lower_bound.reshape(1, d), norm_w.reshape(1, dk))


def hgrn2_mixer(u, h, w, lower_bound, bsz, s_len):
    d = h.shape[1]
    proj = matmul(u, [(w["hg_w_in"], 0)], 4 * d, name="hg_in")
    o = hgrn2_scan(proj.reshape(bsz, s_len, 4 * d), lower_bound, w["hg_norm_w"])
    return matmul(o.reshape(bsz * s_len, d), [(w["hg_w_out"], 0)], d,
                  epilogue=_ep_residual, extras=[(h, "mn")], name="hg_out")


def dense_ffn(v, h, w_in, w_out):
    f = w_out.shape[0]
    hid = matmul(v, [(w_in, 0), (w_in, f)], f, epilogue=_ep_swiglu, out_dtype=BF16, name="ffn_in")
    return matmul(hid, [(w_out, 0)], h.shape[1], epilogue=_ep_residual, extras=[(h, "mn")], bm=512, name="ffn_out")


def _router_kernel(x_ref, r_ref, o_ref, *, n_experts):
    logits = jnp.dot(x_ref[...], r_ref[...], preferred_element_type=F32)
    lane = lax.broadcasted_iota(jnp.int32, logits.shape, 1)
    logits = jnp.where(lane < n_experts, logits, -jnp.inf)
    m1 = jnp.max(logits, axis=-1, keepdims=True)
    i1 = jnp.min(jnp.where(logits == m1, lane, LANES), axis=-1, keepdims=True)
    rest = jnp.where(lane == i1, -jnp.inf, logits)
    m2 = jnp.max(rest, axis=-1, keepdims=True)
    i2 = jnp.min(jnp.where(rest == m2, lane, LANES), axis=-1, keepdims=True)
    e2 = jnp.exp(m2 - m1)
    w1 = 1.0 / (1.0 + e2)
    o_ref[...] = jnp.where(lane == i1, w1, 0.0) + jnp.where(lane == i2, e2 * w1, 0.0)


def moe_router(v, router):
    m, d = v.shape
    n_experts = router.shape[1]
    r_pad = jnp.zeros((d, LANES), BF16).at[:, :n_experts].set(router.astype(BF16))
    bm = _pick(m, 512)
    return pl.pallas_call(
        functools.partial(_router_kernel, n_experts=n_experts),
        grid=(m // bm,),
        in_specs=[pl.BlockSpec((bm, d), lambda i: (i, 0)), pl.BlockSpec((d, LANES), lambda i: (0, 0))],
        out_specs=pl.BlockSpec((bm, LANES), lambda i: (i, 0)),
        out_shape=jax.ShapeDtypeStruct((m, LANES), F32),
        compiler_params=_params("parallel"),
        name="moe_router",
    )(v, r_pad)


def _moe_in_kernel(x_ref, wg_ref, wu_ref, c_ref, o_ref, *, blocks_per_expert):
    x = x_ref[...]
    g = jnp.dot(x, wg_ref[0], preferred_element_type=F32)
    u = jnp.dot(x, wu_ref[0], preferred_element_type=F32)
    e = pl.program_id(1) // blocks_per_expert
    comb = c_ref[...]
    lane = lax.broadcasted_iota(jnp.int32, comb.shape, 1)
    scale = jnp.sum(jnp.where(lane == e, comb, 0.0), axis=-1, keepdims=True)
    o_ref[...] = (_silu(g) * u * scale).astype(o_ref.dtype)


def moe_ffn(v, h, router, w_in, w_out, *, bm=1024, bn=512):
    m, d = v.shape
    n_experts, _, two_de = w_in.shape
    de = two_de // 2
    bm = _pick(m, bm)
    bn = _pick(de, bn)
    bpe = de // bn
    comb = moe_router(v, router)
    hid = pl.pallas_call(
        functools.partial(_moe_in_kernel, blocks_per_expert=bpe),
        grid=(m // bm, n_experts * bpe),
        in_specs=[pl.BlockSpec((bm, d), lambda i, j: (i, 0)),
                  pl.BlockSpec((1, d, bn), lambda i, j: (j // bpe, 0, j % bpe)),
                  pl.BlockSpec((1, d, bn), lambda i, j: (j // bpe, 0, j % bpe + bpe)),
                  pl.BlockSpec((bm, LANES), lambda i, j: (i, 0))],
        out_specs=pl.BlockSpec((bm, bn), lambda i, j: (i, j)),
        out_shape=jax.ShapeDtypeStruct((m, n_experts * de), BF16),
        compiler_params=_params("parallel", "parallel"),
        name="moe_in",
    )(v, w_in, w_in, comb)
    return matmul(hid, [(w_out.reshape(n_experts * de, d), 0)], d, epilogue=_ep_residual, extras=[(h, "mn")],
                  name="moe_out")


def ple_gate(h, p_i, norm_pl, pl_proj, pl_gate):
    d = h.shape[1]
    pp = matmul(p_i, [(pl_proj, 0)], d, name="ple_proj")
    n = rmsnorm(h, norm_pl, name="rmsnorm_ple")
    return matmul(n, [(pl_gate, 0)], d, epilogue=_ep_ple_gate, extras=[(h, "mn"), (pp, "mn")], name="ple_gate")


def _rw_mix_kernel(u_ref, mu_ref, *o_refs):
    u = u_ref[0]
    row = lax.broadcasted_iota(jnp.int32, u.shape, 0)
    dx = jnp.where(row >= 1, pltpu.roll(u, 1, 0), 0.0) - u
    for j, o_ref in enumerate(o_refs):
        o_ref[0] = (u + dx * mu_ref[j:j + 1, :]).astype(o_ref.dtype)


def rw_token_mix(u, mu):
    bsz, s_len, d = u.shape
    cb = _pick(d, LANES)
    n_mix = mu.shape[0]
    spec = pl.BlockSpec((1, s_len, cb), lambda b_, j: (b_, 0, j))
    return pl.pallas_call(
        _rw_mix_kernel,
        grid=(bsz, d // cb),
        in_specs=[spec, pl.BlockSpec((n_mix, cb), lambda b_, j: (0, j))],
        out_specs=[spec] * n_mix,
        out_shape=[jax.ShapeDtypeStruct(u.shape, BF16)] * n_mix,
        compiler_params=_params("parallel", "parallel"),
        name="rwkv_token_mix",
    )(u, mu)


def _dot_hi(a, b):
    return jnp.dot(a, b, preferred_element_type=F32, precision=lax.Precision.HIGHEST)


def _rw_scan_kernel(r_ref, k_ref, v_ref, a_ref, lw_ref, g_ref, kk_ref, ka_ref, rk_ref, lnw_ref, lnb_ref,
                    o_ref, state_ref, *, chunk, heads, n):
    @pl.when(pl.program_id(2) == 0)
    def _():
        state_ref[...] = jnp.zeros_like(state_ref)

    hs = range(heads)
    sls = [slice(j * n, (j + 1) * n) for j in hs]
    ti = lax.broadcasted_iota(jnp.int32, (chunk, chunk), 0)
    si = lax.broadcasted_iota(jnp.int32, (chunk, chunk), 1)
    strict = ti > si
    incl = ti >= si
    dot = functools.partial(jnp.dot, preferred_element_type=F32)

    r = [r_ref[0, :, sl] for sl in sls]
    v = [v_ref[0, :, sl] for sl in sls]
    a = [a_ref[0, :, sl] for sl in sls]
    lw = [lw_ref[0, :, sl] for sl in sls]
    k = [k_ref[0, :, sl] for sl in sls]
    kk = [k[j] * kk_ref[:, sls[j]] for j in hs]
    kk = [kk[j] / jnp.maximum(jnp.sqrt(jnp.sum(kk[j] * kk[j], axis=-1, keepdims=True)), 1e-12) for j in hs]
    kmod = [k[j] * (1.0 + (a[j] - 1.0) * ka_ref[:, sls[j]]) for j in hs]
    kka = [kk[j] * a[j] for j in hs]
    cum = [_cumsum_rows(lw[j], chunk) for j in hs]
    cum_end = [c[chunk - 1:chunk, :] for c in cum]
    mid = [c[chunk // 2 - 1:chunk // 2, :] for c in cum]
    e_neg = [jnp.exp(mid[j] - cum[j]) for j in hs]
    am = [(kk[j] * jnp.exp(cum[j] - lw[j] - mid[j])).astype(BF16) for j in hs]
    bm = [(kka[j] * e_neg[j]).astype(BF16) for j in hs]
    km = [(kmod[j] * e_neg[j]).astype(BF16) for j in hs]
    rm = [(r[j] * jnp.exp(cum[j] - mid[j])).astype(BF16) for j in hs]
    a_abs = [(kk[j] * jnp.exp(cum[j] - lw[j])).astype(BF16) for j in hs]
    r_abs = [(r[j] * jnp.exp(cum[j])).astype(BF16) for j in hs]
    vb = [x.astype(BF16) for x in v]
    st = [state_ref[j] for j in hs]
    stb = [x.astype(BF16) for x in st]

    nb = [(-jnp.where(strict, _dot_nt(am[j], bm[j]), 0.0)).astype(BF16) for j in hs]
    lk = [jnp.where(strict, _dot_nt(am[j], km[j]), 0.0).astype(BF16) for j in hs]
    x = [_dot_nt(a_abs[j], stb[j]) + dot(lk[j], vb[j]) for j in hs]
    x = [x[j] + dot(nb[j], x[j].astype(BF16)) for j in hs]
    p = 2
    while p < chunk:
        nb = [dot(nb[j], nb[j]).astype(BF16) for j in hs]
        x = [x[j] + dot(nb[j], x[j].astype(BF16)) for j in hs]
        p *= 2
    pb = [xj.astype(BF16) for xj in x]
    mk = [jnp.where(incl, _dot_nt(rm[j], km[j]), 0.0).astype(BF16) for j in hs]
    mb = [jnp.where(incl, _dot_nt(rm[j], bm[j]), 0.0).astype(BF16) for j in hs]
    y = [_dot_nt(r_abs[j], stb[j]) + dot(mk[j], vb[j]) - dot(mb[j], pb[j]) for j in hs]
    to_end = [jnp.exp(cum_end[j] - cum[j]) for j in hs]
    for j in hs:
        state_ref[j] = (st[j] * jnp.exp(cum_end[j]) + _dot_tn(vb[j], (kmod[j] * to_end[j]).astype(BF16))
                        - _dot_tn(pb[j], (kka[j] * to_end[j]).astype(BF16)))
    for j in hs:
        sl = sls[j]
        bonus = jnp.sum(r[j] * kmod[j] * rk_ref[:, sl], axis=-1, keepdims=True) * v[j]
        mean = jnp.mean(y[j], axis=-1, keepdims=True)
        yc = y[j] - mean
        var = jnp.mean(yc * yc, axis=-1, keepdims=True)
        yn = yc * lax.rsqrt(var + RW_LN_EPS) * lnw_ref[:, sl] + lnb_ref[:, sl]
        o_ref[0, :, sl] = ((yn + bonus) * g_ref[0, :, sl]).astype(o_ref.dtype)


def rw_scan(r, k, v, a, lw, g, k_k, k_a, r_k, ln_w, ln_b, *, n=RW_HEAD_DIM, chunk=RW_CHUNK, heads=8):
    bsz, s_len, d = r.shape
    chunk = min(chunk, s_len)
    heads = min(heads, d // n)
    hw = heads * n
    seq = pl.BlockSpec((1, chunk, hw), lambda b_, h_, c: (b_, c, h_))
    par = pl.BlockSpec((1, hw), lambda b_, h_, c: (0, h_))
    row = lambda t: t.reshape(1, d)
    kern = functools.partial(_rw_scan_kernel, chunk=chunk, heads=heads, n=n)
    return pl.pallas_call(
        kern,
        grid=(bsz, d // hw, s_len // chunk),
        in_specs=[seq] * 6 + [par] * 5,
        out_specs=seq,
        out_shape=jax.ShapeDtypeStruct(r.shape, BF16),
        scratch_shapes=[pltpu.VMEM((heads, n, n), F32)],
        compiler_params=_params("parallel", "parallel", "arbitrary"),
        name="rwkv7_scan",
    )(r, k, v, a, lw, g, row(k_k), row(k_a), row(r_k), row(ln_w), row(ln_b))


def rwkv7_mixer(u, h, w, bsz, s_len):
    t, d = u.shape
    xr, xw, xk, xv, xa, xg = [x.reshape(t, d) for x in rw_token_mix(u.reshape(bsz, s_len, d), w["rw_mu"])]
    r = matmul(xr, [(w["rw_w_rkv"][0], 0)], d, name="rw_r")
    k = matmul(xk, [(w["rw_w_rkv"][1], 0)], d, name="rw_k")
    v = matmul(xv, [(w["rw_w_rkv"][2], 0)], d, name="rw_v")
    row = lambda x: x.reshape(1, d)
    w_lo = matmul(xw, [(w["rw_w1"], 0)], w["rw_w1"].shape[1], epilogue=_ep_tanh, out_dtype=BF16, name="rw_w1")
    lw = matmul(w_lo, [(w["rw_w2"], 0)], d, epilogue=_ep_rw_logdecay, extras=[(row(w["rw_w0"]), "n")], name="rw_w2")
    a_lo = matmul(xa, [(w["rw_a1"], 0)], w["rw_a1"].shape[1], out_dtype=BF16, name="rw_a1")
    a = matmul(a_lo, [(w["rw_a2"], 0)], d, epilogue=_ep_bias_sigmoid, extras=[(row(w["rw_a0"]), "n")], name="rw_a2")
    g_lo = matmul(xg, [(w["rw_g1"], 0)], w["rw_g1"].shape[1], epilogue=_ep_sigmoid, out_dtype=BF16, name="rw_g1")
    g = matmul(g_lo, [(w["rw_g2"], 0)], d, name="rw_g2")
    shp = (bsz, s_len, d)
    y = rw_scan(r.reshape(shp), k.reshape(shp), v.reshape(shp), a.reshape(shp), lw.reshape(shp), g.reshape(shp),
                w["rw_k_k"], w["rw_k_a"], w["rw_r_k"], w["rw_ln_w"], w["rw_ln_b"])
    return matmul(y.reshape(t, d), [(w["rw_w_out"], 0)], d, epilogue=_ep_residual, extras=[(h, "mn")], name="rw_out")


NEG_BIG = -1e30


def _rope_kernel(x_ref, cc_ref, ss_ref, o_ref, *, n_q_slots, scale):
    x = x_ref[0]
    out = x * cc_ref[...] + pltpu.roll(x, x.shape[-1] // 2, 1) * ss_ref[...]
    out = out * jnp.where(pl.program_id(2) < n_q_slots, scale, 1.0)
    o_ref[0] = out.astype(o_ref.dtype)


def _rope_tables(pos, dim):
    inv = ROPE_THETA ** (-(jnp.arange(0, dim, 2, dtype=F32) / dim))
    ang = pos.astype(F32)[:, None] * inv[None, :]
    cos, sin = jnp.cos(ang), jnp.sin(ang)
    return jnp.concatenate([cos, cos], axis=-1), jnp.concatenate([-sin, sin], axis=-1)


def nsa_rope(proj, n_q_slots, k_slots, dh, scale, tb=512):
    bsz, s_len, _ = proj.shape
    tb = min(tb, s_len)
    cc, ss = _rope_tables(jnp.arange(s_len), dh)
    n_out = n_q_slots + len(k_slots)

    def in_slot(j):
        slot = j
        for idx, ks in enumerate(k_slots):
            slot = jnp.where(j == n_q_slots + idx, ks, slot)
        return slot

    return pl.pallas_call(
        functools.partial(_rope_kernel, n_q_slots=n_q_slots, scale=scale),
        grid=(bsz, s_len // tb, n_out),
        in_specs=[pl.BlockSpec((1, tb, dh), lambda b_, t, j: (b_, t, in_slot(j))),
                  pl.BlockSpec((tb, dh), lambda b_, t, j: (t, 0)),
                  pl.BlockSpec((tb, dh), lambda b_, t, j: (t, 0))],
        out_specs=pl.BlockSpec((1, tb, dh), lambda b_, t, j: (b_, t, j)),
        out_shape=jax.ShapeDtypeStruct((bsz, s_len, n_out * dh), BF16),
        compiler_params=_params("parallel", "parallel", "arbitrary"),
        name="nsa_rope",
    )(proj, cc, ss)


def _cmp_finish_kernel(z_ref, bias_ref, w2_ref, cc_ref, ss_ref, o_ref, *, hidden, rope):
    z = z_ref[0]
    nc = z.shape[0]
    nxt = pltpu.roll(z[:, hidden:], nc - 1, 0)
    hid = _silu(z[:, :hidden] + nxt + bias_ref[...])
    out = jnp.dot(hid.astype(BF16), w2_ref[...], preferred_element_type=F32)
    if rope:
        out = out * cc_ref[...] + pltpu.roll(out, out.shape[-1] // 2, 1) * ss_ref[...]
    o_ref[0] = out.astype(o_ref.dtype)


def nsa_compress(x, pos_emb, w1, w2, bsz, s_len, groups, dh, rope, transpose_out=False):
    stride, blk = NSA_CMP_STRIDE, NSA_CMP_BLOCK
    nc = s_len // stride
    hidden = w1.shape[-1]
    half = stride * dh
    x16 = jnp.transpose(x.reshape(bsz, nc, stride, groups, dh), (0, 3, 1, 2, 4)).reshape(bsz * groups * nc, half)
    w1f = w1.reshape(blk * dh, hidden)
    wcat = jnp.concatenate([w1f[:half], w1f[half:]], axis=1).astype(BF16)
    z = matmul(x16.astype(BF16), [(wcat, 0)], 2 * hidden, name="nsa_cmp_w1")
    bias = matmul(pos_emb.reshape(1, blk * dh).astype(BF16), [(w1f.astype(BF16), 0)], hidden, name="nsa_cmp_pos")
    cc, ss = _rope_tables(jnp.arange(nc) * stride + blk - 1, dh)
    if transpose_out:
        assert not rope
        return pl.pallas_call(
            functools.partial(_cmp_finish_t_kernel, hidden=hidden),
            grid=(bsz * groups,),
            in_specs=[pl.BlockSpec((1, nc, 2 * hidden), lambda i: (i, 0, 0)),
                      pl.BlockSpec((1, hidden), lambda i: (0, 0)),
                      pl.BlockSpec((dh, hidden), lambda i: (0, 0))],
            out_specs=pl.BlockSpec((1, dh, nc), lambda i: (i, 0, 0)),
            out_shape=jax.ShapeDtypeStruct((bsz * groups, dh, nc), BF16),
            compiler_params=_params("parallel"),
            name="nsa_cmp_finish_t",
        )(z.reshape(bsz * groups, nc, 2 * hidden), bias, w2.T.astype(BF16))
    return pl.pallas_call(
        functools.partial(_cmp_finish_kernel, hidden=hidden, rope=rope),
        grid=(bsz * groups,),
        in_specs=[pl.BlockSpec((1, nc, 2 * hidden), lambda i: (i, 0, 0)),
                  pl.BlockSpec((1, hidden), lambda i: (0, 0)),
                  pl.BlockSpec((hidden, dh), lambda i: (0, 0)),
                  pl.BlockSpec((nc, dh), lambda i: (0, 0)),
                  pl.BlockSpec((nc, dh), lambda i: (0, 0))],
        out_specs=pl.BlockSpec((1, nc, dh), lambda i: (i, 0, 0)),
        out_shape=jax.ShapeDtypeStruct((bsz * groups, nc, dh), BF16),
        compiler_params=_params("parallel"),
        name="nsa_cmp_finish",
    )(z.reshape(bsz * groups, nc, 2 * hidden), bias, w2.astype(BF16), cc, ss)


def _nsa_cmp_select_kernel(q_ref, kc_ref, vc_ref, ov_ref, oc_ref, sel_ref, *, tq, rep, dh, topn):
    qi = pl.program_id(2)
    kc = kc_ref[0]
    vc = vc_ref[0]
    nc = kc.shape[0]
    n_sel = sel_ref.shape[-1]
    t = qi * tq + lax.broadcasted_iota(jnp.int32, (tq, nc), 0)
    cmp_end = lax.broadcasted_iota(jnp.int32, (tq, nc), 1) * NSA_CMP_STRIDE + (NSA_CMP_BLOCK - 1)
    visible = cmp_end <= t
    psum = jnp.zeros((tq, nc), F32)
    for r in range(rep):
        s = jnp.where(visible, _dot_nt(q_ref[0, :, r * dh:(r + 1) * dh], kc), NEG_BIG)
        m = jnp.max(s, axis=-1, keepdims=True)
        e = jnp.where(visible, jnp.exp(s - m), 0.0)
        den = jnp.sum(e, axis=-1, keepdims=True)
        p = e / jnp.where(den > 0, den, 1.0)
        oc_ref[0, :, r * dh:(r + 1) * dh] = jnp.dot(p.astype(BF16), vc, preferred_element_type=F32)
        psum = psum + p
    imp = _dot_hi(psum, ov_ref[...])
    blk = lax.broadcasted_iota(jnp.int32, (tq, n_sel), 1)
    cur = (qi * tq + lax.broadcasted_iota(jnp.int32, (tq, n_sel), 0)) // NSA_SEL_BLOCK
    forced = (blk == 0) | (blk == cur) | (blk == cur - 1)
    imp = jnp.where(forced, NSA_FORCED_SCORE, imp)
    imp = jnp.where(blk > cur, -jnp.inf, imp)
    sel = jnp.zeros((tq, n_sel), F32)
    for _ in range(topn):
        m = jnp.max(imp, axis=-1, keepdims=True)
        first = jnp.min(jnp.where(imp == m, blk, n_sel), axis=-1, keepdims=True)
        hit = blk == first
        sel = jnp.where(hit, 1.0, sel)
        imp = jnp.where(hit, -jnp.inf, imp)
    sel_ref[0, 0] = sel


def _flash_step(q_scr, k, v, mask, m_ref, l_ref, acc_ref, rep, tq):
    kb = k.shape[0]
    s = _dot_nt(q_scr[...], k).reshape(rep, tq, kb)
    s = jnp.where(mask[None], s, NEG_BIG)
    m_old = m_ref[...].reshape(rep, tq, -1)[:, :, :1]
    m_new = jnp.maximum(m_old, jnp.max(s, axis=-1, keepdims=True))
    p = jnp.where(mask[None], jnp.exp(s - m_new), 0.0)
    alpha = jnp.exp(m_old - m_new)
    l_old = l_ref[...].reshape(rep, tq, -1)[:, :, :1]
    l_new = alpha * l_old + jnp.sum(p, axis=-1, keepdims=True)
    pv = jnp.dot(p.reshape(rep * tq, kb).astype(BF16), v, preferred_element_type=F32)
    acc_ref[...] = (alpha * acc_ref[...].reshape(rep, tq, -1)).reshape(rep * tq, -1) + pv
    m_ref[...] = jnp.broadcast_to(m_new, (rep, tq, m_ref.shape[-1])).reshape(m_ref.shape)
    l_ref[...] = jnp.broadcast_to(l_new, (rep, tq, l_ref.shape[-1])).reshape(l_ref.shape)


def _flash_init(q_ref, q_scr, m_ref, l_ref, acc_ref, rep, tq, dh):
    for r in range(rep):
        q_scr[r * tq:(r + 1) * tq, :] = q_ref[0, :, r * dh:(r + 1) * dh]
    m_ref[...] = jnp.full_like(m_ref, NEG_BIG)
    l_ref[...] = jnp.zeros_like(l_ref)
    acc_ref[...] = jnp.zeros_like(acc_ref)


def _flash_result(l_ref, acc_ref):
    l = l_ref[...][:, :1]
    return acc_ref[...] / jnp.where(l > 0, l, 1.0)


def _nsa_select_kernel(q_ref, k_ref, v_ref, sel_ref, o_ref, q_scr, m_ref, l_ref, acc_ref, *, tq, kb, rep, dh):
    qi = pl.program_id(2)
    kj = pl.program_id(3)

    @pl.when(kj == 0)
    def _():
        _flash_init(q_ref, q_scr, m_ref, l_ref, acc_ref, rep, tq, dh)

    @pl.when(kj * kb <= qi * tq + tq - 1)
    def _():
        sel = sel_ref[0, 0]
        blk = lax.broadcasted_iota(jnp.int32, sel.shape, 1)
        kpos = kj * kb + lax.broadcasted_iota(jnp.int32, (tq, kb), 1)
        t = qi * tq + lax.broadcasted_iota(jnp.int32, (tq, kb), 0)
        chosen = jnp.zeros((tq, kb), F32)
        for i in range(kb // NSA_SEL_BLOCK):
            col = jnp.sum(jnp.where(blk == kj * (kb // NSA_SEL_BLOCK) + i, sel, 0.0), axis=-1, keepdims=True)
            in_blk = (kpos - kj * kb) // NSA_SEL_BLOCK == i
            chosen = jnp.where(in_blk, col, chosen)
        mask = (chosen > 0) & (kpos <= t)
        _flash_step(q_scr, k_ref[0], v_ref[0].astype(BF16), mask, m_ref, l_ref, acc_ref, rep, tq)

    @pl.when(kj == pl.num_programs(3) - 1)
    def _():
        out = _flash_result(l_ref, acc_ref)
        for r in range(rep):
            o_ref[0, :, r * dh:(r + 1) * dh] = out[r * tq:(r + 1) * tq, :]


def _nsa_window_kernel(q_ref, k_ref, v_ref, oc_ref, os_ref, g_ref, o_ref, q_scr, m_ref, l_ref, acc_ref,
                       *, tq, kb, rep, dh, window, n_steps):
    qi = pl.program_id(2)
    w = pl.program_id(3)
    kblk = qi * (tq // kb) - (n_steps - tq // kb) + w

    @pl.when(w == 0)
    def _():
        _flash_init(q_ref, q_scr, m_ref, l_ref, acc_ref, rep, tq, dh)

    @pl.when(kblk >= 0)
    def _():
        kpos = kblk * kb + lax.broadcasted_iota(jnp.int32, (tq, kb), 1)
        t = qi * tq + lax.broadcasted_iota(jnp.int32, (tq, kb), 0)
        mask = (kpos <= t) & (kpos > t - window)
        _flash_step(q_scr, k_ref[0], v_ref[0].astype(BF16), mask, m_ref, l_ref, acc_ref, rep, tq)

    @pl.when(w == n_steps - 1)
    def _():
        out = _flash_result(l_ref, acc_ref)
        gates = g_ref[0, 0]
        for r in range(rep):
            sl = slice(r * dh, (r + 1) * dh)
            o = (gates[:, 3 * r:3 * r + 1] * oc_ref[0, :, sl] + gates[:, 3 * r + 1:3 * r + 2] * os_ref[0, :, sl]
                 + gates[:, 3 * r + 2:3 * r + 3] * out[r * tq:(r + 1) * tq, :])
            o_ref[0, :, sl] = o.astype(o_ref.dtype)


def nsa_mixer(u, h, w, bsz, s_len):
    t, d = u.shape
    dh, groups = NSA_HEAD_DIM, NSA_N_KV
    n_heads = d // dh
    rep = n_heads // groups
    kvw = groups * dh
    qw = n_heads * dh
    main_w = qw + 6 * kvw
    scale = dh ** -0.5
    tq = kb = min(128, s_len)
    nq = s_len // tq
    n_sel = s_len // NSA_SEL_BLOCK
    topn = min(NSA_TOPK, n_sel)
    w_in = w["nsa_w_in"]
    proj = matmul(u, [(w_in[:, :main_w], 0)], main_w, name="nsa_in").reshape(bsz, s_len, main_w)
    gates = matmul(u, [(w_in[:, main_w:], 0)], w_in.shape[1] - main_w, epilogue=_ep_sigmoid, name="nsa_gates")
    gates = jnp.transpose(gates.reshape(bsz, s_len, groups, rep * 3), (0, 2, 1, 3))
    slot = lambda j: (qw + j * kvw) // dh
    roped = nsa_rope(proj, n_heads, [slot(2) + g for g in range(groups)] + [slot(4) + g for g in range(groups)],
                     dh, scale)
    kc = nsa_compress(proj[..., qw:qw + kvw], w["nsa_cmp_pos_k"], w["nsa_cmp_k_w1"], w["nsa_cmp_k_w2"],
                      bsz, s_len, groups, dh, True)
    vc = nsa_compress(proj[..., qw + kvw:qw + 2 * kvw], w["nsa_cmp_pos_v"], w["nsa_cmp_v_w1"], w["nsa_cmp_v_w2"],
                      bsz, s_len, groups, dh, False)
    nc = kc.shape[1]
    cs = jnp.arange(nc)[:, None] * NSA_CMP_STRIDE
    ss = jnp.arange(n_sel)[None, :] * NSA_SEL_BLOCK
    overlap = jnp.clip(jnp.minimum(cs + NSA_CMP_BLOCK, ss + NSA_SEL_BLOCK) - jnp.maximum(cs, ss), 0, None)
    overlap = overlap.astype(F32) / NSA_CMP_BLOCK

    q_spec3 = pl.BlockSpec((1, tq, rep * dh), lambda b_, g, i: (b_, i, g))
    o_c, sel = pl.pallas_call(
        functools.partial(_nsa_cmp_select_kernel, tq=tq, rep=rep, dh=dh, topn=topn),
        grid=(bsz, groups, nq),
        in_specs=[q_spec3,
                  pl.BlockSpec((1, nc, dh), lambda b_, g, i: (b_ * groups + g, 0, 0)),
                  pl.BlockSpec((1, nc, dh), lambda b_, g, i: (b_ * groups + g, 0, 0)),
                  pl.BlockSpec((nc, n_sel), lambda b_, g, i: (0, 0))],
        out_specs=[q_spec3, pl.BlockSpec((1, 1, tq, n_sel), lambda b_, g, i: (b_, g, i, 0))],
        out_shape=[jax.ShapeDtypeStruct((bsz, s_len, qw), F32),
                   jax.ShapeDtypeStruct((bsz, groups, s_len, n_sel), F32)],
        compiler_params=_params("parallel", "parallel", "parallel"),
        name="nsa_cmp_select",
    )(roped, kc, vc, overlap)

    q_spec = pl.BlockSpec((1, tq, rep * dh), lambda b_, g, i, j: (b_, i, g))
    flash_scratch = [pltpu.VMEM((rep * tq, dh), BF16), pltpu.VMEM((rep * tq, LANES), F32),
                     pltpu.VMEM((rep * tq, LANES), F32), pltpu.VMEM((rep * tq, dh), F32)]
    last_kb = lambda i: (i * tq + tq - 1) // kb
    o_s = pl.pallas_call(
        functools.partial(_nsa_select_kernel, tq=tq, kb=kb, rep=rep, dh=dh),
        grid=(bsz, groups, nq, s_len // kb),
        in_specs=[q_spec,
                  pl.BlockSpec((1, kb, dh), lambda b_, g, i, j: (b_, jnp.minimum(j, last_kb(i)), n_heads + g)),
                  pl.BlockSpec((1, kb, dh), lambda b_, g, i, j: (b_, jnp.minimum(j, last_kb(i)), slot(3) + g)),
                  pl.BlockSpec((1, 1, tq, n_sel), lambda b_, g, i, j: (b_, g, i, 0))],
        out_specs=q_spec,
        out_shape=jax.ShapeDtypeStruct((bsz, s_len, qw), F32),
        scratch_shapes=flash_scratch,
        compiler_params=_params("parallel", "parallel", "parallel", "arbitrary"),
        name="nsa_select_attn",
    )(roped, roped, proj, sel)

    n_steps = NSA_WINDOW // kb + tq // kb
    win_blk = lambda i, j: jnp.maximum(i * (tq // kb) - (n_steps - tq // kb) + j, 0)
    o = pl.pallas_call(
        functools.partial(_nsa_window_kernel, tq=tq, kb=kb, rep=rep, dh=dh, window=NSA_WINDOW, n_steps=n_steps),
        grid=(bsz, groups, nq, n_steps),
        in_specs=[q_spec,
                  pl.BlockSpec((1, kb, dh), lambda b_, g, i, j: (b_, win_blk(i, j), n_heads + groups + g)),
                  pl.BlockSpec((1, kb, dh), lambda b_, g, i, j: (b_, win_blk(i, j), slot(5) + g)),
                  q_spec, q_spec,
                  pl.BlockSpec((1, 1, tq, rep * 3), lambda b_, g, i, j: (b_, g, i, 0))],
        out_specs=q_spec,
        out_shape=jax.ShapeDtypeStruct((bsz, s_len, qw), BF16),
        scratch_shapes=flash_scratch,
        compiler_params=_params("parallel", "parallel", "parallel", "arbitrary"),
        name="nsa_window_attn",
    )(roped, roped, proj, o_c, o_s, gates)
    return matmul(o.reshape(t, qw), [(w["nsa_w_out"], 0)], d, epilogue=_ep_residual, extras=[(h, "mn")], name="nsa_out")


def _rope_t_kernel(x_ref, cc_ref, ss_ref, o_ref, *, n_rope, scale):
    x = x_ref[0]
    roped = (x * cc_ref[...] + pltpu.roll(x, x.shape[-1] // 2, 1) * ss_ref[...]) * scale
    out = jnp.where(pl.program_id(2) < n_rope, roped, x)
    o_ref[0] = out.T.astype(o_ref.dtype)


def nsa_rope_t(proj, slots, n_rope, dh, scale, tb=512):
    bsz, s_len, _ = proj.shape
    tb = min(tb, s_len)
    cc, ss = _rope_tables(jnp.arange(s_len), dh)
    table = jnp.asarray(slots, jnp.int32)
    grid_spec = pltpu.PrefetchScalarGridSpec(
        num_scalar_prefetch=1,
        grid=(bsz, s_len // tb, len(slots)),
        in_specs=[pl.BlockSpec((1, tb, dh), lambda b_, t, j, tab: (b_, t, tab[j])),
                  pl.BlockSpec((tb, dh), lambda b_, t, j, tab: (t, 0)),
                  pl.BlockSpec((tb, dh), lambda b_, t, j, tab: (t, 0))],
        out_specs=pl.BlockSpec((1, dh, tb), lambda b_, t, j, tab: (b_, j, t)),
    )
    kern = lambda tab, x_ref, cc_ref, ss_ref, o_ref: _rope_t_kernel(x_ref, cc_ref, ss_ref, o_ref,
                                                                   n_rope=n_rope, scale=scale)
    return pl.pallas_call(
        kern,
        grid_spec=grid_spec,
        out_shape=jax.ShapeDtypeStruct((bsz, len(slots) * dh, s_len), BF16),
        compiler_params=_params("parallel", "parallel", "arbitrary"),
        name="nsa_rope_t",
    )(table, proj, cc, ss)


def _cmp_finish_t_kernel(z_ref, bias_ref, w2_ref, o_ref, *, hidden):
    z = z_ref[0]
    nc = z.shape[0]
    nxt = pltpu.roll(z[:, hidden:], nc - 1, 0)
    hid = _silu(z[:, :hidden] + nxt + bias_ref[...])
    o_ref[0] = _dot_nt(w2_ref[...], hid.astype(BF16)).astype(o_ref.dtype)


def _nsa_cmp_select_t_kernel(q_ref, kc_ref, vc_ref, ov_ref, oc_ref, sel_ref, *, tq, rep, dh, topn):
    qi = pl.program_id(2)
    kc = kc_ref[0]
    vct = vc_ref[0]
    nc = kc.shape[0]
    n_sel = sel_ref.shape[2]
    t = qi * tq + lax.broadcasted_iota(jnp.int32, (nc, tq), 1)
    cmp_end = lax.broadcasted_iota(jnp.int32, (nc, tq), 0) * NSA_CMP_STRIDE + (NSA_CMP_BLOCK - 1)
    visible = cmp_end <= t
    s = [jnp.where(visible, jnp.dot(kc, q_ref[0, r * dh:(r + 1) * dh, :], preferred_element_type=F32), NEG_BIG)
         for r in range(rep)]
    e = [jnp.where(visible, jnp.exp(x - jnp.max(x, axis=0, keepdims=True)), 0.0) for x in s]
    den = [jnp.sum(x, axis=0, keepdims=True) for x in e]
    p = [e[r] / jnp.where(den[r] > 0, den[r], 1.0) for r in range(rep)]
    for r in range(rep):
        oc_ref[0, r * dh:(r + 1) * dh, :] = jnp.dot(vct, p[r].astype(BF16), preferred_element_type=F32)
    psum = p[0]
    for r in range(1, rep):
        psum = psum + p[r]
    imp = _dot_hi(ov_ref[...], psum)
    blk = lax.broadcasted_iota(jnp.int32, (n_sel, tq), 0)
    cur = (qi * tq + lax.broadcasted_iota(jnp.int32, (n_sel, tq), 1)) // NSA_SEL_BLOCK
    forced = (blk == 0) | (blk == cur) | (blk == cur - 1)
    imp = jnp.where(forced, NSA_FORCED_SCORE, imp)
    imp = jnp.where(blk > cur, -jnp.inf, imp)
    sel = jnp.zeros((n_sel, tq), F32)
    for _ in range(topn):
        m = jnp.max(imp, axis=0, keepdims=True)
        first = jnp.min(jnp.where(imp == m, blk, n_sel), axis=0, keepdims=True)
        hit = blk == first
        sel = jnp.where(hit, 1.0, sel)
        imp = jnp.where(hit, -jnp.inf, imp)
    sel_ref[0, 0] = sel


def _flash_t_init(m_ref, l_ref, acc_ref):
    m_ref[...] = jnp.full_like(m_ref, NEG_BIG)
    l_ref[...] = jnp.zeros_like(l_ref)
    acc_ref[...] = jnp.zeros_like(acc_ref)


def _flash_t_step(q_ref, k, vt, mask, m_ref, l_ref, acc_ref, rep, dh):
    hs = range(rep)
    s = [jnp.where(mask, jnp.dot(k, q_ref[0, r * dh:(r + 1) * dh, :], preferred_element_type=F32), NEG_BIG)
         for r in hs]
    m_old = [m_ref[r] for r in hs]
    m_new = [jnp.maximum(m_old[r], jnp.max(s[r], axis=0, keepdims=True)) for r in hs]
    p = [jnp.exp(s[r] - m_new[r]) for r in hs]
    alpha = [jnp.exp(m_old[r] - m_new[r]) for r in hs]
    pv = [jnp.dot(vt, p[r].astype(BF16), preferred_element_type=F32) for r in hs]
    for r in hs:
        m_ref[r] = m_new[r]
        l_ref[r] = alpha[r] * l_ref[r] + jnp.sum(p[r], axis=0, keepdims=True)
        acc_ref[r] = acc_ref[r] * alpha[r] + pv[r]


def _nsa_select_t_kernel(qi_ref, kj_ref, q_ref, k_ref, vt_ref, sel_ref, o_ref, m_ref, l_ref, acc_ref,
                         *, tq, kb, rep, dh):
    pair = pl.program_id(2)
    qi = qi_ref[pair]
    kj = kj_ref[pair]

    @pl.when(kj == 0)
    def _():
        _flash_t_init(m_ref, l_ref, acc_ref)

    kpos = kj * kb + lax.broadcasted_iota(jnp.int32, (kb, tq), 0)
    t = qi * tq + lax.broadcasted_iota(jnp.int32, (kb, tq), 1)
    per = kb // NSA_SEL_BLOCK
    chosen = jnp.zeros((kb, tq), F32)
    for i in range(per):
        row = sel_ref[0, 0, pl.ds(kj * per + i, 1), :]
        chosen = jnp.where((kpos - kj * kb) // NSA_SEL_BLOCK == i, row, chosen)
    mask = (chosen > 0) & (kpos <= t)
    _flash_t_step(q_ref, k_ref[0], vt_ref[0], mask, m_ref, l_ref, acc_ref, rep, dh)

    @pl.when(kj * kb + kb > qi * tq + tq - 1)
    def _():
        for r in range(rep):
            l = l_ref[r]
            o_ref[0, r * dh:(r + 1) * dh, :] = acc_ref[r] / jnp.where(l > 0, l, 1.0)


def _nsa_window_t_kernel(q_ref, k_ref, vt_ref, oc_ref, os_ref, g_ref, o_ref, m_ref, l_ref, acc_ref,
                         *, tq, kb, rep, dh, window, n_steps):
    qi = pl.program_id(2)
    w = pl.program_id(3)
    kblk = qi * (tq // kb) - (n_steps - tq // kb) + w

    @pl.when(w == 0)
    def _():
        _flash_t_init(m_ref, l_ref, acc_ref)

    @pl.when(kblk >= 0)
    def _():
        kpos = kblk * kb + lax.broadcasted_iota(jnp.int32, (kb, tq), 0)
        t = qi * tq + lax.broadcasted_iota(jnp.int32, (kb, tq), 1)
        mask = (kpos <= t) & (kpos > t - window)
        _flash_t_step(q_ref, k_ref[0], vt_ref[0], mask, m_ref, l_ref, acc_ref, rep, dh)

    @pl.when(w == n_steps - 1)
    def _():
        gates = g_ref[0, 0]
        for r in range(rep):
            rows = slice(r * dh, (r + 1) * dh)
            l = l_ref[r]
            o_w = acc_ref[r] / jnp.where(l > 0, l, 1.0)
            o = (gates[3 * r:3 * r + 1, :] * oc_ref[0, rows, :] + gates[3 * r + 1:3 * r + 2, :] * os_ref[0, rows, :]
                 + gates[3 * r + 2:3 * r + 3, :] * o_w)
            o_ref[0, :, rows] = o.T.astype(o_ref.dtype)


def nsa_mixer_t(u, h, w, bsz, s_len):
    t, d = u.shape
    dh, groups = NSA_HEAD_DIM, NSA_N_KV
    n_heads = d // dh
    rep = n_heads // groups
    kvw = groups * dh
    qw = n_heads * dh
    main_w = qw + 6 * kvw
    scale = dh ** -0.5
    tq = kb = min(128, s_len)
    nq = s_len // tq
    n_sel = s_len // NSA_SEL_BLOCK
    topn = min(NSA_TOPK, n_sel)
    w_in = w["nsa_w_in"]
    proj = matmul(u, [(w_in[:, :main_w], 0)], main_w, name="nsa_in").reshape(bsz, s_len, main_w)
    gates = matmul(u, [(w_in[:, main_w:], 0)], w_in.shape[1] - main_w, epilogue=_ep_sigmoid, name="nsa_gates")
    gates = jnp.transpose(gates.reshape(bsz, s_len, groups, rep * 3), (0, 2, 3, 1))
    slot = lambda j: (qw + j * kvw) // dh
    qvt = nsa_rope_t(proj, list(range(n_heads)) + [slot(3) + g for g in range(groups)]
                     + [slot(5) + g for g in range(groups)], n_heads, dh, scale)
    k_rot = nsa_rope(proj, 0, [slot(2) + g for g in range(groups)] + [slot(4) + g for g in range(groups)], dh, 1.0)
    kc = nsa_compress(proj[..., qw:qw + kvw], w["nsa_cmp_pos_k"], w["nsa_cmp_k_w1"], w["nsa_cmp_k_w2"],
                      bsz, s_len, groups, dh, True)
    vct = nsa_compress(proj[..., qw + kvw:qw + 2 * kvw], w["nsa_cmp_pos_v"], w["nsa_cmp_v_w1"], w["nsa_cmp_v_w2"],
                       bsz, s_len, groups, dh, False, transpose_out=True)
    nc = kc.shape[1]
    cs = jnp.arange(nc)[None, :] * NSA_CMP_STRIDE
    ss = jnp.arange(n_sel)[:, None] * NSA_SEL_BLOCK
    overlap_t = jnp.clip(jnp.minimum(cs + NSA_CMP_BLOCK, ss + NSA_SEL_BLOCK) - jnp.maximum(cs, ss), 0, None)
    overlap_t = overlap_t.astype(F32) / NSA_CMP_BLOCK

    qt_spec3 = pl.BlockSpec((1, rep * dh, tq), lambda b_, g, i: (b_, g, i))
    o_c, sel = pl.pallas_call(
        functools.partial(_nsa_cmp_select_t_kernel, tq=tq, rep=rep, dh=dh, topn=topn),
        grid=(bsz, groups, nq),
        in_specs=[qt_spec3,
                  pl.BlockSpec((1, nc, dh), lambda b_, g, i: (b_ * groups + g, 0, 0)),
                  pl.BlockSpec((1, dh, nc), lambda b_, g, i: (b_ * groups + g, 0, 0)),
                  pl.BlockSpec((n_sel, nc), lambda b_, g, i: (0, 0))],
        out_specs=[qt_spec3, pl.BlockSpec((1, 1, n_sel, tq), lambda b_, g, i: (b_, g, 0, i))],
        out_shape=[jax.ShapeDtypeStruct((bsz, qw, s_len), F32),
                   jax.ShapeDtypeStruct((bsz, groups, n_sel, s_len), F32)],
        compiler_params=_params("parallel", "parallel", "parallel"),
        name="nsa_cmp_select",
    )(qvt, kc, vct, overlap_t)

    flash_scratch = [pltpu.VMEM((rep, 1, tq), F32), pltpu.VMEM((rep, 1, tq), F32), pltpu.VMEM((rep, dh, tq), F32)]
    pairs = [(i, j) for i in range(nq) for j in range((i * tq + tq - 1) // kb + 1)]
    qi_of = jnp.asarray([pr[0] for pr in pairs], jnp.int32)
    kj_of = jnp.asarray([pr[1] for pr in pairs], jnp.int32)
    o_s = pl.pallas_call(
        functools.partial(_nsa_select_t_kernel, tq=tq, kb=kb, rep=rep, dh=dh),
        grid_spec=pltpu.PrefetchScalarGridSpec(
            num_scalar_prefetch=2,
            grid=(bsz, groups, len(pairs)),
            in_specs=[pl.BlockSpec((1, rep * dh, tq), lambda b_, g, pr, qi, kj: (b_, g, qi[pr])),
                      pl.BlockSpec((1, kb, dh), lambda b_, g, pr, qi, kj: (b_, kj[pr], g)),
                      pl.BlockSpec((1, dh, kb), lambda b_, g, pr, qi, kj: (b_, n_heads + g, kj[pr])),
                      pl.BlockSpec((1, 1, n_sel, tq), lambda b_, g, pr, qi, kj: (b_, g, 0, qi[pr]))],
            out_specs=pl.BlockSpec((1, rep * dh, tq), lambda b_, g, pr, qi, kj: (b_, g, qi[pr])),
            scratch_shapes=flash_scratch),
        out_shape=jax.ShapeDtypeStruct((bsz, qw, s_len), F32),
        compiler_params=_params("parallel", "parallel", "arbitrary"),
        name="nsa_select_attn",
    )(qi_of, kj_of, qvt, k_rot, qvt, sel)

    n_steps = NSA_WINDOW // kb + tq // kb
    win_blk = lambda i, j: jnp.maximum(i * (tq // kb) - (n_steps - tq // kb) + j, 0)
    qt_spec = pl.BlockSpec((1, rep * dh, tq), lambda b_, g, i, j: (b_, g, i))
    o = pl.pallas_call(
        functools.partial(_nsa_window_t_kernel, tq=tq, kb=kb, rep=rep, dh=dh, window=NSA_WINDOW, n_steps=n_steps),
        grid=(bsz, groups, nq, n_steps),
        in_specs=[qt_spec,
                  pl.BlockSpec((1, kb, dh), lambda b_, g, i, j: (b_, win_blk(i, j), groups + g)),
                  pl.BlockSpec((1, dh, kb), lambda b_, g, i, j: (b_, n_heads + groups + g, win_blk(i, j))),
                  qt_spec, qt_spec,
                  pl.BlockSpec((1, 1, rep * 3, tq), lambda b_, g, i, j: (b_, g, 0, i))],
        out_specs=pl.BlockSpec((1, tq, rep * dh), lambda b_, g, i, j: (b_, i, g)),
        out_shape=jax.ShapeDtypeStruct((bsz, s_len, qw), BF16),
        scratch_shapes=flash_scratch,
        compiler_params=_params("parallel", "parallel", "parallel", "arbitrary"),
        name="nsa_window_attn",
    )(qvt, k_rot, qvt, o_c, o_s, gates)
    return matmul(o.reshape(t, qw), [(w["nsa_w_out"], 0)], d, epilogue=_ep_residual, extras=[(h, "mn")], name="nsa_out")


_MATMUL_WEIGHTS = ("pl_proj", "pl_gate", "mb_w_in", "mb_w_out", "nsa_w_in", "nsa_w_out", "hg_w_in", "hg_w_out",
                   "rw_w_rkv", "rw_w1", "rw_w2", "rw_a1", "rw_a2", "rw_g1", "rw_g2", "rw_w_out",
                   "ffn0_w_in", "ffn0_w_out", "moe1_w_in", "moe1_w_out", "ffn2_w_in", "ffn2_w_out",
                   "moe3_w_in", "moe3_w_out")


def kernel(x, p, norm_mix, norm_ffn, norm_pl, pl_proj, pl_gate, norm_final, mb_w_in, mb_conv_w, mb_conv_b, mb_dt_bias, mb_a_log, mb_d_skip, mb_norm_w, mb_w_out, nsa_w_in, nsa_cmp_pos_k, nsa_cmp_pos_v, nsa_cmp_k_w1, nsa_cmp_k_w2, nsa_cmp_v_w1, nsa_cmp_v_w2, nsa_w_out, hg_w_in, hg_lb_logits, hg_norm_w, hg_w_out, rw_mu, rw_w_rkv, rw_w0, rw_w1, rw_w2, rw_a0, rw_a1, rw_a2, rw_g1, rw_g2, rw_k_k, rw_k_a, rw_r_k, rw_ln_w, rw_ln_b, rw_w_out, ffn0_w_in, ffn0_w_out, moe1_router, moe1_w_in, moe1_w_out, ffn2_w_in, ffn2_w_out, moe3_router, moe3_w_in, moe3_w_out):
    w = dict(locals())
    for name in _MATMUL_WEIGHTS:
        w[name] = w[name].astype(BF16)
    bsz, s_len, d = x.shape
    depth = p.shape[0]
    t = bsz * s_len
    lb_all = jax.nn.softmax(hg_lb_logits.astype(F32), axis=0)
    lb_all = jnp.cumsum(lb_all, axis=0) - lb_all[0]
    dense = [(w["ffn0_w_in"], w["ffn0_w_out"]), (w["ffn2_w_in"], w["ffn2_w_out"])]
    moe = [(moe1_router, w["moe1_w_in"], w["moe1_w_out"]), (moe3_router, w["moe3_w_in"], w["moe3_w_out"])]
    p_bf = p.reshape(depth, t, p.shape[-1]).astype(BF16)
    h = x.reshape(t, d)
    for i in range(depth):
        kind = i % 4
        if kind == 0:
            h = mamba2_mixer(rmsnorm(h, norm_mix[i]), h, w, bsz, s_len)
        elif kind == 1:
            h = nsa_mixer_t(rmsnorm(h, norm_mix[i]), h, w, bsz, s_len)
        elif kind == 2:
            h = hgrn2_mixer(rmsnorm(h, norm_mix[i]), h, w, lb_all[i], bsz, s_len)
        else:
            h = rwkv7_mixer(rmsnorm(h, norm_mix[i], out_dtype=F32), h, w, bsz, s_len)
        v = rmsnorm(h, norm_ffn[i])
        if i % 2 == 0:
            h = dense_ffn(v, h, *dense[i // 2])
        else:
            h = moe_ffn(v, h, *moe[i // 2])
        h = ple_gate(h, p_bf[i], norm_pl[i], w["pl_proj"][i], w["pl_gate"][i])
    return rmsnorm(h, norm_final, out_dtype=F32).reshape(bsz, s_len, d)
```

```python
import functools
import math

import jax
import jax.numpy as jnp
from jax import lax
from jax.experimental import pallas as pl
from jax.experimental.pallas import tpu as pltpu

F32 = jnp.float32
BF16 = jnp.bfloat16

NORM_EPS = 1e-6
ROPE_THETA = 10000.0

V7X_VMEM_BYTES = 64 * 1024 * 1024
VMEM_LIMIT_BYTES = V7X_VMEM_BYTES - 8 * 1024 * 1024
LANES = 128

MB_HEAD_DIM = 64
MB_N_GROUPS = 8
MB_D_STATE = 128
MB_CONV = 4
MB_CHUNK = 128

NSA_HEAD_DIM = 128
NSA_N_KV = 4
NSA_CMP_BLOCK = 32
NSA_CMP_STRIDE = 16
NSA_SEL_BLOCK = 64
NSA_TOPK = 16
NSA_WINDOW = 512
NSA_FORCED_SCORE = 1e9

HG_HEAD_DIM = 128
HG_CHUNK = 32

RW_HEAD_DIM = 64
RW_LN_EPS = 64e-5
RW_CHUNK = 128

MOE_TOPK = 2


def _params(*semantics):
    return pltpu.CompilerParams(dimension_semantics=semantics, vmem_limit_bytes=VMEM_LIMIT_BYTES)


def _pick(n, target):
    if n <= target:
        return n
    for c in range(target, 0, -1):
        if n % c == 0:
            return c
    return n


def _silu(x):
    return x * jax.nn.sigmoid(x)


def _rmsnorm_kernel(x_ref, g_ref, o_ref):
    x = x_ref[...]
    ms = jnp.mean(x * x, axis=-1, keepdims=True)
    o_ref[...] = (x * lax.rsqrt(ms + NORM_EPS) * g_ref[...]).astype(o_ref.dtype)


def rmsnorm(x, gain, out_dtype=BF16, name="rmsnorm"):
    m, d = x.shape
    bm = _pick(m, 256)
    return pl.pallas_call(
        _rmsnorm_kernel,
        grid=(m // bm,),
        in_specs=[pl.BlockSpec((bm, d), lambda i: (i, 0)), pl.BlockSpec((1, d), lambda i: (0, 0))],
        out_specs=pl.BlockSpec((bm, d), lambda i: (i, 0)),
        out_shape=jax.ShapeDtypeStruct((m, d), out_dtype),
        compiler_params=_params("parallel"),
        name=name,
    )(x, gain.reshape(1, d).astype(F32))


def _mm_kernel(*refs, n_w, n_extra, nk, epilogue):
    x_ref = refs[0]
    w_refs = refs[1:1 + n_w]
    e_refs = refs[1 + n_w:1 + n_w + n_extra]
    o_ref = refs[1 + n_w + n_extra]
    acc_refs = refs[2 + n_w + n_extra:]
    x = x_ref[...]
    if nk == 1:
        accs = [jnp.dot(x, w[...], preferred_element_type=F32) for w in w_refs]
        o_ref[...] = epilogue(accs, [e[...] for e in e_refs]).astype(o_ref.dtype)
        return
    k = pl.program_id(2)

    @pl.when(k == 0)
    def _():
        for a in acc_refs:
            a[...] = jnp.zeros_like(a)

    for a, w in zip(acc_refs, w_refs):
        a[...] += jnp.dot(x, w[...], preferred_element_type=F32)

    @pl.when(k == nk - 1)
    def _():
        o_ref[...] = epilogue([a[...] for a in acc_refs], [e[...] for e in e_refs]).astype(o_ref.dtype)


def _first(accs, extras):
    return accs[0]


def matmul(x, ws, n_out, *, epilogue=_first, extras=(), out_dtype=F32, bm=1024, bn=512, bk=None, name="matmul"):
    m, kdim = x.shape
    bm = _pick(m, bm)
    bn = _pick(n_out, bn)
    if bk is None:
        bk = kdim if kdim <= 4096 else _pick(kdim, 4096)
    nk = kdim // bk
    assert kdim % bk == 0 and m % bm == 0 and n_out % bn == 0
    in_specs = [pl.BlockSpec((bm, bk), lambda i, j, k: (i, k))]
    args = [x]
    for w, off in ws:
        assert off % bn == 0 and w.shape[0] == kdim
        in_specs.append(pl.BlockSpec((bk, bn), functools.partial(lambda i, j, k, o: (k, j + o), o=off // bn)))
        args.append(w)
    for arr, kind in extras:
        if kind == "mn":
            in_specs.append(pl.BlockSpec((bm, bn), lambda i, j, k: (i, j)))
        elif kind == "m":
            in_specs.append(pl.BlockSpec((bm, arr.shape[1]), lambda i, j, k: (i, 0)))
        elif kind == "kn":
            in_specs.append(pl.BlockSpec((arr.shape[0], bn), lambda i, j, k: (0, j)))
        else:
            in_specs.append(pl.BlockSpec((1, bn), lambda i, j, k: (0, j)))
        args.append(arr)
    scratch = [pltpu.VMEM((bm, bn), F32) for _ in ws] if nk > 1 else []
    kern = functools.partial(_mm_kernel, n_w=len(ws), n_extra=len(extras), nk=nk, epilogue=epilogue)
    return pl.pallas_call(
        kern,
        grid=(m // bm, n_out // bn, nk),
        in_specs=in_specs,
        out_specs=pl.BlockSpec((bm, bn), lambda i, j, k: (i, j)),
        out_shape=jax.ShapeDtypeStruct((m, n_out), out_dtype),
        scratch_shapes=scratch,
        compiler_params=_params("parallel", "parallel", "arbitrary"),
        name=name,
    )(*args)


def _ep_residual(accs, extras):
    return extras[0] + accs[0]


def _ep_swiglu(accs, extras):
    return _silu(accs[0]) * accs[1]


def _ep_bias(accs, extras):
    return accs[0] + extras[0]


def _ep_tanh(accs, extras):
    return jnp.tanh(accs[0])


def _ep_sigmoid(accs, extras):
    return jax.nn.sigmoid(accs[0])


def _ep_bias_sigmoid(accs, extras):
    return jax.nn.sigmoid(accs[0] + extras[0])


def _ep_rw_logdecay(accs, extras):
    w = -jax.nn.softplus(-(accs[0] + extras[0])) - 0.5
    return -jnp.exp(w)


def _ep_ple_gate(accs, extras):
    return extras[0] + jnp.dot(extras[1], extras[2], preferred_element_type=F32) * jax.nn.sigmoid(accs[0])


def _conv_silu_kernel(x_ref, w_ref, b_ref, o_ref, *, k_width):
    x = x_ref[0]
    row = lax.broadcasted_iota(jnp.int32, x.shape, 0)
    y = b_ref[...] + w_ref[k_width - 1:k_width, :] * x
    for j in range(k_width - 1):
        shift = k_width - 1 - j
        xs = jnp.where(row >= shift, pltpu.roll(x, shift, 0), 0.0)
        y = y + w_ref[j:j + 1, :] * xs
    o_ref[0] = _silu(y)


def conv_silu(x, w, b):
    bsz, s_len, c = x.shape
    cb = _pick(c, 256)
    k_width = w.shape[0]
    return pl.pallas_call(
        functools.partial(_conv_silu_kernel, k_width=k_width),
        grid=(bsz, c // cb),
        in_specs=[pl.BlockSpec((1, s_len, cb), lambda b_, j: (b_, 0, j)),
                  pl.BlockSpec((k_width, cb), lambda b_, j: (0, j)),
                  pl.BlockSpec((1, cb), lambda b_, j: (0, j))],
        out_specs=pl.BlockSpec((1, s_len, cb), lambda b_, j: (b_, 0, j)),
        out_shape=jax.ShapeDtypeStruct(x.shape, F32),
        compiler_params=_params("parallel", "parallel"),
        name="mamba_conv_silu",
    )(x, w, b.reshape(1, c))


def _cumsum_rows(x, n):
    row = lax.broadcasted_iota(jnp.int32, x.shape, 0)
    s = 1
    while s < n:
        x = x + jnp.where(row >= s, pltpu.roll(x, s, 0), 0.0)
        s *= 2
    return x


def _cumsum_lanes(x, n):
    col = lax.broadcasted_iota(jnp.int32, x.shape, 1)
    s = 1
    while s < n:
        x = x + jnp.where(col >= s, pltpu.roll(x, s, 1), 0.0)
        s *= 2
    return x


def _dot_nt(a, b):
    return lax.dot_general(a, b, (((1,), (1,)), ((), ())), preferred_element_type=F32)


def _dot_tn(a, b):
    return lax.dot_general(a, b, (((0,), (0,)), ((), ())), preferred_element_type=F32)


def _ssd_kernel(xs_ref, b_ref, c_ref, z_ref, dt_ref, dtt_ref, bias_r_ref, bias_c_ref, alog_r_ref, alog_c_ref,
                dskip_ref, normw_ref, o_ref, state_ref, y_ref, *, chunk, heads, p_dim):
    @pl.when(pl.program_id(2) == 0)
    def _():
        state_ref[...] = jnp.zeros_like(state_ref)

    dt = jax.nn.softplus(dt_ref[0, 0] + bias_r_ref[0])
    dtt = jax.nn.softplus(dtt_ref[0, 0] + bias_c_ref[0])
    a_cum = _cumsum_rows(dt * -jnp.exp(alog_r_ref[0]), chunk)
    a_cum_t = _cumsum_lanes(dtt * -jnp.exp(alog_c_ref[0]), chunk)
    xs = xs_ref[0]
    bmat = b_ref[0]
    cmat = c_ref[0].astype(BF16)
    cb = _dot_nt(cmat, bmat.astype(BF16))
    b_t = bmat.T.astype(BF16)
    li = lax.broadcasted_iota(jnp.int32, (chunk, chunk), 0)
    si = lax.broadcasted_iota(jnp.int32, (chunk, chunk), 1)
    causal = li >= si
    per = LANES // p_dim
    lane_seg = lax.broadcasted_iota(jnp.int32, (1, LANES), 1) // p_dim

    def pick(vals):
        out = vals[-1]
        for i in range(per - 2, -1, -1):
            out = jnp.where(lane_seg == i, vals[i], out)
        return out

    dot = functools.partial(jnp.dot, preferred_element_type=F32)
    es = range(heads)
    tiles = range(heads // per)
    col = [a_cum[:, e:e + 1] for e in es]
    a_last = [a_cum_t[e:e + 1, chunk - 1:chunk] for e in es]
    m = [(cb * jnp.exp(jnp.where(causal, col[e] - a_cum_t[e:e + 1, :], -jnp.inf))).astype(BF16) for e in es]
    of = lambda vals, i: [vals[i * per + j] for j in range(per)]
    xdt = [xs[:, i * LANES:(i + 1) * LANES] * pick([dt[:, e:e + 1] for e in of(es, i)]) for i in tiles]
    xdt_b = [x.astype(BF16) for x in xdt]
    st = [state_ref[i] for i in tiles]
    y_in = [pick([dot(m[e], xdt_b[i]) for e in of(es, i)]) for i in tiles]
    y_st = [dot(cmat, st[i].astype(BF16)) * pick([jnp.exp(c) for c in of(col, i)]) for i in tiles]
    to_end = [pick([jnp.exp(a_last[e] - col[e]) for e in of(es, i)]) for i in tiles]
    for i in tiles:
        state_ref[i] = (st[i] * pick([jnp.exp(x) for x in of(a_last, i)])
                        + dot(b_t, (xdt[i] * to_end[i]).astype(BF16)))
        y_ref[:, i * LANES:(i + 1) * LANES] = y_in[i] + y_st[i]
    y = y_ref[...] + xs * dskip_ref[...]
    y = y * _silu(z_ref[0])
    ms = jnp.mean(y * y, axis=-1, keepdims=True)
    o_ref[0] = (y * lax.rsqrt(ms + NORM_EPS) * normw_ref[...]).astype(o_ref.dtype)


def ssd_scan(xbc, z, dt, dt_bias, a_log, d_skip, norm_w, *, chunk=MB_CHUNK):
    bsz, s_len, d_inner = z.shape
    n_heads = dt.shape[-1]
    n_state = MB_D_STATE
    groups = (xbc.shape[-1] - d_inner) // (2 * n_state)
    heads = n_heads // groups
    p_dim = d_inner // n_heads
    gw = heads * p_dim
    assert gw % LANES == 0 and d_inner % n_state == 0
    chunk = min(chunk, s_len)
    nc = s_len // chunk
    b_off = d_inner // n_state
    c_off = b_off + groups
    dt_g = jnp.transpose(dt.reshape(bsz, s_len, groups, heads), (0, 2, 1, 3))
    dt_gt = jnp.transpose(dt_g, (0, 1, 3, 2))
    kern = functools.partial(_ssd_kernel, chunk=chunk, heads=heads, p_dim=p_dim)
    per_group = lambda b_, g, c: (g, 0, 0)
    return pl.pallas_call(
        kern,
        grid=(bsz, groups, nc),
        in_specs=[pl.BlockSpec((1, chunk, gw), lambda b_, g, c: (b_, c, g)),
                  pl.BlockSpec((1, chunk, n_state), lambda b_, g, c: (b_, c, b_off + g)),
                  pl.BlockSpec((1, chunk, n_state), lambda b_, g, c: (b_, c, c_off + g)),
                  pl.BlockSpec((1, chunk, gw), lambda b_, g, c: (b_, c, g)),
                  pl.BlockSpec((1, 1, chunk, heads), lambda b_, g, c: (b_, g, c, 0)),
                  pl.BlockSpec((1, 1, heads, chunk), lambda b_, g, c: (b_, g, 0, c)),
                  pl.BlockSpec((1, 1, heads), per_group),
                  pl.BlockSpec((1, heads, 1), per_group),
                  pl.BlockSpec((1, 1, heads), per_group),
                  pl.BlockSpec((1, heads, 1), per_group),
                  pl.BlockSpec((1, gw), lambda b_, g, c: (0, g)),
                  pl.BlockSpec((1, gw), lambda b_, g, c: (0, g))],
        out_specs=pl.BlockSpec((1, chunk, gw), lambda b_, g, c: (b_, c, g)),
        out_shape=jax.ShapeDtypeStruct(z.shape, BF16),
        scratch_shapes=[pltpu.VMEM((gw // LANES, n_state, LANES), F32), pltpu.VMEM((chunk, gw), F32)],
        compiler_params=_params("parallel", "parallel", "arbitrary"),
        name="mamba_ssd",
    )(xbc, xbc, xbc, z, dt_g, dt_gt,
      dt_bias.reshape(groups, 1, heads), dt_bias.reshape(groups, heads, 1),
      a_log.reshape(groups, 1, heads), a_log.reshape(groups, heads, 1),
      jnp.repeat(d_skip, p_dim).reshape(1, d_inner), norm_w.reshape(1, d_inner))


def mamba2_mixer(u, h, w, bsz, s_len):
    d_inner = w["mb_w_out"].shape[0]
    n_heads = w["mb_dt_bias"].shape[0]
    w_in = w["mb_w_in"]
    xbc_w = w_in.shape[1] - d_inner - n_heads
    z = matmul(u, [(w_in[:, :d_inner], 0)], d_inner, name="mb_in_z")
    xbc = matmul(u, [(w_in[:, d_inner:d_inner + xbc_w], 0)], xbc_w, name="mb_in_xbc")
    dt = matmul(u, [(w_in[:, d_inner + xbc_w:], 0)], n_heads, name="mb_in_dt")
    xbc = conv_silu(xbc.reshape(bsz, s_len, xbc_w), w["mb_conv_w"], w["mb_conv_b"])
    y = ssd_scan(xbc, z.reshape(bsz, s_len, d_inner), dt.reshape(bsz, s_len, n_heads),
                 w["mb_dt_bias"], w["mb_a_log"], w["mb_d_skip"], w["mb_norm_w"])
    return matmul(y.reshape(bsz * s_len, d_inner), [(w["mb_w_out"], 0)], h.shape[1],
                  epilogue=_ep_residual, extras=[(h, "mn")], bm=512, name="mb_out")


def _seg_cumsum_rows(x, seg, reverse=False):
    n = x.shape[0]
    pos = lax.broadcasted_iota(jnp.int32, x.shape, 0) % seg
    s = 1
    while s < seg:
        if reverse:
            x = x + jnp.where(pos < seg - s, pltpu.roll(x, n - s, 0), 0.0)
        else:
            x = x + jnp.where(pos >= s, pltpu.roll(x, s, 0), 0.0)
        s *= 2
    return x


def _hgrn_kernel(q_ref, f_ref, i_ref, g_ref, lb_ref, nw_ref, o_ref, state_ref, *, sub, n_sub, heads, dk):
    @pl.when(pl.program_id(2) == 0)
    def _():
        state_ref[...] = jnp.zeros_like(state_ref)

    lb = lb_ref[...]
    nw = nw_ref[...]
    ti = lax.broadcasted_iota(jnp.int32, (sub, sub), 0)
    si = lax.broadcasted_iota(jnp.int32, (sub, sub), 1)
    causal = ti >= si
    f = lb + (1.0 - lb) * jax.nn.sigmoid(f_ref[0])
    lf = jnp.log(f)
    k = 1.0 - f
    b = _seg_cumsum_rows(lf, sub)
    to_end = _seg_cumsum_rows(lf, sub, reverse=True) - lf
    q_dec = (_silu(q_ref[0]) * jnp.exp(b)).astype(BF16)
    k_dec = (k * jnp.exp(-b)).astype(BF16)
    k_end = (k * jnp.exp(to_end)).astype(BF16)
    v = i_ref[0].astype(BF16)
    cs = range(n_sub)
    hs = range(heads)
    blk = lambda x, c, h: x[c * sub:(c + 1) * sub, h * dk:(h + 1) * dk]
    scores = [[jnp.where(causal, _dot_nt(blk(q_dec, c, h), blk(k_dec, c, h)), 0.0).astype(BF16) for h in hs]
              for c in cs]
    upd = [[_dot_tn(blk(v, c, h), blk(k_end, c, h)) for h in hs] for c in cs]
    states = []
    st = [state_ref[h] for h in hs]
    for c in cs:
        states.append(st)
        decay = jnp.exp(b[(c + 1) * sub - 1:(c + 1) * sub, :])
        st = [st[h] * decay[:, h * dk:(h + 1) * dk] + upd[c][h] for h in hs]
    for h in hs:
        state_ref[h] = st[h]
    for c in cs:
        rows = slice(c * sub, (c + 1) * sub)
        for h in hs:
            o = (jnp.dot(scores[c][h], blk(v, c, h), preferred_element_type=F32)
                 + _dot_nt(blk(q_dec, c, h), states[c][h].astype(BF16)))
            o = o * lax.rsqrt(jnp.mean(o * o, axis=-1, keepdims=True) + NORM_EPS) * nw
            cols = slice(h * dk, (h + 1) * dk)
            o_ref[0, rows, cols] = (o * _silu(g_ref[0, rows, cols])).astype(o_ref.dtype)


def hgrn2_scan(proj, lower_bound, norm_w, *, dk=HG_HEAD_DIM, sub=HG_CHUNK, tb=256, heads=4):
    bsz, s_len, d4 = proj.shape
    d = d4 // 4
    n_heads = d // dk
    tb = min(tb, s_len)
    heads = min(heads, n_heads)
    hw = heads * dk
    n_hb = n_heads // heads
    kern = functools.partial(_hgrn_kernel, sub=sub, n_sub=tb // sub, heads=heads, dk=dk)
    spec = lambda part: pl.BlockSpec((1, tb, hw), lambda b_, h_, t: (b_, t, part * n_hb + h_))
    return pl.pallas_call(
        kern,
        grid=(bsz, n_hb, s_len // tb),
        in_specs=[spec(0), spec(1), spec(2), spec(3),
                  pl.BlockSpec((1, hw), lambda b_, h_, t: (0, h_)),
                  pl.BlockSpec((1, dk), lambda b_, h_, t: (0, 0))],
        out_specs=pl.BlockSpec((1, tb, hw), lambda b_, h_, t: (b_, t, h_)),
        out_shape=jax.ShapeDtypeStruct((bsz, s_len, d), BF16),
        scratch_shapes=[pltpu.VMEM((heads, dk, dk), F32)],
        compiler_params=_params("parallel", "parallel", "arbitrary"),
        name="hgrn2_scan",
    )(proj, proj, proj, proj, lower_bound.reshape(1, d), norm_w.reshape(1, dk))


def hgrn2_mixer(u, h, w, lower_bound, bsz, s_len):
    d = h.shape[1]
    proj = matmul(u, [(w["hg_w_in"], 0)], 4 * d, name="hg_in")
    o = hgrn2_scan(proj.reshape(bsz, s_len, 4 * d), lower_bound, w["hg_norm_w"])
    return matmul(o.reshape(bsz * s_len, d), [(w["hg_w_out"], 0)], d,
                  epilogue=_ep_residual, extras=[(h, "mn")], name="hg_out")


def dense_ffn(v, h, w_in, w_out):
    f = w_out.shape[0]
    hid = matmul(v, [(w_in, 0), (w_in, f)], f, epilogue=_ep_swiglu, out_dtype=BF16, name="ffn_in")
    return matmul(hid, [(w_out, 0)], h.shape[1], epilogue=_ep_residual, extras=[(h, "mn")], bm=512, name="ffn_out")


def _router_kernel(x_ref, r_ref, o_ref, *, n_experts):
    logits = jnp.dot(x_ref[...], r_ref[...], preferred_element_type=F32)
    lane = lax.broadcasted_iota(jnp.int32, logits.shape, 1)
    logits = jnp.where(lane < n_experts, logits, -jnp.inf)
    m1 = jnp.max(logits, axis=-1, keepdims=True)
    i1 = jnp.min(jnp.where(logits == m1, lane, LANES), axis=-1, keepdims=True)
    rest = jnp.where(lane == i1, -jnp.inf, logits)
    m2 = jnp.max(rest, axis=-1, keepdims=True)
    i2 = jnp.min(jnp.where(rest == m2, lane, LANES), axis=-1, keepdims=True)
    e2 = jnp.exp(m2 - m1)
    w1 = 1.0 / (1.0 + e2)
    o_ref[...] = jnp.where(lane == i1, w1, 0.0) + jnp.where(lane == i2, e2 * w1, 0.0)


def moe_router(v, router):
    m, d = v.shape
    n_experts = router.shape[1]
    r_pad = jnp.zeros((d, LANES), BF16).at[:, :n_experts].set(router.astype(BF16))
    bm = _pick(m, 512)
    return pl.pallas_call(
        functools.partial(_router_kernel, n_experts=n_experts),
        grid=(m // bm,),
        in_specs=[pl.BlockSpec((bm, d), lambda i: (i, 0)), pl.BlockSpec((d, LANES), lambda i: (0, 0))],
        out_specs=pl.BlockSpec((bm, LANES), lambda i: (i, 0)),
        out_shape=jax.ShapeDtypeStruct((m, LANES), F32),
        compiler_params=_params("parallel"),
        name="moe_router",
    )(v, r_pad)


def _moe_in_kernel(x_ref, wg_ref, wu_ref, c_ref, o_ref, *, blocks_per_expert):
    x = x_ref[...]
    g = jnp.dot(x, wg_ref[0], preferred_element_type=F32)
    u = jnp.dot(x, wu_ref[0], preferred_element_type=F32)
    e = pl.program_id(1) // blocks_per_expert
    comb = c_ref[...]
    lane = lax.broadcasted_iota(jnp.int32, comb.shape, 1)
    scale = jnp.sum(jnp.where(lane == e, comb, 0.0), axis=-1, keepdims=True)
    o_ref[...] = (_silu(g) * u * scale).astype(o_ref.dtype)


def moe_ffn(v, h, router, w_in, w_out, *, bm=1024, bn=512):
    m, d = v.shape
    n_experts, _, two_de = w_in.shape
    de = two_de // 2
    bm = _pick(m, bm)
    bn = _pick(de, bn)
    bpe = de // bn
    comb = moe_router(v, router)
    hid = pl.pallas_call(
        functools.partial(_moe_in_kernel, blocks_per_expert=bpe),
        grid=(m // bm, n_experts * bpe),
        in_specs=[pl.BlockSpec((bm, d), lambda i, j: (i, 0)),
                  pl.BlockSpec((1, d, bn), lambda i, j: (j // bpe, 0, j % bpe)),
                  pl.BlockSpec((1, d, bn), lambda i, j: (j // bpe, 0, j % bpe + bpe)),
                  pl.BlockSpec((bm, LANES), lambda i, j: (i, 0))],
        out_specs=pl.BlockSpec((bm, bn), lambda i, j: (i, j)),
        out_shape=jax.ShapeDtypeStruct((m, n_experts * de), BF16),
        compiler_params=_params("parallel", "parallel"),
        name="moe_in",
    )(v, w_in, w_in, comb)
    return matmul(hid, [(w_out.reshape(n_experts * de, d), 0)], d, epilogue=_ep_residual, extras=[(h, "mn")],
                  name="moe_out")


def ple_gate(h, p_i, norm_pl, pl_proj, pl_gate):
    d = h.shape[1]
    n = rmsnorm(h, norm_pl, name="rmsnorm_ple")
    return matmul(n, [(pl_gate, 0)], d, epilogue=_ep_ple_gate, extras=[(h, "mn"), (p_i, "m"), (pl_proj, "kn")],
                  name="ple_gate")


def _rw_mix_kernel(u_ref, mu_ref, *o_refs):
    u = u_ref[0]
    row = lax.broadcasted_iota(jnp.int32, u.shape, 0)
    dx = jnp.where(row >= 1, pltpu.roll(u, 1, 0), 0.0) - u
    for j, o_ref in enumerate(o_refs):
        o_ref[0] = (u + dx * mu_ref[j:j + 1, :]).astype(o_ref.dtype)


def rw_token_mix(u, mu):
    bsz, s_len, d = u.shape
    cb = _pick(d, LANES)
    n_mix = mu.shape[0]
    spec = pl.BlockSpec((1, s_len, cb), lambda b_, j: (b_, 0, j))
    return pl.pallas_call(
        _rw_mix_kernel,
        grid=(bsz, d // cb),
        in_specs=[spec, pl.BlockSpec((n_mix, cb), lambda b_, j: (0, j))],
        out_specs=[spec] * n_mix,
        out_shape=[jax.ShapeDtypeStruct(u.shape, BF16)] * n_mix,
        compiler_params=_params("parallel", "parallel"),
        name="rwkv_token_mix",
    )(u, mu)


def _dot_hi(a, b):
    return jnp.dot(a, b, preferred_element_type=F32, precision=lax.Precision.HIGHEST)


def _rw_scan_kernel(r_ref, k_ref, v_ref, a_ref, lw_ref, g_ref, kk_ref, ka_ref, rk_ref, lnw_ref, lnb_ref,
                    o_ref, state_ref, *, chunk, heads, n):
    @pl.when(pl.program_id(2) == 0)
    def _():
        state_ref[...] = jnp.zeros_like(state_ref)

    hs = range(heads)
    sls = [slice(j * n, (j + 1) * n) for j in hs]
    ti = lax.broadcasted_iota(jnp.int32, (chunk, chunk), 0)
    si = lax.broadcasted_iota(jnp.int32, (chunk, chunk), 1)
    strict = ti > si
    incl = ti >= si
    dot = functools.partial(jnp.dot, preferred_element_type=F32)

    r = [r_ref[0, :, sl] for sl in sls]
    v = [v_ref[0, :, sl] for sl in sls]
    a = [a_ref[0, :, sl] for sl in sls]
    lw = [lw_ref[0, :, sl] for sl in sls]
    k = [k_ref[0, :, sl] for sl in sls]
    kk = [k[j] * kk_ref[:, sls[j]] for j in hs]
    kk = [kk[j] / jnp.maximum(jnp.sqrt(jnp.sum(kk[j] * kk[j], axis=-1, keepdims=True)), 1e-12) for j in hs]
    kmod = [k[j] * (1.0 + (a[j] - 1.0) * ka_ref[:, sls[j]]) for j in hs]
    kka = [kk[j] * a[j] for j in hs]
    cum = [_cumsum_rows(lw[j], chunk) for j in hs]
    cum_end = [c[chunk - 1:chunk, :] for c in cum]
    mid = [c[chunk // 2 - 1:chunk // 2, :] for c in cum]
    e_neg = [jnp.exp(mid[j] - cum[j]) for j in hs]
    am = [(kk[j] * jnp.exp(cum[j] - lw[j] - mid[j])).astype(BF16) for j in hs]
    bm = [(kka[j] * e_neg[j]).astype(BF16) for j in hs]
    km = [(kmod[j] * e_neg[j]).astype(BF16) for j in hs]
    rm = [(r[j] * jnp.exp(cum[j] - mid[j])).astype(BF16) for j in hs]
    a_abs = [(kk[j] * jnp.exp(cum[j] - lw[j])).astype(BF16) for j in hs]
    r_abs = [(r[j] * jnp.exp(cum[j])).astype(BF16) for j in hs]
    vb = [x.astype(BF16) for x in v]
    st = [state_ref[j] for j in hs]
    stb = [x.astype(BF16) for x in st]

    nb = [(-jnp.where(strict, _dot_nt(am[j], bm[j]), 0.0)).astype(BF16) for j in hs]
    lk = [jnp.where(strict, _dot_nt(am[j], km[j]), 0.0).astype(BF16) for j in hs]
    x = [_dot_nt(a_abs[j], stb[j]) + dot(lk[j], vb[j]) for j in hs]
    x = [x[j] + dot(nb[j], x[j].astype(BF16)) for j in hs]
    p = 2
    while p < chunk:
        nb = [dot(nb[j], nb[j]).astype(BF16) for j in hs]
        x = [x[j] + dot(nb[j], x[j].astype(BF16)) for j in hs]
        p *= 2
    pb = [xj.astype(BF16) for xj in x]
    mk = [jnp.where(incl, _dot_nt(rm[j], km[j]), 0.0).astype(BF16) for j in hs]
    mb = [jnp.where(incl, _dot_nt(rm[j], bm[j]), 0.0).astype(BF16) for j in hs]
    y = [_dot_nt(r_abs[j], stb[j]) + dot(mk[j], vb[j]) - dot(mb[j], pb[j]) for j in hs]
    to_end = [jnp.exp(cum_end[j] - cum[j]) for j in hs]
    for j in hs:
        state_ref[j] = (st[j] * jnp.exp(cum_end[j]) + _dot_tn(vb[j], (kmod[j] * to_end[j]).astype(BF16))
                        - _dot_tn(pb[j], (kka[j] * to_end[j]).astype(BF16)))
    for j in hs:
        sl = sls[j]
        bonus = jnp.sum(r[j] * kmod[j] * rk_ref[:, sl], axis=-1, keepdims=True) * v[j]
        mean = jnp.mean(y[j], axis=-1, keepdims=True)
        yc = y[j] - mean
        var = jnp.mean(yc * yc, axis=-1, keepdims=True)
        yn = yc * lax.rsqrt(var + RW_LN_EPS) * lnw_ref[:, sl] + lnb_ref[:, sl]
        o_ref[0, :, sl] = ((yn + bonus) * g_ref[0, :, sl]).astype(o_ref.dtype)


def _rw_scan_tile_kernel(r_ref, k_ref, v_ref, a_ref, lw_ref, g_ref, kk_ref, ka_ref, rk_ref, lnw_ref, lnb_ref,
                         o_ref, state_ref, *, chunk, heads, n):
    @pl.when(pl.program_id(2) == 0)
    def _():
        state_ref[...] = jnp.zeros_like(state_ref)

    per = LANES // n
    tiles = range(heads // per)
    sub = range(per)
    ti = lax.broadcasted_iota(jnp.int32, (chunk, chunk), 0)
    si = lax.broadcasted_iota(jnp.int32, (chunk, chunk), 1)
    strict = ti > si
    incl = ti >= si
    lane_seg = lax.broadcasted_iota(jnp.int32, (1, LANES), 1) // n
    seg_is = [lane_seg == j for j in sub]
    same_head = (lax.broadcasted_iota(jnp.int32, (LANES, LANES), 0) // n
                 == lax.broadcasted_iota(jnp.int32, (LANES, LANES), 1) // n)
    dot = functools.partial(jnp.dot, preferred_element_type=F32)
    tile = lambda x, i: x[:, i * LANES:(i + 1) * LANES]

    def pick(vals):
        out = vals[-1]
        for j in range(per - 2, -1, -1):
            out = jnp.where(seg_is[j], vals[j], out)
        return out

    def seg_sum(x):
        return pick([jnp.sum(jnp.where(seg_is[j], x, 0.0), axis=-1, keepdims=True) for j in sub])

    r, k, v, a, lw = r_ref[0], k_ref[0], v_ref[0], a_ref[0], lw_ref[0]
    kk = k * kk_ref[...]
    kmod = k * (1.0 + (a - 1.0) * ka_ref[...])
    cum = _cumsum_rows(lw, chunk)
    cum_end = cum[chunk - 1:chunk, :]
    mid = cum[chunk // 2 - 1:chunk // 2, :]
    bonus_in = r * kmod * rk_ref[...]
    kk_t, bonus_t = [], []
    for i in tiles:
        kki = tile(kk, i)
        kk_t.append(kki / jnp.maximum(jnp.sqrt(seg_sum(kki * kki)), 1e-12))
        bonus_t.append(seg_sum(tile(bonus_in, i)) * tile(v, i))
    kk = jnp.concatenate(kk_t, axis=-1) if len(kk_t) > 1 else kk_t[0]
    kka = kk * a
    e_neg = jnp.exp(mid - cum)
    to_end = jnp.exp(cum_end - cum)
    am = (kk * jnp.exp(cum - lw - mid)).astype(BF16)
    bm = (kka * e_neg).astype(BF16)
    km = (kmod * e_neg).astype(BF16)
    rm = (r * jnp.exp(cum - mid)).astype(BF16)
    a_abs = (kk * jnp.exp(cum - lw)).astype(BF16)
    r_abs = (r * jnp.exp(cum)).astype(BF16)
    k_end = (kmod * to_end).astype(BF16)
    b_end = (kka * to_end).astype(BF16)
    vb = v.astype(BF16)
    st_decay = jnp.exp(cum_end)
    zero = jnp.zeros((), BF16)

    st = [state_ref[i] for i in tiles]
    stb = [s.astype(BF16) for s in st]
    am_h = [[jnp.where(seg_is[j], tile(am, i), zero) for j in sub] for i in tiles]
    rm_h = [[jnp.where(seg_is[j], tile(rm, i), zero) for j in sub] for i in tiles]
    nb = [[(-jnp.where(strict, _dot_nt(am_h[i][j], tile(bm, i)), 0.0)).astype(BF16) for j in sub] for i in tiles]
    lk = [[jnp.where(strict, _dot_nt(am_h[i][j], tile(km, i)), 0.0).astype(BF16) for j in sub] for i in tiles]
    x = [_dot_nt(tile(a_abs, i), stb[i]) + pick([dot(lk[i][j], tile(vb, i)) for j in sub]) for i in tiles]
    xb = [xi.astype(BF16) for xi in x]
    x = [x[i] + pick([dot(nb[i][j], xb[i]) for j in sub]) for i in tiles]
    p = 2
    while p < chunk:
        nb = [[dot(nb[i][j], nb[i][j]).astype(BF16) for j in sub] for i in tiles]
        xb = [xi.astype(BF16) for xi in x]
        x = [x[i] + pick([dot(nb[i][j], xb[i]) for j in sub]) for i in tiles]
        p *= 2
    pb = [xi.astype(BF16) for xi in x]
    mk = [[jnp.where(incl, _dot_nt(rm_h[i][j], tile(km, i)), 0.0).astype(BF16) for j in sub] for i in tiles]
    mb = [[jnp.where(incl, _dot_nt(rm_h[i][j], tile(bm, i)), 0.0).astype(BF16) for j in sub] for i in tiles]
    y = [_dot_nt(tile(r_abs, i), stb[i])
         + pick([dot(mk[i][j], tile(vb, i)) - dot(mb[i][j], pb[i]) for j in sub]) for i in tiles]
    for i in tiles:
        upd = _dot_tn(tile(vb, i), tile(k_end, i)) - _dot_tn(pb[i], tile(b_end, i))
        state_ref[i] = st[i] * tile(st_decay, i) + jnp.where(same_head, upd, 0.0)
    inv_n = 1.0 / n
    for i in tiles:
        cols = slice(i * LANES, (i + 1) * LANES)
        mean = seg_sum(y[i]) * inv_n
        yc = y[i] - mean
        var = seg_sum(yc * yc) * inv_n
        yn = yc * lax.rsqrt(var + RW_LN_EPS) * lnw_ref[:, cols] + lnb_ref[:, cols]
        o_ref[0, :, cols] = ((yn + bonus_t[i]) * g_ref[0, :, cols]).astype(o_ref.dtype)


def rw_scan(r, k, v, a, lw, g, k_k, k_a, r_k, ln_w, ln_b, *, n=RW_HEAD_DIM, chunk=RW_CHUNK, heads=8):
    bsz, s_len, d = r.shape
    chunk = min(chunk, s_len)
    heads = min(heads, d // n)
    hw = heads * n
    seq = pl.BlockSpec((1, chunk, hw), lambda b_, h_, c: (b_, c, h_))
    par = pl.BlockSpec((1, hw), lambda b_, h_, c: (0, h_))
    row = lambda t: t.reshape(1, d)
    assert hw % LANES == 0 and LANES % n == 0
    kern = functools.partial(_rw_scan_tile_kernel, chunk=chunk, heads=heads, n=n)
    return pl.pallas_call(
        kern,
        grid=(bsz, d // hw, s_len // chunk),
        in_specs=[seq] * 6 + [par] * 5,
        out_specs=seq,
        out_shape=jax.ShapeDtypeStruct(r.shape, BF16),
        scratch_shapes=[pltpu.VMEM((hw // LANES, LANES, LANES), F32)],
        compiler_params=_params("parallel", "parallel", "arbitrary"),
        name="rwkv7_scan",
    )(r, k, v, a, lw, g, row(k_k), row(k_a), row(r_k), row(ln_w), row(ln_b))


def rwkv7_mixer(u, h, w, bsz, s_len):
    t, d = u.shape
    xr, xw, xk, xv, xa, xg = [x.reshape(t, d) for x in rw_token_mix(u.reshape(bsz, s_len, d), w["rw_mu"])]
    r = matmul(xr, [(w["rw_w_rkv"][0], 0)], d, name="rw_r")
    k = matmul(xk, [(w["rw_w_rkv"][1], 0)], d, name="rw_k")
    v = matmul(xv, [(w["rw_w_rkv"][2], 0)], d, name="rw_v")
    row = lambda x: x.reshape(1, d)
    w_lo = matmul(xw, [(w["rw_w1"], 0)], w["rw_w1"].shape[1], epilogue=_ep_tanh, out_dtype=BF16, name="rw_w1")
    lw = matmul(w_lo, [(w["rw_w2"], 0)], d, epilogue=_ep_rw_logdecay, extras=[(row(w["rw_w0"]), "n")], name="rw_w2")
    a_lo = matmul(xa, [(w["rw_a1"], 0)], w["rw_a1"].shape[1], out_dtype=BF16, name="rw_a1")
    a = matmul(a_lo, [(w["rw_a2"], 0)], d, epilogue=_ep_bias_sigmoid, extras=[(row(w["rw_a0"]), "n")], name="rw_a2")
    g_lo = matmul(xg, [(w["rw_g1"], 0)], w["rw_g1"].shape[1], epilogue=_ep_sigmoid, out_dtype=BF16, name="rw_g1")
    g = matmul(g_lo, [(w["rw_g2"], 0)], d, name="rw_g2")
    shp = (bsz, s_len, d)
    y = rw_scan(r.reshape(shp), k.reshape(shp), v.reshape(shp), a.reshape(shp), lw.reshape(shp), g.reshape(shp),
                w["rw_k_k"], w["rw_k_a"], w["rw_r_k"], w["rw_ln_w"], w["rw_ln_b"])
    return matmul(y.reshape(t, d), [(w["rw_w_out"], 0)], d, epilogue=_ep_residual, extras=[(h, "mn")], name="rw_out")


NEG_BIG = -1e30


def _rope_kernel(x_ref, cc_ref, ss_ref, o_ref, *, n_q_slots, scale):
    x = x_ref[0]
    out = x * cc_ref[...] + pltpu.roll(x, x.shape[-1] // 2, 1) * ss_ref[...]
    out = out * jnp.where(pl.program_id(2) < n_q_slots, scale, 1.0)
    o_ref[0] = out.astype(o_ref.dtype)


def _rope_tables(pos, dim):
    inv = ROPE_THETA ** (-(jnp.arange(0, dim, 2, dtype=F32) / dim))
    ang = pos.astype(F32)[:, None] * inv[None, :]
    cos, sin = jnp.cos(ang), jnp.sin(ang)
    return jnp.concatenate([cos, cos], axis=-1), jnp.concatenate([-sin, sin], axis=-1)


def nsa_rope(proj, n_q_slots, k_slots, dh, scale, tb=512):
    bsz, s_len, _ = proj.shape
    tb = min(tb, s_len)
    cc, ss = _rope_tables(jnp.arange(s_len), dh)
    n_out = n_q_slots + len(k_slots)

    def in_slot(j):
        slot = j
        for idx, ks in enumerate(k_slots):
            slot = jnp.where(j == n_q_slots + idx, ks, slot)
        return slot

    return pl.pallas_call(
        functools.partial(_rope_kernel, n_q_slots=n_q_slots, scale=scale),
        grid=(bsz, s_len // tb, n_out),
        in_specs=[pl.BlockSpec((1, tb, dh), lambda b_, t, j: (b_, t, in_slot(j))),
                  pl.BlockSpec((tb, dh), lambda b_, t, j: (t, 0)),
                  pl.BlockSpec((tb, dh), lambda b_, t, j: (t, 0))],
        out_specs=pl.BlockSpec((1, tb, dh), lambda b_, t, j: (b_, t, j)),
        out_shape=jax.ShapeDtypeStruct((bsz, s_len, n_out * dh), BF16),
        compiler_params=_params("parallel", "parallel", "arbitrary"),
        name="nsa_rope",
    )(proj, cc, ss)


def _cmp_finish_kernel(z_ref, bias_ref, w2_ref, cc_ref, ss_ref, o_ref, *, hidden, rope):
    z = z_ref[0]
    nc = z.shape[0]
    nxt = pltpu.roll(z[:, hidden:], nc - 1, 0)
    hid = _silu(z[:, :hidden] + nxt + bias_ref[...])
    out = jnp.dot(hid.astype(BF16), w2_ref[...], preferred_element_type=F32)
    if rope:
        out = out * cc_ref[...] + pltpu.roll(out, out.shape[-1] // 2, 1) * ss_ref[...]
    o_ref[0] = out.astype(o_ref.dtype)


def nsa_compress(x, pos_emb, w1, w2, bsz, s_len, groups, dh, rope, transpose_out=False):
    stride, blk = NSA_CMP_STRIDE, NSA_CMP_BLOCK
    nc = s_len // stride
    hidden = w1.shape[-1]
    half = stride * dh
    x16 = jnp.transpose(x.reshape(bsz, nc, stride, groups, dh), (0, 3, 1, 2, 4)).reshape(bsz * groups * nc, half)
    w1f = w1.reshape(blk * dh, hidden)
    wcat = jnp.concatenate([w1f[:half], w1f[half:]], axis=1).astype(BF16)
    z = matmul(x16.astype(BF16), [(wcat, 0)], 2 * hidden, name="nsa_cmp_w1")
    bias = matmul(pos_emb.reshape(1, blk * dh).astype(BF16), [(w1f.astype(BF16), 0)], hidden, name="nsa_cmp_pos")
    cc, ss = _rope_tables(jnp.arange(nc) * stride + blk - 1, dh)
    if transpose_out:
        assert not rope
        return pl.pallas_call(
            functools.partial(_cmp_finish_t_kernel, hidden=hidden),
            grid=(bsz * groups,),
            in_specs=[pl.BlockSpec((1, nc, 2 * hidden), lambda i: (i, 0, 0)),
                      pl.BlockSpec((1, hidden), lambda i: (0, 0)),
                      pl.BlockSpec((dh, hidden), lambda i: (0, 0))],
            out_specs=pl.BlockSpec((1, dh, nc), lambda i: (i, 0, 0)),
            out_shape=jax.ShapeDtypeStruct((bsz * groups, dh, nc), BF16),
            compiler_params=_params("parallel"),
            name="nsa_cmp_finish_t",
        )(z.reshape(bsz * groups, nc, 2 * hidden), bias, w2.T.astype(BF16))
    return pl.pallas_call(
        functools.partial(_cmp_finish_kernel, hidden=hidden, rope=rope),
        grid=(bsz * groups,),
        in_specs=[pl.BlockSpec((1, nc, 2 * hidden), lambda i: (i, 0, 0)),
                  pl.BlockSpec((1, hidden), lambda i: (0, 0)),
                  pl.BlockSpec((hidden, dh), lambda i: (0, 0)),
                  pl.BlockSpec((nc, dh), lambda i: (0, 0)),
                  pl.BlockSpec((nc, dh), lambda i: (0, 0))],
        out_specs=pl.BlockSpec((1, nc, dh), lambda i: (i, 0, 0)),
        out_shape=jax.ShapeDtypeStruct((bsz * groups, nc, dh), BF16),
        compiler_params=_params("parallel"),
        name="nsa_cmp_finish",
    )(z.reshape(bsz * groups, nc, 2 * hidden), bias, w2.astype(BF16), cc, ss)


def _nsa_cmp_select_kernel(q_ref, kc_ref, vc_ref, ov_ref, oc_ref, sel_ref, *, tq, rep, dh, topn):
    qi = pl.program_id(2)
    kc = kc_ref[0]
    vc = vc_ref[0]
    nc = kc.shape[0]
    n_sel = sel_ref.shape[-1]
    t = qi * tq + lax.broadcasted_iota(jnp.int32, (tq, nc), 0)
    cmp_end = lax.broadcasted_iota(jnp.int32, (tq, nc), 1) * NSA_CMP_STRIDE + (NSA_CMP_BLOCK - 1)
    visible = cmp_end <= t
    psum = jnp.zeros((tq, nc), F32)
    for r in range(rep):
        s = jnp.where(visible, _dot_nt(q_ref[0, :, r * dh:(r + 1) * dh], kc), NEG_BIG)
        m = jnp.max(s, axis=-1, keepdims=True)
        e = jnp.where(visible, jnp.exp(s - m), 0.0)
        den = jnp.sum(e, axis=-1, keepdims=True)
        p = e / jnp.where(den > 0, den, 1.0)
        oc_ref[0, :, r * dh:(r + 1) * dh] = jnp.dot(p.astype(BF16), vc, preferred_element_type=F32)
        psum = psum + p
    imp = _dot_hi(psum, ov_ref[...])
    blk = lax.broadcasted_iota(jnp.int32, (tq, n_sel), 1)
    cur = (qi * tq + lax.broadcasted_iota(jnp.int32, (tq, n_sel), 0)) // NSA_SEL_BLOCK
    forced = (blk == 0) | (blk == cur) | (blk == cur - 1)
    imp = jnp.where(forced, NSA_FORCED_SCORE, imp)
    imp = jnp.where(blk > cur, -jnp.inf, imp)
    sel = jnp.zeros((tq, n_sel), F32)
    for _ in range(topn):
        m = jnp.max(imp, axis=-1, keepdims=True)
        first = jnp.min(jnp.where(imp == m, blk, n_sel), axis=-1, keepdims=True)
        hit = blk == first
        sel = jnp.where(hit, 1.0, sel)
        imp = jnp.where(hit, -jnp.inf, imp)
    sel_ref[0, 0] = sel


def _flash_step(q_scr, k, v, mask, m_ref, l_ref, acc_ref, rep, tq):
    kb = k.shape[0]
    s = _dot_nt(q_scr[...], k).reshape(rep, tq, kb)
    s = jnp.where(mask[None], s, NEG_BIG)
    m_old = m_ref[...].reshape(rep, tq, -1)[:, :, :1]
    m_new = jnp.maximum(m_old, jnp.max(s, axis=-1, keepdims=True))
    p = jnp.where(mask[None], jnp.exp(s - m_new), 0.0)
    alpha = jnp.exp(m_old - m_new)
    l_old = l_ref[...].reshape(rep, tq, -1)[:, :, :1]
    l_new = alpha * l_old + jnp.sum(p, axis=-1, keepdims=True)
    pv = jnp.dot(p.reshape(rep * tq, kb).astype(BF16), v, preferred_element_type=F32)
    acc_ref[...] = (alpha * acc_ref[...].reshape(rep, tq, -1)).reshape(rep * tq, -1) + pv
    m_ref[...] = jnp.broadcast_to(m_new, (rep, tq, m_ref.shape[-1])).reshape(m_ref.shape)
    l_ref[...] = jnp.broadcast_to(l_new, (rep, tq, l_ref.shape[-1])).reshape(l_ref.shape)


def _flash_init(q_ref, q_scr, m_ref, l_ref, acc_ref, rep, tq, dh):
    for r in range(rep):
        q_scr[r * tq:(r + 1) * tq, :] = q_ref[0, :, r * dh:(r + 1) * dh]
    m_ref[...] = jnp.full_like(m_ref, NEG_BIG)
    l_ref[...] = jnp.zeros_like(l_ref)
    acc_ref[...] = jnp.zeros_like(acc_ref)


def _flash_result(l_ref, acc_ref):
    l = l_ref[...][:, :1]
    return acc_ref[...] / jnp.where(l > 0, l, 1.0)


def _nsa_select_kernel(q_ref, k_ref, v_ref, sel_ref, o_ref, q_scr, m_ref, l_ref, acc_ref, *, tq, kb, rep, dh):
    qi = pl.program_id(2)
    kj = pl.program_id(3)

    @pl.when(kj == 0)
    def _():
        _flash_init(q_ref, q_scr, m_ref, l_ref, acc_ref, rep, tq, dh)

    @pl.when(kj * kb <= qi * tq + tq - 1)
    def _():
        sel = sel_ref[0, 0]
        blk = lax.broadcasted_iota(jnp.int32, sel.shape, 1)
        kpos = kj * kb + lax.broadcasted_iota(jnp.int32, (tq, kb), 1)
        t = qi * tq + lax.broadcasted_iota(jnp.int32, (tq, kb), 0)
        chosen = jnp.zeros((tq, kb), F32)
        for i in range(kb // NSA_SEL_BLOCK):
            col = jnp.sum(jnp.where(blk == kj * (kb // NSA_SEL_BLOCK) + i, sel, 0.0), axis=-1, keepdims=True)
            in_blk = (kpos - kj * kb) // NSA_SEL_BLOCK == i
            chosen = jnp.where(in_blk, col, chosen)
        mask = (chosen > 0) & (kpos <= t)
        _flash_step(q_scr, k_ref[0], v_ref[0].astype(BF16), mask, m_ref, l_ref, acc_ref, rep, tq)

    @pl.when(kj == pl.num_programs(3) - 1)
    def _():
        out = _flash_result(l_ref, acc_ref)
        for r in range(rep):
            o_ref[0, :, r * dh:(r + 1) * dh] = out[r * tq:(r + 1) * tq, :]


def _nsa_window_kernel(q_ref, k_ref, v_ref, oc_ref, os_ref, g_ref, o_ref, q_scr, m_ref, l_ref, acc_ref,
                       *, tq, kb, rep, dh, window, n_steps):
    qi = pl.program_id(2)
    w = pl.program_id(3)
    kblk = qi * (tq // kb) - (n_steps - tq // kb) + w

    @pl.when(w == 0)
    def _():
        _flash_init(q_ref, q_scr, m_ref, l_ref, acc_ref, rep, tq, dh)

    @pl.when(kblk >= 0)
    def _():
        kpos = kblk * kb + lax.broadcasted_iota(jnp.int32, (tq, kb), 1)
        t = qi * tq + lax.broadcasted_iota(jnp.int32, (tq, kb), 0)
        mask = (kpos <= t) & (kpos > t - window)
        _flash_step(q_scr, k_ref[0], v_ref[0].astype(BF16), mask, m_ref, l_ref, acc_ref, rep, tq)

    @pl.when(w == n_steps - 1)
    def _():
        out = _flash_result(l_ref, acc_ref)
        gates = g_ref[0, 0]
        for r in range(rep):
            sl = slice(r * dh, (r + 1) * dh)
            o = (gates[:, 3 * r:3 * r + 1] * oc_ref[0, :, sl] + gates[:, 3 * r + 1:3 * r + 2] * os_ref[0, :, sl]
                 + gates[:, 3 * r + 2:3 * r + 3] * out[r * tq:(r + 1) * tq, :])
            o_ref[0, :, sl] = o.astype(o_ref.dtype)


def nsa_mixer(u, h, w, bsz, s_len):
    t, d = u.shape
    dh, groups = NSA_HEAD_DIM, NSA_N_KV
    n_heads = d // dh
    rep = n_heads // groups
    kvw = groups * dh
    qw = n_heads * dh
    main_w = qw + 6 * kvw
    scale = dh ** -0.5
    tq = kb = min(128, s_len)
    nq = s_len // tq
    n_sel = s_len // NSA_SEL_BLOCK
    topn = min(NSA_TOPK, n_sel)
    w_in = w["nsa_w_in"]
    proj = matmul(u, [(w_in[:, :main_w], 0)], main_w, name="nsa_in").reshape(bsz, s_len, main_w)
    gates = matmul(u, [(w_in[:, main_w:], 0)], w_in.shape[1] - main_w, epilogue=_ep_sigmoid, name="nsa_gates")
    gates = jnp.transpose(gates.reshape(bsz, s_len, groups, rep * 3), (0, 2, 1, 3))
    slot = lambda j: (qw + j * kvw) // dh
    roped = nsa_rope(proj, n_heads, [slot(2) + g for g in range(groups)] + [slot(4) + g for g in range(groups)],
                     dh, scale)
    kc = nsa_compress(proj[..., qw:qw + kvw], w["nsa_cmp_pos_k"], w["nsa_cmp_k_w1"], w["nsa_cmp_k_w2"],
                      bsz, s_len, groups, dh, True)
    vc = nsa_compress(proj[..., qw + kvw:qw + 2 * kvw], w["nsa_cmp_pos_v"], w["nsa_cmp_v_w1"], w["nsa_cmp_v_w2"],
                      bsz, s_len, groups, dh, False)
    nc = kc.shape[1]
    cs = jnp.arange(nc)[:, None] * NSA_CMP_STRIDE
    ss = jnp.arange(n_sel)[None, :] * NSA_SEL_BLOCK
    overlap = jnp.clip(jnp.minimum(cs + NSA_CMP_BLOCK, ss + NSA_SEL_BLOCK) - jnp.maximum(cs, ss), 0, None)
    overlap = overlap.astype(F32) / NSA_CMP_BLOCK

    q_spec3 = pl.BlockSpec((1, tq, rep * dh), lambda b_, g, i: (b_, i, g))
    o_c, sel = pl.pallas_call(
        functools.partial(_nsa_cmp_select_kernel, tq=tq, rep=rep, dh=dh, topn=topn),
        grid=(bsz, groups, nq),
        in_specs=[q_spec3,
                  pl.BlockSpec((1, nc, dh), lambda b_, g, i: (b_ * groups + g, 0, 0)),
                  pl.BlockSpec((1, nc, dh), lambda b_, g, i: (b_ * groups + g, 0, 0)),
                  pl.BlockSpec((nc, n_sel), lambda b_, g, i: (0, 0))],
        out_specs=[q_spec3, pl.BlockSpec((1, 1, tq, n_sel), lambda b_, g, i: (b_, g, i, 0))],
        out_shape=[jax.ShapeDtypeStruct((bsz, s_len, qw), F32),
                   jax.ShapeDtypeStruct((bsz, groups, s_len, n_sel), F32)],
        compiler_params=_params("parallel", "parallel", "parallel"),
        name="nsa_cmp_select",
    )(roped, kc, vc, overlap)

    q_spec = pl.BlockSpec((1, tq, rep * dh), lambda b_, g, i, j: (b_, i, g))
    flash_scratch = [pltpu.VMEM((rep * tq, dh), BF16), pltpu.VMEM((rep * tq, LANES), F32),
                     pltpu.VMEM((rep * tq, LANES), F32), pltpu.VMEM((rep * tq, dh), F32)]
    last_kb = lambda i: (i * tq + tq - 1) // kb
    o_s = pl.pallas_call(
        functools.partial(_nsa_select_kernel, tq=tq, kb=kb, rep=rep, dh=dh),
        grid=(bsz, groups, nq, s_len // kb),
        in_specs=[q_spec,
                  pl.BlockSpec((1, kb, dh), lambda b_, g, i, j: (b_, jnp.minimum(j, last_kb(i)), n_heads + g)),
                  pl.BlockSpec((1, kb, dh), lambda b_, g, i, j: (b_, jnp.minimum(j, last_kb(i)), slot(3) + g)),
                  pl.BlockSpec((1, 1, tq, n_sel), lambda b_, g, i, j: (b_, g, i, 0))],
        out_specs=q_spec,
        out_shape=jax.ShapeDtypeStruct((bsz, s_len, qw), F32),
        scratch_shapes=flash_scratch,
        compiler_params=_params("parallel", "parallel", "parallel", "arbitrary"),
        name="nsa_select_attn",
    )(roped, roped, proj, sel)

    n_steps = NSA_WINDOW // kb + tq // kb
    win_blk = lambda i, j: jnp.maximum(i * (tq // kb) - (n_steps - tq // kb) + j, 0)
    o = pl.pallas_call(
        functools.partial(_nsa_window_kernel, tq=tq, kb=kb, rep=rep, dh=dh, window=NSA_WINDOW, n_steps=n_steps),
        grid=(bsz, groups, nq, n_steps),
        in_specs=[q_spec,
                  pl.BlockSpec((1, kb, dh), lambda b_, g, i, j: (b_, win_blk(i, j), n_heads + groups + g)),
                  pl.BlockSpec((1, kb, dh), lambda b_, g, i, j: (b_, win_blk(i, j), slot(5) + g)),
                  q_spec, q_spec,
                  pl.BlockSpec((1, 1, tq, rep * 3), lambda b_, g, i, j: (b_, g, i, 0))],
        out_specs=q_spec,
        out_shape=jax.ShapeDtypeStruct((bsz, s_len, qw), BF16),
        scratch_shapes=flash_scratch,
        compiler_params=_params("parallel", "parallel", "parallel", "arbitrary"),
        name="nsa_window_attn",
    )(roped, roped, proj, o_c, o_s, gates)
    return matmul(o.reshape(t, qw), [(w["nsa_w_out"], 0)], d, epilogue=_ep_residual, extras=[(h, "mn")], name="nsa_out")


def _rope_t_kernel(x_ref, cc_ref, ss_ref, o_ref, *, n_rope, scale):
    x = x_ref[0]
    roped = (x * cc_ref[...] + pltpu.roll(x, x.shape[-1] // 2, 1) * ss_ref[...]) * scale
    out = jnp.where(pl.program_id(2) < n_rope, roped, x)
    o_ref[0] = out.T.astype(o_ref.dtype)


def nsa_rope_t(proj, slots, n_rope, dh, scale, tb=512):
    bsz, s_len, _ = proj.shape
    tb = min(tb, s_len)
    cc, ss = _rope_tables(jnp.arange(s_len), dh)
    table = jnp.asarray(slots, jnp.int32)
    grid_spec = pltpu.PrefetchScalarGridSpec(
        num_scalar_prefetch=1,
        grid=(bsz, s_len // tb, len(slots)),
        in_specs=[pl.BlockSpec((1, tb, dh), lambda b_, t, j, tab: (b_, t, tab[j])),
                  pl.BlockSpec((tb, dh), lambda b_, t, j, tab: (t, 0)),
                  pl.BlockSpec((tb, dh), lambda b_, t, j, tab: (t, 0))],
        out_specs=pl.BlockSpec((1, dh, tb), lambda b_, t, j, tab: (b_, j, t)),
    )
    kern = lambda tab, x_ref, cc_ref, ss_ref, o_ref: _rope_t_kernel(x_ref, cc_ref, ss_ref, o_ref,
                                                                   n_rope=n_rope, scale=scale)
    return pl.pallas_call(
        kern,
        grid_spec=grid_spec,
        out_shape=jax.ShapeDtypeStruct((bsz, len(slots) * dh, s_len), BF16),
        compiler_params=_params("parallel", "parallel", "arbitrary"),
        name="nsa_rope_t",
    )(table, proj, cc, ss)


def _cmp_finish_t_kernel(z_ref, bias_ref, w2_ref, o_ref, *, hidden):
    z = z_ref[0]
    nc = z.shape[0]
    nxt = pltpu.roll(z[:, hidden:], nc - 1, 0)
    hid = _silu(z[:, :hidden] + nxt + bias_ref[...])
    o_ref[0] = _dot_nt(w2_ref[...], hid.astype(BF16)).astype(o_ref.dtype)


def _nsa_cmp_select_t_kernel(q_ref, kc_ref, vc_ref, ov_ref, oc_ref, sel_ref, *, tq, rep, dh, topn):
    qi = pl.program_id(2)
    kc = kc_ref[0]
    vct = vc_ref[0]
    nc = kc.shape[0]
    n_sel = sel_ref.shape[2]
    t = qi * tq + lax.broadcasted_iota(jnp.int32, (nc, tq), 1)
    cmp_end = lax.broadcasted_iota(jnp.int32, (nc, tq), 0) * NSA_CMP_STRIDE + (NSA_CMP_BLOCK - 1)
    visible = cmp_end <= t
    s = [jnp.where(visible, jnp.dot(kc, q_ref[0, r * dh:(r + 1) * dh, :], preferred_element_type=F32), NEG_BIG)
         for r in range(rep)]
    e = [jnp.where(visible, jnp.exp(x - jnp.max(x, axis=0, keepdims=True)), 0.0) for x in s]
    den = [jnp.sum(x, axis=0, keepdims=True) for x in e]
    p = [e[r] / jnp.where(den[r] > 0, den[r], 1.0) for r in range(rep)]
    for r in range(rep):
        oc_ref[0, r * dh:(r + 1) * dh, :] = jnp.dot(vct, p[r].astype(BF16), preferred_element_type=F32)
    psum = p[0]
    for r in range(1, rep):
        psum = psum + p[r]
    imp = _dot_hi(ov_ref[...], psum)
    blk = lax.broadcasted_iota(jnp.int32, (n_sel, tq), 0)
    cur = (qi * tq + lax.broadcasted_iota(jnp.int32, (n_sel, tq), 1)) // NSA_SEL_BLOCK
    forced = (blk == 0) | (blk == cur) | (blk == cur - 1)
    imp = jnp.where(forced, NSA_FORCED_SCORE, imp)
    imp = jnp.where(blk > cur, -jnp.inf, imp)
    sel = jnp.zeros((n_sel, tq), F32)
    for _ in range(topn):
        m = jnp.max(imp, axis=0, keepdims=True)
        first = jnp.min(jnp.where(imp == m, blk, n_sel), axis=0, keepdims=True)
        hit = blk == first
        sel = jnp.where(hit, 1.0, sel)
        imp = jnp.where(hit, -jnp.inf, imp)
    sel_ref[0, 0] = sel


def _flash_t_init(m_ref, l_ref, acc_ref):
    m_ref[...] = jnp.full_like(m_ref, NEG_BIG)
    l_ref[...] = jnp.zeros_like(l_ref)
    acc_ref[...] = jnp.zeros_like(acc_ref)


def _flash_t_step(q_ref, k, vt, mask, m_ref, l_ref, acc_ref, rep, dh):
    hs = range(rep)
    s = [jnp.where(mask, jnp.dot(k, q_ref[0, r * dh:(r + 1) * dh, :], preferred_element_type=F32), NEG_BIG)
         for r in hs]
    m_old = [m_ref[r] for r in hs]
    m_new = [jnp.maximum(m_old[r], jnp.max(s[r], axis=0, keepdims=True)) for r in hs]
    p = [jnp.exp(s[r] - m_new[r]) for r in hs]
    alpha = [jnp.exp(m_old[r] - m_new[r]) for r in hs]
    pv = [jnp.dot(vt, p[r].astype(BF16), preferred_element_type=F32) for r in hs]
    for r in hs:
        m_ref[r] = m_new[r]
        l_ref[r] = alpha[r] * l_ref[r] + jnp.sum(p[r], axis=0, keepdims=True)
        acc_ref[r] = acc_ref[r] * alpha[r] + pv[r]


def _nsa_select_t_kernel(qi_ref, kj_ref, q_ref, k_ref, vt_ref, sel_ref, o_ref, m_ref, l_ref, acc_ref,
                         *, tq, kb, rep, dh):
    pair = pl.program_id(2)
    qi = qi_ref[pair]
    kj = kj_ref[pair]

    @pl.when(kj == 0)
    def _():
        _flash_t_init(m_ref, l_ref, acc_ref)

    kpos = kj * kb + lax.broadcasted_iota(jnp.int32, (kb, tq), 0)
    t = qi * tq + lax.broadcasted_iota(jnp.int32, (kb, tq), 1)
    per = kb // NSA_SEL_BLOCK
    chosen = jnp.zeros((kb, tq), F32)
    for i in range(per):
        row = sel_ref[0, 0, pl.ds(kj * per + i, 1), :]
        chosen = jnp.where((kpos - kj * kb) // NSA_SEL_BLOCK == i, row, chosen)
    mask = (chosen > 0) & (kpos <= t)
    _flash_t_step(q_ref, k_ref[0], vt_ref[0], mask, m_ref, l_ref, acc_ref, rep, dh)

    @pl.when(kj * kb + kb > qi * tq + tq - 1)
    def _():
        for r in range(rep):
            l = l_ref[r]
            o_ref[0, r * dh:(r + 1) * dh, :] = acc_ref[r] / jnp.where(l > 0, l, 1.0)


def _nsa_window_t_kernel(q_ref, k_ref, vt_ref, oc_ref, os_ref, g_ref, o_ref, m_ref, l_ref, acc_ref,
                         *, tq, kb, rep, dh, window, n_steps):
    qi = pl.program_id(2)
    w = pl.program_id(3)
    kblk = qi * (tq // kb) - (n_steps - tq // kb) + w

    @pl.when(w == 0)
    def _():
        _flash_t_init(m_ref, l_ref, acc_ref)

    @pl.when(kblk >= 0)
    def _():
        kpos = kblk * kb + lax.broadcasted_iota(jnp.int32, (kb, tq), 0)
        t = qi * tq + lax.broadcasted_iota(jnp.int32, (kb, tq), 1)
        mask = (kpos <= t) & (kpos > t - window)
        _flash_t_step(q_ref, k_ref[0], vt_ref[0], mask, m_ref, l_ref, acc_ref, rep, dh)

    @pl.when(w == n_steps - 1)
    def _():
        gates = g_ref[0, 0]
        for r in range(rep):
            rows = slice(r * dh, (r + 1) * dh)
            l = l_ref[r]
            o_w = acc_ref[r] / jnp.where(l > 0, l, 1.0)
            o = (gates[3 * r:3 * r + 1, :] * oc_ref[0, rows, :] + gates[3 * r + 1:3 * r + 2, :] * os_ref[0, rows, :]
                 + gates[3 * r + 2:3 * r + 3, :] * o_w)
            o_ref[0, :, rows] = o.T.astype(o_ref.dtype)


def nsa_mixer_t(u, h, w, bsz, s_len):
    t, d = u.shape
    dh, groups = NSA_HEAD_DIM, NSA_N_KV
    n_heads = d // dh
    rep = n_heads // groups
    kvw = groups * dh
    qw = n_heads * dh
    main_w = qw + 6 * kvw
    scale = dh ** -0.5
    tq = kb = min(128, s_len)
    nq = s_len // tq
    n_sel = s_len // NSA_SEL_BLOCK
    topn = min(NSA_TOPK, n_sel)
    w_in = w["nsa_w_in"]
    proj = matmul(u, [(w_in[:, :main_w], 0)], main_w, name="nsa_in").reshape(bsz, s_len, main_w)
    gates = matmul(u, [(w_in[:, main_w:], 0)], w_in.shape[1] - main_w, epilogue=_ep_sigmoid, name="nsa_gates")
    gates = jnp.transpose(gates.reshape(bsz, s_len, groups, rep * 3), (0, 2, 3, 1))
    slot = lambda j: (qw + j * kvw) // dh
    qvt = nsa_rope_t(proj, list(range(n_heads)) + [slot(3) + g for g in range(groups)]
                     + [slot(5) + g for g in range(groups)], n_heads, dh, scale)
    k_rot = nsa_rope(proj, 0, [slot(2) + g for g in range(groups)] + [slot(4) + g for g in range(groups)], dh, 1.0)
    kc = nsa_compress(proj[..., qw:qw + kvw], w["nsa_cmp_pos_k"], w["nsa_cmp_k_w1"], w["nsa_cmp_k_w2"],
                      bsz, s_len, groups, dh, True)
    vct = nsa_compress(proj[..., qw + kvw:qw + 2 * kvw], w["nsa_cmp_pos_v"], w["nsa_cmp_v_w1"], w["nsa_cmp_v_w2"],
                       bsz, s_len, groups, dh, False, transpose_out=True)
    nc = kc.shape[1]
    cs = jnp.arange(nc)[None, :] * NSA_CMP_STRIDE
    ss = jnp.arange(n_sel)[:, None] * NSA_SEL_BLOCK
    overlap_t = jnp.clip(jnp.minimum(cs + NSA_CMP_BLOCK, ss + NSA_SEL_BLOCK) - jnp.maximum(cs, ss), 0, None)
    overlap_t = overlap_t.astype(F32) / NSA_CMP_BLOCK

    qt_spec3 = pl.BlockSpec((1, rep * dh, tq), lambda b_, g, i: (b_, g, i))
    o_c, sel = pl.pallas_call(
        functools.partial(_nsa_cmp_select_t_kernel, tq=tq, rep=rep, dh=dh, topn=topn),
        grid=(bsz, groups, nq),
        in_specs=[qt_spec3,
                  pl.BlockSpec((1, nc, dh), lambda b_, g, i: (b_ * groups + g, 0, 0)),
                  pl.BlockSpec((1, dh, nc), lambda b_, g, i: (b_ * groups + g, 0, 0)),
                  pl.BlockSpec((n_sel, nc), lambda b_, g, i: (0, 0))],
        out_specs=[qt_spec3, pl.BlockSpec((1, 1, n_sel, tq), lambda b_, g, i: (b_, g, 0, i))],
        out_shape=[jax.ShapeDtypeStruct((bsz, qw, s_len), F32),
                   jax.ShapeDtypeStruct((bsz, groups, n_sel, s_len), F32)],
        compiler_params=_params("parallel", "parallel", "parallel"),
        name="nsa_cmp_select",
    )(qvt, kc, vct, overlap_t)

    flash_scratch = lambda n: [pltpu.VMEM((rep, 1, n), F32), pltpu.VMEM((rep, 1, n), F32),
                               pltpu.VMEM((rep, dh, n), F32)]
    tqs = min(2 * tq, s_len)
    pairs = [(i, j) for i in range(s_len // tqs) for j in range((i * tqs + tqs - 1) // kb + 1)]
    qi_of = jnp.asarray([pr[0] for pr in pairs], jnp.int32)
    kj_of = jnp.asarray([pr[1] for pr in pairs], jnp.int32)
    o_s = pl.pallas_call(
        functools.partial(_nsa_select_t_kernel, tq=tqs, kb=kb, rep=rep, dh=dh),
        grid_spec=pltpu.PrefetchScalarGridSpec(
            num_scalar_prefetch=2,
            grid=(bsz, groups, len(pairs)),
            in_specs=[pl.BlockSpec((1, rep * dh, tqs), lambda b_, g, pr, qi, kj: (b_, g, qi[pr])),
                      pl.BlockSpec((1, kb, dh), lambda b_, g, pr, qi, kj: (b_, kj[pr], g)),
                      pl.BlockSpec((1, dh, kb), lambda b_, g, pr, qi, kj: (b_, n_heads + g, kj[pr])),
                      pl.BlockSpec((1, 1, n_sel, tqs), lambda b_, g, pr, qi, kj: (b_, g, 0, qi[pr]))],
            out_specs=pl.BlockSpec((1, rep * dh, tqs), lambda b_, g, pr, qi, kj: (b_, g, qi[pr])),
            scratch_shapes=flash_scratch(tqs)),
        out_shape=jax.ShapeDtypeStruct((bsz, qw, s_len), F32),
        compiler_params=_params("parallel", "parallel", "arbitrary"),
        name="nsa_select_attn",
    )(qi_of, kj_of, qvt, k_rot, qvt, sel)

    n_steps = NSA_WINDOW // kb + tq // kb
    win_blk = lambda i, j: jnp.maximum(i * (tq // kb) - (n_steps - tq // kb) + j, 0)
    qt_spec = pl.BlockSpec((1, rep * dh, tq), lambda b_, g, i, j: (b_, g, i))
    o = pl.pallas_call(
        functools.partial(_nsa_window_t_kernel, tq=tq, kb=kb, rep=rep, dh=dh, window=NSA_WINDOW, n_steps=n_steps),
        grid=(bsz, groups, nq, n_steps),
        in_specs=[qt_spec,
                  pl.BlockSpec((1, kb, dh), lambda b_, g, i, j: (b_, win_blk(i, j), groups + g)),
                  pl.BlockSpec((1, dh, kb), lambda b_, g, i, j: (b_, n_heads + groups + g, win_blk(i, j))),
                  qt_spec, qt_spec,
                  pl.BlockSpec((1, 1, rep * 3, tq), lambda b_, g, i, j: (b_, g, 0, i))],
        out_specs=pl.BlockSpec((1, tq, rep * dh), lambda b_, g, i, j: (b_, i, g)),
        out_shape=jax.ShapeDtypeStruct((bsz, s_len, qw), BF16),
        scratch_shapes=flash_scratch(tq),
        compiler_params=_params("parallel", "parallel", "parallel", "arbitrary"),
        name="nsa_window_attn",
    )(qvt, k_rot, qvt, o_c, o_s, gates)
    return matmul(o.reshape(t, qw), [(w["nsa_w_out"], 0)], d, epilogue=_ep_residual, extras=[(h, "mn")], name="nsa_out")


_MATMUL_WEIGHTS = ("pl_proj", "pl_gate", "mb_w_in", "mb_w_out", "nsa_w_in", "nsa_w_out", "hg_w_in", "hg_w_out",
                   "rw_w_rkv", "rw_w1", "rw_w2", "rw_a1", "rw_a2", "rw_g1", "rw_g2", "rw_w_out",
                   "ffn0_w_in", "ffn0_w_out", "moe1_w_in", "moe1_w_out", "ffn2_w_in", "ffn2_w_out",
                   "moe3_w_in", "moe3_w_out")


def kernel(x, p, norm_mix, norm_ffn, norm_pl, pl_proj, pl_gate, norm_final, mb_w_in, mb_conv_w, mb_conv_b, mb_dt_bias, mb_a_log, mb_d_skip, mb_norm_w, mb_w_out, nsa_w_in, nsa_cmp_pos_k, nsa_cmp_pos_v, nsa_cmp_k_w1, nsa_cmp_k_w2, nsa_cmp_v_w1, nsa_cmp_v_w2, nsa_w_out, hg_w_in, hg_lb_logits, hg_norm_w, hg_w_out, rw_mu, rw_w_rkv, rw_w0, rw_w1, rw_w2, rw_a0, rw_a1, rw_a2, rw_g1, rw_g2, rw_k_k, rw_k_a, rw_r_k, rw_ln_w, rw_ln_b, rw_w_out, ffn0_w_in, ffn0_w_out, moe1_router, moe1_w_in, moe1_w_out, ffn2_w_in, ffn2_w_out, moe3_router, moe3_w_in, moe3_w_out):
    w = dict(locals())
    for name in _MATMUL_WEIGHTS:
        w[name] = w[name].astype(BF16)
    bsz, s_len, d = x.shape
    depth = p.shape[0]
    t = bsz * s_len
    lb_all = jax.nn.softmax(hg_lb_logits.astype(F32), axis=0)
    lb_all = jnp.cumsum(lb_all, axis=0) - lb_all[0]
    dense = [(w["ffn0_w_in"], w["ffn0_w_out"]), (w["ffn2_w_in"], w["ffn2_w_out"])]
    moe = [(moe1_router, w["moe1_w_in"], w["moe1_w_out"]), (moe3_router, w["moe3_w_in"], w["moe3_w_out"])]
    p_bf = p.reshape(depth, t, p.shape[-1]).astype(BF16)
    h = x.reshape(t, d)
    for i in range(depth):
        kind = i % 4
        if kind == 0:
            h = mamba2_mixer(rmsnorm(h, norm_mix[i]), h, w, bsz, s_len)
        elif kind == 1:
            h = nsa_mixer_t(rmsnorm(h, norm_mix[i]), h, w, bsz, s_len)
        elif kind == 2:
            h = hgrn2_mixer(rmsnorm(h, norm_mix[i]), h, w, lb_all[i], bsz, s_len)
        else:
            h = rwkv7_mixer(rmsnorm(h, norm_mix[i], out_dtype=F32), h, w, bsz, s_len)
        v = rmsnorm(h, norm_ffn[i])
        if i % 2 == 0:
            h = dense_ffn(v, h, *dense[i // 2])
        else:
            h = moe_ffn(v, h, *moe[i // 2])
        h = ple_gate(h, p_bf[i], norm_pl[i], w["pl_proj"][i], w["pl_gate"][i])
    return rmsnorm(h, norm_final, out_dtype=F32).reshape(bsz, s_len, d)
```

```python
import functools
import math

import jax
import jax.numpy as jnp
from jax import lax
from jax.experimental import pallas as pl
from jax.experimental.pallas import tpu as pltpu

F32 = jnp.float32
BF16 = jnp.bfloat16

NORM_EPS = 1e-6
ROPE_THETA = 10000.0

V7X_VMEM_BYTES = 64 * 1024 * 1024
VMEM_LIMIT_BYTES = V7X_VMEM_BYTES - 8 * 1024 * 1024
LANES = 128

MB_HEAD_DIM = 64
MB_N_GROUPS = 8
MB_D_STATE = 128
MB_CONV = 4
MB_CHUNK = 128

NSA_HEAD_DIM = 128
NSA_N_KV = 4
NSA_CMP_BLOCK = 32
NSA_CMP_STRIDE = 16
NSA_SEL_BLOCK = 64
NSA_TOPK = 16
NSA_WINDOW = 512
NSA_FORCED_SCORE = 1e9

HG_HEAD_DIM = 128
HG_CHUNK = 32

RW_HEAD_DIM = 64
RW_LN_EPS = 64e-5
RW_CHUNK = 128

MOE_TOPK = 2


def _params(*semantics):
    return pltpu.CompilerParams(dimension_semantics=semantics, vmem_limit_bytes=VMEM_LIMIT_BYTES)


def _pick(n, target):
    if n <= target:
        return n
    for c in range(target, 0, -1):
        if n % c == 0:
            return c
    return n


def _silu(x):
    return x * jax.nn.sigmoid(x)


def _rmsnorm_kernel(x_ref, g_ref, o_ref):
    x = x_ref[...]
    ms = jnp.mean(x * x, axis=-1, keepdims=True)
    o_ref[...] = (x * lax.rsqrt(ms + NORM_EPS) * g_ref[...]).astype(o_ref.dtype)


def rmsnorm(x, gain, out_dtype=BF16, name="rmsnorm"):
    m, d = x.shape
    bm = _pick(m, 256)
    return pl.pallas_call(
        _rmsnorm_kernel,
        grid=(m // bm,),
        in_specs=[pl.BlockSpec((bm, d), lambda i: (i, 0)), pl.BlockSpec((1, d), lambda i: (0, 0))],
        out_specs=pl.BlockSpec((bm, d), lambda i: (i, 0)),
        out_shape=jax.ShapeDtypeStruct((m, d), out_dtype),
        compiler_params=_params("parallel"),
        name=name,
    )(x, gain.reshape(1, d).astype(F32))


def _mm_kernel(*refs, n_w, n_extra, nk, epilogue):
    x_ref = refs[0]
    w_refs = refs[1:1 + n_w]
    e_refs = refs[1 + n_w:1 + n_w + n_extra]
    o_ref = refs[1 + n_w + n_extra]
    acc_refs = refs[2 + n_w + n_extra:]
    x = x_ref[...]
    if nk == 1:
        accs = [jnp.dot(x, w[...], preferred_element_type=F32) for w in w_refs]
        o_ref[...] = epilogue(accs, [e[...] for e in e_refs]).astype(o_ref.dtype)
        return
    k = pl.program_id(2)

    @pl.when(k == 0)
    def _():
        for a in acc_refs:
            a[...] = jnp.zeros_like(a)

    for a, w in zip(acc_refs, w_refs):
        a[...] += jnp.dot(x, w[...], preferred_element_type=F32)

    @pl.when(k == nk - 1)
    def _():
        o_ref[...] = epilogue([a[...] for a in acc_refs], [e[...] for e in e_refs]).astype(o_ref.dtype)


def _first(accs, extras):
    return accs[0]


def matmul(x, ws, n_out, *, epilogue=_first, extras=(), out_dtype=F32, bm=1024, bn=512, bk=None, name="matmul"):
    m, kdim = x.shape
    bm = _pick(m, bm)
    bn = _pick(n_out, bn)
    if bk is None:
        bk = kdim if kdim <= 4096 else _pick(kdim, 4096)
    nk = kdim // bk
    assert kdim % bk == 0 and m % bm == 0 and n_out % bn == 0
    in_specs = [pl.BlockSpec((bm, bk), lambda i, j, k: (i, k))]
    args = [x]
    for w, off in ws:
        assert off % bn == 0 and w.shape[0] == kdim
        in_specs.append(pl.BlockSpec((bk, bn), functools.partial(lambda i, j, k, o: (k, j + o), o=off // bn)))
        args.append(w)
    for arr, kind in extras:
        if kind == "mn":
            in_specs.append(pl.BlockSpec((bm, bn), lambda i, j, k: (i, j)))
        elif kind == "m":
            in_specs.append(pl.BlockSpec((bm, arr.shape[1]), lambda i, j, k: (i, 0)))
        elif kind == "kn":
            in_specs.append(pl.BlockSpec((arr.shape[0], bn), lambda i, j, k: (0, j)))
        else:
            in_specs.append(pl.BlockSpec((1, bn), lambda i, j, k: (0, j)))
        args.append(arr)
    scratch = [pltpu.VMEM((bm, bn), F32) for _ in ws] if nk > 1 else []
    kern = functools.partial(_mm_kernel, n_w=len(ws), n_extra=len(extras), nk=nk, epilogue=epilogue)
    return pl.pallas_call(
        kern,
        grid=(m // bm, n_out // bn, nk),
        in_specs=in_specs,
        out_specs=pl.BlockSpec((bm, bn), lambda i, j, k: (i, j)),
        out_shape=jax.ShapeDtypeStruct((m, n_out), out_dtype),
        scratch_shapes=scratch,
        compiler_params=_params("parallel", "parallel", "arbitrary"),
        name=name,
    )(*args)


def _mm_ws_kernel(*refs, n_w, n_extra, epilogue):
    x_ref = refs[0]
    w_refs = refs[1:1 + n_w]
    e_refs = refs[1 + n_w:1 + n_w + n_extra]
    o_ref = refs[1 + n_w + n_extra]
    wb_refs = refs[2 + n_w + n_extra:]

    @pl.when(pl.program_id(1) == 0)
    def _():
        for w, wb in zip(w_refs, wb_refs):
            wb[...] = w[...].reshape(wb.shape).astype(BF16)

    x = x_ref[...]
    accs = [jnp.dot(x, wb[...], preferred_element_type=F32) for wb in wb_refs]
    o_ref[...] = epilogue(accs, [e[...] for e in e_refs]).astype(o_ref.dtype)


def matmul_ws(x, ws, n_out, *, epilogue=_first, extras=(), out_dtype=F32, bm=1024, bn=512, w_buffers=2,
              name="matmul_ws"):
    m, kdim = x.shape
    bm = _pick(m, bm)
    bn = _pick(n_out, bn)
    assert m % bm == 0 and n_out % bn == 0
    mode = {} if w_buffers == 2 else {"pipeline_mode": pl.Buffered(w_buffers)}
    in_specs = [pl.BlockSpec((bm, kdim), lambda j, i: (i, 0))]
    args = [x]
    for w, off in ws:
        if w.ndim == 3:
            e, o = off
            assert o % bn == 0 and w.shape[1] == kdim
            in_specs.append(pl.BlockSpec((1, kdim, bn), functools.partial(lambda j, i, e_, o_: (e_, 0, j + o_),
                                                                          e_=e, o_=o // bn), **mode))
        else:
            assert off % bn == 0 and w.shape[0] == kdim
            in_specs.append(pl.BlockSpec((kdim, bn), functools.partial(lambda j, i, o_: (0, j + o_), o_=off // bn),
                                         **mode))
        args.append(w)
    for arr, kind in extras:
        if kind == "mn":
            in_specs.append(pl.BlockSpec((bm, bn), lambda j, i: (i, j)))
        elif kind == "m":
            in_specs.append(pl.BlockSpec((bm, arr.shape[1]), lambda j, i: (i, 0)))
        elif kind == "kn":
            in_specs.append(pl.BlockSpec((arr.shape[0], bn), lambda j, i: (0, j)))
        else:
            in_specs.append(pl.BlockSpec((1, bn), lambda j, i: (0, j)))
        args.append(arr)
    kern = functools.partial(_mm_ws_kernel, n_w=len(ws), n_extra=len(extras), epilogue=epilogue)
    return pl.pallas_call(
        kern,
        grid=(n_out // bn, m // bm),
        in_specs=in_specs,
        out_specs=pl.BlockSpec((bm, bn), lambda j, i: (i, j)),
        out_shape=jax.ShapeDtypeStruct((m, n_out), out_dtype),
        scratch_shapes=[pltpu.VMEM((kdim, bn), BF16) for _ in ws],
        compiler_params=_params("parallel", "arbitrary"),
        name=name,
    )(*args)


def _ep_residual(accs, extras):
    return extras[0] + accs[0]


def _ep_swiglu(accs, extras):
    return _silu(accs[0]) * accs[1]


def _ep_bias(accs, extras):
    return accs[0] + extras[0]


def _ep_tanh(accs, extras):
    return jnp.tanh(accs[0])


def _ep_sigmoid(accs, extras):
    return jax.nn.sigmoid(accs[0])


def _ep_bias_sigmoid(accs, extras):
    return jax.nn.sigmoid(accs[0] + extras[0])


def _ep_rw_logdecay(accs, extras):
    w = -jax.nn.softplus(-(accs[0] + extras[0])) - 0.5
    return -jnp.exp(w)


def _ep_ple_gate(accs, extras):
    return extras[0] + jnp.dot(extras[1], extras[2], preferred_element_type=F32) * jax.nn.sigmoid(accs[0])


def _conv_silu_kernel(x_ref, w_ref, b_ref, o_ref, *, k_width):
    x = x_ref[0]
    row = lax.broadcasted_iota(jnp.int32, x.shape, 0)
    y = b_ref[...] + w_ref[k_width - 1:k_width, :] * x
    for j in range(k_width - 1):
        shift = k_width - 1 - j
        xs = jnp.where(row >= shift, pltpu.roll(x, shift, 0), 0.0)
        y = y + w_ref[j:j + 1, :] * xs
    o_ref[0] = _silu(y)


def conv_silu(x, w, b):
    bsz, s_len, c = x.shape
    cb = _pick(c, 256)
    k_width = w.shape[0]
    return pl.pallas_call(
        functools.partial(_conv_silu_kernel, k_width=k_width),
        grid=(bsz, c // cb),
        in_specs=[pl.BlockSpec((1, s_len, cb), lambda b_, j: (b_, 0, j)),
                  pl.BlockSpec((k_width, cb), lambda b_, j: (0, j)),
                  pl.BlockSpec((1, cb), lambda b_, j: (0, j))],
        out_specs=pl.BlockSpec((1, s_len, cb), lambda b_, j: (b_, 0, j)),
        out_shape=jax.ShapeDtypeStruct(x.shape, F32),
        compiler_params=_params("parallel", "parallel"),
        name="mamba_conv_silu",
    )(x, w, b.reshape(1, c))


def _cumsum_rows(x, n):
    row = lax.broadcasted_iota(jnp.int32, x.shape, 0)
    s = 1
    while s < n:
        x = x + jnp.where(row >= s, pltpu.roll(x, s, 0), 0.0)
        s *= 2
    return x


def _cumsum_lanes(x, n):
    col = lax.broadcasted_iota(jnp.int32, x.shape, 1)
    s = 1
    while s < n:
        x = x + jnp.where(col >= s, pltpu.roll(x, s, 1), 0.0)
        s *= 2
    return x


def _dot_nt(a, b):
    return lax.dot_general(a, b, (((1,), (1,)), ((), ())), preferred_element_type=F32)


def _dot_tn(a, b):
    return lax.dot_general(a, b, (((0,), (0,)), ((), ())), preferred_element_type=F32)


def _ssd_kernel(xs_ref, b_ref, c_ref, z_ref, dt_ref, dtt_ref, bias_r_ref, bias_c_ref, alog_r_ref, alog_c_ref,
                dskip_ref, normw_ref, o_ref, state_ref, y_ref, *, chunk, heads, p_dim):
    @pl.when(pl.program_id(2) == 0)
    def _():
        state_ref[...] = jnp.zeros_like(state_ref)

    dt = jax.nn.softplus(dt_ref[0, 0] + bias_r_ref[0])
    dtt = jax.nn.softplus(dtt_ref[0, 0] + bias_c_ref[0])
    a_cum = _cumsum_rows(dt * -jnp.exp(alog_r_ref[0]), chunk)
    a_cum_t = _cumsum_lanes(dtt * -jnp.exp(alog_c_ref[0]), chunk)
    xs = xs_ref[0]
    bmat = b_ref[0]
    cmat = c_ref[0].astype(BF16)
    cb = _dot_nt(cmat, bmat.astype(BF16))
    b_t = bmat.T.astype(BF16)
    li = lax.broadcasted_iota(jnp.int32, (chunk, chunk), 0)
    si = lax.broadcasted_iota(jnp.int32, (chunk, chunk), 1)
    causal = li >= si
    per = LANES // p_dim
    lane_seg = lax.broadcasted_iota(jnp.int32, (1, LANES), 1) // p_dim

    def pick(vals):
        out = vals[-1]
        for i in range(per - 2, -1, -1):
            out = jnp.where(lane_seg == i, vals[i], out)
        return out

    dot = functools.partial(jnp.dot, preferred_element_type=F32)
    es = range(heads)
    tiles = range(heads // per)
    col = [a_cum[:, e:e + 1] for e in es]
    a_last = [a_cum_t[e:e + 1, chunk - 1:chunk] for e in es]
    m = [(cb * jnp.exp(jnp.where(causal, col[e] - a_cum_t[e:e + 1, :], -jnp.inf))).astype(BF16) for e in es]
    of = lambda vals, i: [vals[i * per + j] for j in range(per)]
    xdt = [xs[:, i * LANES:(i + 1) * LANES] * pick([dt[:, e:e + 1] for e in of(es, i)]) for i in tiles]
    xdt_b = [x.astype(BF16) for x in xdt]
    st = [state_ref[i] for i in tiles]
    y_in = [pick([dot(m[e], xdt_b[i]) for e in of(es, i)]) for i in tiles]
    y_st = [dot(cmat, st[i].astype(BF16)) * pick([jnp.exp(c) for c in of(col, i)]) for i in tiles]
    to_end = [pick([jnp.exp(a_last[e] - col[e]) for e in of(es, i)]) for i in tiles]
    for i in tiles:
        state_ref[i] = (st[i] * pick([jnp.exp(x) for x in of(a_last, i)])
                        + dot(b_t, (xdt[i] * to_end[i]).astype(BF16)))
        y_ref[:, i * LANES:(i + 1) * LANES] = y_in[i] + y_st[i]
    y = y_ref[...] + xs * dskip_ref[...]
    y = y * _silu(z_ref[0])
    ms = jnp.mean(y * y, axis=-1, keepdims=True)
    o_ref[0] = (y * lax.rsqrt(ms + NORM_EPS) * normw_ref[...]).astype(o_ref.dtype)


def ssd_scan(xbc, z, dt, dt_bias, a_log, d_skip, norm_w, *, chunk=MB_CHUNK):
    bsz, s_len, d_inner = z.shape
    n_heads = dt.shape[-1]
    n_state = MB_D_STATE
    groups = (xbc.shape[-1] - d_inner) // (2 * n_state)
    heads = n_heads // groups
    p_dim = d_inner // n_heads
    gw = heads * p_dim
    assert gw % LANES == 0 and d_inner % n_state == 0
    chunk = min(chunk, s_len)
    nc = s_len // chunk
    b_off = d_inner // n_state
    c_off = b_off + groups
    dt_g = jnp.transpose(dt.reshape(bsz, s_len, groups, heads), (0, 2, 1, 3))
    dt_gt = jnp.transpose(dt_g, (0, 1, 3, 2))
    kern = functools.partial(_ssd_kernel, chunk=chunk, heads=heads, p_dim=p_dim)
    per_group = lambda b_, g, c: (g, 0, 0)
    return pl.pallas_call(
        kern,
        grid=(bsz, groups, nc),
        in_specs=[pl.BlockSpec((1, chunk, gw), lambda b_, g, c: (b_, c, g)),
                  pl.BlockSpec((1, chunk, n_state), lambda b_, g, c: (b_, c, b_off + g)),
                  pl.BlockSpec((1, chunk, n_state), lambda b_, g, c: (b_, c, c_off + g)),
                  pl.BlockSpec((1, chunk, gw), lambda b_, g, c: (b_, c, g)),
                  pl.BlockSpec((1, 1, chunk, heads), lambda b_, g, c: (b_, g, c, 0)),
                  pl.BlockSpec((1, 1, heads, chunk), lambda b_, g, c: (b_, g, 0, c)),
                  pl.BlockSpec((1, 1, heads), per_group),
                  pl.BlockSpec((1, heads, 1), per_group),
                  pl.BlockSpec((1, 1, heads), per_group),
                  pl.BlockSpec((1, heads, 1), per_group),
                  pl.BlockSpec((1, gw), lambda b_, g, c: (0, g)),
                  pl.BlockSpec((1, gw), lambda b_, g, c: (0, g))],
        out_specs=pl.BlockSpec((1, chunk, gw), lambda b_, g, c: (b_, c, g)),
        out_shape=jax.ShapeDtypeStruct(z.shape, BF16),
        scratch_shapes=[pltpu.VMEM((gw // LANES, n_state, LANES), F32), pltpu.VMEM((chunk, gw), F32)],
        compiler_params=_params("parallel", "parallel", "arbitrary"),
        name="mamba_ssd",
    )(xbc, xbc, xbc, z, dt_g, dt_gt,
      dt_bias.reshape(groups, 1, heads), dt_bias.reshape(groups, heads, 1),
      a_log.reshape(groups, 1, heads), a_log.reshape(groups, heads, 1),
      jnp.repeat(d_skip, p_dim).reshape(1, d_inner), norm_w.reshape(1, d_inner))


def mamba2_mixer(u, h, w, bsz, s_len):
    d_inner = w["mb_w_out"].shape[0]
    n_heads = w["mb_dt_bias"].shape[0]
    w_in = w["mb_w_in"]
    xbc_w = w_in.shape[1] - d_inner - n_heads
    z = matmul_ws(u, [(w_in, 0)], d_inner, name="mb_in_z")
    xbc = matmul_ws(u, [(w_in, d_inner)], xbc_w, name="mb_in_xbc")
    dt = matmul_ws(u, [(w_in, d_inner + xbc_w)], n_heads, name="mb_in_dt")
    xbc = conv_silu(xbc.reshape(bsz, s_len, xbc_w), w["mb_conv_w"], w["mb_conv_b"])
    y = ssd_scan(xbc, z.reshape(bsz, s_len, d_inner), dt.reshape(bsz, s_len, n_heads),
                 w["mb_dt_bias"], w["mb_a_log"], w["mb_d_skip"], w["mb_norm_w"])
    return matmul_ws(y.reshape(bsz * s_len, d_inner), [(w["mb_w_out"], 0)], h.shape[1],
                     epilogue=_ep_residual, extras=[(h, "mn")], bm=512, w_buffers=1, name="mb_out")


def _seg_cumsum_rows(x, seg, reverse=False):
    n = x.shape[0]
    pos = lax.broadcasted_iota(jnp.int32, x.shape, 0) % seg
    s = 1
    while s < seg:
        if reverse:
            x = x + jnp.where(pos < seg - s, pltpu.roll(x, n - s, 0), 0.0)
        else:
            x = x + jnp.where(pos >= s, pltpu.roll(x, s, 0), 0.0)
        s *= 2
    return x


def _hgrn_kernel(q_ref, f_ref, i_ref, g_ref, lb_ref, nw_ref, o_ref, state_ref, *, sub, n_sub, heads, dk):
    @pl.when(pl.program_id(2) == 0)
    def _():
        state_ref[...] = jnp.zeros_like(state_ref)

    lb = lb_ref[...]
    nw = nw_ref[...]
    ti = lax.broadcasted_iota(jnp.int32, (sub, sub), 0)
    si = lax.broadcasted_iota(jnp.int32, (sub, sub), 1)
    causal = ti >= si
    f = lb + (1.0 - lb) * jax.nn.sigmoid(f_ref[0])
    lf = jnp.log(f)
    k = 1.0 - f
    b = _seg_cumsum_rows(lf, sub)
    to_end = _seg_cumsum_rows(lf, sub, reverse=True) - lf
    q_dec = (_silu(q_ref[0]) * jnp.exp(b)).astype(BF16)
    k_dec = (k * jnp.exp(-b)).astype(BF16)
    k_end = (k * jnp.exp(to_end)).astype(BF16)
    v = i_ref[0].astype(BF16)
    cs = range(n_sub)
    hs = range(heads)
    blk = lambda x, c, h: x[c * sub:(c + 1) * sub, h * dk:(h + 1) * dk]
    scores = [[jnp.where(causal, _dot_nt(blk(q_dec, c, h), blk(k_dec, c, h)), 0.0).astype(BF16) for h in hs]
              for c in cs]
    upd = [[_dot_tn(blk(v, c, h), blk(k_end, c, h)) for h in hs] for c in cs]
    states = []
    st = [state_ref[h] for h in hs]
    for c in cs:
        states.append(st)
        decay = jnp.exp(b[(c + 1) * sub - 1:(c + 1) * sub, :])
        st = [st[h] * decay[:, h * dk:(h + 1) * dk] + upd[c][h] for h in hs]
    for h in hs:
        state_ref[h] = st[h]
    for c in cs:
        rows = slice(c * sub, (c + 1) * sub)
        for h in hs:
            o = (jnp.dot(scores[c][h], blk(v, c, h), preferred_element_type=F32)
                 + _dot_nt(blk(q_dec, c, h), states[c][h].astype(BF16)))
            o = o * lax.rsqrt(jnp.mean(o * o, axis=-1, keepdims=True) + NORM_EPS) * nw
            cols = slice(h * dk, (h + 1) * dk)
            o_ref[0, rows, cols] = (o * _silu(g_ref[0, rows, cols])).astype(o_ref.dtype)


def hgrn2_scan(proj, lower_bound, norm_w, *, dk=HG_HEAD_DIM, sub=HG_CHUNK, tb=256, heads=4):
    bsz, s_len, d4 = proj.shape
    d = d4 // 4
    n_heads = d // dk
    tb = min(tb, s_len)
    heads = min(heads, n_heads)
    hw = heads * dk
    n_hb = n_heads // heads
    kern = functools.partial(_hgrn_kernel, sub=sub, n_sub=tb // sub, heads=heads, dk=dk)
    spec = lambda part: pl.BlockSpec((1, tb, hw), lambda b_, h_, t: (b_, t, part * n_hb + h_))
    return pl.pallas_call(
        kern,
        grid=(bsz, n_hb, s_len // tb),
        in_specs=[spec(0), spec(1), spec(2), spec(3),
                  pl.BlockSpec((1, hw), lambda b_, h_, t: (0, h_)),
                  pl.BlockSpec((1, dk), lambda b_, h_, t: (0, 0))],
        out_specs=pl.BlockSpec((1, tb, hw), lambda b_, h_, t: (b_, t, h_)),
        out_shape=jax.ShapeDtypeStruct((bsz, s_len, d), BF16),
        scratch_shapes=[pltpu.VMEM((heads, dk, dk), F32)],
        compiler_params=_params("parallel", "parallel", "arbitrary"),
        name="hgrn2_scan",
    )(proj, proj, proj, proj, lower_bound.reshape(1, d), norm_w.reshape(1, dk))


def hgrn2_mixer(u, h, w, lower_bound, bsz, s_len):
    d = h.shape[1]
    proj = matmul_ws(u, [(w["hg_w_in"], 0)], 4 * d, name="hg_in")
    o = hgrn2_scan(proj.reshape(bsz, s_len, 4 * d), lower_bound, w["hg_norm_w"])
    return matmul_ws(o.reshape(bsz * s_len, d), [(w["hg_w_out"], 0)], d,
                     epilogue=_ep_residual, extras=[(h, "mn")], name="hg_out")


def dense_ffn(v, h, w_in, w_out):
    f = w_out.shape[0]
    hid = matmul_ws(v, [(w_in, 0), (w_in, f)], f, epilogue=_ep_swiglu, out_dtype=BF16, bm=512, name="ffn_in")
    return matmul_ws(hid, [(w_out, 0)], h.shape[1], epilogue=_ep_residual, extras=[(h, "mn")], bm=512, w_buffers=1,
                     name="ffn_out")


def _router_kernel(x_ref, r_ref, o_ref, *, n_experts):
    logits = jnp.dot(x_ref[...], r_ref[...], preferred_element_type=F32)
    lane = lax.broadcasted_iota(jnp.int32, logits.shape, 1)
    logits = jnp.where(lane < n_experts, logits, -jnp.inf)
    m1 = jnp.max(logits, axis=-1, keepdims=True)
    i1 = jnp.min(jnp.where(logits == m1, lane, LANES), axis=-1, keepdims=True)
    rest = jnp.where(lane == i1, -jnp.inf, logits)
    m2 = jnp.max(rest, axis=-1, keepdims=True)
    i2 = jnp.min(jnp.where(rest == m2, lane, LANES), axis=-1, keepdims=True)
    e2 = jnp.exp(m2 - m1)
    w1 = 1.0 / (1.0 + e2)
    o_ref[...] = jnp.where(lane == i1, w1, 0.0) + jnp.where(lane == i2, e2 * w1, 0.0)


def moe_router(v, router):
    m, d = v.shape
    n_experts = router.shape[1]
    r_pad = jnp.zeros((d, LANES), BF16).at[:, :n_experts].set(router.astype(BF16))
    bm = _pick(m, 512)
    return pl.pallas_call(
        functools.partial(_router_kernel, n_experts=n_experts),
        grid=(m // bm,),
        in_specs=[pl.BlockSpec((bm, d), lambda i: (i, 0)), pl.BlockSpec((d, LANES), lambda i: (0, 0))],
        out_specs=pl.BlockSpec((bm, LANES), lambda i: (i, 0)),
        out_shape=jax.ShapeDtypeStruct((m, LANES), F32),
        compiler_params=_params("parallel"),
        name="moe_router",
    )(v, r_pad)


def _moe_in_kernel(x_ref, wg_ref, wu_ref, c_ref, o_ref, wgb_ref, wub_ref, *, blocks_per_expert):
    @pl.when(pl.program_id(1) == 0)
    def _():
        wgb_ref[...] = wg_ref[0].astype(BF16)
        wub_ref[...] = wu_ref[0].astype(BF16)

    x = x_ref[...]
    g = jnp.dot(x, wgb_ref[...], preferred_element_type=F32)
    u = jnp.dot(x, wub_ref[...], preferred_element_type=F32)
    e = pl.program_id(0) // blocks_per_expert
    comb = c_ref[...]
    lane = lax.broadcasted_iota(jnp.int32, comb.shape, 1)
    scale = jnp.sum(jnp.where(lane == e, comb, 0.0), axis=-1, keepdims=True)
    o_ref[...] = (_silu(g) * u * scale).astype(o_ref.dtype)


def moe_ffn(v, h, router, w_in, w_out, *, bm=512, bn=512):
    m, d = v.shape
    n_experts, _, two_de = w_in.shape
    de = two_de // 2
    bm = _pick(m, bm)
    bn = _pick(de, bn)
    bpe = de // bn
    comb = moe_router(v, router)
    hid = pl.pallas_call(
        functools.partial(_moe_in_kernel, blocks_per_expert=bpe),
        grid=(n_experts * bpe, m // bm),
        in_specs=[pl.BlockSpec((bm, d), lambda j, i: (i, 0)),
                  pl.BlockSpec((1, d, bn), lambda j, i: (j // bpe, 0, j % bpe)),
                  pl.BlockSpec((1, d, bn), lambda j, i: (j // bpe, 0, j % bpe + bpe)),
                  pl.BlockSpec((bm, LANES), lambda j, i: (i, 0))],
        out_specs=pl.BlockSpec((bm, bn), lambda j, i: (i, j)),
        out_shape=jax.ShapeDtypeStruct((m, n_experts * de), BF16),
        scratch_shapes=[pltpu.VMEM((d, bn), BF16), pltpu.VMEM((d, bn), BF16)],
        compiler_params=_params("parallel", "arbitrary"),
        name="moe_in",
    )(v, w_in, w_in, comb)
    return matmul(hid, [(w_out.reshape(n_experts * de, d), 0)], d, epilogue=_ep_residual, extras=[(h, "mn")],
                  name="moe_out")


def ple_gate(h, p_i, norm_pl, pl_proj, pl_gate, layer):
    d = h.shape[1]
    n = rmsnorm(h, norm_pl, name="rmsnorm_ple")
    return matmul_ws(n, [(pl_gate, (layer, 0))], d, epilogue=_ep_ple_gate,
                     extras=[(h, "mn"), (p_i, "m"), (pl_proj, "kn")], name="ple_gate")


def _rw_mix_kernel(u_ref, mu_ref, *o_refs):
    u = u_ref[0]
    row = lax.broadcasted_iota(jnp.int32, u.shape, 0)
    dx = jnp.where(row >= 1, pltpu.roll(u, 1, 0), 0.0) - u
    for j, o_ref in enumerate(o_refs):
        o_ref[0] = (u + dx * mu_ref[j:j + 1, :]).astype(o_ref.dtype)


def rw_token_mix(u, mu):
    bsz, s_len, d = u.shape
    cb = _pick(d, LANES)
    n_mix = mu.shape[0]
    spec = pl.BlockSpec((1, s_len, cb), lambda b_, j: (b_, 0, j))
    return pl.pallas_call(
        _rw_mix_kernel,
        grid=(bsz, d // cb),
        in_specs=[spec, pl.BlockSpec((n_mix, cb), lambda b_, j: (0, j))],
        out_specs=[spec] * n_mix,
        out_shape=[jax.ShapeDtypeStruct(u.shape, BF16)] * n_mix,
        compiler_params=_params("parallel", "parallel"),
        name="rwkv_token_mix",
    )(u, mu)


def _dot_hi(a, b):
    return jnp.dot(a, b, preferred_element_type=F32, precision=lax.Precision.HIGHEST)


def _rw_scan_kernel(r_ref, k_ref, v_ref, a_ref, lw_ref, g_ref, kk_ref, ka_ref, rk_ref, lnw_ref, lnb_ref,
                    o_ref, state_ref, *, chunk, heads, n):
    @pl.when(pl.program_id(2) == 0)
    def _():
        state_ref[...] = jnp.zeros_like(state_ref)

    hs = range(heads)
    sls = [slice(j * n, (j + 1) * n) for j in hs]
    ti = lax.broadcasted_iota(jnp.int32, (chunk, chunk), 0)
    si = lax.broadcasted_iota(jnp.int32, (chunk, chunk), 1)
    strict = ti > si
    incl = ti >= si
    dot = functools.partial(jnp.dot, preferred_element_type=F32)

    r = [r_ref[0, :, sl] for sl in sls]
    v = [v_ref[0, :, sl] for sl in sls]
    a = [a_ref[0, :, sl] for sl in sls]
    lw = [lw_ref[0, :, sl] for sl in sls]
    k = [k_ref[0, :, sl] for sl in sls]
    kk = [k[j] * kk_ref[:, sls[j]] for j in hs]
    kk = [kk[j] / jnp.maximum(jnp.sqrt(jnp.sum(kk[j] * kk[j], axis=-1, keepdims=True)), 1e-12) for j in hs]
    kmod = [k[j] * (1.0 + (a[j] - 1.0) * ka_ref[:, sls[j]]) for j in hs]
    kka = [kk[j] * a[j] for j in hs]
    cum = [_cumsum_rows(lw[j], chunk) for j in hs]
    cum_end = [c[chunk - 1:chunk, :] for c in cum]
    mid = [c[chunk // 2 - 1:chunk // 2, :] for c in cum]
    e_neg = [jnp.exp(mid[j] - cum[j]) for j in hs]
    am = [(kk[j] * jnp.exp(cum[j] - lw[j] - mid[j])).astype(BF16) for j in hs]
    bm = [(kka[j] * e_neg[j]).astype(BF16) for j in hs]
    km = [(kmod[j] * e_neg[j]).astype(BF16) for j in hs]
    rm = [(r[j] * jnp.exp(cum[j] - mid[j])).astype(BF16) for j in hs]
    a_abs = [(kk[j] * jnp.exp(cum[j] - lw[j])).astype(BF16) for j in hs]
    r_abs = [(r[j] * jnp.exp(cum[j])).astype(BF16) for j in hs]
    vb = [x.astype(BF16) for x in v]
    st = [state_ref[j] for j in hs]
    stb = [x.astype(BF16) for x in st]

    nb = [(-jnp.where(strict, _dot_nt(am[j], bm[j]), 0.0)).astype(BF16) for j in hs]
    lk = [jnp.where(strict, _dot_nt(am[j], km[j]), 0.0).astype(BF16) for j in hs]
    x = [_dot_nt(a_abs[j], stb[j]) + dot(lk[j], vb[j]) for j in hs]
    x = [x[j] + dot(nb[j], x[j].astype(BF16)) for j in hs]
    p = 2
    while p < chunk:
        nb = [dot(nb[j], nb[j]).astype(BF16) for j in hs]
        x = [x[j] + dot(nb[j], x[j].astype(BF16)) for j in hs]
        p *= 2
    pb = [xj.astype(BF16) for xj in x]
    mk = [jnp.where(incl, _dot_nt(rm[j], km[j]), 0.0).astype(BF16) for j in hs]
    mb = [jnp.where(incl, _dot_nt(rm[j], bm[j]), 0.0).astype(BF16) for j in hs]
    y = [_dot_nt(r_abs[j], stb[j]) + dot(mk[j], vb[j]) - dot(mb[j], pb[j]) for j in hs]
    to_end = [jnp.exp(cum_end[j] - cum[j]) for j in hs]
    for j in hs:
        state_ref[j] = (st[j] * jnp.exp(cum_end[j]) + _dot_tn(vb[j], (kmod[j] * to_end[j]).astype(BF16))
                        - _dot_tn(pb[j], (kka[j] * to_end[j]).astype(BF16)))
    for j in hs:
        sl = sls[j]
        bonus = jnp.sum(r[j] * kmod[j] * rk_ref[:, sl], axis=-1, keepdims=True) * v[j]
        mean = jnp.mean(y[j], axis=-1, keepdims=True)
        yc = y[j] - mean
        var = jnp.mean(yc * yc, axis=-1, keepdims=True)
        yn = yc * lax.rsqrt(var + RW_LN_EPS) * lnw_ref[:, sl] + lnb_ref[:, sl]
        o_ref[0, :, sl] = ((yn + bonus) * g_ref[0, :, sl]).astype(o_ref.dtype)


def _rw_scan_tile_kernel(r_ref, k_ref, v_ref, a_ref, lw_ref, g_ref, kk_ref, ka_ref, rk_ref, lnw_ref, lnb_ref,
                         o_ref, state_ref, *, chunk, heads, n):
    @pl.when(pl.program_id(2) == 0)
    def _():
        state_ref[...] = jnp.zeros_like(state_ref)

    per = LANES // n
    tiles = range(heads // per)
    sub = range(per)
    ti = lax.broadcasted_iota(jnp.int32, (chunk, chunk), 0)
    si = lax.broadcasted_iota(jnp.int32, (chunk, chunk), 1)
    strict = ti > si
    incl = ti >= si
    lane_seg = lax.broadcasted_iota(jnp.int32, (1, LANES), 1) // n
    seg_is = [lane_seg == j for j in sub]
    same_head = (lax.broadcasted_iota(jnp.int32, (LANES, LANES), 0) // n
                 == lax.broadcasted_iota(jnp.int32, (LANES, LANES), 1) // n)
    dot = functools.partial(jnp.dot, preferred_element_type=F32)
    tile = lambda x, i: x[:, i * LANES:(i + 1) * LANES]

    def pick(vals):
        out = vals[-1]
        for j in range(per - 2, -1, -1):
            out = jnp.where(seg_is[j], vals[j], out)
        return out

    def seg_sum(x):
        return pick([jnp.sum(jnp.where(seg_is[j], x, 0.0), axis=-1, keepdims=True) for j in sub])

    r, k, v, a, lw = r_ref[0], k_ref[0], v_ref[0], a_ref[0], lw_ref[0]
    kk = k * kk_ref[...]
    kmod = k * (1.0 + (a - 1.0) * ka_ref[...])
    cum = _cumsum_rows(lw, chunk)
    cum_end = cum[chunk - 1:chunk, :]
    mid = cum[chunk // 2 - 1:chunk // 2, :]
    bonus_in = r * kmod * rk_ref[...]
    kk_t, bonus_t = [], []
    for i in tiles:
        kki = tile(kk, i)
        kk_t.append(kki / jnp.maximum(jnp.sqrt(seg_sum(kki * kki)), 1e-12))
        bonus_t.append(seg_sum(tile(bonus_in, i)) * tile(v, i))
    kk = jnp.concatenate(kk_t, axis=-1) if len(kk_t) > 1 else kk_t[0]
    kka = kk * a
    e_neg = jnp.exp(mid - cum)
    to_end = jnp.exp(cum_end - cum)
    am = (kk * jnp.exp(cum - lw - mid)).astype(BF16)
    bm = (kka * e_neg).astype(BF16)
    km = (kmod * e_neg).astype(BF16)
    rm = (r * jnp.exp(cum - mid)).astype(BF16)
    a_abs = (kk * jnp.exp(cum - lw)).astype(BF16)
    r_abs = (r * jnp.exp(cum)).astype(BF16)
    k_end = (kmod * to_end).astype(BF16)
    b_end = (kka * to_end).astype(BF16)
    vb = v.astype(BF16)
    st_decay = jnp.exp(cum_end)
    zero = jnp.zeros((), BF16)

    st = [state_ref[i] for i in tiles]
    stb = [s.astype(BF16) for s in st]
    am_h = [[jnp.where(seg_is[j], tile(am, i), zero) for j in sub] for i in tiles]
    rm_h = [[jnp.where(seg_is[j], tile(rm, i), zero) for j in sub] for i in tiles]
    nb = [[(-jnp.where(strict, _dot_nt(am_h[i][j], tile(bm, i)), 0.0)).astype(BF16) for j in sub] for i in tiles]
    lk = [[jnp.where(strict, _dot_nt(am_h[i][j], tile(km, i)), 0.0).astype(BF16) for j in sub] for i in tiles]
    x = [_dot_nt(tile(a_abs, i), stb[i]) + pick([dot(lk[i][j], tile(vb, i)) for j in sub]) for i in tiles]
    xb = [xi.astype(BF16) for xi in x]
    x = [x[i] + pick([dot(nb[i][j], xb[i]) for j in sub]) for i in tiles]
    p = 2
    while p < chunk:
        nb = [[dot(nb[i][j], nb[i][j]).astype(BF16) for j in sub] for i in tiles]
        xb = [xi.astype(BF16) for xi in x]
        x = [x[i] + pick([dot(nb[i][j], xb[i]) for j in sub]) for i in tiles]
        p *= 2
    pb = [xi.astype(BF16) for xi in x]
    mk = [[jnp.where(incl, _dot_nt(rm_h[i][j], tile(km, i)), 0.0).astype(BF16) for j in sub] for i in tiles]
    mb = [[jnp.where(incl, _dot_nt(rm_h[i][j], tile(bm, i)), 0.0).astype(BF16) for j in sub] for i in tiles]
    y = [_dot_nt(tile(r_abs, i), stb[i])
         + pick([dot(mk[i][j], tile(vb, i)) - dot(mb[i][j], pb[i]) for j in sub]) for i in tiles]
    for i in tiles:
        upd = _dot_tn(tile(vb, i), tile(k_end, i)) - _dot_tn(pb[i], tile(b_end, i))
        state_ref[i] = st[i] * tile(st_decay, i) + jnp.where(same_head, upd, 0.0)
    inv_n = 1.0 / n
    for i in tiles:
        cols = slice(i * LANES, (i + 1) * LANES)
        mean = seg_sum(y[i]) * inv_n
        yc = y[i] - mean
        var = seg_sum(yc * yc) * inv_n
        yn = yc * lax.rsqrt(var + RW_LN_EPS) * lnw_ref[:, cols] + lnb_ref[:, cols]
        o_ref[0, :, cols] = ((yn + bonus_t[i]) * g_ref[0, :, cols]).astype(o_ref.dtype)


def rw_scan(r, k, v, a, lw, g, k_k, k_a, r_k, ln_w, ln_b, *, n=RW_HEAD_DIM, chunk=RW_CHUNK, heads=8):
    bsz, s_len, d = r.shape
    chunk = min(chunk, s_len)
    heads = min(heads, d // n)
    hw = heads * n
    seq = pl.BlockSpec((1, chunk, hw), lambda b_, h_, c: (b_, c, h_))
    par = pl.BlockSpec((1, hw), lambda b_, h_, c: (0, h_))
    row = lambda t: t.reshape(1, d)
    assert hw % LANES == 0 and LANES % n == 0
    kern = functools.partial(_rw_scan_tile_kernel, chunk=chunk, heads=heads, n=n)
    return pl.pallas_call(
        kern,
        grid=(bsz, d // hw, s_len // chunk),
        in_specs=[seq] * 6 + [par] * 5,
        out_specs=seq,
        out_shape=jax.ShapeDtypeStruct(r.shape, BF16),
        scratch_shapes=[pltpu.VMEM((hw // LANES, LANES, LANES), F32)],
        compiler_params=_params("parallel", "parallel", "arbitrary"),
        name="rwkv7_scan",
    )(r, k, v, a, lw, g, row(k_k), row(k_a), row(r_k), row(ln_w), row(ln_b))


def rwkv7_mixer(u, h, w, bsz, s_len):
    t, d = u.shape
    xr, xw, xk, xv, xa, xg = [x.reshape(t, d) for x in rw_token_mix(u.reshape(bsz, s_len, d), w["rw_mu"])]
    r = matmul_ws(xr, [(w["rw_w_rkv"], (0, 0))], d, name="rw_r")
    k = matmul_ws(xk, [(w["rw_w_rkv"], (1, 0))], d, name="rw_k")
    v = matmul_ws(xv, [(w["rw_w_rkv"], (2, 0))], d, name="rw_v")
    row = lambda x: x.reshape(1, d)
    w_lo = matmul(xw, [(w["rw_w1"], 0)], w["rw_w1"].shape[1], epilogue=_ep_tanh, out_dtype=BF16, name="rw_w1")
    lw = matmul(w_lo, [(w["rw_w2"], 0)], d, epilogue=_ep_rw_logdecay, extras=[(row(w["rw_w0"]), "n")], name="rw_w2")
    a_lo = matmul(xa, [(w["rw_a1"], 0)], w["rw_a1"].shape[1], out_dtype=BF16, name="rw_a1")
    a = matmul(a_lo, [(w["rw_a2"], 0)], d, epilogue=_ep_bias_sigmoid, extras=[(row(w["rw_a0"]), "n")], name="rw_a2")
    g_lo = matmul(xg, [(w["rw_g1"], 0)], w["rw_g1"].shape[1], epilogue=_ep_sigmoid, out_dtype=BF16, name="rw_g1")
    g = matmul(g_lo, [(w["rw_g2"], 0)], d, name="rw_g2")
    shp = (bsz, s_len, d)
    y = rw_scan(r.reshape(shp), k.reshape(shp), v.reshape(shp), a.reshape(shp), lw.reshape(shp), g.reshape(shp),
                w["rw_k_k"], w["rw_k_a"], w["rw_r_k"], w["rw_ln_w"], w["rw_ln_b"])
    return matmul_ws(y.reshape(t, d), [(w["rw_w_out"], 0)], d, epilogue=_ep_residual, extras=[(h, "mn")],
                     name="rw_out")


NEG_BIG = -1e30


def _rope_kernel(x_ref, cc_ref, ss_ref, o_ref, *, n_q_slots, scale):
    x = x_ref[0]
    out = x * cc_ref[...] + pltpu.roll(x, x.shape[-1] // 2, 1) * ss_ref[...]
    out = out * jnp.where(pl.program_id(2) < n_q_slots, scale, 1.0)
    o_ref[0] = out.astype(o_ref.dtype)


def _rope_tables(pos, dim):
    inv = ROPE_THETA ** (-(jnp.arange(0, dim, 2, dtype=F32) / dim))
    ang = pos.astype(F32)[:, None] * inv[None, :]
    cos, sin = jnp.cos(ang), jnp.sin(ang)
    return jnp.concatenate([cos, cos], axis=-1), jnp.concatenate([-sin, sin], axis=-1)


def nsa_rope(proj, n_q_slots, k_slots, dh, scale, tb=512):
    bsz, s_len, _ = proj.shape
    tb = min(tb, s_len)
    cc, ss = _rope_tables(jnp.arange(s_len), dh)
    n_out = n_q_slots + len(k_slots)

    def in_slot(j):
        slot = j
        for idx, ks in enumerate(k_slots):
            slot = jnp.where(j == n_q_slots + idx, ks, slot)
        return slot

    return pl.pallas_call(
        functools.partial(_rope_kernel, n_q_slots=n_q_slots, scale=scale),
        grid=(bsz, s_len // tb, n_out),
        in_specs=[pl.BlockSpec((1, tb, dh), lambda b_, t, j: (b_, t, in_slot(j))),
                  pl.BlockSpec((tb, dh), lambda b_, t, j: (t, 0)),
                  pl.BlockSpec((tb, dh), lambda b_, t, j: (t, 0))],
        out_specs=pl.BlockSpec((1, tb, dh), lambda b_, t, j: (b_, t, j)),
        out_shape=jax.ShapeDtypeStruct((bsz, s_len, n_out * dh), BF16),
        compiler_params=_params("parallel", "parallel", "arbitrary"),
        name="nsa_rope",
    )(proj, cc, ss)


def _cmp_finish_kernel(z_ref, bias_ref, w2_ref, cc_ref, ss_ref, o_ref, *, hidden, rope):
    z = z_ref[0]
    nc = z.shape[0]
    nxt = pltpu.roll(z[:, hidden:], nc - 1, 0)
    hid = _silu(z[:, :hidden] + nxt + bias_ref[...])
    out = jnp.dot(hid.astype(BF16), w2_ref[...], preferred_element_type=F32)
    if rope:
        out = out * cc_ref[...] + pltpu.roll(out, out.shape[-1] // 2, 1) * ss_ref[...]
    o_ref[0] = out.astype(o_ref.dtype)


def nsa_compress(x, pos_emb, w1, w2, bsz, s_len, groups, dh, rope, transpose_out=False):
    stride, blk = NSA_CMP_STRIDE, NSA_CMP_BLOCK
    nc = s_len // stride
    hidden = w1.shape[-1]
    half = stride * dh
    x16 = jnp.transpose(x.reshape(bsz, nc, stride, groups, dh), (0, 3, 1, 2, 4)).reshape(bsz * groups * nc, half)
    w1f = w1.reshape(blk * dh, hidden)
    wcat = jnp.concatenate([w1f[:half], w1f[half:]], axis=1).astype(BF16)
    z = matmul(x16.astype(BF16), [(wcat, 0)], 2 * hidden, name="nsa_cmp_w1")
    bias = matmul(pos_emb.reshape(1, blk * dh).astype(BF16), [(w1f.astype(BF16), 0)], hidden, name="nsa_cmp_pos")
    cc, ss = _rope_tables(jnp.arange(nc) * stride + blk - 1, dh)
    if transpose_out:
        assert not rope
        return pl.pallas_call(
            functools.partial(_cmp_finish_t_kernel, hidden=hidden),
            grid=(bsz * groups,),
            in_specs=[pl.BlockSpec((1, nc, 2 * hidden), lambda i: (i, 0, 0)),
                      pl.BlockSpec((1, hidden), lambda i: (0, 0)),
                      pl.BlockSpec((dh, hidden), lambda i: (0, 0))],
            out_specs=pl.BlockSpec((1, dh, nc), lambda i: (i, 0, 0)),
            out_shape=jax.ShapeDtypeStruct((bsz * groups, dh, nc), BF16),
            compiler_params=_params("parallel"),
            name="nsa_cmp_finish_t",
        )(z.reshape(bsz * groups, nc, 2 * hidden), bias, w2.T.astype(BF16))
    return pl.pallas_call(
        functools.partial(_cmp_finish_kernel, hidden=hidden, rope=rope),
        grid=(bsz * groups,),
        in_specs=[pl.BlockSpec((1, nc, 2 * hidden), lambda i: (i, 0, 0)),
                  pl.BlockSpec((1, hidden), lambda i: (0, 0)),
                  pl.BlockSpec((hidden, dh), lambda i: (0, 0)),
                  pl.BlockSpec((nc, dh), lambda i: (0, 0)),
                  pl.BlockSpec((nc, dh), lambda i: (0, 0))],
        out_specs=pl.BlockSpec((1, nc, dh), lambda i: (i, 0, 0)),
        out_shape=jax.ShapeDtypeStruct((bsz * groups, nc, dh), BF16),
        compiler_params=_params("parallel"),
        name="nsa_cmp_finish",
    )(z.reshape(bsz * groups, nc, 2 * hidden), bias, w2.astype(BF16), cc, ss)


def _nsa_cmp_select_kernel(q_ref, kc_ref, vc_ref, ov_ref, oc_ref, sel_ref, *, tq, rep, dh, topn):
    qi = pl.program_id(2)
    kc = kc_ref[0]
    vc = vc_ref[0]
    nc = kc.shape[0]
    n_sel = sel_ref.shape[-1]
    t = qi * tq + lax.broadcasted_iota(jnp.int32, (tq, nc), 0)
    cmp_end = lax.broadcasted_iota(jnp.int32, (tq, nc), 1) * NSA_CMP_STRIDE + (NSA_CMP_BLOCK - 1)
    visible = cmp_end <= t
    psum = jnp.zeros((tq, nc), F32)
    for r in range(rep):
        s = jnp.where(visible, _dot_nt(q_ref[0, :, r * dh:(r + 1) * dh], kc), NEG_BIG)
        m = jnp.max(s, axis=-1, keepdims=True)
        e = jnp.where(visible, jnp.exp(s - m), 0.0)
        den = jnp.sum(e, axis=-1, keepdims=True)
        p = e / jnp.where(den > 0, den, 1.0)
        oc_ref[0, :, r * dh:(r + 1) * dh] = jnp.dot(p.astype(BF16), vc, preferred_element_type=F32)
        psum = psum + p
    imp = _dot_hi(psum, ov_ref[...])
    blk = lax.broadcasted_iota(jnp.int32, (tq, n_sel), 1)
    cur = (qi * tq + lax.broadcasted_iota(jnp.int32, (tq, n_sel), 0)) // NSA_SEL_BLOCK
    forced = (blk == 0) | (blk == cur) | (blk == cur - 1)
    imp = jnp.where(forced, NSA_FORCED_SCORE, imp)
    imp = jnp.where(blk > cur, -jnp.inf, imp)
    sel = jnp.zeros((tq, n_sel), F32)
    for _ in range(topn):
        m = jnp.max(imp, axis=-1, keepdims=True)
        first = jnp.min(jnp.where(imp == m, blk, n_sel), axis=-1, keepdims=True)
        hit = blk == first
        sel = jnp.where(hit, 1.0, sel)
        imp = jnp.where(hit, -jnp.inf, imp)
    sel_ref[0, 0] = sel


def _flash_step(q_scr, k, v, mask, m_ref, l_ref, acc_ref, rep, tq):
    kb = k.shape[0]
    s = _dot_nt(q_scr[...], k).reshape(rep, tq, kb)
    s = jnp.where(mask[None], s, NEG_BIG)
    m_old = m_ref[...].reshape(rep, tq, -1)[:, :, :1]
    m_new = jnp.maximum(m_old, jnp.max(s, axis=-1, keepdims=True))
    p = jnp.where(mask[None], jnp.exp(s - m_new), 0.0)
    alpha = jnp.exp(m_old - m_new)
    l_old = l_ref[...].reshape(rep, tq, -1)[:, :, :1]
    l_new = alpha * l_old + jnp.sum(p, axis=-1, keepdims=True)
    pv = jnp.dot(p.reshape(rep * tq, kb).astype(BF16), v, preferred_element_type=F32)
    acc_ref[...] = (alpha * acc_ref[...].reshape(rep, tq, -1)).reshape(rep * tq, -1) + pv
    m_ref[...] = jnp.broadcast_to(m_new, (rep, tq, m_ref.shape[-1])).reshape(m_ref.shape)
    l_ref[...] = jnp.broadcast_to(l_new, (rep, tq, l_ref.shape[-1])).reshape(l_ref.shape)


def _flash_init(q_ref, q_scr, m_ref, l_ref, acc_ref, rep, tq, dh):
    for r in range(rep):
        q_scr[r * tq:(r + 1) * tq, :] = q_ref[0, :, r * dh:(r + 1) * dh]
    m_ref[...] = jnp.full_like(m_ref, NEG_BIG)
    l_ref[...] = jnp.zeros_like(l_ref)
    acc_ref[...] = jnp.zeros_like(acc_ref)


def _flash_result(l_ref, acc_ref):
    l = l_ref[...][:, :1]
    return acc_ref[...] / jnp.where(l > 0, l, 1.0)


def _nsa_select_kernel(q_ref, k_ref, v_ref, sel_ref, o_ref, q_scr, m_ref, l_ref, acc_ref, *, tq, kb, rep, dh):
    qi = pl.program_id(2)
    kj = pl.program_id(3)

    @pl.when(kj == 0)
    def _():
        _flash_init(q_ref, q_scr, m_ref, l_ref, acc_ref, rep, tq, dh)

    @pl.when(kj * kb <= qi * tq + tq - 1)
    def _():
        sel = sel_ref[0, 0]
        blk = lax.broadcasted_iota(jnp.int32, sel.shape, 1)
        kpos = kj * kb + lax.broadcasted_iota(jnp.int32, (tq, kb), 1)
        t = qi * tq + lax.broadcasted_iota(jnp.int32, (tq, kb), 0)
        chosen = jnp.zeros((tq, kb), F32)
        for i in range(kb // NSA_SEL_BLOCK):
            col = jnp.sum(jnp.where(blk == kj * (kb // NSA_SEL_BLOCK) + i, sel, 0.0), axis=-1, keepdims=True)
            in_blk = (kpos - kj * kb) // NSA_SEL_BLOCK == i
            chosen = jnp.where(in_blk, col, chosen)
        mask = (chosen > 0) & (kpos <= t)
        _flash_step(q_scr, k_ref[0], v_ref[0].astype(BF16), mask, m_ref, l_ref, acc_ref, rep, tq)

    @pl.when(kj == pl.num_programs(3) - 1)
    def _():
        out = _flash_result(l_ref, acc_ref)
        for r in range(rep):
            o_ref[0, :, r * dh:(r + 1) * dh] = out[r * tq:(r + 1) * tq, :]


def _nsa_window_kernel(q_ref, k_ref, v_ref, oc_ref, os_ref, g_ref, o_ref, q_scr, m_ref, l_ref, acc_ref,
                       *, tq, kb, rep, dh, window, n_steps):
    qi = pl.program_id(2)
    w = pl.program_id(3)
    kblk = qi * (tq // kb) - (n_steps - tq // kb) + w

    @pl.when(w == 0)
    def _():
        _flash_init(q_ref, q_scr, m_ref, l_ref, acc_ref, rep, tq, dh)

    @pl.when(kblk >= 0)
    def _():
        kpos = kblk * kb + lax.broadcasted_iota(jnp.int32, (tq, kb), 1)
        t = qi * tq + lax.broadcasted_iota(jnp.int32, (tq, kb), 0)
        mask = (kpos <= t) & (kpos > t - window)
        _flash_step(q_scr, k_ref[0], v_ref[0].astype(BF16), mask, m_ref, l_ref, acc_ref, rep, tq)

    @pl.when(w == n_steps - 1)
    def _():
        out = _flash_result(l_ref, acc_ref)
        gates = g_ref[0, 0]
        for r in range(rep):
            sl = slice(r * dh, (r + 1) * dh)
            o = (gates[:, 3 * r:3 * r + 1] * oc_ref[0, :, sl] + gates[:, 3 * r + 1:3 * r + 2] * os_ref[0, :, sl]
                 + gates[:, 3 * r + 2:3 * r + 3] * out[r * tq:(r + 1) * tq, :])
            o_ref[0, :, sl] = o.astype(o_ref.dtype)


def nsa_mixer(u, h, w, bsz, s_len):
    t, d = u.shape
    dh, groups = NSA_HEAD_DIM, NSA_N_KV
    n_heads = d // dh
    rep = n_heads // groups
    kvw = groups * dh
    qw = n_heads * dh
    main_w = qw + 6 * kvw
    scale = dh ** -0.5
    tq = kb = min(128, s_len)
    nq = s_len // tq
    n_sel = s_len // NSA_SEL_BLOCK
    topn = min(NSA_TOPK, n_sel)
    w_in = w["nsa_w_in"]
    proj = matmul_ws(u, [(w_in, 0)], main_w, name="nsa_in").reshape(bsz, s_len, main_w)
    gates = matmul(u, [(w_in[:, main_w:].astype(BF16), 0)], w_in.shape[1] - main_w, epilogue=_ep_sigmoid,
                   name="nsa_gates")
    gates = jnp.transpose(gates.reshape(bsz, s_len, groups, rep * 3), (0, 2, 1, 3))
    slot = lambda j: (qw + j * kvw) // dh
    roped = nsa_rope(proj, n_heads, [slot(2) + g for g in range(groups)] + [slot(4) + g for g in range(groups)],
                     dh, scale)
    kc = nsa_compress(proj[..., qw:qw + kvw], w["nsa_cmp_pos_k"], w["nsa_cmp_k_w1"], w["nsa_cmp_k_w2"],
                      bsz, s_len, groups, dh, True)
    vc = nsa_compress(proj[..., qw + kvw:qw + 2 * kvw], w["nsa_cmp_pos_v"], w["nsa_cmp_v_w1"], w["nsa_cmp_v_w2"],
                      bsz, s_len, groups, dh, False)
    nc = kc.shape[1]
    cs = jnp.arange(nc)[:, None] * NSA_CMP_STRIDE
    ss = jnp.arange(n_sel)[None, :] * NSA_SEL_BLOCK
    overlap = jnp.clip(jnp.minimum(cs + NSA_CMP_BLOCK, ss + NSA_SEL_BLOCK) - jnp.maximum(cs, ss), 0, None)
    overlap = overlap.astype(F32) / NSA_CMP_BLOCK

    q_spec3 = pl.BlockSpec((1, tq, rep * dh), lambda b_, g, i: (b_, i, g))
    o_c, sel = pl.pallas_call(
        functools.partial(_nsa_cmp_select_kernel, tq=tq, rep=rep, dh=dh, topn=topn),
        grid=(bsz, groups, nq),
        in_specs=[q_spec3,
                  pl.BlockSpec((1, nc, dh), lambda b_, g, i: (b_ * groups + g, 0, 0)),
                  pl.BlockSpec((1, nc, dh), lambda b_, g, i: (b_ * groups + g, 0, 0)),
                  pl.BlockSpec((nc, n_sel), lambda b_, g, i: (0, 0))],
        out_specs=[q_spec3, pl.BlockSpec((1, 1, tq, n_sel), lambda b_, g, i: (b_, g, i, 0))],
        out_shape=[jax.ShapeDtypeStruct((bsz, s_len, qw), F32),
                   jax.ShapeDtypeStruct((bsz, groups, s_len, n_sel), F32)],
        compiler_params=_params("parallel", "parallel", "parallel"),
        name="nsa_cmp_select",
    )(roped, kc, vc, overlap)

    q_spec = pl.BlockSpec((1, tq, rep * dh), lambda b_, g, i, j: (b_, i, g))
    flash_scratch = [pltpu.VMEM((rep * tq, dh), BF16), pltpu.VMEM((rep * tq, LANES), F32),
                     pltpu.VMEM((rep * tq, LANES), F32), pltpu.VMEM((rep * tq, dh), F32)]
    last_kb = lambda i: (i * tq + tq - 1) // kb
    o_s = pl.pallas_call(
        functools.partial(_nsa_select_kernel, tq=tq, kb=kb, rep=rep, dh=dh),
        grid=(bsz, groups, nq, s_len // kb),
        in_specs=[q_spec,
                  pl.BlockSpec((1, kb, dh), lambda b_, g, i, j: (b_, jnp.minimum(j, last_kb(i)), n_heads + g)),
                  pl.BlockSpec((1, kb, dh), lambda b_, g, i, j: (b_, jnp.minimum(j, last_kb(i)), slot(3) + g)),
                  pl.BlockSpec((1, 1, tq, n_sel), lambda b_, g, i, j: (b_, g, i, 0))],
        out_specs=q_spec,
        out_shape=jax.ShapeDtypeStruct((bsz, s_len, qw), F32),
        scratch_shapes=flash_scratch,
        compiler_params=_params("parallel", "parallel", "parallel", "arbitrary"),
        name="nsa_select_attn",
    )(roped, roped, proj, sel)

    n_steps = NSA_WINDOW // kb + tq // kb
    win_blk = lambda i, j: jnp.maximum(i * (tq // kb) - (n_steps - tq // kb) + j, 0)
    o = pl.pallas_call(
        functools.partial(_nsa_window_kernel, tq=tq, kb=kb, rep=rep, dh=dh, window=NSA_WINDOW, n_steps=n_steps),
        grid=(bsz, groups, nq, n_steps),
        in_specs=[q_spec,
                  pl.BlockSpec((1, kb, dh), lambda b_, g, i, j: (b_, win_blk(i, j), n_heads + groups + g)),
                  pl.BlockSpec((1, kb, dh), lambda b_, g, i, j: (b_, win_blk(i, j), slot(5) + g)),
                  q_spec, q_spec,
                  pl.BlockSpec((1, 1, tq, rep * 3), lambda b_, g, i, j: (b_, g, i, 0))],
        out_specs=q_spec,
        out_shape=jax.ShapeDtypeStruct((bsz, s_len, qw), BF16),
        scratch_shapes=flash_scratch,
        compiler_params=_params("parallel", "parallel", "parallel", "arbitrary"),
        name="nsa_window_attn",
    )(roped, roped, proj, o_c, o_s, gates)
    return matmul_ws(o.reshape(t, qw), [(w["nsa_w_out"], 0)], d, epilogue=_ep_residual, extras=[(h, "mn")],
                     name="nsa_out")


def _rope_t_kernel(x_ref, cc_ref, ss_ref, o_ref, *, n_rope, scale):
    x = x_ref[0]
    roped = (x * cc_ref[...] + pltpu.roll(x, x.shape[-1] // 2, 1) * ss_ref[...]) * scale
    out = jnp.where(pl.program_id(2) < n_rope, roped, x)
    o_ref[0] = out.T.astype(o_ref.dtype)


def nsa_rope_t(proj, slots, n_rope, dh, scale, tb=512):
    bsz, s_len, _ = proj.shape
    tb = min(tb, s_len)
    cc, ss = _rope_tables(jnp.arange(s_len), dh)
    table = jnp.asarray(slots, jnp.int32)
    grid_spec = pltpu.PrefetchScalarGridSpec(
        num_scalar_prefetch=1,
        grid=(bsz, s_len // tb, len(slots)),
        in_specs=[pl.BlockSpec((1, tb, dh), lambda b_, t, j, tab: (b_, t, tab[j])),
                  pl.BlockSpec((tb, dh), lambda b_, t, j, tab: (t, 0)),
                  pl.BlockSpec((tb, dh), lambda b_, t, j, tab: (t, 0))],
        out_specs=pl.BlockSpec((1, dh, tb), lambda b_, t, j, tab: (b_, j, t)),
    )
    kern = lambda tab, x_ref, cc_ref, ss_ref, o_ref: _rope_t_kernel(x_ref, cc_ref, ss_ref, o_ref,
                                                                   n_rope=n_rope, scale=scale)
    return pl.pallas_call(
        kern,
        grid_spec=grid_spec,
        out_shape=jax.ShapeDtypeStruct((bsz, len(slots) * dh, s_len), BF16),
        compiler_params=_params("parallel", "parallel", "arbitrary"),
        name="nsa_rope_t",
    )(table, proj, cc, ss)


def _cmp_finish_t_kernel(z_ref, bias_ref, w2_ref, o_ref, *, hidden):
    z = z_ref[0]
    nc = z.shape[0]
    nxt = pltpu.roll(z[:, hidden:], nc - 1, 0)
    hid = _silu(z[:, :hidden] + nxt + bias_ref[...])
    o_ref[0] = _dot_nt(w2_ref[...], hid.astype(BF16)).astype(o_ref.dtype)


def _nsa_cmp_select_t_kernel(q_ref, kc_ref, vc_ref, ov_ref, oc_ref, sel_ref, *, tq, rep, dh, topn):
    qi = pl.program_id(2)
    kc = kc_ref[0]
    vct = vc_ref[0]
    nc = kc.shape[0]
    n_sel = sel_ref.shape[2]
    t = qi * tq + lax.broadcasted_iota(jnp.int32, (nc, tq), 1)
    cmp_end = lax.broadcasted_iota(jnp.int32, (nc, tq), 0) * NSA_CMP_STRIDE + (NSA_CMP_BLOCK - 1)
    visible = cmp_end <= t
    s = [jnp.where(visible, jnp.dot(kc, q_ref[0, r * dh:(r + 1) * dh, :], preferred_element_type=F32), NEG_BIG)
         for r in range(rep)]
    e = [jnp.where(visible, jnp.exp(x - jnp.max(x, axis=0, keepdims=True)), 0.0) for x in s]
    den = [jnp.sum(x, axis=0, keepdims=True) for x in e]
    p = [e[r] / jnp.where(den[r] > 0, den[r], 1.0) for r in range(rep)]
    for r in range(rep):
        oc_ref[0, r * dh:(r + 1) * dh, :] = jnp.dot(vct, p[r].astype(BF16), preferred_element_type=F32)
    psum = p[0]
    for r in range(1, rep):
        psum = psum + p[r]
    imp = _dot_hi(ov_ref[...], psum)
    blk = lax.broadcasted_iota(jnp.int32, (n_sel, tq), 0)
    cur = (qi * tq + lax.broadcasted_iota(jnp.int32, (n_sel, tq), 1)) // NSA_SEL_BLOCK
    forced = (blk == 0) | (blk == cur) | (blk == cur - 1)
    imp = jnp.where(forced, NSA_FORCED_SCORE, imp)
    imp = jnp.where(blk > cur, -jnp.inf, imp)
    sel = jnp.zeros((n_sel, tq), F32)
    for _ in range(topn):
        m = jnp.max(imp, axis=0, keepdims=True)
        first = jnp.min(jnp.where(imp == m, blk, n_sel), axis=0, keepdims=True)
        hit = blk == first
        sel = jnp.where(hit, 1.0, sel)
        imp = jnp.where(hit, -jnp.inf, imp)
    sel_ref[0, 0] = sel


def _flash_t_init(m_ref, l_ref, acc_ref):
    m_ref[...] = jnp.full_like(m_ref, NEG_BIG)
    l_ref[...] = jnp.zeros_like(l_ref)
    acc_ref[...] = jnp.zeros_like(acc_ref)


def _flash_t_step(q_ref, k, vt, mask, m_ref, l_ref, acc_ref, rep, dh):
    hs = range(rep)
    s = [jnp.where(mask, jnp.dot(k, q_ref[0, r * dh:(r + 1) * dh, :], preferred_element_type=F32), NEG_BIG)
         for r in hs]
    m_old = [m_ref[r] for r in hs]
    m_new = [jnp.maximum(m_old[r], jnp.max(s[r], axis=0, keepdims=True)) for r in hs]
    p = [jnp.exp(s[r] - m_new[r]) for r in hs]
    alpha = [jnp.exp(m_old[r] - m_new[r]) for r in hs]
    pv = [jnp.dot(vt, p[r].astype(BF16), preferred_element_type=F32) for r in hs]
    for r in hs:
        m_ref[r] = m_new[r]
        l_ref[r] = alpha[r] * l_ref[r] + jnp.sum(p[r], axis=0, keepdims=True)
        acc_ref[r] = acc_ref[r] * alpha[r] + pv[r]


def _nsa_select_t_kernel(qi_ref, kj_ref, q_ref, k_ref, vt_ref, sel_ref, o_ref, m_ref, l_ref, acc_ref,
                         *, tq, kb, rep, dh):
    pair = pl.program_id(2)
    qi = qi_ref[pair]
    kj = kj_ref[pair]

    @pl.when(kj == 0)
    def _():
        _flash_t_init(m_ref, l_ref, acc_ref)

    kpos = kj * kb + lax.broadcasted_iota(jnp.int32, (kb, tq), 0)
    t = qi * tq + lax.broadcasted_iota(jnp.int32, (kb, tq), 1)
    per = kb // NSA_SEL_BLOCK
    chosen = jnp.zeros((kb, tq), F32)
    for i in range(per):
        row = sel_ref[0, 0, pl.ds(kj * per + i, 1), :]
        chosen = jnp.where((kpos - kj * kb) // NSA_SEL_BLOCK == i, row, chosen)
    mask = (chosen > 0) & (kpos <= t)
    _flash_t_step(q_ref, k_ref[0], vt_ref[0], mask, m_ref, l_ref, acc_ref, rep, dh)

    @pl.when(kj * kb + kb > qi * tq + tq - 1)
    def _():
        for r in range(rep):
            l = l_ref[r]
            o_ref[0, r * dh:(r + 1) * dh, :] = acc_ref[r] / jnp.where(l > 0, l, 1.0)


def _nsa_window_t_kernel(q_ref, k_ref, vt_ref, oc_ref, os_ref, g_ref, o_ref, m_ref, l_ref, acc_ref,
                         *, tq, kb, rep, dh, window, n_steps):
    qi = pl.program_id(2)
    w = pl.program_id(3)
    kblk = qi * (tq // kb) - (n_steps - tq // kb) + w

    @pl.when(w == 0)
    def _():
        _flash_t_init(m_ref, l_ref, acc_ref)

    @pl.when(kblk >= 0)
    def _():
        kpos = kblk * kb + lax.broadcasted_iota(jnp.int32, (kb, tq), 0)
        t = qi * tq + lax.broadcasted_iota(jnp.int32, (kb, tq), 1)
        mask = (kpos <= t) & (kpos > t - window)
        _flash_t_step(q_ref, k_ref[0], vt_ref[0], mask, m_ref, l_ref, acc_ref, rep, dh)

    @pl.when(w == n_steps - 1)
    def _():
        gates = g_ref[0, 0]
        for r in range(rep):
            rows = slice(r * dh, (r + 1) * dh)
            l = l_ref[r]
            o_w = acc_ref[r] / jnp.where(l > 0, l, 1.0)
            o = (gates[3 * r:3 * r + 1, :] * oc_ref[0, rows, :] + gates[3 * r + 1:3 * r + 2, :] * os_ref[0, rows, :]
                 + gates[3 * r + 2:3 * r + 3, :] * o_w)
            o_ref[0, :, rows] = o.T.astype(o_ref.dtype)


def nsa_mixer_t(u, h, w, bsz, s_len):
    t, d = u.shape
    dh, groups = NSA_HEAD_DIM, NSA_N_KV
    n_heads = d // dh
    rep = n_heads // groups
    kvw = groups * dh
    qw = n_heads * dh
    main_w = qw + 6 * kvw
    scale = dh ** -0.5
    tq = kb = min(128, s_len)
    nq = s_len // tq
    n_sel = s_len // NSA_SEL_BLOCK
    topn = min(NSA_TOPK, n_sel)
    w_in = w["nsa_w_in"]
    proj = matmul_ws(u, [(w_in, 0)], main_w, name="nsa_in").reshape(bsz, s_len, main_w)
    gates = matmul(u, [(w_in[:, main_w:].astype(BF16), 0)], w_in.shape[1] - main_w, epilogue=_ep_sigmoid,
                   name="nsa_gates")
    gates = jnp.transpose(gates.reshape(bsz, s_len, groups, rep * 3), (0, 2, 3, 1))
    slot = lambda j: (qw + j * kvw) // dh
    qvt = nsa_rope_t(proj, list(range(n_heads)) + [slot(3) + g for g in range(groups)]
                     + [slot(5) + g for g in range(groups)], n_heads, dh, scale)
    k_rot = nsa_rope(proj, 0, [slot(2) + g for g in range(groups)] + [slot(4) + g for g in range(groups)], dh, 1.0)
    kc = nsa_compress(proj[..., qw:qw + kvw], w["nsa_cmp_pos_k"], w["nsa_cmp_k_w1"], w["nsa_cmp_k_w2"],
                      bsz, s_len, groups, dh, True)
    vct = nsa_compress(proj[..., qw + kvw:qw + 2 * kvw], w["nsa_cmp_pos_v"], w["nsa_cmp_v_w1"], w["nsa_cmp_v_w2"],
                       bsz, s_len, groups, dh, False, transpose_out=True)
    nc = kc.shape[1]
    cs = jnp.arange(nc)[None, :] * NSA_CMP_STRIDE
    ss = jnp.arange(n_sel)[:, None] * NSA_SEL_BLOCK
    overlap_t = jnp.clip(jnp.minimum(cs + NSA_CMP_BLOCK, ss + NSA_SEL_BLOCK) - jnp.maximum(cs, ss), 0, None)
    overlap_t = overlap_t.astype(F32) / NSA_CMP_BLOCK

    qt_spec3 = pl.BlockSpec((1, rep * dh, tq), lambda b_, g, i: (b_, g, i))
    o_c, sel = pl.pallas_call(
        functools.partial(_nsa_cmp_select_t_kernel, tq=tq, rep=rep, dh=dh, topn=topn),
        grid=(bsz, groups, nq),
        in_specs=[qt_spec3,
                  pl.BlockSpec((1, nc, dh), lambda b_, g, i: (b_ * groups + g, 0, 0)),
                  pl.BlockSpec((1, dh, nc), lambda b_, g, i: (b_ * groups + g, 0, 0)),
                  pl.BlockSpec((n_sel, nc), lambda b_, g, i: (0, 0))],
        out_specs=[qt_spec3, pl.BlockSpec((1, 1, n_sel, tq), lambda b_, g, i: (b_, g, 0, i))],
        out_shape=[jax.ShapeDtypeStruct((bsz, qw, s_len), F32),
                   jax.ShapeDtypeStruct((bsz, groups, n_sel, s_len), F32)],
        compiler_params=_params("parallel", "parallel", "parallel"),
        name="nsa_cmp_select",
    )(qvt, kc, vct, overlap_t)

    flash_scratch = lambda n: [pltpu.VMEM((rep, 1, n), F32), pltpu.VMEM((rep, 1, n), F32),
                               pltpu.VMEM((rep, dh, n), F32)]
    tqs = min(2 * tq, s_len)
    pairs = [(i, j) for i in range(s_len // tqs) for j in range((i * tqs + tqs - 1) // kb + 1)]
    qi_of = jnp.asarray([pr[0] for pr in pairs], jnp.int32)
    kj_of = jnp.asarray([pr[1] for pr in pairs], jnp.int32)
    o_s = pl.pallas_call(
        functools.partial(_nsa_select_t_kernel, tq=tqs, kb=kb, rep=rep, dh=dh),
        grid_spec=pltpu.PrefetchScalarGridSpec(
            num_scalar_prefetch=2,
            grid=(bsz, groups, len(pairs)),
            in_specs=[pl.BlockSpec((1, rep * dh, tqs), lambda b_, g, pr, qi, kj: (b_, g, qi[pr])),
                      pl.BlockSpec((1, kb, dh), lambda b_, g, pr, qi, kj: (b_, kj[pr], g)),
                      pl.BlockSpec((1, dh, kb), lambda b_, g, pr, qi, kj: (b_, n_heads + g, kj[pr])),
                      pl.BlockSpec((1, 1, n_sel, tqs), lambda b_, g, pr, qi, kj: (b_, g, 0, qi[pr]))],
            out_specs=pl.BlockSpec((1, rep * dh, tqs), lambda b_, g, pr, qi, kj: (b_, g, qi[pr])),
            scratch_shapes=flash_scratch(tqs)),
        out_shape=jax.ShapeDtypeStruct((bsz, qw, s_len), F32),
        compiler_params=_params("parallel", "parallel", "arbitrary"),
        name="nsa_select_attn",
    )(qi_of, kj_of, qvt, k_rot, qvt, sel)

    n_steps = NSA_WINDOW // kb + tq // kb
    win_blk = lambda i, j: jnp.maximum(i * (tq // kb) - (n_steps - tq // kb) + j, 0)
    qt_spec = pl.BlockSpec((1, rep * dh, tq), lambda b_, g, i, j: (b_, g, i))
    o = pl.pallas_call(
        functools.partial(_nsa_window_t_kernel, tq=tq, kb=kb, rep=rep, dh=dh, window=NSA_WINDOW, n_steps=n_steps),
        grid=(bsz, groups, nq, n_steps),
        in_specs=[qt_spec,
                  pl.BlockSpec((1, kb, dh), lambda b_, g, i, j: (b_, win_blk(i, j), groups + g)),
                  pl.BlockSpec((1, dh, kb), lambda b_, g, i, j: (b_, n_heads + groups + g, win_blk(i, j))),
                  qt_spec, qt_spec,
                  pl.BlockSpec((1, 1, rep * 3, tq), lambda b_, g, i, j: (b_, g, 0, i))],
        out_specs=pl.BlockSpec((1, tq, rep * dh), lambda b_, g, i, j: (b_, i, g)),
        out_shape=jax.ShapeDtypeStruct((bsz, s_len, qw), BF16),
        scratch_shapes=flash_scratch(tq),
        compiler_params=_params("parallel", "parallel", "parallel", "arbitrary"),
        name="nsa_window_attn",
    )(qvt, k_rot, qvt, o_c, o_s, gates)
    return matmul_ws(o.reshape(t, qw), [(w["nsa_w_out"], 0)], d, epilogue=_ep_residual, extras=[(h, "mn")],
                     name="nsa_out")


_MATMUL_WEIGHTS = ("pl_proj", "rw_w1", "rw_w2", "rw_a1", "rw_a2", "rw_g1", "rw_g2", "moe1_w_out", "moe3_w_out")


def kernel(x, p, norm_mix, norm_ffn, norm_pl, pl_proj, pl_gate, norm_final, mb_w_in, mb_conv_w, mb_conv_b, mb_dt_bias, mb_a_log, mb_d_skip, mb_norm_w, mb_w_out, nsa_w_in, nsa_cmp_pos_k, nsa_cmp_pos_v, nsa_cmp_k_w1, nsa_cmp_k_w2, nsa_cmp_v_w1, nsa_cmp_v_w2, nsa_w_out, hg_w_in, hg_lb_logits, hg_norm_w, hg_w_out, rw_mu, rw_w_rkv, rw_w0, rw_w1, rw_w2, rw_a0, rw_a1, rw_a2, rw_g1, rw_g2, rw_k_k, rw_k_a, rw_r_k, rw_ln_w, rw_ln_b, rw_w_out, ffn0_w_in, ffn0_w_out, moe1_router, moe1_w_in, moe1_w_out, ffn2_w_in, ffn2_w_out, moe3_router, moe3_w_in, moe3_w_out):
    w = dict(locals())
    for name in _MATMUL_WEIGHTS:
        w[name] = w[name].astype(BF16)
    bsz, s_len, d = x.shape
    depth = p.shape[0]
    t = bsz * s_len
    lb_all = jax.nn.softmax(hg_lb_logits.astype(F32), axis=0)
    lb_all = jnp.cumsum(lb_all, axis=0) - lb_all[0]
    dense = [(w["ffn0_w_in"], w["ffn0_w_out"]), (w["ffn2_w_in"], w["ffn2_w_out"])]
    moe = [(moe1_router, w["moe1_w_in"], w["moe1_w_out"]), (moe3_router, w["moe3_w_in"], w["moe3_w_out"])]
    p_bf = p.reshape(depth, t, p.shape[-1]).astype(BF16)
    h = x.reshape(t, d)
    for i in range(depth):
        kind = i % 4
        if kind == 0:
            h = mamba2_mixer(rmsnorm(h, norm_mix[i]), h, w, bsz, s_len)
        elif kind == 1:
            h = nsa_mixer_t(rmsnorm(h, norm_mix[i]), h, w, bsz, s_len)
        elif kind == 2:
            h = hgrn2_mixer(rmsnorm(h, norm_mix[i]), h, w, lb_all[i], bsz, s_len)
        else:
            h = rwkv7_mixer(rmsnorm(h, norm_mix[i], out_dtype=F32), h, w, bsz, s_len)
        v = rmsnorm(h, norm_ffn[i])
        if i % 2 == 0:
            h = dense_ffn(v, h, *dense[i // 2])
        else:
            h = moe_ffn(v, h, *moe[i // 2])
        h = ple_gate(h, p_bf[i], norm_pl[i], w["pl_proj"][i], pl_gate, i)
    return rmsnorm(h, norm_final, out_dtype=F32).reshape(bsz, s_len, d)
```

```python
import functools
import math

import jax
import jax.numpy as jnp
from jax import lax
from jax.experimental import pallas as pl
from jax.experimental.pallas import tpu as pltpu

F32 = jnp.float32
BF16 = jnp.bfloat16

NORM_EPS = 1e-6
ROPE_THETA = 10000.0

V7X_VMEM_BYTES = 64 * 1024 * 1024
VMEM_LIMIT_BYTES = V7X_VMEM_BYTES - 8 * 1024 * 1024
LANES = 128

MB_HEAD_DIM = 64
MB_N_GROUPS = 8
MB_D_STATE = 128
MB_CONV = 4
MB_CHUNK = 128

NSA_HEAD_DIM = 128
NSA_N_KV = 4
NSA_CMP_BLOCK = 32
NSA_CMP_STRIDE = 16
NSA_SEL_BLOCK = 64
NSA_TOPK = 16
NSA_WINDOW = 512
NSA_FORCED_SCORE = 1e9

HG_HEAD_DIM = 128
HG_CHUNK = 32

RW_HEAD_DIM = 64
RW_LN_EPS = 64e-5
RW_CHUNK = 128

MOE_TOPK = 2


def _params(*semantics):
    return pltpu.CompilerParams(dimension_semantics=semantics, vmem_limit_bytes=VMEM_LIMIT_BYTES)


def _pick(n, target):
    if n <= target:
        return n
    for c in range(target, 0, -1):
        if n % c == 0:
            return c
    return n


def _silu(x):
    return x * jax.nn.sigmoid(x)


def _rmsnorm_kernel(x_ref, g_ref, o_ref):
    x = x_ref[...]
    ms = jnp.mean(x * x, axis=-1, keepdims=True)
    o_ref[...] = (x * lax.rsqrt(ms + NORM_EPS) * g_ref[...]).astype(o_ref.dtype)


def rmsnorm(x, gain, out_dtype=BF16, name="rmsnorm"):
    m, d = x.shape
    bm = _pick(m, 256)
    return pl.pallas_call(
        _rmsnorm_kernel,
        grid=(m // bm,),
        in_specs=[pl.BlockSpec((bm, d), lambda i: (i, 0)), pl.BlockSpec((1, d), lambda i: (0, 0))],
        out_specs=pl.BlockSpec((bm, d), lambda i: (i, 0)),
        out_shape=jax.ShapeDtypeStruct((m, d), out_dtype),
        compiler_params=_params("parallel"),
        name=name,
    )(x, gain.reshape(1, d).astype(F32))


def _mm_kernel(*refs, n_w, n_extra, nk, epilogue):
    x_ref = refs[0]
    w_refs = refs[1:1 + n_w]
    e_refs = refs[1 + n_w:1 + n_w + n_extra]
    o_ref = refs[1 + n_w + n_extra]
    acc_refs = refs[2 + n_w + n_extra:]
    x = x_ref[...]
    if nk == 1:
        accs = [jnp.dot(x, w[...], preferred_element_type=F32) for w in w_refs]
        o_ref[...] = epilogue(accs, [e[...] for e in e_refs]).astype(o_ref.dtype)
        return
    k = pl.program_id(2)

    @pl.when(k == 0)
    def _():
        for a in acc_refs:
            a[...] = jnp.zeros_like(a)

    for a, w in zip(acc_refs, w_refs):
        a[...] += jnp.dot(x, w[...], preferred_element_type=F32)

    @pl.when(k == nk - 1)
    def _():
        o_ref[...] = epilogue([a[...] for a in acc_refs], [e[...] for e in e_refs]).astype(o_ref.dtype)


def _first(accs, extras):
    return accs[0]


def matmul(x, ws, n_out, *, epilogue=_first, extras=(), out_dtype=F32, bm=1024, bn=512, bk=None, name="matmul"):
    m, kdim = x.shape
    bm = _pick(m, bm)
    bn = _pick(n_out, bn)
    if bk is None:
        bk = kdim if kdim <= 4096 else _pick(kdim, 4096)
    nk = kdim // bk
    assert kdim % bk == 0 and m % bm == 0 and n_out % bn == 0
    in_specs = [pl.BlockSpec((bm, bk), lambda i, j, k: (i, k))]
    args = [x]
    for w, off in ws:
        assert off % bn == 0 and w.shape[0] == kdim
        in_specs.append(pl.BlockSpec((bk, bn), functools.partial(lambda i, j, k, o: (k, j + o), o=off // bn)))
        args.append(w)
    for arr, kind in extras:
        if kind == "mn":
            in_specs.append(pl.BlockSpec((bm, bn), lambda i, j, k: (i, j)))
        elif kind == "m":
            in_specs.append(pl.BlockSpec((bm, arr.shape[1]), lambda i, j, k: (i, 0)))
        elif kind == "kn":
            in_specs.append(pl.BlockSpec((arr.shape[0], bn), lambda i, j, k: (0, j)))
        else:
            in_specs.append(pl.BlockSpec((1, bn), lambda i, j, k: (0, j)))
        args.append(arr)
    scratch = [pltpu.VMEM((bm, bn), F32) for _ in ws] if nk > 1 else []
    kern = functools.partial(_mm_kernel, n_w=len(ws), n_extra=len(extras), nk=nk, epilogue=epilogue)
    return pl.pallas_call(
        kern,
        grid=(m // bm, n_out // bn, nk),
        in_specs=in_specs,
        out_specs=pl.BlockSpec((bm, bn), lambda i, j, k: (i, j)),
        out_shape=jax.ShapeDtypeStruct((m, n_out), out_dtype),
        scratch_shapes=scratch,
        compiler_params=_params("parallel", "parallel", "arbitrary"),
        name=name,
    )(*args)


def _mm_ws_kernel(*refs, n_w, n_extra, epilogue):
    x_ref = refs[0]
    w_refs = refs[1:1 + n_w]
    e_refs = refs[1 + n_w:1 + n_w + n_extra]
    o_ref = refs[1 + n_w + n_extra]
    wb_refs = refs[2 + n_w + n_extra:]

    @pl.when(pl.program_id(1) == 0)
    def _():
        for w, wb in zip(w_refs, wb_refs):
            wb[...] = w[...].reshape(wb.shape).astype(BF16)

    x = x_ref[...]
    accs = [jnp.dot(x, wb[...], preferred_element_type=F32) for wb in wb_refs]
    o_ref[...] = epilogue(accs, [e[...] for e in e_refs]).astype(o_ref.dtype)


def matmul_ws(x, ws, n_out, *, epilogue=_first, extras=(), out_dtype=F32, bm=1024, bn=512, w_buffers=2,
              name="matmul_ws"):
    m, kdim = x.shape
    bm = _pick(m, bm)
    bn = _pick(n_out, bn)
    assert m % bm == 0 and n_out % bn == 0
    mode = {} if w_buffers == 2 else {"pipeline_mode": pl.Buffered(w_buffers)}
    in_specs = [pl.BlockSpec((bm, kdim), lambda j, i: (i, 0))]
    args = [x]
    for w, off in ws:
        if w.ndim == 3:
            e, o = off
            assert o % bn == 0 and w.shape[1] == kdim
            in_specs.append(pl.BlockSpec((1, kdim, bn), functools.partial(lambda j, i, e_, o_: (e_, 0, j + o_),
                                                                          e_=e, o_=o // bn), **mode))
        else:
            assert off % bn == 0 and w.shape[0] == kdim
            in_specs.append(pl.BlockSpec((kdim, bn), functools.partial(lambda j, i, o_: (0, j + o_), o_=off // bn),
                                         **mode))
        args.append(w)
    for arr, kind in extras:
        if kind == "mn":
            in_specs.append(pl.BlockSpec((bm, bn), lambda j, i: (i, j)))
        elif kind == "m":
            in_specs.append(pl.BlockSpec((bm, arr.shape[1]), lambda j, i: (i, 0)))
        elif kind == "kn":
            in_specs.append(pl.BlockSpec((arr.shape[0], bn), lambda j, i: (0, j)))
        else:
            in_specs.append(pl.BlockSpec((1, bn), lambda j, i: (0, j)))
        args.append(arr)
    kern = functools.partial(_mm_ws_kernel, n_w=len(ws), n_extra=len(extras), epilogue=epilogue)
    return pl.pallas_call(
        kern,
        grid=(n_out // bn, m // bm),
        in_specs=in_specs,
        out_specs=pl.BlockSpec((bm, bn), lambda j, i: (i, j)),
        out_shape=jax.ShapeDtypeStruct((m, n_out), out_dtype),
        scratch_shapes=[pltpu.VMEM((kdim, bn), BF16) for _ in ws],
        compiler_params=_params("parallel", "arbitrary"),
        name=name,
    )(*args)


def _ep_residual(accs, extras):
    return extras[0] + accs[0]


def _ep_swiglu(accs, extras):
    return _silu(accs[0]) * accs[1]


def _ep_bias(accs, extras):
    return accs[0] + extras[0]


def _ep_tanh(accs, extras):
    return jnp.tanh(accs[0])


def _ep_sigmoid(accs, extras):
    return jax.nn.sigmoid(accs[0])


def _ep_bias_sigmoid(accs, extras):
    return jax.nn.sigmoid(accs[0] + extras[0])


def _ep_rw_logdecay(accs, extras):
    w = -jax.nn.softplus(-(accs[0] + extras[0])) - 0.5
    return -jnp.exp(w)


def _ep_ple_gate(accs, extras):
    return extras[0] + jnp.dot(extras[1], extras[2], preferred_element_type=F32) * jax.nn.sigmoid(accs[0])


def _conv_silu_kernel(x_ref, w_ref, b_ref, o_ref, *, k_width):
    x = x_ref[0]
    row = lax.broadcasted_iota(jnp.int32, x.shape, 0)
    y = b_ref[...] + w_ref[k_width - 1:k_width, :] * x
    for j in range(k_width - 1):
        shift = k_width - 1 - j
        xs = jnp.where(row >= shift, pltpu.roll(x, shift, 0), 0.0)
        y = y + w_ref[j:j + 1, :] * xs
    o_ref[0] = _silu(y)


def conv_silu(x, w, b):
    bsz, s_len, c = x.shape
    cb = _pick(c, 256)
    k_width = w.shape[0]
    return pl.pallas_call(
        functools.partial(_conv_silu_kernel, k_width=k_width),
        grid=(bsz, c // cb),
        in_specs=[pl.BlockSpec((1, s_len, cb), lambda b_, j: (b_, 0, j)),
                  pl.BlockSpec((k_width, cb), lambda b_, j: (0, j)),
                  pl.BlockSpec((1, cb), lambda b_, j: (0, j))],
        out_specs=pl.BlockSpec((1, s_len, cb), lambda b_, j: (b_, 0, j)),
        out_shape=jax.ShapeDtypeStruct(x.shape, F32),
        compiler_params=_params("parallel", "parallel"),
        name="mamba_conv_silu",
    )(x, w, b.reshape(1, c))


def _cumsum_rows(x, n):
    row = lax.broadcasted_iota(jnp.int32, x.shape, 0)
    s = 1
    while s < n:
        x = x + jnp.where(row >= s, pltpu.roll(x, s, 0), 0.0)
        s *= 2
    return x


def _cumsum_lanes(x, n):
    col = lax.broadcasted_iota(jnp.int32, x.shape, 1)
    s = 1
    while s < n:
        x = x + jnp.where(col >= s, pltpu.roll(x, s, 1), 0.0)
        s *= 2
    return x


def _dot_nt(a, b):
    return lax.dot_general(a, b, (((1,), (1,)), ((), ())), preferred_element_type=F32)


def _dot_tn(a, b):
    return lax.dot_general(a, b, (((0,), (0,)), ((), ())), preferred_element_type=F32)


def _ssd_kernel(xs_ref, b_ref, c_ref, z_ref, dt_ref, dtt_ref, bias_r_ref, bias_c_ref, alog_r_ref, alog_c_ref,
                dskip_ref, normw_ref, o_ref, state_ref, y_ref, *, chunk, heads, p_dim):
    @pl.when(pl.program_id(2) == 0)
    def _():
        state_ref[...] = jnp.zeros_like(state_ref)

    dt = jax.nn.softplus(dt_ref[0, 0] + bias_r_ref[0])
    dtt = jax.nn.softplus(dtt_ref[0, 0] + bias_c_ref[0])
    a_cum = _cumsum_rows(dt * -jnp.exp(alog_r_ref[0]), chunk)
    a_cum_t = _cumsum_lanes(dtt * -jnp.exp(alog_c_ref[0]), chunk)
    xs = xs_ref[0]
    bmat = b_ref[0]
    cmat = c_ref[0].astype(BF16)
    cb = _dot_nt(cmat, bmat.astype(BF16))
    b_t = bmat.T.astype(BF16)
    li = lax.broadcasted_iota(jnp.int32, (chunk, chunk), 0)
    si = lax.broadcasted_iota(jnp.int32, (chunk, chunk), 1)
    causal = li >= si
    per = LANES // p_dim
    lane_seg = lax.broadcasted_iota(jnp.int32, (1, LANES), 1) // p_dim

    def pick(vals):
        out = vals[-1]
        for i in range(per - 2, -1, -1):
            out = jnp.where(lane_seg == i, vals[i], out)
        return out

    dot = functools.partial(jnp.dot, preferred_element_type=F32)
    es = range(heads)
    tiles = range(heads // per)
    col = [a_cum[:, e:e + 1] for e in es]
    a_last = [a_cum_t[e:e + 1, chunk - 1:chunk] for e in es]
    m = [(cb * jnp.exp(jnp.where(causal, col[e] - a_cum_t[e:e + 1, :], -jnp.inf))).astype(BF16) for e in es]
    of = lambda vals, i: [vals[i * per + j] for j in range(per)]
    xdt = [xs[:, i * LANES:(i + 1) * LANES] * pick([dt[:, e:e + 1] for e in of(es, i)]) for i in tiles]
    xdt_b = [x.astype(BF16) for x in xdt]
    st = [state_ref[i] for i in tiles]
    y_in = [pick([dot(m[e], xdt_b[i]) for e in of(es, i)]) for i in tiles]
    y_st = [dot(cmat, st[i].astype(BF16)) * pick([jnp.exp(c) for c in of(col, i)]) for i in tiles]
    to_end = [pick([jnp.exp(a_last[e] - col[e]) for e in of(es, i)]) for i in tiles]
    for i in tiles:
        state_ref[i] = (st[i] * pick([jnp.exp(x) for x in of(a_last, i)])
                        + dot(b_t, (xdt[i] * to_end[i]).astype(BF16)))
        y_ref[:, i * LANES:(i + 1) * LANES] = y_in[i] + y_st[i]
    y = y_ref[...] + xs * dskip_ref[...]
    y = y * _silu(z_ref[0])
    ms = jnp.mean(y * y, axis=-1, keepdims=True)
    o_ref[0] = (y * lax.rsqrt(ms + NORM_EPS) * normw_ref[...]).astype(o_ref.dtype)


def ssd_scan(xbc, z, dt, dt_bias, a_log, d_skip, norm_w, *, chunk=MB_CHUNK):
    bsz, s_len, d_inner = z.shape
    n_heads = dt.shape[-1]
    n_state = MB_D_STATE
    groups = (xbc.shape[-1] - d_inner) // (2 * n_state)
    heads = n_heads // groups
    p_dim = d_inner // n_heads
    gw = heads * p_dim
    assert gw % LANES == 0 and d_inner % n_state == 0
    chunk = min(chunk, s_len)
    nc = s_len // chunk
    b_off = d_inner // n_state
    c_off = b_off + groups
    dt_g = jnp.transpose(dt.reshape(bsz, s_len, groups, heads), (0, 2, 1, 3))
    dt_gt = jnp.transpose(dt_g, (0, 1, 3, 2))
    kern = functools.partial(_ssd_kernel, chunk=chunk, heads=heads, p_dim=p_dim)
    per_group = lambda b_, g, c: (g, 0, 0)
    return pl.pallas_call(
        kern,
        grid=(bsz, groups, nc),
        in_specs=[pl.BlockSpec((1, chunk, gw), lambda b_, g, c: (b_, c, g)),
                  pl.BlockSpec((1, chunk, n_state), lambda b_, g, c: (b_, c, b_off + g)),
                  pl.BlockSpec((1, chunk, n_state), lambda b_, g, c: (b_, c, c_off + g)),
                  pl.BlockSpec((1, chunk, gw), lambda b_, g, c: (b_, c, g)),
                  pl.BlockSpec((1, 1, chunk, heads), lambda b_, g, c: (b_, g, c, 0)),
                  pl.BlockSpec((1, 1, heads, chunk), lambda b_, g, c: (b_, g, 0, c)),
                  pl.BlockSpec((1, 1, heads), per_group),
                  pl.BlockSpec((1, heads, 1), per_group),
                  pl.BlockSpec((1, 1, heads), per_group),
                  pl.BlockSpec((1, heads, 1), per_group),
                  pl.BlockSpec((1, gw), lambda b_, g, c: (0, g)),
                  pl.BlockSpec((1, gw), lambda b_, g, c: (0, g))],
        out_specs=pl.BlockSpec((1, chunk, gw), lambda b_, g, c: (b_, c, g)),
        out_shape=jax.ShapeDtypeStruct(z.shape, BF16),
        scratch_shapes=[pltpu.VMEM((gw // LANES, n_state, LANES), F32), pltpu.VMEM((chunk, gw), F32)],
        compiler_params=_params("parallel", "parallel", "arbitrary"),
        name="mamba_ssd",
    )(xbc, xbc, xbc, z, dt_g, dt_gt,
      dt_bias.reshape(groups, 1, heads), dt_bias.reshape(groups, heads, 1),
      a_log.reshape(groups, 1, heads), a_log.reshape(groups, heads, 1),
      jnp.repeat(d_skip, p_dim).reshape(1, d_inner), norm_w.reshape(1, d_inner))


def mamba2_mixer(u, h, w, bsz, s_len):
    d_inner = w["mb_w_out"].shape[0]
    n_heads = w["mb_dt_bias"].shape[0]
    w_in = w["mb_w_in"]
    xbc_w = w_in.shape[1] - d_inner - n_heads
    z = matmul_ws(u, [(w_in, 0)], d_inner, name="mb_in_z")
    xbc = matmul_ws(u, [(w_in, d_inner)], xbc_w, name="mb_in_xbc")
    dt = matmul_ws(u, [(w_in, d_inner + xbc_w)], n_heads, name="mb_in_dt")
    xbc = conv_silu(xbc.reshape(bsz, s_len, xbc_w), w["mb_conv_w"], w["mb_conv_b"])
    y = ssd_scan(xbc, z.reshape(bsz, s_len, d_inner), dt.reshape(bsz, s_len, n_heads),
                 w["mb_dt_bias"], w["mb_a_log"], w["mb_d_skip"], w["mb_norm_w"])
    return matmul_ws(y.reshape(bsz * s_len, d_inner), [(w["mb_w_out"], 0)], h.shape[1],
                     epilogue=_ep_residual, extras=[(h, "mn")], bm=512, w_buffers=1, name="mb_out")


def _seg_cumsum_rows(x, seg, reverse=False):
    n = x.shape[0]
    pos = lax.broadcasted_iota(jnp.int32, x.shape, 0) % seg
    s = 1
    while s < seg:
        if reverse:
            x = x + jnp.where(pos < seg - s, pltpu.roll(x, n - s, 0), 0.0)
        else:
            x = x + jnp.where(pos >= s, pltpu.roll(x, s, 0), 0.0)
        s *= 2
    return x


def _hgrn_kernel(q_ref, f_ref, i_ref, g_ref, lb_ref, nw_ref, o_ref, state_ref, *, sub, n_sub, heads, dk):
    @pl.when(pl.program_id(2) == 0)
    def _():
        state_ref[...] = jnp.zeros_like(state_ref)

    lb = lb_ref[...]
    nw = nw_ref[...]
    ti = lax.broadcasted_iota(jnp.int32, (sub, sub), 0)
    si = lax.broadcasted_iota(jnp.int32, (sub, sub), 1)
    causal = ti >= si
    f = lb + (1.0 - lb) * jax.nn.sigmoid(f_ref[0])
    lf = jnp.log(f)
    k = 1.0 - f
    b = _seg_cumsum_rows(lf, sub)
    to_end = _seg_cumsum_rows(lf, sub, reverse=True) - lf
    q_dec = (_silu(q_ref[0]) * jnp.exp(b)).astype(BF16)
    k_dec = (k * jnp.exp(-b)).astype(BF16)
    k_end = (k * jnp.exp(to_end)).astype(BF16)
    v = i_ref[0].astype(BF16)
    cs = range(n_sub)
    hs = range(heads)
    blk = lambda x, c, h: x[c * sub:(c + 1) * sub, h * dk:(h + 1) * dk]
    scores = [[jnp.where(causal, _dot_nt(blk(q_dec, c, h), blk(k_dec, c, h)), 0.0).astype(BF16) for h in hs]
              for c in cs]
    upd = [[_dot_tn(blk(v, c, h), blk(k_end, c, h)) for h in hs] for c in cs]
    states = []
    st = [state_ref[h] for h in hs]
    for c in cs:
        states.append(st)
        decay = jnp.exp(b[(c + 1) * sub - 1:(c + 1) * sub, :])
        st = [st[h] * decay[:, h * dk:(h + 1) * dk] + upd[c][h] for h in hs]
    for h in hs:
        state_ref[h] = st[h]
    for c in cs:
        rows = slice(c * sub, (c + 1) * sub)
        for h in hs:
            o = (jnp.dot(scores[c][h], blk(v, c, h), preferred_element_type=F32)
                 + _dot_nt(blk(q_dec, c, h), states[c][h].astype(BF16)))
            o = o * lax.rsqrt(jnp.mean(o * o, axis=-1, keepdims=True) + NORM_EPS) * nw
            cols = slice(h * dk, (h + 1) * dk)
            o_ref[0, rows, cols] = (o * _silu(g_ref[0, rows, cols])).astype(o_ref.dtype)


def hgrn2_scan(proj, lower_bound, norm_w, *, dk=HG_HEAD_DIM, sub=HG_CHUNK, tb=256, heads=4):
    bsz, s_len, d4 = proj.shape
    d = d4 // 4
    n_heads = d // dk
    tb = min(tb, s_len)
    heads = min(heads, n_heads)
    hw = heads * dk
    n_hb = n_heads // heads
    kern = functools.partial(_hgrn_kernel, sub=sub, n_sub=tb // sub, heads=heads, dk=dk)
    spec = lambda part: pl.BlockSpec((1, tb, hw), lambda b_, h_, t: (b_, t, part * n_hb + h_))
    return pl.pallas_call(
        kern,
        grid=(bsz, n_hb, s_len // tb),
        in_specs=[spec(0), spec(1), spec(2), spec(3),
                  pl.BlockSpec((1, hw), lambda b_, h_, t: (0, h_)),
                  pl.BlockSpec((1, dk), lambda b_, h_, t: (0, 0))],
        out_specs=pl.BlockSpec((1, tb, hw), lambda b_, h_, t: (b_, t, h_)),
        out_shape=jax.ShapeDtypeStruct((bsz, s_len, d), BF16),
        scratch_shapes=[pltpu.VMEM((heads, dk, dk), F32)],
        compiler_params=_params("parallel", "parallel", "arbitrary"),
        name="hgrn2_scan",
    )(proj, proj, proj, proj, lower_bound.reshape(1, d), norm_w.reshape(1, dk))


def hgrn2_mixer(u, h, w, lower_bound, bsz, s_len):
    d = h.shape[1]
    proj = matmul_ws(u, [(w["hg_w_in"], 0)], 4 * d, name="hg_in")
    o = hgrn2_scan(proj.reshape(bsz, s_len, 4 * d), lower_bound, w["hg_norm_w"])
    return matmul_ws(o.reshape(bsz * s_len, d), [(w["hg_w_out"], 0)], d,
                     epilogue=_ep_residual, extras=[(h, "mn")], name="hg_out")


def dense_ffn(v, h, w_in, w_out):
    f = w_out.shape[0]
    hid = matmul_ws(v, [(w_in, 0), (w_in, f)], f, epilogue=_ep_swiglu, out_dtype=BF16, bm=1024, bn=256,
                    name="ffn_in")
    return matmul_ws(hid, [(w_out, 0)], h.shape[1], epilogue=_ep_residual, extras=[(h, "mn")], bm=512, w_buffers=1,
                     name="ffn_out")


def _router_kernel(x_ref, r_ref, o_ref, *, n_experts):
    logits = jnp.dot(x_ref[...], r_ref[...], preferred_element_type=F32)
    lane = lax.broadcasted_iota(jnp.int32, logits.shape, 1)
    logits = jnp.where(lane < n_experts, logits, -jnp.inf)
    m1 = jnp.max(logits, axis=-1, keepdims=True)
    i1 = jnp.min(jnp.where(logits == m1, lane, LANES), axis=-1, keepdims=True)
    rest = jnp.where(lane == i1, -jnp.inf, logits)
    m2 = jnp.max(rest, axis=-1, keepdims=True)
    i2 = jnp.min(jnp.where(rest == m2, lane, LANES), axis=-1, keepdims=True)
    e2 = jnp.exp(m2 - m1)
    w1 = 1.0 / (1.0 + e2)
    o_ref[...] = jnp.where(lane == i1, w1, 0.0) + jnp.where(lane == i2, e2 * w1, 0.0)


def moe_router(v, router):
    m, d = v.shape
    n_experts = router.shape[1]
    r_pad = jnp.zeros((d, LANES), BF16).at[:, :n_experts].set(router.astype(BF16))
    bm = _pick(m, 512)
    return pl.pallas_call(
        functools.partial(_router_kernel, n_experts=n_experts),
        grid=(m // bm,),
        in_specs=[pl.BlockSpec((bm, d), lambda i: (i, 0)), pl.BlockSpec((d, LANES), lambda i: (0, 0))],
        out_specs=pl.BlockSpec((bm, LANES), lambda i: (i, 0)),
        out_shape=jax.ShapeDtypeStruct((m, LANES), F32),
        compiler_params=_params("parallel"),
        name="moe_router",
    )(v, r_pad)


def _moe_in_kernel(x_ref, wg_ref, wu_ref, c_ref, o_ref, wgb_ref, wub_ref, *, blocks_per_expert):
    @pl.when(pl.program_id(1) == 0)
    def _():
        wgb_ref[...] = wg_ref[0].astype(BF16)
        wub_ref[...] = wu_ref[0].astype(BF16)

    x = x_ref[...]
    g = jnp.dot(x, wgb_ref[...], preferred_element_type=F32)
    u = jnp.dot(x, wub_ref[...], preferred_element_type=F32)
    e = pl.program_id(0) // blocks_per_expert
    comb = c_ref[...]
    lane = lax.broadcasted_iota(jnp.int32, comb.shape, 1)
    scale = jnp.sum(jnp.where(lane == e, comb, 0.0), axis=-1, keepdims=True)
    o_ref[...] = (_silu(g) * u * scale).astype(o_ref.dtype)


def moe_ffn(v, h, router, w_in, w_out, *, bm=512, bn=512):
    m, d = v.shape
    n_experts, _, two_de = w_in.shape
    de = two_de // 2
    bm = _pick(m, bm)
    bn = _pick(de, bn)
    bpe = de // bn
    comb = moe_router(v, router)
    hid = pl.pallas_call(
        functools.partial(_moe_in_kernel, blocks_per_expert=bpe),
        grid=(n_experts * bpe, m // bm),
        in_specs=[pl.BlockSpec((bm, d), lambda j, i: (i, 0)),
                  pl.BlockSpec((1, d, bn), lambda j, i: (j // bpe, 0, j % bpe)),
                  pl.BlockSpec((1, d, bn), lambda j, i: (j // bpe, 0, j % bpe + bpe)),
                  pl.BlockSpec((bm, LANES), lambda j, i: (i, 0))],
        out_specs=pl.BlockSpec((bm, bn), lambda j, i: (i, j)),
        out_shape=jax.ShapeDtypeStruct((m, n_experts * de), BF16),
        scratch_shapes=[pltpu.VMEM((d, bn), BF16), pltpu.VMEM((d, bn), BF16)],
        compiler_params=_params("parallel", "arbitrary"),
        name="moe_in",
    )(v, w_in, w_in, comb)
    return matmul(hid, [(w_out.reshape(n_experts * de, d), 0)], d, epilogue=_ep_residual, extras=[(h, "mn")],
                  name="moe_out")


MOE_BLOCK = 1024
MOE_UNIT = 128
MOE_TILE = 512


def _moe_gather_kernel(x_ref, tok_ref, o_ref):
    tok = tok_ref[0]
    lane = lax.broadcasted_iota(jnp.int32, (tok.shape[0], x_ref.shape[0]), 1)
    onehot = jnp.where(tok == lane, 1.0, 0.0).astype(BF16)
    o_ref[...] = jnp.dot(onehot, x_ref[...], preferred_element_type=F32).astype(o_ref.dtype)


def _moe_expert_in_kernel(src_ref, exp_ref, first_ref, used_ref, *refs, per):
    x_refs = refs[:per]
    wg_ref, wu_ref, rw_ref, o_ref, wgb_ref, wub_ref, x_scr = refs[per:]
    t = pl.program_id(1)

    @pl.when(t < used_ref[0])
    def _():
        @pl.when(first_ref[t] == 1)
        def _():
            wgb_ref[...] = wg_ref[0].astype(BF16)
            wub_ref[...] = wu_ref[0].astype(BF16)

        unit = x_refs[0].shape[0]
        for i in range(per):
            x_scr[i * unit:(i + 1) * unit, :] = x_refs[i][...]
        x = x_scr[...]
        g = jnp.dot(x, wgb_ref[...], preferred_element_type=F32)
        u = jnp.dot(x, wub_ref[...], preferred_element_type=F32)
        o_ref[...] = (_silu(g) * u * rw_ref[...]).astype(o_ref.dtype)

    @pl.when(t >= used_ref[0])
    def _():
        o_ref[...] = jnp.zeros_like(o_ref)


def _moe_expert_out_kernel(exp_ref, hid_ref, w_ref, o_ref):
    o_ref[...] = jnp.dot(hid_ref[...], w_ref[0], preferred_element_type=F32).astype(o_ref.dtype)


def _moe_scatter_kernel(dst_ref, h_ref, tok_ref, *refs, per):
    y_refs = refs[:per]
    o_ref, y_scr = refs[per:]

    @pl.when(pl.program_id(2) == 0)
    def _():
        o_ref[...] = h_ref[...]

    tok = tok_ref[0]
    row = lax.broadcasted_iota(jnp.int32, (o_ref.shape[0], tok.shape[1]), 0)
    onehot_t = jnp.where(tok == row, 1.0, 0.0).astype(BF16)
    unit = y_refs[0].shape[0]
    for i in range(per):
        y_scr[i * unit:(i + 1) * unit, :] = y_refs[i][...]
    o_ref[...] += jnp.dot(onehot_t, y_scr[...], preferred_element_type=F32)


def moe_ffn_routed(v, h, router, w_in, w_out, *, tb=MOE_BLOCK, unit=MOE_UNIT, tile=MOE_TILE, bn=512, bo=1024):
    m, d = v.shape
    n_experts, _, two_de = w_in.shape
    de = two_de // 2
    tb = min(tb, m)
    nb = m // tb
    per = tile // unit
    bn = _pick(de, bn)
    bo = _pick(d, bo)
    n_assign = MOE_TOPK * tb
    n_slots = -(-(n_assign // unit + n_experts + 1) // per) * per
    groups = n_slots // per
    n_units = nb * (n_assign // unit + n_experts) + n_experts * (per - 1)
    n_tiles = -(-n_units // per)
    n_units = n_tiles * per

    comb = moe_router(v, router)
    wts, ids = lax.top_k(comb[:, :n_experts], MOE_TOPK)
    ea = ids.reshape(nb, n_assign)
    wa = wts.reshape(nb, n_assign)
    ta = jnp.broadcast_to(jnp.repeat(jnp.arange(tb, dtype=jnp.int32), MOE_TOPK)[None], (nb, n_assign))
    order = jnp.argsort(ea, axis=1, stable=True)
    se = jnp.take_along_axis(ea, order, axis=1)
    st = jnp.take_along_axis(ta, order, axis=1)
    sw = jnp.take_along_axis(wa, order, axis=1)
    counts = jnp.sum(jax.nn.one_hot(ea, n_experts, dtype=jnp.int32), axis=1)
    units = -(-counts // unit)
    excl = lambda x, axis: jnp.cumsum(x, axis=axis) - x
    slot_start = excl(units, 1)
    row_start = excl(counts, 1)
    pos = (jnp.take_along_axis(slot_start, se, axis=1) * unit
           + jnp.arange(n_assign, dtype=jnp.int32)[None] - jnp.take_along_axis(row_start, se, axis=1))
    bidx = jnp.arange(nb, dtype=jnp.int32)[:, None]
    row_token = jnp.full((nb, n_slots * unit), -1, jnp.int32).at[bidx, pos].set(st)
    row_weight = jnp.zeros((nb, n_slots * unit), F32).at[bidx, pos].set(sw)
    per_expert = jnp.sum(units, axis=0)
    per_expert_pad = -(-per_expert // per) * per
    e_off = excl(per_expert_pad, 0)
    before = excl(units, 0)
    slot = jnp.arange(n_slots, dtype=jnp.int32)
    slot_end = jnp.cumsum(units, axis=1)
    e_of_slot = jnp.sum(slot[None, :, None] >= slot_end[:, None, :], axis=-1)
    used_slot = e_of_slot < n_experts
    e_clip = jnp.minimum(e_of_slot, n_experts - 1)
    dst_unit = (e_off[e_clip] + jnp.take_along_axis(before, e_clip, axis=1)
                + slot[None] - jnp.take_along_axis(slot_start, e_clip, axis=1))
    dst_unit = jnp.where(used_slot, dst_unit, 0).astype(jnp.int32)
    flat_slot = (bidx * n_slots + slot[None]).astype(jnp.int32)
    zero_slot = n_slots - 1
    src_unit = jnp.full((n_units,), zero_slot, jnp.int32).at[
        jnp.where(used_slot, dst_unit, n_units).reshape(-1)].set(flat_slot.reshape(-1), mode="drop")
    tile_end = jnp.cumsum(per_expert_pad) // per
    tile_ids = jnp.arange(n_tiles, dtype=jnp.int32)
    tile_expert = jnp.minimum(jnp.sum(tile_ids[:, None] >= tile_end[None, :], axis=-1), n_experts - 1).astype(jnp.int32)
    tiles_used = tile_end[-1:].astype(jnp.int32)
    first = jnp.concatenate([jnp.ones((1,), jnp.int32),
                             (tile_expert[1:] != tile_expert[:-1]).astype(jnp.int32)])
    rw_em = row_weight.reshape(nb * n_slots, unit)[src_unit].reshape(n_units * unit, 1)

    xs = pl.pallas_call(
        _moe_gather_kernel,
        grid=(nb, groups),
        in_specs=[pl.BlockSpec((tb, d), lambda b_, g: (b_, 0)),
                  pl.BlockSpec((1, tile, 1), lambda b_, g: (b_, g, 0))],
        out_specs=pl.BlockSpec((tile, d), lambda b_, g: (b_ * groups + g, 0)),
        out_shape=jax.ShapeDtypeStruct((nb * n_slots * unit, d), BF16),
        compiler_params=_params("parallel", "arbitrary"),
        name="moe_gather",
    )(v, row_token.reshape(nb, n_slots * unit, 1))

    bpe = de // bn
    unit_spec = lambda i: pl.BlockSpec((unit, d), lambda j, t, src, ex, fi, us: (src[per * t + i], 0))
    hid = pl.pallas_call(
        functools.partial(_moe_expert_in_kernel, per=per),
        grid_spec=pltpu.PrefetchScalarGridSpec(
            num_scalar_prefetch=4,
            grid=(bpe, n_tiles),
            in_specs=[unit_spec(i) for i in range(per)] + [
                pl.BlockSpec((1, d, bn), lambda j, t, src, ex, fi, us: (ex[t], 0, j),
                             pipeline_mode=pl.Buffered(1)),
                pl.BlockSpec((1, d, bn), lambda j, t, src, ex, fi, us: (ex[t], 0, j + bpe),
                             pipeline_mode=pl.Buffered(1)),
                pl.BlockSpec((tile, 1), lambda j, t, src, ex, fi, us: (t, 0))],
            out_specs=pl.BlockSpec((tile, bn), lambda j, t, src, ex, fi, us: (t, j)),
            scratch_shapes=[pltpu.VMEM((d, bn), BF16), pltpu.VMEM((d, bn), BF16), pltpu.VMEM((tile, d), BF16)]),
        out_shape=jax.ShapeDtypeStruct((n_tiles * tile, de), BF16),
        compiler_params=_params("arbitrary", "arbitrary"),
        name="moe_expert_in",
    )(src_unit, tile_expert, first, tiles_used, *([xs] * per), w_in, w_in, rw_em)

    ys = pl.pallas_call(
        _moe_expert_out_kernel,
        grid_spec=pltpu.PrefetchScalarGridSpec(
            num_scalar_prefetch=1,
            grid=(d // bo, n_tiles),
            in_specs=[pl.BlockSpec((tile, de), lambda n, t, ex: (t, 0)),
                      pl.BlockSpec((1, de, bo), lambda n, t, ex: (ex[t], 0, n))],
            out_specs=pl.BlockSpec((tile, bo), lambda n, t, ex: (t, n))),
        out_shape=jax.ShapeDtypeStruct((n_tiles * tile, d), BF16),
        compiler_params=_params("parallel", "arbitrary"),
        name="moe_expert_out",
    )(tile_expert, hid, w_out)

    y_spec = lambda i: pl.BlockSpec((unit, bo), lambda b_, n, g, dst: (dst[(b_ * groups + g) * per + i], n))
    return pl.pallas_call(
        functools.partial(_moe_scatter_kernel, per=per),
        grid_spec=pltpu.PrefetchScalarGridSpec(
            num_scalar_prefetch=1,
            grid=(nb, d // bo, groups),
            in_specs=[pl.BlockSpec((tb, bo), lambda b_, n, g, dst: (b_, n)),
                      pl.BlockSpec((1, 1, tile), lambda b_, n, g, dst: (b_ * groups + g, 0, 0))]
                     + [y_spec(i) for i in range(per)],
            out_specs=pl.BlockSpec((tb, bo), lambda b_, n, g, dst: (b_, n)),
            scratch_shapes=[pltpu.VMEM((tile, bo), BF16)]),
        out_shape=jax.ShapeDtypeStruct((m, d), F32),
        compiler_params=_params("parallel", "parallel", "arbitrary"),
        name="moe_scatter",
    )(dst_unit.reshape(-1), h, row_token.reshape(nb * groups, 1, tile), *([ys] * per))


def ple_gate(h, p_i, norm_pl, pl_proj, pl_gate, layer):
    d = h.shape[1]
    n = rmsnorm(h, norm_pl, name="rmsnorm_ple")
    return matmul_ws(n, [(pl_gate, (layer, 0))], d, epilogue=_ep_ple_gate,
                     extras=[(h, "mn"), (p_i, "m"), (pl_proj, "kn")], name="ple_gate")


def _rw_mix_kernel(u_ref, mu_ref, *o_refs):
    u = u_ref[0]
    row = lax.broadcasted_iota(jnp.int32, u.shape, 0)
    dx = jnp.where(row >= 1, pltpu.roll(u, 1, 0), 0.0) - u
    for j, o_ref in enumerate(o_refs):
        o_ref[0] = (u + dx * mu_ref[j:j + 1, :]).astype(o_ref.dtype)


def rw_token_mix(u, mu):
    bsz, s_len, d = u.shape
    cb = _pick(d, LANES)
    n_mix = mu.shape[0]
    spec = pl.BlockSpec((1, s_len, cb), lambda b_, j: (b_, 0, j))
    return pl.pallas_call(
        _rw_mix_kernel,
        grid=(bsz, d // cb),
        in_specs=[spec, pl.BlockSpec((n_mix, cb), lambda b_, j: (0, j))],
        out_specs=[spec] * n_mix,
        out_shape=[jax.ShapeDtypeStruct(u.shape, BF16)] * n_mix,
        compiler_params=_params("parallel", "parallel"),
        name="rwkv_token_mix",
    )(u, mu)


def _dot_hi(a, b):
    return jnp.dot(a, b, preferred_element_type=F32, precision=lax.Precision.HIGHEST)


def _rw_scan_kernel(r_ref, k_ref, v_ref, a_ref, lw_ref, g_ref, kk_ref, ka_ref, rk_ref, lnw_ref, lnb_ref,
                    o_ref, state_ref, *, chunk, heads, n):
    @pl.when(pl.program_id(2) == 0)
    def _():
        state_ref[...] = jnp.zeros_like(state_ref)

    hs = range(heads)
    sls = [slice(j * n, (j + 1) * n) for j in hs]
    ti = lax.broadcasted_iota(jnp.int32, (chunk, chunk), 0)
    si = lax.broadcasted_iota(jnp.int32, (chunk, chunk), 1)
    strict = ti > si
    incl = ti >= si
    dot = functools.partial(jnp.dot, preferred_element_type=F32)

    r = [r_ref[0, :, sl] for sl in sls]
    v = [v_ref[0, :, sl] for sl in sls]
    a = [a_ref[0, :, sl] for sl in sls]
    lw = [lw_ref[0, :, sl] for sl in sls]
    k = [k_ref[0, :, sl] for sl in sls]
    kk = [k[j] * kk_ref[:, sls[j]] for j in hs]
    kk = [kk[j] / jnp.maximum(jnp.sqrt(jnp.sum(kk[j] * kk[j], axis=-1, keepdims=True)), 1e-12) for j in hs]
    kmod = [k[j] * (1.0 + (a[j] - 1.0) * ka_ref[:, sls[j]]) for j in hs]
    kka = [kk[j] * a[j] for j in hs]
    cum = [_cumsum_rows(lw[j], chunk) for j in hs]
    cum_end = [c[chunk - 1:chunk, :] for c in cum]
    mid = [c[chunk // 2 - 1:chunk // 2, :] for c in cum]
    e_neg = [jnp.exp(mid[j] - cum[j]) for j in hs]
    am = [(kk[j] * jnp.exp(cum[j] - lw[j] - mid[j])).astype(BF16) for j in hs]
    bm = [(kka[j] * e_neg[j]).astype(BF16) for j in hs]
    km = [(kmod[j] * e_neg[j]).astype(BF16) for j in hs]
    rm = [(r[j] * jnp.exp(cum[j] - mid[j])).astype(BF16) for j in hs]
    a_abs = [(kk[j] * jnp.exp(cum[j] - lw[j])).astype(BF16) for j in hs]
    r_abs = [(r[j] * jnp.exp(cum[j])).astype(BF16) for j in hs]
    vb = [x.astype(BF16) for x in v]
    st = [state_ref[j] for j in hs]
    stb = [x.astype(BF16) for x in st]

    nb = [(-jnp.where(strict, _dot_nt(am[j], bm[j]), 0.0)).astype(BF16) for j in hs]
    lk = [jnp.where(strict, _dot_nt(am[j], km[j]), 0.0).astype(BF16) for j in hs]
    x = [_dot_nt(a_abs[j], stb[j]) + dot(lk[j], vb[j]) for j in hs]
    x = [x[j] + dot(nb[j], x[j].astype(BF16)) for j in hs]
    p = 2
    while p < chunk:
        nb = [dot(nb[j], nb[j]).astype(BF16) for j in hs]
        x = [x[j] + dot(nb[j], x[j].astype(BF16)) for j in hs]
        p *= 2
    pb = [xj.astype(BF16) for xj in x]
    mk = [jnp.where(incl, _dot_nt(rm[j], km[j]), 0.0).astype(BF16) for j in hs]
    mb = [jnp.where(incl, _dot_nt(rm[j], bm[j]), 0.0).astype(BF16) for j in hs]
    y = [_dot_nt(r_abs[j], stb[j]) + dot(mk[j], vb[j]) - dot(mb[j], pb[j]) for j in hs]
    to_end = [jnp.exp(cum_end[j] - cum[j]) for j in hs]
    for j in hs:
        state_ref[j] = (st[j] * jnp.exp(cum_end[j]) + _dot_tn(vb[j], (kmod[j] * to_end[j]).astype(BF16))
                        - _dot_tn(pb[j], (kka[j] * to_end[j]).astype(BF16)))
    for j in hs:
        sl = sls[j]
        bonus = jnp.sum(r[j] * kmod[j] * rk_ref[:, sl], axis=-1, keepdims=True) * v[j]
        mean = jnp.mean(y[j], axis=-1, keepdims=True)
        yc = y[j] - mean
        var = jnp.mean(yc * yc, axis=-1, keepdims=True)
        yn = yc * lax.rsqrt(var + RW_LN_EPS) * lnw_ref[:, sl] + lnb_ref[:, sl]
        o_ref[0, :, sl] = ((yn + bonus) * g_ref[0, :, sl]).astype(o_ref.dtype)


def _rw_scan_tile_kernel(r_ref, k_ref, v_ref, a_ref, lw_ref, g_ref, kk_ref, ka_ref, rk_ref, lnw_ref, lnb_ref,
                         o_ref, state_ref, *, chunk, heads, n):
    @pl.when(pl.program_id(2) == 0)
    def _():
        state_ref[...] = jnp.zeros_like(state_ref)

    per = LANES // n
    tiles = range(heads // per)
    sub = range(per)
    ti = lax.broadcasted_iota(jnp.int32, (chunk, chunk), 0)
    si = lax.broadcasted_iota(jnp.int32, (chunk, chunk), 1)
    strict = ti > si
    incl = ti >= si
    lane_seg = lax.broadcasted_iota(jnp.int32, (1, LANES), 1) // n
    seg_is = [lane_seg == j for j in sub]
    same_head = (lax.broadcasted_iota(jnp.int32, (LANES, LANES), 0) // n
                 == lax.broadcasted_iota(jnp.int32, (LANES, LANES), 1) // n)
    dot = functools.partial(jnp.dot, preferred_element_type=F32)
    tile = lambda x, i: x[:, i * LANES:(i + 1) * LANES]

    def pick(vals):
        out = vals[-1]
        for j in range(per - 2, -1, -1):
            out = jnp.where(seg_is[j], vals[j], out)
        return out

    def seg_sum(x):
        return pick([jnp.sum(jnp.where(seg_is[j], x, 0.0), axis=-1, keepdims=True) for j in sub])

    r, k, v, a, lw = r_ref[0], k_ref[0], v_ref[0], a_ref[0], lw_ref[0]
    kk = k * kk_ref[...]
    kmod = k * (1.0 + (a - 1.0) * ka_ref[...])
    cum = _cumsum_rows(lw, chunk)
    cum_end = cum[chunk - 1:chunk, :]
    mid = cum[chunk // 2 - 1:chunk // 2, :]
    bonus_in = r * kmod * rk_ref[...]
    kk_t, bonus_t = [], []
    for i in tiles:
        kki = tile(kk, i)
        kk_t.append(kki / jnp.maximum(jnp.sqrt(seg_sum(kki * kki)), 1e-12))
        bonus_t.append(seg_sum(tile(bonus_in, i)) * tile(v, i))
    kk = jnp.concatenate(kk_t, axis=-1) if len(kk_t) > 1 else kk_t[0]
    kka = kk * a
    e_neg = jnp.exp(mid - cum)
    to_end = jnp.exp(cum_end - cum)
    am = (kk * jnp.exp(cum - lw - mid)).astype(BF16)
    bm = (kka * e_neg).astype(BF16)
    km = (kmod * e_neg).astype(BF16)
    rm = (r * jnp.exp(cum - mid)).astype(BF16)
    a_abs = (kk * jnp.exp(cum - lw)).astype(BF16)
    r_abs = (r * jnp.exp(cum)).astype(BF16)
    k_end = (kmod * to_end).astype(BF16)
    b_end = (kka * to_end).astype(BF16)
    vb = v.astype(BF16)
    st_decay = jnp.exp(cum_end)
    zero = jnp.zeros((), BF16)

    st = [state_ref[i] for i in tiles]
    stb = [s.astype(BF16) for s in st]
    am_h = [[jnp.where(seg_is[j], tile(am, i), zero) for j in sub] for i in tiles]
    rm_h = [[jnp.where(seg_is[j], tile(rm, i), zero) for j in sub] for i in tiles]
    nb = [[(-jnp.where(strict, _dot_nt(am_h[i][j], tile(bm, i)), 0.0)).astype(BF16) for j in sub] for i in tiles]
    lk = [[jnp.where(strict, _dot_nt(am_h[i][j], tile(km, i)), 0.0).astype(BF16) for j in sub] for i in tiles]
    x = [_dot_nt(tile(a_abs, i), stb[i]) + pick([dot(lk[i][j], tile(vb, i)) for j in sub]) for i in tiles]
    xb = [xi.astype(BF16) for xi in x]
    x = [x[i] + pick([dot(nb[i][j], xb[i]) for j in sub]) for i in tiles]
    p = 2
    while p < chunk:
        nb = [[dot(nb[i][j], nb[i][j]).astype(BF16) for j in sub] for i in tiles]
        xb = [xi.astype(BF16) for xi in x]
        x = [x[i] + pick([dot(nb[i][j], xb[i]) for j in sub]) for i in tiles]
        p *= 2
    pb = [xi.astype(BF16) for xi in x]
    mk = [[jnp.where(incl, _dot_nt(rm_h[i][j], tile(km, i)), 0.0).astype(BF16) for j in sub] for i in tiles]
    mb = [[jnp.where(incl, _dot_nt(rm_h[i][j], tile(bm, i)), 0.0).astype(BF16) for j in sub] for i in tiles]
    y = [_dot_nt(tile(r_abs, i), stb[i])
         + pick([dot(mk[i][j], tile(vb, i)) - dot(mb[i][j], pb[i]) for j in sub]) for i in tiles]
    for i in tiles:
        upd = _dot_tn(tile(vb, i), tile(k_end, i)) - _dot_tn(pb[i], tile(b_end, i))
        state_ref[i] = st[i] * tile(st_decay, i) + jnp.where(same_head, upd, 0.0)
    inv_n = 1.0 / n
    for i in tiles:
        cols = slice(i * LANES, (i + 1) * LANES)
        mean = seg_sum(y[i]) * inv_n
        yc = y[i] - mean
        var = seg_sum(yc * yc) * inv_n
        yn = yc * lax.rsqrt(var + RW_LN_EPS) * lnw_ref[:, cols] + lnb_ref[:, cols]
        o_ref[0, :, cols] = ((yn + bonus_t[i]) * g_ref[0, :, cols]).astype(o_ref.dtype)


def rw_scan(r, k, v, a, lw, g, k_k, k_a, r_k, ln_w, ln_b, *, n=RW_HEAD_DIM, chunk=RW_CHUNK, heads=8):
    bsz, s_len, d = r.shape
    chunk = min(chunk, s_len)
    heads = min(heads, d // n)
    hw = heads * n
    seq = pl.BlockSpec((1, chunk, hw), lambda b_, h_, c: (b_, c, h_))
    par = pl.BlockSpec((1, hw), lambda b_, h_, c: (0, h_))
    row = lambda t: t.reshape(1, d)
    assert hw % LANES == 0 and LANES % n == 0
    kern = functools.partial(_rw_scan_tile_kernel, chunk=chunk, heads=heads, n=n)
    return pl.pallas_call(
        kern,
        grid=(bsz, d // hw, s_len // chunk),
        in_specs=[seq] * 6 + [par] * 5,
        out_specs=seq,
        out_shape=jax.ShapeDtypeStruct(r.shape, BF16),
        scratch_shapes=[pltpu.VMEM((hw // LANES, LANES, LANES), F32)],
        compiler_params=_params("parallel", "parallel", "arbitrary"),
        name="rwkv7_scan",
    )(r, k, v, a, lw, g, row(k_k), row(k_a), row(r_k), row(ln_w), row(ln_b))


def rwkv7_mixer(u, h, w, bsz, s_len):
    t, d = u.shape
    xr, xw, xk, xv, xa, xg = [x.reshape(t, d) for x in rw_token_mix(u.reshape(bsz, s_len, d), w["rw_mu"])]
    r = matmul_ws(xr, [(w["rw_w_rkv"], (0, 0))], d, name="rw_r")
    k = matmul_ws(xk, [(w["rw_w_rkv"], (1, 0))], d, name="rw_k")
    v = matmul_ws(xv, [(w["rw_w_rkv"], (2, 0))], d, name="rw_v")
    row = lambda x: x.reshape(1, d)
    w_lo = matmul(xw, [(w["rw_w1"], 0)], w["rw_w1"].shape[1], epilogue=_ep_tanh, out_dtype=BF16, name="rw_w1")
    lw = matmul(w_lo, [(w["rw_w2"], 0)], d, epilogue=_ep_rw_logdecay, extras=[(row(w["rw_w0"]), "n")], name="rw_w2")
    a_lo = matmul(xa, [(w["rw_a1"], 0)], w["rw_a1"].shape[1], out_dtype=BF16, name="rw_a1")
    a = matmul(a_lo, [(w["rw_a2"], 0)], d, epilogue=_ep_bias_sigmoid, extras=[(row(w["rw_a0"]), "n")], name="rw_a2")
    g_lo = matmul(xg, [(w["rw_g1"], 0)], w["rw_g1"].shape[1], epilogue=_ep_sigmoid, out_dtype=BF16, name="rw_g1")
    g = matmul(g_lo, [(w["rw_g2"], 0)], d, name="rw_g2")
    shp = (bsz, s_len, d)
    y = rw_scan(r.reshape(shp), k.reshape(shp), v.reshape(shp), a.reshape(shp), lw.reshape(shp), g.reshape(shp),
                w["rw_k_k"], w["rw_k_a"], w["rw_r_k"], w["rw_ln_w"], w["rw_ln_b"])
    return matmul_ws(y.reshape(t, d), [(w["rw_w_out"], 0)], d, epilogue=_ep_residual, extras=[(h, "mn")],
                     name="rw_out")


NEG_BIG = -1e30


def _rope_kernel(x_ref, cc_ref, ss_ref, o_ref, *, n_q_slots, scale):
    x = x_ref[0]
    out = x * cc_ref[...] + pltpu.roll(x, x.shape[-1] // 2, 1) * ss_ref[...]
    out = out * jnp.where(pl.program_id(2) < n_q_slots, scale, 1.0)
    o_ref[0] = out.astype(o_ref.dtype)


def _rope_tables(pos, dim):
    inv = ROPE_THETA ** (-(jnp.arange(0, dim, 2, dtype=F32) / dim))
    ang = pos.astype(F32)[:, None] * inv[None, :]
    cos, sin = jnp.cos(ang), jnp.sin(ang)
    return jnp.concatenate([cos, cos], axis=-1), jnp.concatenate([-sin, sin], axis=-1)


def nsa_rope(proj, n_q_slots, k_slots, dh, scale, tb=512):
    bsz, s_len, _ = proj.shape
    tb = min(tb, s_len)
    cc, ss = _rope_tables(jnp.arange(s_len), dh)
    n_out = n_q_slots + len(k_slots)

    def in_slot(j):
        slot = j
        for idx, ks in enumerate(k_slots):
            slot = jnp.where(j == n_q_slots + idx, ks, slot)
        return slot

    return pl.pallas_call(
        functools.partial(_rope_kernel, n_q_slots=n_q_slots, scale=scale),
        grid=(bsz, s_len // tb, n_out),
        in_specs=[pl.BlockSpec((1, tb, dh), lambda b_, t, j: (b_, t, in_slot(j))),
                  pl.BlockSpec((tb, dh), lambda b_, t, j: (t, 0)),
                  pl.BlockSpec((tb, dh), lambda b_, t, j: (t, 0))],
        out_specs=pl.BlockSpec((1, tb, dh), lambda b_, t, j: (b_, t, j)),
        out_shape=jax.ShapeDtypeStruct((bsz, s_len, n_out * dh), BF16),
        compiler_params=_params("parallel", "parallel", "arbitrary"),
        name="nsa_rope",
    )(proj, cc, ss)


def _cmp_finish_kernel(z_ref, bias_ref, w2_ref, cc_ref, ss_ref, o_ref, *, hidden, rope):
    z = z_ref[0]
    nc = z.shape[0]
    nxt = pltpu.roll(z[:, hidden:], nc - 1, 0)
    hid = _silu(z[:, :hidden] + nxt + bias_ref[...])
    out = jnp.dot(hid.astype(BF16), w2_ref[...], preferred_element_type=F32)
    if rope:
        out = out * cc_ref[...] + pltpu.roll(out, out.shape[-1] // 2, 1) * ss_ref[...]
    o_ref[0] = out.astype(o_ref.dtype)


def nsa_compress(x, pos_emb, w1, w2, bsz, s_len, groups, dh, rope, transpose_out=False):
    stride, blk = NSA_CMP_STRIDE, NSA_CMP_BLOCK
    nc = s_len // stride
    hidden = w1.shape[-1]
    half = stride * dh
    x16 = jnp.transpose(x.reshape(bsz, nc, stride, groups, dh), (0, 3, 1, 2, 4)).reshape(bsz * groups * nc, half)
    w1f = w1.reshape(blk * dh, hidden)
    wcat = jnp.concatenate([w1f[:half], w1f[half:]], axis=1).astype(BF16)
    z = matmul(x16.astype(BF16), [(wcat, 0)], 2 * hidden, name="nsa_cmp_w1")
    bias = matmul(pos_emb.reshape(1, blk * dh).astype(BF16), [(w1f.astype(BF16), 0)], hidden, name="nsa_cmp_pos")
    cc, ss = _rope_tables(jnp.arange(nc) * stride + blk - 1, dh)
    if transpose_out:
        assert not rope
        return pl.pallas_call(
            functools.partial(_cmp_finish_t_kernel, hidden=hidden),
            grid=(bsz * groups,),
            in_specs=[pl.BlockSpec((1, nc, 2 * hidden), lambda i: (i, 0, 0)),
                      pl.BlockSpec((1, hidden), lambda i: (0, 0)),
                      pl.BlockSpec((dh, hidden), lambda i: (0, 0))],
            out_specs=pl.BlockSpec((1, dh, nc), lambda i: (i, 0, 0)),
            out_shape=jax.ShapeDtypeStruct((bsz * groups, dh, nc), BF16),
            compiler_params=_params("parallel"),
            name="nsa_cmp_finish_t",
        )(z.reshape(bsz * groups, nc, 2 * hidden), bias, w2.T.astype(BF16))
    return pl.pallas_call(
        functools.partial(_cmp_finish_kernel, hidden=hidden, rope=rope),
        grid=(bsz * groups,),
        in_specs=[pl.BlockSpec((1, nc, 2 * hidden), lambda i: (i, 0, 0)),
                  pl.BlockSpec((1, hidden), lambda i: (0, 0)),
                  pl.BlockSpec((hidden, dh), lambda i: (0, 0)),
                  pl.BlockSpec((nc, dh), lambda i: (0, 0)),
                  pl.BlockSpec((nc, dh), lambda i: (0, 0))],
        out_specs=pl.BlockSpec((1, nc, dh), lambda i: (i, 0, 0)),
        out_shape=jax.ShapeDtypeStruct((bsz * groups, nc, dh), BF16),
        compiler_params=_params("parallel"),
        name="nsa_cmp_finish",
    )(z.reshape(bsz * groups, nc, 2 * hidden), bias, w2.astype(BF16), cc, ss)


def _nsa_cmp_select_kernel(q_ref, kc_ref, vc_ref, ov_ref, oc_ref, sel_ref, *, tq, rep, dh, topn):
    qi = pl.program_id(2)
    kc = kc_ref[0]
    vc = vc_ref[0]
    nc = kc.shape[0]
    n_sel = sel_ref.shape[-1]
    t = qi * tq + lax.broadcasted_iota(jnp.int32, (tq, nc), 0)
    cmp_end = lax.broadcasted_iota(jnp.int32, (tq, nc), 1) * NSA_CMP_STRIDE + (NSA_CMP_BLOCK - 1)
    visible = cmp_end <= t
    psum = jnp.zeros((tq, nc), F32)
    for r in range(rep):
        s = jnp.where(visible, _dot_nt(q_ref[0, :, r * dh:(r + 1) * dh], kc), NEG_BIG)
        m = jnp.max(s, axis=-1, keepdims=True)
        e = jnp.where(visible, jnp.exp(s - m), 0.0)
        den = jnp.sum(e, axis=-1, keepdims=True)
        p = e / jnp.where(den > 0, den, 1.0)
        oc_ref[0, :, r * dh:(r + 1) * dh] = jnp.dot(p.astype(BF16), vc, preferred_element_type=F32)
        psum = psum + p
    imp = _dot_hi(psum, ov_ref[...])
    blk = lax.broadcasted_iota(jnp.int32, (tq, n_sel), 1)
    cur = (qi * tq + lax.broadcasted_iota(jnp.int32, (tq, n_sel), 0)) // NSA_SEL_BLOCK
    forced = (blk == 0) | (blk == cur) | (blk == cur - 1)
    imp = jnp.where(forced, NSA_FORCED_SCORE, imp)
    imp = jnp.where(blk > cur, -jnp.inf, imp)
    sel = jnp.zeros((tq, n_sel), F32)
    for _ in range(topn):
        m = jnp.max(imp, axis=-1, keepdims=True)
        first = jnp.min(jnp.where(imp == m, blk, n_sel), axis=-1, keepdims=True)
        hit = blk == first
        sel = jnp.where(hit, 1.0, sel)
        imp = jnp.where(hit, -jnp.inf, imp)
    sel_ref[0, 0] = sel


def _flash_step(q_scr, k, v, mask, m_ref, l_ref, acc_ref, rep, tq):
    kb = k.shape[0]
    s = _dot_nt(q_scr[...], k).reshape(rep, tq, kb)
    s = jnp.where(mask[None], s, NEG_BIG)
    m_old = m_ref[...].reshape(rep, tq, -1)[:, :, :1]
    m_new = jnp.maximum(m_old, jnp.max(s, axis=-1, keepdims=True))
    p = jnp.where(mask[None], jnp.exp(s - m_new), 0.0)
    alpha = jnp.exp(m_old - m_new)
    l_old = l_ref[...].reshape(rep, tq, -1)[:, :, :1]
    l_new = alpha * l_old + jnp.sum(p, axis=-1, keepdims=True)
    pv = jnp.dot(p.reshape(rep * tq, kb).astype(BF16), v, preferred_element_type=F32)
    acc_ref[...] = (alpha * acc_ref[...].reshape(rep, tq, -1)).reshape(rep * tq, -1) + pv
    m_ref[...] = jnp.broadcast_to(m_new, (rep, tq, m_ref.shape[-1])).reshape(m_ref.shape)
    l_ref[...] = jnp.broadcast_to(l_new, (rep, tq, l_ref.shape[-1])).reshape(l_ref.shape)


def _flash_init(q_ref, q_scr, m_ref, l_ref, acc_ref, rep, tq, dh):
    for r in range(rep):
        q_scr[r * tq:(r + 1) * tq, :] = q_ref[0, :, r * dh:(r + 1) * dh]
    m_ref[...] = jnp.full_like(m_ref, NEG_BIG)
    l_ref[...] = jnp.zeros_like(l_ref)
    acc_ref[...] = jnp.zeros_like(acc_ref)


def _flash_result(l_ref, acc_ref):
    l = l_ref[...][:, :1]
    return acc_ref[...] / jnp.where(l > 0, l, 1.0)


def _nsa_select_kernel(q_ref, k_ref, v_ref, sel_ref, o_ref, q_scr, m_ref, l_ref, acc_ref, *, tq, kb, rep, dh):
    qi = pl.program_id(2)
    kj = pl.program_id(3)

    @pl.when(kj == 0)
    def _():
        _flash_init(q_ref, q_scr, m_ref, l_ref, acc_ref, rep, tq, dh)

    @pl.when(kj * kb <= qi * tq + tq - 1)
    def _():
        sel = sel_ref[0, 0]
        blk = lax.broadcasted_iota(jnp.int32, sel.shape, 1)
        kpos = kj * kb + lax.broadcasted_iota(jnp.int32, (tq, kb), 1)
        t = qi * tq + lax.broadcasted_iota(jnp.int32, (tq, kb), 0)
        chosen = jnp.zeros((tq, kb), F32)
        for i in range(kb // NSA_SEL_BLOCK):
            col = jnp.sum(jnp.where(blk == kj * (kb // NSA_SEL_BLOCK) + i, sel, 0.0), axis=-1, keepdims=True)
            in_blk = (kpos - kj * kb) // NSA_SEL_BLOCK == i
            chosen = jnp.where(in_blk, col, chosen)
        mask = (chosen > 0) & (kpos <= t)
        _flash_step(q_scr, k_ref[0], v_ref[0].astype(BF16), mask, m_ref, l_ref, acc_ref, rep, tq)

    @pl.when(kj == pl.num_programs(3) - 1)
    def _():
        out = _flash_result(l_ref, acc_ref)
        for r in range(rep):
            o_ref[0, :, r * dh:(r + 1) * dh] = out[r * tq:(r + 1) * tq, :]


def _nsa_window_kernel(q_ref, k_ref, v_ref, oc_ref, os_ref, g_ref, o_ref, q_scr, m_ref, l_ref, acc_ref,
                       *, tq, kb, rep, dh, window, n_steps):
    qi = pl.program_id(2)
    w = pl.program_id(3)
    kblk = qi * (tq // kb) - (n_steps - tq // kb) + w

    @pl.when(w == 0)
    def _():
        _flash_init(q_ref, q_scr, m_ref, l_ref, acc_ref, rep, tq, dh)

    @pl.when(kblk >= 0)
    def _():
        kpos = kblk * kb + lax.broadcasted_iota(jnp.int32, (tq, kb), 1)
        t = qi * tq + lax.broadcasted_iota(jnp.int32, (tq, kb), 0)
        mask = (kpos <= t) & (kpos > t - window)
        _flash_step(q_scr, k_ref[0], v_ref[0].astype(BF16), mask, m_ref, l_ref, acc_ref, rep, tq)

    @pl.when(w == n_steps - 1)
    def _():
        out = _flash_result(l_ref, acc_ref)
        gates = g_ref[0, 0]
        for r in range(rep):
            sl = slice(r * dh, (r + 1) * dh)
            o = (gates[:, 3 * r:3 * r + 1] * oc_ref[0, :, sl] + gates[:, 3 * r + 1:3 * r + 2] * os_ref[0, :, sl]
                 + gates[:, 3 * r + 2:3 * r + 3] * out[r * tq:(r + 1) * tq, :])
            o_ref[0, :, sl] = o.astype(o_ref.dtype)


def nsa_mixer(u, h, w, bsz, s_len):
    t, d = u.shape
    dh, groups = NSA_HEAD_DIM, NSA_N_KV
    n_heads = d // dh
    rep = n_heads // groups
    kvw = groups * dh
    qw = n_heads * dh
    main_w = qw + 6 * kvw
    scale = dh ** -0.5
    tq = kb = min(128, s_len)
    nq = s_len // tq
    n_sel = s_len // NSA_SEL_BLOCK
    topn = min(NSA_TOPK, n_sel)
    w_in = w["nsa_w_in"]
    proj = matmul_ws(u, [(w_in, 0)], main_w, name="nsa_in").reshape(bsz, s_len, main_w)
    gates = matmul(u, [(w_in[:, main_w:].astype(BF16), 0)], w_in.shape[1] - main_w, epilogue=_ep_sigmoid,
                   name="nsa_gates")
    gates = jnp.transpose(gates.reshape(bsz, s_len, groups, rep * 3), (0, 2, 1, 3))
    slot = lambda j: (qw + j * kvw) // dh
    roped = nsa_rope(proj, n_heads, [slot(2) + g for g in range(groups)] + [slot(4) + g for g in range(groups)],
                     dh, scale)
    kc = nsa_compress(proj[..., qw:qw + kvw], w["nsa_cmp_pos_k"], w["nsa_cmp_k_w1"], w["nsa_cmp_k_w2"],
                      bsz, s_len, groups, dh, True)
    vc = nsa_compress(proj[..., qw + kvw:qw + 2 * kvw], w["nsa_cmp_pos_v"], w["nsa_cmp_v_w1"], w["nsa_cmp_v_w2"],
                      bsz, s_len, groups, dh, False)
    nc = kc.shape[1]
    cs = jnp.arange(nc)[:, None] * NSA_CMP_STRIDE
    ss = jnp.arange(n_sel)[None, :] * NSA_SEL_BLOCK
    overlap = jnp.clip(jnp.minimum(cs + NSA_CMP_BLOCK, ss + NSA_SEL_BLOCK) - jnp.maximum(cs, ss), 0, None)
    overlap = overlap.astype(F32) / NSA_CMP_BLOCK

    q_spec3 = pl.BlockSpec((1, tq, rep * dh), lambda b_, g, i: (b_, i, g))
    o_c, sel = pl.pallas_call(
        functools.partial(_nsa_cmp_select_kernel, tq=tq, rep=rep, dh=dh, topn=topn),
        grid=(bsz, groups, nq),
        in_specs=[q_spec3,
                  pl.BlockSpec((1, nc, dh), lambda b_, g, i: (b_ * groups + g, 0, 0)),
                  pl.BlockSpec((1, nc, dh), lambda b_, g, i: (b_ * groups + g, 0, 0)),
                  pl.BlockSpec((nc, n_sel), lambda b_, g, i: (0, 0))],
        out_specs=[q_spec3, pl.BlockSpec((1, 1, tq, n_sel), lambda b_, g, i: (b_, g, i, 0))],
        out_shape=[jax.ShapeDtypeStruct((bsz, s_len, qw), F32),
                   jax.ShapeDtypeStruct((bsz, groups, s_len, n_sel), F32)],
        compiler_params=_params("parallel", "parallel", "parallel"),
        name="nsa_cmp_select",
    )(roped, kc, vc, overlap)

    q_spec = pl.BlockSpec((1, tq, rep * dh), lambda b_, g, i, j: (b_, i, g))
    flash_scratch = [pltpu.VMEM((rep * tq, dh), BF16), pltpu.VMEM((rep * tq, LANES), F32),
                     pltpu.VMEM((rep * tq, LANES), F32), pltpu.VMEM((rep * tq, dh), F32)]
    last_kb = lambda i: (i * tq + tq - 1) // kb
    o_s = pl.pallas_call(
        functools.partial(_nsa_select_kernel, tq=tq, kb=kb, rep=rep, dh=dh),
        grid=(bsz, groups, nq, s_len // kb),
        in_specs=[q_spec,
                  pl.BlockSpec((1, kb, dh), lambda b_, g, i, j: (b_, jnp.minimum(j, last_kb(i)), n_heads + g)),
                  pl.BlockSpec((1, kb, dh), lambda b_, g, i, j: (b_, jnp.minimum(j, last_kb(i)), slot(3) + g)),
                  pl.BlockSpec((1, 1, tq, n_sel), lambda b_, g, i, j: (b_, g, i, 0))],
        out_specs=q_spec,
        out_shape=jax.ShapeDtypeStruct((bsz, s_len, qw), F32),
        scratch_shapes=flash_scratch,
        compiler_params=_params("parallel", "parallel", "parallel", "arbitrary"),
        name="nsa_select_attn",
    )(roped, roped, proj, sel)

    n_steps = NSA_WINDOW // kb + tq // kb
    win_blk = lambda i, j: jnp.maximum(i * (tq // kb) - (n_steps - tq // kb) + j, 0)
    o = pl.pallas_call(
        functools.partial(_nsa_window_kernel, tq=tq, kb=kb, rep=rep, dh=dh, window=NSA_WINDOW, n_steps=n_steps),
        grid=(bsz, groups, nq, n_steps),
        in_specs=[q_spec,
                  pl.BlockSpec((1, kb, dh), lambda b_, g, i, j: (b_, win_blk(i, j), n_heads + groups + g)),
                  pl.BlockSpec((1, kb, dh), lambda b_, g, i, j: (b_, win_blk(i, j), slot(5) + g)),
                  q_spec, q_spec,
                  pl.BlockSpec((1, 1, tq, rep * 3), lambda b_, g, i, j: (b_, g, i, 0))],
        out_specs=q_spec,
        out_shape=jax.ShapeDtypeStruct((bsz, s_len, qw), BF16),
        scratch_shapes=flash_scratch,
        compiler_params=_params("parallel", "parallel", "parallel", "arbitrary"),
        name="nsa_window_attn",
    )(roped, roped, proj, o_c, o_s, gates)
    return matmul_ws(o.reshape(t, qw), [(w["nsa_w_out"], 0)], d, epilogue=_ep_residual, extras=[(h, "mn")],
                     name="nsa_out")


def _rope_t_kernel(x_ref, cc_ref, ss_ref, o_ref, *, n_rope, scale):
    x = x_ref[0]
    roped = (x * cc_ref[...] + pltpu.roll(x, x.shape[-1] // 2, 1) * ss_ref[...]) * scale
    out = jnp.where(pl.program_id(2) < n_rope, roped, x)
    o_ref[0] = out.T.astype(o_ref.dtype)


def nsa_rope_t(proj, slots, n_rope, dh, scale, tb=512):
    bsz, s_len, _ = proj.shape
    tb = min(tb, s_len)
    cc, ss = _rope_tables(jnp.arange(s_len), dh)
    table = jnp.asarray(slots, jnp.int32)
    grid_spec = pltpu.PrefetchScalarGridSpec(
        num_scalar_prefetch=1,
        grid=(bsz, s_len // tb, len(slots)),
        in_specs=[pl.BlockSpec((1, tb, dh), lambda b_, t, j, tab: (b_, t, tab[j])),
                  pl.BlockSpec((tb, dh), lambda b_, t, j, tab: (t, 0)),
                  pl.BlockSpec((tb, dh), lambda b_, t, j, tab: (t, 0))],
        out_specs=pl.BlockSpec((1, dh, tb), lambda b_, t, j, tab: (b_, j, t)),
    )
    kern = lambda tab, x_ref, cc_ref, ss_ref, o_ref: _rope_t_kernel(x_ref, cc_ref, ss_ref, o_ref,
                                                                   n_rope=n_rope, scale=scale)
    return pl.pallas_call(
        kern,
        grid_spec=grid_spec,
        out_shape=jax.ShapeDtypeStruct((bsz, len(slots) * dh, s_len), BF16),
        compiler_params=_params("parallel", "parallel", "arbitrary"),
        name="nsa_rope_t",
    )(table, proj, cc, ss)


def _cmp_finish_t_kernel(z_ref, bias_ref, w2_ref, o_ref, *, hidden):
    z = z_ref[0]
    nc = z.shape[0]
    nxt = pltpu.roll(z[:, hidden:], nc - 1, 0)
    hid = _silu(z[:, :hidden] + nxt + bias_ref[...])
    o_ref[0] = _dot_nt(w2_ref[...], hid.astype(BF16)).astype(o_ref.dtype)


def _nsa_cmp_select_t_kernel(q_ref, kc_ref, vc_ref, ov_ref, oc_ref, sel_ref, *, tq, rep, dh, topn):
    qi = pl.program_id(2)
    kc = kc_ref[0]
    vct = vc_ref[0]
    nc = kc.shape[0]
    n_sel = sel_ref.shape[2]
    t = qi * tq + lax.broadcasted_iota(jnp.int32, (nc, tq), 1)
    cmp_end = lax.broadcasted_iota(jnp.int32, (nc, tq), 0) * NSA_CMP_STRIDE + (NSA_CMP_BLOCK - 1)
    visible = cmp_end <= t
    s = [jnp.where(visible, jnp.dot(kc, q_ref[0, r * dh:(r + 1) * dh, :], preferred_element_type=F32), NEG_BIG)
         for r in range(rep)]
    e = [jnp.where(visible, jnp.exp(x - jnp.max(x, axis=0, keepdims=True)), 0.0) for x in s]
    den = [jnp.sum(x, axis=0, keepdims=True) for x in e]
    p = [e[r] / jnp.where(den[r] > 0, den[r], 1.0) for r in range(rep)]
    for r in range(rep):
        oc_ref[0, r * dh:(r + 1) * dh, :] = jnp.dot(vct, p[r].astype(BF16), preferred_element_type=F32)
    psum = p[0]
    for r in range(1, rep):
        psum = psum + p[r]
    imp = _dot_hi(ov_ref[...], psum)
    blk = lax.broadcasted_iota(jnp.int32, (n_sel, tq), 0)
    cur = (qi * tq + lax.broadcasted_iota(jnp.int32, (n_sel, tq), 1)) // NSA_SEL_BLOCK
    forced = (blk == 0) | (blk == cur) | (blk == cur - 1)
    imp = jnp.where(forced, NSA_FORCED_SCORE, imp)
    imp = jnp.where(blk > cur, -jnp.inf, imp)
    sel = jnp.zeros((n_sel, tq), F32)
    for _ in range(topn):
        m = jnp.max(imp, axis=0, keepdims=True)
        first = jnp.min(jnp.where(imp == m, blk, n_sel), axis=0, keepdims=True)
        hit = blk == first
        sel = jnp.where(hit, 1.0, sel)
        imp = jnp.where(hit, -jnp.inf, imp)
    sel_ref[0, 0] = sel


def _flash_t_init(m_ref, l_ref, acc_ref):
    m_ref[...] = jnp.full_like(m_ref, NEG_BIG)
    l_ref[...] = jnp.zeros_like(l_ref)
    acc_ref[...] = jnp.zeros_like(acc_ref)


def _flash_t_step(q_ref, k, vt, mask, m_ref, l_ref, acc_ref, rep, dh):
    hs = range(rep)
    s = [jnp.where(mask, jnp.dot(k, q_ref[0, r * dh:(r + 1) * dh, :], preferred_element_type=F32), NEG_BIG)
         for r in hs]
    m_old = [m_ref[r] for r in hs]
    m_new = [jnp.maximum(m_old[r], jnp.max(s[r], axis=0, keepdims=True)) for r in hs]
    p = [jnp.exp(s[r] - m_new[r]) for r in hs]
    alpha = [jnp.exp(m_old[r] - m_new[r]) for r in hs]
    pv = [jnp.dot(vt, p[r].astype(BF16), preferred_element_type=F32) for r in hs]
    for r in hs:
        m_ref[r] = m_new[r]
        l_ref[r] = alpha[r] * l_ref[r] + jnp.sum(p[r], axis=0, keepdims=True)
        acc_ref[r] = acc_ref[r] * alpha[r] + pv[r]


def _nsa_select_t_kernel(qi_ref, kj_ref, q_ref, k_ref, vt_ref, sel_ref, o_ref, m_ref, l_ref, acc_ref,
                         *, tq, kb, rep, dh):
    pair = pl.program_id(2)
    qi = qi_ref[pair]
    kj = kj_ref[pair]

    @pl.when(kj == 0)
    def _():
        _flash_t_init(m_ref, l_ref, acc_ref)

    kpos = kj * kb + lax.broadcasted_iota(jnp.int32, (kb, tq), 0)
    t = qi * tq + lax.broadcasted_iota(jnp.int32, (kb, tq), 1)
    per = kb // NSA_SEL_BLOCK
    chosen = jnp.zeros((kb, tq), F32)
    for i in range(per):
        row = sel_ref[0, 0, pl.ds(kj * per + i, 1), :]
        chosen = jnp.where((kpos - kj * kb) // NSA_SEL_BLOCK == i, row, chosen)
    mask = (chosen > 0) & (kpos <= t)
    _flash_t_step(q_ref, k_ref[0], vt_ref[0], mask, m_ref, l_ref, acc_ref, rep, dh)

    @pl.when(kj * kb + kb > qi * tq + tq - 1)
    def _():
        for r in range(rep):
            l = l_ref[r]
            o_ref[0, r * dh:(r + 1) * dh, :] = acc_ref[r] / jnp.where(l > 0, l, 1.0)


def _nsa_window_t_kernel(q_ref, k_ref, vt_ref, oc_ref, os_ref, g_ref, o_ref, m_ref, l_ref, acc_ref,
                         *, tq, kb, rep, dh, window, n_steps):
    qi = pl.program_id(2)
    w = pl.program_id(3)
    kblk = qi * (tq // kb) - (n_steps - tq // kb) + w

    @pl.when(w == 0)
    def _():
        _flash_t_init(m_ref, l_ref, acc_ref)

    @pl.when(kblk >= 0)
    def _():
        kpos = kblk * kb + lax.broadcasted_iota(jnp.int32, (kb, tq), 0)
        t = qi * tq + lax.broadcasted_iota(jnp.int32, (kb, tq), 1)
        mask = (kpos <= t) & (kpos > t - window)
        _flash_t_step(q_ref, k_ref[0], vt_ref[0], mask, m_ref, l_ref, acc_ref, rep, dh)

    @pl.when(w == n_steps - 1)
    def _():
        gates = g_ref[0, 0]
        for r in range(rep):
            rows = slice(r * dh, (r + 1) * dh)
            l = l_ref[r]
            o_w = acc_ref[r] / jnp.where(l > 0, l, 1.0)
            o = (gates[3 * r:3 * r + 1, :] * oc_ref[0, rows, :] + gates[3 * r + 1:3 * r + 2, :] * os_ref[0, rows, :]
                 + gates[3 * r + 2:3 * r + 3, :] * o_w)
            o_ref[0, :, rows] = o.T.astype(o_ref.dtype)


def nsa_mixer_t(u, h, w, bsz, s_len):
    t, d = u.shape
    dh, groups = NSA_HEAD_DIM, NSA_N_KV
    n_heads = d // dh
    rep = n_heads // groups
    kvw = groups * dh
    qw = n_heads * dh
    main_w = qw + 6 * kvw
    scale = dh ** -0.5
    tq = kb = min(128, s_len)
    nq = s_len // tq
    n_sel = s_len // NSA_SEL_BLOCK
    topn = min(NSA_TOPK, n_sel)
    w_in = w["nsa_w_in"]
    proj = matmul_ws(u, [(w_in, 0)], main_w, name="nsa_in").reshape(bsz, s_len, main_w)
    gates = matmul(u, [(w_in[:, main_w:].astype(BF16), 0)], w_in.shape[1] - main_w, epilogue=_ep_sigmoid,
                   name="nsa_gates")
    gates = jnp.transpose(gates.reshape(bsz, s_len, groups, rep * 3), (0, 2, 3, 1))
    slot = lambda j: (qw + j * kvw) // dh
    qvt = nsa_rope_t(proj, list(range(n_heads)) + [slot(3) + g for g in range(groups)]
                     + [slot(5) + g for g in range(groups)], n_heads, dh, scale)
    k_rot = nsa_rope(proj, 0, [slot(2) + g for g in range(groups)] + [slot(4) + g for g in range(groups)], dh, 1.0)
    kc = nsa_compress(proj[..., qw:qw + kvw], w["nsa_cmp_pos_k"], w["nsa_cmp_k_w1"], w["nsa_cmp_k_w2"],
                      bsz, s_len, groups, dh, True)
    vct = nsa_compress(proj[..., qw + kvw:qw + 2 * kvw], w["nsa_cmp_pos_v"], w["nsa_cmp_v_w1"], w["nsa_cmp_v_w2"],
                       bsz, s_len, groups, dh, False, transpose_out=True)
    nc = kc.shape[1]
    cs = jnp.arange(nc)[None, :] * NSA_CMP_STRIDE
    ss = jnp.arange(n_sel)[:, None] * NSA_SEL_BLOCK
    overlap_t = jnp.clip(jnp.minimum(cs + NSA_CMP_BLOCK, ss + NSA_SEL_BLOCK) - jnp.maximum(cs, ss), 0, None)
    overlap_t = overlap_t.astype(F32) / NSA_CMP_BLOCK

    qt_spec3 = pl.BlockSpec((1, rep * dh, tq), lambda b_, g, i: (b_, g, i))
    o_c, sel = pl.pallas_call(
        functools.partial(_nsa_cmp_select_t_kernel, tq=tq, rep=rep, dh=dh, topn=topn),
        grid=(bsz, groups, nq),
        in_specs=[qt_spec3,
                  pl.BlockSpec((1, nc, dh), lambda b_, g, i: (b_ * groups + g, 0, 0)),
                  pl.BlockSpec((1, dh, nc), lambda b_, g, i: (b_ * groups + g, 0, 0)),
                  pl.BlockSpec((n_sel, nc), lambda b_, g, i: (0, 0))],
        out_specs=[qt_spec3, pl.BlockSpec((1, 1, n_sel, tq), lambda b_, g, i: (b_, g, 0, i))],
        out_shape=[jax.ShapeDtypeStruct((bsz, qw, s_len), F32),
                   jax.ShapeDtypeStruct((bsz, groups, n_sel, s_len), F32)],
        compiler_params=_params("parallel", "parallel", "parallel"),
        name="nsa_cmp_select",
    )(qvt, kc, vct, overlap_t)

    flash_scratch = lambda n: [pltpu.VMEM((rep, 1, n), F32), pltpu.VMEM((rep, 1, n), F32),
                               pltpu.VMEM((rep, dh, n), F32)]
    tqs = min(2 * tq, s_len)
    pairs = [(i, j) for i in range(s_len // tqs) for j in range((i * tqs + tqs - 1) // kb + 1)]
    qi_of = jnp.asarray([pr[0] for pr in pairs], jnp.int32)
    kj_of = jnp.asarray([pr[1] for pr in pairs], jnp.int32)
    o_s = pl.pallas_call(
        functools.partial(_nsa_select_t_kernel, tq=tqs, kb=kb, rep=rep, dh=dh),
        grid_spec=pltpu.PrefetchScalarGridSpec(
            num_scalar_prefetch=2,
            grid=(bsz, groups, len(pairs)),
            in_specs=[pl.BlockSpec((1, rep * dh, tqs), lambda b_, g, pr, qi, kj: (b_, g, qi[pr])),
                      pl.BlockSpec((1, kb, dh), lambda b_, g, pr, qi, kj: (b_, kj[pr], g)),
                      pl.BlockSpec((1, dh, kb), lambda b_, g, pr, qi, kj: (b_, n_heads + g, kj[pr])),
                      pl.BlockSpec((1, 1, n_sel, tqs), lambda b_, g, pr, qi, kj: (b_, g, 0, qi[pr]))],
            out_specs=pl.BlockSpec((1, rep * dh, tqs), lambda b_, g, pr, qi, kj: (b_, g, qi[pr])),
            scratch_shapes=flash_scratch(tqs)),
        out_shape=jax.ShapeDtypeStruct((bsz, qw, s_len), F32),
        compiler_params=_params("parallel", "parallel", "arbitrary"),
        name="nsa_select_attn",
    )(qi_of, kj_of, qvt, k_rot, qvt, sel)

    n_steps = NSA_WINDOW // kb + tq // kb
    win_blk = lambda i, j: jnp.maximum(i * (tq // kb) - (n_steps - tq // kb) + j, 0)
    qt_spec = pl.BlockSpec((1, rep * dh, tq), lambda b_, g, i, j: (b_, g, i))
    o = pl.pallas_call(
        functools.partial(_nsa_window_t_kernel, tq=tq, kb=kb, rep=rep, dh=dh, window=NSA_WINDOW, n_steps=n_steps),
        grid=(bsz, groups, nq, n_steps),
        in_specs=[qt_spec,
                  pl.BlockSpec((1, kb, dh), lambda b_, g, i, j: (b_, win_blk(i, j), groups + g)),
                  pl.BlockSpec((1, dh, kb), lambda b_, g, i, j: (b_, n_heads + groups + g, win_blk(i, j))),
                  qt_spec, qt_spec,
                  pl.BlockSpec((1, 1, rep * 3, tq), lambda b_, g, i, j: (b_, g, 0, i))],
        out_specs=pl.BlockSpec((1, tq, rep * dh), lambda b_, g, i, j: (b_, i, g)),
        out_shape=jax.ShapeDtypeStruct((bsz, s_len, qw), BF16),
        scratch_shapes=flash_scratch(tq),
        compiler_params=_params("parallel", "parallel", "parallel", "arbitrary"),
        name="nsa_window_attn",
    )(qvt, k_rot, qvt, o_c, o_s, gates)
    return matmul_ws(o.reshape(t, qw), [(w["nsa_w_out"], 0)], d, epilogue=_ep_residual, extras=[(h, "mn")],
                     name="nsa_out")


_MATMUL_WEIGHTS = ("pl_proj", "rw_w1", "rw_w2", "rw_a1", "rw_a2", "rw_g1", "rw_g2", "moe1_w_out", "moe3_w_out")


def kernel(x, p, norm_mix, norm_ffn, norm_pl, pl_proj, pl_gate, norm_final, mb_w_in, mb_conv_w, mb_conv_b, mb_dt_bias, mb_a_log, mb_d_skip, mb_norm_w, mb_w_out, nsa_w_in, nsa_cmp_pos_k, nsa_cmp_pos_v, nsa_cmp_k_w1, nsa_cmp_k_w2, nsa_cmp_v_w1, nsa_cmp_v_w2, nsa_w_out, hg_w_in, hg_lb_logits, hg_norm_w, hg_w_out, rw_mu, rw_w_rkv, rw_w0, rw_w1, rw_w2, rw_a0, rw_a1, rw_a2, rw_g1, rw_g2, rw_k_k, rw_k_a, rw_r_k, rw_ln_w, rw_ln_b, rw_w_out, ffn0_w_in, ffn0_w_out, moe1_router, moe1_w_in, moe1_w_out, ffn2_w_in, ffn2_w_out, moe3_router, moe3_w_in, moe3_w_out):
    w = dict(locals())
    for name in _MATMUL_WEIGHTS:
        w[name] = w[name].astype(BF16)
    bsz, s_len, d = x.shape
    depth = p.shape[0]
    t = bsz * s_len
    lb_all = jax.nn.softmax(hg_lb_logits.astype(F32), axis=0)
    lb_all = jnp.cumsum(lb_all, axis=0) - lb_all[0]
    dense = [(w["ffn0_w_in"], w["ffn0_w_out"]), (w["ffn2_w_in"], w["ffn2_w_out"])]
    moe = [(moe1_router, w["moe1_w_in"], w["moe1_w_out"]), (moe3_router, w["moe3_w_in"], w["moe3_w_out"])]
    p_bf = p.reshape(depth, t, p.shape[-1]).astype(BF16)
    h = x.reshape(t, d)
    for i in range(depth):
        kind = i % 4
        if kind == 0:
            h = mamba2_mixer(rmsnorm(h, norm_mix[i]), h, w, bsz, s_len)
        elif kind == 1:
            h = nsa_mixer_t(rmsnorm(h, norm_mix[i]), h, w, bsz, s_len)
        elif kind == 2:
            h = hgrn2_mixer(rmsnorm(h, norm_mix[i]), h, w, lb_all[i], bsz, s_len)
        else:
            h = rwkv7_mixer(rmsnorm(h, norm_mix[i], out_dtype=F32), h, w, bsz, s_len)
        v = rmsnorm(h, norm_ffn[i])
        if i % 2 == 0:
            h = dense_ffn(v, h, *dense[i // 2])
        else:
            h = moe_ffn_routed(v, h, *moe[i // 2])
        h = ple_gate(h, p_bf[i], norm_pl[i], w["pl_proj"][i], pl_gate, i)
    return rmsnorm(h, norm_final, out_dtype=F32).reshape(bsz, s_len, d)
```

```python
import functools
import math

import jax
import jax.numpy as jnp
from jax import lax
from jax.experimental import pallas as pl
from jax.experimental.pallas import tpu as pltpu

F32 = jnp.float32
BF16 = jnp.bfloat16

NORM_EPS = 1e-6
ROPE_THETA = 10000.0

V7X_VMEM_BYTES = 64 * 1024 * 1024
VMEM_LIMIT_BYTES = V7X_VMEM_BYTES - 8 * 1024 * 1024
LANES = 128

MB_HEAD_DIM = 64
MB_N_GROUPS = 8
MB_D_STATE = 128
MB_CONV = 4
MB_CHUNK = 128

NSA_HEAD_DIM = 128
NSA_N_KV = 4
NSA_CMP_BLOCK = 32
NSA_CMP_STRIDE = 16
NSA_SEL_BLOCK = 64
NSA_TOPK = 16
NSA_WINDOW = 512
NSA_FORCED_SCORE = 1e9

HG_HEAD_DIM = 128
HG_CHUNK = 32

RW_HEAD_DIM = 64
RW_LN_EPS = 64e-5
RW_CHUNK = 128

MOE_TOPK = 2


def _params(*semantics):
    return pltpu.CompilerParams(dimension_semantics=semantics, vmem_limit_bytes=VMEM_LIMIT_BYTES)


def _pick(n, target):
    if n <= target:
        return n
    for c in range(target, 0, -1):
        if n % c == 0:
            return c
    return n


def _silu(x):
    return x * jax.nn.sigmoid(x)


def _rmsnorm_kernel(x_ref, g_ref, o_ref):
    x = x_ref[...]
    ms = jnp.mean(x * x, axis=-1, keepdims=True)
    o_ref[...] = (x * lax.rsqrt(ms + NORM_EPS) * g_ref[...]).astype(o_ref.dtype)


def rmsnorm(x, gain, out_dtype=BF16, name="rmsnorm"):
    m, d = x.shape
    bm = _pick(m, 256)
    return pl.pallas_call(
        _rmsnorm_kernel,
        grid=(m // bm,),
        in_specs=[pl.BlockSpec((bm, d), lambda i: (i, 0)), pl.BlockSpec((1, d), lambda i: (0, 0))],
        out_specs=pl.BlockSpec((bm, d), lambda i: (i, 0)),
        out_shape=jax.ShapeDtypeStruct((m, d), out_dtype),
        compiler_params=_params("parallel"),
        name=name,
    )(x, gain.reshape(1, d).astype(F32))


def _mm_kernel(*refs, n_w, n_extra, nk, epilogue):
    x_ref = refs[0]
    w_refs = refs[1:1 + n_w]
    e_refs = refs[1 + n_w:1 + n_w + n_extra]
    o_ref = refs[1 + n_w + n_extra]
    acc_refs = refs[2 + n_w + n_extra:]
    x = x_ref[...]
    if nk == 1:
        accs = [jnp.dot(x, w[...], preferred_element_type=F32) for w in w_refs]
        o_ref[...] = epilogue(accs, [e[...] for e in e_refs]).astype(o_ref.dtype)
        return
    k = pl.program_id(2)

    @pl.when(k == 0)
    def _():
        for a in acc_refs:
            a[...] = jnp.zeros_like(a)

    for a, w in zip(acc_refs, w_refs):
        a[...] += jnp.dot(x, w[...], preferred_element_type=F32)

    @pl.when(k == nk - 1)
    def _():
        o_ref[...] = epilogue([a[...] for a in acc_refs], [e[...] for e in e_refs]).astype(o_ref.dtype)


def _first(accs, extras):
    return accs[0]


def matmul(x, ws, n_out, *, epilogue=_first, extras=(), out_dtype=F32, bm=1024, bn=512, bk=None, name="matmul"):
    m, kdim = x.shape
    bm = _pick(m, bm)
    bn = _pick(n_out, bn)
    if bk is None:
        bk = kdim if kdim <= 4096 else _pick(kdim, 4096)
    nk = kdim // bk
    assert kdim % bk == 0 and m % bm == 0 and n_out % bn == 0
    in_specs = [pl.BlockSpec((bm, bk), lambda i, j, k: (i, k))]
    args = [x]
    for w, off in ws:
        assert off % bn == 0 and w.shape[0] == kdim
        in_specs.append(pl.BlockSpec((bk, bn), functools.partial(lambda i, j, k, o: (k, j + o), o=off // bn)))
        args.append(w)
    for arr, kind in extras:
        if kind == "mn":
            in_specs.append(pl.BlockSpec((bm, bn), lambda i, j, k: (i, j)))
        elif kind == "m":
            in_specs.append(pl.BlockSpec((bm, arr.shape[1]), lambda i, j, k: (i, 0)))
        elif kind == "kn":
            in_specs.append(pl.BlockSpec((arr.shape[0], bn), lambda i, j, k: (0, j)))
        else:
            in_specs.append(pl.BlockSpec((1, bn), lambda i, j, k: (0, j)))
        args.append(arr)
    scratch = [pltpu.VMEM((bm, bn), F32) for _ in ws] if nk > 1 else []
    kern = functools.partial(_mm_kernel, n_w=len(ws), n_extra=len(extras), nk=nk, epilogue=epilogue)
    return pl.pallas_call(
        kern,
        grid=(m // bm, n_out // bn, nk),
        in_specs=in_specs,
        out_specs=pl.BlockSpec((bm, bn), lambda i, j, k: (i, j)),
        out_shape=jax.ShapeDtypeStruct((m, n_out), out_dtype),
        scratch_shapes=scratch,
        compiler_params=_params("parallel", "parallel", "arbitrary"),
        name=name,
    )(*args)


def _mm_ws_kernel(*refs, n_w, n_extra, epilogue):
    x_ref = refs[0]
    w_refs = refs[1:1 + n_w]
    e_refs = refs[1 + n_w:1 + n_w + n_extra]
    o_ref = refs[1 + n_w + n_extra]
    wb_refs = refs[2 + n_w + n_extra:]

    @pl.when(pl.program_id(1) == 0)
    def _():
        for w, wb in zip(w_refs, wb_refs):
            wb[...] = w[...].reshape(wb.shape).astype(BF16)

    x = x_ref[...]
    accs = [jnp.dot(x, wb[...], preferred_element_type=F32) for wb in wb_refs]
    o_ref[...] = epilogue(accs, [e[...] for e in e_refs]).astype(o_ref.dtype)


def matmul_ws(x, ws, n_out, *, epilogue=_first, extras=(), out_dtype=F32, bm=1024, bn=512, w_buffers=2,
              name="matmul_ws"):
    m, kdim = x.shape
    bm = _pick(m, bm)
    bn = _pick(n_out, bn)
    assert m % bm == 0 and n_out % bn == 0
    mode = {} if w_buffers == 2 else {"pipeline_mode": pl.Buffered(w_buffers)}
    in_specs = [pl.BlockSpec((bm, kdim), lambda j, i: (i, 0))]
    args = [x]
    for w, off in ws:
        if w.ndim == 3:
            e, o = off
            assert o % bn == 0 and w.shape[1] == kdim
            in_specs.append(pl.BlockSpec((1, kdim, bn), functools.partial(lambda j, i, e_, o_: (e_, 0, j + o_),
                                                                          e_=e, o_=o // bn), **mode))
        else:
            assert off % bn == 0 and w.shape[0] == kdim
            in_specs.append(pl.BlockSpec((kdim, bn), functools.partial(lambda j, i, o_: (0, j + o_), o_=off // bn),
                                         **mode))
        args.append(w)
    for arr, kind in extras:
        if kind == "mn":
            in_specs.append(pl.BlockSpec((bm, bn), lambda j, i: (i, j)))
        elif kind == "m":
            in_specs.append(pl.BlockSpec((bm, arr.shape[1]), lambda j, i: (i, 0)))
        elif kind == "kn":
            in_specs.append(pl.BlockSpec((arr.shape[0], bn), lambda j, i: (0, j)))
        else:
            in_specs.append(pl.BlockSpec((1, bn), lambda j, i: (0, j)))
        args.append(arr)
    kern = functools.partial(_mm_ws_kernel, n_w=len(ws), n_extra=len(extras), epilogue=epilogue)
    return pl.pallas_call(
        kern,
        grid=(n_out // bn, m // bm),
        in_specs=in_specs,
        out_specs=pl.BlockSpec((bm, bn), lambda j, i: (i, j)),
        out_shape=jax.ShapeDtypeStruct((m, n_out), out_dtype),
        scratch_shapes=[pltpu.VMEM((kdim, bn), BF16) for _ in ws],
        compiler_params=_params("parallel", "arbitrary"),
        name=name,
    )(*args)


def _ep_residual(accs, extras):
    return extras[0] + accs[0]


def _ep_swiglu(accs, extras):
    return _silu(accs[0]) * accs[1]


def _ep_bias(accs, extras):
    return accs[0] + extras[0]


def _ep_tanh(accs, extras):
    return jnp.tanh(accs[0])


def _ep_sigmoid(accs, extras):
    return jax.nn.sigmoid(accs[0])


def _ep_bias_sigmoid(accs, extras):
    return jax.nn.sigmoid(accs[0] + extras[0])


def _ep_rw_logdecay(accs, extras):
    w = -jax.nn.softplus(-(accs[0] + extras[0])) - 0.5
    return -jnp.exp(w)


def _ep_ple_gate(accs, extras):
    return extras[0] + jnp.dot(extras[1], extras[2], preferred_element_type=F32) * jax.nn.sigmoid(accs[0])


def _conv_silu_kernel(x_ref, w_ref, b_ref, o_ref, *, k_width):
    x = x_ref[0]
    row = lax.broadcasted_iota(jnp.int32, x.shape, 0)
    y = b_ref[...] + w_ref[k_width - 1:k_width, :] * x
    for j in range(k_width - 1):
        shift = k_width - 1 - j
        xs = jnp.where(row >= shift, pltpu.roll(x, shift, 0), 0.0)
        y = y + w_ref[j:j + 1, :] * xs
    o_ref[0] = _silu(y)


def conv_silu(x, w, b):
    bsz, s_len, c = x.shape
    cb = _pick(c, 256)
    k_width = w.shape[0]
    return pl.pallas_call(
        functools.partial(_conv_silu_kernel, k_width=k_width),
        grid=(bsz, c // cb),
        in_specs=[pl.BlockSpec((1, s_len, cb), lambda b_, j: (b_, 0, j)),
                  pl.BlockSpec((k_width, cb), lambda b_, j: (0, j)),
                  pl.BlockSpec((1, cb), lambda b_, j: (0, j))],
        out_specs=pl.BlockSpec((1, s_len, cb), lambda b_, j: (b_, 0, j)),
        out_shape=jax.ShapeDtypeStruct(x.shape, F32),
        compiler_params=_params("parallel", "parallel"),
        name="mamba_conv_silu",
    )(x, w, b.reshape(1, c))


def _cumsum_rows(x, n):
    row = lax.broadcasted_iota(jnp.int32, x.shape, 0)
    s = 1
    while s < n:
        x = x + jnp.where(row >= s, pltpu.roll(x, s, 0), 0.0)
        s *= 2
    return x


def _cumsum_lanes(x, n):
    col = lax.broadcasted_iota(jnp.int32, x.shape, 1)
    s = 1
    while s < n:
        x = x + jnp.where(col >= s, pltpu.roll(x, s, 1), 0.0)
        s *= 2
    return x


def _dot_nt(a, b):
    return lax.dot_general(a, b, (((1,), (1,)), ((), ())), preferred_element_type=F32)


def _dot_tn(a, b):
    return lax.dot_general(a, b, (((0,), (0,)), ((), ())), preferred_element_type=F32)


def _ssd_kernel(xs_ref, b_ref, c_ref, z_ref, dt_ref, dtt_ref, bias_r_ref, bias_c_ref, alog_r_ref, alog_c_ref,
                dskip_ref, normw_ref, o_ref, state_ref, y_ref, *, chunk, heads, p_dim):
    @pl.when(pl.program_id(2) == 0)
    def _():
        state_ref[...] = jnp.zeros_like(state_ref)

    dt = jax.nn.softplus(dt_ref[0, 0] + bias_r_ref[0])
    dtt = jax.nn.softplus(dtt_ref[0, 0] + bias_c_ref[0])
    a_cum = _cumsum_rows(dt * -jnp.exp(alog_r_ref[0]), chunk)
    a_cum_t = _cumsum_lanes(dtt * -jnp.exp(alog_c_ref[0]), chunk)
    xs = xs_ref[0]
    bmat = b_ref[0]
    cmat = c_ref[0].astype(BF16)
    cb = _dot_nt(cmat, bmat.astype(BF16))
    b_t = bmat.T.astype(BF16)
    li = lax.broadcasted_iota(jnp.int32, (chunk, chunk), 0)
    si = lax.broadcasted_iota(jnp.int32, (chunk, chunk), 1)
    causal = li >= si
    per = LANES // p_dim
    lane_seg = lax.broadcasted_iota(jnp.int32, (1, LANES), 1) // p_dim

    def pick(vals):
        out = vals[-1]
        for i in range(per - 2, -1, -1):
            out = jnp.where(lane_seg == i, vals[i], out)
        return out

    dot = functools.partial(jnp.dot, preferred_element_type=F32)
    es = range(heads)
    tiles = range(heads // per)
    head_row = lax.broadcasted_iota(jnp.int32, (heads, heads * LANES), 0)
    to_tile = jnp.where(lax.broadcasted_iota(jnp.int32, (heads, heads * LANES), 1) // LANES == head_row, 1.0, 0.0)
    cum_t = _dot_hi(a_cum, to_tile)
    of = lambda vals, i: [vals[i * per + j] for j in range(per)]
    tile = lambda x, i: x[:, i * LANES:(i + 1) * LANES]
    cum_c = jnp.concatenate([pick([tile(cum_t, e) for e in of(es, i)]) for i in tiles], axis=-1)
    dt_c = jnp.concatenate([pick([dt[:, e:e + 1] for e in of(es, i)]) for i in tiles], axis=-1)
    last_c = cum_c[chunk - 1:chunk, :]
    m = [(cb * jnp.exp(jnp.where(causal, tile(cum_t, e) - a_cum_t[e:e + 1, :], -jnp.inf))).astype(BF16) for e in es]
    xdt = xs * dt_c
    xdt_b = xdt.astype(BF16)
    xend_b = (xdt * jnp.exp(last_c - cum_c)).astype(BF16)
    grow = jnp.exp(cum_c)
    st_decay = jnp.exp(last_c)
    st = [state_ref[i] for i in tiles]
    y_in = [pick([dot(m[e], tile(xdt_b, i)) for e in of(es, i)]) for i in tiles]
    y_st = [dot(cmat, st[i].astype(BF16)) * tile(grow, i) for i in tiles]
    for i in tiles:
        state_ref[i] = st[i] * tile(st_decay, i) + dot(b_t, tile(xend_b, i))
        y_ref[:, i * LANES:(i + 1) * LANES] = y_in[i] + y_st[i]
    y = y_ref[...] + xs * dskip_ref[...]
    y = y * _silu(z_ref[0])
    ms = jnp.mean(y * y, axis=-1, keepdims=True)
    o_ref[0] = (y * lax.rsqrt(ms + NORM_EPS) * normw_ref[...]).astype(o_ref.dtype)


def ssd_scan(xbc, z, dt, dt_bias, a_log, d_skip, norm_w, *, chunk=MB_CHUNK):
    bsz, s_len, d_inner = z.shape
    n_heads = dt.shape[-1]
    n_state = MB_D_STATE
    groups = (xbc.shape[-1] - d_inner) // (2 * n_state)
    heads = n_heads // groups
    p_dim = d_inner // n_heads
    gw = heads * p_dim
    assert gw % LANES == 0 and d_inner % n_state == 0
    chunk = min(chunk, s_len)
    nc = s_len // chunk
    b_off = d_inner // n_state
    c_off = b_off + groups
    dt_g = jnp.transpose(dt.reshape(bsz, s_len, groups, heads), (0, 2, 1, 3))
    dt_gt = jnp.transpose(dt_g, (0, 1, 3, 2))
    kern = functools.partial(_ssd_kernel, chunk=chunk, heads=heads, p_dim=p_dim)
    per_group = lambda b_, g, c: (g, 0, 0)
    return pl.pallas_call(
        kern,
        grid=(bsz, groups, nc),
        in_specs=[pl.BlockSpec((1, chunk, gw), lambda b_, g, c: (b_, c, g)),
                  pl.BlockSpec((1, chunk, n_state), lambda b_, g, c: (b_, c, b_off + g)),
                  pl.BlockSpec((1, chunk, n_state), lambda b_, g, c: (b_, c, c_off + g)),
                  pl.BlockSpec((1, chunk, gw), lambda b_, g, c: (b_, c, g)),
                  pl.BlockSpec((1, 1, chunk, heads), lambda b_, g, c: (b_, g, c, 0)),
                  pl.BlockSpec((1, 1, heads, chunk), lambda b_, g, c: (b_, g, 0, c)),
                  pl.BlockSpec((1, 1, heads), per_group),
                  pl.BlockSpec((1, heads, 1), per_group),
                  pl.BlockSpec((1, 1, heads), per_group),
                  pl.BlockSpec((1, heads, 1), per_group),
                  pl.BlockSpec((1, gw), lambda b_, g, c: (0, g)),
                  pl.BlockSpec((1, gw), lambda b_, g, c: (0, g))],
        out_specs=pl.BlockSpec((1, chunk, gw), lambda b_, g, c: (b_, c, g)),
        out_shape=jax.ShapeDtypeStruct(z.shape, BF16),
        scratch_shapes=[pltpu.VMEM((gw // LANES, n_state, LANES), F32), pltpu.VMEM((chunk, gw), F32)],
        compiler_params=_params("parallel", "parallel", "arbitrary"),
        name="mamba_ssd",
    )(xbc, xbc, xbc, z, dt_g, dt_gt,
      dt_bias.reshape(groups, 1, heads), dt_bias.reshape(groups, heads, 1),
      a_log.reshape(groups, 1, heads), a_log.reshape(groups, heads, 1),
      jnp.repeat(d_skip, p_dim).reshape(1, d_inner), norm_w.reshape(1, d_inner))


def mamba2_mixer(u, h, w, bsz, s_len):
    d_inner = w["mb_w_out"].shape[0]
    n_heads = w["mb_dt_bias"].shape[0]
    w_in = w["mb_w_in"]
    xbc_w = w_in.shape[1] - d_inner - n_heads
    z = matmul_ws(u, [(w_in, 0)], d_inner, name="mb_in_z")
    xbc = matmul_ws(u, [(w_in, d_inner)], xbc_w, name="mb_in_xbc")
    dt = matmul_ws(u, [(w_in, d_inner + xbc_w)], n_heads, name="mb_in_dt")
    xbc = conv_silu(xbc.reshape(bsz, s_len, xbc_w), w["mb_conv_w"], w["mb_conv_b"])
    y = ssd_scan(xbc, z.reshape(bsz, s_len, d_inner), dt.reshape(bsz, s_len, n_heads),
                 w["mb_dt_bias"], w["mb_a_log"], w["mb_d_skip"], w["mb_norm_w"])
    return matmul_ws(y.reshape(bsz * s_len, d_inner), [(w["mb_w_out"], 0)], h.shape[1],
                     epilogue=_ep_residual, extras=[(h, "mn")], bm=512, w_buffers=1, name="mb_out")


def _seg_cumsum_rows(x, seg, reverse=False):
    n = x.shape[0]
    pos = lax.broadcasted_iota(jnp.int32, x.shape, 0) % seg
    s = 1
    while s < seg:
        if reverse:
            x = x + jnp.where(pos < seg - s, pltpu.roll(x, n - s, 0), 0.0)
        else:
            x = x + jnp.where(pos >= s, pltpu.roll(x, s, 0), 0.0)
        s *= 2
    return x


def _hgrn_kernel(q_ref, f_ref, i_ref, g_ref, lb_ref, nw_ref, o_ref, state_ref, *, sub, n_sub, heads, dk):
    @pl.when(pl.program_id(2) == 0)
    def _():
        state_ref[...] = jnp.zeros_like(state_ref)

    lb = lb_ref[...]
    nw = nw_ref[...]
    ti = lax.broadcasted_iota(jnp.int32, (sub, sub), 0)
    si = lax.broadcasted_iota(jnp.int32, (sub, sub), 1)
    causal = ti >= si
    f = lb + (1.0 - lb) * jax.nn.sigmoid(f_ref[0])
    lf = jnp.log(f)
    k = 1.0 - f
    b = _seg_cumsum_rows(lf, sub)
    to_end = _seg_cumsum_rows(lf, sub, reverse=True) - lf
    q_dec = (_silu(q_ref[0]) * jnp.exp(b)).astype(BF16)
    k_dec = (k * jnp.exp(-b)).astype(BF16)
    k_end = (k * jnp.exp(to_end)).astype(BF16)
    v = i_ref[0].astype(BF16)
    cs = range(n_sub)
    hs = range(heads)
    blk = lambda x, c, h: x[c * sub:(c + 1) * sub, h * dk:(h + 1) * dk]
    scores = [[jnp.where(causal, _dot_nt(blk(q_dec, c, h), blk(k_dec, c, h)), 0.0).astype(BF16) for h in hs]
              for c in cs]
    upd = [[_dot_tn(blk(v, c, h), blk(k_end, c, h)) for h in hs] for c in cs]
    states = []
    st = [state_ref[h] for h in hs]
    for c in cs:
        states.append(st)
        decay = jnp.exp(b[(c + 1) * sub - 1:(c + 1) * sub, :])
        st = [st[h] * decay[:, h * dk:(h + 1) * dk] + upd[c][h] for h in hs]
    for h in hs:
        state_ref[h] = st[h]
    for c in cs:
        rows = slice(c * sub, (c + 1) * sub)
        for h in hs:
            o = (jnp.dot(scores[c][h], blk(v, c, h), preferred_element_type=F32)
                 + _dot_nt(blk(q_dec, c, h), states[c][h].astype(BF16)))
            o = o * lax.rsqrt(jnp.mean(o * o, axis=-1, keepdims=True) + NORM_EPS) * nw
            cols = slice(h * dk, (h + 1) * dk)
            o_ref[0, rows, cols] = (o * _silu(g_ref[0, rows, cols])).astype(o_ref.dtype)


def hgrn2_scan(proj, lower_bound, norm_w, *, dk=HG_HEAD_DIM, sub=HG_CHUNK, tb=256, heads=4):
    bsz, s_len, d4 = proj.shape
    d = d4 // 4
    n_heads = d // dk
    tb = min(tb, s_len)
    heads = min(heads, n_heads)
    hw = heads * dk
    n_hb = n_heads // heads
    kern = functools.partial(_hgrn_kernel, sub=sub, n_sub=tb // sub, heads=heads, dk=dk)
    spec = lambda part: pl.BlockSpec((1, tb, hw), lambda b_, h_, t: (b_, t, part * n_hb + h_))
    return pl.pallas_call(
        kern,
        grid=(bsz, n_hb, s_len // tb),
        in_specs=[spec(0), spec(1), spec(2), spec(3),
                  pl.BlockSpec((1, hw), lambda b_, h_, t: (0, h_)),
                  pl.BlockSpec((1, dk), lambda b_, h_, t: (0, 0))],
        out_specs=pl.BlockSpec((1, tb, hw), lambda b_, h_, t: (b_, t, h_)),
        out_shape=jax.ShapeDtypeStruct((bsz, s_len, d), BF16),
        scratch_shapes=[pltpu.VMEM((heads, dk, dk), F32)],
        compiler_params=_params("parallel", "parallel", "arbitrary"),
        name="hgrn2_scan",
    )(proj, proj, proj, proj, lower_bound.reshape(1, d), norm_w.reshape(1, dk))


def hgrn2_mixer(u, h, w, lower_bound, bsz, s_len):
    d = h.shape[1]
    proj = matmul_ws(u, [(w["hg_w_in"], 0)], 4 * d, name="hg_in")
    o = hgrn2_scan(proj.reshape(bsz, s_len, 4 * d), lower_bound, w["hg_norm_w"])
    return matmul_ws(o.reshape(bsz * s_len, d), [(w["hg_w_out"], 0)], d,
                     epilogue=_ep_residual, extras=[(h, "mn")], name="hg_out")


def dense_ffn(v, h, w_in, w_out):
    f = w_out.shape[0]
    hid = matmul_ws(v, [(w_in, 0), (w_in, f)], f, epilogue=_ep_swiglu, out_dtype=BF16, bm=1024, bn=256,
                    name="ffn_in")
    return matmul_ws(hid, [(w_out, 0)], h.shape[1], epilogue=_ep_residual, extras=[(h, "mn")], bm=512, w_buffers=1,
                     name="ffn_out")


def _router_kernel(x_ref, r_ref, o_ref, *, n_experts):
    logits = jnp.dot(x_ref[...], r_ref[...], preferred_element_type=F32)
    lane = lax.broadcasted_iota(jnp.int32, logits.shape, 1)
    logits = jnp.where(lane < n_experts, logits, -jnp.inf)
    m1 = jnp.max(logits, axis=-1, keepdims=True)
    i1 = jnp.min(jnp.where(logits == m1, lane, LANES), axis=-1, keepdims=True)
    rest = jnp.where(lane == i1, -jnp.inf, logits)
    m2 = jnp.max(rest, axis=-1, keepdims=True)
    i2 = jnp.min(jnp.where(rest == m2, lane, LANES), axis=-1, keepdims=True)
    e2 = jnp.exp(m2 - m1)
    w1 = 1.0 / (1.0 + e2)
    o_ref[...] = jnp.where(lane == i1, w1, 0.0) + jnp.where(lane == i2, e2 * w1, 0.0)


def moe_router(v, router):
    m, d = v.shape
    n_experts = router.shape[1]
    r_pad = jnp.zeros((d, LANES), BF16).at[:, :n_experts].set(router.astype(BF16))
    bm = _pick(m, 512)
    return pl.pallas_call(
        functools.partial(_router_kernel, n_experts=n_experts),
        grid=(m // bm,),
        in_specs=[pl.BlockSpec((bm, d), lambda i: (i, 0)), pl.BlockSpec((d, LANES), lambda i: (0, 0))],
        out_specs=pl.BlockSpec((bm, LANES), lambda i: (i, 0)),
        out_shape=jax.ShapeDtypeStruct((m, LANES), F32),
        compiler_params=_params("parallel"),
        name="moe_router",
    )(v, r_pad)


def _moe_in_kernel(x_ref, wg_ref, wu_ref, c_ref, o_ref, wgb_ref, wub_ref, *, blocks_per_expert):
    @pl.when(pl.program_id(1) == 0)
    def _():
        wgb_ref[...] = wg_ref[0].astype(BF16)
        wub_ref[...] = wu_ref[0].astype(BF16)

    x = x_ref[...]
    g = jnp.dot(x, wgb_ref[...], preferred_element_type=F32)
    u = jnp.dot(x, wub_ref[...], preferred_element_type=F32)
    e = pl.program_id(0) // blocks_per_expert
    comb = c_ref[...]
    lane = lax.broadcasted_iota(jnp.int32, comb.shape, 1)
    scale = jnp.sum(jnp.where(lane == e, comb, 0.0), axis=-1, keepdims=True)
    o_ref[...] = (_silu(g) * u * scale).astype(o_ref.dtype)


def moe_ffn(v, h, router, w_in, w_out, *, bm=512, bn=512):
    m, d = v.shape
    n_experts, _, two_de = w_in.shape
    de = two_de // 2
    bm = _pick(m, bm)
    bn = _pick(de, bn)
    bpe = de // bn
    comb = moe_router(v, router)
    hid = pl.pallas_call(
        functools.partial(_moe_in_kernel, blocks_per_expert=bpe),
        grid=(n_experts * bpe, m // bm),
        in_specs=[pl.BlockSpec((bm, d), lambda j, i: (i, 0)),
                  pl.BlockSpec((1, d, bn), lambda j, i: (j // bpe, 0, j % bpe)),
                  pl.BlockSpec((1, d, bn), lambda j, i: (j // bpe, 0, j % bpe + bpe)),
                  pl.BlockSpec((bm, LANES), lambda j, i: (i, 0))],
        out_specs=pl.BlockSpec((bm, bn), lambda j, i: (i, j)),
        out_shape=jax.ShapeDtypeStruct((m, n_experts * de), BF16),
        scratch_shapes=[pltpu.VMEM((d, bn), BF16), pltpu.VMEM((d, bn), BF16)],
        compiler_params=_params("parallel", "arbitrary"),
        name="moe_in",
    )(v, w_in, w_in, comb)
    return matmul(hid, [(w_out.reshape(n_experts * de, d), 0)], d, epilogue=_ep_residual, extras=[(h, "mn")],
                  name="moe_out")


MOE_BLOCK = 1024
MOE_UNIT = 128
MOE_TILE = 512


def _moe_gather_kernel(x_ref, tok_ref, o_ref):
    tok = tok_ref[0]
    lane = lax.broadcasted_iota(jnp.int32, (tok.shape[0], x_ref.shape[0]), 1)
    onehot = jnp.where(tok == lane, 1.0, 0.0).astype(BF16)
    o_ref[...] = jnp.dot(onehot, x_ref[...], preferred_element_type=F32).astype(o_ref.dtype)


def _moe_expert_in_kernel(src_ref, exp_ref, first_ref, used_ref, *refs, per):
    x_refs = refs[:per]
    wg_ref, wu_ref, rw_ref, o_ref, wgb_ref, wub_ref, x_scr = refs[per:]
    t = pl.program_id(1)

    @pl.when(t < used_ref[0])
    def _():
        @pl.when(first_ref[t] == 1)
        def _():
            wgb_ref[...] = wg_ref[0].astype(BF16)
            wub_ref[...] = wu_ref[0].astype(BF16)

        unit = x_refs[0].shape[0]
        for i in range(per):
            x_scr[i * unit:(i + 1) * unit, :] = x_refs[i][...]
        x = x_scr[...]
        g = jnp.dot(x, wgb_ref[...], preferred_element_type=F32)
        u = jnp.dot(x, wub_ref[...], preferred_element_type=F32)
        o_ref[...] = (_silu(g) * u * rw_ref[...]).astype(o_ref.dtype)

    @pl.when(t >= used_ref[0])
    def _():
        o_ref[...] = jnp.zeros_like(o_ref)


def _moe_expert_out_kernel(exp_ref, used_ref, hid_ref, w_ref, o_ref):
    @pl.when(pl.program_id(1) < used_ref[0])
    def _():
        o_ref[...] = jnp.dot(hid_ref[...], w_ref[0], preferred_element_type=F32).astype(o_ref.dtype)

    @pl.when(pl.program_id(1) >= used_ref[0])
    def _():
        o_ref[...] = jnp.zeros_like(o_ref)


def _moe_scatter_kernel(dst_ref, h_ref, tok_ref, *refs, per):
    y_refs = refs[:per]
    o_ref, y_scr = refs[per:]

    @pl.when(pl.program_id(2) == 0)
    def _():
        o_ref[...] = h_ref[...]

    tok = tok_ref[0]
    row = lax.broadcasted_iota(jnp.int32, (o_ref.shape[0], tok.shape[1]), 0)
    onehot_t = jnp.where(tok == row, 1.0, 0.0).astype(BF16)
    unit = y_refs[0].shape[0]
    for i in range(per):
        y_scr[i * unit:(i + 1) * unit, :] = y_refs[i][...]
    o_ref[...] += jnp.dot(onehot_t, y_scr[...], preferred_element_type=F32)


def moe_ffn_routed(v, h, router, w_in, w_out, *, tb=MOE_BLOCK, unit=MOE_UNIT, tile=MOE_TILE, bn=512, bo=1024):
    m, d = v.shape
    n_experts, _, two_de = w_in.shape
    de = two_de // 2
    tb = min(tb, m)
    nb = m // tb
    per = tile // unit
    bn = _pick(de, bn)
    bo = _pick(d, bo)
    n_assign = MOE_TOPK * tb
    n_slots = -(-(n_assign // unit + n_experts + 1) // per) * per
    groups = n_slots // per
    n_units = nb * (n_assign // unit + n_experts) + n_experts * (per - 1)
    n_tiles = -(-n_units // per)
    n_units = n_tiles * per

    comb = moe_router(v, router)
    wts, ids = lax.top_k(comb[:, :n_experts], MOE_TOPK)
    ea = ids.reshape(nb, n_assign)
    wa = wts.reshape(nb, n_assign)
    ta = jnp.broadcast_to(jnp.repeat(jnp.arange(tb, dtype=jnp.int32), MOE_TOPK)[None], (nb, n_assign))
    se, st, sw = lax.sort((ea, ta, wa), dimension=1, num_keys=1, is_stable=True)
    counts = jnp.sum(jax.nn.one_hot(ea, n_experts, dtype=jnp.int32), axis=1)
    units = -(-counts // unit)
    excl = lambda x, axis: jnp.cumsum(x, axis=axis) - x
    slot_start = excl(units, 1)
    row_start = excl(counts, 1)
    is_e = se[..., None] == jnp.arange(n_experts, dtype=jnp.int32)
    lookup = lambda table: jnp.sum(jnp.where(is_e, table[:, None, :], 0), axis=-1)
    pos = lookup(slot_start) * unit + jnp.arange(n_assign, dtype=jnp.int32)[None] - lookup(row_start)
    bidx = jnp.arange(nb, dtype=jnp.int32)[:, None]
    hit = pos[:, None, :] == jnp.arange(n_slots * unit, dtype=jnp.int32)[None, :, None]
    row_token = jnp.sum(jnp.where(hit, st[:, None, :] + 1, 0), axis=-1) - 1
    row_weight = jnp.sum(jnp.where(hit, sw[:, None, :], 0.0), axis=-1)
    per_expert = jnp.sum(units, axis=0)
    per_expert_pad = -(-per_expert // per) * per
    e_off = excl(per_expert_pad, 0)
    before = excl(units, 0)
    slot = jnp.arange(n_slots, dtype=jnp.int32)
    slot_end = jnp.cumsum(units, axis=1)
    e_of_slot = jnp.sum(slot[None, :, None] >= slot_end[:, None, :], axis=-1)
    used_slot = e_of_slot < n_experts
    e_clip = jnp.minimum(e_of_slot, n_experts - 1)
    dst_unit = (e_off[e_clip] + jnp.take_along_axis(before, e_clip, axis=1)
                + slot[None] - jnp.take_along_axis(slot_start, e_clip, axis=1))
    dst_unit = jnp.where(used_slot, dst_unit, 0).astype(jnp.int32)
    flat_slot = (bidx * n_slots + slot[None]).astype(jnp.int32)
    zero_slot = n_slots - 1
    src_unit = jnp.full((n_units,), zero_slot, jnp.int32).at[
        jnp.where(used_slot, dst_unit, n_units).reshape(-1)].set(flat_slot.reshape(-1), mode="drop")
    tile_end = jnp.cumsum(per_expert_pad) // per
    tile_ids = jnp.arange(n_tiles, dtype=jnp.int32)
    tile_expert = jnp.minimum(jnp.sum(tile_ids[:, None] >= tile_end[None, :], axis=-1), n_experts - 1).astype(jnp.int32)
    tiles_used = tile_end[-1:].astype(jnp.int32)
    first = jnp.concatenate([jnp.ones((1,), jnp.int32),
                             (tile_expert[1:] != tile_expert[:-1]).astype(jnp.int32)])
    rw_em = row_weight.reshape(nb * n_slots, unit)[src_unit].reshape(n_units * unit, 1)

    xs = pl.pallas_call(
        _moe_gather_kernel,
        grid=(nb, groups),
        in_specs=[pl.BlockSpec((tb, d), lambda b_, g: (b_, 0)),
                  pl.BlockSpec((1, tile, 1), lambda b_, g: (b_, g, 0))],
        out_specs=pl.BlockSpec((tile, d), lambda b_, g: (b_ * groups + g, 0)),
        out_shape=jax.ShapeDtypeStruct((nb * n_slots * unit, d), BF16),
        compiler_params=_params("parallel", "arbitrary"),
        name="moe_gather",
    )(v, row_token.reshape(nb, n_slots * unit, 1))

    bpe = de // bn
    unit_spec = lambda i: pl.BlockSpec((unit, d), lambda j, t, src, ex, fi, us: (src[per * t + i], 0))
    hid = pl.pallas_call(
        functools.partial(_moe_expert_in_kernel, per=per),
        grid_spec=pltpu.PrefetchScalarGridSpec(
            num_scalar_prefetch=4,
            grid=(bpe, n_tiles),
            in_specs=[unit_spec(i) for i in range(per)] + [
                pl.BlockSpec((1, d, bn), lambda j, t, src, ex, fi, us: (ex[t], 0, j),
                             pipeline_mode=pl.Buffered(1)),
                pl.BlockSpec((1, d, bn), lambda j, t, src, ex, fi, us: (ex[t], 0, j + bpe),
                             pipeline_mode=pl.Buffered(1)),
                pl.BlockSpec((tile, 1), lambda j, t, src, ex, fi, us: (t, 0))],
            out_specs=pl.BlockSpec((tile, bn), lambda j, t, src, ex, fi, us: (t, j)),
            scratch_shapes=[pltpu.VMEM((d, bn), BF16), pltpu.VMEM((d, bn), BF16), pltpu.VMEM((tile, d), BF16)]),
        out_shape=jax.ShapeDtypeStruct((n_tiles * tile, de), BF16),
        compiler_params=_params("arbitrary", "arbitrary"),
        name="moe_expert_in",
    )(src_unit, tile_expert, first, tiles_used, *([xs] * per), w_in, w_in, rw_em)

    bo2 = _pick(d, 2 * bo)
    ys = pl.pallas_call(
        _moe_expert_out_kernel,
        grid_spec=pltpu.PrefetchScalarGridSpec(
            num_scalar_prefetch=2,
            grid=(d // bo2, n_tiles),
            in_specs=[pl.BlockSpec((tile, de), lambda n, t, ex, us: (t, 0)),
                      pl.BlockSpec((1, de, bo2), lambda n, t, ex, us: (ex[t], 0, n))],
            out_specs=pl.BlockSpec((tile, bo2), lambda n, t, ex, us: (t, n))),
        out_shape=jax.ShapeDtypeStruct((n_tiles * tile, d), BF16),
        compiler_params=_params("parallel", "arbitrary"),
        name="moe_expert_out",
    )(tile_expert, tiles_used, hid, w_out)

    y_spec = lambda i: pl.BlockSpec((unit, bo), lambda b_, n, g, dst: (dst[(b_ * groups + g) * per + i], n))
    return pl.pallas_call(
        functools.partial(_moe_scatter_kernel, per=per),
        grid_spec=pltpu.PrefetchScalarGridSpec(
            num_scalar_prefetch=1,
            grid=(nb, d // bo, groups),
            in_specs=[pl.BlockSpec((tb, bo), lambda b_, n, g, dst: (b_, n)),
                      pl.BlockSpec((1, 1, tile), lambda b_, n, g, dst: (b_ * groups + g, 0, 0))]
                     + [y_spec(i) for i in range(per)],
            out_specs=pl.BlockSpec((tb, bo), lambda b_, n, g, dst: (b_, n)),
            scratch_shapes=[pltpu.VMEM((tile, bo), BF16)]),
        out_shape=jax.ShapeDtypeStruct((m, d), F32),
        compiler_params=_params("parallel", "parallel", "arbitrary"),
        name="moe_scatter",
    )(dst_unit.reshape(-1), h, row_token.reshape(nb * groups, 1, tile), *([ys] * per))


def ple_gate(h, p_i, norm_pl, pl_proj, pl_gate, layer):
    d = h.shape[1]
    n = rmsnorm(h, norm_pl, name="rmsnorm_ple")
    return matmul_ws(n, [(pl_gate, (layer, 0))], d, epilogue=_ep_ple_gate,
                     extras=[(h, "mn"), (p_i, "m"), (pl_proj, "kn")], name="ple_gate")


def _rw_mix_kernel(u_ref, mu_ref, *o_refs):
    u = u_ref[0]
    row = lax.broadcasted_iota(jnp.int32, u.shape, 0)
    dx = jnp.where(row >= 1, pltpu.roll(u, 1, 0), 0.0) - u
    for j, o_ref in enumerate(o_refs):
        o_ref[0] = (u + dx * mu_ref[j:j + 1, :]).astype(o_ref.dtype)


def rw_token_mix(u, mu):
    bsz, s_len, d = u.shape
    cb = _pick(d, LANES)
    n_mix = mu.shape[0]
    spec = pl.BlockSpec((1, s_len, cb), lambda b_, j: (b_, 0, j))
    return pl.pallas_call(
        _rw_mix_kernel,
        grid=(bsz, d // cb),
        in_specs=[spec, pl.BlockSpec((n_mix, cb), lambda b_, j: (0, j))],
        out_specs=[spec] * n_mix,
        out_shape=[jax.ShapeDtypeStruct(u.shape, BF16)] * n_mix,
        compiler_params=_params("parallel", "parallel"),
        name="rwkv_token_mix",
    )(u, mu)


def _dot_hi(a, b):
    return jnp.dot(a, b, preferred_element_type=F32, precision=lax.Precision.HIGHEST)


def _rw_scan_kernel(r_ref, k_ref, v_ref, a_ref, lw_ref, g_ref, kk_ref, ka_ref, rk_ref, lnw_ref, lnb_ref,
                    o_ref, state_ref, *, chunk, heads, n):
    @pl.when(pl.program_id(2) == 0)
    def _():
        state_ref[...] = jnp.zeros_like(state_ref)

    hs = range(heads)
    sls = [slice(j * n, (j + 1) * n) for j in hs]
    ti = lax.broadcasted_iota(jnp.int32, (chunk, chunk), 0)
    si = lax.broadcasted_iota(jnp.int32, (chunk, chunk), 1)
    strict = ti > si
    incl = ti >= si
    dot = functools.partial(jnp.dot, preferred_element_type=F32)

    r = [r_ref[0, :, sl] for sl in sls]
    v = [v_ref[0, :, sl] for sl in sls]
    a = [a_ref[0, :, sl] for sl in sls]
    lw = [lw_ref[0, :, sl] for sl in sls]
    k = [k_ref[0, :, sl] for sl in sls]
    kk = [k[j] * kk_ref[:, sls[j]] for j in hs]
    kk = [kk[j] / jnp.maximum(jnp.sqrt(jnp.sum(kk[j] * kk[j], axis=-1, keepdims=True)), 1e-12) for j in hs]
    kmod = [k[j] * (1.0 + (a[j] - 1.0) * ka_ref[:, sls[j]]) for j in hs]
    kka = [kk[j] * a[j] for j in hs]
    cum = [_cumsum_rows(lw[j], chunk) for j in hs]
    cum_end = [c[chunk - 1:chunk, :] for c in cum]
    mid = [c[chunk // 2 - 1:chunk // 2, :] for c in cum]
    e_neg = [jnp.exp(mid[j] - cum[j]) for j in hs]
    am = [(kk[j] * jnp.exp(cum[j] - lw[j] - mid[j])).astype(BF16) for j in hs]
    bm = [(kka[j] * e_neg[j]).astype(BF16) for j in hs]
    km = [(kmod[j] * e_neg[j]).astype(BF16) for j in hs]
    rm = [(r[j] * jnp.exp(cum[j] - mid[j])).astype(BF16) for j in hs]
    a_abs = [(kk[j] * jnp.exp(cum[j] - lw[j])).astype(BF16) for j in hs]
    r_abs = [(r[j] * jnp.exp(cum[j])).astype(BF16) for j in hs]
    vb = [x.astype(BF16) for x in v]
    st = [state_ref[j] for j in hs]
    stb = [x.astype(BF16) for x in st]

    nb = [(-jnp.where(strict, _dot_nt(am[j], bm[j]), 0.0)).astype(BF16) for j in hs]
    lk = [jnp.where(strict, _dot_nt(am[j], km[j]), 0.0).astype(BF16) for j in hs]
    x = [_dot_nt(a_abs[j], stb[j]) + dot(lk[j], vb[j]) for j in hs]
    x = [x[j] + dot(nb[j], x[j].astype(BF16)) for j in hs]
    p = 2
    while p < chunk:
        nb = [dot(nb[j], nb[j]).astype(BF16) for j in hs]
        x = [x[j] + dot(nb[j], x[j].astype(BF16)) for j in hs]
        p *= 2
    pb = [xj.astype(BF16) for xj in x]
    mk = [jnp.where(incl, _dot_nt(rm[j], km[j]), 0.0).astype(BF16) for j in hs]
    mb = [jnp.where(incl, _dot_nt(rm[j], bm[j]), 0.0).astype(BF16) for j in hs]
    y = [_dot_nt(r_abs[j], stb[j]) + dot(mk[j], vb[j]) - dot(mb[j], pb[j]) for j in hs]
    to_end = [jnp.exp(cum_end[j] - cum[j]) for j in hs]
    for j in hs:
        state_ref[j] = (st[j] * jnp.exp(cum_end[j]) + _dot_tn(vb[j], (kmod[j] * to_end[j]).astype(BF16))
                        - _dot_tn(pb[j], (kka[j] * to_end[j]).astype(BF16)))
    for j in hs:
        sl = sls[j]
        bonus = jnp.sum(r[j] * kmod[j] * rk_ref[:, sl], axis=-1, keepdims=True) * v[j]
        mean = jnp.mean(y[j], axis=-1, keepdims=True)
        yc = y[j] - mean
        var = jnp.mean(yc * yc, axis=-1, keepdims=True)
        yn = yc * lax.rsqrt(var + RW_LN_EPS) * lnw_ref[:, sl] + lnb_ref[:, sl]
        o_ref[0, :, sl] = ((yn + bonus) * g_ref[0, :, sl]).astype(o_ref.dtype)


def _rw_scan_tile_kernel(r_ref, k_ref, v_ref, a_ref, lw_ref, g_ref, kk_ref, ka_ref, rk_ref, lnw_ref, lnb_ref,
                         o_ref, state_ref, *, chunk, heads, n):
    @pl.when(pl.program_id(2) == 0)
    def _():
        state_ref[...] = jnp.zeros_like(state_ref)

    per = LANES // n
    tiles = range(heads // per)
    sub = range(per)
    ti = lax.broadcasted_iota(jnp.int32, (chunk, chunk), 0)
    si = lax.broadcasted_iota(jnp.int32, (chunk, chunk), 1)
    strict = ti > si
    incl = ti >= si
    lane_seg = lax.broadcasted_iota(jnp.int32, (1, LANES), 1) // n
    seg_is = [lane_seg == j for j in sub]
    same_head = (lax.broadcasted_iota(jnp.int32, (LANES, LANES), 0) // n
                 == lax.broadcasted_iota(jnp.int32, (LANES, LANES), 1) // n)
    dot = functools.partial(jnp.dot, preferred_element_type=F32)
    tile = lambda x, i: x[:, i * LANES:(i + 1) * LANES]

    def pick(vals):
        out = vals[-1]
        for j in range(per - 2, -1, -1):
            out = jnp.where(seg_is[j], vals[j], out)
        return out

    def seg_sum(x):
        return pick([jnp.sum(jnp.where(seg_is[j], x, 0.0), axis=-1, keepdims=True) for j in sub])

    r, k, v, a, lw = r_ref[0], k_ref[0], v_ref[0], a_ref[0], lw_ref[0]
    kk = k * kk_ref[...]
    kmod = k * (1.0 + (a - 1.0) * ka_ref[...])
    cum = _cumsum_rows(lw, chunk)
    cum_end = cum[chunk - 1:chunk, :]
    mid = cum[chunk // 2 - 1:chunk // 2, :]
    bonus_in = r * kmod * rk_ref[...]
    kk_t, bonus_t = [], []
    for i in tiles:
        kki = tile(kk, i)
        kk_t.append(kki / jnp.maximum(jnp.sqrt(seg_sum(kki * kki)), 1e-12))
        bonus_t.append(seg_sum(tile(bonus_in, i)) * tile(v, i))
    kk = jnp.concatenate(kk_t, axis=-1) if len(kk_t) > 1 else kk_t[0]
    kka = kk * a
    e_neg = jnp.exp(mid - cum)
    to_end = jnp.exp(cum_end - cum)
    am = (kk * jnp.exp(cum - lw - mid)).astype(BF16)
    bm = (kka * e_neg).astype(BF16)
    km = (kmod * e_neg).astype(BF16)
    rm = (r * jnp.exp(cum - mid)).astype(BF16)
    a_abs = (kk * jnp.exp(cum - lw)).astype(BF16)
    r_abs = (r * jnp.exp(cum)).astype(BF16)
    k_end = (kmod * to_end).astype(BF16)
    b_end = (kka * to_end).astype(BF16)
    vb = v.astype(BF16)
    st_decay = jnp.exp(cum_end)
    zero = jnp.zeros((), BF16)

    st = [state_ref[i] for i in tiles]
    stb = [s.astype(BF16) for s in st]
    am_h = [[jnp.where(seg_is[j], tile(am, i), zero) for j in sub] for i in tiles]
    rm_h = [[jnp.where(seg_is[j], tile(rm, i), zero) for j in sub] for i in tiles]
    nb = [[(-jnp.where(strict, _dot_nt(am_h[i][j], tile(bm, i)), 0.0)).astype(BF16) for j in sub] for i in tiles]
    lk = [[jnp.where(strict, _dot_nt(am_h[i][j], tile(km, i)), 0.0).astype(BF16) for j in sub] for i in tiles]
    x = [_dot_nt(tile(a_abs, i), stb[i]) + pick([dot(lk[i][j], tile(vb, i)) for j in sub]) for i in tiles]
    xb = [xi.astype(BF16) for xi in x]
    x = [x[i] + pick([dot(nb[i][j], xb[i]) for j in sub]) for i in tiles]
    p = 2
    while p < chunk:
        nb = [[dot(nb[i][j], nb[i][j]).astype(BF16) for j in sub] for i in tiles]
        xb = [xi.astype(BF16) for xi in x]
        x = [x[i] + pick([dot(nb[i][j], xb[i]) for j in sub]) for i in tiles]
        p *= 2
    pb = [xi.astype(BF16) for xi in x]
    mk = [[jnp.where(incl, _dot_nt(rm_h[i][j], tile(km, i)), 0.0).astype(BF16) for j in sub] for i in tiles]
    mb = [[jnp.where(incl, _dot_nt(rm_h[i][j], tile(bm, i)), 0.0).astype(BF16) for j in sub] for i in tiles]
    y = [_dot_nt(tile(r_abs, i), stb[i])
         + pick([dot(mk[i][j], tile(vb, i)) - dot(mb[i][j], pb[i]) for j in sub]) for i in tiles]
    for i in tiles:
        upd = _dot_tn(tile(vb, i), tile(k_end, i)) - _dot_tn(pb[i], tile(b_end, i))
        state_ref[i] = st[i] * tile(st_decay, i) + jnp.where(same_head, upd, 0.0)
    inv_n = 1.0 / n
    for i in tiles:
        cols = slice(i * LANES, (i + 1) * LANES)
        mean = seg_sum(y[i]) * inv_n
        yc = y[i] - mean
        var = seg_sum(yc * yc) * inv_n
        yn = yc * lax.rsqrt(var + RW_LN_EPS) * lnw_ref[:, cols] + lnb_ref[:, cols]
        o_ref[0, :, cols] = ((yn + bonus_t[i]) * g_ref[0, :, cols]).astype(o_ref.dtype)


def rw_scan(r, k, v, a, lw, g, k_k, k_a, r_k, ln_w, ln_b, *, n=RW_HEAD_DIM, chunk=RW_CHUNK, heads=8):
    bsz, s_len, d = r.shape
    chunk = min(chunk, s_len)
    heads = min(heads, d // n)
    hw = heads * n
    seq = pl.BlockSpec((1, chunk, hw), lambda b_, h_, c: (b_, c, h_))
    par = pl.BlockSpec((1, hw), lambda b_, h_, c: (0, h_))
    row = lambda t: t.reshape(1, d)
    assert hw % LANES == 0 and LANES % n == 0
    kern = functools.partial(_rw_scan_tile_kernel, chunk=chunk, heads=heads, n=n)
    return pl.pallas_call(
        kern,
        grid=(bsz, d // hw, s_len // chunk),
        in_specs=[seq] * 6 + [par] * 5,
        out_specs=seq,
        out_shape=jax.ShapeDtypeStruct(r.shape, BF16),
        scratch_shapes=[pltpu.VMEM((hw // LANES, LANES, LANES), F32)],
        compiler_params=_params("parallel", "parallel", "arbitrary"),
        name="rwkv7_scan",
    )(r, k, v, a, lw, g, row(k_k), row(k_a), row(r_k), row(ln_w), row(ln_b))


def rwkv7_mixer(u, h, w, bsz, s_len):
    t, d = u.shape
    xr, xw, xk, xv, xa, xg = [x.reshape(t, d) for x in rw_token_mix(u.reshape(bsz, s_len, d), w["rw_mu"])]
    r = matmul_ws(xr, [(w["rw_w_rkv"], (0, 0))], d, name="rw_r")
    k = matmul_ws(xk, [(w["rw_w_rkv"], (1, 0))], d, name="rw_k")
    v = matmul_ws(xv, [(w["rw_w_rkv"], (2, 0))], d, name="rw_v")
    row = lambda x: x.reshape(1, d)
    w_lo = matmul(xw, [(w["rw_w1"], 0)], w["rw_w1"].shape[1], epilogue=_ep_tanh, out_dtype=BF16, name="rw_w1")
    lw = matmul(w_lo, [(w["rw_w2"], 0)], d, epilogue=_ep_rw_logdecay, extras=[(row(w["rw_w0"]), "n")], name="rw_w2")
    a_lo = matmul(xa, [(w["rw_a1"], 0)], w["rw_a1"].shape[1], out_dtype=BF16, name="rw_a1")
    a = matmul(a_lo, [(w["rw_a2"], 0)], d, epilogue=_ep_bias_sigmoid, extras=[(row(w["rw_a0"]), "n")], name="rw_a2")
    g_lo = matmul(xg, [(w["rw_g1"], 0)], w["rw_g1"].shape[1], epilogue=_ep_sigmoid, out_dtype=BF16, name="rw_g1")
    g = matmul(g_lo, [(w["rw_g2"], 0)], d, name="rw_g2")
    shp = (bsz, s_len, d)
    y = rw_scan(r.reshape(shp), k.reshape(shp), v.reshape(shp), a.reshape(shp), lw.reshape(shp), g.reshape(shp),
                w["rw_k_k"], w["rw_k_a"], w["rw_r_k"], w["rw_ln_w"], w["rw_ln_b"])
    return matmul_ws(y.reshape(t, d), [(w["rw_w_out"], 0)], d, epilogue=_ep_residual, extras=[(h, "mn")],
                     name="rw_out")


NEG_BIG = -1e30


def _rope_kernel(x_ref, cc_ref, ss_ref, o_ref, *, n_q_slots, scale):
    x = x_ref[0]
    out = x * cc_ref[...] + pltpu.roll(x, x.shape[-1] // 2, 1) * ss_ref[...]
    out = out * jnp.where(pl.program_id(2) < n_q_slots, scale, 1.0)
    o_ref[0] = out.astype(o_ref.dtype)


def _rope_tables(pos, dim):
    inv = ROPE_THETA ** (-(jnp.arange(0, dim, 2, dtype=F32) / dim))
    ang = pos.astype(F32)[:, None] * inv[None, :]
    cos, sin = jnp.cos(ang), jnp.sin(ang)
    return jnp.concatenate([cos, cos], axis=-1), jnp.concatenate([-sin, sin], axis=-1)


def nsa_rope(proj, n_q_slots, k_slots, dh, scale, tb=512):
    bsz, s_len, _ = proj.shape
    tb = min(tb, s_len)
    cc, ss = _rope_tables(jnp.arange(s_len), dh)
    n_out = n_q_slots + len(k_slots)

    def in_slot(j):
        slot = j
        for idx, ks in enumerate(k_slots):
            slot = jnp.where(j == n_q_slots + idx, ks, slot)
        return slot

    return pl.pallas_call(
        functools.partial(_rope_kernel, n_q_slots=n_q_slots, scale=scale),
        grid=(bsz, s_len // tb, n_out),
        in_specs=[pl.BlockSpec((1, tb, dh), lambda b_, t, j: (b_, t, in_slot(j))),
                  pl.BlockSpec((tb, dh), lambda b_, t, j: (t, 0)),
                  pl.BlockSpec((tb, dh), lambda b_, t, j: (t, 0))],
        out_specs=pl.BlockSpec((1, tb, dh), lambda b_, t, j: (b_, t, j)),
        out_shape=jax.ShapeDtypeStruct((bsz, s_len, n_out * dh), BF16),
        compiler_params=_params("parallel", "parallel", "arbitrary"),
        name="nsa_rope",
    )(proj, cc, ss)


def _cmp_finish_kernel(z_ref, bias_ref, w2_ref, cc_ref, ss_ref, o_ref, *, hidden, rope):
    z = z_ref[0]
    nc = z.shape[0]
    nxt = pltpu.roll(z[:, hidden:], nc - 1, 0)
    hid = _silu(z[:, :hidden] + nxt + bias_ref[...])
    out = jnp.dot(hid.astype(BF16), w2_ref[...], preferred_element_type=F32)
    if rope:
        out = out * cc_ref[...] + pltpu.roll(out, out.shape[-1] // 2, 1) * ss_ref[...]
    o_ref[0] = out.astype(o_ref.dtype)


def nsa_compress(x, pos_emb, w1, w2, bsz, s_len, groups, dh, rope, transpose_out=False):
    stride, blk = NSA_CMP_STRIDE, NSA_CMP_BLOCK
    nc = s_len // stride
    hidden = w1.shape[-1]
    half = stride * dh
    x16 = jnp.transpose(x.reshape(bsz, nc, stride, groups, dh), (0, 3, 1, 2, 4)).reshape(bsz * groups * nc, half)
    w1f = w1.reshape(blk * dh, hidden)
    wcat = jnp.concatenate([w1f[:half], w1f[half:]], axis=1).astype(BF16)
    z = matmul(x16.astype(BF16), [(wcat, 0)], 2 * hidden, name="nsa_cmp_w1")
    bias = matmul(pos_emb.reshape(1, blk * dh).astype(BF16), [(w1f.astype(BF16), 0)], hidden, name="nsa_cmp_pos")
    cc, ss = _rope_tables(jnp.arange(nc) * stride + blk - 1, dh)
    if transpose_out:
        assert not rope
        return pl.pallas_call(
            functools.partial(_cmp_finish_t_kernel, hidden=hidden),
            grid=(bsz * groups,),
            in_specs=[pl.BlockSpec((1, nc, 2 * hidden), lambda i: (i, 0, 0)),
                      pl.BlockSpec((1, hidden), lambda i: (0, 0)),
                      pl.BlockSpec((dh, hidden), lambda i: (0, 0))],
            out_specs=pl.BlockSpec((1, dh, nc), lambda i: (i, 0, 0)),
            out_shape=jax.ShapeDtypeStruct((bsz * groups, dh, nc), BF16),
            compiler_params=_params("parallel"),
            name="nsa_cmp_finish_t",
        )(z.reshape(bsz * groups, nc, 2 * hidden), bias, w2.T.astype(BF16))
    return pl.pallas_call(
        functools.partial(_cmp_finish_kernel, hidden=hidden, rope=rope),
        grid=(bsz * groups,),
        in_specs=[pl.BlockSpec((1, nc, 2 * hidden), lambda i: (i, 0, 0)),
                  pl.BlockSpec((1, hidden), lambda i: (0, 0)),
                  pl.BlockSpec((hidden, dh), lambda i: (0, 0)),
                  pl.BlockSpec((nc, dh), lambda i: (0, 0)),
                  pl.BlockSpec((nc, dh), lambda i: (0, 0))],
        out_specs=pl.BlockSpec((1, nc, dh), lambda i: (i, 0, 0)),
        out_shape=jax.ShapeDtypeStruct((bsz * groups, nc, dh), BF16),
        compiler_params=_params("parallel"),
        name="nsa_cmp_finish",
    )(z.reshape(bsz * groups, nc, 2 * hidden), bias, w2.astype(BF16), cc, ss)


def _nsa_cmp_select_kernel(q_ref, kc_ref, vc_ref, ov_ref, oc_ref, sel_ref, *, tq, rep, dh, topn):
    qi = pl.program_id(2)
    kc = kc_ref[0]
    vc = vc_ref[0]
    nc = kc.shape[0]
    n_sel = sel_ref.shape[-1]
    t = qi * tq + lax.broadcasted_iota(jnp.int32, (tq, nc), 0)
    cmp_end = lax.broadcasted_iota(jnp.int32, (tq, nc), 1) * NSA_CMP_STRIDE + (NSA_CMP_BLOCK - 1)
    visible = cmp_end <= t
    psum = jnp.zeros((tq, nc), F32)
    for r in range(rep):
        s = jnp.where(visible, _dot_nt(q_ref[0, :, r * dh:(r + 1) * dh], kc), NEG_BIG)
        m = jnp.max(s, axis=-1, keepdims=True)
        e = jnp.where(visible, jnp.exp(s - m), 0.0)
        den = jnp.sum(e, axis=-1, keepdims=True)
        p = e / jnp.where(den > 0, den, 1.0)
        oc_ref[0, :, r * dh:(r + 1) * dh] = jnp.dot(p.astype(BF16), vc, preferred_element_type=F32)
        psum = psum + p
    imp = _dot_hi(psum, ov_ref[...])
    blk = lax.broadcasted_iota(jnp.int32, (tq, n_sel), 1)
    cur = (qi * tq + lax.broadcasted_iota(jnp.int32, (tq, n_sel), 0)) // NSA_SEL_BLOCK
    forced = (blk == 0) | (blk == cur) | (blk == cur - 1)
    imp = jnp.where(forced, NSA_FORCED_SCORE, imp)
    imp = jnp.where(blk > cur, -jnp.inf, imp)
    sel = jnp.zeros((tq, n_sel), F32)
    for _ in range(topn):
        m = jnp.max(imp, axis=-1, keepdims=True)
        first = jnp.min(jnp.where(imp == m, blk, n_sel), axis=-1, keepdims=True)
        hit = blk == first
        sel = jnp.where(hit, 1.0, sel)
        imp = jnp.where(hit, -jnp.inf, imp)
    sel_ref[0, 0] = sel


def _flash_step(q_scr, k, v, mask, m_ref, l_ref, acc_ref, rep, tq):
    kb = k.shape[0]
    s = _dot_nt(q_scr[...], k).reshape(rep, tq, kb)
    s = jnp.where(mask[None], s, NEG_BIG)
    m_old = m_ref[...].reshape(rep, tq, -1)[:, :, :1]
    m_new = jnp.maximum(m_old, jnp.max(s, axis=-1, keepdims=True))
    p = jnp.where(mask[None], jnp.exp(s - m_new), 0.0)
    alpha = jnp.exp(m_old - m_new)
    l_old = l_ref[...].reshape(rep, tq, -1)[:, :, :1]
    l_new = alpha * l_old + jnp.sum(p, axis=-1, keepdims=True)
    pv = jnp.dot(p.reshape(rep * tq, kb).astype(BF16), v, preferred_element_type=F32)
    acc_ref[...] = (alpha * acc_ref[...].reshape(rep, tq, -1)).reshape(rep * tq, -1) + pv
    m_ref[...] = jnp.broadcast_to(m_new, (rep, tq, m_ref.shape[-1])).reshape(m_ref.shape)
    l_ref[...] = jnp.broadcast_to(l_new, (rep, tq, l_ref.shape[-1])).reshape(l_ref.shape)


def _flash_init(q_ref, q_scr, m_ref, l_ref, acc_ref, rep, tq, dh):
    for r in range(rep):
        q_scr[r * tq:(r + 1) * tq, :] = q_ref[0, :, r * dh:(r + 1) * dh]
    m_ref[...] = jnp.full_like(m_ref, NEG_BIG)
    l_ref[...] = jnp.zeros_like(l_ref)
    acc_ref[...] = jnp.zeros_like(acc_ref)


def _flash_result(l_ref, acc_ref):
    l = l_ref[...][:, :1]
    return acc_ref[...] / jnp.where(l > 0, l, 1.0)


def _nsa_select_kernel(q_ref, k_ref, v_ref, sel_ref, o_ref, q_scr, m_ref, l_ref, acc_ref, *, tq, kb, rep, dh):
    qi = pl.program_id(2)
    kj = pl.program_id(3)

    @pl.when(kj == 0)
    def _():
        _flash_init(q_ref, q_scr, m_ref, l_ref, acc_ref, rep, tq, dh)

    @pl.when(kj * kb <= qi * tq + tq - 1)
    def _():
        sel = sel_ref[0, 0]
        blk = lax.broadcasted_iota(jnp.int32, sel.shape, 1)
        kpos = kj * kb + lax.broadcasted_iota(jnp.int32, (tq, kb), 1)
        t = qi * tq + lax.broadcasted_iota(jnp.int32, (tq, kb), 0)
        chosen = jnp.zeros((tq, kb), F32)
        for i in range(kb // NSA_SEL_BLOCK):
            col = jnp.sum(jnp.where(blk == kj * (kb // NSA_SEL_BLOCK) + i, sel, 0.0), axis=-1, keepdims=True)
            in_blk = (kpos - kj * kb) // NSA_SEL_BLOCK == i
            chosen = jnp.where(in_blk, col, chosen)
        mask = (chosen > 0) & (kpos <= t)
        _flash_step(q_scr, k_ref[0], v_ref[0].astype(BF16), mask, m_ref, l_ref, acc_ref, rep, tq)

    @pl.when(kj == pl.num_programs(3) - 1)
    def _():
        out = _flash_result(l_ref, acc_ref)
        for r in range(rep):
            o_ref[0, :, r * dh:(r + 1) * dh] = out[r * tq:(r + 1) * tq, :]


def _nsa_window_kernel(q_ref, k_ref, v_ref, oc_ref, os_ref, g_ref, o_ref, q_scr, m_ref, l_ref, acc_ref,
                       *, tq, kb, rep, dh, window, n_steps):
    qi = pl.program_id(2)
    w = pl.program_id(3)
    kblk = qi * (tq // kb) - (n_steps - tq // kb) + w

    @pl.when(w == 0)
    def _():
        _flash_init(q_ref, q_scr, m_ref, l_ref, acc_ref, rep, tq, dh)

    @pl.when(kblk >= 0)
    def _():
        kpos = kblk * kb + lax.broadcasted_iota(jnp.int32, (tq, kb), 1)
        t = qi * tq + lax.broadcasted_iota(jnp.int32, (tq, kb), 0)
        mask = (kpos <= t) & (kpos > t - window)
        _flash_step(q_scr, k_ref[0], v_ref[0].astype(BF16), mask, m_ref, l_ref, acc_ref, rep, tq)

    @pl.when(w == n_steps - 1)
    def _():
        out = _flash_result(l_ref, acc_ref)
        gates = g_ref[0, 0]
        for r in range(rep):
            sl = slice(r * dh, (r + 1) * dh)
            o = (gates[:, 3 * r:3 * r + 1] * oc_ref[0, :, sl] + gates[:, 3 * r + 1:3 * r + 2] * os_ref[0, :, sl]
                 + gates[:, 3 * r + 2:3 * r + 3] * out[r * tq:(r + 1) * tq, :])
            o_ref[0, :, sl] = o.astype(o_ref.dtype)


def nsa_mixer(u, h, w, bsz, s_len):
    t, d = u.shape
    dh, groups = NSA_HEAD_DIM, NSA_N_KV
    n_heads = d // dh
    rep = n_heads // groups
    kvw = groups * dh
    qw = n_heads * dh
    main_w = qw + 6 * kvw
    scale = dh ** -0.5
    tq = kb = min(128, s_len)
    nq = s_len // tq
    n_sel = s_len // NSA_SEL_BLOCK
    topn = min(NSA_TOPK, n_sel)
    w_in = w["nsa_w_in"]
    proj = matmul_ws(u, [(w_in, 0)], main_w, name="nsa_in").reshape(bsz, s_len, main_w)
    gates = matmul(u, [(w_in[:, main_w:].astype(BF16), 0)], w_in.shape[1] - main_w, epilogue=_ep_sigmoid,
                   name="nsa_gates")
    gates = jnp.transpose(gates.reshape(bsz, s_len, groups, rep * 3), (0, 2, 1, 3))
    slot = lambda j: (qw + j * kvw) // dh
    roped = nsa_rope(proj, n_heads, [slot(2) + g for g in range(groups)] + [slot(4) + g for g in range(groups)],
                     dh, scale)
    kc = nsa_compress(proj[..., qw:qw + kvw], w["nsa_cmp_pos_k"], w["nsa_cmp_k_w1"], w["nsa_cmp_k_w2"],
                      bsz, s_len, groups, dh, True)
    vc = nsa_compress(proj[..., qw + kvw:qw + 2 * kvw], w["nsa_cmp_pos_v"], w["nsa_cmp_v_w1"], w["nsa_cmp_v_w2"],
                      bsz, s_len, groups, dh, False)
    nc = kc.shape[1]
    cs = jnp.arange(nc)[:, None] * NSA_CMP_STRIDE
    ss = jnp.arange(n_sel)[None, :] * NSA_SEL_BLOCK
    overlap = jnp.clip(jnp.minimum(cs + NSA_CMP_BLOCK, ss + NSA_SEL_BLOCK) - jnp.maximum(cs, ss), 0, None)
    overlap = overlap.astype(F32) / NSA_CMP_BLOCK

    q_spec3 = pl.BlockSpec((1, tq, rep * dh), lambda b_, g, i: (b_, i, g))
    o_c, sel = pl.pallas_call(
        functools.partial(_nsa_cmp_select_kernel, tq=tq, rep=rep, dh=dh, topn=topn),
        grid=(bsz, groups, nq),
        in_specs=[q_spec3,
                  pl.BlockSpec((1, nc, dh), lambda b_, g, i: (b_ * groups + g, 0, 0)),
                  pl.BlockSpec((1, nc, dh), lambda b_, g, i: (b_ * groups + g, 0, 0)),
                  pl.BlockSpec((nc, n_sel), lambda b_, g, i: (0, 0))],
        out_specs=[q_spec3, pl.BlockSpec((1, 1, tq, n_sel), lambda b_, g, i: (b_, g, i, 0))],
        out_shape=[jax.ShapeDtypeStruct((bsz, s_len, qw), F32),
                   jax.ShapeDtypeStruct((bsz, groups, s_len, n_sel), F32)],
        compiler_params=_params("parallel", "parallel", "parallel"),
        name="nsa_cmp_select",
    )(roped, kc, vc, overlap)

    q_spec = pl.BlockSpec((1, tq, rep * dh), lambda b_, g, i, j: (b_, i, g))
    flash_scratch = [pltpu.VMEM((rep * tq, dh), BF16), pltpu.VMEM((rep * tq, LANES), F32),
                     pltpu.VMEM((rep * tq, LANES), F32), pltpu.VMEM((rep * tq, dh), F32)]
    last_kb = lambda i: (i * tq + tq - 1) // kb
    o_s = pl.pallas_call(
        functools.partial(_nsa_select_kernel, tq=tq, kb=kb, rep=rep, dh=dh),
        grid=(bsz, groups, nq, s_len // kb),
        in_specs=[q_spec,
                  pl.BlockSpec((1, kb, dh), lambda b_, g, i, j: (b_, jnp.minimum(j, last_kb(i)), n_heads + g)),
                  pl.BlockSpec((1, kb, dh), lambda b_, g, i, j: (b_, jnp.minimum(j, last_kb(i)), slot(3) + g)),
                  pl.BlockSpec((1, 1, tq, n_sel), lambda b_, g, i, j: (b_, g, i, 0))],
        out_specs=q_spec,
        out_shape=jax.ShapeDtypeStruct((bsz, s_len, qw), F32),
        scratch_shapes=flash_scratch,
        compiler_params=_params("parallel", "parallel", "parallel", "arbitrary"),
        name="nsa_select_attn",
    )(roped, roped, proj, sel)

    n_steps = NSA_WINDOW // kb + tq // kb
    win_blk = lambda i, j: jnp.maximum(i * (tq // kb) - (n_steps - tq // kb) + j, 0)
    o = pl.pallas_call(
        functools.partial(_nsa_window_kernel, tq=tq, kb=kb, rep=rep, dh=dh, window=NSA_WINDOW, n_steps=n_steps),
        grid=(bsz, groups, nq, n_steps),
        in_specs=[q_spec,
                  pl.BlockSpec((1, kb, dh), lambda b_, g, i, j: (b_, win_blk(i, j), n_heads + groups + g)),
                  pl.BlockSpec((1, kb, dh), lambda b_, g, i, j: (b_, win_blk(i, j), slot(5) + g)),
                  q_spec, q_spec,
                  pl.BlockSpec((1, 1, tq, rep * 3), lambda b_, g, i, j: (b_, g, i, 0))],
        out_specs=q_spec,
        out_shape=jax.ShapeDtypeStruct((bsz, s_len, qw), BF16),
        scratch_shapes=flash_scratch,
        compiler_params=_params("parallel", "parallel", "parallel", "arbitrary"),
        name="nsa_window_attn",
    )(roped, roped, proj, o_c, o_s, gates)
    return matmul_ws(o.reshape(t, qw), [(w["nsa_w_out"], 0)], d, epilogue=_ep_residual, extras=[(h, "mn")],
                     name="nsa_out")


def _rope_t_kernel(x_ref, cc_ref, ss_ref, o_ref, *, n_rope, scale):
    x = x_ref[0]
    roped = (x * cc_ref[...] + pltpu.roll(x, x.shape[-1] // 2, 1) * ss_ref[...]) * scale
    out = jnp.where(pl.program_id(2) < n_rope, roped, x)
    o_ref[0] = out.T.astype(o_ref.dtype)


def nsa_rope_t(proj, slots, n_rope, dh, scale, tb=512):
    bsz, s_len, _ = proj.shape
    tb = min(tb, s_len)
    cc, ss = _rope_tables(jnp.arange(s_len), dh)
    table = jnp.asarray(slots, jnp.int32)
    grid_spec = pltpu.PrefetchScalarGridSpec(
        num_scalar_prefetch=1,
        grid=(bsz, s_len // tb, len(slots)),
        in_specs=[pl.BlockSpec((1, tb, dh), lambda b_, t, j, tab: (b_, t, tab[j])),
                  pl.BlockSpec((tb, dh), lambda b_, t, j, tab: (t, 0)),
                  pl.BlockSpec((tb, dh), lambda b_, t, j, tab: (t, 0))],
        out_specs=pl.BlockSpec((1, dh, tb), lambda b_, t, j, tab: (b_, j, t)),
    )
    kern = lambda tab, x_ref, cc_ref, ss_ref, o_ref: _rope_t_kernel(x_ref, cc_ref, ss_ref, o_ref,
                                                                   n_rope=n_rope, scale=scale)
    return pl.pallas_call(
        kern,
        grid_spec=grid_spec,
        out_shape=jax.ShapeDtypeStruct((bsz, len(slots) * dh, s_len), BF16),
        compiler_params=_params("parallel", "parallel", "arbitrary"),
        name="nsa_rope_t",
    )(table, proj, cc, ss)


def _cmp_finish_t_kernel(z_ref, bias_ref, w2_ref, o_ref, *, hidden):
    z = z_ref[0]
    nc = z.shape[0]
    nxt = pltpu.roll(z[:, hidden:], nc - 1, 0)
    hid = _silu(z[:, :hidden] + nxt + bias_ref[...])
    o_ref[0] = _dot_nt(w2_ref[...], hid.astype(BF16)).astype(o_ref.dtype)


def _nsa_cmp_select_t_kernel(q_ref, kc_ref, vc_ref, ov_ref, oc_ref, sel_ref, *, tq, rep, dh, topn):
    qi = pl.program_id(2)
    kc = kc_ref[0]
    vct = vc_ref[0]
    nc = kc.shape[0]
    n_sel = sel_ref.shape[2]
    t = qi * tq + lax.broadcasted_iota(jnp.int32, (nc, tq), 1)
    cmp_end = lax.broadcasted_iota(jnp.int32, (nc, tq), 0) * NSA_CMP_STRIDE + (NSA_CMP_BLOCK - 1)
    visible = cmp_end <= t
    s = [jnp.where(visible, jnp.dot(kc, q_ref[0, r * dh:(r + 1) * dh, :], preferred_element_type=F32), NEG_BIG)
         for r in range(rep)]
    e = [jnp.where(visible, jnp.exp(x - jnp.max(x, axis=0, keepdims=True)), 0.0) for x in s]
    den = [jnp.sum(x, axis=0, keepdims=True) for x in e]
    p = [e[r] / jnp.where(den[r] > 0, den[r], 1.0) for r in range(rep)]
    for r in range(rep):
        oc_ref[0, r * dh:(r + 1) * dh, :] = jnp.dot(vct, p[r].astype(BF16), preferred_element_type=F32)
    psum = p[0]
    for r in range(1, rep):
        psum = psum + p[r]
    imp = _dot_hi(ov_ref[...], psum)
    blk = lax.broadcasted_iota(jnp.int32, (n_sel, tq), 0)
    cur = (qi * tq + lax.broadcasted_iota(jnp.int32, (n_sel, tq), 1)) // NSA_SEL_BLOCK
    forced = (blk == 0) | (blk == cur) | (blk == cur - 1)
    imp = jnp.where(forced, NSA_FORCED_SCORE, imp)
    imp = jnp.where(blk > cur, -jnp.inf, imp)
    sel = jnp.zeros((n_sel, tq), F32)
    for _ in range(topn):
        m = jnp.max(imp, axis=0, keepdims=True)
        first = jnp.min(jnp.where(imp == m, blk, n_sel), axis=0, keepdims=True)
        hit = blk == first
        sel = jnp.where(hit, 1.0, sel)
        imp = jnp.where(hit, -jnp.inf, imp)
    sel_ref[0, 0] = sel


def _flash_t_init(m_ref, l_ref, acc_ref):
    m_ref[...] = jnp.full_like(m_ref, NEG_BIG)
    l_ref[...] = jnp.zeros_like(l_ref)
    acc_ref[...] = jnp.zeros_like(acc_ref)


def _flash_t_step(q_ref, k, vt, mask, m_ref, l_ref, acc_ref, rep, dh):
    hs = range(rep)
    s = [jnp.where(mask, jnp.dot(k, q_ref[0, r * dh:(r + 1) * dh, :], preferred_element_type=F32), NEG_BIG)
         for r in hs]
    m_old = [m_ref[r] for r in hs]
    m_new = [jnp.maximum(m_old[r], jnp.max(s[r], axis=0, keepdims=True)) for r in hs]
    p = [jnp.exp(s[r] - m_new[r]) for r in hs]
    alpha = [jnp.exp(m_old[r] - m_new[r]) for r in hs]
    pv = [jnp.dot(vt, p[r].astype(BF16), preferred_element_type=F32) for r in hs]
    for r in hs:
        m_ref[r] = m_new[r]
        l_ref[r] = alpha[r] * l_ref[r] + jnp.sum(p[r], axis=0, keepdims=True)
        acc_ref[r] = acc_ref[r] * alpha[r] + pv[r]


def _nsa_select_t_kernel(qi_ref, kj_ref, q_ref, k_ref, vt_ref, sel_ref, o_ref, m_ref, l_ref, acc_ref,
                         *, tq, kb, rep, dh):
    pair = pl.program_id(2)
    qi = qi_ref[pair]
    kj = kj_ref[pair]

    @pl.when(kj == 0)
    def _():
        _flash_t_init(m_ref, l_ref, acc_ref)

    kpos = kj * kb + lax.broadcasted_iota(jnp.int32, (kb, tq), 0)
    t = qi * tq + lax.broadcasted_iota(jnp.int32, (kb, tq), 1)
    per = kb // NSA_SEL_BLOCK
    chosen = jnp.zeros((kb, tq), F32)
    for i in range(per):
        row = sel_ref[0, 0, pl.ds(kj * per + i, 1), :]
        chosen = jnp.where((kpos - kj * kb) // NSA_SEL_BLOCK == i, row, chosen)
    mask = (chosen > 0) & (kpos <= t)
    _flash_t_step(q_ref, k_ref[0], vt_ref[0], mask, m_ref, l_ref, acc_ref, rep, dh)

    @pl.when(kj * kb + kb > qi * tq + tq - 1)
    def _():
        for r in range(rep):
            l = l_ref[r]
            o_ref[0, r * dh:(r + 1) * dh, :] = acc_ref[r] / jnp.where(l > 0, l, 1.0)


def _nsa_window_t_kernel(q_ref, k_ref, vt_ref, oc_ref, os_ref, g_ref, o_ref, m_ref, l_ref, acc_ref,
                         *, tq, kb, rep, dh, window, n_steps):
    qi = pl.program_id(2)
    w = pl.program_id(3)
    kblk = qi * (tq // kb) - (n_steps - tq // kb) + w

    @pl.when(w == 0)
    def _():
        _flash_t_init(m_ref, l_ref, acc_ref)

    @pl.when(kblk >= 0)
    def _():
        kpos = kblk * kb + lax.broadcasted_iota(jnp.int32, (kb, tq), 0)
        t = qi * tq + lax.broadcasted_iota(jnp.int32, (kb, tq), 1)
        mask = (kpos <= t) & (kpos > t - window)
        _flash_t_step(q_ref, k_ref[0], vt_ref[0], mask, m_ref, l_ref, acc_ref, rep, dh)

    @pl.when(w == n_steps - 1)
    def _():
        gates = g_ref[0, 0]
        for r in range(rep):
            rows = slice(r * dh, (r + 1) * dh)
            l = l_ref[r]
            o_w = acc_ref[r] / jnp.where(l > 0, l, 1.0)
            o = (gates[3 * r:3 * r + 1, :] * oc_ref[0, rows, :] + gates[3 * r + 1:3 * r + 2, :] * os_ref[0, rows, :]
                 + gates[3 * r + 2:3 * r + 3, :] * o_w)
            o_ref[0, :, rows] = o.T.astype(o_ref.dtype)


def nsa_mixer_t(u, h, w, bsz, s_len):
    t, d = u.shape
    dh, groups = NSA_HEAD_DIM, NSA_N_KV
    n_heads = d // dh
    rep = n_heads // groups
    kvw = groups * dh
    qw = n_heads * dh
    main_w = qw + 6 * kvw
    scale = dh ** -0.5
    tq = kb = min(128, s_len)
    nq = s_len // tq
    n_sel = s_len // NSA_SEL_BLOCK
    topn = min(NSA_TOPK, n_sel)
    w_in = w["nsa_w_in"]
    proj = matmul_ws(u, [(w_in, 0)], main_w, name="nsa_in").reshape(bsz, s_len, main_w)
    gates = matmul(u, [(w_in[:, main_w:].astype(BF16), 0)], w_in.shape[1] - main_w, epilogue=_ep_sigmoid,
                   name="nsa_gates")
    gates = jnp.transpose(gates.reshape(bsz, s_len, groups, rep * 3), (0, 2, 3, 1))
    slot = lambda j: (qw + j * kvw) // dh
    qvt = nsa_rope_t(proj, list(range(n_heads)) + [slot(3) + g for g in range(groups)]
                     + [slot(5) + g for g in range(groups)], n_heads, dh, scale)
    k_rot = nsa_rope(proj, 0, [slot(2) + g for g in range(groups)] + [slot(4) + g for g in range(groups)], dh, 1.0)
    kc = nsa_compress(proj[..., qw:qw + kvw], w["nsa_cmp_pos_k"], w["nsa_cmp_k_w1"], w["nsa_cmp_k_w2"],
                      bsz, s_len, groups, dh, True)
    vct = nsa_compress(proj[..., qw + kvw:qw + 2 * kvw], w["nsa_cmp_pos_v"], w["nsa_cmp_v_w1"], w["nsa_cmp_v_w2"],
                       bsz, s_len, groups, dh, False, transpose_out=True)
    nc = kc.shape[1]
    cs = jnp.arange(nc)[None, :] * NSA_CMP_STRIDE
    ss = jnp.arange(n_sel)[:, None] * NSA_SEL_BLOCK
    overlap_t = jnp.clip(jnp.minimum(cs + NSA_CMP_BLOCK, ss + NSA_SEL_BLOCK) - jnp.maximum(cs, ss), 0, None)
    overlap_t = overlap_t.astype(F32) / NSA_CMP_BLOCK

    qt_spec3 = pl.BlockSpec((1, rep * dh, tq), lambda b_, g, i: (b_, g, i))
    o_c, sel = pl.pallas_call(
        functools.partial(_nsa_cmp_select_t_kernel, tq=tq, rep=rep, dh=dh, topn=topn),
        grid=(bsz, groups, nq),
        in_specs=[qt_spec3,
                  pl.BlockSpec((1, nc, dh), lambda b_, g, i: (b_ * groups + g, 0, 0)),
                  pl.BlockSpec((1, dh, nc), lambda b_, g, i: (b_ * groups + g, 0, 0)),
                  pl.BlockSpec((n_sel, nc), lambda b_, g, i: (0, 0))],
        out_specs=[qt_spec3, pl.BlockSpec((1, 1, n_sel, tq), lambda b_, g, i: (b_, g, 0, i))],
        out_shape=[jax.ShapeDtypeStruct((bsz, qw, s_len), F32),
                   jax.ShapeDtypeStruct((bsz, groups, n_sel, s_len), F32)],
        compiler_params=_params("parallel", "parallel", "parallel"),
        name="nsa_cmp_select",
    )(qvt, kc, vct, overlap_t)

    flash_scratch = lambda n: [pltpu.VMEM((rep, 1, n), F32), pltpu.VMEM((rep, 1, n), F32),
                               pltpu.VMEM((rep, dh, n), F32)]
    tqs = min(2 * tq, s_len)
    pairs = [(i, j) for i in range(s_len // tqs) for j in range((i * tqs + tqs - 1) // kb + 1)]
    qi_of = jnp.asarray([pr[0] for pr in pairs], jnp.int32)
    kj_of = jnp.asarray([pr[1] for pr in pairs], jnp.int32)
    o_s = pl.pallas_call(
        functools.partial(_nsa_select_t_kernel, tq=tqs, kb=kb, rep=rep, dh=dh),
        grid_spec=pltpu.PrefetchScalarGridSpec(
            num_scalar_prefetch=2,
            grid=(bsz, groups, len(pairs)),
            in_specs=[pl.BlockSpec((1, rep * dh, tqs), lambda b_, g, pr, qi, kj: (b_, g, qi[pr])),
                      pl.BlockSpec((1, kb, dh), lambda b_, g, pr, qi, kj: (b_, kj[pr], g)),
                      pl.BlockSpec((1, dh, kb), lambda b_, g, pr, qi, kj: (b_, n_heads + g, kj[pr])),
                      pl.BlockSpec((1, 1, n_sel, tqs), lambda b_, g, pr, qi, kj: (b_, g, 0, qi[pr]))],
            out_specs=pl.BlockSpec((1, rep * dh, tqs), lambda b_, g, pr, qi, kj: (b_, g, qi[pr])),
            scratch_shapes=flash_scratch(tqs)),
        out_shape=jax.ShapeDtypeStruct((bsz, qw, s_len), F32),
        compiler_params=_params("parallel", "parallel", "arbitrary"),
        name="nsa_select_attn",
    )(qi_of, kj_of, qvt, k_rot, qvt, sel)

    n_steps = NSA_WINDOW // kb + tqs // kb
    win_blk = lambda i, j: jnp.maximum(i * (tqs // kb) - (n_steps - tqs // kb) + j, 0)
    qt_spec = pl.BlockSpec((1, rep * dh, tqs), lambda b_, g, i, j: (b_, g, i))
    o = pl.pallas_call(
        functools.partial(_nsa_window_t_kernel, tq=tqs, kb=kb, rep=rep, dh=dh, window=NSA_WINDOW, n_steps=n_steps),
        grid=(bsz, groups, s_len // tqs, n_steps),
        in_specs=[qt_spec,
                  pl.BlockSpec((1, kb, dh), lambda b_, g, i, j: (b_, win_blk(i, j), groups + g)),
                  pl.BlockSpec((1, dh, kb), lambda b_, g, i, j: (b_, n_heads + groups + g, win_blk(i, j))),
                  qt_spec, qt_spec,
                  pl.BlockSpec((1, 1, rep * 3, tqs), lambda b_, g, i, j: (b_, g, 0, i))],
        out_specs=pl.BlockSpec((1, tqs, rep * dh), lambda b_, g, i, j: (b_, i, g)),
        out_shape=jax.ShapeDtypeStruct((bsz, s_len, qw), BF16),
        scratch_shapes=flash_scratch(tqs),
        compiler_params=_params("parallel", "parallel", "parallel", "arbitrary"),
        name="nsa_window_attn",
    )(qvt, k_rot, qvt, o_c, o_s, gates)
    return matmul_ws(o.reshape(t, qw), [(w["nsa_w_out"], 0)], d, epilogue=_ep_residual, extras=[(h, "mn")],
                     name="nsa_out")


_MATMUL_WEIGHTS = ("pl_proj", "rw_w1", "rw_w2", "rw_a1", "rw_a2", "rw_g1", "rw_g2", "moe1_w_out", "moe3_w_out")


def kernel(x, p, norm_mix, norm_ffn, norm_pl, pl_proj, pl_gate, norm_final, mb_w_in, mb_conv_w, mb_conv_b, mb_dt_bias, mb_a_log, mb_d_skip, mb_norm_w, mb_w_out, nsa_w_in, nsa_cmp_pos_k, nsa_cmp_pos_v, nsa_cmp_k_w1, nsa_cmp_k_w2, nsa_cmp_v_w1, nsa_cmp_v_w2, nsa_w_out, hg_w_in, hg_lb_logits, hg_norm_w, hg_w_out, rw_mu, rw_w_rkv, rw_w0, rw_w1, rw_w2, rw_a0, rw_a1, rw_a2, rw_g1, rw_g2, rw_k_k, rw_k_a, rw_r_k, rw_ln_w, rw_ln_b, rw_w_out, ffn0_w_in, ffn0_w_out, moe1_router, moe1_w_in, moe1_w_out, ffn2_w_in, ffn2_w_out, moe3_router, moe3_w_in, moe3_w_out):
    w = dict(locals())
    for name in _MATMUL_WEIGHTS:
        w[name] = w[name].astype(BF16)
    bsz, s_len, d = x.shape
    depth = p.shape[0]
    t = bsz * s_len
    lb_all = jax.nn.softmax(hg_lb_logits.astype(F32), axis=0)
    lb_all = jnp.cumsum(lb_all, axis=0) - lb_all[0]
    dense = [(w["ffn0_w_in"], w["ffn0_w_out"]), (w["ffn2_w_in"], w["ffn2_w_out"])]
    moe = [(moe1_router, w["moe1_w_in"], w["moe1_w_out"]), (moe3_router, w["moe3_w_in"], w["moe3_w_out"])]
    p_bf = p.reshape(depth, t, p.shape[-1]).astype(BF16)
    h = x.reshape(t, d)
    for i in range(depth):
        kind = i % 4
        if kind == 0:
            h = mamba2_mixer(rmsnorm(h, norm_mix[i]), h, w, bsz, s_len)
        elif kind == 1:
            h = nsa_mixer_t(rmsnorm(h, norm_mix[i]), h, w, bsz, s_len)
        elif kind == 2:
            h = hgrn2_mixer(rmsnorm(h, norm_mix[i]), h, w, lb_all[i], bsz, s_len)
        else:
            h = rwkv7_mixer(rmsnorm(h, norm_mix[i], out_dtype=F32), h, w, bsz, s_len)
        v = rmsnorm(h, norm_ffn[i])
        if i % 2 == 0:
            h = dense_ffn(v, h, *dense[i // 2])
        else:
            h = moe_ffn_routed(v, h, *moe[i // 2])
        h = ple_gate(h, p_bf[i], norm_pl[i], w["pl_proj"][i], pl_gate, i)
    return rmsnorm(h, norm_final, out_dtype=F32).reshape(bsz, s_len, d)
```

```python
import functools
import math

import jax
import jax.numpy as jnp
from jax import lax
from jax.experimental import pallas as pl
from jax.experimental.pallas import tpu as pltpu

F32 = jnp.float32
BF16 = jnp.bfloat16

NORM_EPS = 1e-6
ROPE_THETA = 10000.0

V7X_VMEM_BYTES = 64 * 1024 * 1024
VMEM_LIMIT_BYTES = V7X_VMEM_BYTES - 8 * 1024 * 1024
LANES = 128

MB_HEAD_DIM = 64
MB_N_GROUPS = 8
MB_D_STATE = 128
MB_CONV = 4
MB_CHUNK = 128

NSA_HEAD_DIM = 128
NSA_N_KV = 4
NSA_CMP_BLOCK = 32
NSA_CMP_STRIDE = 16
NSA_SEL_BLOCK = 64
NSA_TOPK = 16
NSA_WINDOW = 512
NSA_FORCED_SCORE = 1e9

HG_HEAD_DIM = 128
HG_CHUNK = 32

RW_HEAD_DIM = 64
RW_LN_EPS = 64e-5
RW_CHUNK = 128

MOE_TOPK = 2


def _params(*semantics):
    return pltpu.CompilerParams(dimension_semantics=semantics, vmem_limit_bytes=VMEM_LIMIT_BYTES)


def _pick(n, target):
    if n <= target:
        return n
    for c in range(target, 0, -1):
        if n % c == 0:
            return c
    return n


def _silu(x):
    return x * jax.nn.sigmoid(x)


def _rmsnorm_kernel(x_ref, g_ref, o_ref):
    x = x_ref[...]
    ms = jnp.mean(x * x, axis=-1, keepdims=True)
    o_ref[...] = (x * lax.rsqrt(ms + NORM_EPS) * g_ref[...]).astype(o_ref.dtype)


def rmsnorm(x, gain, out_dtype=BF16, name="rmsnorm"):
    m, d = x.shape
    bm = _pick(m, 256)
    return pl.pallas_call(
        _rmsnorm_kernel,
        grid=(m // bm,),
        in_specs=[pl.BlockSpec((bm, d), lambda i: (i, 0)), pl.BlockSpec((1, d), lambda i: (0, 0))],
        out_specs=pl.BlockSpec((bm, d), lambda i: (i, 0)),
        out_shape=jax.ShapeDtypeStruct((m, d), out_dtype),
        compiler_params=_params("parallel"),
        name=name,
    )(x, gain.reshape(1, d).astype(F32))


def _mm_kernel(*refs, n_w, n_extra, nk, epilogue):
    x_ref = refs[0]
    w_refs = refs[1:1 + n_w]
    e_refs = refs[1 + n_w:1 + n_w + n_extra]
    o_ref = refs[1 + n_w + n_extra]
    acc_refs = refs[2 + n_w + n_extra:]
    x = x_ref[...]
    if nk == 1:
        accs = [jnp.dot(x, w[...], preferred_element_type=F32) for w in w_refs]
        o_ref[...] = epilogue(accs, [e[...] for e in e_refs]).astype(o_ref.dtype)
        return
    k = pl.program_id(2)

    @pl.when(k == 0)
    def _():
        for a in acc_refs:
            a[...] = jnp.zeros_like(a)

    for a, w in zip(acc_refs, w_refs):
        a[...] += jnp.dot(x, w[...], preferred_element_type=F32)

    @pl.when(k == nk - 1)
    def _():
        o_ref[...] = epilogue([a[...] for a in acc_refs], [e[...] for e in e_refs]).astype(o_ref.dtype)


def _first(accs, extras):
    return accs[0]


def matmul(x, ws, n_out, *, epilogue=_first, extras=(), out_dtype=F32, bm=1024, bn=512, bk=None, name="matmul"):
    m, kdim = x.shape
    bm = _pick(m, bm)
    bn = _pick(n_out, bn)
    if bk is None:
        bk = kdim if kdim <= 4096 else _pick(kdim, 4096)
    nk = kdim // bk
    assert kdim % bk == 0 and m % bm == 0 and n_out % bn == 0
    in_specs = [pl.BlockSpec((bm, bk), lambda i, j, k: (i, k))]
    args = [x]
    for w, off in ws:
        assert off % bn == 0 and w.shape[0] == kdim
        in_specs.append(pl.BlockSpec((bk, bn), functools.partial(lambda i, j, k, o: (k, j + o), o=off // bn)))
        args.append(w)
    for arr, kind in extras:
        if kind == "mn":
            in_specs.append(pl.BlockSpec((bm, bn), lambda i, j, k: (i, j)))
        elif kind == "m":
            in_specs.append(pl.BlockSpec((bm, arr.shape[1]), lambda i, j, k: (i, 0)))
        elif kind == "kn":
            in_specs.append(pl.BlockSpec((arr.shape[0], bn), lambda i, j, k: (0, j)))
        else:
            in_specs.append(pl.BlockSpec((1, bn), lambda i, j, k: (0, j)))
        args.append(arr)
    scratch = [pltpu.VMEM((bm, bn), F32) for _ in ws] if nk > 1 else []
    kern = functools.partial(_mm_kernel, n_w=len(ws), n_extra=len(extras), nk=nk, epilogue=epilogue)
    return pl.pallas_call(
        kern,
        grid=(m // bm, n_out // bn, nk),
        in_specs=in_specs,
        out_specs=pl.BlockSpec((bm, bn), lambda i, j, k: (i, j)),
        out_shape=jax.ShapeDtypeStruct((m, n_out), out_dtype),
        scratch_shapes=scratch,
        compiler_params=_params("parallel", "parallel", "arbitrary"),
        name=name,
    )(*args)


def _mm_ws_kernel(*refs, n_w, n_extra, epilogue):
    x_ref = refs[0]
    w_refs = refs[1:1 + n_w]
    e_refs = refs[1 + n_w:1 + n_w + n_extra]
    o_ref = refs[1 + n_w + n_extra]
    wb_refs = refs[2 + n_w + n_extra:]

    @pl.when(pl.program_id(1) == 0)
    def _():
        for w, wb in zip(w_refs, wb_refs):
            wb[...] = w[...].reshape(wb.shape).astype(BF16)

    x = x_ref[...]
    accs = [jnp.dot(x, wb[...], preferred_element_type=F32) for wb in wb_refs]
    o_ref[...] = epilogue(accs, [e[...] for e in e_refs]).astype(o_ref.dtype)


def matmul_ws(x, ws, n_out, *, epilogue=_first, extras=(), out_dtype=F32, bm=1024, bn=512, w_buffers=2,
              name="matmul_ws"):
    m, kdim = x.shape
    bm = _pick(m, bm)
    bn = _pick(n_out, bn)
    assert m % bm == 0 and n_out % bn == 0
    mode = {} if w_buffers == 2 else {"pipeline_mode": pl.Buffered(w_buffers)}
    in_specs = [pl.BlockSpec((bm, kdim), lambda j, i: (i, 0))]
    args = [x]
    for w, off in ws:
        if w.ndim == 3:
            e, o = off
            assert o % bn == 0 and w.shape[1] == kdim
            in_specs.append(pl.BlockSpec((1, kdim, bn), functools.partial(lambda j, i, e_, o_: (e_, 0, j + o_),
                                                                          e_=e, o_=o // bn), **mode))
        else:
            assert off % bn == 0 and w.shape[0] == kdim
            in_specs.append(pl.BlockSpec((kdim, bn), functools.partial(lambda j, i, o_: (0, j + o_), o_=off // bn),
                                         **mode))
        args.append(w)
    for arr, kind in extras:
        if kind == "mn":
            in_specs.append(pl.BlockSpec((bm, bn), lambda j, i: (i, j)))
        elif kind == "m":
            in_specs.append(pl.BlockSpec((bm, arr.shape[1]), lambda j, i: (i, 0)))
        elif kind == "kn":
            in_specs.append(pl.BlockSpec((arr.shape[0], bn), lambda j, i: (0, j)))
        else:
            in_specs.append(pl.BlockSpec((1, bn), lambda j, i: (0, j)))
        args.append(arr)
    kern = functools.partial(_mm_ws_kernel, n_w=len(ws), n_extra=len(extras), epilogue=epilogue)
    return pl.pallas_call(
        kern,
        grid=(n_out // bn, m // bm),
        in_specs=in_specs,
        out_specs=pl.BlockSpec((bm, bn), lambda j, i: (i, j)),
        out_shape=jax.ShapeDtypeStruct((m, n_out), out_dtype),
        scratch_shapes=[pltpu.VMEM((kdim, bn), BF16) for _ in ws],
        compiler_params=_params("parallel", "arbitrary"),
        name=name,
    )(*args)


def _ep_residual(accs, extras):
    return extras[0] + accs[0]


def _ep_swiglu(accs, extras):
    return _silu(accs[0]) * accs[1]


def _ep_bias(accs, extras):
    return accs[0] + extras[0]


def _ep_tanh(accs, extras):
    return jnp.tanh(accs[0])


def _ep_sigmoid(accs, extras):
    return jax.nn.sigmoid(accs[0])


def _ep_bias_sigmoid(accs, extras):
    return jax.nn.sigmoid(accs[0] + extras[0])


def _ep_rw_logdecay(accs, extras):
    w = -jax.nn.softplus(-(accs[0] + extras[0])) - 0.5
    return -jnp.exp(w)


def _ep_ple_gate(accs, extras):
    pp = jnp.dot(extras[1].astype(BF16), extras[2], preferred_element_type=F32)
    return extras[0] + pp * jax.nn.sigmoid(accs[0])


def _conv_silu_kernel(x_ref, w_ref, b_ref, o_ref, *, k_width):
    x = x_ref[0]
    row = lax.broadcasted_iota(jnp.int32, x.shape, 0)
    y = b_ref[...] + w_ref[k_width - 1:k_width, :] * x
    for j in range(k_width - 1):
        shift = k_width - 1 - j
        xs = jnp.where(row >= shift, pltpu.roll(x, shift, 0), 0.0)
        y = y + w_ref[j:j + 1, :] * xs
    o_ref[0] = _silu(y)


def conv_silu(x, w, b):
    bsz, s_len, c = x.shape
    cb = _pick(c, 256)
    k_width = w.shape[0]
    return pl.pallas_call(
        functools.partial(_conv_silu_kernel, k_width=k_width),
        grid=(bsz, c // cb),
        in_specs=[pl.BlockSpec((1, s_len, cb), lambda b_, j: (b_, 0, j)),
                  pl.BlockSpec((k_width, cb), lambda b_, j: (0, j)),
                  pl.BlockSpec((1, cb), lambda b_, j: (0, j))],
        out_specs=pl.BlockSpec((1, s_len, cb), lambda b_, j: (b_, 0, j)),
        out_shape=jax.ShapeDtypeStruct(x.shape, F32),
        compiler_params=_params("parallel", "parallel"),
        name="mamba_conv_silu",
    )(x, w, b.reshape(1, c))


def _cumsum_rows(x, n):
    row = lax.broadcasted_iota(jnp.int32, x.shape, 0)
    s = 1
    while s < n:
        x = x + jnp.where(row >= s, pltpu.roll(x, s, 0), 0.0)
        s *= 2
    return x


def _cumsum_lanes(x, n):
    col = lax.broadcasted_iota(jnp.int32, x.shape, 1)
    s = 1
    while s < n:
        x = x + jnp.where(col >= s, pltpu.roll(x, s, 1), 0.0)
        s *= 2
    return x


def _dot_nt(a, b):
    return lax.dot_general(a, b, (((1,), (1,)), ((), ())), preferred_element_type=F32)


def _dot_tn(a, b):
    return lax.dot_general(a, b, (((0,), (0,)), ((), ())), preferred_element_type=F32)


def _ssd_kernel(xs_ref, b_ref, c_ref, z_ref, dt_ref, dtt_ref, bias_r_ref, bias_c_ref, alog_r_ref, alog_c_ref,
                dskip_ref, normw_ref, o_ref, state_ref, y_ref, *, chunk, heads, p_dim):
    @pl.when(pl.program_id(2) == 0)
    def _():
        state_ref[...] = jnp.zeros_like(state_ref)

    dt = jax.nn.softplus(dt_ref[0, 0] + bias_r_ref[0])
    dtt = jax.nn.softplus(dtt_ref[0, 0] + bias_c_ref[0])
    a_cum = _cumsum_rows(dt * -jnp.exp(alog_r_ref[0]), chunk)
    a_cum_t = _cumsum_lanes(dtt * -jnp.exp(alog_c_ref[0]), chunk)
    xs = xs_ref[0]
    bmat = b_ref[0]
    cmat = c_ref[0].astype(BF16)
    cb = _dot_nt(cmat, bmat.astype(BF16))
    b_t = bmat.T.astype(BF16)
    li = lax.broadcasted_iota(jnp.int32, (chunk, chunk), 0)
    si = lax.broadcasted_iota(jnp.int32, (chunk, chunk), 1)
    causal = li >= si
    per = LANES // p_dim
    lane_seg = lax.broadcasted_iota(jnp.int32, (1, LANES), 1) // p_dim

    def pick(vals):
        out = vals[-1]
        for i in range(per - 2, -1, -1):
            out = jnp.where(lane_seg == i, vals[i], out)
        return out

    dot = functools.partial(jnp.dot, preferred_element_type=F32)
    es = range(heads)
    tiles = range(heads // per)
    head_row = lax.broadcasted_iota(jnp.int32, (heads, heads * LANES), 0)
    to_tile = jnp.where(lax.broadcasted_iota(jnp.int32, (heads, heads * LANES), 1) // LANES == head_row, 1.0, 0.0)
    cum_t = _dot_hi(a_cum, to_tile)
    of = lambda vals, i: [vals[i * per + j] for j in range(per)]
    tile = lambda x, i: x[:, i * LANES:(i + 1) * LANES]
    cum_c = jnp.concatenate([pick([tile(cum_t, e) for e in of(es, i)]) for i in tiles], axis=-1)
    dt_c = jnp.concatenate([pick([dt[:, e:e + 1] for e in of(es, i)]) for i in tiles], axis=-1)
    last_c = cum_c[chunk - 1:chunk, :]
    m = [(cb * jnp.exp(jnp.where(causal, tile(cum_t, e) - a_cum_t[e:e + 1, :], -jnp.inf))).astype(BF16) for e in es]
    xdt = xs * dt_c
    xdt_b = xdt.astype(BF16)
    xend_b = (xdt * jnp.exp(last_c - cum_c)).astype(BF16)
    grow = jnp.exp(cum_c)
    st_decay = jnp.exp(last_c)
    st = [state_ref[i] for i in tiles]
    y_in = [pick([dot(m[e], tile(xdt_b, i)) for e in of(es, i)]) for i in tiles]
    y_st = [dot(cmat, st[i].astype(BF16)) * tile(grow, i) for i in tiles]
    for i in tiles:
        state_ref[i] = st[i] * tile(st_decay, i) + dot(b_t, tile(xend_b, i))
        y_ref[:, i * LANES:(i + 1) * LANES] = y_in[i] + y_st[i]
    y = y_ref[...] + xs * dskip_ref[...]
    y = y * _silu(z_ref[0])
    ms = jnp.mean(y * y, axis=-1, keepdims=True)
    o_ref[0] = (y * lax.rsqrt(ms + NORM_EPS) * normw_ref[...]).astype(o_ref.dtype)


def ssd_scan(xbc, z, dt, dt_bias, a_log, d_skip, norm_w, *, chunk=MB_CHUNK):
    bsz, s_len, d_inner = z.shape
    n_heads = dt.shape[-1]
    n_state = MB_D_STATE
    groups = (xbc.shape[-1] - d_inner) // (2 * n_state)
    heads = n_heads // groups
    p_dim = d_inner // n_heads
    gw = heads * p_dim
    assert gw % LANES == 0 and d_inner % n_state == 0
    chunk = min(chunk, s_len)
    nc = s_len // chunk
    b_off = d_inner // n_state
    c_off = b_off + groups
    dt_g = jnp.transpose(dt.reshape(bsz, s_len, groups, heads), (0, 2, 1, 3))
    dt_gt = jnp.transpose(dt_g, (0, 1, 3, 2))
    kern = functools.partial(_ssd_kernel, chunk=chunk, heads=heads, p_dim=p_dim)
    per_group = lambda b_, g, c: (g, 0, 0)
    return pl.pallas_call(
        kern,
        grid=(bsz, groups, nc),
        in_specs=[pl.BlockSpec((1, chunk, gw), lambda b_, g, c: (b_, c, g)),
                  pl.BlockSpec((1, chunk, n_state), lambda b_, g, c: (b_, c, b_off + g)),
                  pl.BlockSpec((1, chunk, n_state), lambda b_, g, c: (b_, c, c_off + g)),
                  pl.BlockSpec((1, chunk, gw), lambda b_, g, c: (b_, c, g)),
                  pl.BlockSpec((1, 1, chunk, heads), lambda b_, g, c: (b_, g, c, 0)),
                  pl.BlockSpec((1, 1, heads, chunk), lambda b_, g, c: (b_, g, 0, c)),
                  pl.BlockSpec((1, 1, heads), per_group),
                  pl.BlockSpec((1, heads, 1), per_group),
                  pl.BlockSpec((1, 1, heads), per_group),
                  pl.BlockSpec((1, heads, 1), per_group),
                  pl.BlockSpec((1, gw), lambda b_, g, c: (0, g)),
                  pl.BlockSpec((1, gw), lambda b_, g, c: (0, g))],
        out_specs=pl.BlockSpec((1, chunk, gw), lambda b_, g, c: (b_, c, g)),
        out_shape=jax.ShapeDtypeStruct(z.shape, BF16),
        scratch_shapes=[pltpu.VMEM((gw // LANES, n_state, LANES), F32), pltpu.VMEM((chunk, gw), F32)],
        compiler_params=_params("parallel", "parallel", "arbitrary"),
        name="mamba_ssd",
    )(xbc, xbc, xbc, z, dt_g, dt_gt,
      dt_bias.reshape(groups, 1, heads), dt_bias.reshape(groups, heads, 1),
      a_log.reshape(groups, 1, heads), a_log.reshape(groups, heads, 1),
      jnp.repeat(d_skip, p_dim).reshape(1, d_inner), norm_w.reshape(1, d_inner))


def mamba2_mixer(u, h, w, bsz, s_len):
    d_inner = w["mb_w_out"].shape[0]
    n_heads = w["mb_dt_bias"].shape[0]
    w_in = w["mb_w_in"]
    xbc_w = w_in.shape[1] - d_inner - n_heads
    z = matmul_ws(u, [(w_in, 0)], d_inner, name="mb_in_z")
    xbc = matmul_ws(u, [(w_in, d_inner)], xbc_w, name="mb_in_xbc")
    dt = matmul_ws(u, [(w_in, d_inner + xbc_w)], n_heads, name="mb_in_dt")
    xbc = conv_silu(xbc.reshape(bsz, s_len, xbc_w), w["mb_conv_w"], w["mb_conv_b"])
    y = ssd_scan(xbc, z.reshape(bsz, s_len, d_inner), dt.reshape(bsz, s_len, n_heads),
                 w["mb_dt_bias"], w["mb_a_log"], w["mb_d_skip"], w["mb_norm_w"])
    return matmul_ws(y.reshape(bsz * s_len, d_inner), [(w["mb_w_out"], 0)], h.shape[1],
                     epilogue=_ep_residual, extras=[(h, "mn")], bm=512, w_buffers=1, name="mb_out")


def _seg_cumsum_rows(x, seg, reverse=False):
    n = x.shape[0]
    pos = lax.broadcasted_iota(jnp.int32, x.shape, 0) % seg
    s = 1
    while s < seg:
        if reverse:
            x = x + jnp.where(pos < seg - s, pltpu.roll(x, n - s, 0), 0.0)
        else:
            x = x + jnp.where(pos >= s, pltpu.roll(x, s, 0), 0.0)
        s *= 2
    return x


def _hgrn_kernel(q_ref, f_ref, i_ref, g_ref, lb_ref, nw_ref, o_ref, state_ref, *, sub, n_sub, heads, dk):
    @pl.when(pl.program_id(2) == 0)
    def _():
        state_ref[...] = jnp.zeros_like(state_ref)

    lb = lb_ref[...]
    nw = nw_ref[...]
    ti = lax.broadcasted_iota(jnp.int32, (sub, sub), 0)
    si = lax.broadcasted_iota(jnp.int32, (sub, sub), 1)
    causal = ti >= si
    f = lb + (1.0 - lb) * jax.nn.sigmoid(f_ref[0])
    lf = jnp.log(f)
    k = 1.0 - f
    b = _seg_cumsum_rows(lf, sub)
    to_end = _seg_cumsum_rows(lf, sub, reverse=True) - lf
    q_dec = (_silu(q_ref[0]) * jnp.exp(b)).astype(BF16)
    k_dec = (k * jnp.exp(-b)).astype(BF16)
    k_end = (k * jnp.exp(to_end)).astype(BF16)
    v = i_ref[0].astype(BF16)
    cs = range(n_sub)
    hs = range(heads)
    blk = lambda x, c, h: x[c * sub:(c + 1) * sub, h * dk:(h + 1) * dk]
    scores = [[jnp.where(causal, _dot_nt(blk(q_dec, c, h), blk(k_dec, c, h)), 0.0).astype(BF16) for h in hs]
              for c in cs]
    upd = [[_dot_tn(blk(v, c, h), blk(k_end, c, h)) for h in hs] for c in cs]
    states = []
    st = [state_ref[h] for h in hs]
    for c in cs:
        states.append(st)
        decay = jnp.exp(b[(c + 1) * sub - 1:(c + 1) * sub, :])
        st = [st[h] * decay[:, h * dk:(h + 1) * dk] + upd[c][h] for h in hs]
    for h in hs:
        state_ref[h] = st[h]
    for c in cs:
        rows = slice(c * sub, (c + 1) * sub)
        for h in hs:
            o = (jnp.dot(scores[c][h], blk(v, c, h), preferred_element_type=F32)
                 + _dot_nt(blk(q_dec, c, h), states[c][h].astype(BF16)))
            o = o * lax.rsqrt(jnp.mean(o * o, axis=-1, keepdims=True) + NORM_EPS) * nw
            cols = slice(h * dk, (h + 1) * dk)
            o_ref[0, rows, cols] = (o * _silu(g_ref[0, rows, cols])).astype(o_ref.dtype)


def hgrn2_scan(proj, lower_bound, norm_w, *, dk=HG_HEAD_DIM, sub=HG_CHUNK, tb=256, heads=4):
    bsz, s_len, d4 = proj.shape
    d = d4 // 4
    n_heads = d // dk
    tb = min(tb, s_len)
    heads = min(heads, n_heads)
    hw = heads * dk
    n_hb = n_heads // heads
    kern = functools.partial(_hgrn_kernel, sub=sub, n_sub=tb // sub, heads=heads, dk=dk)
    spec = lambda part: pl.BlockSpec((1, tb, hw), lambda b_, h_, t: (b_, t, part * n_hb + h_))
    return pl.pallas_call(
        kern,
        grid=(bsz, n_hb, s_len // tb),
        in_specs=[spec(0), spec(1), spec(2), spec(3),
                  pl.BlockSpec((1, hw), lambda b_, h_, t: (0, h_)),
                  pl.BlockSpec((1, dk), lambda b_, h_, t: (0, 0))],
        out_specs=pl.BlockSpec((1, tb, hw), lambda b_, h_, t: (b_, t, h_)),
        out_shape=jax.ShapeDtypeStruct((bsz, s_len, d), BF16),
        scratch_shapes=[pltpu.VMEM((heads, dk, dk), F32)],
        compiler_params=_params("parallel", "parallel", "arbitrary"),
        name="hgrn2_scan",
    )(proj, proj, proj, proj, lower_bound.reshape(1, d), norm_w.reshape(1, dk))


def hgrn2_mixer(u, h, w, lower_bound, bsz, s_len):
    d = h.shape[1]
    proj = matmul_ws(u, [(w["hg_w_in"], 0)], 4 * d, name="hg_in")
    o = hgrn2_scan(proj.reshape(bsz, s_len, 4 * d), lower_bound, w["hg_norm_w"])
    return matmul_ws(o.reshape(bsz * s_len, d), [(w["hg_w_out"], 0)], d,
                     epilogue=_ep_residual, extras=[(h, "mn")], name="hg_out")


def dense_ffn(v, h, w_in, w_out):
    f = w_out.shape[0]
    hid = matmul_ws(v, [(w_in, 0), (w_in, f)], f, epilogue=_ep_swiglu, out_dtype=BF16, bm=1024, bn=256,
                    name="ffn_in")
    return matmul_ws(hid, [(w_out, 0)], h.shape[1], epilogue=_ep_residual, extras=[(h, "mn")], bm=512, w_buffers=1,
                     name="ffn_out")


def _router_kernel(x_ref, r_ref, o_ref, *, n_experts):
    logits = jnp.dot(x_ref[...], r_ref[...], preferred_element_type=F32)
    lane = lax.broadcasted_iota(jnp.int32, logits.shape, 1)
    logits = jnp.where(lane < n_experts, logits, -jnp.inf)
    m1 = jnp.max(logits, axis=-1, keepdims=True)
    i1 = jnp.min(jnp.where(logits == m1, lane, LANES), axis=-1, keepdims=True)
    rest = jnp.where(lane == i1, -jnp.inf, logits)
    m2 = jnp.max(rest, axis=-1, keepdims=True)
    i2 = jnp.min(jnp.where(rest == m2, lane, LANES), axis=-1, keepdims=True)
    e2 = jnp.exp(m2 - m1)
    w1 = 1.0 / (1.0 + e2)
    o_ref[...] = jnp.where(lane == i1, w1, 0.0) + jnp.where(lane == i2, e2 * w1, 0.0)


def moe_router(v, router):
    m, d = v.shape
    n_experts = router.shape[1]
    r_pad = jnp.zeros((d, LANES), BF16).at[:, :n_experts].set(router.astype(BF16))
    bm = _pick(m, 512)
    return pl.pallas_call(
        functools.partial(_router_kernel, n_experts=n_experts),
        grid=(m // bm,),
        in_specs=[pl.BlockSpec((bm, d), lambda i: (i, 0)), pl.BlockSpec((d, LANES), lambda i: (0, 0))],
        out_specs=pl.BlockSpec((bm, LANES), lambda i: (i, 0)),
        out_shape=jax.ShapeDtypeStruct((m, LANES), F32),
        compiler_params=_params("parallel"),
        name="moe_router",
    )(v, r_pad)


def _moe_in_kernel(x_ref, wg_ref, wu_ref, c_ref, o_ref, wgb_ref, wub_ref, *, blocks_per_expert):
    @pl.when(pl.program_id(1) == 0)
    def _():
        wgb_ref[...] = wg_ref[0].astype(BF16)
        wub_ref[...] = wu_ref[0].astype(BF16)

    x = x_ref[...]
    g = jnp.dot(x, wgb_ref[...], preferred_element_type=F32)
    u = jnp.dot(x, wub_ref[...], preferred_element_type=F32)
    e = pl.program_id(0) // blocks_per_expert
    comb = c_ref[...]
    lane = lax.broadcasted_iota(jnp.int32, comb.shape, 1)
    scale = jnp.sum(jnp.where(lane == e, comb, 0.0), axis=-1, keepdims=True)
    o_ref[...] = (_silu(g) * u * scale).astype(o_ref.dtype)


def moe_ffn(v, h, router, w_in, w_out, *, bm=512, bn=512):
    m, d = v.shape
    n_experts, _, two_de = w_in.shape
    de = two_de // 2
    bm = _pick(m, bm)
    bn = _pick(de, bn)
    bpe = de // bn
    comb = moe_router(v, router)
    hid = pl.pallas_call(
        functools.partial(_moe_in_kernel, blocks_per_expert=bpe),
        grid=(n_experts * bpe, m // bm),
        in_specs=[pl.BlockSpec((bm, d), lambda j, i: (i, 0)),
                  pl.BlockSpec((1, d, bn), lambda j, i: (j // bpe, 0, j % bpe)),
                  pl.BlockSpec((1, d, bn), lambda j, i: (j // bpe, 0, j % bpe + bpe)),
                  pl.BlockSpec((bm, LANES), lambda j, i: (i, 0))],
        out_specs=pl.BlockSpec((bm, bn), lambda j, i: (i, j)),
        out_shape=jax.ShapeDtypeStruct((m, n_experts * de), BF16),
        scratch_shapes=[pltpu.VMEM((d, bn), BF16), pltpu.VMEM((d, bn), BF16)],
        compiler_params=_params("parallel", "arbitrary"),
        name="moe_in",
    )(v, w_in, w_in, comb)
    return matmul(hid, [(w_out.reshape(n_experts * de, d), 0)], d, epilogue=_ep_residual, extras=[(h, "mn")],
                  name="moe_out")


MOE_BLOCK = 1024
MOE_UNIT = 128
MOE_TILE = 512


def _moe_gather_kernel(x_ref, tok_ref, o_ref):
    tok = tok_ref[0]
    lane = lax.broadcasted_iota(jnp.int32, (tok.shape[0], x_ref.shape[0]), 1)
    onehot = jnp.where(tok == lane, 1.0, 0.0).astype(BF16)
    o_ref[...] = jnp.dot(onehot, x_ref[...], preferred_element_type=F32).astype(o_ref.dtype)


def _moe_expert_in_kernel(src_ref, exp_ref, first_ref, used_ref, *refs, per):
    x_refs = refs[:per]
    wg_ref, wu_ref, rw_ref, o_ref, wgb_ref, wub_ref, x_scr = refs[per:]
    t = pl.program_id(1)

    @pl.when(t < used_ref[0])
    def _():
        @pl.when(first_ref[t] == 1)
        def _():
            wgb_ref[...] = wg_ref[0].astype(BF16)
            wub_ref[...] = wu_ref[0].astype(BF16)

        unit = x_refs[0].shape[0]
        for i in range(per):
            x_scr[i * unit:(i + 1) * unit, :] = x_refs[i][...]
        x = x_scr[...]
        g = jnp.dot(x, wgb_ref[...], preferred_element_type=F32)
        u = jnp.dot(x, wub_ref[...], preferred_element_type=F32)
        o_ref[...] = (_silu(g) * u * rw_ref[...]).astype(o_ref.dtype)

    @pl.when(t >= used_ref[0])
    def _():
        o_ref[...] = jnp.zeros_like(o_ref)


def _moe_expert_out_kernel(exp_ref, first_ref, used_ref, hid_ref, w_ref, o_ref, wb_ref):
    t = pl.program_id(1)

    @pl.when(t < used_ref[0])
    def _():
        @pl.when(first_ref[t] == 1)
        def _():
            wb_ref[...] = w_ref[0].astype(BF16)

        o_ref[...] = jnp.dot(hid_ref[...], wb_ref[...], preferred_element_type=F32).astype(o_ref.dtype)

    @pl.when(pl.program_id(1) >= used_ref[0])
    def _():
        o_ref[...] = jnp.zeros_like(o_ref)


def _moe_scatter_kernel(dst_ref, h_ref, tok_ref, *refs, per):
    y_refs = refs[:per]
    o_ref, y_scr = refs[per:]

    @pl.when(pl.program_id(2) == 0)
    def _():
        o_ref[...] = h_ref[...]

    tok = tok_ref[0]
    row = lax.broadcasted_iota(jnp.int32, (o_ref.shape[0], tok.shape[1]), 0)
    onehot_t = jnp.where(tok == row, 1.0, 0.0).astype(BF16)
    unit = y_refs[0].shape[0]
    for i in range(per):
        y_scr[i * unit:(i + 1) * unit, :] = y_refs[i][...]
    o_ref[...] += jnp.dot(onehot_t, y_scr[...], preferred_element_type=F32)


def moe_ffn_routed(v, h, router, w_in, w_out, *, tb=MOE_BLOCK, unit=MOE_UNIT, tile=MOE_TILE, bn=512, bo=1024):
    m, d = v.shape
    n_experts, _, two_de = w_in.shape
    de = two_de // 2
    tb = min(tb, m)
    nb = m // tb
    per = tile // unit
    bn = _pick(de, bn)
    bo = _pick(d, bo)
    n_assign = MOE_TOPK * tb
    n_slots = -(-(n_assign // unit + n_experts + 1) // per) * per
    groups = n_slots // per
    n_units = nb * (n_assign // unit + n_experts) + n_experts * (per - 1)
    n_tiles = -(-n_units // per)
    n_units = n_tiles * per

    comb = moe_router(v, router)
    wts, ids = lax.top_k(comb[:, :n_experts], MOE_TOPK)
    ea = ids.reshape(nb, n_assign)
    wa = wts.reshape(nb, n_assign)
    ta = jnp.broadcast_to(jnp.repeat(jnp.arange(tb, dtype=jnp.int32), MOE_TOPK)[None], (nb, n_assign))
    se, st, sw = lax.sort((ea, ta, wa), dimension=1, num_keys=1, is_stable=True)
    counts = jnp.sum(jax.nn.one_hot(ea, n_experts, dtype=jnp.int32), axis=1)
    units = -(-counts // unit)
    excl = lambda x, axis: jnp.cumsum(x, axis=axis) - x
    slot_start = excl(units, 1)
    row_start = excl(counts, 1)
    is_e = se[..., None] == jnp.arange(n_experts, dtype=jnp.int32)
    lookup = lambda table: jnp.sum(jnp.where(is_e, table[:, None, :], 0), axis=-1)
    pos = lookup(slot_start) * unit + jnp.arange(n_assign, dtype=jnp.int32)[None] - lookup(row_start)
    bidx = jnp.arange(nb, dtype=jnp.int32)[:, None]
    hit = pos[:, None, :] == jnp.arange(n_slots * unit, dtype=jnp.int32)[None, :, None]
    row_token = jnp.sum(jnp.where(hit, st[:, None, :] + 1, 0), axis=-1) - 1
    row_weight = jnp.sum(jnp.where(hit, sw[:, None, :], 0.0), axis=-1)
    per_expert = jnp.sum(units, axis=0)
    per_expert_pad = -(-per_expert // per) * per
    e_off = excl(per_expert_pad, 0)
    before = excl(units, 0)
    slot = jnp.arange(n_slots, dtype=jnp.int32)
    slot_end = jnp.cumsum(units, axis=1)
    e_of_slot = jnp.sum(slot[None, :, None] >= slot_end[:, None, :], axis=-1)
    used_slot = e_of_slot < n_experts
    e_clip = jnp.minimum(e_of_slot, n_experts - 1)
    dst_unit = (e_off[e_clip] + jnp.take_along_axis(before, e_clip, axis=1)
                + slot[None] - jnp.take_along_axis(slot_start, e_clip, axis=1))
    dst_unit = jnp.where(used_slot, dst_unit, 0).astype(jnp.int32)
    flat_slot = (bidx * n_slots + slot[None]).astype(jnp.int32)
    zero_slot = n_slots - 1
    src_unit = jnp.full((n_units,), zero_slot, jnp.int32).at[
        jnp.where(used_slot, dst_unit, n_units).reshape(-1)].set(flat_slot.reshape(-1), mode="drop")
    tile_end = jnp.cumsum(per_expert_pad) // per
    tile_ids = jnp.arange(n_tiles, dtype=jnp.int32)
    tile_expert = jnp.minimum(jnp.sum(tile_ids[:, None] >= tile_end[None, :], axis=-1), n_experts - 1).astype(jnp.int32)
    tiles_used = tile_end[-1:].astype(jnp.int32)
    first = jnp.concatenate([jnp.ones((1,), jnp.int32),
                             (tile_expert[1:] != tile_expert[:-1]).astype(jnp.int32)])
    rw_em = row_weight.reshape(nb * n_slots, unit)[src_unit].reshape(n_units * unit, 1)

    xs = pl.pallas_call(
        _moe_gather_kernel,
        grid=(nb, groups),
        in_specs=[pl.BlockSpec((tb, d), lambda b_, g: (b_, 0)),
                  pl.BlockSpec((1, tile, 1), lambda b_, g: (b_, g, 0))],
        out_specs=pl.BlockSpec((tile, d), lambda b_, g: (b_ * groups + g, 0)),
        out_shape=jax.ShapeDtypeStruct((nb * n_slots * unit, d), BF16),
        compiler_params=_params("parallel", "arbitrary"),
        name="moe_gather",
    )(v, row_token.reshape(nb, n_slots * unit, 1))

    bpe = de // bn
    unit_spec = lambda i: pl.BlockSpec((unit, d), lambda j, t, src, ex, fi, us: (src[per * t + i], 0))
    hid = pl.pallas_call(
        functools.partial(_moe_expert_in_kernel, per=per),
        grid_spec=pltpu.PrefetchScalarGridSpec(
            num_scalar_prefetch=4,
            grid=(bpe, n_tiles),
            in_specs=[unit_spec(i) for i in range(per)] + [
                pl.BlockSpec((1, d, bn), lambda j, t, src, ex, fi, us: (ex[t], 0, j),
                             pipeline_mode=pl.Buffered(1)),
                pl.BlockSpec((1, d, bn), lambda j, t, src, ex, fi, us: (ex[t], 0, j + bpe),
                             pipeline_mode=pl.Buffered(1)),
                pl.BlockSpec((tile, 1), lambda j, t, src, ex, fi, us: (t, 0))],
            out_specs=pl.BlockSpec((tile, bn), lambda j, t, src, ex, fi, us: (t, j)),
            scratch_shapes=[pltpu.VMEM((d, bn), BF16), pltpu.VMEM((d, bn), BF16), pltpu.VMEM((tile, d), BF16)]),
        out_shape=jax.ShapeDtypeStruct((n_tiles * tile, de), BF16),
        compiler_params=_params("arbitrary", "arbitrary"),
        name="moe_expert_in",
    )(src_unit, tile_expert, first, tiles_used, *([xs] * per), w_in, w_in, rw_em)

    bo2 = _pick(d, 2 * bo)
    ys = pl.pallas_call(
        _moe_expert_out_kernel,
        grid_spec=pltpu.PrefetchScalarGridSpec(
            num_scalar_prefetch=3,
            grid=(d // bo2, n_tiles),
            in_specs=[pl.BlockSpec((tile, de), lambda n, t, ex, fi, us: (t, 0)),
                      pl.BlockSpec((1, de, bo2), lambda n, t, ex, fi, us: (ex[t], 0, n))],
            out_specs=pl.BlockSpec((tile, bo2), lambda n, t, ex, fi, us: (t, n)),
            scratch_shapes=[pltpu.VMEM((de, bo2), BF16)]),
        out_shape=jax.ShapeDtypeStruct((n_tiles * tile, d), BF16),
        compiler_params=_params("arbitrary", "arbitrary"),
        name="moe_expert_out",
    )(tile_expert, first, tiles_used, hid, w_out)

    y_spec = lambda i: pl.BlockSpec((unit, bo), lambda b_, n, g, dst: (dst[(b_ * groups + g) * per + i], n))
    return pl.pallas_call(
        functools.partial(_moe_scatter_kernel, per=per),
        grid_spec=pltpu.PrefetchScalarGridSpec(
            num_scalar_prefetch=1,
            grid=(nb, d // bo, groups),
            in_specs=[pl.BlockSpec((tb, bo), lambda b_, n, g, dst: (b_, n)),
                      pl.BlockSpec((1, 1, tile), lambda b_, n, g, dst: (b_ * groups + g, 0, 0))]
                     + [y_spec(i) for i in range(per)],
            out_specs=pl.BlockSpec((tb, bo), lambda b_, n, g, dst: (b_, n)),
            scratch_shapes=[pltpu.VMEM((tile, bo), BF16)]),
        out_shape=jax.ShapeDtypeStruct((m, d), F32),
        compiler_params=_params("parallel", "parallel", "arbitrary"),
        name="moe_scatter",
    )(dst_unit.reshape(-1), h, row_token.reshape(nb * groups, 1, tile), *([ys] * per))


def ple_gate(h, p_i, norm_pl, pl_proj, pl_gate, layer):
    d = h.shape[1]
    n = rmsnorm(h, norm_pl, name="rmsnorm_ple")
    return matmul_ws(n, [(pl_gate, (layer, 0))], d, epilogue=_ep_ple_gate,
                     extras=[(h, "mn"), (p_i, "m"), (pl_proj, "kn")], name="ple_gate")


def _rw_mix_kernel(u_ref, mu_ref, *o_refs):
    u = u_ref[0]
    row = lax.broadcasted_iota(jnp.int32, u.shape, 0)
    dx = jnp.where(row >= 1, pltpu.roll(u, 1, 0), 0.0) - u
    for j, o_ref in enumerate(o_refs):
        o_ref[0] = (u + dx * mu_ref[j:j + 1, :]).astype(o_ref.dtype)


def rw_token_mix(u, mu):
    bsz, s_len, d = u.shape
    cb = _pick(d, LANES)
    n_mix = mu.shape[0]
    spec = pl.BlockSpec((1, s_len, cb), lambda b_, j: (b_, 0, j))
    return pl.pallas_call(
        _rw_mix_kernel,
        grid=(bsz, d // cb),
        in_specs=[spec, pl.BlockSpec((n_mix, cb), lambda b_, j: (0, j))],
        out_specs=[spec] * n_mix,
        out_shape=[jax.ShapeDtypeStruct(u.shape, BF16)] * n_mix,
        compiler_params=_params("parallel", "parallel"),
        name="rwkv_token_mix",
    )(u, mu)


def _dot_hi(a, b):
    return jnp.dot(a, b, preferred_element_type=F32, precision=lax.Precision.HIGHEST)


def _rw_scan_kernel(r_ref, k_ref, v_ref, a_ref, lw_ref, g_ref, kk_ref, ka_ref, rk_ref, lnw_ref, lnb_ref,
                    o_ref, state_ref, *, chunk, heads, n):
    @pl.when(pl.program_id(2) == 0)
    def _():
        state_ref[...] = jnp.zeros_like(state_ref)

    hs = range(heads)
    sls = [slice(j * n, (j + 1) * n) for j in hs]
    ti = lax.broadcasted_iota(jnp.int32, (chunk, chunk), 0)
    si = lax.broadcasted_iota(jnp.int32, (chunk, chunk), 1)
    strict = ti > si
    incl = ti >= si
    dot = functools.partial(jnp.dot, preferred_element_type=F32)

    r = [r_ref[0, :, sl] for sl in sls]
    v = [v_ref[0, :, sl] for sl in sls]
    a = [a_ref[0, :, sl] for sl in sls]
    lw = [lw_ref[0, :, sl] for sl in sls]
    k = [k_ref[0, :, sl] for sl in sls]
    kk = [k[j] * kk_ref[:, sls[j]] for j in hs]
    kk = [kk[j] / jnp.maximum(jnp.sqrt(jnp.sum(kk[j] * kk[j], axis=-1, keepdims=True)), 1e-12) for j in hs]
    kmod = [k[j] * (1.0 + (a[j] - 1.0) * ka_ref[:, sls[j]]) for j in hs]
    kka = [kk[j] * a[j] for j in hs]
    cum = [_cumsum_rows(lw[j], chunk) for j in hs]
    cum_end = [c[chunk - 1:chunk, :] for c in cum]
    mid = [c[chunk // 2 - 1:chunk // 2, :] for c in cum]
    e_neg = [jnp.exp(mid[j] - cum[j]) for j in hs]
    am = [(kk[j] * jnp.exp(cum[j] - lw[j] - mid[j])).astype(BF16) for j in hs]
    bm = [(kka[j] * e_neg[j]).astype(BF16) for j in hs]
    km = [(kmod[j] * e_neg[j]).astype(BF16) for j in hs]
    rm = [(r[j] * jnp.exp(cum[j] - mid[j])).astype(BF16) for j in hs]
    a_abs = [(kk[j] * jnp.exp(cum[j] - lw[j])).astype(BF16) for j in hs]
    r_abs = [(r[j] * jnp.exp(cum[j])).astype(BF16) for j in hs]
    vb = [x.astype(BF16) for x in v]
    st = [state_ref[j] for j in hs]
    stb = [x.astype(BF16) for x in st]

    nb = [(-jnp.where(strict, _dot_nt(am[j], bm[j]), 0.0)).astype(BF16) for j in hs]
    lk = [jnp.where(strict, _dot_nt(am[j], km[j]), 0.0).astype(BF16) for j in hs]
    x = [_dot_nt(a_abs[j], stb[j]) + dot(lk[j], vb[j]) for j in hs]
    x = [x[j] + dot(nb[j], x[j].astype(BF16)) for j in hs]
    p = 2
    while p < chunk:
        nb = [dot(nb[j], nb[j]).astype(BF16) for j in hs]
        x = [x[j] + dot(nb[j], x[j].astype(BF16)) for j in hs]
        p *= 2
    pb = [xj.astype(BF16) for xj in x]
    mk = [jnp.where(incl, _dot_nt(rm[j], km[j]), 0.0).astype(BF16) for j in hs]
    mb = [jnp.where(incl, _dot_nt(rm[j], bm[j]), 0.0).astype(BF16) for j in hs]
    y = [_dot_nt(r_abs[j], stb[j]) + dot(mk[j], vb[j]) - dot(mb[j], pb[j]) for j in hs]
    to_end = [jnp.exp(cum_end[j] - cum[j]) for j in hs]
    for j in hs:
        state_ref[j] = (st[j] * jnp.exp(cum_end[j]) + _dot_tn(vb[j], (kmod[j] * to_end[j]).astype(BF16))
                        - _dot_tn(pb[j], (kka[j] * to_end[j]).astype(BF16)))
    for j in hs:
        sl = sls[j]
        bonus = jnp.sum(r[j] * kmod[j] * rk_ref[:, sl], axis=-1, keepdims=True) * v[j]
        mean = jnp.mean(y[j], axis=-1, keepdims=True)
        yc = y[j] - mean
        var = jnp.mean(yc * yc, axis=-1, keepdims=True)
        yn = yc * lax.rsqrt(var + RW_LN_EPS) * lnw_ref[:, sl] + lnb_ref[:, sl]
        o_ref[0, :, sl] = ((yn + bonus) * g_ref[0, :, sl]).astype(o_ref.dtype)


def _rw_scan_tile_kernel(r_ref, k_ref, v_ref, a_ref, lw_ref, g_ref, kk_ref, ka_ref, rk_ref, lnw_ref, lnb_ref,
                         o_ref, state_ref, *, chunk, heads, n):
    @pl.when(pl.program_id(2) == 0)
    def _():
        state_ref[...] = jnp.zeros_like(state_ref)

    per = LANES // n
    tiles = range(heads // per)
    sub = range(per)
    ti = lax.broadcasted_iota(jnp.int32, (chunk, chunk), 0)
    si = lax.broadcasted_iota(jnp.int32, (chunk, chunk), 1)
    strict = ti > si
    incl = ti >= si
    lane_seg = lax.broadcasted_iota(jnp.int32, (1, LANES), 1) // n
    seg_is = [lane_seg == j for j in sub]
    same_head = (lax.broadcasted_iota(jnp.int32, (LANES, LANES), 0) // n
                 == lax.broadcasted_iota(jnp.int32, (LANES, LANES), 1) // n)
    dot = functools.partial(jnp.dot, preferred_element_type=F32)
    tile = lambda x, i: x[:, i * LANES:(i + 1) * LANES]

    def pick(vals):
        out = vals[-1]
        for j in range(per - 2, -1, -1):
            out = jnp.where(seg_is[j], vals[j], out)
        return out

    def seg_sum(x):
        return pick([jnp.sum(jnp.where(seg_is[j], x, 0.0), axis=-1, keepdims=True) for j in sub])

    r, k, v, a, lw = r_ref[0], k_ref[0], v_ref[0], a_ref[0], lw_ref[0]
    kk = k * kk_ref[...]
    kmod = k * (1.0 + (a - 1.0) * ka_ref[...])
    cum = _cumsum_rows(lw, chunk)
    cum_end = cum[chunk - 1:chunk, :]
    mid = cum[chunk // 2 - 1:chunk // 2, :]
    bonus_in = r * kmod * rk_ref[...]
    kk_t, bonus_t = [], []
    for i in tiles:
        kki = tile(kk, i)
        kk_t.append(kki / jnp.maximum(jnp.sqrt(seg_sum(kki * kki)), 1e-12))
        bonus_t.append(seg_sum(tile(bonus_in, i)) * tile(v, i))
    kk = jnp.concatenate(kk_t, axis=-1) if len(kk_t) > 1 else kk_t[0]
    kka = kk * a
    e_neg = jnp.exp(mid - cum)
    to_end = jnp.exp(cum_end - cum)
    am = (kk * jnp.exp(cum - lw - mid)).astype(BF16)
    bm = (kka * e_neg).astype(BF16)
    km = (kmod * e_neg).astype(BF16)
    rm = (r * jnp.exp(cum - mid)).astype(BF16)
    a_abs = (kk * jnp.exp(cum - lw)).astype(BF16)
    r_abs = (r * jnp.exp(cum)).astype(BF16)
    k_end = (kmod * to_end).astype(BF16)
    b_end = (kka * to_end).astype(BF16)
    vb = v.astype(BF16)
    st_decay = jnp.exp(cum_end)
    zero = jnp.zeros((), BF16)

    st = [state_ref[i] for i in tiles]
    stb = [s.astype(BF16) for s in st]
    am_h = [[jnp.where(seg_is[j], tile(am, i), zero) for j in sub] for i in tiles]
    rm_h = [[jnp.where(seg_is[j], tile(rm, i), zero) for j in sub] for i in tiles]
    nb = [[(-jnp.where(strict, _dot_nt(am_h[i][j], tile(bm, i)), 0.0)).astype(BF16) for j in sub] for i in tiles]
    lk = [[jnp.where(strict, _dot_nt(am_h[i][j], tile(km, i)), 0.0).astype(BF16) for j in sub] for i in tiles]
    x = [_dot_nt(tile(a_abs, i), stb[i]) + pick([dot(lk[i][j], tile(vb, i)) for j in sub]) for i in tiles]
    xb = [xi.astype(BF16) for xi in x]
    x = [x[i] + pick([dot(nb[i][j], xb[i]) for j in sub]) for i in tiles]
    p = 2
    while p < chunk:
        nb = [[dot(nb[i][j], nb[i][j]).astype(BF16) for j in sub] for i in tiles]
        xb = [xi.astype(BF16) for xi in x]
        x = [x[i] + pick([dot(nb[i][j], xb[i]) for j in sub]) for i in tiles]
        p *= 2
    pb = [xi.astype(BF16) for xi in x]
    mk = [[jnp.where(incl, _dot_nt(rm_h[i][j], tile(km, i)), 0.0).astype(BF16) for j in sub] for i in tiles]
    mb = [[jnp.where(incl, _dot_nt(rm_h[i][j], tile(bm, i)), 0.0).astype(BF16) for j in sub] for i in tiles]
    y = [_dot_nt(tile(r_abs, i), stb[i])
         + pick([dot(mk[i][j], tile(vb, i)) - dot(mb[i][j], pb[i]) for j in sub]) for i in tiles]
    for i in tiles:
        upd = _dot_tn(tile(vb, i), tile(k_end, i)) - _dot_tn(pb[i], tile(b_end, i))
        state_ref[i] = st[i] * tile(st_decay, i) + jnp.where(same_head, upd, 0.0)
    inv_n = 1.0 / n
    for i in tiles:
        cols = slice(i * LANES, (i + 1) * LANES)
        mean = seg_sum(y[i]) * inv_n
        yc = y[i] - mean
        var = seg_sum(yc * yc) * inv_n
        yn = yc * lax.rsqrt(var + RW_LN_EPS) * lnw_ref[:, cols] + lnb_ref[:, cols]
        o_ref[0, :, cols] = ((yn + bonus_t[i]) * g_ref[0, :, cols]).astype(o_ref.dtype)


def rw_scan(r, k, v, a, lw, g, k_k, k_a, r_k, ln_w, ln_b, *, n=RW_HEAD_DIM, chunk=RW_CHUNK, heads=8):
    bsz, s_len, d = r.shape
    chunk = min(chunk, s_len)
    heads = min(heads, d // n)
    hw = heads * n
    seq = pl.BlockSpec((1, chunk, hw), lambda b_, h_, c: (b_, c, h_))
    par = pl.BlockSpec((1, hw), lambda b_, h_, c: (0, h_))
    row = lambda t: t.reshape(1, d)
    assert hw % LANES == 0 and LANES % n == 0
    kern = functools.partial(_rw_scan_tile_kernel, chunk=chunk, heads=heads, n=n)
    return pl.pallas_call(
        kern,
        grid=(bsz, d // hw, s_len // chunk),
        in_specs=[seq] * 6 + [par] * 5,
        out_specs=seq,
        out_shape=jax.ShapeDtypeStruct(r.shape, BF16),
        scratch_shapes=[pltpu.VMEM((hw // LANES, LANES, LANES), F32)],
        compiler_params=_params("parallel", "parallel", "arbitrary"),
        name="rwkv7_scan",
    )(r, k, v, a, lw, g, row(k_k), row(k_a), row(r_k), row(ln_w), row(ln_b))


def rwkv7_mixer(u, h, w, bsz, s_len):
    t, d = u.shape
    xr, xw, xk, xv, xa, xg = [x.reshape(t, d) for x in rw_token_mix(u.reshape(bsz, s_len, d), w["rw_mu"])]
    r = matmul_ws(xr, [(w["rw_w_rkv"], (0, 0))], d, name="rw_r")
    k = matmul_ws(xk, [(w["rw_w_rkv"], (1, 0))], d, name="rw_k")
    v = matmul_ws(xv, [(w["rw_w_rkv"], (2, 0))], d, name="rw_v")
    row = lambda x: x.reshape(1, d)
    w_lo = matmul(xw, [(w["rw_w1"], 0)], w["rw_w1"].shape[1], epilogue=_ep_tanh, out_dtype=BF16, name="rw_w1")
    wide = 2048
    lw = matmul(w_lo, [(w["rw_w2"], 0)], d, epilogue=_ep_rw_logdecay, extras=[(row(w["rw_w0"]), "n")], bn=wide,
                name="rw_w2")
    a_lo = matmul(xa, [(w["rw_a1"], 0)], w["rw_a1"].shape[1], out_dtype=BF16, name="rw_a1")
    a = matmul(a_lo, [(w["rw_a2"], 0)], d, epilogue=_ep_bias_sigmoid, extras=[(row(w["rw_a0"]), "n")], bn=wide,
               name="rw_a2")
    g_lo = matmul(xg, [(w["rw_g1"], 0)], w["rw_g1"].shape[1], epilogue=_ep_sigmoid, out_dtype=BF16, name="rw_g1")
    g = matmul(g_lo, [(w["rw_g2"], 0)], d, bn=wide, name="rw_g2")
    shp = (bsz, s_len, d)
    y = rw_scan(r.reshape(shp), k.reshape(shp), v.reshape(shp), a.reshape(shp), lw.reshape(shp), g.reshape(shp),
                w["rw_k_k"], w["rw_k_a"], w["rw_r_k"], w["rw_ln_w"], w["rw_ln_b"])
    return matmul_ws(y.reshape(t, d), [(w["rw_w_out"], 0)], d, epilogue=_ep_residual, extras=[(h, "mn")],
                     name="rw_out")


NEG_BIG = -1e30


def _rope_kernel(x_ref, cc_ref, ss_ref, o_ref, *, n_q_slots, scale):
    x = x_ref[0]
    out = x * cc_ref[...] + pltpu.roll(x, x.shape[-1] // 2, 1) * ss_ref[...]
    out = out * jnp.where(pl.program_id(2) < n_q_slots, scale, 1.0)
    o_ref[0] = out.astype(o_ref.dtype)


def _rope_tables(pos, dim):
    inv = ROPE_THETA ** (-(jnp.arange(0, dim, 2, dtype=F32) / dim))
    ang = pos.astype(F32)[:, None] * inv[None, :]
    cos, sin = jnp.cos(ang), jnp.sin(ang)
    return jnp.concatenate([cos, cos], axis=-1), jnp.concatenate([-sin, sin], axis=-1)


def nsa_rope(proj, n_q_slots, k_slots, dh, scale, tb=512):
    bsz, s_len, _ = proj.shape
    tb = min(tb, s_len)
    cc, ss = _rope_tables(jnp.arange(s_len), dh)
    n_out = n_q_slots + len(k_slots)

    def in_slot(j):
        slot = j
        for idx, ks in enumerate(k_slots):
            slot = jnp.where(j == n_q_slots + idx, ks, slot)
        return slot

    return pl.pallas_call(
        functools.partial(_rope_kernel, n_q_slots=n_q_slots, scale=scale),
        grid=(bsz, s_len // tb, n_out),
        in_specs=[pl.BlockSpec((1, tb, dh), lambda b_, t, j: (b_, t, in_slot(j))),
                  pl.BlockSpec((tb, dh), lambda b_, t, j: (t, 0)),
                  pl.BlockSpec((tb, dh), lambda b_, t, j: (t, 0))],
        out_specs=pl.BlockSpec((1, tb, dh), lambda b_, t, j: (b_, t, j)),
        out_shape=jax.ShapeDtypeStruct((bsz, s_len, n_out * dh), BF16),
        compiler_params=_params("parallel", "parallel", "arbitrary"),
        name="nsa_rope",
    )(proj, cc, ss)


def _cmp_finish_kernel(z_ref, bias_ref, w2_ref, cc_ref, ss_ref, o_ref, *, hidden, rope):
    z = z_ref[0]
    nc = z.shape[0]
    nxt = pltpu.roll(z[:, hidden:], nc - 1, 0)
    hid = _silu(z[:, :hidden] + nxt + bias_ref[...])
    out = jnp.dot(hid.astype(BF16), w2_ref[...], preferred_element_type=F32)
    if rope:
        out = out * cc_ref[...] + pltpu.roll(out, out.shape[-1] // 2, 1) * ss_ref[...]
    o_ref[0] = out.astype(o_ref.dtype)


def nsa_compress(x, pos_emb, w1, w2, bsz, s_len, groups, dh, rope, transpose_out=False):
    stride, blk = NSA_CMP_STRIDE, NSA_CMP_BLOCK
    nc = s_len // stride
    hidden = w1.shape[-1]
    half = stride * dh
    x16 = jnp.transpose(x.reshape(bsz, nc, stride, groups, dh), (0, 3, 1, 2, 4)).reshape(bsz * groups * nc, half)
    w1f = w1.reshape(blk * dh, hidden)
    wcat = jnp.concatenate([w1f[:half], w1f[half:]], axis=1).astype(BF16)
    z = matmul(x16.astype(BF16), [(wcat, 0)], 2 * hidden, name="nsa_cmp_w1")
    bias = matmul(pos_emb.reshape(1, blk * dh).astype(BF16), [(w1f.astype(BF16), 0)], hidden, name="nsa_cmp_pos")
    cc, ss = _rope_tables(jnp.arange(nc) * stride + blk - 1, dh)
    if transpose_out:
        assert not rope
        return pl.pallas_call(
            functools.partial(_cmp_finish_t_kernel, hidden=hidden),
            grid=(bsz * groups,),
            in_specs=[pl.BlockSpec((1, nc, 2 * hidden), lambda i: (i, 0, 0)),
                      pl.BlockSpec((1, hidden), lambda i: (0, 0)),
                      pl.BlockSpec((dh, hidden), lambda i: (0, 0))],
            out_specs=pl.BlockSpec((1, dh, nc), lambda i: (i, 0, 0)),
            out_shape=jax.ShapeDtypeStruct((bsz * groups, dh, nc), BF16),
            compiler_params=_params("parallel"),
            name="nsa_cmp_finish_t",
        )(z.reshape(bsz * groups, nc, 2 * hidden), bias, w2.T.astype(BF16))
    return pl.pallas_call(
        functools.partial(_cmp_finish_kernel, hidden=hidden, rope=rope),
        grid=(bsz * groups,),
        in_specs=[pl.BlockSpec((1, nc, 2 * hidden), lambda i: (i, 0, 0)),
                  pl.BlockSpec((1, hidden), lambda i: (0, 0)),
                  pl.BlockSpec((hidden, dh), lambda i: (0, 0)),
                  pl.BlockSpec((nc, dh), lambda i: (0, 0)),
                  pl.BlockSpec((nc, dh), lambda i: (0, 0))],
        out_specs=pl.BlockSpec((1, nc, dh), lambda i: (i, 0, 0)),
        out_shape=jax.ShapeDtypeStruct((bsz * groups, nc, dh), BF16),
        compiler_params=_params("parallel"),
        name="nsa_cmp_finish",
    )(z.reshape(bsz * groups, nc, 2 * hidden), bias, w2.astype(BF16), cc, ss)


def _nsa_cmp_select_kernel(q_ref, kc_ref, vc_ref, ov_ref, oc_ref, sel_ref, *, tq, rep, dh, topn):
    qi = pl.program_id(2)
    kc = kc_ref[0]
    vc = vc_ref[0]
    nc = kc.shape[0]
    n_sel = sel_ref.shape[-1]
    t = qi * tq + lax.broadcasted_iota(jnp.int32, (tq, nc), 0)
    cmp_end = lax.broadcasted_iota(jnp.int32, (tq, nc), 1) * NSA_CMP_STRIDE + (NSA_CMP_BLOCK - 1)
    visible = cmp_end <= t
    psum = jnp.zeros((tq, nc), F32)
    for r in range(rep):
        s = jnp.where(visible, _dot_nt(q_ref[0, :, r * dh:(r + 1) * dh], kc), NEG_BIG)
        m = jnp.max(s, axis=-1, keepdims=True)
        e = jnp.where(visible, jnp.exp(s - m), 0.0)
        den = jnp.sum(e, axis=-1, keepdims=True)
        p = e / jnp.where(den > 0, den, 1.0)
        oc_ref[0, :, r * dh:(r + 1) * dh] = jnp.dot(p.astype(BF16), vc, preferred_element_type=F32)
        psum = psum + p
    imp = _dot_hi(psum, ov_ref[...])
    blk = lax.broadcasted_iota(jnp.int32, (tq, n_sel), 1)
    cur = (qi * tq + lax.broadcasted_iota(jnp.int32, (tq, n_sel), 0)) // NSA_SEL_BLOCK
    forced = (blk == 0) | (blk == cur) | (blk == cur - 1)
    imp = jnp.where(forced, NSA_FORCED_SCORE, imp)
    imp = jnp.where(blk > cur, -jnp.inf, imp)
    sel = jnp.zeros((tq, n_sel), F32)
    for _ in range(topn):
        m = jnp.max(imp, axis=-1, keepdims=True)
        first = jnp.min(jnp.where(imp == m, blk, n_sel), axis=-1, keepdims=True)
        hit = blk == first
        sel = jnp.where(hit, 1.0, sel)
        imp = jnp.where(hit, -jnp.inf, imp)
    sel_ref[0, 0] = sel


def _flash_step(q_scr, k, v, mask, m_ref, l_ref, acc_ref, rep, tq):
    kb = k.shape[0]
    s = _dot_nt(q_scr[...], k).reshape(rep, tq, kb)
    s = jnp.where(mask[None], s, NEG_BIG)
    m_old = m_ref[...].reshape(rep, tq, -1)[:, :, :1]
    m_new = jnp.maximum(m_old, jnp.max(s, axis=-1, keepdims=True))
    p = jnp.where(mask[None], jnp.exp(s - m_new), 0.0)
    alpha = jnp.exp(m_old - m_new)
    l_old = l_ref[...].reshape(rep, tq, -1)[:, :, :1]
    l_new = alpha * l_old + jnp.sum(p, axis=-1, keepdims=True)
    pv = jnp.dot(p.reshape(rep * tq, kb).astype(BF16), v, preferred_element_type=F32)
    acc_ref[...] = (alpha * acc_ref[...].reshape(rep, tq, -1)).reshape(rep * tq, -1) + pv
    m_ref[...] = jnp.broadcast_to(m_new, (rep, tq, m_ref.shape[-1])).reshape(m_ref.shape)
    l_ref[...] = jnp.broadcast_to(l_new, (rep, tq, l_ref.shape[-1])).reshape(l_ref.shape)


def _flash_init(q_ref, q_scr, m_ref, l_ref, acc_ref, rep, tq, dh):
    for r in range(rep):
        q_scr[r * tq:(r + 1) * tq, :] = q_ref[0, :, r * dh:(r + 1) * dh]
    m_ref[...] = jnp.full_like(m_ref, NEG_BIG)
    l_ref[...] = jnp.zeros_like(l_ref)
    acc_ref[...] = jnp.zeros_like(acc_ref)


def _flash_result(l_ref, acc_ref):
    l = l_ref[...][:, :1]
    return acc_ref[...] / jnp.where(l > 0, l, 1.0)


def _nsa_select_kernel(q_ref, k_ref, v_ref, sel_ref, o_ref, q_scr, m_ref, l_ref, acc_ref, *, tq, kb, rep, dh):
    qi = pl.program_id(2)
    kj = pl.program_id(3)

    @pl.when(kj == 0)
    def _():
        _flash_init(q_ref, q_scr, m_ref, l_ref, acc_ref, rep, tq, dh)

    @pl.when(kj * kb <= qi * tq + tq - 1)
    def _():
        sel = sel_ref[0, 0]
        blk = lax.broadcasted_iota(jnp.int32, sel.shape, 1)
        kpos = kj * kb + lax.broadcasted_iota(jnp.int32, (tq, kb), 1)
        t = qi * tq + lax.broadcasted_iota(jnp.int32, (tq, kb), 0)
        chosen = jnp.zeros((tq, kb), F32)
        for i in range(kb // NSA_SEL_BLOCK):
            col = jnp.sum(jnp.where(blk == kj * (kb // NSA_SEL_BLOCK) + i, sel, 0.0), axis=-1, keepdims=True)
            in_blk = (kpos - kj * kb) // NSA_SEL_BLOCK == i
            chosen = jnp.where(in_blk, col, chosen)
        mask = (chosen > 0) & (kpos <= t)
        _flash_step(q_scr, k_ref[0], v_ref[0].astype(BF16), mask, m_ref, l_ref, acc_ref, rep, tq)

    @pl.when(kj == pl.num_programs(3) - 1)
    def _():
        out = _flash_result(l_ref, acc_ref)
        for r in range(rep):
            o_ref[0, :, r * dh:(r + 1) * dh] = out[r * tq:(r + 1) * tq, :]


def _nsa_window_kernel(q_ref, k_ref, v_ref, oc_ref, os_ref, g_ref, o_ref, q_scr, m_ref, l_ref, acc_ref,
                       *, tq, kb, rep, dh, window, n_steps):
    qi = pl.program_id(2)
    w = pl.program_id(3)
    kblk = qi * (tq // kb) - (n_steps - tq // kb) + w

    @pl.when(w == 0)
    def _():
        _flash_init(q_ref, q_scr, m_ref, l_ref, acc_ref, rep, tq, dh)

    @pl.when(kblk >= 0)
    def _():
        kpos = kblk * kb + lax.broadcasted_iota(jnp.int32, (tq, kb), 1)
        t = qi * tq + lax.broadcasted_iota(jnp.int32, (tq, kb), 0)
        mask = (kpos <= t) & (kpos > t - window)
        _flash_step(q_scr, k_ref[0], v_ref[0].astype(BF16), mask, m_ref, l_ref, acc_ref, rep, tq)

    @pl.when(w == n_steps - 1)
    def _():
        out = _flash_result(l_ref, acc_ref)
        gates = g_ref[0, 0]
        for r in range(rep):
            sl = slice(r * dh, (r + 1) * dh)
            o = (gates[:, 3 * r:3 * r + 1] * oc_ref[0, :, sl] + gates[:, 3 * r + 1:3 * r + 2] * os_ref[0, :, sl]
                 + gates[:, 3 * r + 2:3 * r + 3] * out[r * tq:(r + 1) * tq, :])
            o_ref[0, :, sl] = o.astype(o_ref.dtype)


def nsa_mixer(u, h, w, bsz, s_len):
    t, d = u.shape
    dh, groups = NSA_HEAD_DIM, NSA_N_KV
    n_heads = d // dh
    rep = n_heads // groups
    kvw = groups * dh
    qw = n_heads * dh
    main_w = qw + 6 * kvw
    scale = dh ** -0.5
    tq = kb = min(128, s_len)
    nq = s_len // tq
    n_sel = s_len // NSA_SEL_BLOCK
    topn = min(NSA_TOPK, n_sel)
    w_in = w["nsa_w_in"]
    proj = matmul_ws(u, [(w_in, 0)], main_w, name="nsa_in").reshape(bsz, s_len, main_w)
    gates = matmul(u, [(w_in[:, main_w:].astype(BF16), 0)], w_in.shape[1] - main_w, epilogue=_ep_sigmoid,
                   name="nsa_gates")
    gates = jnp.transpose(gates.reshape(bsz, s_len, groups, rep * 3), (0, 2, 1, 3))
    slot = lambda j: (qw + j * kvw) // dh
    roped = nsa_rope(proj, n_heads, [slot(2) + g for g in range(groups)] + [slot(4) + g for g in range(groups)],
                     dh, scale)
    kc = nsa_compress(proj[..., qw:qw + kvw], w["nsa_cmp_pos_k"], w["nsa_cmp_k_w1"], w["nsa_cmp_k_w2"],
                      bsz, s_len, groups, dh, True)
    vc = nsa_compress(proj[..., qw + kvw:qw + 2 * kvw], w["nsa_cmp_pos_v"], w["nsa_cmp_v_w1"], w["nsa_cmp_v_w2"],
                      bsz, s_len, groups, dh, False)
    nc = kc.shape[1]
    cs = jnp.arange(nc)[:, None] * NSA_CMP_STRIDE
    ss = jnp.arange(n_sel)[None, :] * NSA_SEL_BLOCK
    overlap = jnp.clip(jnp.minimum(cs + NSA_CMP_BLOCK, ss + NSA_SEL_BLOCK) - jnp.maximum(cs, ss), 0, None)
    overlap = overlap.astype(F32) / NSA_CMP_BLOCK

    q_spec3 = pl.BlockSpec((1, tq, rep * dh), lambda b_, g, i: (b_, i, g))
    o_c, sel = pl.pallas_call(
        functools.partial(_nsa_cmp_select_kernel, tq=tq, rep=rep, dh=dh, topn=topn),
        grid=(bsz, groups, nq),
        in_specs=[q_spec3,
                  pl.BlockSpec((1, nc, dh), lambda b_, g, i: (b_ * groups + g, 0, 0)),
                  pl.BlockSpec((1, nc, dh), lambda b_, g, i: (b_ * groups + g, 0, 0)),
                  pl.BlockSpec((nc, n_sel), lambda b_, g, i: (0, 0))],
        out_specs=[q_spec3, pl.BlockSpec((1, 1, tq, n_sel), lambda b_, g, i: (b_, g, i, 0))],
        out_shape=[jax.ShapeDtypeStruct((bsz, s_len, qw), F32),
                   jax.ShapeDtypeStruct((bsz, groups, s_len, n_sel), F32)],
        compiler_params=_params("parallel", "parallel", "parallel"),
        name="nsa_cmp_select",
    )(roped, kc, vc, overlap)

    q_spec = pl.BlockSpec((1, tq, rep * dh), lambda b_, g, i, j: (b_, i, g))
    flash_scratch = [pltpu.VMEM((rep * tq, dh), BF16), pltpu.VMEM((rep * tq, LANES), F32),
                     pltpu.VMEM((rep * tq, LANES), F32), pltpu.VMEM((rep * tq, dh), F32)]
    last_kb = lambda i: (i * tq + tq - 1) // kb
    o_s = pl.pallas_call(
        functools.partial(_nsa_select_kernel, tq=tq, kb=kb, rep=rep, dh=dh),
        grid=(bsz, groups, nq, s_len // kb),
        in_specs=[q_spec,
                  pl.BlockSpec((1, kb, dh), lambda b_, g, i, j: (b_, jnp.minimum(j, last_kb(i)), n_heads + g)),
                  pl.BlockSpec((1, kb, dh), lambda b_, g, i, j: (b_, jnp.minimum(j, last_kb(i)), slot(3) + g)),
                  pl.BlockSpec((1, 1, tq, n_sel), lambda b_, g, i, j: (b_, g, i, 0))],
        out_specs=q_spec,
        out_shape=jax.ShapeDtypeStruct((bsz, s_len, qw), F32),
        scratch_shapes=flash_scratch,
        compiler_params=_params("parallel", "parallel", "parallel", "arbitrary"),
        name="nsa_select_attn",
    )(roped, roped, proj, sel)

    n_steps = NSA_WINDOW // kb + tq // kb
    win_blk = lambda i, j: jnp.maximum(i * (tq // kb) - (n_steps - tq // kb) + j, 0)
    o = pl.pallas_call(
        functools.partial(_nsa_window_kernel, tq=tq, kb=kb, rep=rep, dh=dh, window=NSA_WINDOW, n_steps=n_steps),
        grid=(bsz, groups, nq, n_steps),
        in_specs=[q_spec,
                  pl.BlockSpec((1, kb, dh), lambda b_, g, i, j: (b_, win_blk(i, j), n_heads + groups + g)),
                  pl.BlockSpec((1, kb, dh), lambda b_, g, i, j: (b_, win_blk(i, j), slot(5) + g)),
                  q_spec, q_spec,
                  pl.BlockSpec((1, 1, tq, rep * 3), lambda b_, g, i, j: (b_, g, i, 0))],
        out_specs=q_spec,
        out_shape=jax.ShapeDtypeStruct((bsz, s_len, qw), BF16),
        scratch_shapes=flash_scratch,
        compiler_params=_params("parallel", "parallel", "parallel", "arbitrary"),
        name="nsa_window_attn",
    )(roped, roped, proj, o_c, o_s, gates)
    return matmul_ws(o.reshape(t, qw), [(w["nsa_w_out"], 0)], d, epilogue=_ep_residual, extras=[(h, "mn")],
                     name="nsa_out")


def _rope_t_kernel(x_ref, cc_ref, ss_ref, o_ref, *, n_rope, scale, group, dh):
    first_slot = pl.program_id(2) * group
    for i in range(group):
        x = x_ref[0, :, i * dh:(i + 1) * dh]
        roped = (x * cc_ref[...] + pltpu.roll(x, dh // 2, 1) * ss_ref[...]) * scale
        out = jnp.where(first_slot + i < n_rope, roped, x)
        o_ref[0, i * dh:(i + 1) * dh, :] = out.T.astype(o_ref.dtype)


def nsa_rope_t(proj, slots, n_rope, dh, scale, tb=512, group=4):
    bsz, s_len, _ = proj.shape
    tb = min(tb, s_len)
    cc, ss = _rope_tables(jnp.arange(s_len), dh)
    assert len(slots) % group == 0
    firsts = slots[::group]
    assert all(f % group == 0 and slots[i * group:(i + 1) * group] == list(range(f, f + group))
               for i, f in enumerate(firsts))
    table = jnp.asarray([f // group for f in firsts], jnp.int32)
    grid_spec = pltpu.PrefetchScalarGridSpec(
        num_scalar_prefetch=1,
        grid=(bsz, s_len // tb, len(firsts)),
        in_specs=[pl.BlockSpec((1, tb, group * dh), lambda b_, t, j, tab: (b_, t, tab[j])),
                  pl.BlockSpec((tb, dh), lambda b_, t, j, tab: (t, 0)),
                  pl.BlockSpec((tb, dh), lambda b_, t, j, tab: (t, 0))],
        out_specs=pl.BlockSpec((1, group * dh, tb), lambda b_, t, j, tab: (b_, j, t)),
    )
    kern = lambda tab, x_ref, cc_ref, ss_ref, o_ref: _rope_t_kernel(x_ref, cc_ref, ss_ref, o_ref, n_rope=n_rope,
                                                                   scale=scale, group=group, dh=dh)
    return pl.pallas_call(
        kern,
        grid_spec=grid_spec,
        out_shape=jax.ShapeDtypeStruct((bsz, len(slots) * dh, s_len), BF16),
        compiler_params=_params("parallel", "parallel", "arbitrary"),
        name="nsa_rope_t",
    )(table, proj, cc, ss)


def _cmp_finish_t_kernel(z_ref, bias_ref, w2_ref, o_ref, *, hidden):
    z = z_ref[0]
    nc = z.shape[0]
    nxt = pltpu.roll(z[:, hidden:], nc - 1, 0)
    hid = _silu(z[:, :hidden] + nxt + bias_ref[...])
    o_ref[0] = _dot_nt(w2_ref[...], hid.astype(BF16)).astype(o_ref.dtype)


def _nsa_cmp_select_t_kernel(q_ref, kc_ref, vc_ref, ov_ref, oc_ref, sel_ref, *, tq, rep, dh, topn):
    qi = pl.program_id(2)
    kc = kc_ref[0]
    vct = vc_ref[0]
    nc = kc.shape[0]
    n_sel = sel_ref.shape[2]
    t = qi * tq + lax.broadcasted_iota(jnp.int32, (nc, tq), 1)
    cmp_end = lax.broadcasted_iota(jnp.int32, (nc, tq), 0) * NSA_CMP_STRIDE + (NSA_CMP_BLOCK - 1)
    visible = cmp_end <= t
    s = [jnp.where(visible, jnp.dot(kc, q_ref[0, r * dh:(r + 1) * dh, :], preferred_element_type=F32), NEG_BIG)
         for r in range(rep)]
    e = [jnp.where(visible, jnp.exp2(x - jnp.max(x, axis=0, keepdims=True)), 0.0) for x in s]
    den = [jnp.sum(x, axis=0, keepdims=True) for x in e]
    p = [e[r] / jnp.where(den[r] > 0, den[r], 1.0) for r in range(rep)]
    for r in range(rep):
        oc_ref[0, r * dh:(r + 1) * dh, :] = jnp.dot(vct, p[r].astype(BF16), preferred_element_type=F32)
    psum = p[0]
    for r in range(1, rep):
        psum = psum + p[r]
    imp = _dot_hi(ov_ref[...], psum)
    blk = lax.broadcasted_iota(jnp.int32, (n_sel, tq), 0)
    cur = (qi * tq + lax.broadcasted_iota(jnp.int32, (n_sel, tq), 1)) // NSA_SEL_BLOCK
    forced = (blk == 0) | (blk == cur) | (blk == cur - 1)
    imp = jnp.where(forced, NSA_FORCED_SCORE, imp)
    imp = jnp.where(blk > cur, -jnp.inf, imp)
    sel = jnp.zeros((n_sel, tq), F32)
    for _ in range(topn):
        m = jnp.max(imp, axis=0, keepdims=True)
        first = jnp.min(jnp.where(imp == m, blk, n_sel), axis=0, keepdims=True)
        hit = blk == first
        sel = jnp.where(hit, 1.0, sel)
        imp = jnp.where(hit, -jnp.inf, imp)
    sel_ref[0, 0] = sel


def _flash_t_init(m_ref, l_ref, acc_ref):
    m_ref[...] = jnp.full_like(m_ref, NEG_BIG)
    l_ref[...] = jnp.zeros_like(l_ref)
    acc_ref[...] = jnp.zeros_like(acc_ref)


def _flash_t_step(q_ref, k, vt, mask, m_ref, l_ref, acc_ref, rep, dh):
    hs = range(rep)
    s = [jnp.where(mask, jnp.dot(k, q_ref[0, r * dh:(r + 1) * dh, :], preferred_element_type=F32), NEG_BIG)
         for r in hs]
    m_old = [m_ref[r] for r in hs]
    m_new = [jnp.maximum(m_old[r], jnp.max(s[r], axis=0, keepdims=True)) for r in hs]
    p = [jnp.exp2(s[r] - m_new[r]).astype(BF16) for r in hs]
    alpha = [jnp.exp2(m_old[r] - m_new[r]) for r in hs]
    pv = [jnp.dot(vt, p[r], preferred_element_type=F32) for r in hs]
    ones = jnp.ones((8, k.shape[0]), BF16)
    psum = [jnp.dot(ones, p[r], preferred_element_type=F32)[0:1] for r in hs]
    for r in hs:
        m_ref[r] = m_new[r]
        l_ref[r] = alpha[r] * l_ref[r] + psum[r]
        acc_ref[r] = acc_ref[r] * alpha[r] + pv[r]


def _nsa_select_t_kernel(qi_ref, kj_ref, q_ref, k_ref, vt_ref, sel_ref, o_ref, m_ref, l_ref, acc_ref,
                         *, tq, kb, rep, dh):
    pair = pl.program_id(2)
    qi = qi_ref[pair]
    kj = kj_ref[pair]

    @pl.when(kj == 0)
    def _():
        _flash_t_init(m_ref, l_ref, acc_ref)

    kpos = kj * kb + lax.broadcasted_iota(jnp.int32, (kb, tq), 0)
    t = qi * tq + lax.broadcasted_iota(jnp.int32, (kb, tq), 1)
    per = kb // NSA_SEL_BLOCK
    chosen = jnp.zeros((kb, tq), F32)
    for i in range(per):
        row = sel_ref[0, 0, pl.ds(kj * per + i, 1), :]
        chosen = jnp.where((kpos - kj * kb) // NSA_SEL_BLOCK == i, row, chosen)
    mask = (chosen > 0) & (kpos <= t)
    _flash_t_step(q_ref, k_ref[0], vt_ref[0], mask, m_ref, l_ref, acc_ref, rep, dh)

    @pl.when(kj * kb + kb > qi * tq + tq - 1)
    def _():
        for r in range(rep):
            l = l_ref[r]
            o_ref[0, r * dh:(r + 1) * dh, :] = acc_ref[r] / jnp.where(l > 0, l, 1.0)


def _nsa_window_t_kernel(q_ref, k_ref, vt_ref, oc_ref, os_ref, g_ref, o_ref, m_ref, l_ref, acc_ref,
                         *, tq, kb, rep, dh, window, n_steps):
    qi = pl.program_id(2)
    w = pl.program_id(3)
    kblk = qi * (tq // kb) - (n_steps - tq // kb) + w

    @pl.when(w == 0)
    def _():
        _flash_t_init(m_ref, l_ref, acc_ref)

    @pl.when(kblk >= 0)
    def _():
        kpos = kblk * kb + lax.broadcasted_iota(jnp.int32, (kb, tq), 0)
        t = qi * tq + lax.broadcasted_iota(jnp.int32, (kb, tq), 1)
        mask = (kpos <= t) & (kpos > t - window)
        _flash_t_step(q_ref, k_ref[0], vt_ref[0], mask, m_ref, l_ref, acc_ref, rep, dh)

    @pl.when(w == n_steps - 1)
    def _():
        gates = g_ref[0, 0]
        for r in range(rep):
            rows = slice(r * dh, (r + 1) * dh)
            l = l_ref[r]
            o_w = acc_ref[r] / jnp.where(l > 0, l, 1.0)
            o = (gates[3 * r:3 * r + 1, :] * oc_ref[0, rows, :] + gates[3 * r + 1:3 * r + 2, :] * os_ref[0, rows, :]
                 + gates[3 * r + 2:3 * r + 3, :] * o_w)
            o_ref[0, :, rows] = o.T.astype(o_ref.dtype)


def nsa_mixer_t(u, h, w, bsz, s_len):
    t, d = u.shape
    dh, groups = NSA_HEAD_DIM, NSA_N_KV
    n_heads = d // dh
    rep = n_heads // groups
    kvw = groups * dh
    qw = n_heads * dh
    main_w = qw + 6 * kvw
    scale = dh ** -0.5
    tq = kb = min(128, s_len)
    nq = s_len // tq
    n_sel = s_len // NSA_SEL_BLOCK
    topn = min(NSA_TOPK, n_sel)
    w_in = w["nsa_w_in"]
    proj = matmul_ws(u, [(w_in, 0)], main_w, name="nsa_in").reshape(bsz, s_len, main_w)
    gates = matmul(u, [(w_in[:, main_w:].astype(BF16), 0)], w_in.shape[1] - main_w, epilogue=_ep_sigmoid,
                   name="nsa_gates")
    gates = jnp.transpose(gates.reshape(bsz, s_len, groups, rep * 3), (0, 2, 3, 1))
    slot = lambda j: (qw + j * kvw) // dh
    qvt = nsa_rope_t(proj, list(range(n_heads)) + [slot(3) + g for g in range(groups)]
                     + [slot(5) + g for g in range(groups)], n_heads, dh, scale * math.log2(math.e), group=groups)
    k_rot = nsa_rope(proj, 0, [slot(2) + g for g in range(groups)] + [slot(4) + g for g in range(groups)], dh, 1.0)
    kc = nsa_compress(proj[..., qw:qw + kvw], w["nsa_cmp_pos_k"], w["nsa_cmp_k_w1"], w["nsa_cmp_k_w2"],
                      bsz, s_len, groups, dh, True)
    vct = nsa_compress(proj[..., qw + kvw:qw + 2 * kvw], w["nsa_cmp_pos_v"], w["nsa_cmp_v_w1"], w["nsa_cmp_v_w2"],
                       bsz, s_len, groups, dh, False, transpose_out=True)
    nc = kc.shape[1]
    cs = jnp.arange(nc)[None, :] * NSA_CMP_STRIDE
    ss = jnp.arange(n_sel)[:, None] * NSA_SEL_BLOCK
    overlap_t = jnp.clip(jnp.minimum(cs + NSA_CMP_BLOCK, ss + NSA_SEL_BLOCK) - jnp.maximum(cs, ss), 0, None)
    overlap_t = overlap_t.astype(F32) / NSA_CMP_BLOCK

    qt_spec3 = pl.BlockSpec((1, rep * dh, tq), lambda b_, g, i: (b_, g, i))
    o_c, sel = pl.pallas_call(
        functools.partial(_nsa_cmp_select_t_kernel, tq=tq, rep=rep, dh=dh, topn=topn),
        grid=(bsz, groups, nq),
        in_specs=[qt_spec3,
                  pl.BlockSpec((1, nc, dh), lambda b_, g, i: (b_ * groups + g, 0, 0)),
                  pl.BlockSpec((1, dh, nc), lambda b_, g, i: (b_ * groups + g, 0, 0)),
                  pl.BlockSpec((n_sel, nc), lambda b_, g, i: (0, 0))],
        out_specs=[qt_spec3, pl.BlockSpec((1, 1, n_sel, tq), lambda b_, g, i: (b_, g, 0, i))],
        out_shape=[jax.ShapeDtypeStruct((bsz, qw, s_len), F32),
                   jax.ShapeDtypeStruct((bsz, groups, n_sel, s_len), F32)],
        compiler_params=_params("parallel", "parallel", "parallel"),
        name="nsa_cmp_select",
    )(qvt, kc, vct, overlap_t)

    flash_scratch = lambda n: [pltpu.VMEM((rep, 1, n), F32), pltpu.VMEM((rep, 1, n), F32),
                               pltpu.VMEM((rep, dh, n), F32)]
    tqs = min(2 * tq, s_len)
    pairs = [(i, j) for i in range(s_len // tqs) for j in range((i * tqs + tqs - 1) // kb + 1)]
    qi_of = jnp.asarray([pr[0] for pr in pairs], jnp.int32)
    kj_of = jnp.asarray([pr[1] for pr in pairs], jnp.int32)
    o_s = pl.pallas_call(
        functools.partial(_nsa_select_t_kernel, tq=tqs, kb=kb, rep=rep, dh=dh),
        grid_spec=pltpu.PrefetchScalarGridSpec(
            num_scalar_prefetch=2,
            grid=(bsz, groups, len(pairs)),
            in_specs=[pl.BlockSpec((1, rep * dh, tqs), lambda b_, g, pr, qi, kj: (b_, g, qi[pr])),
                      pl.BlockSpec((1, kb, dh), lambda b_, g, pr, qi, kj: (b_, kj[pr], g)),
                      pl.BlockSpec((1, dh, kb), lambda b_, g, pr, qi, kj: (b_, n_heads + g, kj[pr])),
                      pl.BlockSpec((1, 1, n_sel, tqs), lambda b_, g, pr, qi, kj: (b_, g, 0, qi[pr]))],
            out_specs=pl.BlockSpec((1, rep * dh, tqs), lambda b_, g, pr, qi, kj: (b_, g, qi[pr])),
            scratch_shapes=flash_scratch(tqs)),
        out_shape=jax.ShapeDtypeStruct((bsz, qw, s_len), F32),
        compiler_params=_params("parallel", "parallel", "arbitrary"),
        name="nsa_select_attn",
    )(qi_of, kj_of, qvt, k_rot, qvt, sel)

    n_steps = NSA_WINDOW // kb + tqs // kb
    win_blk = lambda i, j: jnp.maximum(i * (tqs // kb) - (n_steps - tqs // kb) + j, 0)
    qt_spec = pl.BlockSpec((1, rep * dh, tqs), lambda b_, g, i, j: (b_, g, i))
    o = pl.pallas_call(
        functools.partial(_nsa_window_t_kernel, tq=tqs, kb=kb, rep=rep, dh=dh, window=NSA_WINDOW, n_steps=n_steps),
        grid=(bsz, groups, s_len // tqs, n_steps),
        in_specs=[qt_spec,
                  pl.BlockSpec((1, kb, dh), lambda b_, g, i, j: (b_, win_blk(i, j), groups + g)),
                  pl.BlockSpec((1, dh, kb), lambda b_, g, i, j: (b_, n_heads + groups + g, win_blk(i, j))),
                  qt_spec, qt_spec,
                  pl.BlockSpec((1, 1, rep * 3, tqs), lambda b_, g, i, j: (b_, g, 0, i))],
        out_specs=pl.BlockSpec((1, tqs, rep * dh), lambda b_, g, i, j: (b_, i, g)),
        out_shape=jax.ShapeDtypeStruct((bsz, s_len, qw), BF16),
        scratch_shapes=flash_scratch(tqs),
        compiler_params=_params("parallel", "parallel", "parallel", "arbitrary"),
        name="nsa_window_attn",
    )(qvt, k_rot, qvt, o_c, o_s, gates)
    return matmul_ws(o.reshape(t, qw), [(w["nsa_w_out"], 0)], d, epilogue=_ep_residual, extras=[(h, "mn")],
                     name="nsa_out")


_MATMUL_WEIGHTS = ("pl_proj", "rw_w1", "rw_w2", "rw_a1", "rw_a2", "rw_g1", "rw_g2")


def kernel(x, p, norm_mix, norm_ffn, norm_pl, pl_proj, pl_gate, norm_final, mb_w_in, mb_conv_w, mb_conv_b, mb_dt_bias, mb_a_log, mb_d_skip, mb_norm_w, mb_w_out, nsa_w_in, nsa_cmp_pos_k, nsa_cmp_pos_v, nsa_cmp_k_w1, nsa_cmp_k_w2, nsa_cmp_v_w1, nsa_cmp_v_w2, nsa_w_out, hg_w_in, hg_lb_logits, hg_norm_w, hg_w_out, rw_mu, rw_w_rkv, rw_w0, rw_w1, rw_w2, rw_a0, rw_a1, rw_a2, rw_g1, rw_g2, rw_k_k, rw_k_a, rw_r_k, rw_ln_w, rw_ln_b, rw_w_out, ffn0_w_in, ffn0_w_out, moe1_router, moe1_w_in, moe1_w_out, ffn2_w_in, ffn2_w_out, moe3_router, moe3_w_in, moe3_w_out):
    w = dict(locals())
    for name in _MATMUL_WEIGHTS:
        w[name] = w[name].astype(BF16)
    bsz, s_len, d = x.shape
    depth = p.shape[0]
    t = bsz * s_len
    lb_all = jax.nn.softmax(hg_lb_logits.astype(F32), axis=0)
    lb_all = jnp.cumsum(lb_all, axis=0) - lb_all[0]
    dense = [(w["ffn0_w_in"], w["ffn0_w_out"]), (w["ffn2_w_in"], w["ffn2_w_out"])]
    moe = [(moe1_router, w["moe1_w_in"], w["moe1_w_out"]), (moe3_router, w["moe3_w_in"], w["moe3_w_out"])]
    p_bf = p.reshape(depth, t, p.shape[-1])
    h = x.reshape(t, d)
    for i in range(depth):
        kind = i % 4
        if kind == 0:
            h = mamba2_mixer(rmsnorm(h, norm_mix[i]), h, w, bsz, s_len)
        elif kind == 1:
            h = nsa_mixer_t(rmsnorm(h, norm_mix[i]), h, w, bsz, s_len)
        elif kind == 2:
            h = hgrn2_mixer(rmsnorm(h, norm_mix[i]), h, w, lb_all[i], bsz, s_len)
        else:
            h = rwkv7_mixer(rmsnorm(h, norm_mix[i], out_dtype=F32), h, w, bsz, s_len)
        v = rmsnorm(h, norm_ffn[i])
        if i % 2 == 0:
            h = dense_ffn(v, h, *dense[i // 2])
        else:
            h = moe_ffn_routed(v, h, *moe[i // 2])
        h = ple_gate(h, p_bf[i], norm_pl[i], w["pl_proj"][i], pl_gate, i)
    return rmsnorm(h, norm_final, out_dtype=F32).reshape(bsz, s_len, d)
```

```python
import functools
import math

import jax
import jax.numpy as jnp
from jax import lax
from jax.experimental import pallas as pl
from jax.experimental.pallas import tpu as pltpu

F32 = jnp.float32
BF16 = jnp.bfloat16

NORM_EPS = 1e-6
ROPE_THETA = 10000.0

V7X_VMEM_BYTES = 64 * 1024 * 1024
VMEM_LIMIT_BYTES = V7X_VMEM_BYTES - 8 * 1024 * 1024
LANES = 128

MB_HEAD_DIM = 64
MB_N_GROUPS = 8
MB_D_STATE = 128
MB_CONV = 4
MB_CHUNK = 128

NSA_HEAD_DIM = 128
NSA_N_KV = 4
NSA_CMP_BLOCK = 32
NSA_CMP_STRIDE = 16
NSA_SEL_BLOCK = 64
NSA_TOPK = 16
NSA_WINDOW = 512
NSA_FORCED_SCORE = 1e9

HG_HEAD_DIM = 128
HG_CHUNK = 32

RW_HEAD_DIM = 64
RW_LN_EPS = 64e-5
RW_CHUNK = 128

MOE_TOPK = 2


def _params(*semantics):
    return pltpu.CompilerParams(dimension_semantics=semantics, vmem_limit_bytes=VMEM_LIMIT_BYTES)


def _pick(n, target):
    if n <= target:
        return n
    for c in range(target, 0, -1):
        if n % c == 0:
            return c
    return n


def _silu(x):
    return x * jax.nn.sigmoid(x)


def _rmsnorm_kernel(x_ref, g_ref, o_ref):
    x = x_ref[...]
    ms = jnp.mean(x * x, axis=-1, keepdims=True)
    o_ref[...] = (x * lax.rsqrt(ms + NORM_EPS) * g_ref[...]).astype(o_ref.dtype)


def rmsnorm(x, gain, out_dtype=BF16, name="rmsnorm"):
    m, d = x.shape
    bm = _pick(m, 256)
    return pl.pallas_call(
        _rmsnorm_kernel,
        grid=(m // bm,),
        in_specs=[pl.BlockSpec((bm, d), lambda i: (i, 0)), pl.BlockSpec((1, d), lambda i: (0, 0))],
        out_specs=pl.BlockSpec((bm, d), lambda i: (i, 0)),
        out_shape=jax.ShapeDtypeStruct((m, d), out_dtype),
        compiler_params=_params("parallel"),
        name=name,
    )(x, gain.reshape(1, d).astype(F32))


def _mm_kernel(*refs, n_w, n_extra, nk, epilogue):
    x_ref = refs[0]
    w_refs = refs[1:1 + n_w]
    e_refs = refs[1 + n_w:1 + n_w + n_extra]
    o_ref = refs[1 + n_w + n_extra]
    acc_refs = refs[2 + n_w + n_extra:]
    x = x_ref[...]
    if nk == 1:
        accs = [jnp.dot(x, w[...], preferred_element_type=F32) for w in w_refs]
        o_ref[...] = epilogue(accs, [e[...] for e in e_refs]).astype(o_ref.dtype)
        return
    k = pl.program_id(2)

    @pl.when(k == 0)
    def _():
        for a in acc_refs:
            a[...] = jnp.zeros_like(a)

    for a, w in zip(acc_refs, w_refs):
        a[...] += jnp.dot(x, w[...], preferred_element_type=F32)

    @pl.when(k == nk - 1)
    def _():
        o_ref[...] = epilogue([a[...] for a in acc_refs], [e[...] for e in e_refs]).astype(o_ref.dtype)


def _first(accs, extras):
    return accs[0]


def matmul(x, ws, n_out, *, epilogue=_first, extras=(), out_dtype=F32, bm=1024, bn=512, bk=None, name="matmul"):
    m, kdim = x.shape
    bm = _pick(m, bm)
    bn = _pick(n_out, bn)
    if bk is None:
        bk = kdim if kdim <= 4096 else _pick(kdim, 4096)
    nk = kdim // bk
    assert kdim % bk == 0 and m % bm == 0 and n_out % bn == 0
    in_specs = [pl.BlockSpec((bm, bk), lambda i, j, k: (i, k))]
    args = [x]
    for w, off in ws:
        assert off % bn == 0 and w.shape[0] == kdim
        in_specs.append(pl.BlockSpec((bk, bn), functools.partial(lambda i, j, k, o: (k, j + o), o=off // bn)))
        args.append(w)
    for arr, kind in extras:
        if kind == "mn":
            in_specs.append(pl.BlockSpec((bm, bn), lambda i, j, k: (i, j)))
        elif kind == "m":
            in_specs.append(pl.BlockSpec((bm, arr.shape[1]), lambda i, j, k: (i, 0)))
        elif kind == "kn":
            in_specs.append(pl.BlockSpec((arr.shape[0], bn), lambda i, j, k: (0, j)))
        else:
            in_specs.append(pl.BlockSpec((1, bn), lambda i, j, k: (0, j)))
        args.append(arr)
    scratch = [pltpu.VMEM((bm, bn), F32) for _ in ws] if nk > 1 else []
    kern = functools.partial(_mm_kernel, n_w=len(ws), n_extra=len(extras), nk=nk, epilogue=epilogue)
    return pl.pallas_call(
        kern,
        grid=(m // bm, n_out // bn, nk),
        in_specs=in_specs,
        out_specs=pl.BlockSpec((bm, bn), lambda i, j, k: (i, j)),
        out_shape=jax.ShapeDtypeStruct((m, n_out), out_dtype),
        scratch_shapes=scratch,
        compiler_params=_params("parallel", "parallel", "arbitrary"),
        name=name,
    )(*args)


WS_CAST_CHUNK = 512


def _mm_ws_kernel(*refs, n_w, n_extra, epilogue):
    x_ref = refs[0]
    w_refs = refs[1:1 + n_w]
    e_refs = refs[1 + n_w:1 + n_w + n_extra]
    o_ref = refs[1 + n_w + n_extra]
    wb_refs = refs[2 + n_w + n_extra:]

    kdim = x_ref.shape[1]
    ck = _pick(kdim, WS_CAST_CHUNK)

    @pl.when(pl.program_id(1) == 0)
    def _():
        accs = [None] * n_w
        for c in range(kdim // ck):
            rows = slice(c * ck, (c + 1) * ck)
            xc = x_ref[:, rows]
            for n, (w, wb) in enumerate(zip(w_refs, wb_refs)):
                wc = (w[0, rows, :] if len(w.shape) == 3 else w[rows, :]).astype(BF16)
                wb[rows, :] = wc
                part = jnp.dot(xc, wc, preferred_element_type=F32)
                accs[n] = part if accs[n] is None else accs[n] + part
        o_ref[...] = epilogue(accs, [e[...] for e in e_refs]).astype(o_ref.dtype)

    @pl.when(pl.program_id(1) != 0)
    def _():
        x = x_ref[...]
        accs = [jnp.dot(x, wb[...], preferred_element_type=F32) for wb in wb_refs]
        o_ref[...] = epilogue(accs, [e[...] for e in e_refs]).astype(o_ref.dtype)


def matmul_ws(x, ws, n_out, *, epilogue=_first, extras=(), out_dtype=F32, bm=1024, bn=512, w_buffers=2,
              name="matmul_ws"):
    m, kdim = x.shape
    bm = _pick(m, bm)
    bn = _pick(n_out, bn)
    assert m % bm == 0 and n_out % bn == 0
    mode = {} if w_buffers == 2 else {"pipeline_mode": pl.Buffered(w_buffers)}
    in_specs = [pl.BlockSpec((bm, kdim), lambda j, i: (i, 0))]
    args = [x]
    for w, off in ws:
        if w.ndim == 3:
            e, o = off
            assert o % bn == 0 and w.shape[1] == kdim
            in_specs.append(pl.BlockSpec((1, kdim, bn), functools.partial(lambda j, i, e_, o_: (e_, 0, j + o_),
                                                                          e_=e, o_=o // bn), **mode))
        else:
            assert off % bn == 0 and w.shape[0] == kdim
            in_specs.append(pl.BlockSpec((kdim, bn), functools.partial(lambda j, i, o_: (0, j + o_), o_=off // bn),
                                         **mode))
        args.append(w)
    for arr, kind in extras:
        if kind == "mn":
            in_specs.append(pl.BlockSpec((bm, bn), lambda j, i: (i, j)))
        elif kind == "m":
            in_specs.append(pl.BlockSpec((bm, arr.shape[1]), lambda j, i: (i, 0)))
        elif kind == "kn":
            in_specs.append(pl.BlockSpec((arr.shape[0], bn), lambda j, i: (0, j)))
        else:
            in_specs.append(pl.BlockSpec((1, bn), lambda j, i: (0, j)))
        args.append(arr)
    kern = functools.partial(_mm_ws_kernel, n_w=len(ws), n_extra=len(extras), epilogue=epilogue)
    return pl.pallas_call(
        kern,
        grid=(n_out // bn, m // bm),
        in_specs=in_specs,
        out_specs=pl.BlockSpec((bm, bn), lambda j, i: (i, j)),
        out_shape=jax.ShapeDtypeStruct((m, n_out), out_dtype),
        scratch_shapes=[pltpu.VMEM((kdim, bn), BF16) for _ in ws],
        compiler_params=_params("parallel", "arbitrary"),
        name=name,
    )(*args)


def _ep_residual(accs, extras):
    return extras[0] + accs[0]


def _ep_swiglu(accs, extras):
    return _silu(accs[0]) * accs[1]


def _ep_bias(accs, extras):
    return accs[0] + extras[0]


def _ep_tanh(accs, extras):
    return jnp.tanh(accs[0])


def _ep_sigmoid(accs, extras):
    return jax.nn.sigmoid(accs[0])


def _ep_bias_sigmoid(accs, extras):
    return jax.nn.sigmoid(accs[0] + extras[0])


def _ep_rw_logdecay(accs, extras):
    w = -jax.nn.softplus(-(accs[0] + extras[0])) - 0.5
    return -jnp.exp(w)


def _ep_ple_gate(accs, extras):
    pp = jnp.dot(extras[1].astype(BF16), extras[2], preferred_element_type=F32)
    return extras[0] + pp * jax.nn.sigmoid(accs[0])


def _conv_silu_kernel(x_ref, w_ref, b_ref, o_ref, *, k_width):
    x = x_ref[0]
    row = lax.broadcasted_iota(jnp.int32, x.shape, 0)
    y = b_ref[...] + w_ref[k_width - 1:k_width, :] * x
    for j in range(k_width - 1):
        shift = k_width - 1 - j
        xs = jnp.where(row >= shift, pltpu.roll(x, shift, 0), 0.0)
        y = y + w_ref[j:j + 1, :] * xs
    o_ref[0] = _silu(y)


def conv_silu(x, w, b):
    bsz, s_len, c = x.shape
    cb = _pick(c, 256)
    k_width = w.shape[0]
    return pl.pallas_call(
        functools.partial(_conv_silu_kernel, k_width=k_width),
        grid=(bsz, c // cb),
        in_specs=[pl.BlockSpec((1, s_len, cb), lambda b_, j: (b_, 0, j)),
                  pl.BlockSpec((k_width, cb), lambda b_, j: (0, j)),
                  pl.BlockSpec((1, cb), lambda b_, j: (0, j))],
        out_specs=pl.BlockSpec((1, s_len, cb), lambda b_, j: (b_, 0, j)),
        out_shape=jax.ShapeDtypeStruct(x.shape, F32),
        compiler_params=_params("parallel", "parallel"),
        name="mamba_conv_silu",
    )(x, w, b.reshape(1, c))


def _cumsum_rows(x, n):
    row = lax.broadcasted_iota(jnp.int32, x.shape, 0)
    s = 1
    while s < n:
        x = x + jnp.where(row >= s, pltpu.roll(x, s, 0), 0.0)
        s *= 2
    return x


def _cumsum_lanes(x, n):
    col = lax.broadcasted_iota(jnp.int32, x.shape, 1)
    s = 1
    while s < n:
        x = x + jnp.where(col >= s, pltpu.roll(x, s, 1), 0.0)
        s *= 2
    return x


def _dot_nt(a, b):
    return lax.dot_general(a, b, (((1,), (1,)), ((), ())), preferred_element_type=F32)


def _dot_tn(a, b):
    return lax.dot_general(a, b, (((0,), (0,)), ((), ())), preferred_element_type=F32)


def _ssd_kernel(xs_ref, b_ref, c_ref, z_ref, dt_ref, dtt_ref, bias_r_ref, bias_c_ref, alog_r_ref, alog_c_ref,
                dskip_ref, normw_ref, o_ref, state_ref, y_ref, *, chunk, heads, p_dim):
    @pl.when(pl.program_id(2) == 0)
    def _():
        state_ref[...] = jnp.zeros_like(state_ref)

    dt = jax.nn.softplus(dt_ref[0, 0] + bias_r_ref[0])
    dtt = jax.nn.softplus(dtt_ref[0, 0] + bias_c_ref[0])
    a_cum = _cumsum_rows(dt * -jnp.exp(alog_r_ref[0]), chunk)
    a_cum_t = _cumsum_lanes(dtt * -jnp.exp(alog_c_ref[0]), chunk)
    xs = xs_ref[0]
    bmat = b_ref[0]
    cmat = c_ref[0].astype(BF16)
    cb = _dot_nt(cmat, bmat.astype(BF16))
    b_t = bmat.T.astype(BF16)
    li = lax.broadcasted_iota(jnp.int32, (chunk, chunk), 0)
    si = lax.broadcasted_iota(jnp.int32, (chunk, chunk), 1)
    causal = li >= si
    per = LANES // p_dim
    lane_seg = lax.broadcasted_iota(jnp.int32, (1, LANES), 1) // p_dim

    def pick(vals):
        out = vals[-1]
        for i in range(per - 2, -1, -1):
            out = jnp.where(lane_seg == i, vals[i], out)
        return out

    dot = functools.partial(jnp.dot, preferred_element_type=F32)
    es = range(heads)
    tiles = range(heads // per)
    head_row = lax.broadcasted_iota(jnp.int32, (heads, heads * LANES), 0)
    to_tile = jnp.where(lax.broadcasted_iota(jnp.int32, (heads, heads * LANES), 1) // LANES == head_row, 1.0, 0.0)
    cum_t = _dot_hi(a_cum, to_tile)
    of = lambda vals, i: [vals[i * per + j] for j in range(per)]
    tile = lambda x, i: x[:, i * LANES:(i + 1) * LANES]
    cum_c = jnp.concatenate([pick([tile(cum_t, e) for e in of(es, i)]) for i in tiles], axis=-1)
    dt_c = jnp.concatenate([pick([dt[:, e:e + 1] for e in of(es, i)]) for i in tiles], axis=-1)
    last_c = cum_c[chunk - 1:chunk, :]
    m = [(cb * jnp.exp(jnp.where(causal, tile(cum_t, e) - a_cum_t[e:e + 1, :], -jnp.inf))).astype(BF16) for e in es]
    xdt = xs * dt_c
    xdt_b = xdt.astype(BF16)
    xend_b = (xdt * jnp.exp(last_c - cum_c)).astype(BF16)
    grow = jnp.exp(cum_c)
    st_decay = jnp.exp(last_c)
    st = [state_ref[i] for i in tiles]
    y_in = [pick([dot(m[e], tile(xdt_b, i)) for e in of(es, i)]) for i in tiles]
    y_st = [dot(cmat, st[i].astype(BF16)) * tile(grow, i) for i in tiles]
    for i in tiles:
        state_ref[i] = st[i] * tile(st_decay, i) + dot(b_t, tile(xend_b, i))
        y_ref[:, i * LANES:(i + 1) * LANES] = y_in[i] + y_st[i]
    y = y_ref[...] + xs * dskip_ref[...]
    y = y * _silu(z_ref[0])
    ms = jnp.mean(y * y, axis=-1, keepdims=True)
    o_ref[0] = (y * lax.rsqrt(ms + NORM_EPS) * normw_ref[...]).astype(o_ref.dtype)


def ssd_scan(xbc, z, dt, dt_bias, a_log, d_skip, norm_w, *, chunk=MB_CHUNK):
    bsz, s_len, d_inner = z.shape
    n_heads = dt.shape[-1]
    n_state = MB_D_STATE
    groups = (xbc.shape[-1] - d_inner) // (2 * n_state)
    heads = n_heads // groups
    p_dim = d_inner // n_heads
    gw = heads * p_dim
    assert gw % LANES == 0 and d_inner % n_state == 0
    chunk = min(chunk, s_len)
    nc = s_len // chunk
    b_off = d_inner // n_state
    c_off = b_off + groups
    dt_g = jnp.transpose(dt.reshape(bsz, s_len, groups, heads), (0, 2, 1, 3))
    dt_gt = jnp.transpose(dt_g, (0, 1, 3, 2))
    kern = functools.partial(_ssd_kernel, chunk=chunk, heads=heads, p_dim=p_dim)
    per_group = lambda b_, g, c: (g, 0, 0)
    return pl.pallas_call(
        kern,
        grid=(bsz, groups, nc),
        in_specs=[pl.BlockSpec((1, chunk, gw), lambda b_, g, c: (b_, c, g)),
                  pl.BlockSpec((1, chunk, n_state), lambda b_, g, c: (b_, c, b_off + g)),
                  pl.BlockSpec((1, chunk, n_state), lambda b_, g, c: (b_, c, c_off + g)),
                  pl.BlockSpec((1, chunk, gw), lambda b_, g, c: (b_, c, g)),
                  pl.BlockSpec((1, 1, chunk, heads), lambda b_, g, c: (b_, g, c, 0)),
                  pl.BlockSpec((1, 1, heads, chunk), lambda b_, g, c: (b_, g, 0, c)),
                  pl.BlockSpec((1, 1, heads), per_group),
                  pl.BlockSpec((1, heads, 1), per_group),
                  pl.BlockSpec((1, 1, heads), per_group),
                  pl.BlockSpec((1, heads, 1), per_group),
                  pl.BlockSpec((1, gw), lambda b_, g, c: (0, g)),
                  pl.BlockSpec((1, gw), lambda b_, g, c: (0, g))],
        out_specs=pl.BlockSpec((1, chunk, gw), lambda b_, g, c: (b_, c, g)),
        out_shape=jax.ShapeDtypeStruct(z.shape, BF16),
        scratch_shapes=[pltpu.VMEM((gw // LANES, n_state, LANES), F32), pltpu.VMEM((chunk, gw), F32)],
        compiler_params=_params("parallel", "parallel", "arbitrary"),
        name="mamba_ssd",
    )(xbc, xbc, xbc, z, dt_g, dt_gt,
      dt_bias.reshape(groups, 1, heads), dt_bias.reshape(groups, heads, 1),
      a_log.reshape(groups, 1, heads), a_log.reshape(groups, heads, 1),
      jnp.repeat(d_skip, p_dim).reshape(1, d_inner), norm_w.reshape(1, d_inner))


def mamba2_mixer(u, h, w, bsz, s_len):
    d_inner = w["mb_w_out"].shape[0]
    n_heads = w["mb_dt_bias"].shape[0]
    w_in = w["mb_w_in"]
    xbc_w = w_in.shape[1] - d_inner - n_heads
    z = matmul_ws(u, [(w_in, 0)], d_inner, name="mb_in_z")
    xbc = matmul_ws(u, [(w_in, d_inner)], xbc_w, name="mb_in_xbc")
    dt = matmul_ws(u, [(w_in, d_inner + xbc_w)], n_heads, name="mb_in_dt")
    xbc = conv_silu(xbc.reshape(bsz, s_len, xbc_w), w["mb_conv_w"], w["mb_conv_b"])
    y = ssd_scan(xbc, z.reshape(bsz, s_len, d_inner), dt.reshape(bsz, s_len, n_heads),
                 w["mb_dt_bias"], w["mb_a_log"], w["mb_d_skip"], w["mb_norm_w"])
    return matmul_ws(y.reshape(bsz * s_len, d_inner), [(w["mb_w_out"], 0)], h.shape[1],
                     epilogue=_ep_residual, extras=[(h, "mn")], bm=512, w_buffers=1, name="mb_out")


def _seg_cumsum_rows(x, seg, reverse=False):
    n = x.shape[0]
    pos = lax.broadcasted_iota(jnp.int32, x.shape, 0) % seg
    s = 1
    while s < seg:
        if reverse:
            x = x + jnp.where(pos < seg - s, pltpu.roll(x, n - s, 0), 0.0)
        else:
            x = x + jnp.where(pos >= s, pltpu.roll(x, s, 0), 0.0)
        s *= 2
    return x


def _hgrn_kernel(q_ref, f_ref, i_ref, g_ref, lb_ref, nw_ref, o_ref, state_ref, *, sub, n_sub, heads, dk):
    @pl.when(pl.program_id(2) == 0)
    def _():
        state_ref[...] = jnp.zeros_like(state_ref)

    lb = lb_ref[...]
    nw = nw_ref[...]
    ti = lax.broadcasted_iota(jnp.int32, (sub, sub), 0)
    si = lax.broadcasted_iota(jnp.int32, (sub, sub), 1)
    causal = ti >= si
    f = lb + (1.0 - lb) * jax.nn.sigmoid(f_ref[0])
    lf = jnp.log(f)
    k = 1.0 - f
    b = _seg_cumsum_rows(lf, sub)
    to_end = _seg_cumsum_rows(lf, sub, reverse=True) - lf
    q_dec = (_silu(q_ref[0]) * jnp.exp(b)).astype(BF16)
    k_dec = (k * jnp.exp(-b)).astype(BF16)
    k_end = (k * jnp.exp(to_end)).astype(BF16)
    v = i_ref[0].astype(BF16)
    cs = range(n_sub)
    hs = range(heads)
    blk = lambda x, c, h: x[c * sub:(c + 1) * sub, h * dk:(h + 1) * dk]
    scores = [[jnp.where(causal, _dot_nt(blk(q_dec, c, h), blk(k_dec, c, h)), 0.0).astype(BF16) for h in hs]
              for c in cs]
    upd = [[_dot_tn(blk(v, c, h), blk(k_end, c, h)) for h in hs] for c in cs]
    states = []
    st = [state_ref[h] for h in hs]
    for c in cs:
        states.append(st)
        decay = jnp.exp(b[(c + 1) * sub - 1:(c + 1) * sub, :])
        st = [st[h] * decay[:, h * dk:(h + 1) * dk] + upd[c][h] for h in hs]
    for h in hs:
        state_ref[h] = st[h]
    for c in cs:
        rows = slice(c * sub, (c + 1) * sub)
        for h in hs:
            o = (jnp.dot(scores[c][h], blk(v, c, h), preferred_element_type=F32)
                 + _dot_nt(blk(q_dec, c, h), states[c][h].astype(BF16)))
            o = o * lax.rsqrt(jnp.mean(o * o, axis=-1, keepdims=True) + NORM_EPS) * nw
            cols = slice(h * dk, (h + 1) * dk)
            o_ref[0, rows, cols] = (o * _silu(g_ref[0, rows, cols])).astype(o_ref.dtype)


def hgrn2_scan(proj, lower_bound, norm_w, *, dk=HG_HEAD_DIM, sub=HG_CHUNK, tb=256, heads=4):
    bsz, s_len, d4 = proj.shape
    d = d4 // 4
    n_heads = d // dk
    tb = min(tb, s_len)
    heads = min(heads, n_heads)
    hw = heads * dk
    n_hb = n_heads // heads
    kern = functools.partial(_hgrn_kernel, sub=sub, n_sub=tb // sub, heads=heads, dk=dk)
    spec = lambda part: pl.BlockSpec((1, tb, hw), lambda b_, h_, t: (b_, t, part * n_hb + h_))
    return pl.pallas_call(
        kern,
        grid=(bsz, n_hb, s_len // tb),
        in_specs=[spec(0), spec(1), spec(2), spec(3),
                  pl.BlockSpec((1, hw), lambda b_, h_, t: (0, h_)),
                  pl.BlockSpec((1, dk), lambda b_, h_, t: (0, 0))],
        out_specs=pl.BlockSpec((1, tb, hw), lambda b_, h_, t: (b_, t, h_)),
        out_shape=jax.ShapeDtypeStruct((bsz, s_len, d), BF16),
        scratch_shapes=[pltpu.VMEM((heads, dk, dk), F32)],
        compiler_params=_params("parallel", "parallel", "arbitrary"),
        name="hgrn2_scan",
    )(proj, proj, proj, proj, lower_bound.reshape(1, d), norm_w.reshape(1, dk))


def hgrn2_mixer(u, h, w, lower_bound, bsz, s_len):
    d = h.shape[1]
    proj = matmul_ws(u, [(w["hg_w_in"], 0)], 4 * d, name="hg_in")
    o = hgrn2_scan(proj.reshape(bsz, s_len, 4 * d), lower_bound, w["hg_norm_w"])
    return matmul_ws(o.reshape(bsz * s_len, d), [(w["hg_w_out"], 0)], d,
                     epilogue=_ep_residual, extras=[(h, "mn")], name="hg_out")


def dense_ffn(v, h, w_in, w_out):
    f = w_out.shape[0]
    hid = matmul_ws(v, [(w_in, 0), (w_in, f)], f, epilogue=_ep_swiglu, out_dtype=BF16, bm=1024, bn=256,
                    name="ffn_in")
    return matmul_ws(hid, [(w_out, 0)], h.shape[1], epilogue=_ep_residual, extras=[(h, "mn")], bm=512, w_buffers=1,
                     name="ffn_out")


def _router_kernel(x_ref, r_ref, o_ref, *, n_experts):
    logits = jnp.dot(x_ref[...], r_ref[...], preferred_element_type=F32)
    lane = lax.broadcasted_iota(jnp.int32, logits.shape, 1)
    logits = jnp.where(lane < n_experts, logits, -jnp.inf)
    m1 = jnp.max(logits, axis=-1, keepdims=True)
    i1 = jnp.min(jnp.where(logits == m1, lane, LANES), axis=-1, keepdims=True)
    rest = jnp.where(lane == i1, -jnp.inf, logits)
    m2 = jnp.max(rest, axis=-1, keepdims=True)
    i2 = jnp.min(jnp.where(rest == m2, lane, LANES), axis=-1, keepdims=True)
    e2 = jnp.exp(m2 - m1)
    w1 = 1.0 / (1.0 + e2)
    o_ref[...] = jnp.where(lane == i1, w1, 0.0) + jnp.where(lane == i2, e2 * w1, 0.0)


def moe_router(v, router):
    m, d = v.shape
    n_experts = router.shape[1]
    r_pad = jnp.zeros((d, LANES), BF16).at[:, :n_experts].set(router.astype(BF16))
    bm = _pick(m, 512)
    return pl.pallas_call(
        functools.partial(_router_kernel, n_experts=n_experts),
        grid=(m // bm,),
        in_specs=[pl.BlockSpec((bm, d), lambda i: (i, 0)), pl.BlockSpec((d, LANES), lambda i: (0, 0))],
        out_specs=pl.BlockSpec((bm, LANES), lambda i: (i, 0)),
        out_shape=jax.ShapeDtypeStruct((m, LANES), F32),
        compiler_params=_params("parallel"),
        name="moe_router",
    )(v, r_pad)


def _moe_in_kernel(x_ref, wg_ref, wu_ref, c_ref, o_ref, wgb_ref, wub_ref, *, blocks_per_expert):
    @pl.when(pl.program_id(1) == 0)
    def _():
        wgb_ref[...] = wg_ref[0].astype(BF16)
        wub_ref[...] = wu_ref[0].astype(BF16)

    x = x_ref[...]
    g = jnp.dot(x, wgb_ref[...], preferred_element_type=F32)
    u = jnp.dot(x, wub_ref[...], preferred_element_type=F32)
    e = pl.program_id(0) // blocks_per_expert
    comb = c_ref[...]
    lane = lax.broadcasted_iota(jnp.int32, comb.shape, 1)
    scale = jnp.sum(jnp.where(lane == e, comb, 0.0), axis=-1, keepdims=True)
    o_ref[...] = (_silu(g) * u * scale).astype(o_ref.dtype)


def moe_ffn(v, h, router, w_in, w_out, *, bm=512, bn=512):
    m, d = v.shape
    n_experts, _, two_de = w_in.shape
    de = two_de // 2
    bm = _pick(m, bm)
    bn = _pick(de, bn)
    bpe = de // bn
    comb = moe_router(v, router)
    hid = pl.pallas_call(
        functools.partial(_moe_in_kernel, blocks_per_expert=bpe),
        grid=(n_experts * bpe, m // bm),
        in_specs=[pl.BlockSpec((bm, d), lambda j, i: (i, 0)),
                  pl.BlockSpec((1, d, bn), lambda j, i: (j // bpe, 0, j % bpe)),
                  pl.BlockSpec((1, d, bn), lambda j, i: (j // bpe, 0, j % bpe + bpe)),
                  pl.BlockSpec((bm, LANES), lambda j, i: (i, 0))],
        out_specs=pl.BlockSpec((bm, bn), lambda j, i: (i, j)),
        out_shape=jax.ShapeDtypeStruct((m, n_experts * de), BF16),
        scratch_shapes=[pltpu.VMEM((d, bn), BF16), pltpu.VMEM((d, bn), BF16)],
        compiler_params=_params("parallel", "arbitrary"),
        name="moe_in",
    )(v, w_in, w_in, comb)
    return matmul(hid, [(w_out.reshape(n_experts * de, d), 0)], d, epilogue=_ep_residual, extras=[(h, "mn")],
                  name="moe_out")


MOE_BLOCK = 1024
MOE_UNIT = 128
MOE_TILE = 512


def _moe_gather_kernel(x_ref, tok_ref, o_ref):
    tok = tok_ref[0]
    lane = lax.broadcasted_iota(jnp.int32, (tok.shape[0], x_ref.shape[0]), 1)
    onehot = jnp.where(tok == lane, 1.0, 0.0).astype(BF16)
    o_ref[...] = jnp.dot(onehot, x_ref[...], preferred_element_type=F32).astype(o_ref.dtype)


def _moe_expert_in_kernel(src_ref, exp_ref, first_ref, used_ref, *refs, per):
    x_refs = refs[:per]
    wg_ref, wu_ref, rw_ref, o_ref, wgb_ref, wub_ref, x_scr = refs[per:]
    t = pl.program_id(1)

    @pl.when(t < used_ref[0])
    def _():
        @pl.when(first_ref[t] == 1)
        def _():
            wgb_ref[...] = wg_ref[0].astype(BF16)
            wub_ref[...] = wu_ref[0].astype(BF16)

        unit = x_refs[0].shape[0]
        for i in range(per):
            x_scr[i * unit:(i + 1) * unit, :] = x_refs[i][...]
        x = x_scr[...]
        g = jnp.dot(x, wgb_ref[...], preferred_element_type=F32)
        u = jnp.dot(x, wub_ref[...], preferred_element_type=F32)
        o_ref[...] = (_silu(g) * u * rw_ref[...]).astype(o_ref.dtype)

    @pl.when(t >= used_ref[0])
    def _():
        o_ref[...] = jnp.zeros_like(o_ref)


def _moe_expert_out_kernel(exp_ref, first_ref, used_ref, hid_ref, w_ref, o_ref, wb_ref):
    t = pl.program_id(1)

    @pl.when(t < used_ref[0])
    def _():
        @pl.when(first_ref[t] == 1)
        def _():
            wb_ref[...] = w_ref[0].astype(BF16)

        o_ref[...] = jnp.dot(hid_ref[...], wb_ref[...], preferred_element_type=F32).astype(o_ref.dtype)

    @pl.when(pl.program_id(1) >= used_ref[0])
    def _():
        o_ref[...] = jnp.zeros_like(o_ref)


def _moe_scatter_kernel(dst_ref, h_ref, tok_ref, *refs, per):
    y_refs = refs[:per]
    o_ref, y_scr = refs[per:]

    @pl.when(pl.program_id(2) == 0)
    def _():
        o_ref[...] = h_ref[...]

    tok = tok_ref[0]
    row = lax.broadcasted_iota(jnp.int32, (o_ref.shape[0], tok.shape[1]), 0)
    onehot_t = jnp.where(tok == row, 1.0, 0.0).astype(BF16)
    unit = y_refs[0].shape[0]
    for i in range(per):
        y_scr[i * unit:(i + 1) * unit, :] = y_refs[i][...]
    o_ref[...] += jnp.dot(onehot_t, y_scr[...], preferred_element_type=F32)


def moe_ffn_routed(v, h, router, w_in, w_out, *, tb=MOE_BLOCK, unit=MOE_UNIT, tile=MOE_TILE, bn=512, bo=1024):
    m, d = v.shape
    n_experts, _, two_de = w_in.shape
    de = two_de // 2
    tb = min(tb, m)
    nb = m // tb
    per = tile // unit
    bn = _pick(de, bn)
    bo = _pick(d, bo)
    n_assign = MOE_TOPK * tb
    n_slots = -(-(n_assign // unit + n_experts + 1) // per) * per
    groups = n_slots // per
    n_units = nb * (n_assign // unit + n_experts) + n_experts * (per - 1)
    n_tiles = -(-n_units // per)
    n_units = n_tiles * per

    comb = moe_router(v, router)
    wts, ids = lax.top_k(comb[:, :n_experts], MOE_TOPK)
    ea = ids.reshape(nb, n_assign)
    wa = wts.reshape(nb, n_assign)
    ta = jnp.broadcast_to(jnp.repeat(jnp.arange(tb, dtype=jnp.int32), MOE_TOPK)[None], (nb, n_assign))
    se, st, sw = lax.sort((ea, ta, wa), dimension=1, num_keys=1, is_stable=True)
    counts = jnp.sum(jax.nn.one_hot(ea, n_experts, dtype=jnp.int32), axis=1)
    units = -(-counts // unit)
    excl = lambda x, axis: jnp.cumsum(x, axis=axis) - x
    slot_start = excl(units, 1)
    row_start = excl(counts, 1)
    is_e = se[..., None] == jnp.arange(n_experts, dtype=jnp.int32)
    lookup = lambda table: jnp.sum(jnp.where(is_e, table[:, None, :], 0), axis=-1)
    pos = lookup(slot_start) * unit + jnp.arange(n_assign, dtype=jnp.int32)[None] - lookup(row_start)
    bidx = jnp.arange(nb, dtype=jnp.int32)[:, None]
    hit = pos[:, None, :] == jnp.arange(n_slots * unit, dtype=jnp.int32)[None, :, None]
    row_token = jnp.sum(jnp.where(hit, st[:, None, :] + 1, 0), axis=-1) - 1
    row_weight = jnp.sum(jnp.where(hit, sw[:, None, :], 0.0), axis=-1)
    per_expert = jnp.sum(units, axis=0)
    per_expert_pad = -(-per_expert // per) * per
    e_off = excl(per_expert_pad, 0)
    before = excl(units, 0)
    slot = jnp.arange(n_slots, dtype=jnp.int32)
    slot_end = jnp.cumsum(units, axis=1)
    e_of_slot = jnp.sum(slot[None, :, None] >= slot_end[:, None, :], axis=-1)
    used_slot = e_of_slot < n_experts
    e_clip = jnp.minimum(e_of_slot, n_experts - 1)
    dst_unit = (e_off[e_clip] + jnp.take_along_axis(before, e_clip, axis=1)
                + slot[None] - jnp.take_along_axis(slot_start, e_clip, axis=1))
    dst_unit = jnp.where(used_slot, dst_unit, 0).astype(jnp.int32)
    flat_slot = (bidx * n_slots + slot[None]).astype(jnp.int32)
    zero_slot = n_slots - 1
    src_unit = jnp.full((n_units,), zero_slot, jnp.int32).at[
        jnp.where(used_slot, dst_unit, n_units).reshape(-1)].set(flat_slot.reshape(-1), mode="drop")
    tile_end = jnp.cumsum(per_expert_pad) // per
    tile_ids = jnp.arange(n_tiles, dtype=jnp.int32)
    tile_expert = jnp.minimum(jnp.sum(tile_ids[:, None] >= tile_end[None, :], axis=-1), n_experts - 1).astype(jnp.int32)
    tiles_used = tile_end[-1:].astype(jnp.int32)
    first = jnp.concatenate([jnp.ones((1,), jnp.int32),
                             (tile_expert[1:] != tile_expert[:-1]).astype(jnp.int32)])
    rw_em = row_weight.reshape(nb * n_slots, unit)[src_unit].reshape(n_units * unit, 1)

    xs = pl.pallas_call(
        _moe_gather_kernel,
        grid=(nb, groups),
        in_specs=[pl.BlockSpec((tb, d), lambda b_, g: (b_, 0)),
                  pl.BlockSpec((1, tile, 1), lambda b_, g: (b_, g, 0))],
        out_specs=pl.BlockSpec((tile, d), lambda b_, g: (b_ * groups + g, 0)),
        out_shape=jax.ShapeDtypeStruct((nb * n_slots * unit, d), BF16),
        compiler_params=_params("parallel", "arbitrary"),
        name="moe_gather",
    )(v, row_token.reshape(nb, n_slots * unit, 1))

    bpe = de // bn
    unit_spec = lambda i: pl.BlockSpec((unit, d), lambda j, t, src, ex, fi, us: (src[per * t + i], 0))
    hid = pl.pallas_call(
        functools.partial(_moe_expert_in_kernel, per=per),
        grid_spec=pltpu.PrefetchScalarGridSpec(
            num_scalar_prefetch=4,
            grid=(bpe, n_tiles),
            in_specs=[unit_spec(i) for i in range(per)] + [
                pl.BlockSpec((1, d, bn), lambda j, t, src, ex, fi, us: (ex[t], 0, j),
                             pipeline_mode=pl.Buffered(1)),
                pl.BlockSpec((1, d, bn), lambda j, t, src, ex, fi, us: (ex[t], 0, j + bpe),
                             pipeline_mode=pl.Buffered(1)),
                pl.BlockSpec((tile, 1), lambda j, t, src, ex, fi, us: (t, 0))],
            out_specs=pl.BlockSpec((tile, bn), lambda j, t, src, ex, fi, us: (t, j)),
            scratch_shapes=[pltpu.VMEM((d, bn), BF16), pltpu.VMEM((d, bn), BF16), pltpu.VMEM((tile, d), BF16)]),
        out_shape=jax.ShapeDtypeStruct((n_tiles * tile, de), BF16),
        compiler_params=_params("arbitrary", "arbitrary"),
        name="moe_expert_in",
    )(src_unit, tile_expert, first, tiles_used, *([xs] * per), w_in, w_in, rw_em)

    bo2 = _pick(d, 2 * bo)
    ys = pl.pallas_call(
        _moe_expert_out_kernel,
        grid_spec=pltpu.PrefetchScalarGridSpec(
            num_scalar_prefetch=3,
            grid=(d // bo2, n_tiles),
            in_specs=[pl.BlockSpec((tile, de), lambda n, t, ex, fi, us: (t, 0)),
                      pl.BlockSpec((1, de, bo2), lambda n, t, ex, fi, us: (ex[t], 0, n))],
            out_specs=pl.BlockSpec((tile, bo2), lambda n, t, ex, fi, us: (t, n)),
            scratch_shapes=[pltpu.VMEM((de, bo2), BF16)]),
        out_shape=jax.ShapeDtypeStruct((n_tiles * tile, d), BF16),
        compiler_params=_params("arbitrary", "arbitrary"),
        name="moe_expert_out",
    )(tile_expert, first, tiles_used, hid, w_out)

    y_spec = lambda i: pl.BlockSpec((unit, bo2), lambda b_, n, g, dst: (dst[(b_ * groups + g) * per + i], n))
    return pl.pallas_call(
        functools.partial(_moe_scatter_kernel, per=per),
        grid_spec=pltpu.PrefetchScalarGridSpec(
            num_scalar_prefetch=1,
            grid=(nb, d // bo2, groups),
            in_specs=[pl.BlockSpec((tb, bo2), lambda b_, n, g, dst: (b_, n)),
                      pl.BlockSpec((1, 1, tile), lambda b_, n, g, dst: (b_ * groups + g, 0, 0))]
                     + [y_spec(i) for i in range(per)],
            out_specs=pl.BlockSpec((tb, bo2), lambda b_, n, g, dst: (b_, n)),
            scratch_shapes=[pltpu.VMEM((tile, bo2), BF16)]),
        out_shape=jax.ShapeDtypeStruct((m, d), F32),
        compiler_params=_params("parallel", "parallel", "arbitrary"),
        name="moe_scatter",
    )(dst_unit.reshape(-1), h, row_token.reshape(nb * groups, 1, tile), *([ys] * per))


def ple_gate(h, p_i, norm_pl, pl_proj, pl_gate, layer):
    d = h.shape[1]
    n = rmsnorm(h, norm_pl, name="rmsnorm_ple")
    return matmul_ws(n, [(pl_gate, (layer, 0))], d, epilogue=_ep_ple_gate,
                     extras=[(h, "mn"), (p_i, "m"), (pl_proj, "kn")], name="ple_gate")


def _rw_mix_kernel(u_ref, mu_ref, *o_refs):
    u = u_ref[0]
    row = lax.broadcasted_iota(jnp.int32, u.shape, 0)
    dx = jnp.where(row >= 1, pltpu.roll(u, 1, 0), 0.0) - u
    for j, o_ref in enumerate(o_refs):
        o_ref[0] = (u + dx * mu_ref[j:j + 1, :]).astype(o_ref.dtype)


def rw_token_mix(u, mu):
    bsz, s_len, d = u.shape
    cb = _pick(d, LANES)
    n_mix = mu.shape[0]
    spec = pl.BlockSpec((1, s_len, cb), lambda b_, j: (b_, 0, j))
    return pl.pallas_call(
        _rw_mix_kernel,
        grid=(bsz, d // cb),
        in_specs=[spec, pl.BlockSpec((n_mix, cb), lambda b_, j: (0, j))],
        out_specs=[spec] * n_mix,
        out_shape=[jax.ShapeDtypeStruct(u.shape, BF16)] * n_mix,
        compiler_params=_params("parallel", "parallel"),
        name="rwkv_token_mix",
    )(u, mu)


def _dot_hi(a, b):
    return jnp.dot(a, b, preferred_element_type=F32, precision=lax.Precision.HIGHEST)


def _rw_scan_kernel(r_ref, k_ref, v_ref, a_ref, lw_ref, g_ref, kk_ref, ka_ref, rk_ref, lnw_ref, lnb_ref,
                    o_ref, state_ref, *, chunk, heads, n):
    @pl.when(pl.program_id(2) == 0)
    def _():
        state_ref[...] = jnp.zeros_like(state_ref)

    hs = range(heads)
    sls = [slice(j * n, (j + 1) * n) for j in hs]
    ti = lax.broadcasted_iota(jnp.int32, (chunk, chunk), 0)
    si = lax.broadcasted_iota(jnp.int32, (chunk, chunk), 1)
    strict = ti > si
    incl = ti >= si
    dot = functools.partial(jnp.dot, preferred_element_type=F32)

    r = [r_ref[0, :, sl] for sl in sls]
    v = [v_ref[0, :, sl] for sl in sls]
    a = [a_ref[0, :, sl] for sl in sls]
    lw = [lw_ref[0, :, sl] for sl in sls]
    k = [k_ref[0, :, sl] for sl in sls]
    kk = [k[j] * kk_ref[:, sls[j]] for j in hs]
    kk = [kk[j] / jnp.maximum(jnp.sqrt(jnp.sum(kk[j] * kk[j], axis=-1, keepdims=True)), 1e-12) for j in hs]
    kmod = [k[j] * (1.0 + (a[j] - 1.0) * ka_ref[:, sls[j]]) for j in hs]
    kka = [kk[j] * a[j] for j in hs]
    cum = [_cumsum_rows(lw[j], chunk) for j in hs]
    cum_end = [c[chunk - 1:chunk, :] for c in cum]
    mid = [c[chunk // 2 - 1:chunk // 2, :] for c in cum]
    e_neg = [jnp.exp(mid[j] - cum[j]) for j in hs]
    am = [(kk[j] * jnp.exp(cum[j] - lw[j] - mid[j])).astype(BF16) for j in hs]
    bm = [(kka[j] * e_neg[j]).astype(BF16) for j in hs]
    km = [(kmod[j] * e_neg[j]).astype(BF16) for j in hs]
    rm = [(r[j] * jnp.exp(cum[j] - mid[j])).astype(BF16) for j in hs]
    a_abs = [(kk[j] * jnp.exp(cum[j] - lw[j])).astype(BF16) for j in hs]
    r_abs = [(r[j] * jnp.exp(cum[j])).astype(BF16) for j in hs]
    vb = [x.astype(BF16) for x in v]
    st = [state_ref[j] for j in hs]
    stb = [x.astype(BF16) for x in st]

    nb = [(-jnp.where(strict, _dot_nt(am[j], bm[j]), 0.0)).astype(BF16) for j in hs]
    lk = [jnp.where(strict, _dot_nt(am[j], km[j]), 0.0).astype(BF16) for j in hs]
    x = [_dot_nt(a_abs[j], stb[j]) + dot(lk[j], vb[j]) for j in hs]
    x = [x[j] + dot(nb[j], x[j].astype(BF16)) for j in hs]
    p = 2
    while p < chunk:
        nb = [dot(nb[j], nb[j]).astype(BF16) for j in hs]
        x = [x[j] + dot(nb[j], x[j].astype(BF16)) for j in hs]
        p *= 2
    pb = [xj.astype(BF16) for xj in x]
    mk = [jnp.where(incl, _dot_nt(rm[j], km[j]), 0.0).astype(BF16) for j in hs]
    mb = [jnp.where(incl, _dot_nt(rm[j], bm[j]), 0.0).astype(BF16) for j in hs]
    y = [_dot_nt(r_abs[j], stb[j]) + dot(mk[j], vb[j]) - dot(mb[j], pb[j]) for j in hs]
    to_end = [jnp.exp(cum_end[j] - cum[j]) for j in hs]
    for j in hs:
        state_ref[j] = (st[j] * jnp.exp(cum_end[j]) + _dot_tn(vb[j], (kmod[j] * to_end[j]).astype(BF16))
                        - _dot_tn(pb[j], (kka[j] * to_end[j]).astype(BF16)))
    for j in hs:
        sl = sls[j]
        bonus = jnp.sum(r[j] * kmod[j] * rk_ref[:, sl], axis=-1, keepdims=True) * v[j]
        mean = jnp.mean(y[j], axis=-1, keepdims=True)
        yc = y[j] - mean
        var = jnp.mean(yc * yc, axis=-1, keepdims=True)
        yn = yc * lax.rsqrt(var + RW_LN_EPS) * lnw_ref[:, sl] + lnb_ref[:, sl]
        o_ref[0, :, sl] = ((yn + bonus) * g_ref[0, :, sl]).astype(o_ref.dtype)


def _rw_scan_tile_kernel(r_ref, k_ref, v_ref, a_ref, lw_ref, g_ref, kk_ref, ka_ref, rk_ref, lnw_ref, lnb_ref,
                         o_ref, state_ref, *, chunk, heads, n):
    @pl.when(pl.program_id(2) == 0)
    def _():
        state_ref[...] = jnp.zeros_like(state_ref)

    per = LANES // n
    tiles = range(heads // per)
    sub = range(per)
    ti = lax.broadcasted_iota(jnp.int32, (chunk, chunk), 0)
    si = lax.broadcasted_iota(jnp.int32, (chunk, chunk), 1)
    strict = ti > si
    incl = ti >= si
    lane_seg = lax.broadcasted_iota(jnp.int32, (1, LANES), 1) // n
    seg_is = [lane_seg == j for j in sub]
    same_head = (lax.broadcasted_iota(jnp.int32, (LANES, LANES), 0) // n
                 == lax.broadcasted_iota(jnp.int32, (LANES, LANES), 1) // n)
    dot = functools.partial(jnp.dot, preferred_element_type=F32)
    tile = lambda x, i: x[:, i * LANES:(i + 1) * LANES]

    def pick(vals):
        out = vals[-1]
        for j in range(per - 2, -1, -1):
            out = jnp.where(seg_is[j], vals[j], out)
        return out

    def seg_sum(x):
        return pick([jnp.sum(jnp.where(seg_is[j], x, 0.0), axis=-1, keepdims=True) for j in sub])

    r, k, v, a, lw = r_ref[0], k_ref[0], v_ref[0], a_ref[0], lw_ref[0]
    kk = k * kk_ref[...]
    kmod = k * (1.0 + (a - 1.0) * ka_ref[...])
    cum = _cumsum_rows(lw, chunk)
    cum_end = cum[chunk - 1:chunk, :]
    mid = cum[chunk // 2 - 1:chunk // 2, :]
    bonus_in = r * kmod * rk_ref[...]
    kk_t, bonus_t = [], []
    for i in tiles:
        kki = tile(kk, i)
        kk_t.append(kki / jnp.maximum(jnp.sqrt(seg_sum(kki * kki)), 1e-12))
        bonus_t.append(seg_sum(tile(bonus_in, i)) * tile(v, i))
    kk = jnp.concatenate(kk_t, axis=-1) if len(kk_t) > 1 else kk_t[0]
    kka = kk * a
    e_neg = jnp.exp(mid - cum)
    to_end = jnp.exp(cum_end - cum)
    am = (kk * jnp.exp(cum - lw - mid)).astype(BF16)
    bm = (kka * e_neg).astype(BF16)
    km = (kmod * e_neg).astype(BF16)
    rm = (r * jnp.exp(cum - mid)).astype(BF16)
    a_abs = (kk * jnp.exp(cum - lw)).astype(BF16)
    r_abs = (r * jnp.exp(cum)).astype(BF16)
    k_end = (kmod * to_end).astype(BF16)
    b_end = (kka * to_end).astype(BF16)
    vb = v.astype(BF16)
    st_decay = jnp.exp(cum_end)
    zero = jnp.zeros((), BF16)

    st = [state_ref[i] for i in tiles]
    stb = [s.astype(BF16) for s in st]
    am_h = [[jnp.where(seg_is[j], tile(am, i), zero) for j in sub] for i in tiles]
    rm_h = [[jnp.where(seg_is[j], tile(rm, i), zero) for j in sub] for i in tiles]
    nb = [[(-jnp.where(strict, _dot_nt(am_h[i][j], tile(bm, i)), 0.0)).astype(BF16) for j in sub] for i in tiles]
    lk = [[jnp.where(strict, _dot_nt(am_h[i][j], tile(km, i)), 0.0).astype(BF16) for j in sub] for i in tiles]
    x = [_dot_nt(tile(a_abs, i), stb[i]) + pick([dot(lk[i][j], tile(vb, i)) for j in sub]) for i in tiles]
    xb = [xi.astype(BF16) for xi in x]
    x = [x[i] + pick([dot(nb[i][j], xb[i]) for j in sub]) for i in tiles]
    p = 2
    while p < chunk:
        nb = [[dot(nb[i][j], nb[i][j]).astype(BF16) for j in sub] for i in tiles]
        xb = [xi.astype(BF16) for xi in x]
        x = [x[i] + pick([dot(nb[i][j], xb[i]) for j in sub]) for i in tiles]
        p *= 2
    pb = [xi.astype(BF16) for xi in x]
    mk = [[jnp.where(incl, _dot_nt(rm_h[i][j], tile(km, i)), 0.0).astype(BF16) for j in sub] for i in tiles]
    mb = [[jnp.where(incl, _dot_nt(rm_h[i][j], tile(bm, i)), 0.0).astype(BF16) for j in sub] for i in tiles]
    y = [_dot_nt(tile(r_abs, i), stb[i])
         + pick([dot(mk[i][j], tile(vb, i)) - dot(mb[i][j], pb[i]) for j in sub]) for i in tiles]
    for i in tiles:
        upd = _dot_tn(tile(vb, i), tile(k_end, i)) - _dot_tn(pb[i], tile(b_end, i))
        state_ref[i] = st[i] * tile(st_decay, i) + jnp.where(same_head, upd, 0.0)
    inv_n = 1.0 / n
    for i in tiles:
        cols = slice(i * LANES, (i + 1) * LANES)
        mean = seg_sum(y[i]) * inv_n
        yc = y[i] - mean
        var = seg_sum(yc * yc) * inv_n
        yn = yc * lax.rsqrt(var + RW_LN_EPS) * lnw_ref[:, cols] + lnb_ref[:, cols]
        o_ref[0, :, cols] = ((yn + bonus_t[i]) * g_ref[0, :, cols]).astype(o_ref.dtype)


def rw_scan(r, k, v, a, lw, g, k_k, k_a, r_k, ln_w, ln_b, *, n=RW_HEAD_DIM, chunk=RW_CHUNK, heads=8):
    bsz, s_len, d = r.shape
    chunk = min(chunk, s_len)
    heads = min(heads, d // n)
    hw = heads * n
    seq = pl.BlockSpec((1, chunk, hw), lambda b_, h_, c: (b_, c, h_))
    par = pl.BlockSpec((1, hw), lambda b_, h_, c: (0, h_))
    row = lambda t: t.reshape(1, d)
    assert hw % LANES == 0 and LANES % n == 0
    kern = functools.partial(_rw_scan_tile_kernel, chunk=chunk, heads=heads, n=n)
    return pl.pallas_call(
        kern,
        grid=(bsz, d // hw, s_len // chunk),
        in_specs=[seq] * 6 + [par] * 5,
        out_specs=seq,
        out_shape=jax.ShapeDtypeStruct(r.shape, BF16),
        scratch_shapes=[pltpu.VMEM((hw // LANES, LANES, LANES), F32)],
        compiler_params=_params("parallel", "parallel", "arbitrary"),
        name="rwkv7_scan",
    )(r, k, v, a, lw, g, row(k_k), row(k_a), row(r_k), row(ln_w), row(ln_b))


def rwkv7_mixer(u, h, w, bsz, s_len):
    t, d = u.shape
    xr, xw, xk, xv, xa, xg = [x.reshape(t, d) for x in rw_token_mix(u.reshape(bsz, s_len, d), w["rw_mu"])]
    r = matmul_ws(xr, [(w["rw_w_rkv"], (0, 0))], d, name="rw_r")
    k = matmul_ws(xk, [(w["rw_w_rkv"], (1, 0))], d, name="rw_k")
    v = matmul_ws(xv, [(w["rw_w_rkv"], (2, 0))], d, name="rw_v")
    row = lambda x: x.reshape(1, d)
    w_lo = matmul(xw, [(w["rw_w1"], 0)], w["rw_w1"].shape[1], epilogue=_ep_tanh, out_dtype=BF16, name="rw_w1")
    wide = 2048
    lw = matmul(w_lo, [(w["rw_w2"], 0)], d, epilogue=_ep_rw_logdecay, extras=[(row(w["rw_w0"]), "n")], bn=wide,
                name="rw_w2")
    a_lo = matmul(xa, [(w["rw_a1"], 0)], w["rw_a1"].shape[1], out_dtype=BF16, name="rw_a1")
    a = matmul(a_lo, [(w["rw_a2"], 0)], d, epilogue=_ep_bias_sigmoid, extras=[(row(w["rw_a0"]), "n")], bn=wide,
               name="rw_a2")
    g_lo = matmul(xg, [(w["rw_g1"], 0)], w["rw_g1"].shape[1], epilogue=_ep_sigmoid, out_dtype=BF16, name="rw_g1")
    g = matmul(g_lo, [(w["rw_g2"], 0)], d, bn=wide, name="rw_g2")
    shp = (bsz, s_len, d)
    y = rw_scan(r.reshape(shp), k.reshape(shp), v.reshape(shp), a.reshape(shp), lw.reshape(shp), g.reshape(shp),
                w["rw_k_k"], w["rw_k_a"], w["rw_r_k"], w["rw_ln_w"], w["rw_ln_b"])
    return matmul_ws(y.reshape(t, d), [(w["rw_w_out"], 0)], d, epilogue=_ep_residual, extras=[(h, "mn")],
                     name="rw_out")


NEG_BIG = -1e30


def _rope_kernel(x_ref, cc_ref, ss_ref, o_ref, *, n_q_slots, scale):
    x = x_ref[0]
    out = x * cc_ref[...] + pltpu.roll(x, x.shape[-1] // 2, 1) * ss_ref[...]
    out = out * jnp.where(pl.program_id(2) < n_q_slots, scale, 1.0)
    o_ref[0] = out.astype(o_ref.dtype)


def _rope_tables(pos, dim):
    inv = ROPE_THETA ** (-(jnp.arange(0, dim, 2, dtype=F32) / dim))
    ang = pos.astype(F32)[:, None] * inv[None, :]
    cos, sin = jnp.cos(ang), jnp.sin(ang)
    return jnp.concatenate([cos, cos], axis=-1), jnp.concatenate([-sin, sin], axis=-1)


def nsa_rope(proj, n_q_slots, k_slots, dh, scale, tb=512):
    bsz, s_len, _ = proj.shape
    tb = min(tb, s_len)
    cc, ss = _rope_tables(jnp.arange(s_len), dh)
    n_out = n_q_slots + len(k_slots)

    def in_slot(j):
        slot = j
        for idx, ks in enumerate(k_slots):
            slot = jnp.where(j == n_q_slots + idx, ks, slot)
        return slot

    return pl.pallas_call(
        functools.partial(_rope_kernel, n_q_slots=n_q_slots, scale=scale),
        grid=(bsz, s_len // tb, n_out),
        in_specs=[pl.BlockSpec((1, tb, dh), lambda b_, t, j: (b_, t, in_slot(j))),
                  pl.BlockSpec((tb, dh), lambda b_, t, j: (t, 0)),
                  pl.BlockSpec((tb, dh), lambda b_, t, j: (t, 0))],
        out_specs=pl.BlockSpec((1, tb, dh), lambda b_, t, j: (b_, t, j)),
        out_shape=jax.ShapeDtypeStruct((bsz, s_len, n_out * dh), BF16),
        compiler_params=_params("parallel", "parallel", "arbitrary"),
        name="nsa_rope",
    )(proj, cc, ss)


def _cmp_finish_kernel(z_ref, bias_ref, w2_ref, cc_ref, ss_ref, o_ref, *, hidden, rope):
    z = z_ref[0]
    nc = z.shape[0]
    nxt = pltpu.roll(z[:, hidden:], nc - 1, 0)
    hid = _silu(z[:, :hidden] + nxt + bias_ref[...])
    out = jnp.dot(hid.astype(BF16), w2_ref[...], preferred_element_type=F32)
    if rope:
        out = out * cc_ref[...] + pltpu.roll(out, out.shape[-1] // 2, 1) * ss_ref[...]
    o_ref[0] = out.astype(o_ref.dtype)


def nsa_compress(x, pos_emb, w1, w2, bsz, s_len, groups, dh, rope, transpose_out=False):
    stride, blk = NSA_CMP_STRIDE, NSA_CMP_BLOCK
    nc = s_len // stride
    hidden = w1.shape[-1]
    half = stride * dh
    x16 = jnp.transpose(x.reshape(bsz, nc, stride, groups, dh), (0, 3, 1, 2, 4)).reshape(bsz * groups * nc, half)
    w1f = w1.reshape(blk * dh, hidden)
    wcat = jnp.concatenate([w1f[:half], w1f[half:]], axis=1).astype(BF16)
    z = matmul(x16.astype(BF16), [(wcat, 0)], 2 * hidden, name="nsa_cmp_w1")
    bias = matmul(pos_emb.reshape(1, blk * dh).astype(BF16), [(w1f.astype(BF16), 0)], hidden, name="nsa_cmp_pos")
    cc, ss = _rope_tables(jnp.arange(nc) * stride + blk - 1, dh)
    if transpose_out:
        assert not rope
        return pl.pallas_call(
            functools.partial(_cmp_finish_t_kernel, hidden=hidden),
            grid=(bsz * groups,),
            in_specs=[pl.BlockSpec((1, nc, 2 * hidden), lambda i: (i, 0, 0)),
                      pl.BlockSpec((1, hidden), lambda i: (0, 0)),
                      pl.BlockSpec((dh, hidden), lambda i: (0, 0))],
            out_specs=pl.BlockSpec((1, dh, nc), lambda i: (i, 0, 0)),
            out_shape=jax.ShapeDtypeStruct((bsz * groups, dh, nc), BF16),
            compiler_params=_params("parallel"),
            name="nsa_cmp_finish_t",
        )(z.reshape(bsz * groups, nc, 2 * hidden), bias, w2.T.astype(BF16))
    return pl.pallas_call(
        functools.partial(_cmp_finish_kernel, hidden=hidden, rope=rope),
        grid=(bsz * groups,),
        in_specs=[pl.BlockSpec((1, nc, 2 * hidden), lambda i: (i, 0, 0)),
                  pl.BlockSpec((1, hidden), lambda i: (0, 0)),
                  pl.BlockSpec((hidden, dh), lambda i: (0, 0)),
                  pl.BlockSpec((nc, dh), lambda i: (0, 0)),
                  pl.BlockSpec((nc, dh), lambda i: (0, 0))],
        out_specs=pl.BlockSpec((1, nc, dh), lambda i: (i, 0, 0)),
        out_shape=jax.ShapeDtypeStruct((bsz * groups, nc, dh), BF16),
        compiler_params=_params("parallel"),
        name="nsa_cmp_finish",
    )(z.reshape(bsz * groups, nc, 2 * hidden), bias, w2.astype(BF16), cc, ss)


def _nsa_cmp_select_kernel(q_ref, kc_ref, vc_ref, ov_ref, oc_ref, sel_ref, *, tq, rep, dh, topn):
    qi = pl.program_id(2)
    kc = kc_ref[0]
    vc = vc_ref[0]
    nc = kc.shape[0]
    n_sel = sel_ref.shape[-1]
    t = qi * tq + lax.broadcasted_iota(jnp.int32, (tq, nc), 0)
    cmp_end = lax.broadcasted_iota(jnp.int32, (tq, nc), 1) * NSA_CMP_STRIDE + (NSA_CMP_BLOCK - 1)
    visible = cmp_end <= t
    psum = jnp.zeros((tq, nc), F32)
    for r in range(rep):
        s = jnp.where(visible, _dot_nt(q_ref[0, :, r * dh:(r + 1) * dh], kc), NEG_BIG)
        m = jnp.max(s, axis=-1, keepdims=True)
        e = jnp.where(visible, jnp.exp(s - m), 0.0)
        den = jnp.sum(e, axis=-1, keepdims=True)
        p = e / jnp.where(den > 0, den, 1.0)
        oc_ref[0, :, r * dh:(r + 1) * dh] = jnp.dot(p.astype(BF16), vc, preferred_element_type=F32)
        psum = psum + p
    imp = _dot_hi(psum, ov_ref[...])
    blk = lax.broadcasted_iota(jnp.int32, (tq, n_sel), 1)
    cur = (qi * tq + lax.broadcasted_iota(jnp.int32, (tq, n_sel), 0)) // NSA_SEL_BLOCK
    forced = (blk == 0) | (blk == cur) | (blk == cur - 1)
    imp = jnp.where(forced, NSA_FORCED_SCORE, imp)
    imp = jnp.where(blk > cur, -jnp.inf, imp)
    sel = jnp.zeros((tq, n_sel), F32)
    for _ in range(topn):
        m = jnp.max(imp, axis=-1, keepdims=True)
        first = jnp.min(jnp.where(imp == m, blk, n_sel), axis=-1, keepdims=True)
        hit = blk == first
        sel = jnp.where(hit, 1.0, sel)
        imp = jnp.where(hit, -jnp.inf, imp)
    sel_ref[0, 0] = sel


def _flash_step(q_scr, k, v, mask, m_ref, l_ref, acc_ref, rep, tq):
    kb = k.shape[0]
    s = _dot_nt(q_scr[...], k).reshape(rep, tq, kb)
    s = jnp.where(mask[None], s, NEG_BIG)
    m_old = m_ref[...].reshape(rep, tq, -1)[:, :, :1]
    m_new = jnp.maximum(m_old, jnp.max(s, axis=-1, keepdims=True))
    p = jnp.where(mask[None], jnp.exp(s - m_new), 0.0)
    alpha = jnp.exp(m_old - m_new)
    l_old = l_ref[...].reshape(rep, tq, -1)[:, :, :1]
    l_new = alpha * l_old + jnp.sum(p, axis=-1, keepdims=True)
    pv = jnp.dot(p.reshape(rep * tq, kb).astype(BF16), v, preferred_element_type=F32)
    acc_ref[...] = (alpha * acc_ref[...].reshape(rep, tq, -1)).reshape(rep * tq, -1) + pv
    m_ref[...] = jnp.broadcast_to(m_new, (rep, tq, m_ref.shape[-1])).reshape(m_ref.shape)
    l_ref[...] = jnp.broadcast_to(l_new, (rep, tq, l_ref.shape[-1])).reshape(l_ref.shape)


def _flash_init(q_ref, q_scr, m_ref, l_ref, acc_ref, rep, tq, dh):
    for r in range(rep):
        q_scr[r * tq:(r + 1) * tq, :] = q_ref[0, :, r * dh:(r + 1) * dh]
    m_ref[...] = jnp.full_like(m_ref, NEG_BIG)
    l_ref[...] = jnp.zeros_like(l_ref)
    acc_ref[...] = jnp.zeros_like(acc_ref)


def _flash_result(l_ref, acc_ref):
    l = l_ref[...][:, :1]
    return acc_ref[...] / jnp.where(l > 0, l, 1.0)


def _nsa_select_kernel(q_ref, k_ref, v_ref, sel_ref, o_ref, q_scr, m_ref, l_ref, acc_ref, *, tq, kb, rep, dh):
    qi = pl.program_id(2)
    kj = pl.program_id(3)

    @pl.when(kj == 0)
    def _():
        _flash_init(q_ref, q_scr, m_ref, l_ref, acc_ref, rep, tq, dh)

    @pl.when(kj * kb <= qi * tq + tq - 1)
    def _():
        sel = sel_ref[0, 0]
        blk = lax.broadcasted_iota(jnp.int32, sel.shape, 1)
        kpos = kj * kb + lax.broadcasted_iota(jnp.int32, (tq, kb), 1)
        t = qi * tq + lax.broadcasted_iota(jnp.int32, (tq, kb), 0)
        chosen = jnp.zeros((tq, kb), F32)
        for i in range(kb // NSA_SEL_BLOCK):
            col = jnp.sum(jnp.where(blk == kj * (kb // NSA_SEL_BLOCK) + i, sel, 0.0), axis=-1, keepdims=True)
            in_blk = (kpos - kj * kb) // NSA_SEL_BLOCK == i
            chosen = jnp.where(in_blk, col, chosen)
        mask = (chosen > 0) & (kpos <= t)
        _flash_step(q_scr, k_ref[0], v_ref[0].astype(BF16), mask, m_ref, l_ref, acc_ref, rep, tq)

    @pl.when(kj == pl.num_programs(3) - 1)
    def _():
        out = _flash_result(l_ref, acc_ref)
        for r in range(rep):
            o_ref[0, :, r * dh:(r + 1) * dh] = out[r * tq:(r + 1) * tq, :]


def _nsa_window_kernel(q_ref, k_ref, v_ref, oc_ref, os_ref, g_ref, o_ref, q_scr, m_ref, l_ref, acc_ref,
                       *, tq, kb, rep, dh, window, n_steps):
    qi = pl.program_id(2)
    w = pl.program_id(3)
    kblk = qi * (tq // kb) - (n_steps - tq // kb) + w

    @pl.when(w == 0)
    def _():
        _flash_init(q_ref, q_scr, m_ref, l_ref, acc_ref, rep, tq, dh)

    @pl.when(kblk >= 0)
    def _():
        kpos = kblk * kb + lax.broadcasted_iota(jnp.int32, (tq, kb), 1)
        t = qi * tq + lax.broadcasted_iota(jnp.int32, (tq, kb), 0)
        mask = (kpos <= t) & (kpos > t - window)
        _flash_step(q_scr, k_ref[0], v_ref[0].astype(BF16), mask, m_ref, l_ref, acc_ref, rep, tq)

    @pl.when(w == n_steps - 1)
    def _():
        out = _flash_result(l_ref, acc_ref)
        gates = g_ref[0, 0]
        for r in range(rep):
            sl = slice(r * dh, (r + 1) * dh)
            o = (gates[:, 3 * r:3 * r + 1] * oc_ref[0, :, sl] + gates[:, 3 * r + 1:3 * r + 2] * os_ref[0, :, sl]
                 + gates[:, 3 * r + 2:3 * r + 3] * out[r * tq:(r + 1) * tq, :])
            o_ref[0, :, sl] = o.astype(o_ref.dtype)


def nsa_mixer(u, h, w, bsz, s_len):
    t, d = u.shape
    dh, groups = NSA_HEAD_DIM, NSA_N_KV
    n_heads = d // dh
    rep = n_heads // groups
    kvw = groups * dh
    qw = n_heads * dh
    main_w = qw + 6 * kvw
    scale = dh ** -0.5
    tq = kb = min(128, s_len)
    nq = s_len // tq
    n_sel = s_len // NSA_SEL_BLOCK
    topn = min(NSA_TOPK, n_sel)
    w_in = w["nsa_w_in"]
    proj = matmul_ws(u, [(w_in, 0)], main_w, name="nsa_in").reshape(bsz, s_len, main_w)
    gates = matmul(u, [(w_in[:, main_w:].astype(BF16), 0)], w_in.shape[1] - main_w, epilogue=_ep_sigmoid,
                   name="nsa_gates")
    gates = jnp.transpose(gates.reshape(bsz, s_len, groups, rep * 3), (0, 2, 1, 3))
    slot = lambda j: (qw + j * kvw) // dh
    roped = nsa_rope(proj, n_heads, [slot(2) + g for g in range(groups)] + [slot(4) + g for g in range(groups)],
                     dh, scale)
    kc = nsa_compress(proj[..., qw:qw + kvw], w["nsa_cmp_pos_k"], w["nsa_cmp_k_w1"], w["nsa_cmp_k_w2"],
                      bsz, s_len, groups, dh, True)
    vc = nsa_compress(proj[..., qw + kvw:qw + 2 * kvw], w["nsa_cmp_pos_v"], w["nsa_cmp_v_w1"], w["nsa_cmp_v_w2"],
                      bsz, s_len, groups, dh, False)
    nc = kc.shape[1]
    cs = jnp.arange(nc)[:, None] * NSA_CMP_STRIDE
    ss = jnp.arange(n_sel)[None, :] * NSA_SEL_BLOCK
    overlap = jnp.clip(jnp.minimum(cs + NSA_CMP_BLOCK, ss + NSA_SEL_BLOCK) - jnp.maximum(cs, ss), 0, None)
    overlap = overlap.astype(F32) / NSA_CMP_BLOCK

    q_spec3 = pl.BlockSpec((1, tq, rep * dh), lambda b_, g, i: (b_, i, g))
    o_c, sel = pl.pallas_call(
        functools.partial(_nsa_cmp_select_kernel, tq=tq, rep=rep, dh=dh, topn=topn),
        grid=(bsz, groups, nq),
        in_specs=[q_spec3,
                  pl.BlockSpec((1, nc, dh), lambda b_, g, i: (b_ * groups + g, 0, 0)),
                  pl.BlockSpec((1, nc, dh), lambda b_, g, i: (b_ * groups + g, 0, 0)),
                  pl.BlockSpec((nc, n_sel), lambda b_, g, i: (0, 0))],
        out_specs=[q_spec3, pl.BlockSpec((1, 1, tq, n_sel), lambda b_, g, i: (b_, g, i, 0))],
        out_shape=[jax.ShapeDtypeStruct((bsz, s_len, qw), F32),
                   jax.ShapeDtypeStruct((bsz, groups, s_len, n_sel), F32)],
        compiler_params=_params("parallel", "parallel", "parallel"),
        name="nsa_cmp_select",
    )(roped, kc, vc, overlap)

    q_spec = pl.BlockSpec((1, tq, rep * dh), lambda b_, g, i, j: (b_, i, g))
    flash_scratch = [pltpu.VMEM((rep * tq, dh), BF16), pltpu.VMEM((rep * tq, LANES), F32),
                     pltpu.VMEM((rep * tq, LANES), F32), pltpu.VMEM((rep * tq, dh), F32)]
    last_kb = lambda i: (i * tq + tq - 1) // kb
    o_s = pl.pallas_call(
        functools.partial(_nsa_select_kernel, tq=tq, kb=kb, rep=rep, dh=dh),
        grid=(bsz, groups, nq, s_len // kb),
        in_specs=[q_spec,
                  pl.BlockSpec((1, kb, dh), lambda b_, g, i, j: (b_, jnp.minimum(j, last_kb(i)), n_heads + g)),
                  pl.BlockSpec((1, kb, dh), lambda b_, g, i, j: (b_, jnp.minimum(j, last_kb(i)), slot(3) + g)),
                  pl.BlockSpec((1, 1, tq, n_sel), lambda b_, g, i, j: (b_, g, i, 0))],
        out_specs=q_spec,
        out_shape=jax.ShapeDtypeStruct((bsz, s_len, qw), F32),
        scratch_shapes=flash_scratch,
        compiler_params=_params("parallel", "parallel", "parallel", "arbitrary"),
        name="nsa_select_attn",
    )(roped, roped, proj, sel)

    n_steps = NSA_WINDOW // kb + tq // kb
    win_blk = lambda i, j: jnp.maximum(i * (tq // kb) - (n_steps - tq // kb) + j, 0)
    o = pl.pallas_call(
        functools.partial(_nsa_window_kernel, tq=tq, kb=kb, rep=rep, dh=dh, window=NSA_WINDOW, n_steps=n_steps),
        grid=(bsz, groups, nq, n_steps),
        in_specs=[q_spec,
                  pl.BlockSpec((1, kb, dh), lambda b_, g, i, j: (b_, win_blk(i, j), n_heads + groups + g)),
                  pl.BlockSpec((1, kb, dh), lambda b_, g, i, j: (b_, win_blk(i, j), slot(5) + g)),
                  q_spec, q_spec,
                  pl.BlockSpec((1, 1, tq, rep * 3), lambda b_, g, i, j: (b_, g, i, 0))],
        out_specs=q_spec,
        out_shape=jax.ShapeDtypeStruct((bsz, s_len, qw), BF16),
        scratch_shapes=flash_scratch,
        compiler_params=_params("parallel", "parallel", "parallel", "arbitrary"),
        name="nsa_window_attn",
    )(roped, roped, proj, o_c, o_s, gates)
    return matmul_ws(o.reshape(t, qw), [(w["nsa_w_out"], 0)], d, epilogue=_ep_residual, extras=[(h, "mn")],
                     name="nsa_out")


def _rope_t_kernel(x_ref, cc_ref, ss_ref, o_ref, *, n_rope, scale, group, dh):
    first_slot = pl.program_id(2) * group
    for i in range(group):
        x = x_ref[0, :, i * dh:(i + 1) * dh]
        roped = (x * cc_ref[...] + pltpu.roll(x, dh // 2, 1) * ss_ref[...]) * scale
        out = jnp.where(first_slot + i < n_rope, roped, x)
        o_ref[0, i * dh:(i + 1) * dh, :] = out.T.astype(o_ref.dtype)


def nsa_rope_t(proj, slots, n_rope, dh, scale, tb=512, group=4):
    bsz, s_len, _ = proj.shape
    tb = min(tb, s_len)
    cc, ss = _rope_tables(jnp.arange(s_len), dh)
    assert len(slots) % group == 0
    firsts = slots[::group]
    assert all(f % group == 0 and slots[i * group:(i + 1) * group] == list(range(f, f + group))
               for i, f in enumerate(firsts))
    table = jnp.asarray([f // group for f in firsts], jnp.int32)
    grid_spec = pltpu.PrefetchScalarGridSpec(
        num_scalar_prefetch=1,
        grid=(bsz, s_len // tb, len(firsts)),
        in_specs=[pl.BlockSpec((1, tb, group * dh), lambda b_, t, j, tab: (b_, t, tab[j])),
                  pl.BlockSpec((tb, dh), lambda b_, t, j, tab: (t, 0)),
                  pl.BlockSpec((tb, dh), lambda b_, t, j, tab: (t, 0))],
        out_specs=pl.BlockSpec((1, group * dh, tb), lambda b_, t, j, tab: (b_, j, t)),
    )
    kern = lambda tab, x_ref, cc_ref, ss_ref, o_ref: _rope_t_kernel(x_ref, cc_ref, ss_ref, o_ref, n_rope=n_rope,
                                                                   scale=scale, group=group, dh=dh)
    return pl.pallas_call(
        kern,
        grid_spec=grid_spec,
        out_shape=jax.ShapeDtypeStruct((bsz, len(slots) * dh, s_len), BF16),
        compiler_params=_params("parallel", "parallel", "arbitrary"),
        name="nsa_rope_t",
    )(table, proj, cc, ss)


def _cmp_finish_t_kernel(z_ref, bias_ref, w2_ref, o_ref, *, hidden):
    z = z_ref[0]
    nc = z.shape[0]
    nxt = pltpu.roll(z[:, hidden:], nc - 1, 0)
    hid = _silu(z[:, :hidden] + nxt + bias_ref[...])
    o_ref[0] = _dot_nt(w2_ref[...], hid.astype(BF16)).astype(o_ref.dtype)


def _nsa_cmp_select_t_kernel(q_ref, kc_ref, vc_ref, ov_ref, oc_ref, sel_ref, *, tq, rep, dh, topn):
    qi = pl.program_id(2)
    kc = kc_ref[0]
    vct = vc_ref[0]
    nc = kc.shape[0]
    n_sel = sel_ref.shape[2]
    t = qi * tq + lax.broadcasted_iota(jnp.int32, (nc, tq), 1)
    cmp_end = lax.broadcasted_iota(jnp.int32, (nc, tq), 0) * NSA_CMP_STRIDE + (NSA_CMP_BLOCK - 1)
    visible = cmp_end <= t
    s = [jnp.where(visible, jnp.dot(kc, q_ref[0, r * dh:(r + 1) * dh, :], preferred_element_type=F32), NEG_BIG)
         for r in range(rep)]
    e = [jnp.where(visible, jnp.exp2(x - jnp.max(x, axis=0, keepdims=True)), 0.0) for x in s]
    den = [jnp.sum(x, axis=0, keepdims=True) for x in e]
    p = [e[r] / jnp.where(den[r] > 0, den[r], 1.0) for r in range(rep)]
    for r in range(rep):
        oc_ref[0, r * dh:(r + 1) * dh, :] = jnp.dot(vct, p[r].astype(BF16), preferred_element_type=F32)
    psum = p[0]
    for r in range(1, rep):
        psum = psum + p[r]
    imp = _dot_hi(ov_ref[...], psum)
    blk = lax.broadcasted_iota(jnp.int32, (n_sel, tq), 0)
    cur = (qi * tq + lax.broadcasted_iota(jnp.int32, (n_sel, tq), 1)) // NSA_SEL_BLOCK
    forced = (blk == 0) | (blk == cur) | (blk == cur - 1)
    imp = jnp.where(forced, NSA_FORCED_SCORE, imp)
    imp = jnp.where(blk > cur, -jnp.inf, imp)
    sel = jnp.zeros((n_sel, tq), F32)
    for _ in range(topn):
        m = jnp.max(imp, axis=0, keepdims=True)
        first = jnp.min(jnp.where(imp == m, blk, n_sel), axis=0, keepdims=True)
        hit = blk == first
        sel = jnp.where(hit, 1.0, sel)
        imp = jnp.where(hit, -jnp.inf, imp)
    sel_ref[0, 0] = sel


def _flash_t_init(m_ref, l_ref, acc_ref):
    m_ref[...] = jnp.full_like(m_ref, NEG_BIG)
    l_ref[...] = jnp.zeros_like(l_ref)
    acc_ref[...] = jnp.zeros_like(acc_ref)


def _flash_t_step(q_ref, k, vt, mask, m_ref, l_ref, acc_ref, rep, dh):
    hs = range(rep)
    s = [jnp.where(mask, jnp.dot(k, q_ref[0, r * dh:(r + 1) * dh, :], preferred_element_type=F32), NEG_BIG)
         for r in hs]
    m_old = [m_ref[r] for r in hs]
    m_new = [jnp.maximum(m_old[r], jnp.max(s[r], axis=0, keepdims=True)) for r in hs]
    p = [jnp.exp2(s[r] - m_new[r]).astype(BF16) for r in hs]
    alpha = [jnp.exp2(m_old[r] - m_new[r]) for r in hs]
    pv = [jnp.dot(vt, p[r], preferred_element_type=F32) for r in hs]
    ones = jnp.ones((8, k.shape[0]), BF16)
    psum = [jnp.dot(ones, p[r], preferred_element_type=F32)[0:1] for r in hs]
    for r in hs:
        m_ref[r] = m_new[r]
        l_ref[r] = alpha[r] * l_ref[r] + psum[r]
        acc_ref[r] = acc_ref[r] * alpha[r] + pv[r]


def _nsa_select_t_kernel(qi_ref, kj_ref, q_ref, k_ref, vt_ref, sel_ref, o_ref, m_ref, l_ref, acc_ref,
                         *, tq, kb, rep, dh):
    pair = pl.program_id(2)
    qi = qi_ref[pair]
    kj = kj_ref[pair]

    @pl.when(kj == 0)
    def _():
        _flash_t_init(m_ref, l_ref, acc_ref)

    kpos = kj * kb + lax.broadcasted_iota(jnp.int32, (kb, tq), 0)
    t = qi * tq + lax.broadcasted_iota(jnp.int32, (kb, tq), 1)
    per = kb // NSA_SEL_BLOCK
    chosen = jnp.zeros((kb, tq), F32)
    for i in range(per):
        row = sel_ref[0, 0, pl.ds(kj * per + i, 1), :]
        chosen = jnp.where((kpos - kj * kb) // NSA_SEL_BLOCK == i, row, chosen)
    mask = (chosen > 0) & (kpos <= t)
    _flash_t_step(q_ref, k_ref[0], vt_ref[0], mask, m_ref, l_ref, acc_ref, rep, dh)

    @pl.when(kj * kb + kb > qi * tq + tq - 1)
    def _():
        for r in range(rep):
            l = l_ref[r]
            o_ref[0, r * dh:(r + 1) * dh, :] = acc_ref[r] / jnp.where(l > 0, l, 1.0)


def _nsa_window_t_kernel(q_ref, k_ref, vt_ref, oc_ref, os_ref, g_ref, o_ref, m_ref, l_ref, acc_ref,
                         *, tq, kb, rep, dh, window, n_steps):
    qi = pl.program_id(2)
    w = pl.program_id(3)
    kblk = qi * (tq // kb) - (n_steps - tq // kb) + w

    @pl.when(w == 0)
    def _():
        _flash_t_init(m_ref, l_ref, acc_ref)

    @pl.when(kblk >= 0)
    def _():
        kpos = kblk * kb + lax.broadcasted_iota(jnp.int32, (kb, tq), 0)
        t = qi * tq + lax.broadcasted_iota(jnp.int32, (kb, tq), 1)
        mask = (kpos <= t) & (kpos > t - window)
        _flash_t_step(q_ref, k_ref[0], vt_ref[0], mask, m_ref, l_ref, acc_ref, rep, dh)

    @pl.when(w == n_steps - 1)
    def _():
        gates = g_ref[0, 0]
        for r in range(rep):
            rows = slice(r * dh, (r + 1) * dh)
            l = l_ref[r]
            o_w = acc_ref[r] / jnp.where(l > 0, l, 1.0)
            o = (gates[3 * r:3 * r + 1, :] * oc_ref[0, rows, :] + gates[3 * r + 1:3 * r + 2, :] * os_ref[0, rows, :]
                 + gates[3 * r + 2:3 * r + 3, :] * o_w)
            o_ref[0, :, rows] = o.T.astype(o_ref.dtype)


def nsa_mixer_t(u, h, w, bsz, s_len):
    t, d = u.shape
    dh, groups = NSA_HEAD_DIM, NSA_N_KV
    n_heads = d // dh
    rep = n_heads // groups
    kvw = groups * dh
    qw = n_heads * dh
    main_w = qw + 6 * kvw
    scale = dh ** -0.5
    tq = kb = min(128, s_len)
    nq = s_len // tq
    n_sel = s_len // NSA_SEL_BLOCK
    topn = min(NSA_TOPK, n_sel)
    w_in = w["nsa_w_in"]
    proj = matmul_ws(u, [(w_in, 0)], main_w, name="nsa_in").reshape(bsz, s_len, main_w)
    gates = matmul(u, [(w_in[:, main_w:].astype(BF16), 0)], w_in.shape[1] - main_w, epilogue=_ep_sigmoid,
                   name="nsa_gates")
    gates = jnp.transpose(gates.reshape(bsz, s_len, groups, rep * 3), (0, 2, 3, 1))
    slot = lambda j: (qw + j * kvw) // dh
    qvt = nsa_rope_t(proj, list(range(n_heads)) + [slot(3) + g for g in range(groups)]
                     + [slot(5) + g for g in range(groups)], n_heads, dh, scale * math.log2(math.e), group=groups)
    k_rot = nsa_rope(proj, 0, [slot(2) + g for g in range(groups)] + [slot(4) + g for g in range(groups)], dh, 1.0)
    kc = nsa_compress(proj[..., qw:qw + kvw], w["nsa_cmp_pos_k"], w["nsa_cmp_k_w1"], w["nsa_cmp_k_w2"],
                      bsz, s_len, groups, dh, True)
    vct = nsa_compress(proj[..., qw + kvw:qw + 2 * kvw], w["nsa_cmp_pos_v"], w["nsa_cmp_v_w1"], w["nsa_cmp_v_w2"],
                       bsz, s_len, groups, dh, False, transpose_out=True)
    nc = kc.shape[1]
    cs = jnp.arange(nc)[None, :] * NSA_CMP_STRIDE
    ss = jnp.arange(n_sel)[:, None] * NSA_SEL_BLOCK
    overlap_t = jnp.clip(jnp.minimum(cs + NSA_CMP_BLOCK, ss + NSA_SEL_BLOCK) - jnp.maximum(cs, ss), 0, None)
    overlap_t = overlap_t.astype(F32) / NSA_CMP_BLOCK

    qt_spec3 = pl.BlockSpec((1, rep * dh, tq), lambda b_, g, i: (b_, g, i))
    o_c, sel = pl.pallas_call(
        functools.partial(_nsa_cmp_select_t_kernel, tq=tq, rep=rep, dh=dh, topn=topn),
        grid=(bsz, groups, nq),
        in_specs=[qt_spec3,
                  pl.BlockSpec((1, nc, dh), lambda b_, g, i: (b_ * groups + g, 0, 0)),
                  pl.BlockSpec((1, dh, nc), lambda b_, g, i: (b_ * groups + g, 0, 0)),
                  pl.BlockSpec((n_sel, nc), lambda b_, g, i: (0, 0))],
        out_specs=[qt_spec3, pl.BlockSpec((1, 1, n_sel, tq), lambda b_, g, i: (b_, g, 0, i))],
        out_shape=[jax.ShapeDtypeStruct((bsz, qw, s_len), F32),
                   jax.ShapeDtypeStruct((bsz, groups, n_sel, s_len), F32)],
        compiler_params=_params("parallel", "parallel", "parallel"),
        name="nsa_cmp_select",
    )(qvt, kc, vct, overlap_t)

    flash_scratch = lambda n: [pltpu.VMEM((rep, 1, n), F32), pltpu.VMEM((rep, 1, n), F32),
                               pltpu.VMEM((rep, dh, n), F32)]
    tqs = min(2 * tq, s_len)
    kb = tqs
    pairs = [(i, j) for i in range(s_len // tqs) for j in range((i * tqs + tqs - 1) // kb + 1)]
    qi_of = jnp.asarray([pr[0] for pr in pairs], jnp.int32)
    kj_of = jnp.asarray([pr[1] for pr in pairs], jnp.int32)
    o_s = pl.pallas_call(
        functools.partial(_nsa_select_t_kernel, tq=tqs, kb=kb, rep=rep, dh=dh),
        grid_spec=pltpu.PrefetchScalarGridSpec(
            num_scalar_prefetch=2,
            grid=(bsz, groups, len(pairs)),
            in_specs=[pl.BlockSpec((1, rep * dh, tqs), lambda b_, g, pr, qi, kj: (b_, g, qi[pr])),
                      pl.BlockSpec((1, kb, dh), lambda b_, g, pr, qi, kj: (b_, kj[pr], g)),
                      pl.BlockSpec((1, dh, kb), lambda b_, g, pr, qi, kj: (b_, n_heads + g, kj[pr])),
                      pl.BlockSpec((1, 1, n_sel, tqs), lambda b_, g, pr, qi, kj: (b_, g, 0, qi[pr]))],
            out_specs=pl.BlockSpec((1, rep * dh, tqs), lambda b_, g, pr, qi, kj: (b_, g, qi[pr])),
            scratch_shapes=flash_scratch(tqs)),
        out_shape=jax.ShapeDtypeStruct((bsz, qw, s_len), F32),
        compiler_params=_params("parallel", "parallel", "arbitrary"),
        name="nsa_select_attn",
    )(qi_of, kj_of, qvt, k_rot, qvt, sel)

    n_steps = -(-NSA_WINDOW // kb) + tqs // kb
    win_blk = lambda i, j: jnp.maximum(i * (tqs // kb) - (n_steps - tqs // kb) + j, 0)
    qt_spec = pl.BlockSpec((1, rep * dh, tqs), lambda b_, g, i, j: (b_, g, i))
    o = pl.pallas_call(
        functools.partial(_nsa_window_t_kernel, tq=tqs, kb=kb, rep=rep, dh=dh, window=NSA_WINDOW, n_steps=n_steps),
        grid=(bsz, groups, s_len // tqs, n_steps),
        in_specs=[qt_spec,
                  pl.BlockSpec((1, kb, dh), lambda b_, g, i, j: (b_, win_blk(i, j), groups + g)),
                  pl.BlockSpec((1, dh, kb), lambda b_, g, i, j: (b_, n_heads + groups + g, win_blk(i, j))),
                  qt_spec, qt_spec,
                  pl.BlockSpec((1, 1, rep * 3, tqs), lambda b_, g, i, j: (b_, g, 0, i))],
        out_specs=pl.BlockSpec((1, tqs, rep * dh), lambda b_, g, i, j: (b_, i, g)),
        out_shape=jax.ShapeDtypeStruct((bsz, s_len, qw), BF16),
        scratch_shapes=flash_scratch(tqs),
        compiler_params=_params("parallel", "parallel", "parallel", "arbitrary"),
        name="nsa_window_attn",
    )(qvt, k_rot, qvt, o_c, o_s, gates)
    return matmul_ws(o.reshape(t, qw), [(w["nsa_w_out"], 0)], d, epilogue=_ep_residual, extras=[(h, "mn")],
                     name="nsa_out")


_MATMUL_WEIGHTS = ("pl_proj", "rw_w1", "rw_w2", "rw_a1", "rw_a2", "rw_g1", "rw_g2")


def kernel(x, p, norm_mix, norm_ffn, norm_pl, pl_proj, pl_gate, norm_final, mb_w_in, mb_conv_w, mb_conv_b, mb_dt_bias, mb_a_log, mb_d_skip, mb_norm_w, mb_w_out, nsa_w_in, nsa_cmp_pos_k, nsa_cmp_pos_v, nsa_cmp_k_w1, nsa_cmp_k_w2, nsa_cmp_v_w1, nsa_cmp_v_w2, nsa_w_out, hg_w_in, hg_lb_logits, hg_norm_w, hg_w_out, rw_mu, rw_w_rkv, rw_w0, rw_w1, rw_w2, rw_a0, rw_a1, rw_a2, rw_g1, rw_g2, rw_k_k, rw_k_a, rw_r_k, rw_ln_w, rw_ln_b, rw_w_out, ffn0_w_in, ffn0_w_out, moe1_router, moe1_w_in, moe1_w_out, ffn2_w_in, ffn2_w_out, moe3_router, moe3_w_in, moe3_w_out):
    w = dict(locals())
    for name in _MATMUL_WEIGHTS:
        w[name] = w[name].astype(BF16)
    bsz, s_len, d = x.shape
    depth = p.shape[0]
    t = bsz * s_len
    lb_all = jax.nn.softmax(hg_lb_logits.astype(F32), axis=0)
    lb_all = jnp.cumsum(lb_all, axis=0) - lb_all[0]
    dense = [(w["ffn0_w_in"], w["ffn0_w_out"]), (w["ffn2_w_in"], w["ffn2_w_out"])]
    moe = [(moe1_router, w["moe1_w_in"], w["moe1_w_out"]), (moe3_router, w["moe3_w_in"], w["moe3_w_out"])]
    p_bf = p.reshape(depth, t, p.shape[-1])
    h = x.reshape(t, d)
    for i in range(depth):
        kind = i % 4
        if kind == 0:
            h = mamba2_mixer(rmsnorm(h, norm_mix[i]), h, w, bsz, s_len)
        elif kind == 1:
            h = nsa_mixer_t(rmsnorm(h, norm_mix[i]), h, w, bsz, s_len)
        elif kind == 2:
            h = hgrn2_mixer(rmsnorm(h, norm_mix[i]), h, w, lb_all[i], bsz, s_len)
        else:
            h = rwkv7_mixer(rmsnorm(h, norm_mix[i], out_dtype=F32), h, w, bsz, s_len)
        v = rmsnorm(h, norm_ffn[i])
        if i % 2 == 0:
            h = dense_ffn(v, h, *dense[i // 2])
        else:
            h = moe_ffn_routed(v, h, *moe[i // 2])
        h = ple_gate(h, p_bf[i], norm_pl[i], w["pl_proj"][i], pl_gate, i)
    return rmsnorm(h, norm_final, out_dtype=F32).reshape(bsz, s_len, d)
```

```python
import functools
import math

import jax
import jax.numpy as jnp
from jax import lax
from jax.experimental import pallas as pl
from jax.experimental.pallas import tpu as pltpu

F32 = jnp.float32
BF16 = jnp.bfloat16

NORM_EPS = 1e-6
ROPE_THETA = 10000.0

V7X_VMEM_BYTES = 64 * 1024 * 1024
VMEM_LIMIT_BYTES = V7X_VMEM_BYTES - 8 * 1024 * 1024
LANES = 128

MB_HEAD_DIM = 64
MB_N_GROUPS = 8
MB_D_STATE = 128
MB_CONV = 4
MB_CHUNK = 128

NSA_HEAD_DIM = 128
NSA_N_KV = 4
NSA_CMP_BLOCK = 32
NSA_CMP_STRIDE = 16
NSA_SEL_BLOCK = 64
NSA_TOPK = 16
NSA_WINDOW = 512
NSA_FORCED_SCORE = 1e9

HG_HEAD_DIM = 128
HG_CHUNK = 32

RW_HEAD_DIM = 64
RW_LN_EPS = 64e-5
RW_CHUNK = 128

MOE_TOPK = 2


def _params(*semantics):
    return pltpu.CompilerParams(dimension_semantics=semantics, vmem_limit_bytes=VMEM_LIMIT_BYTES)


def _pick(n, target):
    if n <= target:
        return n
    for c in range(target, 0, -1):
        if n % c == 0:
            return c
    return n


def _silu(x):
    return x * jax.nn.sigmoid(x)


def _rmsnorm_kernel(x_ref, g_ref, o_ref):
    x = x_ref[...]
    ms = jnp.mean(x * x, axis=-1, keepdims=True)
    o_ref[...] = (x * lax.rsqrt(ms + NORM_EPS) * g_ref[...]).astype(o_ref.dtype)


def rmsnorm(x, gain, out_dtype=BF16, name="rmsnorm"):
    m, d = x.shape
    bm = _pick(m, 256)
    return pl.pallas_call(
        _rmsnorm_kernel,
        grid=(m // bm,),
        in_specs=[pl.BlockSpec((bm, d), lambda i: (i, 0)), pl.BlockSpec((1, d), lambda i: (0, 0))],
        out_specs=pl.BlockSpec((bm, d), lambda i: (i, 0)),
        out_shape=jax.ShapeDtypeStruct((m, d), out_dtype),
        compiler_params=_params("parallel"),
        name=name,
    )(x, gain.reshape(1, d).astype(F32))


def _mm_kernel(*refs, n_w, n_extra, nk, epilogue):
    x_ref = refs[0]
    w_refs = refs[1:1 + n_w]
    e_refs = refs[1 + n_w:1 + n_w + n_extra]
    o_ref = refs[1 + n_w + n_extra]
    acc_refs = refs[2 + n_w + n_extra:]
    x = x_ref[...]
    if nk == 1:
        accs = [jnp.dot(x, w[...], preferred_element_type=F32) for w in w_refs]
        o_ref[...] = epilogue(accs, [e[...] for e in e_refs]).astype(o_ref.dtype)
        return
    k = pl.program_id(2)

    @pl.when(k == 0)
    def _():
        for a in acc_refs:
            a[...] = jnp.zeros_like(a)

    for a, w in zip(acc_refs, w_refs):
        a[...] += jnp.dot(x, w[...], preferred_element_type=F32)

    @pl.when(k == nk - 1)
    def _():
        o_ref[...] = epilogue([a[...] for a in acc_refs], [e[...] for e in e_refs]).astype(o_ref.dtype)


def _first(accs, extras):
    return accs[0]


def matmul(x, ws, n_out, *, epilogue=_first, extras=(), out_dtype=F32, bm=1024, bn=512, bk=None, name="matmul"):
    m, kdim = x.shape
    bm = _pick(m, bm)
    bn = _pick(n_out, bn)
    if bk is None:
        bk = kdim if kdim <= 4096 else _pick(kdim, 4096)
    nk = kdim // bk
    assert kdim % bk == 0 and m % bm == 0 and n_out % bn == 0
    in_specs = [pl.BlockSpec((bm, bk), lambda i, j, k: (i, k))]
    args = [x]
    for w, off in ws:
        assert off % bn == 0 and w.shape[0] == kdim
        in_specs.append(pl.BlockSpec((bk, bn), functools.partial(lambda i, j, k, o: (k, j + o), o=off // bn)))
        args.append(w)
    for arr, kind in extras:
        if kind == "mn":
            in_specs.append(pl.BlockSpec((bm, bn), lambda i, j, k: (i, j)))
        elif kind == "m":
            in_specs.append(pl.BlockSpec((bm, arr.shape[1]), lambda i, j, k: (i, 0)))
        elif kind == "kn":
            in_specs.append(pl.BlockSpec((arr.shape[0], bn), lambda i, j, k: (0, j)))
        else:
            in_specs.append(pl.BlockSpec((1, bn), lambda i, j, k: (0, j)))
        args.append(arr)
    scratch = [pltpu.VMEM((bm, bn), F32) for _ in ws] if nk > 1 else []
    kern = functools.partial(_mm_kernel, n_w=len(ws), n_extra=len(extras), nk=nk, epilogue=epilogue)
    return pl.pallas_call(
        kern,
        grid=(m // bm, n_out // bn, nk),
        in_specs=in_specs,
        out_specs=pl.BlockSpec((bm, bn), lambda i, j, k: (i, j)),
        out_shape=jax.ShapeDtypeStruct((m, n_out), out_dtype),
        scratch_shapes=scratch,
        compiler_params=_params("parallel", "parallel", "arbitrary"),
        name=name,
    )(*args)


WS_CAST_CHUNK = 512


def _mm_ws_kernel(*refs, n_w, n_extra, epilogue):
    x_ref = refs[0]
    w_refs = refs[1:1 + n_w]
    e_refs = refs[1 + n_w:1 + n_w + n_extra]
    o_ref = refs[1 + n_w + n_extra]
    wb_refs = refs[2 + n_w + n_extra:]

    kdim = x_ref.shape[1]
    ck = _pick(kdim, WS_CAST_CHUNK)

    @pl.when(pl.program_id(1) == 0)
    def _():
        accs = [None] * n_w
        for c in range(kdim // ck):
            rows = slice(c * ck, (c + 1) * ck)
            xc = x_ref[:, rows]
            for n, (w, wb) in enumerate(zip(w_refs, wb_refs)):
                wc = (w[0, rows, :] if len(w.shape) == 3 else w[rows, :]).astype(BF16)
                wb[rows, :] = wc
                part = jnp.dot(xc, wc, preferred_element_type=F32)
                accs[n] = part if accs[n] is None else accs[n] + part
        o_ref[...] = epilogue(accs, [e[...] for e in e_refs]).astype(o_ref.dtype)

    @pl.when(pl.program_id(1) != 0)
    def _():
        x = x_ref[...]
        accs = [jnp.dot(x, wb[...], preferred_element_type=F32) for wb in wb_refs]
        o_ref[...] = epilogue(accs, [e[...] for e in e_refs]).astype(o_ref.dtype)


def matmul_ws(x, ws, n_out, *, epilogue=_first, extras=(), out_dtype=F32, bm=1024, bn=512, w_buffers=2,
              name="matmul_ws"):
    m, kdim = x.shape
    bm = _pick(m, bm)
    bn = _pick(n_out, bn)
    assert m % bm == 0 and n_out % bn == 0
    mode = {} if w_buffers == 2 else {"pipeline_mode": pl.Buffered(w_buffers)}
    in_specs = [pl.BlockSpec((bm, kdim), lambda j, i: (i, 0))]
    args = [x]
    for w, off in ws:
        if w.ndim == 3:
            e, o = off
            assert o % bn == 0 and w.shape[1] == kdim
            in_specs.append(pl.BlockSpec((1, kdim, bn), functools.partial(lambda j, i, e_, o_: (e_, 0, j + o_),
                                                                          e_=e, o_=o // bn), **mode))
        else:
            assert off % bn == 0 and w.shape[0] == kdim
            in_specs.append(pl.BlockSpec((kdim, bn), functools.partial(lambda j, i, o_: (0, j + o_), o_=off // bn),
                                         **mode))
        args.append(w)
    for arr, kind in extras:
        if kind == "mn":
            in_specs.append(pl.BlockSpec((bm, bn), lambda j, i: (i, j)))
        elif kind == "m":
            in_specs.append(pl.BlockSpec((bm, arr.shape[1]), lambda j, i: (i, 0)))
        elif kind == "kn":
            in_specs.append(pl.BlockSpec((arr.shape[0], bn), lambda j, i: (0, j)))
        else:
            in_specs.append(pl.BlockSpec((1, bn), lambda j, i: (0, j)))
        args.append(arr)
    kern = functools.partial(_mm_ws_kernel, n_w=len(ws), n_extra=len(extras), epilogue=epilogue)
    return pl.pallas_call(
        kern,
        grid=(n_out // bn, m // bm),
        in_specs=in_specs,
        out_specs=pl.BlockSpec((bm, bn), lambda j, i: (i, j)),
        out_shape=jax.ShapeDtypeStruct((m, n_out), out_dtype),
        scratch_shapes=[pltpu.VMEM((kdim, bn), BF16) for _ in ws],
        compiler_params=_params("parallel", "arbitrary"),
        name=name,
    )(*args)


def _ep_residual(accs, extras):
    return extras[0] + accs[0]


def _ep_swiglu(accs, extras):
    return _silu(accs[0]) * accs[1]


def _ep_bias(accs, extras):
    return accs[0] + extras[0]


def _ep_tanh(accs, extras):
    return jnp.tanh(accs[0])


def _ep_sigmoid(accs, extras):
    return jax.nn.sigmoid(accs[0])


def _ep_bias_sigmoid(accs, extras):
    return jax.nn.sigmoid(accs[0] + extras[0])


def _ep_rw_logdecay(accs, extras):
    w = -jax.nn.softplus(-(accs[0] + extras[0])) - 0.5
    return -jnp.exp(w)


def _ep_ple_gate(accs, extras):
    pp = jnp.dot(extras[1].astype(BF16), extras[2], preferred_element_type=F32)
    return extras[0] + pp * jax.nn.sigmoid(accs[0])


def _conv_silu_kernel(x_ref, w_ref, b_ref, o_ref, *, k_width):
    x = x_ref[0]
    row = lax.broadcasted_iota(jnp.int32, x.shape, 0)
    y = b_ref[...] + w_ref[k_width - 1:k_width, :] * x
    for j in range(k_width - 1):
        shift = k_width - 1 - j
        xs = jnp.where(row >= shift, pltpu.roll(x, shift, 0), 0.0)
        y = y + w_ref[j:j + 1, :] * xs
    o_ref[0] = _silu(y)


def conv_silu(x, w, b):
    bsz, s_len, c = x.shape
    cb = _pick(c, 256)
    k_width = w.shape[0]
    return pl.pallas_call(
        functools.partial(_conv_silu_kernel, k_width=k_width),
        grid=(bsz, c // cb),
        in_specs=[pl.BlockSpec((1, s_len, cb), lambda b_, j: (b_, 0, j)),
                  pl.BlockSpec((k_width, cb), lambda b_, j: (0, j)),
                  pl.BlockSpec((1, cb), lambda b_, j: (0, j))],
        out_specs=pl.BlockSpec((1, s_len, cb), lambda b_, j: (b_, 0, j)),
        out_shape=jax.ShapeDtypeStruct(x.shape, F32),
        compiler_params=_params("parallel", "parallel"),
        name="mamba_conv_silu",
    )(x, w, b.reshape(1, c))


def _cumsum_rows(x, n):
    row = lax.broadcasted_iota(jnp.int32, x.shape, 0)
    s = 1
    while s < n:
        x = x + jnp.where(row >= s, pltpu.roll(x, s, 0), 0.0)
        s *= 2
    return x


def _cumsum_lanes(x, n):
    col = lax.broadcasted_iota(jnp.int32, x.shape, 1)
    s = 1
    while s < n:
        x = x + jnp.where(col >= s, pltpu.roll(x, s, 1), 0.0)
        s *= 2
    return x


def _dot_nt(a, b):
    return lax.dot_general(a, b, (((1,), (1,)), ((), ())), preferred_element_type=F32)


def _dot_tn(a, b):
    return lax.dot_general(a, b, (((0,), (0,)), ((), ())), preferred_element_type=F32)


def _ssd_kernel(xs_ref, b_ref, c_ref, z_ref, dt_ref, dtt_ref, bias_r_ref, bias_c_ref, alog_r_ref, alog_c_ref,
                dskip_ref, normw_ref, o_ref, state_ref, y_ref, *, chunk, heads, p_dim):
    @pl.when(pl.program_id(2) == 0)
    def _():
        state_ref[...] = jnp.zeros_like(state_ref)

    dt = jax.nn.softplus(dt_ref[0, 0] + bias_r_ref[0])
    dtt = jax.nn.softplus(dtt_ref[0, 0] + bias_c_ref[0])
    a_cum = _cumsum_rows(dt * -jnp.exp(alog_r_ref[0]), chunk)
    a_cum_t = _cumsum_lanes(dtt * -jnp.exp(alog_c_ref[0]), chunk)
    xs = xs_ref[0]
    bmat = b_ref[0]
    cmat = c_ref[0].astype(BF16)
    cb = _dot_nt(cmat, bmat.astype(BF16))
    b_t = bmat.T.astype(BF16)
    li = lax.broadcasted_iota(jnp.int32, (chunk, chunk), 0)
    si = lax.broadcasted_iota(jnp.int32, (chunk, chunk), 1)
    causal = li >= si
    per = LANES // p_dim
    lane_seg = lax.broadcasted_iota(jnp.int32, (1, LANES), 1) // p_dim

    def pick(vals):
        out = vals[-1]
        for i in range(per - 2, -1, -1):
            out = jnp.where(lane_seg == i, vals[i], out)
        return out

    dot = functools.partial(jnp.dot, preferred_element_type=F32)
    es = range(heads)
    tiles = range(heads // per)
    head_row = lax.broadcasted_iota(jnp.int32, (heads, heads * LANES), 0)
    to_tile = jnp.where(lax.broadcasted_iota(jnp.int32, (heads, heads * LANES), 1) // LANES == head_row, 1.0, 0.0)
    cum_t = _dot_hi(a_cum, to_tile)
    of = lambda vals, i: [vals[i * per + j] for j in range(per)]
    tile = lambda x, i: x[:, i * LANES:(i + 1) * LANES]
    cum_c = jnp.concatenate([pick([tile(cum_t, e) for e in of(es, i)]) for i in tiles], axis=-1)
    dt_c = jnp.concatenate([pick([dt[:, e:e + 1] for e in of(es, i)]) for i in tiles], axis=-1)
    last_c = cum_c[chunk - 1:chunk, :]
    m = [(cb * jnp.exp(jnp.where(causal, tile(cum_t, e) - a_cum_t[e:e + 1, :], -jnp.inf))).astype(BF16) for e in es]
    xdt = xs * dt_c
    xdt_b = xdt.astype(BF16)
    xend_b = (xdt * jnp.exp(last_c - cum_c)).astype(BF16)
    grow = jnp.exp(cum_c)
    st_decay = jnp.exp(last_c)
    st = [state_ref[i] for i in tiles]
    y_in = [pick([dot(m[e], tile(xdt_b, i)) for e in of(es, i)]) for i in tiles]
    y_st = [dot(cmat, st[i].astype(BF16)) * tile(grow, i) for i in tiles]
    for i in tiles:
        state_ref[i] = st[i] * tile(st_decay, i) + dot(b_t, tile(xend_b, i))
        y_ref[:, i * LANES:(i + 1) * LANES] = y_in[i] + y_st[i]
    y = y_ref[...] + xs * dskip_ref[...]
    y = y * _silu(z_ref[0])
    ms = jnp.mean(y * y, axis=-1, keepdims=True)
    o_ref[0] = (y * lax.rsqrt(ms + NORM_EPS) * normw_ref[...]).astype(o_ref.dtype)


def ssd_scan(xbc, z, dt, dt_bias, a_log, d_skip, norm_w, *, chunk=MB_CHUNK):
    bsz, s_len, d_inner = z.shape
    n_heads = dt.shape[-1]
    n_state = MB_D_STATE
    groups = (xbc.shape[-1] - d_inner) // (2 * n_state)
    heads = n_heads // groups
    p_dim = d_inner // n_heads
    gw = heads * p_dim
    assert gw % LANES == 0 and d_inner % n_state == 0
    chunk = min(chunk, s_len)
    nc = s_len // chunk
    b_off = d_inner // n_state
    c_off = b_off + groups
    dt_g = jnp.transpose(dt.reshape(bsz, s_len, groups, heads), (0, 2, 1, 3))
    dt_gt = jnp.transpose(dt_g, (0, 1, 3, 2))
    kern = functools.partial(_ssd_kernel, chunk=chunk, heads=heads, p_dim=p_dim)
    per_group = lambda b_, g, c: (g, 0, 0)
    return pl.pallas_call(
        kern,
        grid=(bsz, groups, nc),
        in_specs=[pl.BlockSpec((1, chunk, gw), lambda b_, g, c: (b_, c, g)),
                  pl.BlockSpec((1, chunk, n_state), lambda b_, g, c: (b_, c, b_off + g)),
                  pl.BlockSpec((1, chunk, n_state), lambda b_, g, c: (b_, c, c_off + g)),
                  pl.BlockSpec((1, chunk, gw), lambda b_, g, c: (b_, c, g)),
                  pl.BlockSpec((1, 1, chunk, heads), lambda b_, g, c: (b_, g, c, 0)),
                  pl.BlockSpec((1, 1, heads, chunk), lambda b_, g, c: (b_, g, 0, c)),
                  pl.BlockSpec((1, 1, heads), per_group),
                  pl.BlockSpec((1, heads, 1), per_group),
                  pl.BlockSpec((1, 1, heads), per_group),
                  pl.BlockSpec((1, heads, 1), per_group),
                  pl.BlockSpec((1, gw), lambda b_, g, c: (0, g)),
                  pl.BlockSpec((1, gw), lambda b_, g, c: (0, g))],
        out_specs=pl.BlockSpec((1, chunk, gw), lambda b_, g, c: (b_, c, g)),
        out_shape=jax.ShapeDtypeStruct(z.shape, BF16),
        scratch_shapes=[pltpu.VMEM((gw // LANES, n_state, LANES), F32), pltpu.VMEM((chunk, gw), F32)],
        compiler_params=_params("parallel", "parallel", "arbitrary"),
        name="mamba_ssd",
    )(xbc, xbc, xbc, z, dt_g, dt_gt,
      dt_bias.reshape(groups, 1, heads), dt_bias.reshape(groups, heads, 1),
      a_log.reshape(groups, 1, heads), a_log.reshape(groups, heads, 1),
      jnp.repeat(d_skip, p_dim).reshape(1, d_inner), norm_w.reshape(1, d_inner))


def mamba2_mixer(u, h, w, bsz, s_len):
    d_inner = w["mb_w_out"].shape[0]
    n_heads = w["mb_dt_bias"].shape[0]
    w_in = w["mb_w_in"]
    xbc_w = w_in.shape[1] - d_inner - n_heads
    z = matmul_ws(u, [(w_in, 0)], d_inner, name="mb_in_z")
    xbc = matmul_ws(u, [(w_in, d_inner)], xbc_w, name="mb_in_xbc")
    dt = matmul_ws(u, [(w_in, d_inner + xbc_w)], n_heads, name="mb_in_dt")
    xbc = conv_silu(xbc.reshape(bsz, s_len, xbc_w), w["mb_conv_w"], w["mb_conv_b"])
    y = ssd_scan(xbc, z.reshape(bsz, s_len, d_inner), dt.reshape(bsz, s_len, n_heads),
                 w["mb_dt_bias"], w["mb_a_log"], w["mb_d_skip"], w["mb_norm_w"])
    return matmul_ws(y.reshape(bsz * s_len, d_inner), [(w["mb_w_out"], 0)], h.shape[1],
                     epilogue=_ep_residual, extras=[(h, "mn")], bm=512, w_buffers=1, name="mb_out")


def _seg_cumsum_rows(x, seg, reverse=False):
    n = x.shape[0]
    pos = lax.broadcasted_iota(jnp.int32, x.shape, 0) % seg
    s = 1
    while s < seg:
        if reverse:
            x = x + jnp.where(pos < seg - s, pltpu.roll(x, n - s, 0), 0.0)
        else:
            x = x + jnp.where(pos >= s, pltpu.roll(x, s, 0), 0.0)
        s *= 2
    return x


def _hgrn_kernel(q_ref, f_ref, i_ref, g_ref, lb_ref, nw_ref, o_ref, state_ref, *, sub, n_sub, heads, dk):
    @pl.when(pl.program_id(2) == 0)
    def _():
        state_ref[...] = jnp.zeros_like(state_ref)

    lb = lb_ref[...]
    nw = nw_ref[...]
    ti = lax.broadcasted_iota(jnp.int32, (sub, sub), 0)
    si = lax.broadcasted_iota(jnp.int32, (sub, sub), 1)
    causal = ti >= si
    f = lb + (1.0 - lb) * jax.nn.sigmoid(f_ref[0])
    lf = jnp.log(f)
    k = 1.0 - f
    b = _seg_cumsum_rows(lf, sub)
    to_end = _seg_cumsum_rows(lf, sub, reverse=True) - lf
    q_dec = (_silu(q_ref[0]) * jnp.exp(b)).astype(BF16)
    k_dec = (k * jnp.exp(-b)).astype(BF16)
    k_end = (k * jnp.exp(to_end)).astype(BF16)
    v = i_ref[0].astype(BF16)
    cs = range(n_sub)
    hs = range(heads)
    blk = lambda x, c, h: x[c * sub:(c + 1) * sub, h * dk:(h + 1) * dk]
    scores = [[jnp.where(causal, _dot_nt(blk(q_dec, c, h), blk(k_dec, c, h)), 0.0).astype(BF16) for h in hs]
              for c in cs]
    upd = [[_dot_tn(blk(v, c, h), blk(k_end, c, h)) for h in hs] for c in cs]
    states = []
    st = [state_ref[h] for h in hs]
    for c in cs:
        states.append(st)
        decay = jnp.exp(b[(c + 1) * sub - 1:(c + 1) * sub, :])
        st = [st[h] * decay[:, h * dk:(h + 1) * dk] + upd[c][h] for h in hs]
    for h in hs:
        state_ref[h] = st[h]
    for c in cs:
        rows = slice(c * sub, (c + 1) * sub)
        for h in hs:
            o = (jnp.dot(scores[c][h], blk(v, c, h), preferred_element_type=F32)
                 + _dot_nt(blk(q_dec, c, h), states[c][h].astype(BF16)))
            o = o * lax.rsqrt(jnp.mean(o * o, axis=-1, keepdims=True) + NORM_EPS) * nw
            cols = slice(h * dk, (h + 1) * dk)
            o_ref[0, rows, cols] = (o * _silu(g_ref[0, rows, cols])).astype(o_ref.dtype)


def hgrn2_scan(proj, lower_bound, norm_w, *, dk=HG_HEAD_DIM, sub=HG_CHUNK, tb=256, heads=8):
    bsz, s_len, d4 = proj.shape
    d = d4 // 4
    n_heads = d // dk
    tb = min(tb, s_len)
    heads = min(heads, n_heads)
    hw = heads * dk
    n_hb = n_heads // heads
    kern = functools.partial(_hgrn_kernel, sub=sub, n_sub=tb // sub, heads=heads, dk=dk)
    spec = lambda part: pl.BlockSpec((1, tb, hw), lambda b_, h_, t: (b_, t, part * n_hb + h_))
    return pl.pallas_call(
        kern,
        grid=(bsz, n_hb, s_len // tb),
        in_specs=[spec(0), spec(1), spec(2), spec(3),
                  pl.BlockSpec((1, hw), lambda b_, h_, t: (0, h_)),
                  pl.BlockSpec((1, dk), lambda b_, h_, t: (0, 0))],
        out_specs=pl.BlockSpec((1, tb, hw), lambda b_, h_, t: (b_, t, h_)),
        out_shape=jax.ShapeDtypeStruct((bsz, s_len, d), BF16),
        scratch_shapes=[pltpu.VMEM((heads, dk, dk), F32)],
        compiler_params=_params("parallel", "parallel", "arbitrary"),
        name="hgrn2_scan",
    )(proj, proj, proj, proj, lower_bound.reshape(1, d), norm_w.reshape(1, dk))


def hgrn2_mixer(u, h, w, lower_bound, bsz, s_len):
    d = h.shape[1]
    proj = matmul_ws(u, [(w["hg_w_in"], 0)], 4 * d, bm=512, bn=1024, name="hg_in")
    o = hgrn2_scan(proj.reshape(bsz, s_len, 4 * d), lower_bound, w["hg_norm_w"])
    return matmul_ws(o.reshape(bsz * s_len, d), [(w["hg_w_out"], 0)], d,
                     epilogue=_ep_residual, extras=[(h, "mn")], name="hg_out")


def dense_ffn(v, h, w_in, w_out):
    f = w_out.shape[0]
    hid = matmul_ws(v, [(w_in, 0), (w_in, f)], f, epilogue=_ep_swiglu, out_dtype=BF16, bm=1024, bn=256,
                    name="ffn_in")
    return matmul_ws(hid, [(w_out, 0)], h.shape[1], epilogue=_ep_residual, extras=[(h, "mn")], bm=512, w_buffers=1,
                     name="ffn_out")


def _router_kernel(x_ref, r_ref, o_ref, *, n_experts):
    logits = jnp.dot(x_ref[...], r_ref[...], preferred_element_type=F32)
    lane = lax.broadcasted_iota(jnp.int32, logits.shape, 1)
    logits = jnp.where(lane < n_experts, logits, -jnp.inf)
    m1 = jnp.max(logits, axis=-1, keepdims=True)
    i1 = jnp.min(jnp.where(logits == m1, lane, LANES), axis=-1, keepdims=True)
    rest = jnp.where(lane == i1, -jnp.inf, logits)
    m2 = jnp.max(rest, axis=-1, keepdims=True)
    i2 = jnp.min(jnp.where(rest == m2, lane, LANES), axis=-1, keepdims=True)
    e2 = jnp.exp(m2 - m1)
    w1 = 1.0 / (1.0 + e2)
    o_ref[...] = jnp.where(lane == i1, w1, 0.0) + jnp.where(lane == i2, e2 * w1, 0.0)


def moe_router(v, router):
    m, d = v.shape
    n_experts = router.shape[1]
    r_pad = jnp.zeros((d, LANES), BF16).at[:, :n_experts].set(router.astype(BF16))
    bm = _pick(m, 512)
    return pl.pallas_call(
        functools.partial(_router_kernel, n_experts=n_experts),
        grid=(m // bm,),
        in_specs=[pl.BlockSpec((bm, d), lambda i: (i, 0)), pl.BlockSpec((d, LANES), lambda i: (0, 0))],
        out_specs=pl.BlockSpec((bm, LANES), lambda i: (i, 0)),
        out_shape=jax.ShapeDtypeStruct((m, LANES), F32),
        compiler_params=_params("parallel"),
        name="moe_router",
    )(v, r_pad)


def _moe_in_kernel(x_ref, wg_ref, wu_ref, c_ref, o_ref, wgb_ref, wub_ref, *, blocks_per_expert):
    @pl.when(pl.program_id(1) == 0)
    def _():
        wgb_ref[...] = wg_ref[0].astype(BF16)
        wub_ref[...] = wu_ref[0].astype(BF16)

    x = x_ref[...]
    g = jnp.dot(x, wgb_ref[...], preferred_element_type=F32)
    u = jnp.dot(x, wub_ref[...], preferred_element_type=F32)
    e = pl.program_id(0) // blocks_per_expert
    comb = c_ref[...]
    lane = lax.broadcasted_iota(jnp.int32, comb.shape, 1)
    scale = jnp.sum(jnp.where(lane == e, comb, 0.0), axis=-1, keepdims=True)
    o_ref[...] = (_silu(g) * u * scale).astype(o_ref.dtype)


def moe_ffn(v, h, router, w_in, w_out, *, bm=512, bn=512):
    m, d = v.shape
    n_experts, _, two_de = w_in.shape
    de = two_de // 2
    bm = _pick(m, bm)
    bn = _pick(de, bn)
    bpe = de // bn
    comb = moe_router(v, router)
    hid = pl.pallas_call(
        functools.partial(_moe_in_kernel, blocks_per_expert=bpe),
        grid=(n_experts * bpe, m // bm),
        in_specs=[pl.BlockSpec((bm, d), lambda j, i: (i, 0)),
                  pl.BlockSpec((1, d, bn), lambda j, i: (j // bpe, 0, j % bpe)),
                  pl.BlockSpec((1, d, bn), lambda j, i: (j // bpe, 0, j % bpe + bpe)),
                  pl.BlockSpec((bm, LANES), lambda j, i: (i, 0))],
        out_specs=pl.BlockSpec((bm, bn), lambda j, i: (i, j)),
        out_shape=jax.ShapeDtypeStruct((m, n_experts * de), BF16),
        scratch_shapes=[pltpu.VMEM((d, bn), BF16), pltpu.VMEM((d, bn), BF16)],
        compiler_params=_params("parallel", "arbitrary"),
        name="moe_in",
    )(v, w_in, w_in, comb)
    return matmul(hid, [(w_out.reshape(n_experts * de, d), 0)], d, epilogue=_ep_residual, extras=[(h, "mn")],
                  name="moe_out")


MOE_BLOCK = 1024
MOE_UNIT = 64
MOE_TILE = 512


def _moe_gather_kernel(x_ref, tok_ref, o_ref):
    tok = tok_ref[0]
    lane = lax.broadcasted_iota(jnp.int32, (tok.shape[0], x_ref.shape[0]), 1)
    onehot = jnp.where(tok == lane, 1.0, 0.0).astype(BF16)
    o_ref[...] = jnp.dot(onehot, x_ref[...], preferred_element_type=F32).astype(o_ref.dtype)


def _moe_expert_in_kernel(src_ref, exp_ref, first_ref, used_ref, *refs, per):
    x_refs = refs[:per]
    wg_ref, wu_ref, rw_ref, o_ref, wgb_ref, wub_ref, x_scr = refs[per:]
    t = pl.program_id(1)

    @pl.when(t < used_ref[0])
    def _():
        @pl.when(first_ref[t] == 1)
        def _():
            wgb_ref[...] = wg_ref[0].astype(BF16)
            wub_ref[...] = wu_ref[0].astype(BF16)

        unit = x_refs[0].shape[0]
        for i in range(per):
            x_scr[i * unit:(i + 1) * unit, :] = x_refs[i][...]
        x = x_scr[...]
        g = jnp.dot(x, wgb_ref[...], preferred_element_type=F32)
        u = jnp.dot(x, wub_ref[...], preferred_element_type=F32)
        o_ref[...] = (_silu(g) * u * rw_ref[...]).astype(o_ref.dtype)

    @pl.when(t >= used_ref[0])
    def _():
        o_ref[...] = jnp.zeros_like(o_ref)


def _moe_expert_out_kernel(exp_ref, first_ref, used_ref, hid_ref, w_ref, o_ref, wb_ref):
    t = pl.program_id(1)

    @pl.when(t < used_ref[0])
    def _():
        @pl.when(first_ref[t] == 1)
        def _():
            wb_ref[...] = w_ref[0].astype(BF16)

        o_ref[...] = jnp.dot(hid_ref[...], wb_ref[...], preferred_element_type=F32).astype(o_ref.dtype)

    @pl.when(pl.program_id(1) >= used_ref[0])
    def _():
        o_ref[...] = jnp.zeros_like(o_ref)


def _moe_scatter_kernel(dst_ref, h_ref, tok_ref, *refs, per):
    y_refs = refs[:per]
    o_ref, y_scr = refs[per:]

    @pl.when(pl.program_id(2) == 0)
    def _():
        o_ref[...] = h_ref[...]

    tok = tok_ref[0]
    row = lax.broadcasted_iota(jnp.int32, (o_ref.shape[0], tok.shape[1]), 0)
    onehot_t = jnp.where(tok == row, 1.0, 0.0).astype(BF16)
    unit = y_refs[0].shape[0]
    for i in range(per):
        y_scr[i * unit:(i + 1) * unit, :] = y_refs[i][...]
    o_ref[...] += jnp.dot(onehot_t, y_scr[...], preferred_element_type=F32)


def moe_ffn_routed(v, h, router, w_in, w_out, *, tb=MOE_BLOCK, unit=MOE_UNIT, tile=MOE_TILE, bn=512, bo=1024):
    m, d = v.shape
    n_experts, _, two_de = w_in.shape
    de = two_de // 2
    tb = min(tb, m)
    nb = m // tb
    per = tile // unit
    bn = _pick(de, bn)
    bo = _pick(d, bo)
    n_assign = MOE_TOPK * tb
    n_slots = -(-(n_assign // unit + n_experts + 1) // per) * per
    groups = n_slots // per
    n_units = nb * (n_assign // unit + n_experts) + n_experts * (per - 1)
    n_tiles = -(-n_units // per)
    n_units = n_tiles * per

    comb = moe_router(v, router)
    wts, ids = lax.top_k(comb[:, :n_experts], MOE_TOPK)
    ea = ids.reshape(nb, n_assign)
    wa = wts.reshape(nb, n_assign)
    ta = jnp.broadcast_to(jnp.repeat(jnp.arange(tb, dtype=jnp.int32), MOE_TOPK)[None], (nb, n_assign))
    se, st, sw = lax.sort((ea, ta, wa), dimension=1, num_keys=1, is_stable=True)
    counts = jnp.sum(jax.nn.one_hot(ea, n_experts, dtype=jnp.int32), axis=1)
    units = -(-counts // unit)
    excl = lambda x, axis: jnp.cumsum(x, axis=axis) - x
    slot_start = excl(units, 1)
    row_start = excl(counts, 1)
    is_e = se[..., None] == jnp.arange(n_experts, dtype=jnp.int32)
    lookup = lambda table: jnp.sum(jnp.where(is_e, table[:, None, :], 0), axis=-1)
    pos = lookup(slot_start) * unit + jnp.arange(n_assign, dtype=jnp.int32)[None] - lookup(row_start)
    bidx = jnp.arange(nb, dtype=jnp.int32)[:, None]
    hit = pos[:, None, :] == jnp.arange(n_slots * unit, dtype=jnp.int32)[None, :, None]
    row_token = jnp.sum(jnp.where(hit, st[:, None, :] + 1, 0), axis=-1) - 1
    row_weight = jnp.sum(jnp.where(hit, sw[:, None, :], 0.0), axis=-1)
    per_expert = jnp.sum(units, axis=0)
    per_expert_pad = -(-per_expert // per) * per
    e_off = excl(per_expert_pad, 0)
    before = excl(units, 0)
    slot = jnp.arange(n_slots, dtype=jnp.int32)
    slot_end = jnp.cumsum(units, axis=1)
    e_of_slot = jnp.sum(slot[None, :, None] >= slot_end[:, None, :], axis=-1)
    used_slot = e_of_slot < n_experts
    e_clip = jnp.minimum(e_of_slot, n_experts - 1)
    dst_unit = (e_off[e_clip] + jnp.take_along_axis(before, e_clip, axis=1)
                + slot[None] - jnp.take_along_axis(slot_start, e_clip, axis=1))
    dst_unit = jnp.where(used_slot, dst_unit, 0).astype(jnp.int32)
    flat_slot = (bidx * n_slots + slot[None]).astype(jnp.int32)
    zero_slot = n_slots - 1
    src_unit = jnp.full((n_units,), zero_slot, jnp.int32).at[
        jnp.where(used_slot, dst_unit, n_units).reshape(-1)].set(flat_slot.reshape(-1), mode="drop")
    tile_end = jnp.cumsum(per_expert_pad) // per
    tile_ids = jnp.arange(n_tiles, dtype=jnp.int32)
    tile_expert = jnp.minimum(jnp.sum(tile_ids[:, None] >= tile_end[None, :], axis=-1), n_experts - 1).astype(jnp.int32)
    tiles_used = tile_end[-1:].astype(jnp.int32)
    first = jnp.concatenate([jnp.ones((1,), jnp.int32),
                             (tile_expert[1:] != tile_expert[:-1]).astype(jnp.int32)])
    rw_em = row_weight.reshape(nb * n_slots, unit)[src_unit].reshape(n_units * unit, 1)

    xs = pl.pallas_call(
        _moe_gather_kernel,
        grid=(nb, groups),
        in_specs=[pl.BlockSpec((tb, d), lambda b_, g: (b_, 0)),
                  pl.BlockSpec((1, tile, 1), lambda b_, g: (b_, g, 0))],
        out_specs=pl.BlockSpec((tile, d), lambda b_, g: (b_ * groups + g, 0)),
        out_shape=jax.ShapeDtypeStruct((nb * n_slots * unit, d), BF16),
        compiler_params=_params("parallel", "arbitrary"),
        name="moe_gather",
    )(v, row_token.reshape(nb, n_slots * unit, 1))

    bpe = de // bn
    unit_spec = lambda i: pl.BlockSpec((unit, d), lambda j, t, src, ex, fi, us: (src[per * t + i], 0))
    hid = pl.pallas_call(
        functools.partial(_moe_expert_in_kernel, per=per),
        grid_spec=pltpu.PrefetchScalarGridSpec(
            num_scalar_prefetch=4,
            grid=(bpe, n_tiles),
            in_specs=[unit_spec(i) for i in range(per)] + [
                pl.BlockSpec((1, d, bn), lambda j, t, src, ex, fi, us: (ex[t], 0, j),
                             pipeline_mode=pl.Buffered(1)),
                pl.BlockSpec((1, d, bn), lambda j, t, src, ex, fi, us: (ex[t], 0, j + bpe),
                             pipeline_mode=pl.Buffered(1)),
                pl.BlockSpec((tile, 1), lambda j, t, src, ex, fi, us: (t, 0))],
            out_specs=pl.BlockSpec((tile, bn), lambda j, t, src, ex, fi, us: (t, j)),
            scratch_shapes=[pltpu.VMEM((d, bn), BF16), pltpu.VMEM((d, bn), BF16), pltpu.VMEM((tile, d), BF16)]),
        out_shape=jax.ShapeDtypeStruct((n_tiles * tile, de), BF16),
        compiler_params=_params("arbitrary", "arbitrary"),
        name="moe_expert_in",
    )(src_unit, tile_expert, first, tiles_used, *([xs] * per), w_in, w_in, rw_em)

    bo2 = _pick(d, 2 * bo)
    ys = pl.pallas_call(
        _moe_expert_out_kernel,
        grid_spec=pltpu.PrefetchScalarGridSpec(
            num_scalar_prefetch=3,
            grid=(d // bo2, n_tiles),
            in_specs=[pl.BlockSpec((tile, de), lambda n, t, ex, fi, us: (t, 0)),
                      pl.BlockSpec((1, de, bo2), lambda n, t, ex, fi, us: (ex[t], 0, n))],
            out_specs=pl.BlockSpec((tile, bo2), lambda n, t, ex, fi, us: (t, n)),
            scratch_shapes=[pltpu.VMEM((de, bo2), BF16)]),
        out_shape=jax.ShapeDtypeStruct((n_tiles * tile, d), BF16),
        compiler_params=_params("arbitrary", "arbitrary"),
        name="moe_expert_out",
    )(tile_expert, first, tiles_used, hid, w_out)

    y_spec = lambda i: pl.BlockSpec((unit, bo2), lambda b_, n, g, dst: (dst[(b_ * groups + g) * per + i], n))
    return pl.pallas_call(
        functools.partial(_moe_scatter_kernel, per=per),
        grid_spec=pltpu.PrefetchScalarGridSpec(
            num_scalar_prefetch=1,
            grid=(nb, d // bo2, groups),
            in_specs=[pl.BlockSpec((tb, bo2), lambda b_, n, g, dst: (b_, n)),
                      pl.BlockSpec((1, 1, tile), lambda b_, n, g, dst: (b_ * groups + g, 0, 0))]
                     + [y_spec(i) for i in range(per)],
            out_specs=pl.BlockSpec((tb, bo2), lambda b_, n, g, dst: (b_, n)),
            scratch_shapes=[pltpu.VMEM((tile, bo2), BF16)]),
        out_shape=jax.ShapeDtypeStruct((m, d), F32),
        compiler_params=_params("parallel", "parallel", "arbitrary"),
        name="moe_scatter",
    )(dst_unit.reshape(-1), h, row_token.reshape(nb * groups, 1, tile), *([ys] * per))


def ple_gate(h, p_i, norm_pl, pl_proj, pl_gate, layer):
    d = h.shape[1]
    n = rmsnorm(h, norm_pl, name="rmsnorm_ple")
    return matmul_ws(n, [(pl_gate, (layer, 0))], d, epilogue=_ep_ple_gate,
                     extras=[(h, "mn"), (p_i, "m"), (pl_proj, "kn")], name="ple_gate")


def _rw_mix_kernel(u_ref, mu_ref, *o_refs):
    u = u_ref[0]
    row = lax.broadcasted_iota(jnp.int32, u.shape, 0)
    dx = jnp.where(row >= 1, pltpu.roll(u, 1, 0), 0.0) - u
    for j, o_ref in enumerate(o_refs):
        o_ref[0] = (u + dx * mu_ref[j:j + 1, :]).astype(o_ref.dtype)


def rw_token_mix(u, mu):
    bsz, s_len, d = u.shape
    cb = _pick(d, LANES)
    n_mix = mu.shape[0]
    spec = pl.BlockSpec((1, s_len, cb), lambda b_, j: (b_, 0, j))
    return pl.pallas_call(
        _rw_mix_kernel,
        grid=(bsz, d // cb),
        in_specs=[spec, pl.BlockSpec((n_mix, cb), lambda b_, j: (0, j))],
        out_specs=[spec] * n_mix,
        out_shape=[jax.ShapeDtypeStruct(u.shape, BF16)] * n_mix,
        compiler_params=_params("parallel", "parallel"),
        name="rwkv_token_mix",
    )(u, mu)


def _dot_hi(a, b):
    return jnp.dot(a, b, preferred_element_type=F32, precision=lax.Precision.HIGHEST)


def _rw_scan_kernel(r_ref, k_ref, v_ref, a_ref, lw_ref, g_ref, kk_ref, ka_ref, rk_ref, lnw_ref, lnb_ref,
                    o_ref, state_ref, *, chunk, heads, n):
    @pl.when(pl.program_id(2) == 0)
    def _():
        state_ref[...] = jnp.zeros_like(state_ref)

    hs = range(heads)
    sls = [slice(j * n, (j + 1) * n) for j in hs]
    ti = lax.broadcasted_iota(jnp.int32, (chunk, chunk), 0)
    si = lax.broadcasted_iota(jnp.int32, (chunk, chunk), 1)
    strict = ti > si
    incl = ti >= si
    dot = functools.partial(jnp.dot, preferred_element_type=F32)

    r = [r_ref[0, :, sl] for sl in sls]
    v = [v_ref[0, :, sl] for sl in sls]
    a = [a_ref[0, :, sl] for sl in sls]
    lw = [lw_ref[0, :, sl] for sl in sls]
    k = [k_ref[0, :, sl] for sl in sls]
    kk = [k[j] * kk_ref[:, sls[j]] for j in hs]
    kk = [kk[j] / jnp.maximum(jnp.sqrt(jnp.sum(kk[j] * kk[j], axis=-1, keepdims=True)), 1e-12) for j in hs]
    kmod = [k[j] * (1.0 + (a[j] - 1.0) * ka_ref[:, sls[j]]) for j in hs]
    kka = [kk[j] * a[j] for j in hs]
    cum = [_cumsum_rows(lw[j], chunk) for j in hs]
    cum_end = [c[chunk - 1:chunk, :] for c in cum]
    mid = [c[chunk // 2 - 1:chunk // 2, :] for c in cum]
    e_neg = [jnp.exp(mid[j] - cum[j]) for j in hs]
    am = [(kk[j] * jnp.exp(cum[j] - lw[j] - mid[j])).astype(BF16) for j in hs]
    bm = [(kka[j] * e_neg[j]).astype(BF16) for j in hs]
    km = [(kmod[j] * e_neg[j]).astype(BF16) for j in hs]
    rm = [(r[j] * jnp.exp(cum[j] - mid[j])).astype(BF16) for j in hs]
    a_abs = [(kk[j] * jnp.exp(cum[j] - lw[j])).astype(BF16) for j in hs]
    r_abs = [(r[j] * jnp.exp(cum[j])).astype(BF16) for j in hs]
    vb = [x.astype(BF16) for x in v]
    st = [state_ref[j] for j in hs]
    stb = [x.astype(BF16) for x in st]

    nb = [(-jnp.where(strict, _dot_nt(am[j], bm[j]), 0.0)).astype(BF16) for j in hs]
    lk = [jnp.where(strict, _dot_nt(am[j], km[j]), 0.0).astype(BF16) for j in hs]
    x = [_dot_nt(a_abs[j], stb[j]) + dot(lk[j], vb[j]) for j in hs]
    x = [x[j] + dot(nb[j], x[j].astype(BF16)) for j in hs]
    p = 2
    while p < chunk:
        nb = [dot(nb[j], nb[j]).astype(BF16) for j in hs]
        x = [x[j] + dot(nb[j], x[j].astype(BF16)) for j in hs]
        p *= 2
    pb = [xj.astype(BF16) for xj in x]
    mk = [jnp.where(incl, _dot_nt(rm[j], km[j]), 0.0).astype(BF16) for j in hs]
    mb = [jnp.where(incl, _dot_nt(rm[j], bm[j]), 0.0).astype(BF16) for j in hs]
    y = [_dot_nt(r_abs[j], stb[j]) + dot(mk[j], vb[j]) - dot(mb[j], pb[j]) for j in hs]
    to_end = [jnp.exp(cum_end[j] - cum[j]) for j in hs]
    for j in hs:
        state_ref[j] = (st[j] * jnp.exp(cum_end[j]) + _dot_tn(vb[j], (kmod[j] * to_end[j]).astype(BF16))
                        - _dot_tn(pb[j], (kka[j] * to_end[j]).astype(BF16)))
    for j in hs:
        sl = sls[j]
        bonus = jnp.sum(r[j] * kmod[j] * rk_ref[:, sl], axis=-1, keepdims=True) * v[j]
        mean = jnp.mean(y[j], axis=-1, keepdims=True)
        yc = y[j] - mean
        var = jnp.mean(yc * yc, axis=-1, keepdims=True)
        yn = yc * lax.rsqrt(var + RW_LN_EPS) * lnw_ref[:, sl] + lnb_ref[:, sl]
        o_ref[0, :, sl] = ((yn + bonus) * g_ref[0, :, sl]).astype(o_ref.dtype)


def _rw_scan_tile_kernel(r_ref, k_ref, v_ref, a_ref, lw_ref, g_ref, kk_ref, ka_ref, rk_ref, lnw_ref, lnb_ref,
                         o_ref, state_ref, *, chunk, heads, n):
    @pl.when(pl.program_id(2) == 0)
    def _():
        state_ref[...] = jnp.zeros_like(state_ref)

    per = LANES // n
    tiles = range(heads // per)
    sub = range(per)
    ti = lax.broadcasted_iota(jnp.int32, (chunk, chunk), 0)
    si = lax.broadcasted_iota(jnp.int32, (chunk, chunk), 1)
    strict = ti > si
    incl = ti >= si
    lane_seg = lax.broadcasted_iota(jnp.int32, (1, LANES), 1) // n
    seg_is = [lane_seg == j for j in sub]
    same_head = (lax.broadcasted_iota(jnp.int32, (LANES, LANES), 0) // n
                 == lax.broadcasted_iota(jnp.int32, (LANES, LANES), 1) // n)
    dot = functools.partial(jnp.dot, preferred_element_type=F32)
    tile = lambda x, i: x[:, i * LANES:(i + 1) * LANES]

    def pick(vals):
        out = vals[-1]
        for j in range(per - 2, -1, -1):
            out = jnp.where(seg_is[j], vals[j], out)
        return out

    def seg_sum(x):
        return pick([jnp.sum(jnp.where(seg_is[j], x, 0.0), axis=-1, keepdims=True) for j in sub])

    r, k, v, a, lw = r_ref[0], k_ref[0], v_ref[0], a_ref[0], lw_ref[0]
    kk = k * kk_ref[...]
    kmod = k * (1.0 + (a - 1.0) * ka_ref[...])
    cum = _cumsum_rows(lw, chunk)
    cum_end = cum[chunk - 1:chunk, :]
    mid = cum[chunk // 2 - 1:chunk // 2, :]
    bonus_in = r * kmod * rk_ref[...]
    kk_t, bonus_t = [], []
    for i in tiles:
        kki = tile(kk, i)
        kk_t.append(kki / jnp.maximum(jnp.sqrt(seg_sum(kki * kki)), 1e-12))
        bonus_t.append(seg_sum(tile(bonus_in, i)) * tile(v, i))
    kk = jnp.concatenate(kk_t, axis=-1) if len(kk_t) > 1 else kk_t[0]
    kka = kk * a
    e_neg = jnp.exp(mid - cum)
    to_end = jnp.exp(cum_end - cum)
    am = (kk * jnp.exp(cum - lw - mid)).astype(BF16)
    bm = (kka * e_neg).astype(BF16)
    km = (kmod * e_neg).astype(BF16)
    rm = (r * jnp.exp(cum - mid)).astype(BF16)
    a_abs = (kk * jnp.exp(cum - lw)).astype(BF16)
    r_abs = (r * jnp.exp(cum)).astype(BF16)
    k_end = (kmod * to_end).astype(BF16)
    b_end = (kka * to_end).astype(BF16)
    vb = v.astype(BF16)
    st_decay = jnp.exp(cum_end)
    zero = jnp.zeros((), BF16)

    st = [state_ref[i] for i in tiles]
    stb = [s.astype(BF16) for s in st]
    am_h = [[jnp.where(seg_is[j], tile(am, i), zero) for j in sub] for i in tiles]
    rm_h = [[jnp.where(seg_is[j], tile(rm, i), zero) for j in sub] for i in tiles]
    nb = [[(-jnp.where(strict, _dot_nt(am_h[i][j], tile(bm, i)), 0.0)).astype(BF16) for j in sub] for i in tiles]
    lk = [[jnp.where(strict, _dot_nt(am_h[i][j], tile(km, i)), 0.0).astype(BF16) for j in sub] for i in tiles]
    x = [_dot_nt(tile(a_abs, i), stb[i]) + pick([dot(lk[i][j], tile(vb, i)) for j in sub]) for i in tiles]
    xb = [xi.astype(BF16) for xi in x]
    x = [x[i] + pick([dot(nb[i][j], xb[i]) for j in sub]) for i in tiles]
    p = 2
    while p < chunk:
        nb = [[dot(nb[i][j], nb[i][j]).astype(BF16) for j in sub] for i in tiles]
        xb = [xi.astype(BF16) for xi in x]
        x = [x[i] + pick([dot(nb[i][j], xb[i]) for j in sub]) for i in tiles]
        p *= 2
    pb = [xi.astype(BF16) for xi in x]
    mk = [[jnp.where(incl, _dot_nt(rm_h[i][j], tile(km, i)), 0.0).astype(BF16) for j in sub] for i in tiles]
    mb = [[jnp.where(incl, _dot_nt(rm_h[i][j], tile(bm, i)), 0.0).astype(BF16) for j in sub] for i in tiles]
    y = [_dot_nt(tile(r_abs, i), stb[i])
         + pick([dot(mk[i][j], tile(vb, i)) - dot(mb[i][j], pb[i]) for j in sub]) for i in tiles]
    for i in tiles:
        upd = _dot_tn(tile(vb, i), tile(k_end, i)) - _dot_tn(pb[i], tile(b_end, i))
        state_ref[i] = st[i] * tile(st_decay, i) + jnp.where(same_head, upd, 0.0)
    inv_n = 1.0 / n
    for i in tiles:
        cols = slice(i * LANES, (i + 1) * LANES)
        mean = seg_sum(y[i]) * inv_n
        yc = y[i] - mean
        var = seg_sum(yc * yc) * inv_n
        yn = yc * lax.rsqrt(var + RW_LN_EPS) * lnw_ref[:, cols] + lnb_ref[:, cols]
        o_ref[0, :, cols] = ((yn + bonus_t[i]) * g_ref[0, :, cols]).astype(o_ref.dtype)


def rw_scan(r, k, v, a, lw, g, k_k, k_a, r_k, ln_w, ln_b, *, n=RW_HEAD_DIM, chunk=RW_CHUNK, heads=8):
    bsz, s_len, d = r.shape
    chunk = min(chunk, s_len)
    heads = min(heads, d // n)
    hw = heads * n
    seq = pl.BlockSpec((1, chunk, hw), lambda b_, h_, c: (b_, c, h_))
    par = pl.BlockSpec((1, hw), lambda b_, h_, c: (0, h_))
    row = lambda t: t.reshape(1, d)
    assert hw % LANES == 0 and LANES % n == 0
    kern = functools.partial(_rw_scan_tile_kernel, chunk=chunk, heads=heads, n=n)
    return pl.pallas_call(
        kern,
        grid=(bsz, d // hw, s_len // chunk),
        in_specs=[seq] * 6 + [par] * 5,
        out_specs=seq,
        out_shape=jax.ShapeDtypeStruct(r.shape, BF16),
        scratch_shapes=[pltpu.VMEM((hw // LANES, LANES, LANES), F32)],
        compiler_params=_params("parallel", "parallel", "arbitrary"),
        name="rwkv7_scan",
    )(r, k, v, a, lw, g, row(k_k), row(k_a), row(r_k), row(ln_w), row(ln_b))


def rwkv7_mixer(u, h, w, bsz, s_len):
    t, d = u.shape
    xr, xw, xk, xv, xa, xg = [x.reshape(t, d) for x in rw_token_mix(u.reshape(bsz, s_len, d), w["rw_mu"])]
    r = matmul_ws(xr, [(w["rw_w_rkv"], (0, 0))], d, name="rw_r")
    k = matmul_ws(xk, [(w["rw_w_rkv"], (1, 0))], d, name="rw_k")
    v = matmul_ws(xv, [(w["rw_w_rkv"], (2, 0))], d, name="rw_v")
    row = lambda x: x.reshape(1, d)
    w_lo = matmul(xw, [(w["rw_w1"], 0)], w["rw_w1"].shape[1], epilogue=_ep_tanh, out_dtype=BF16, name="rw_w1")
    wide = 2048
    lw = matmul(w_lo, [(w["rw_w2"], 0)], d, epilogue=_ep_rw_logdecay, extras=[(row(w["rw_w0"]), "n")], bn=wide,
                name="rw_w2")
    a_lo = matmul(xa, [(w["rw_a1"], 0)], w["rw_a1"].shape[1], out_dtype=BF16, name="rw_a1")
    a = matmul(a_lo, [(w["rw_a2"], 0)], d, epilogue=_ep_bias_sigmoid, extras=[(row(w["rw_a0"]), "n")], bn=wide,
               name="rw_a2")
    g_lo = matmul(xg, [(w["rw_g1"], 0)], w["rw_g1"].shape[1], epilogue=_ep_sigmoid, out_dtype=BF16, name="rw_g1")
    g = matmul(g_lo, [(w["rw_g2"], 0)], d, bn=wide, name="rw_g2")
    shp = (bsz, s_len, d)
    y = rw_scan(r.reshape(shp), k.reshape(shp), v.reshape(shp), a.reshape(shp), lw.reshape(shp), g.reshape(shp),
                w["rw_k_k"], w["rw_k_a"], w["rw_r_k"], w["rw_ln_w"], w["rw_ln_b"])
    return matmul_ws(y.reshape(t, d), [(w["rw_w_out"], 0)], d, epilogue=_ep_residual, extras=[(h, "mn")],
                     name="rw_out")


NEG_BIG = -1e30


def _rope_kernel(x_ref, cc_ref, ss_ref, o_ref, *, n_q_slots, scale):
    x = x_ref[0]
    out = x * cc_ref[...] + pltpu.roll(x, x.shape[-1] // 2, 1) * ss_ref[...]
    out = out * jnp.where(pl.program_id(2) < n_q_slots, scale, 1.0)
    o_ref[0] = out.astype(o_ref.dtype)


def _rope_tables(pos, dim):
    inv = ROPE_THETA ** (-(jnp.arange(0, dim, 2, dtype=F32) / dim))
    ang = pos.astype(F32)[:, None] * inv[None, :]
    cos, sin = jnp.cos(ang), jnp.sin(ang)
    return jnp.concatenate([cos, cos], axis=-1), jnp.concatenate([-sin, sin], axis=-1)


def nsa_rope(proj, n_q_slots, k_slots, dh, scale, tb=512):
    bsz, s_len, _ = proj.shape
    tb = min(tb, s_len)
    cc, ss = _rope_tables(jnp.arange(s_len), dh)
    n_out = n_q_slots + len(k_slots)

    def in_slot(j):
        slot = j
        for idx, ks in enumerate(k_slots):
            slot = jnp.where(j == n_q_slots + idx, ks, slot)
        return slot

    return pl.pallas_call(
        functools.partial(_rope_kernel, n_q_slots=n_q_slots, scale=scale),
        grid=(bsz, s_len // tb, n_out),
        in_specs=[pl.BlockSpec((1, tb, dh), lambda b_, t, j: (b_, t, in_slot(j))),
                  pl.BlockSpec((tb, dh), lambda b_, t, j: (t, 0)),
                  pl.BlockSpec((tb, dh), lambda b_, t, j: (t, 0))],
        out_specs=pl.BlockSpec((1, tb, dh), lambda b_, t, j: (b_, t, j)),
        out_shape=jax.ShapeDtypeStruct((bsz, s_len, n_out * dh), BF16),
        compiler_params=_params("parallel", "parallel", "arbitrary"),
        name="nsa_rope",
    )(proj, cc, ss)


def _cmp_finish_kernel(z_ref, bias_ref, w2_ref, cc_ref, ss_ref, o_ref, *, hidden, rope):
    z = z_ref[0]
    nc = z.shape[0]
    nxt = pltpu.roll(z[:, hidden:], nc - 1, 0)
    hid = _silu(z[:, :hidden] + nxt + bias_ref[...])
    out = jnp.dot(hid.astype(BF16), w2_ref[...], preferred_element_type=F32)
    if rope:
        out = out * cc_ref[...] + pltpu.roll(out, out.shape[-1] // 2, 1) * ss_ref[...]
    o_ref[0] = out.astype(o_ref.dtype)


def nsa_compress(x, pos_emb, w1, w2, bsz, s_len, groups, dh, rope, transpose_out=False):
    stride, blk = NSA_CMP_STRIDE, NSA_CMP_BLOCK
    nc = s_len // stride
    hidden = w1.shape[-1]
    half = stride * dh
    x16 = jnp.transpose(x.reshape(bsz, nc, stride, groups, dh), (0, 3, 1, 2, 4)).reshape(bsz * groups * nc, half)
    w1f = w1.reshape(blk * dh, hidden)
    wcat = jnp.concatenate([w1f[:half], w1f[half:]], axis=1).astype(BF16)
    z = matmul(x16.astype(BF16), [(wcat, 0)], 2 * hidden, name="nsa_cmp_w1")
    bias = matmul(pos_emb.reshape(1, blk * dh).astype(BF16), [(w1f.astype(BF16), 0)], hidden, name="nsa_cmp_pos")
    cc, ss = _rope_tables(jnp.arange(nc) * stride + blk - 1, dh)
    if transpose_out:
        assert not rope
        return pl.pallas_call(
            functools.partial(_cmp_finish_t_kernel, hidden=hidden),
            grid=(bsz * groups,),
            in_specs=[pl.BlockSpec((1, nc, 2 * hidden), lambda i: (i, 0, 0)),
                      pl.BlockSpec((1, hidden), lambda i: (0, 0)),
                      pl.BlockSpec((dh, hidden), lambda i: (0, 0))],
            out_specs=pl.BlockSpec((1, dh, nc), lambda i: (i, 0, 0)),
            out_shape=jax.ShapeDtypeStruct((bsz * groups, dh, nc), BF16),
            compiler_params=_params("parallel"),
            name="nsa_cmp_finish_t",
        )(z.reshape(bsz * groups, nc, 2 * hidden), bias, w2.T.astype(BF16))
    return pl.pallas_call(
        functools.partial(_cmp_finish_kernel, hidden=hidden, rope=rope),
        grid=(bsz * groups,),
        in_specs=[pl.BlockSpec((1, nc, 2 * hidden), lambda i: (i, 0, 0)),
                  pl.BlockSpec((1, hidden), lambda i: (0, 0)),
                  pl.BlockSpec((hidden, dh), lambda i: (0, 0)),
                  pl.BlockSpec((nc, dh), lambda i: (0, 0)),
                  pl.BlockSpec((nc, dh), lambda i: (0, 0))],
        out_specs=pl.BlockSpec((1, nc, dh), lambda i: (i, 0, 0)),
        out_shape=jax.ShapeDtypeStruct((bsz * groups, nc, dh), BF16),
        compiler_params=_params("parallel"),
        name="nsa_cmp_finish",
    )(z.reshape(bsz * groups, nc, 2 * hidden), bias, w2.astype(BF16), cc, ss)


def _nsa_cmp_select_kernel(q_ref, kc_ref, vc_ref, ov_ref, oc_ref, sel_ref, *, tq, rep, dh, topn):
    qi = pl.program_id(2)
    kc = kc_ref[0]
    vc = vc_ref[0]
    nc = kc.shape[0]
    n_sel = sel_ref.shape[-1]
    t = qi * tq + lax.broadcasted_iota(jnp.int32, (tq, nc), 0)
    cmp_end = lax.broadcasted_iota(jnp.int32, (tq, nc), 1) * NSA_CMP_STRIDE + (NSA_CMP_BLOCK - 1)
    visible = cmp_end <= t
    psum = jnp.zeros((tq, nc), F32)
    for r in range(rep):
        s = jnp.where(visible, _dot_nt(q_ref[0, :, r * dh:(r + 1) * dh], kc), NEG_BIG)
        m = jnp.max(s, axis=-1, keepdims=True)
        e = jnp.where(visible, jnp.exp(s - m), 0.0)
        den = jnp.sum(e, axis=-1, keepdims=True)
        p = e / jnp.where(den > 0, den, 1.0)
        oc_ref[0, :, r * dh:(r + 1) * dh] = jnp.dot(p.astype(BF16), vc, preferred_element_type=F32)
        psum = psum + p
    imp = _dot_hi(psum, ov_ref[...])
    blk = lax.broadcasted_iota(jnp.int32, (tq, n_sel), 1)
    cur = (qi * tq + lax.broadcasted_iota(jnp.int32, (tq, n_sel), 0)) // NSA_SEL_BLOCK
    forced = (blk == 0) | (blk == cur) | (blk == cur - 1)
    imp = jnp.where(forced, NSA_FORCED_SCORE, imp)
    imp = jnp.where(blk > cur, -jnp.inf, imp)
    sel = jnp.zeros((tq, n_sel), F32)
    for _ in range(topn):
        m = jnp.max(imp, axis=-1, keepdims=True)
        first = jnp.min(jnp.where(imp == m, blk, n_sel), axis=-1, keepdims=True)
        hit = blk == first
        sel = jnp.where(hit, 1.0, sel)
        imp = jnp.where(hit, -jnp.inf, imp)
    sel_ref[0, 0] = sel


def _flash_step(q_scr, k, v, mask, m_ref, l_ref, acc_ref, rep, tq):
    kb = k.shape[0]
    s = _dot_nt(q_scr[...], k).reshape(rep, tq, kb)
    s = jnp.where(mask[None], s, NEG_BIG)
    m_old = m_ref[...].reshape(rep, tq, -1)[:, :, :1]
    m_new = jnp.maximum(m_old, jnp.max(s, axis=-1, keepdims=True))
    p = jnp.where(mask[None], jnp.exp(s - m_new), 0.0)
    alpha = jnp.exp(m_old - m_new)
    l_old = l_ref[...].reshape(rep, tq, -1)[:, :, :1]
    l_new = alpha * l_old + jnp.sum(p, axis=-1, keepdims=True)
    pv = jnp.dot(p.reshape(rep * tq, kb).astype(BF16), v, preferred_element_type=F32)
    acc_ref[...] = (alpha * acc_ref[...].reshape(rep, tq, -1)).reshape(rep * tq, -1) + pv
    m_ref[...] = jnp.broadcast_to(m_new, (rep, tq, m_ref.shape[-1])).reshape(m_ref.shape)
    l_ref[...] = jnp.broadcast_to(l_new, (rep, tq, l_ref.shape[-1])).reshape(l_ref.shape)


def _flash_init(q_ref, q_scr, m_ref, l_ref, acc_ref, rep, tq, dh):
    for r in range(rep):
        q_scr[r * tq:(r + 1) * tq, :] = q_ref[0, :, r * dh:(r + 1) * dh]
    m_ref[...] = jnp.full_like(m_ref, NEG_BIG)
    l_ref[...] = jnp.zeros_like(l_ref)
    acc_ref[...] = jnp.zeros_like(acc_ref)


def _flash_result(l_ref, acc_ref):
    l = l_ref[...][:, :1]
    return acc_ref[...] / jnp.where(l > 0, l, 1.0)


def _nsa_select_kernel(q_ref, k_ref, v_ref, sel_ref, o_ref, q_scr, m_ref, l_ref, acc_ref, *, tq, kb, rep, dh):
    qi = pl.program_id(2)
    kj = pl.program_id(3)

    @pl.when(kj == 0)
    def _():
        _flash_init(q_ref, q_scr, m_ref, l_ref, acc_ref, rep, tq, dh)

    @pl.when(kj * kb <= qi * tq + tq - 1)
    def _():
        sel = sel_ref[0, 0]
        blk = lax.broadcasted_iota(jnp.int32, sel.shape, 1)
        kpos = kj * kb + lax.broadcasted_iota(jnp.int32, (tq, kb), 1)
        t = qi * tq + lax.broadcasted_iota(jnp.int32, (tq, kb), 0)
        chosen = jnp.zeros((tq, kb), F32)
        for i in range(kb // NSA_SEL_BLOCK):
            col = jnp.sum(jnp.where(blk == kj * (kb // NSA_SEL_BLOCK) + i, sel, 0.0), axis=-1, keepdims=True)
            in_blk = (kpos - kj * kb) // NSA_SEL_BLOCK == i
            chosen = jnp.where(in_blk, col, chosen)
        mask = (chosen > 0) & (kpos <= t)
        _flash_step(q_scr, k_ref[0], v_ref[0].astype(BF16), mask, m_ref, l_ref, acc_ref, rep, tq)

    @pl.when(kj == pl.num_programs(3) - 1)
    def _():
        out = _flash_result(l_ref, acc_ref)
        for r in range(rep):
            o_ref[0, :, r * dh:(r + 1) * dh] = out[r * tq:(r + 1) * tq, :]


def _nsa_window_kernel(q_ref, k_ref, v_ref, oc_ref, os_ref, g_ref, o_ref, q_scr, m_ref, l_ref, acc_ref,
                       *, tq, kb, rep, dh, window, n_steps):
    qi = pl.program_id(2)
    w = pl.program_id(3)
    kblk = qi * (tq // kb) - (n_steps - tq // kb) + w

    @pl.when(w == 0)
    def _():
        _flash_init(q_ref, q_scr, m_ref, l_ref, acc_ref, rep, tq, dh)

    @pl.when(kblk >= 0)
    def _():
        kpos = kblk * kb + lax.broadcasted_iota(jnp.int32, (tq, kb), 1)
        t = qi * tq + lax.broadcasted_iota(jnp.int32, (tq, kb), 0)
        mask = (kpos <= t) & (kpos > t - window)
        _flash_step(q_scr, k_ref[0], v_ref[0].astype(BF16), mask, m_ref, l_ref, acc_ref, rep, tq)

    @pl.when(w == n_steps - 1)
    def _():
        out = _flash_result(l_ref, acc_ref)
        gates = g_ref[0, 0]
        for r in range(rep):
            sl = slice(r * dh, (r + 1) * dh)
            o = (gates[:, 3 * r:3 * r + 1] * oc_ref[0, :, sl] + gates[:, 3 * r + 1:3 * r + 2] * os_ref[0, :, sl]
                 + gates[:, 3 * r + 2:3 * r + 3] * out[r * tq:(r + 1) * tq, :])
            o_ref[0, :, sl] = o.astype(o_ref.dtype)


def nsa_mixer(u, h, w, bsz, s_len):
    t, d = u.shape
    dh, groups = NSA_HEAD_DIM, NSA_N_KV
    n_heads = d // dh
    rep = n_heads // groups
    kvw = groups * dh
    qw = n_heads * dh
    main_w = qw + 6 * kvw
    scale = dh ** -0.5
    tq = kb = min(128, s_len)
    nq = s_len // tq
    n_sel = s_len // NSA_SEL_BLOCK
    topn = min(NSA_TOPK, n_sel)
    w_in = w["nsa_w_in"]
    proj = matmul_ws(u, [(w_in, 0)], main_w, name="nsa_in").reshape(bsz, s_len, main_w)
    gates = matmul(u, [(w_in[:, main_w:].astype(BF16), 0)], w_in.shape[1] - main_w, epilogue=_ep_sigmoid,
                   name="nsa_gates")
    gates = jnp.transpose(gates.reshape(bsz, s_len, groups, rep * 3), (0, 2, 1, 3))
    slot = lambda j: (qw + j * kvw) // dh
    roped = nsa_rope(proj, n_heads, [slot(2) + g for g in range(groups)] + [slot(4) + g for g in range(groups)],
                     dh, scale)
    kc = nsa_compress(proj[..., qw:qw + kvw], w["nsa_cmp_pos_k"], w["nsa_cmp_k_w1"], w["nsa_cmp_k_w2"],
                      bsz, s_len, groups, dh, True)
    vc = nsa_compress(proj[..., qw + kvw:qw + 2 * kvw], w["nsa_cmp_pos_v"], w["nsa_cmp_v_w1"], w["nsa_cmp_v_w2"],
                      bsz, s_len, groups, dh, False)
    nc = kc.shape[1]
    cs = jnp.arange(nc)[:, None] * NSA_CMP_STRIDE
    ss = jnp.arange(n_sel)[None, :] * NSA_SEL_BLOCK
    overlap = jnp.clip(jnp.minimum(cs + NSA_CMP_BLOCK, ss + NSA_SEL_BLOCK) - jnp.maximum(cs, ss), 0, None)
    overlap = overlap.astype(F32) / NSA_CMP_BLOCK

    q_spec3 = pl.BlockSpec((1, tq, rep * dh), lambda b_, g, i: (b_, i, g))
    o_c, sel = pl.pallas_call(
        functools.partial(_nsa_cmp_select_kernel, tq=tq, rep=rep, dh=dh, topn=topn),
        grid=(bsz, groups, nq),
        in_specs=[q_spec3,
                  pl.BlockSpec((1, nc, dh), lambda b_, g, i: (b_ * groups + g, 0, 0)),
                  pl.BlockSpec((1, nc, dh), lambda b_, g, i: (b_ * groups + g, 0, 0)),
                  pl.BlockSpec((nc, n_sel), lambda b_, g, i: (0, 0))],
        out_specs=[q_spec3, pl.BlockSpec((1, 1, tq, n_sel), lambda b_, g, i: (b_, g, i, 0))],
        out_shape=[jax.ShapeDtypeStruct((bsz, s_len, qw), F32),
                   jax.ShapeDtypeStruct((bsz, groups, s_len, n_sel), F32)],
        compiler_params=_params("parallel", "parallel", "parallel"),
        name="nsa_cmp_select",
    )(roped, kc, vc, overlap)

    q_spec = pl.BlockSpec((1, tq, rep * dh), lambda b_, g, i, j: (b_, i, g))
    flash_scratch = [pltpu.VMEM((rep * tq, dh), BF16), pltpu.VMEM((rep * tq, LANES), F32),
                     pltpu.VMEM((rep * tq, LANES), F32), pltpu.VMEM((rep * tq, dh), F32)]
    last_kb = lambda i: (i * tq + tq - 1) // kb
    o_s = pl.pallas_call(
        functools.partial(_nsa_select_kernel, tq=tq, kb=kb, rep=rep, dh=dh),
        grid=(bsz, groups, nq, s_len // kb),
        in_specs=[q_spec,
                  pl.BlockSpec((1, kb, dh), lambda b_, g, i, j: (b_, jnp.minimum(j, last_kb(i)), n_heads + g)),
                  pl.BlockSpec((1, kb, dh), lambda b_, g, i, j: (b_, jnp.minimum(j, last_kb(i)), slot(3) + g)),
                  pl.BlockSpec((1, 1, tq, n_sel), lambda b_, g, i, j: (b_, g, i, 0))],
        out_specs=q_spec,
        out_shape=jax.ShapeDtypeStruct((bsz, s_len, qw), F32),
        scratch_shapes=flash_scratch,
        compiler_params=_params("parallel", "parallel", "parallel", "arbitrary"),
        name="nsa_select_attn",
    )(roped, roped, proj, sel)

    n_steps = NSA_WINDOW // kb + tq // kb
    win_blk = lambda i, j: jnp.maximum(i * (tq // kb) - (n_steps - tq // kb) + j, 0)
    o = pl.pallas_call(
        functools.partial(_nsa_window_kernel, tq=tq, kb=kb, rep=rep, dh=dh, window=NSA_WINDOW, n_steps=n_steps),
        grid=(bsz, groups, nq, n_steps),
        in_specs=[q_spec,
                  pl.BlockSpec((1, kb, dh), lambda b_, g, i, j: (b_, win_blk(i, j), n_heads + groups + g)),
                  pl.BlockSpec((1, kb, dh), lambda b_, g, i, j: (b_, win_blk(i, j), slot(5) + g)),
                  q_spec, q_spec,
                  pl.BlockSpec((1, 1, tq, rep * 3), lambda b_, g, i, j: (b_, g, i, 0))],
        out_specs=q_spec,
        out_shape=jax.ShapeDtypeStruct((bsz, s_len, qw), BF16),
        scratch_shapes=flash_scratch,
        compiler_params=_params("parallel", "parallel", "parallel", "arbitrary"),
        name="nsa_window_attn",
    )(roped, roped, proj, o_c, o_s, gates)
    return matmul_ws(o.reshape(t, qw), [(w["nsa_w_out"], 0)], d, epilogue=_ep_residual, extras=[(h, "mn")],
                     name="nsa_out")


def _rope_t_kernel(x_ref, cc_ref, ss_ref, o_ref, *, n_rope, scale, group, dh):
    first_slot = pl.program_id(2) * group
    for i in range(group):
        x = x_ref[0, :, i * dh:(i + 1) * dh]
        roped = (x * cc_ref[...] + pltpu.roll(x, dh // 2, 1) * ss_ref[...]) * scale
        out = jnp.where(first_slot + i < n_rope, roped, x)
        o_ref[0, i * dh:(i + 1) * dh, :] = out.T.astype(o_ref.dtype)


def nsa_rope_t(proj, slots, n_rope, dh, scale, tb=512, group=4):
    bsz, s_len, _ = proj.shape
    tb = min(tb, s_len)
    cc, ss = _rope_tables(jnp.arange(s_len), dh)
    assert len(slots) % group == 0
    firsts = slots[::group]
    assert all(f % group == 0 and slots[i * group:(i + 1) * group] == list(range(f, f + group))
               for i, f in enumerate(firsts))
    table = jnp.asarray([f // group for f in firsts], jnp.int32)
    grid_spec = pltpu.PrefetchScalarGridSpec(
        num_scalar_prefetch=1,
        grid=(bsz, s_len // tb, len(firsts)),
        in_specs=[pl.BlockSpec((1, tb, group * dh), lambda b_, t, j, tab: (b_, t, tab[j])),
                  pl.BlockSpec((tb, dh), lambda b_, t, j, tab: (t, 0)),
                  pl.BlockSpec((tb, dh), lambda b_, t, j, tab: (t, 0))],
        out_specs=pl.BlockSpec((1, group * dh, tb), lambda b_, t, j, tab: (b_, j, t)),
    )
    kern = lambda tab, x_ref, cc_ref, ss_ref, o_ref: _rope_t_kernel(x_ref, cc_ref, ss_ref, o_ref, n_rope=n_rope,
                                                                   scale=scale, group=group, dh=dh)
    return pl.pallas_call(
        kern,
        grid_spec=grid_spec,
        out_shape=jax.ShapeDtypeStruct((bsz, len(slots) * dh, s_len), BF16),
        compiler_params=_params("parallel", "parallel", "arbitrary"),
        name="nsa_rope_t",
    )(table, proj, cc, ss)


def _cmp_finish_t_kernel(z_ref, bias_ref, w2_ref, o_ref, *, hidden):
    z = z_ref[0]
    nc = z.shape[0]
    nxt = pltpu.roll(z[:, hidden:], nc - 1, 0)
    hid = _silu(z[:, :hidden] + nxt + bias_ref[...])
    o_ref[0] = _dot_nt(w2_ref[...], hid.astype(BF16)).astype(o_ref.dtype)


def _nsa_cmp_select_t_kernel(q_ref, kc_ref, vc_ref, ov_ref, oc_ref, sel_ref, *, tq, rep, dh, topn):
    qi = pl.program_id(2)
    kc = kc_ref[0]
    vct = vc_ref[0]
    nc = kc.shape[0]
    n_sel = sel_ref.shape[2]
    t = qi * tq + lax.broadcasted_iota(jnp.int32, (nc, tq), 1)
    cmp_end = lax.broadcasted_iota(jnp.int32, (nc, tq), 0) * NSA_CMP_STRIDE + (NSA_CMP_BLOCK - 1)
    visible = cmp_end <= t
    s = [jnp.where(visible, jnp.dot(kc, q_ref[0, r * dh:(r + 1) * dh, :], preferred_element_type=F32), NEG_BIG)
         for r in range(rep)]
    e = [jnp.where(visible, jnp.exp2(x - jnp.max(x, axis=0, keepdims=True)), 0.0) for x in s]
    den = [jnp.sum(x, axis=0, keepdims=True) for x in e]
    p = [e[r] / jnp.where(den[r] > 0, den[r], 1.0) for r in range(rep)]
    for r in range(rep):
        oc_ref[0, r * dh:(r + 1) * dh, :] = jnp.dot(vct, p[r].astype(BF16), preferred_element_type=F32)
    psum = p[0]
    for r in range(1, rep):
        psum = psum + p[r]
    imp = _dot_hi(ov_ref[...], psum)
    blk = lax.broadcasted_iota(jnp.int32, (n_sel, tq), 0)
    cur = (qi * tq + lax.broadcasted_iota(jnp.int32, (n_sel, tq), 1)) // NSA_SEL_BLOCK
    forced = (blk == 0) | (blk == cur) | (blk == cur - 1)
    imp = jnp.where(forced, NSA_FORCED_SCORE, imp)
    imp = jnp.where(blk > cur, -jnp.inf, imp)
    sel = jnp.zeros((n_sel, tq), F32)
    for _ in range(topn):
        m = jnp.max(imp, axis=0, keepdims=True)
        first = jnp.min(jnp.where(imp == m, blk, n_sel), axis=0, keepdims=True)
        hit = blk == first
        sel = jnp.where(hit, 1.0, sel)
        imp = jnp.where(hit, -jnp.inf, imp)
    sel_ref[0, 0] = sel


def _flash_t_init(m_ref, l_ref, acc_ref):
    m_ref[...] = jnp.full_like(m_ref, NEG_BIG)
    l_ref[...] = jnp.zeros_like(l_ref)
    acc_ref[...] = jnp.zeros_like(acc_ref)


def _flash_t_step(q_ref, k, vt, mask, m_ref, l_ref, acc_ref, rep, dh):
    hs = range(rep)
    s = [jnp.where(mask, jnp.dot(k, q_ref[0, r * dh:(r + 1) * dh, :], preferred_element_type=F32), NEG_BIG)
         for r in hs]
    m_old = [m_ref[r] for r in hs]
    m_new = [jnp.maximum(m_old[r], jnp.max(s[r], axis=0, keepdims=True)) for r in hs]
    p = [jnp.exp2(s[r] - m_new[r]).astype(BF16) for r in hs]
    alpha = [jnp.exp2(m_old[r] - m_new[r]) for r in hs]
    pv = [jnp.dot(vt, p[r], preferred_element_type=F32) for r in hs]
    ones = jnp.ones((8, k.shape[0]), BF16)
    psum = [jnp.dot(ones, p[r], preferred_element_type=F32)[0:1] for r in hs]
    for r in hs:
        m_ref[r] = m_new[r]
        l_ref[r] = alpha[r] * l_ref[r] + psum[r]
        acc_ref[r] = acc_ref[r] * alpha[r] + pv[r]


def _nsa_select_t_kernel(qi_ref, kj_ref, q_ref, k_ref, vt_ref, sel_ref, o_ref, m_ref, l_ref, acc_ref,
                         *, tq, kb, rep, dh):
    pair = pl.program_id(2)
    qi = qi_ref[pair]
    kj = kj_ref[pair]

    @pl.when(kj == 0)
    def _():
        _flash_t_init(m_ref, l_ref, acc_ref)

    kpos = kj * kb + lax.broadcasted_iota(jnp.int32, (kb, tq), 0)
    t = qi * tq + lax.broadcasted_iota(jnp.int32, (kb, tq), 1)
    per = kb // NSA_SEL_BLOCK
    chosen = jnp.zeros((kb, tq), F32)
    for i in range(per):
        row = sel_ref[0, 0, pl.ds(kj * per + i, 1), :]
        chosen = jnp.where((kpos - kj * kb) // NSA_SEL_BLOCK == i, row, chosen)
    mask = (chosen > 0) & (kpos <= t)
    _flash_t_step(q_ref, k_ref[0], vt_ref[0], mask, m_ref, l_ref, acc_ref, rep, dh)

    @pl.when(kj * kb + kb > qi * tq + tq - 1)
    def _():
        for r in range(rep):
            l = l_ref[r]
            o_ref[0, r * dh:(r + 1) * dh, :] = acc_ref[r] / jnp.where(l > 0, l, 1.0)


def _nsa_window_t_kernel(q_ref, k_ref, vt_ref, oc_ref, os_ref, g_ref, o_ref, m_ref, l_ref, acc_ref,
                         *, tq, kb, rep, dh, window, n_steps):
    qi = pl.program_id(2)
    w = pl.program_id(3)
    kblk = qi * (tq // kb) - (n_steps - tq // kb) + w

    @pl.when(w == 0)
    def _():
        _flash_t_init(m_ref, l_ref, acc_ref)

    @pl.when(kblk >= 0)
    def _():
        kpos = kblk * kb + lax.broadcasted_iota(jnp.int32, (kb, tq), 0)
        t = qi * tq + lax.broadcasted_iota(jnp.int32, (kb, tq), 1)
        mask = (kpos <= t) & (kpos > t - window)
        _flash_t_step(q_ref, k_ref[0], vt_ref[0], mask, m_ref, l_ref, acc_ref, rep, dh)

    @pl.when(w == n_steps - 1)
    def _():
        gates = g_ref[0, 0]
        for r in range(rep):
            rows = slice(r * dh, (r + 1) * dh)
            l = l_ref[r]
            o_w = acc_ref[r] / jnp.where(l > 0, l, 1.0)
            o = (gates[3 * r:3 * r + 1, :] * oc_ref[0, rows, :] + gates[3 * r + 1:3 * r + 2, :] * os_ref[0, rows, :]
                 + gates[3 * r + 2:3 * r + 3, :] * o_w)
            o_ref[0, :, rows] = o.T.astype(o_ref.dtype)


def nsa_mixer_t(u, h, w, bsz, s_len):
    t, d = u.shape
    dh, groups = NSA_HEAD_DIM, NSA_N_KV
    n_heads = d // dh
    rep = n_heads // groups
    kvw = groups * dh
    qw = n_heads * dh
    main_w = qw + 6 * kvw
    scale = dh ** -0.5
    tq = kb = min(128, s_len)
    nq = s_len // tq
    n_sel = s_len // NSA_SEL_BLOCK
    topn = min(NSA_TOPK, n_sel)
    w_in = w["nsa_w_in"]
    proj = matmul_ws(u, [(w_in, 0)], main_w, name="nsa_in").reshape(bsz, s_len, main_w)
    gates = matmul(u, [(w_in[:, main_w:].astype(BF16), 0)], w_in.shape[1] - main_w, epilogue=_ep_sigmoid,
                   name="nsa_gates")
    gates = jnp.transpose(gates.reshape(bsz, s_len, groups, rep * 3), (0, 2, 3, 1))
    slot = lambda j: (qw + j * kvw) // dh
    qvt = nsa_rope_t(proj, list(range(n_heads)) + [slot(3) + g for g in range(groups)]
                     + [slot(5) + g for g in range(groups)], n_heads, dh, scale * math.log2(math.e), group=groups)
    k_rot = nsa_rope(proj, 0, [slot(2) + g for g in range(groups)] + [slot(4) + g for g in range(groups)], dh, 1.0)
    kc = nsa_compress(proj[..., qw:qw + kvw], w["nsa_cmp_pos_k"], w["nsa_cmp_k_w1"], w["nsa_cmp_k_w2"],
                      bsz, s_len, groups, dh, True)
    vct = nsa_compress(proj[..., qw + kvw:qw + 2 * kvw], w["nsa_cmp_pos_v"], w["nsa_cmp_v_w1"], w["nsa_cmp_v_w2"],
                       bsz, s_len, groups, dh, False, transpose_out=True)
    nc = kc.shape[1]
    cs = jnp.arange(nc)[None, :] * NSA_CMP_STRIDE
    ss = jnp.arange(n_sel)[:, None] * NSA_SEL_BLOCK
    overlap_t = jnp.clip(jnp.minimum(cs + NSA_CMP_BLOCK, ss + NSA_SEL_BLOCK) - jnp.maximum(cs, ss), 0, None)
    overlap_t = overlap_t.astype(F32) / NSA_CMP_BLOCK

    qt_spec3 = pl.BlockSpec((1, rep * dh, tq), lambda b_, g, i: (b_, g, i))
    o_c, sel = pl.pallas_call(
        functools.partial(_nsa_cmp_select_t_kernel, tq=tq, rep=rep, dh=dh, topn=topn),
        grid=(bsz, groups, nq),
        in_specs=[qt_spec3,
                  pl.BlockSpec((1, nc, dh), lambda b_, g, i: (b_ * groups + g, 0, 0)),
                  pl.BlockSpec((1, dh, nc), lambda b_, g, i: (b_ * groups + g, 0, 0)),
                  pl.BlockSpec((n_sel, nc), lambda b_, g, i: (0, 0))],
        out_specs=[qt_spec3, pl.BlockSpec((1, 1, n_sel, tq), lambda b_, g, i: (b_, g, 0, i))],
        out_shape=[jax.ShapeDtypeStruct((bsz, qw, s_len), F32),
                   jax.ShapeDtypeStruct((bsz, groups, n_sel, s_len), F32)],
        compiler_params=_params("parallel", "parallel", "parallel"),
        name="nsa_cmp_select",
    )(qvt, kc, vct, overlap_t)

    flash_scratch = lambda n: [pltpu.VMEM((rep, 1, n), F32), pltpu.VMEM((rep, 1, n), F32),
                               pltpu.VMEM((rep, dh, n), F32)]
    tqs = min(2 * tq, s_len)
    kb = tqs
    pairs = [(i, j) for i in range(s_len // tqs) for j in range((i * tqs + tqs - 1) // kb + 1)]
    qi_of = jnp.asarray([pr[0] for pr in pairs], jnp.int32)
    kj_of = jnp.asarray([pr[1] for pr in pairs], jnp.int32)
    o_s = pl.pallas_call(
        functools.partial(_nsa_select_t_kernel, tq=tqs, kb=kb, rep=rep, dh=dh),
        grid_spec=pltpu.PrefetchScalarGridSpec(
            num_scalar_prefetch=2,
            grid=(bsz, groups, len(pairs)),
            in_specs=[pl.BlockSpec((1, rep * dh, tqs), lambda b_, g, pr, qi, kj: (b_, g, qi[pr])),
                      pl.BlockSpec((1, kb, dh), lambda b_, g, pr, qi, kj: (b_, kj[pr], g)),
                      pl.BlockSpec((1, dh, kb), lambda b_, g, pr, qi, kj: (b_, n_heads + g, kj[pr])),
                      pl.BlockSpec((1, 1, n_sel, tqs), lambda b_, g, pr, qi, kj: (b_, g, 0, qi[pr]))],
            out_specs=pl.BlockSpec((1, rep * dh, tqs), lambda b_, g, pr, qi, kj: (b_, g, qi[pr])),
            scratch_shapes=flash_scratch(tqs)),
        out_shape=jax.ShapeDtypeStruct((bsz, qw, s_len), F32),
        compiler_params=_params("parallel", "parallel", "arbitrary"),
        name="nsa_select_attn",
    )(qi_of, kj_of, qvt, k_rot, qvt, sel)

    n_steps = -(-NSA_WINDOW // kb) + tqs // kb
    win_blk = lambda i, j: jnp.maximum(i * (tqs // kb) - (n_steps - tqs // kb) + j, 0)
    qt_spec = pl.BlockSpec((1, rep * dh, tqs), lambda b_, g, i, j: (b_, g, i))
    o = pl.pallas_call(
        functools.partial(_nsa_window_t_kernel, tq=tqs, kb=kb, rep=rep, dh=dh, window=NSA_WINDOW, n_steps=n_steps),
        grid=(bsz, groups, s_len // tqs, n_steps),
        in_specs=[qt_spec,
                  pl.BlockSpec((1, kb, dh), lambda b_, g, i, j: (b_, win_blk(i, j), groups + g)),
                  pl.BlockSpec((1, dh, kb), lambda b_, g, i, j: (b_, n_heads + groups + g, win_blk(i, j))),
                  qt_spec, qt_spec,
                  pl.BlockSpec((1, 1, rep * 3, tqs), lambda b_, g, i, j: (b_, g, 0, i))],
        out_specs=pl.BlockSpec((1, tqs, rep * dh), lambda b_, g, i, j: (b_, i, g)),
        out_shape=jax.ShapeDtypeStruct((bsz, s_len, qw), BF16),
        scratch_shapes=flash_scratch(tqs),
        compiler_params=_params("parallel", "parallel", "parallel", "arbitrary"),
        name="nsa_window_attn",
    )(qvt, k_rot, qvt, o_c, o_s, gates)
    return matmul_ws(o.reshape(t, qw), [(w["nsa_w_out"], 0)], d, epilogue=_ep_residual, extras=[(h, "mn")],
                     name="nsa_out")


_MATMUL_WEIGHTS = ("pl_proj", "rw_w1", "rw_w2", "rw_a1", "rw_a2", "rw_g1", "rw_g2")


def kernel(x, p, norm_mix, norm_ffn, norm_pl, pl_proj, pl_gate, norm_final, mb_w_in, mb_conv_w, mb_conv_b, mb_dt_bias, mb_a_log, mb_d_skip, mb_norm_w, mb_w_out, nsa_w_in, nsa_cmp_pos_k, nsa_cmp_pos_v, nsa_cmp_k_w1, nsa_cmp_k_w2, nsa_cmp_v_w1, nsa_cmp_v_w2, nsa_w_out, hg_w_in, hg_lb_logits, hg_norm_w, hg_w_out, rw_mu, rw_w_rkv, rw_w0, rw_w1, rw_w2, rw_a0, rw_a1, rw_a2, rw_g1, rw_g2, rw_k_k, rw_k_a, rw_r_k, rw_ln_w, rw_ln_b, rw_w_out, ffn0_w_in, ffn0_w_out, moe1_router, moe1_w_in, moe1_w_out, ffn2_w_in, ffn2_w_out, moe3_router, moe3_w_in, moe3_w_out):
    w = dict(locals())
    for name in _MATMUL_WEIGHTS:
        w[name] = w[name].astype(BF16)
    bsz, s_len, d = x.shape
    depth = p.shape[0]
    t = bsz * s_len
    lb_all = jax.nn.softmax(hg_lb_logits.astype(F32), axis=0)
    lb_all = jnp.cumsum(lb_all, axis=0) - lb_all[0]
    dense = [(w["ffn0_w_in"], w["ffn0_w_out"]), (w["ffn2_w_in"], w["ffn2_w_out"])]
    moe = [(moe1_router, w["moe1_w_in"], w["moe1_w_out"]), (moe3_router, w["moe3_w_in"], w["moe3_w_out"])]
    p_bf = p.reshape(depth, t, p.shape[-1])
    h = x.reshape(t, d)
    for i in range(depth):
        kind = i % 4
        if kind == 0:
            h = mamba2_mixer(rmsnorm(h, norm_mix[i]), h, w, bsz, s_len)
        elif kind == 1:
            h = nsa_mixer_t(rmsnorm(h, norm_mix[i]), h, w, bsz, s_len)
        elif kind == 2:
            h = hgrn2_mixer(rmsnorm(h, norm_mix[i]), h, w, lb_all[i], bsz, s_len)
        else:
            h = rwkv7_mixer(rmsnorm(h, norm_mix[i], out_dtype=F32), h, w, bsz, s_len)
        v = rmsnorm(h, norm_ffn[i])
        if i % 2 == 0:
            h = dense_ffn(v, h, *dense[i // 2])
        else:
            h = moe_ffn_routed(v, h, *moe[i // 2])
        h = ple_gate(h, p_bf[i], norm_pl[i], w["pl_proj"][i], pl_gate, i)
    return rmsnorm(h, norm_final, out_dtype=F32).reshape(bsz, s_len, d)
```

```python
import functools
import math

import jax
import jax.numpy as jnp
from jax import lax
from jax.experimental import pallas as pl
from jax.experimental.pallas import tpu as pltpu

F32 = jnp.float32
BF16 = jnp.bfloat16

NORM_EPS = 1e-6
ROPE_THETA = 10000.0

V7X_VMEM_BYTES = 64 * 1024 * 1024
VMEM_LIMIT_BYTES = V7X_VMEM_BYTES - 8 * 1024 * 1024
LANES = 128

MB_D_STATE = 128
MB_CHUNK = 128

NSA_HEAD_DIM = 128
NSA_N_KV = 4
NSA_CMP_BLOCK = 32
NSA_CMP_STRIDE = 16
NSA_SEL_BLOCK = 64
NSA_TOPK = 16
NSA_WINDOW = 512
NSA_FORCED_SCORE = 1e9

HG_HEAD_DIM = 128
HG_CHUNK = 32

RW_HEAD_DIM = 64
RW_LN_EPS = 64e-5
RW_CHUNK = 128

MOE_TOPK = 2


def _params(*semantics):
    return pltpu.CompilerParams(dimension_semantics=semantics, vmem_limit_bytes=VMEM_LIMIT_BYTES)


def _pick(n, target):
    if n <= target:
        return n
    for c in range(target, 0, -1):
        if n % c == 0:
            return c
    return n


def _silu(x):
    return x * jax.nn.sigmoid(x)


def _rmsnorm_kernel(x_ref, g_ref, o_ref):
    x = x_ref[...]
    ms = jnp.mean(x * x, axis=-1, keepdims=True)
    o_ref[...] = (x * lax.rsqrt(ms + NORM_EPS) * g_ref[...]).astype(o_ref.dtype)


def rmsnorm(x, gain, out_dtype=BF16, name="rmsnorm"):
    m, d = x.shape
    bm = _pick(m, 256)
    return pl.pallas_call(
        _rmsnorm_kernel,
        grid=(m // bm,),
        in_specs=[pl.BlockSpec((bm, d), lambda i: (i, 0)), pl.BlockSpec((1, d), lambda i: (0, 0))],
        out_specs=pl.BlockSpec((bm, d), lambda i: (i, 0)),
        out_shape=jax.ShapeDtypeStruct((m, d), out_dtype),
        compiler_params=_params("parallel"),
        name=name,
    )(x, gain.reshape(1, d).astype(F32))


def _mm_kernel(*refs, n_w, n_extra, nk, epilogue):
    x_ref = refs[0]
    w_refs = refs[1:1 + n_w]
    e_refs = refs[1 + n_w:1 + n_w + n_extra]
    o_ref = refs[1 + n_w + n_extra]
    acc_refs = refs[2 + n_w + n_extra:]
    x = x_ref[...]
    if nk == 1:
        accs = [jnp.dot(x, w[...], preferred_element_type=F32) for w in w_refs]
        o_ref[...] = epilogue(accs, [e[...] for e in e_refs]).astype(o_ref.dtype)
        return
    k = pl.program_id(2)

    @pl.when(k == 0)
    def _():
        for a in acc_refs:
            a[...] = jnp.zeros_like(a)

    for a, w in zip(acc_refs, w_refs):
        a[...] += jnp.dot(x, w[...], preferred_element_type=F32)

    @pl.when(k == nk - 1)
    def _():
        o_ref[...] = epilogue([a[...] for a in acc_refs], [e[...] for e in e_refs]).astype(o_ref.dtype)


def _first(accs, extras):
    return accs[0]


def matmul(x, ws, n_out, *, epilogue=_first, extras=(), out_dtype=F32, bm=1024, bn=512, bk=None, name="matmul"):
    m, kdim = x.shape
    bm = _pick(m, bm)
    bn = _pick(n_out, bn)
    if bk is None:
        bk = kdim if kdim <= 4096 else _pick(kdim, 4096)
    nk = kdim // bk
    assert kdim % bk == 0 and m % bm == 0 and n_out % bn == 0
    in_specs = [pl.BlockSpec((bm, bk), lambda i, j, k: (i, k))]
    args = [x]
    for w, off in ws:
        assert off % bn == 0 and w.shape[0] == kdim
        in_specs.append(pl.BlockSpec((bk, bn), functools.partial(lambda i, j, k, o: (k, j + o), o=off // bn)))
        args.append(w)
    for arr, kind in extras:
        if kind == "mn":
            in_specs.append(pl.BlockSpec((bm, bn), lambda i, j, k: (i, j)))
        elif kind == "m":
            in_specs.append(pl.BlockSpec((bm, arr.shape[1]), lambda i, j, k: (i, 0)))
        elif kind == "kn":
            in_specs.append(pl.BlockSpec((arr.shape[0], bn), lambda i, j, k: (0, j)))
        else:
            in_specs.append(pl.BlockSpec((1, bn), lambda i, j, k: (0, j)))
        args.append(arr)
    scratch = [pltpu.VMEM((bm, bn), F32) for _ in ws] if nk > 1 else []
    kern = functools.partial(_mm_kernel, n_w=len(ws), n_extra=len(extras), nk=nk, epilogue=epilogue)
    return pl.pallas_call(
        kern,
        grid=(m // bm, n_out // bn, nk),
        in_specs=in_specs,
        out_specs=pl.BlockSpec((bm, bn), lambda i, j, k: (i, j)),
        out_shape=jax.ShapeDtypeStruct((m, n_out), out_dtype),
        scratch_shapes=scratch,
        compiler_params=_params("parallel", "parallel", "arbitrary"),
        name=name,
    )(*args)


WS_CAST_CHUNK = 512


def _mm_ws_kernel(*refs, n_w, n_extra, epilogue):
    x_ref = refs[0]
    w_refs = refs[1:1 + n_w]
    e_refs = refs[1 + n_w:1 + n_w + n_extra]
    o_ref = refs[1 + n_w + n_extra]
    wb_refs = refs[2 + n_w + n_extra:]

    kdim = x_ref.shape[1]
    ck = _pick(kdim, WS_CAST_CHUNK)

    @pl.when(pl.program_id(1) == 0)
    def _():
        accs = [None] * n_w
        for c in range(kdim // ck):
            rows = slice(c * ck, (c + 1) * ck)
            xc = x_ref[:, rows]
            for n, (w, wb) in enumerate(zip(w_refs, wb_refs)):
                wc = (w[0, rows, :] if len(w.shape) == 3 else w[rows, :]).astype(BF16)
                wb[rows, :] = wc
                part = jnp.dot(xc, wc, preferred_element_type=F32)
                accs[n] = part if accs[n] is None else accs[n] + part
        o_ref[...] = epilogue(accs, [e[...] for e in e_refs]).astype(o_ref.dtype)

    @pl.when(pl.program_id(1) != 0)
    def _():
        x = x_ref[...]
        accs = [jnp.dot(x, wb[...], preferred_element_type=F32) for wb in wb_refs]
        o_ref[...] = epilogue(accs, [e[...] for e in e_refs]).astype(o_ref.dtype)


def matmul_ws(x, ws, n_out, *, epilogue=_first, extras=(), out_dtype=F32, bm=1024, bn=512, w_buffers=2,
              name="matmul_ws"):
    m, kdim = x.shape
    bm = _pick(m, bm)
    bn = _pick(n_out, bn)
    assert m % bm == 0 and n_out % bn == 0
    mode = {} if w_buffers == 2 else {"pipeline_mode": pl.Buffered(w_buffers)}
    in_specs = [pl.BlockSpec((bm, kdim), lambda j, i: (i, 0))]
    args = [x]
    for w, off in ws:
        if w.ndim == 3:
            e, o = off
            assert o % bn == 0 and w.shape[1] == kdim
            in_specs.append(pl.BlockSpec((1, kdim, bn), functools.partial(lambda j, i, e_, o_: (e_, 0, j + o_),
                                                                          e_=e, o_=o // bn), **mode))
        else:
            assert off % bn == 0 and w.shape[0] == kdim
            in_specs.append(pl.BlockSpec((kdim, bn), functools.partial(lambda j, i, o_: (0, j + o_), o_=off // bn),
                                         **mode))
        args.append(w)
    for arr, kind in extras:
        if kind == "mn":
            in_specs.append(pl.BlockSpec((bm, bn), lambda j, i: (i, j)))
        elif kind == "m":
            in_specs.append(pl.BlockSpec((bm, arr.shape[1]), lambda j, i: (i, 0)))
        elif kind == "kn":
            in_specs.append(pl.BlockSpec((arr.shape[0], bn), lambda j, i: (0, j)))
        else:
            in_specs.append(pl.BlockSpec((1, bn), lambda j, i: (0, j)))
        args.append(arr)
    kern = functools.partial(_mm_ws_kernel, n_w=len(ws), n_extra=len(extras), epilogue=epilogue)
    return pl.pallas_call(
        kern,
        grid=(n_out // bn, m // bm),
        in_specs=in_specs,
        out_specs=pl.BlockSpec((bm, bn), lambda j, i: (i, j)),
        out_shape=jax.ShapeDtypeStruct((m, n_out), out_dtype),
        scratch_shapes=[pltpu.VMEM((kdim, bn), BF16) for _ in ws],
        compiler_params=_params("parallel", "arbitrary"),
        name=name,
    )(*args)


def _ep_residual(accs, extras):
    return extras[0] + accs[0]


def _ep_swiglu(accs, extras):
    return _silu(accs[0]) * accs[1]


def _ep_tanh(accs, extras):
    return jnp.tanh(accs[0])


def _ep_sigmoid(accs, extras):
    return jax.nn.sigmoid(accs[0])


def _ep_bias_sigmoid(accs, extras):
    return jax.nn.sigmoid(accs[0] + extras[0])


def _ep_rw_logdecay(accs, extras):
    w = -jax.nn.softplus(-(accs[0] + extras[0])) - 0.5
    return -jnp.exp(w)


def _ep_ple_gate(accs, extras):
    pp = jnp.dot(extras[1].astype(BF16), extras[2], preferred_element_type=F32)
    return extras[0] + pp * jax.nn.sigmoid(accs[0])


def _conv_silu_kernel(x_ref, w_ref, b_ref, o_ref, *, k_width):
    x = x_ref[0]
    row = lax.broadcasted_iota(jnp.int32, x.shape, 0)
    y = b_ref[...] + w_ref[k_width - 1:k_width, :] * x
    for j in range(k_width - 1):
        shift = k_width - 1 - j
        xs = jnp.where(row >= shift, pltpu.roll(x, shift, 0), 0.0)
        y = y + w_ref[j:j + 1, :] * xs
    o_ref[0] = _silu(y)


def conv_silu(x, w, b):
    bsz, s_len, c = x.shape
    cb = _pick(c, 256)
    k_width = w.shape[0]
    return pl.pallas_call(
        functools.partial(_conv_silu_kernel, k_width=k_width),
        grid=(bsz, c // cb),
        in_specs=[pl.BlockSpec((1, s_len, cb), lambda b_, j: (b_, 0, j)),
                  pl.BlockSpec((k_width, cb), lambda b_, j: (0, j)),
                  pl.BlockSpec((1, cb), lambda b_, j: (0, j))],
        out_specs=pl.BlockSpec((1, s_len, cb), lambda b_, j: (b_, 0, j)),
        out_shape=jax.ShapeDtypeStruct(x.shape, F32),
        compiler_params=_params("parallel", "parallel"),
        name="mamba_conv_silu",
    )(x, w, b.reshape(1, c))


def _cumsum_rows(x, n):
    row = lax.broadcasted_iota(jnp.int32, x.shape, 0)
    s = 1
    while s < n:
        x = x + jnp.where(row >= s, pltpu.roll(x, s, 0), 0.0)
        s *= 2
    return x


def _cumsum_lanes(x, n):
    col = lax.broadcasted_iota(jnp.int32, x.shape, 1)
    s = 1
    while s < n:
        x = x + jnp.where(col >= s, pltpu.roll(x, s, 1), 0.0)
        s *= 2
    return x


def _dot_nt(a, b):
    return lax.dot_general(a, b, (((1,), (1,)), ((), ())), preferred_element_type=F32)


def _dot_tn(a, b):
    return lax.dot_general(a, b, (((0,), (0,)), ((), ())), preferred_element_type=F32)


def _ssd_kernel(xs_ref, b_ref, c_ref, z_ref, dt_ref, dtt_ref, bias_r_ref, bias_c_ref, alog_r_ref, alog_c_ref,
                dskip_ref, normw_ref, o_ref, state_ref, y_ref, *, chunk, heads, p_dim):
    @pl.when(pl.program_id(2) == 0)
    def _():
        state_ref[...] = jnp.zeros_like(state_ref)

    dt = jax.nn.softplus(dt_ref[0, 0] + bias_r_ref[0])
    dtt = jax.nn.softplus(dtt_ref[0, 0] + bias_c_ref[0])
    a_cum = _cumsum_rows(dt * -jnp.exp(alog_r_ref[0]), chunk)
    a_cum_t = _cumsum_lanes(dtt * -jnp.exp(alog_c_ref[0]), chunk)
    xs = xs_ref[0]
    bmat = b_ref[0]
    cmat = c_ref[0].astype(BF16)
    cb = _dot_nt(cmat, bmat.astype(BF16))
    b_t = bmat.T.astype(BF16)
    li = lax.broadcasted_iota(jnp.int32, (chunk, chunk), 0)
    si = lax.broadcasted_iota(jnp.int32, (chunk, chunk), 1)
    causal = li >= si
    per = LANES // p_dim
    lane_seg = lax.broadcasted_iota(jnp.int32, (1, LANES), 1) // p_dim

    def pick(vals):
        out = vals[-1]
        for i in range(per - 2, -1, -1):
            out = jnp.where(lane_seg == i, vals[i], out)
        return out

    dot = functools.partial(jnp.dot, preferred_element_type=F32)
    es = range(heads)
    tiles = range(heads // per)
    head_row = lax.broadcasted_iota(jnp.int32, (heads, heads * LANES), 0)
    to_tile = jnp.where(lax.broadcasted_iota(jnp.int32, (heads, heads * LANES), 1) // LANES == head_row, 1.0, 0.0)
    cum_t = _dot_hi(a_cum, to_tile)
    of = lambda vals, i: [vals[i * per + j] for j in range(per)]
    tile = lambda x, i: x[:, i * LANES:(i + 1) * LANES]
    cum_c = jnp.concatenate([pick([tile(cum_t, e) for e in of(es, i)]) for i in tiles], axis=-1)
    dt_c = jnp.concatenate([pick([dt[:, e:e + 1] for e in of(es, i)]) for i in tiles], axis=-1)
    last_c = cum_c[chunk - 1:chunk, :]
    m = [(cb * jnp.exp(jnp.where(causal, tile(cum_t, e) - a_cum_t[e:e + 1, :], -jnp.inf))).astype(BF16) for e in es]
    xdt = xs * dt_c
    xdt_b = xdt.astype(BF16)
    xend_b = (xdt * jnp.exp(last_c - cum_c)).astype(BF16)
    grow = jnp.exp(cum_c)
    st_decay = jnp.exp(last_c)
    st = [state_ref[i] for i in tiles]
    y_in = [pick([dot(m[e], tile(xdt_b, i)) for e in of(es, i)]) for i in tiles]
    y_st = [dot(cmat, st[i].astype(BF16)) * tile(grow, i) for i in tiles]
    for i in tiles:
        state_ref[i] = st[i] * tile(st_decay, i) + dot(b_t, tile(xend_b, i))
        y_ref[:, i * LANES:(i + 1) * LANES] = y_in[i] + y_st[i]
    y = y_ref[...] + xs * dskip_ref[...]
    y = y * _silu(z_ref[0])
    ms = jnp.mean(y * y, axis=-1, keepdims=True)
    o_ref[0] = (y * lax.rsqrt(ms + NORM_EPS) * normw_ref[...]).astype(o_ref.dtype)


def ssd_scan(xbc, z, dt, dt_bias, a_log, d_skip, norm_w, *, chunk=MB_CHUNK):
    bsz, s_len, d_inner = z.shape
    n_heads = dt.shape[-1]
    n_state = MB_D_STATE
    groups = (xbc.shape[-1] - d_inner) // (2 * n_state)
    heads = n_heads // groups
    p_dim = d_inner // n_heads
    gw = heads * p_dim
    assert gw % LANES == 0 and d_inner % n_state == 0
    chunk = min(chunk, s_len)
    nc = s_len // chunk
    b_off = d_inner // n_state
    c_off = b_off + groups
    dt_g = jnp.transpose(dt.reshape(bsz, s_len, groups, heads), (0, 2, 1, 3))
    dt_gt = jnp.transpose(dt_g, (0, 1, 3, 2))
    kern = functools.partial(_ssd_kernel, chunk=chunk, heads=heads, p_dim=p_dim)
    per_group = lambda b_, g, c: (g, 0, 0)
    return pl.pallas_call(
        kern,
        grid=(bsz, groups, nc),
        in_specs=[pl.BlockSpec((1, chunk, gw), lambda b_, g, c: (b_, c, g)),
                  pl.BlockSpec((1, chunk, n_state), lambda b_, g, c: (b_, c, b_off + g)),
                  pl.BlockSpec((1, chunk, n_state), lambda b_, g, c: (b_, c, c_off + g)),
                  pl.BlockSpec((1, chunk, gw), lambda b_, g, c: (b_, c, g)),
                  pl.BlockSpec((1, 1, chunk, heads), lambda b_, g, c: (b_, g, c, 0)),
                  pl.BlockSpec((1, 1, heads, chunk), lambda b_, g, c: (b_, g, 0, c)),
                  pl.BlockSpec((1, 1, heads), per_group),
                  pl.BlockSpec((1, heads, 1), per_group),
                  pl.BlockSpec((1, 1, heads), per_group),
                  pl.BlockSpec((1, heads, 1), per_group),
                  pl.BlockSpec((1, gw), lambda b_, g, c: (0, g)),
                  pl.BlockSpec((1, gw), lambda b_, g, c: (0, g))],
        out_specs=pl.BlockSpec((1, chunk, gw), lambda b_, g, c: (b_, c, g)),
        out_shape=jax.ShapeDtypeStruct(z.shape, BF16),
        scratch_shapes=[pltpu.VMEM((gw // LANES, n_state, LANES), F32), pltpu.VMEM((chunk, gw), F32)],
        compiler_params=_params("parallel", "parallel", "arbitrary"),
        name="mamba_ssd",
    )(xbc, xbc, xbc, z, dt_g, dt_gt,
      dt_bias.reshape(groups, 1, heads), dt_bias.reshape(groups, heads, 1),
      a_log.reshape(groups, 1, heads), a_log.reshape(groups, heads, 1),
      jnp.repeat(d_skip, p_dim).reshape(1, d_inner), norm_w.reshape(1, d_inner))


def mamba2_mixer(u, h, w, bsz, s_len):
    d_inner = w["mb_w_out"].shape[0]
    n_heads = w["mb_dt_bias"].shape[0]
    w_in = w["mb_w_in"]
    xbc_w = w_in.shape[1] - d_inner - n_heads
    z = matmul_ws(u, [(w_in, 0)], d_inner, name="mb_in_z")
    xbc = matmul_ws(u, [(w_in, d_inner)], xbc_w, name="mb_in_xbc")
    dt = matmul_ws(u, [(w_in, d_inner + xbc_w)], n_heads, name="mb_in_dt")
    xbc = conv_silu(xbc.reshape(bsz, s_len, xbc_w), w["mb_conv_w"], w["mb_conv_b"])
    y = ssd_scan(xbc, z.reshape(bsz, s_len, d_inner), dt.reshape(bsz, s_len, n_heads),
                 w["mb_dt_bias"], w["mb_a_log"], w["mb_d_skip"], w["mb_norm_w"])
    return matmul_ws(y.reshape(bsz * s_len, d_inner), [(w["mb_w_out"], 0)], h.shape[1],
                     epilogue=_ep_residual, extras=[(h, "mn")], bm=512, w_buffers=1, name="mb_out")


def _seg_cumsum_rows(x, seg, reverse=False):
    n = x.shape[0]
    pos = lax.broadcasted_iota(jnp.int32, x.shape, 0) % seg
    s = 1
    while s < seg:
        if reverse:
            x = x + jnp.where(pos < seg - s, pltpu.roll(x, n - s, 0), 0.0)
        else:
            x = x + jnp.where(pos >= s, pltpu.roll(x, s, 0), 0.0)
        s *= 2
    return x


def _hgrn_kernel(q_ref, f_ref, i_ref, g_ref, lb_ref, nw_ref, o_ref, state_ref, *, sub, n_sub, heads, dk):
    @pl.when(pl.program_id(2) == 0)
    def _():
        state_ref[...] = jnp.zeros_like(state_ref)

    lb = lb_ref[...]
    nw = nw_ref[...]
    ti = lax.broadcasted_iota(jnp.int32, (sub, sub), 0)
    si = lax.broadcasted_iota(jnp.int32, (sub, sub), 1)
    causal = ti >= si
    f = lb + (1.0 - lb) * jax.nn.sigmoid(f_ref[0])
    lf = jnp.log(f)
    k = 1.0 - f
    b = _seg_cumsum_rows(lf, sub)
    to_end = _seg_cumsum_rows(lf, sub, reverse=True) - lf
    q_dec = (_silu(q_ref[0]) * jnp.exp(b)).astype(BF16)
    k_dec = (k * jnp.exp(-b)).astype(BF16)
    k_end = (k * jnp.exp(to_end)).astype(BF16)
    v = i_ref[0].astype(BF16)
    cs = range(n_sub)
    hs = range(heads)
    blk = lambda x, c, h: x[c * sub:(c + 1) * sub, h * dk:(h + 1) * dk]
    scores = [[jnp.where(causal, _dot_nt(blk(q_dec, c, h), blk(k_dec, c, h)), 0.0).astype(BF16) for h in hs]
              for c in cs]
    upd = [[_dot_tn(blk(v, c, h), blk(k_end, c, h)) for h in hs] for c in cs]
    states = []
    st = [state_ref[h] for h in hs]
    for c in cs:
        states.append(st)
        decay = jnp.exp(b[(c + 1) * sub - 1:(c + 1) * sub, :])
        st = [st[h] * decay[:, h * dk:(h + 1) * dk] + upd[c][h] for h in hs]
    for h in hs:
        state_ref[h] = st[h]
    for c in cs:
        rows = slice(c * sub, (c + 1) * sub)
        for h in hs:
            o = (jnp.dot(scores[c][h], blk(v, c, h), preferred_element_type=F32)
                 + _dot_nt(blk(q_dec, c, h), states[c][h].astype(BF16)))
            o = o * lax.rsqrt(jnp.mean(o * o, axis=-1, keepdims=True) + NORM_EPS) * nw
            cols = slice(h * dk, (h + 1) * dk)
            o_ref[0, rows, cols] = (o * _silu(g_ref[0, rows, cols])).astype(o_ref.dtype)


def hgrn2_scan(proj, lower_bound, norm_w, *, dk=HG_HEAD_DIM, sub=HG_CHUNK, tb=256, heads=8):
    bsz, s_len, d4 = proj.shape
    d = d4 // 4
    n_heads = d // dk
    tb = min(tb, s_len)
    heads = min(heads, n_heads)
    hw = heads * dk
    n_hb = n_heads // heads
    kern = functools.partial(_hgrn_kernel, sub=sub, n_sub=tb // sub, heads=heads, dk=dk)
    spec = lambda part: pl.BlockSpec((1, tb, hw), lambda b_, h_, t: (b_, t, part * n_hb + h_))
    return pl.pallas_call(
        kern,
        grid=(bsz, n_hb, s_len // tb),
        in_specs=[spec(0), spec(1), spec(2), spec(3),
                  pl.BlockSpec((1, hw), lambda b_, h_, t: (0, h_)),
                  pl.BlockSpec((1, dk), lambda b_, h_, t: (0, 0))],
        out_specs=pl.BlockSpec((1, tb, hw), lambda b_, h_, t: (b_, t, h_)),
        out_shape=jax.ShapeDtypeStruct((bsz, s_len, d), BF16),
        scratch_shapes=[pltpu.VMEM((heads, dk, dk), F32)],
        compiler_params=_params("parallel", "parallel", "arbitrary"),
        name="hgrn2_scan",
    )(proj, proj, proj, proj, lower_bound.reshape(1, d), norm_w.reshape(1, dk))


def hgrn2_mixer(u, h, w, lower_bound, bsz, s_len):
    d = h.shape[1]
    proj = matmul_ws(u, [(w["hg_w_in"], 0)], 4 * d, bm=512, bn=1024, name="hg_in")
    o = hgrn2_scan(proj.reshape(bsz, s_len, 4 * d), lower_bound, w["hg_norm_w"])
    return matmul_ws(o.reshape(bsz * s_len, d), [(w["hg_w_out"], 0)], d,
                     epilogue=_ep_residual, extras=[(h, "mn")], name="hg_out")


def dense_ffn(v, h, w_in, w_out):
    f = w_out.shape[0]
    hid = matmul_ws(v, [(w_in, 0), (w_in, f)], f, epilogue=_ep_swiglu, out_dtype=BF16, bm=1024, bn=256,
                    name="ffn_in")
    return matmul_ws(hid, [(w_out, 0)], h.shape[1], epilogue=_ep_residual, extras=[(h, "mn")], bm=512, w_buffers=1,
                     name="ffn_out")


def _router_kernel(x_ref, r_ref, o_ref, *, n_experts):
    logits = jnp.dot(x_ref[...], r_ref[...], preferred_element_type=F32)
    lane = lax.broadcasted_iota(jnp.int32, logits.shape, 1)
    logits = jnp.where(lane < n_experts, logits, -jnp.inf)
    m1 = jnp.max(logits, axis=-1, keepdims=True)
    i1 = jnp.min(jnp.where(logits == m1, lane, LANES), axis=-1, keepdims=True)
    rest = jnp.where(lane == i1, -jnp.inf, logits)
    m2 = jnp.max(rest, axis=-1, keepdims=True)
    i2 = jnp.min(jnp.where(rest == m2, lane, LANES), axis=-1, keepdims=True)
    e2 = jnp.exp(m2 - m1)
    w1 = 1.0 / (1.0 + e2)
    o_ref[...] = jnp.where(lane == i1, w1, 0.0) + jnp.where(lane == i2, e2 * w1, 0.0)


def moe_router(v, router):
    m, d = v.shape
    n_experts = router.shape[1]
    r_pad = jnp.zeros((d, LANES), BF16).at[:, :n_experts].set(router.astype(BF16))
    bm = _pick(m, 512)
    return pl.pallas_call(
        functools.partial(_router_kernel, n_experts=n_experts),
        grid=(m // bm,),
        in_specs=[pl.BlockSpec((bm, d), lambda i: (i, 0)), pl.BlockSpec((d, LANES), lambda i: (0, 0))],
        out_specs=pl.BlockSpec((bm, LANES), lambda i: (i, 0)),
        out_shape=jax.ShapeDtypeStruct((m, LANES), F32),
        compiler_params=_params("parallel"),
        name="moe_router",
    )(v, r_pad)


MOE_BLOCK = 1024
MOE_UNIT = 32
MOE_TILE = 512


def _moe_gather_kernel(x_ref, tok_ref, o_ref):
    tok = tok_ref[0]
    lane = lax.broadcasted_iota(jnp.int32, (tok.shape[0], x_ref.shape[0]), 1)
    onehot = jnp.where(tok == lane, 1.0, 0.0).astype(BF16)
    o_ref[...] = jnp.dot(onehot, x_ref[...], preferred_element_type=F32).astype(o_ref.dtype)


def _moe_expert_in_kernel(src_ref, exp_ref, first_ref, used_ref, *refs, per):
    x_refs = refs[:per]
    wg_ref, wu_ref, rw_ref, o_ref, wgb_ref, wub_ref, x_scr = refs[per:]
    t = pl.program_id(1)

    @pl.when(t < used_ref[0])
    def _():
        @pl.when(first_ref[t] == 1)
        def _():
            wgb_ref[...] = wg_ref[0].astype(BF16)
            wub_ref[...] = wu_ref[0].astype(BF16)

        unit = x_refs[0].shape[0]
        for i in range(per):
            x_scr[i * unit:(i + 1) * unit, :] = x_refs[i][...]
        x = x_scr[...]
        g = jnp.dot(x, wgb_ref[...], preferred_element_type=F32)
        u = jnp.dot(x, wub_ref[...], preferred_element_type=F32)
        o_ref[...] = (_silu(g) * u * rw_ref[...]).astype(o_ref.dtype)

    @pl.when(t >= used_ref[0])
    def _():
        o_ref[...] = jnp.zeros_like(o_ref)


def _moe_expert_out_kernel(exp_ref, first_ref, used_ref, hid_ref, w_ref, o_ref, wb_ref):
    t = pl.program_id(1)

    @pl.when(t < used_ref[0])
    def _():
        @pl.when(first_ref[t] == 1)
        def _():
            wb_ref[...] = w_ref[0].astype(BF16)

        o_ref[...] = jnp.dot(hid_ref[...], wb_ref[...], preferred_element_type=F32).astype(o_ref.dtype)

    @pl.when(pl.program_id(1) >= used_ref[0])
    def _():
        o_ref[...] = jnp.zeros_like(o_ref)


def _moe_scatter_kernel(dst_ref, h_ref, tok_ref, *refs, per):
    y_refs = refs[:per]
    o_ref, y_scr = refs[per:]

    @pl.when(pl.program_id(2) == 0)
    def _():
        o_ref[...] = h_ref[...]

    tok = tok_ref[0]
    row = lax.broadcasted_iota(jnp.int32, (o_ref.shape[0], tok.shape[1]), 0)
    onehot_t = jnp.where(tok == row, 1.0, 0.0).astype(BF16)
    unit = y_refs[0].shape[0]
    for i in range(per):
        y_scr[i * unit:(i + 1) * unit, :] = y_refs[i][...]
    o_ref[...] += jnp.dot(onehot_t, y_scr[...], preferred_element_type=F32)


def moe_ffn_routed(v, h, router, w_in, w_out, *, tb=MOE_BLOCK, unit=MOE_UNIT, tile=MOE_TILE, bn=512, bo=1024):
    m, d = v.shape
    n_experts, _, two_de = w_in.shape
    de = two_de // 2
    tb = min(tb, m)
    nb = m // tb
    per = tile // unit
    bn = _pick(de, bn)
    bo = _pick(d, bo)
    n_assign = MOE_TOPK * tb
    n_slots = -(-(n_assign // unit + n_experts + 1) // per) * per
    groups = n_slots // per
    n_units = nb * (n_assign // unit + n_experts) + n_experts * (per - 1)
    n_tiles = -(-n_units // per)
    n_units = n_tiles * per

    comb = moe_router(v, router)
    wts, ids = lax.top_k(comb[:, :n_experts], MOE_TOPK)
    ea = ids.reshape(nb, n_assign)
    wa = wts.reshape(nb, n_assign)
    ta = jnp.broadcast_to(jnp.repeat(jnp.arange(tb, dtype=jnp.int32), MOE_TOPK)[None], (nb, n_assign))
    se, st, sw = lax.sort((ea, ta, wa), dimension=1, num_keys=1, is_stable=True)
    counts = jnp.sum(jax.nn.one_hot(ea, n_experts, dtype=jnp.int32), axis=1)
    units = -(-counts // unit)
    excl = lambda x, axis: jnp.cumsum(x, axis=axis) - x
    slot_start = excl(units, 1)
    row_start = excl(counts, 1)
    is_e = se[..., None] == jnp.arange(n_experts, dtype=jnp.int32)
    lookup = lambda table: jnp.sum(jnp.where(is_e, table[:, None, :], 0), axis=-1)
    pos = lookup(slot_start) * unit + jnp.arange(n_assign, dtype=jnp.int32)[None] - lookup(row_start)
    bidx = jnp.arange(nb, dtype=jnp.int32)[:, None]
    hit = pos[:, None, :] == jnp.arange(n_slots * unit, dtype=jnp.int32)[None, :, None]
    row_token = jnp.sum(jnp.where(hit, st[:, None, :] + 1, 0), axis=-1) - 1
    row_weight = jnp.sum(jnp.where(hit, sw[:, None, :], 0.0), axis=-1)
    per_expert = jnp.sum(units, axis=0)
    per_expert_pad = -(-per_expert // per) * per
    e_off = excl(per_expert_pad, 0)
    before = excl(units, 0)
    slot = jnp.arange(n_slots, dtype=jnp.int32)
    slot_end = jnp.cumsum(units, axis=1)
    e_of_slot = jnp.sum(slot[None, :, None] >= slot_end[:, None, :], axis=-1)
    used_slot = e_of_slot < n_experts
    e_clip = jnp.minimum(e_of_slot, n_experts - 1)
    dst_unit = (e_off[e_clip] + jnp.take_along_axis(before, e_clip, axis=1)
                + slot[None] - jnp.take_along_axis(slot_start, e_clip, axis=1))
    dst_unit = jnp.where(used_slot, dst_unit, 0).astype(jnp.int32)
    flat_slot = (bidx * n_slots + slot[None]).astype(jnp.int32)
    zero_slot = n_slots - 1
    src_unit = jnp.full((n_units,), zero_slot, jnp.int32).at[
        jnp.where(used_slot, dst_unit, n_units).reshape(-1)].set(flat_slot.reshape(-1), mode="drop")
    tile_end = jnp.cumsum(per_expert_pad) // per
    tile_ids = jnp.arange(n_tiles, dtype=jnp.int32)
    tile_expert = jnp.minimum(jnp.sum(tile_ids[:, None] >= tile_end[None, :], axis=-1), n_experts - 1).astype(jnp.int32)
    tiles_used = tile_end[-1:].astype(jnp.int32)
    first = jnp.concatenate([jnp.ones((1,), jnp.int32),
                             (tile_expert[1:] != tile_expert[:-1]).astype(jnp.int32)])
    rw_em = row_weight.reshape(nb * n_slots, unit)[src_unit].reshape(n_units * unit, 1)

    xs = pl.pallas_call(
        _moe_gather_kernel,
        grid=(nb, groups),
        in_specs=[pl.BlockSpec((tb, d), lambda b_, g: (b_, 0)),
                  pl.BlockSpec((1, tile, 1), lambda b_, g: (b_, g, 0))],
        out_specs=pl.BlockSpec((tile, d), lambda b_, g: (b_ * groups + g, 0)),
        out_shape=jax.ShapeDtypeStruct((nb * n_slots * unit, d), BF16),
        compiler_params=_params("parallel", "arbitrary"),
        name="moe_gather",
    )(v, row_token.reshape(nb, n_slots * unit, 1))

    bpe = de // bn
    unit_spec = lambda i: pl.BlockSpec((unit, d), lambda j, t, src, ex, fi, us: (src[per * t + i], 0))
    hid = pl.pallas_call(
        functools.partial(_moe_expert_in_kernel, per=per),
        grid_spec=pltpu.PrefetchScalarGridSpec(
            num_scalar_prefetch=4,
            grid=(bpe, n_tiles),
            in_specs=[unit_spec(i) for i in range(per)] + [
                pl.BlockSpec((1, d, bn), lambda j, t, src, ex, fi, us: (ex[t], 0, j),
                             pipeline_mode=pl.Buffered(1)),
                pl.BlockSpec((1, d, bn), lambda j, t, src, ex, fi, us: (ex[t], 0, j + bpe),
                             pipeline_mode=pl.Buffered(1)),
                pl.BlockSpec((tile, 1), lambda j, t, src, ex, fi, us: (t, 0))],
            out_specs=pl.BlockSpec((tile, bn), lambda j, t, src, ex, fi, us: (t, j)),
            scratch_shapes=[pltpu.VMEM((d, bn), BF16), pltpu.VMEM((d, bn), BF16), pltpu.VMEM((tile, d), BF16)]),
        out_shape=jax.ShapeDtypeStruct((n_tiles * tile, de), BF16),
        compiler_params=_params("arbitrary", "arbitrary"),
        name="moe_expert_in",
    )(src_unit, tile_expert, first, tiles_used, *([xs] * per), w_in, w_in, rw_em)

    bo2 = _pick(d, 2 * bo)
    ys = pl.pallas_call(
        _moe_expert_out_kernel,
        grid_spec=pltpu.PrefetchScalarGridSpec(
            num_scalar_prefetch=3,
            grid=(d // bo2, n_tiles),
            in_specs=[pl.BlockSpec((tile, de), lambda n, t, ex, fi, us: (t, 0)),
                      pl.BlockSpec((1, de, bo2), lambda n, t, ex, fi, us: (ex[t], 0, n))],
            out_specs=pl.BlockSpec((tile, bo2), lambda n, t, ex, fi, us: (t, n)),
            scratch_shapes=[pltpu.VMEM((de, bo2), BF16)]),
        out_shape=jax.ShapeDtypeStruct((n_tiles * tile, d), BF16),
        compiler_params=_params("arbitrary", "arbitrary"),
        name="moe_expert_out",
    )(tile_expert, first, tiles_used, hid, w_out)

    y_spec = lambda i: pl.BlockSpec((unit, bo2), lambda b_, n, g, dst: (dst[(b_ * groups + g) * per + i], n))
    return pl.pallas_call(
        functools.partial(_moe_scatter_kernel, per=per),
        grid_spec=pltpu.PrefetchScalarGridSpec(
            num_scalar_prefetch=1,
            grid=(nb, d // bo2, groups),
            in_specs=[pl.BlockSpec((tb, bo2), lambda b_, n, g, dst: (b_, n)),
                      pl.BlockSpec((1, 1, tile), lambda b_, n, g, dst: (b_ * groups + g, 0, 0))]
                     + [y_spec(i) for i in range(per)],
            out_specs=pl.BlockSpec((tb, bo2), lambda b_, n, g, dst: (b_, n)),
            scratch_shapes=[pltpu.VMEM((tile, bo2), BF16)]),
        out_shape=jax.ShapeDtypeStruct((m, d), F32),
        compiler_params=_params("parallel", "parallel", "arbitrary"),
        name="moe_scatter",
    )(dst_unit.reshape(-1), h, row_token.reshape(nb * groups, 1, tile), *([ys] * per))


def ple_gate(h, p_i, norm_pl, pl_proj, pl_gate, layer):
    d = h.shape[1]
    n = rmsnorm(h, norm_pl, name="rmsnorm_ple")
    return matmul_ws(n, [(pl_gate, (layer, 0))], d, epilogue=_ep_ple_gate,
                     extras=[(h, "mn"), (p_i, "m"), (pl_proj, "kn")], name="ple_gate")


def _rw_mix_kernel(u_ref, mu_ref, *o_refs):
    u = u_ref[0]
    row = lax.broadcasted_iota(jnp.int32, u.shape, 0)
    dx = jnp.where(row >= 1, pltpu.roll(u, 1, 0), 0.0) - u
    for j, o_ref in enumerate(o_refs):
        o_ref[0] = (u + dx * mu_ref[j:j + 1, :]).astype(o_ref.dtype)


def rw_token_mix(u, mu):
    bsz, s_len, d = u.shape
    cb = _pick(d, LANES)
    n_mix = mu.shape[0]
    spec = pl.BlockSpec((1, s_len, cb), lambda b_, j: (b_, 0, j))
    return pl.pallas_call(
        _rw_mix_kernel,
        grid=(bsz, d // cb),
        in_specs=[spec, pl.BlockSpec((n_mix, cb), lambda b_, j: (0, j))],
        out_specs=[spec] * n_mix,
        out_shape=[jax.ShapeDtypeStruct(u.shape, BF16)] * n_mix,
        compiler_params=_params("parallel", "parallel"),
        name="rwkv_token_mix",
    )(u, mu)


def _dot_hi(a, b):
    return jnp.dot(a, b, preferred_element_type=F32, precision=lax.Precision.HIGHEST)


def _rw_scan_tile_kernel(r_ref, k_ref, v_ref, a_ref, lw_ref, g_ref, kk_ref, ka_ref, rk_ref, lnw_ref, lnb_ref,
                         o_ref, state_ref, *, chunk, heads, n):
    @pl.when(pl.program_id(2) == 0)
    def _():
        state_ref[...] = jnp.zeros_like(state_ref)

    per = LANES // n
    tiles = range(heads // per)
    sub = range(per)
    ti = lax.broadcasted_iota(jnp.int32, (chunk, chunk), 0)
    si = lax.broadcasted_iota(jnp.int32, (chunk, chunk), 1)
    strict = ti > si
    incl = ti >= si
    lane_seg = lax.broadcasted_iota(jnp.int32, (1, LANES), 1) // n
    seg_is = [lane_seg == j for j in sub]
    same_head = (lax.broadcasted_iota(jnp.int32, (LANES, LANES), 0) // n
                 == lax.broadcasted_iota(jnp.int32, (LANES, LANES), 1) // n)
    dot = functools.partial(jnp.dot, preferred_element_type=F32)
    tile = lambda x, i: x[:, i * LANES:(i + 1) * LANES]

    def pick(vals):
        out = vals[-1]
        for j in range(per - 2, -1, -1):
            out = jnp.where(seg_is[j], vals[j], out)
        return out

    def seg_sum(x):
        return pick([jnp.sum(jnp.where(seg_is[j], x, 0.0), axis=-1, keepdims=True) for j in sub])

    r, k, v, a, lw = r_ref[0], k_ref[0], v_ref[0], a_ref[0], lw_ref[0]
    kk = k * kk_ref[...]
    kmod = k * (1.0 + (a - 1.0) * ka_ref[...])
    cum = _cumsum_rows(lw, chunk)
    cum_end = cum[chunk - 1:chunk, :]
    mid = cum[chunk // 2 - 1:chunk // 2, :]
    bonus_in = r * kmod * rk_ref[...]
    kk_t, bonus_t = [], []
    for i in tiles:
        kki = tile(kk, i)
        kk_t.append(kki / jnp.maximum(jnp.sqrt(seg_sum(kki * kki)), 1e-12))
        bonus_t.append(seg_sum(tile(bonus_in, i)) * tile(v, i))
    kk = jnp.concatenate(kk_t, axis=-1) if len(kk_t) > 1 else kk_t[0]
    kka = kk * a
    e_neg = jnp.exp(mid - cum)
    to_end = jnp.exp(cum_end - cum)
    am = (kk * jnp.exp(cum - lw - mid)).astype(BF16)
    bm = (kka * e_neg).astype(BF16)
    km = (kmod * e_neg).astype(BF16)
    rm = (r * jnp.exp(cum - mid)).astype(BF16)
    a_abs = (kk * jnp.exp(cum - lw)).astype(BF16)
    r_abs = (r * jnp.exp(cum)).astype(BF16)
    k_end = (kmod * to_end).astype(BF16)
    b_end = (kka * to_end).astype(BF16)
    vb = v.astype(BF16)
    st_decay = jnp.exp(cum_end)
    zero = jnp.zeros((), BF16)

    st = [state_ref[i] for i in tiles]
    stb = [s.astype(BF16) for s in st]
    am_h = [[jnp.where(seg_is[j], tile(am, i), zero) for j in sub] for i in tiles]
    rm_h = [[jnp.where(seg_is[j], tile(rm, i), zero) for j in sub] for i in tiles]
    nb = [[(-jnp.where(strict, _dot_nt(am_h[i][j], tile(bm, i)), 0.0)).astype(BF16) for j in sub] for i in tiles]
    lk = [[jnp.where(strict, _dot_nt(am_h[i][j], tile(km, i)), 0.0).astype(BF16) for j in sub] for i in tiles]
    x = [_dot_nt(tile(a_abs, i), stb[i]) + pick([dot(lk[i][j], tile(vb, i)) for j in sub]) for i in tiles]
    xb = [xi.astype(BF16) for xi in x]
    x = [x[i] + pick([dot(nb[i][j], xb[i]) for j in sub]) for i in tiles]
    p = 2
    while p < chunk:
        nb = [[dot(nb[i][j], nb[i][j]).astype(BF16) for j in sub] for i in tiles]
        xb = [xi.astype(BF16) for xi in x]
        x = [x[i] + pick([dot(nb[i][j], xb[i]) for j in sub]) for i in tiles]
        p *= 2
    pb = [xi.astype(BF16) for xi in x]
    mk = [[jnp.where(incl, _dot_nt(rm_h[i][j], tile(km, i)), 0.0).astype(BF16) for j in sub] for i in tiles]
    mb = [[jnp.where(incl, _dot_nt(rm_h[i][j], tile(bm, i)), 0.0).astype(BF16) for j in sub] for i in tiles]
    y = [_dot_nt(tile(r_abs, i), stb[i])
         + pick([dot(mk[i][j], tile(vb, i)) - dot(mb[i][j], pb[i]) for j in sub]) for i in tiles]
    for i in tiles:
        upd = _dot_tn(tile(vb, i), tile(k_end, i)) - _dot_tn(pb[i], tile(b_end, i))
        state_ref[i] = st[i] * tile(st_decay, i) + jnp.where(same_head, upd, 0.0)
    inv_n = 1.0 / n
    for i in tiles:
        cols = slice(i * LANES, (i + 1) * LANES)
        mean = seg_sum(y[i]) * inv_n
        yc = y[i] - mean
        var = seg_sum(yc * yc) * inv_n
        yn = yc * lax.rsqrt(var + RW_LN_EPS) * lnw_ref[:, cols] + lnb_ref[:, cols]
        o_ref[0, :, cols] = ((yn + bonus_t[i]) * g_ref[0, :, cols]).astype(o_ref.dtype)


def rw_scan(r, k, v, a, lw, g, k_k, k_a, r_k, ln_w, ln_b, *, n=RW_HEAD_DIM, chunk=RW_CHUNK, heads=8):
    bsz, s_len, d = r.shape
    chunk = min(chunk, s_len)
    heads = min(heads, d // n)
    hw = heads * n
    seq = pl.BlockSpec((1, chunk, hw), lambda b_, h_, c: (b_, c, h_))
    par = pl.BlockSpec((1, hw), lambda b_, h_, c: (0, h_))
    row = lambda t: t.reshape(1, d)
    assert hw % LANES == 0 and LANES % n == 0
    kern = functools.partial(_rw_scan_tile_kernel, chunk=chunk, heads=heads, n=n)
    return pl.pallas_call(
        kern,
        grid=(bsz, d // hw, s_len // chunk),
        in_specs=[seq] * 6 + [par] * 5,
        out_specs=seq,
        out_shape=jax.ShapeDtypeStruct(r.shape, BF16),
        scratch_shapes=[pltpu.VMEM((hw // LANES, LANES, LANES), F32)],
        compiler_params=_params("parallel", "parallel", "arbitrary"),
        name="rwkv7_scan",
    )(r, k, v, a, lw, g, row(k_k), row(k_a), row(r_k), row(ln_w), row(ln_b))


def rwkv7_mixer(u, h, w, bsz, s_len):
    t, d = u.shape
    xr, xw, xk, xv, xa, xg = [x.reshape(t, d) for x in rw_token_mix(u.reshape(bsz, s_len, d), w["rw_mu"])]
    r = matmul_ws(xr, [(w["rw_w_rkv"], (0, 0))], d, name="rw_r")
    k = matmul_ws(xk, [(w["rw_w_rkv"], (1, 0))], d, name="rw_k")
    v = matmul_ws(xv, [(w["rw_w_rkv"], (2, 0))], d, name="rw_v")
    row = lambda x: x.reshape(1, d)
    w_lo = matmul(xw, [(w["rw_w1"], 0)], w["rw_w1"].shape[1], epilogue=_ep_tanh, out_dtype=BF16, name="rw_w1")
    wide = 2048
    lw = matmul(w_lo, [(w["rw_w2"], 0)], d, epilogue=_ep_rw_logdecay, extras=[(row(w["rw_w0"]), "n")], bn=wide,
                name="rw_w2")
    a_lo = matmul(xa, [(w["rw_a1"], 0)], w["rw_a1"].shape[1], out_dtype=BF16, name="rw_a1")
    a = matmul(a_lo, [(w["rw_a2"], 0)], d, epilogue=_ep_bias_sigmoid, extras=[(row(w["rw_a0"]), "n")], bn=wide,
               name="rw_a2")
    g_lo = matmul(xg, [(w["rw_g1"], 0)], w["rw_g1"].shape[1], epilogue=_ep_sigmoid, out_dtype=BF16, name="rw_g1")
    g = matmul(g_lo, [(w["rw_g2"], 0)], d, bn=wide, name="rw_g2")
    shp = (bsz, s_len, d)
    y = rw_scan(r.reshape(shp), k.reshape(shp), v.reshape(shp), a.reshape(shp), lw.reshape(shp), g.reshape(shp),
                w["rw_k_k"], w["rw_k_a"], w["rw_r_k"], w["rw_ln_w"], w["rw_ln_b"])
    return matmul_ws(y.reshape(t, d), [(w["rw_w_out"], 0)], d, epilogue=_ep_residual, extras=[(h, "mn")],
                     name="rw_out")


NEG_BIG = -1e30


def _rope_kernel(x_ref, cc_ref, ss_ref, o_ref, *, n_q_slots, scale):
    x = x_ref[0]
    out = x * cc_ref[...] + pltpu.roll(x, x.shape[-1] // 2, 1) * ss_ref[...]
    out = out * jnp.where(pl.program_id(2) < n_q_slots, scale, 1.0)
    o_ref[0] = out.astype(o_ref.dtype)


def _rope_tables(pos, dim):
    inv = ROPE_THETA ** (-(jnp.arange(0, dim, 2, dtype=F32) / dim))
    ang = pos.astype(F32)[:, None] * inv[None, :]
    cos, sin = jnp.cos(ang), jnp.sin(ang)
    return jnp.concatenate([cos, cos], axis=-1), jnp.concatenate([-sin, sin], axis=-1)


def nsa_rope(proj, n_q_slots, k_slots, dh, scale, tb=512):
    bsz, s_len, _ = proj.shape
    tb = min(tb, s_len)
    cc, ss = _rope_tables(jnp.arange(s_len), dh)
    n_out = n_q_slots + len(k_slots)

    def in_slot(j):
        slot = j
        for idx, ks in enumerate(k_slots):
            slot = jnp.where(j == n_q_slots + idx, ks, slot)
        return slot

    return pl.pallas_call(
        functools.partial(_rope_kernel, n_q_slots=n_q_slots, scale=scale),
        grid=(bsz, s_len // tb, n_out),
        in_specs=[pl.BlockSpec((1, tb, dh), lambda b_, t, j: (b_, t, in_slot(j))),
                  pl.BlockSpec((tb, dh), lambda b_, t, j: (t, 0)),
                  pl.BlockSpec((tb, dh), lambda b_, t, j: (t, 0))],
        out_specs=pl.BlockSpec((1, tb, dh), lambda b_, t, j: (b_, t, j)),
        out_shape=jax.ShapeDtypeStruct((bsz, s_len, n_out * dh), BF16),
        compiler_params=_params("parallel", "parallel", "arbitrary"),
        name="nsa_rope",
    )(proj, cc, ss)


def _cmp_finish_kernel(z_ref, bias_ref, w2_ref, cc_ref, ss_ref, o_ref, *, hidden, rope):
    z = z_ref[0]
    nc = z.shape[0]
    nxt = pltpu.roll(z[:, hidden:], nc - 1, 0)
    hid = _silu(z[:, :hidden] + nxt + bias_ref[...])
    out = jnp.dot(hid.astype(BF16), w2_ref[...], preferred_element_type=F32)
    if rope:
        out = out * cc_ref[...] + pltpu.roll(out, out.shape[-1] // 2, 1) * ss_ref[...]
    o_ref[0] = out.astype(o_ref.dtype)


def nsa_compress(x, pos_emb, w1, w2, bsz, s_len, groups, dh, rope, transpose_out=False):
    stride, blk = NSA_CMP_STRIDE, NSA_CMP_BLOCK
    nc = s_len // stride
    hidden = w1.shape[-1]
    half = stride * dh
    x16 = jnp.transpose(x.reshape(bsz, nc, stride, groups, dh), (0, 3, 1, 2, 4)).reshape(bsz * groups * nc, half)
    w1f = w1.reshape(blk * dh, hidden)
    wcat = jnp.concatenate([w1f[:half], w1f[half:]], axis=1).astype(BF16)
    z = matmul(x16.astype(BF16), [(wcat, 0)], 2 * hidden, name="nsa_cmp_w1")
    bias = matmul(pos_emb.reshape(1, blk * dh).astype(BF16), [(w1f.astype(BF16), 0)], hidden, name="nsa_cmp_pos")
    cc, ss = _rope_tables(jnp.arange(nc) * stride + blk - 1, dh)
    if transpose_out:
        assert not rope
        return pl.pallas_call(
            functools.partial(_cmp_finish_t_kernel, hidden=hidden),
            grid=(bsz * groups,),
            in_specs=[pl.BlockSpec((1, nc, 2 * hidden), lambda i: (i, 0, 0)),
                      pl.BlockSpec((1, hidden), lambda i: (0, 0)),
                      pl.BlockSpec((dh, hidden), lambda i: (0, 0))],
            out_specs=pl.BlockSpec((1, dh, nc), lambda i: (i, 0, 0)),
            out_shape=jax.ShapeDtypeStruct((bsz * groups, dh, nc), BF16),
            compiler_params=_params("parallel"),
            name="nsa_cmp_finish_t",
        )(z.reshape(bsz * groups, nc, 2 * hidden), bias, w2.T.astype(BF16))
    return pl.pallas_call(
        functools.partial(_cmp_finish_kernel, hidden=hidden, rope=rope),
        grid=(bsz * groups,),
        in_specs=[pl.BlockSpec((1, nc, 2 * hidden), lambda i: (i, 0, 0)),
                  pl.BlockSpec((1, hidden), lambda i: (0, 0)),
                  pl.BlockSpec((hidden, dh), lambda i: (0, 0)),
                  pl.BlockSpec((nc, dh), lambda i: (0, 0)),
                  pl.BlockSpec((nc, dh), lambda i: (0, 0))],
        out_specs=pl.BlockSpec((1, nc, dh), lambda i: (i, 0, 0)),
        out_shape=jax.ShapeDtypeStruct((bsz * groups, nc, dh), BF16),
        compiler_params=_params("parallel"),
        name="nsa_cmp_finish",
    )(z.reshape(bsz * groups, nc, 2 * hidden), bias, w2.astype(BF16), cc, ss)


def _rope_t_kernel(x_ref, cc_ref, ss_ref, o_ref, *, n_rope, scale, group, dh):
    first_slot = pl.program_id(2) * group
    for i in range(group):
        x = x_ref[0, :, i * dh:(i + 1) * dh]
        roped = (x * cc_ref[...] + pltpu.roll(x, dh // 2, 1) * ss_ref[...]) * scale
        out = jnp.where(first_slot + i < n_rope, roped, x)
        o_ref[0, i * dh:(i + 1) * dh, :] = out.T.astype(o_ref.dtype)


def nsa_rope_t(proj, slots, n_rope, dh, scale, tb=512, group=4):
    bsz, s_len, _ = proj.shape
    tb = min(tb, s_len)
    cc, ss = _rope_tables(jnp.arange(s_len), dh)
    assert len(slots) % group == 0
    firsts = slots[::group]
    assert all(f % group == 0 and slots[i * group:(i + 1) * group] == list(range(f, f + group))
               for i, f in enumerate(firsts))
    table = jnp.asarray([f // group for f in firsts], jnp.int32)
    grid_spec = pltpu.PrefetchScalarGridSpec(
        num_scalar_prefetch=1,
        grid=(bsz, s_len // tb, len(firsts)),
        in_specs=[pl.BlockSpec((1, tb, group * dh), lambda b_, t, j, tab: (b_, t, tab[j])),
                  pl.BlockSpec((tb, dh), lambda b_, t, j, tab: (t, 0)),
                  pl.BlockSpec((tb, dh), lambda b_, t, j, tab: (t, 0))],
        out_specs=pl.BlockSpec((1, group * dh, tb), lambda b_, t, j, tab: (b_, j, t)),
    )
    kern = lambda tab, x_ref, cc_ref, ss_ref, o_ref: _rope_t_kernel(x_ref, cc_ref, ss_ref, o_ref, n_rope=n_rope,
                                                                   scale=scale, group=group, dh=dh)
    return pl.pallas_call(
        kern,
        grid_spec=grid_spec,
        out_shape=jax.ShapeDtypeStruct((bsz, len(slots) * dh, s_len), BF16),
        compiler_params=_params("parallel", "parallel", "arbitrary"),
        name="nsa_rope_t",
    )(table, proj, cc, ss)


def _cmp_finish_t_kernel(z_ref, bias_ref, w2_ref, o_ref, *, hidden):
    z = z_ref[0]
    nc = z.shape[0]
    nxt = pltpu.roll(z[:, hidden:], nc - 1, 0)
    hid = _silu(z[:, :hidden] + nxt + bias_ref[...])
    o_ref[0] = _dot_nt(w2_ref[...], hid.astype(BF16)).astype(o_ref.dtype)


def _nsa_cmp_select_t_kernel(q_ref, kc_ref, vc_ref, ov_ref, oc_ref, sel_ref, *, tq, rep, dh, topn):
    qi = pl.program_id(2)
    kc = kc_ref[0]
    vct = vc_ref[0]
    nc = kc.shape[0]
    n_sel = sel_ref.shape[2]
    t = qi * tq + lax.broadcasted_iota(jnp.int32, (nc, tq), 1)
    cmp_end = lax.broadcasted_iota(jnp.int32, (nc, tq), 0) * NSA_CMP_STRIDE + (NSA_CMP_BLOCK - 1)
    visible = cmp_end <= t
    s = [jnp.where(visible, jnp.dot(kc, q_ref[0, r * dh:(r + 1) * dh, :], preferred_element_type=F32), NEG_BIG)
         for r in range(rep)]
    e = [jnp.where(visible, jnp.exp2(x - jnp.max(x, axis=0, keepdims=True)), 0.0) for x in s]
    den = [jnp.sum(x, axis=0, keepdims=True) for x in e]
    p = [e[r] / jnp.where(den[r] > 0, den[r], 1.0) for r in range(rep)]
    for r in range(rep):
        oc_ref[0, r * dh:(r + 1) * dh, :] = jnp.dot(vct, p[r].astype(BF16), preferred_element_type=F32)
    psum = p[0]
    for r in range(1, rep):
        psum = psum + p[r]
    imp = _dot_hi(ov_ref[...], psum)
    blk = lax.broadcasted_iota(jnp.int32, (n_sel, tq), 0)
    cur = (qi * tq + lax.broadcasted_iota(jnp.int32, (n_sel, tq), 1)) // NSA_SEL_BLOCK
    forced = (blk == 0) | (blk == cur) | (blk == cur - 1)
    imp = jnp.where(forced, NSA_FORCED_SCORE, imp)
    imp = jnp.where(blk > cur, -jnp.inf, imp)
    sel = jnp.zeros((n_sel, tq), F32)
    for _ in range(topn):
        m = jnp.max(imp, axis=0, keepdims=True)
        first = jnp.min(jnp.where(imp == m, blk, n_sel), axis=0, keepdims=True)
        hit = blk == first
        sel = jnp.where(hit, 1.0, sel)
        imp = jnp.where(hit, -jnp.inf, imp)
    sel_ref[0, 0] = sel


def _flash_t_init(m_ref, l_ref, acc_ref):
    m_ref[...] = jnp.full_like(m_ref, NEG_BIG)
    l_ref[...] = jnp.zeros_like(l_ref)
    acc_ref[...] = jnp.zeros_like(acc_ref)


def _flash_t_step(q_ref, k, vt, mask, m_ref, l_ref, acc_ref, rep, dh):
    hs = range(rep)
    s = [jnp.where(mask, jnp.dot(k, q_ref[0, r * dh:(r + 1) * dh, :], preferred_element_type=F32), NEG_BIG)
         for r in hs]
    m_old = [m_ref[r] for r in hs]
    m_new = [jnp.maximum(m_old[r], jnp.max(s[r], axis=0, keepdims=True)) for r in hs]
    p = [jnp.exp2(s[r] - m_new[r]).astype(BF16) for r in hs]
    alpha = [jnp.exp2(m_old[r] - m_new[r]) for r in hs]
    pv = [jnp.dot(vt, p[r], preferred_element_type=F32) for r in hs]
    ones = jnp.ones((8, k.shape[0]), BF16)
    psum = [jnp.dot(ones, p[r], preferred_element_type=F32)[0:1] for r in hs]
    for r in hs:
        m_ref[r] = m_new[r]
        l_ref[r] = alpha[r] * l_ref[r] + psum[r]
        acc_ref[r] = acc_ref[r] * alpha[r] + pv[r]


def _nsa_select_t_kernel(qi_ref, kj_ref, q_ref, k_ref, vt_ref, sel_ref, o_ref, m_ref, l_ref, acc_ref,
                         *, tq, kb, rep, dh):
    pair = pl.program_id(2)
    qi = qi_ref[pair]
    kj = kj_ref[pair]

    @pl.when(kj == 0)
    def _():
        _flash_t_init(m_ref, l_ref, acc_ref)

    kpos = kj * kb + lax.broadcasted_iota(jnp.int32, (kb, tq), 0)
    t = qi * tq + lax.broadcasted_iota(jnp.int32, (kb, tq), 1)
    per = kb // NSA_SEL_BLOCK
    chosen = jnp.zeros((kb, tq), F32)
    for i in range(per):
        row = sel_ref[0, 0, pl.ds(kj * per + i, 1), :]
        chosen = jnp.where((kpos - kj * kb) // NSA_SEL_BLOCK == i, row, chosen)
    mask = (chosen > 0) & (kpos <= t)
    _flash_t_step(q_ref, k_ref[0], vt_ref[0], mask, m_ref, l_ref, acc_ref, rep, dh)

    @pl.when(kj * kb + kb > qi * tq + tq - 1)
    def _():
        for r in range(rep):
            l = l_ref[r]
            o_ref[0, r * dh:(r + 1) * dh, :] = acc_ref[r] / jnp.where(l > 0, l, 1.0)


def _nsa_window_t_kernel(q_ref, k_ref, vt_ref, oc_ref, os_ref, g_ref, o_ref, m_ref, l_ref, acc_ref,
                         *, tq, kb, rep, dh, window, n_steps):
    qi = pl.program_id(2)
    w = pl.program_id(3)
    kblk = qi * (tq // kb) - (n_steps - tq // kb) + w

    @pl.when(w == 0)
    def _():
        _flash_t_init(m_ref, l_ref, acc_ref)

    @pl.when(kblk >= 0)
    def _():
        kpos = kblk * kb + lax.broadcasted_iota(jnp.int32, (kb, tq), 0)
        t = qi * tq + lax.broadcasted_iota(jnp.int32, (kb, tq), 1)
        mask = (kpos <= t) & (kpos > t - window)
        _flash_t_step(q_ref, k_ref[0], vt_ref[0], mask, m_ref, l_ref, acc_ref, rep, dh)

    @pl.when(w == n_steps - 1)
    def _():
        gates = g_ref[0, 0]
        for r in range(rep):
            rows = slice(r * dh, (r + 1) * dh)
            l = l_ref[r]
            o_w = acc_ref[r] / jnp.where(l > 0, l, 1.0)
            o = (gates[3 * r:3 * r + 1, :] * oc_ref[0, rows, :] + gates[3 * r + 1:3 * r + 2, :] * os_ref[0, rows, :]
                 + gates[3 * r + 2:3 * r + 3, :] * o_w)
            o_ref[0, :, rows] = o.T.astype(o_ref.dtype)


def nsa_mixer_t(u, h, w, bsz, s_len):
    t, d = u.shape
    dh, groups = NSA_HEAD_DIM, NSA_N_KV
    n_heads = d // dh
    rep = n_heads // groups
    kvw = groups * dh
    qw = n_heads * dh
    main_w = qw + 6 * kvw
    scale = dh ** -0.5
    tq = kb = min(128, s_len)
    nq = s_len // tq
    n_sel = s_len // NSA_SEL_BLOCK
    topn = min(NSA_TOPK, n_sel)
    w_in = w["nsa_w_in"]
    proj = matmul_ws(u, [(w_in, 0)], main_w, name="nsa_in").reshape(bsz, s_len, main_w)
    gates = matmul(u, [(w_in[:, main_w:].astype(BF16), 0)], w_in.shape[1] - main_w, epilogue=_ep_sigmoid,
                   name="nsa_gates")
    gates = jnp.transpose(gates.reshape(bsz, s_len, groups, rep * 3), (0, 2, 3, 1))
    slot = lambda j: (qw + j * kvw) // dh
    qvt = nsa_rope_t(proj, list(range(n_heads)) + [slot(3) + g for g in range(groups)]
                     + [slot(5) + g for g in range(groups)], n_heads, dh, scale * math.log2(math.e), group=groups)
    k_rot = nsa_rope(proj, 0, [slot(2) + g for g in range(groups)] + [slot(4) + g for g in range(groups)], dh, 1.0)
    kc = nsa_compress(proj[..., qw:qw + kvw], w["nsa_cmp_pos_k"], w["nsa_cmp_k_w1"], w["nsa_cmp_k_w2"],
                      bsz, s_len, groups, dh, True)
    vct = nsa_compress(proj[..., qw + kvw:qw + 2 * kvw], w["nsa_cmp_pos_v"], w["nsa_cmp_v_w1"], w["nsa_cmp_v_w2"],
                       bsz, s_len, groups, dh, False, transpose_out=True)
    nc = kc.shape[1]
    cs = jnp.arange(nc)[None, :] * NSA_CMP_STRIDE
    ss = jnp.arange(n_sel)[:, None] * NSA_SEL_BLOCK
    overlap_t = jnp.clip(jnp.minimum(cs + NSA_CMP_BLOCK, ss + NSA_SEL_BLOCK) - jnp.maximum(cs, ss), 0, None)
    overlap_t = overlap_t.astype(F32) / NSA_CMP_BLOCK

    qt_spec3 = pl.BlockSpec((1, rep * dh, tq), lambda b_, g, i: (b_, g, i))
    o_c, sel = pl.pallas_call(
        functools.partial(_nsa_cmp_select_t_kernel, tq=tq, rep=rep, dh=dh, topn=topn),
        grid=(bsz, groups, nq),
        in_specs=[qt_spec3,
                  pl.BlockSpec((1, nc, dh), lambda b_, g, i: (b_ * groups + g, 0, 0)),
                  pl.BlockSpec((1, dh, nc), lambda b_, g, i: (b_ * groups + g, 0, 0)),
                  pl.BlockSpec((n_sel, nc), lambda b_, g, i: (0, 0))],
        out_specs=[qt_spec3, pl.BlockSpec((1, 1, n_sel, tq), lambda b_, g, i: (b_, g, 0, i))],
        out_shape=[jax.ShapeDtypeStruct((bsz, qw, s_len), F32),
                   jax.ShapeDtypeStruct((bsz, groups, n_sel, s_len), F32)],
        compiler_params=_params("parallel", "parallel", "parallel"),
        name="nsa_cmp_select",
    )(qvt, kc, vct, overlap_t)

    flash_scratch = lambda n: [pltpu.VMEM((rep, 1, n), F32), pltpu.VMEM((rep, 1, n), F32),
                               pltpu.VMEM((rep, dh, n), F32)]
    tqs = min(2 * tq, s_len)
    kb = tqs
    pairs = [(i, j) for i in range(s_len // tqs) for j in range((i * tqs + tqs - 1) // kb + 1)]
    qi_of = jnp.asarray([pr[0] for pr in pairs], jnp.int32)
    kj_of = jnp.asarray([pr[1] for pr in pairs], jnp.int32)
    o_s = pl.pallas_call(
        functools.partial(_nsa_select_t_kernel, tq=tqs, kb=kb, rep=rep, dh=dh),
        grid_spec=pltpu.PrefetchScalarGridSpec(
            num_scalar_prefetch=2,
            grid=(bsz, groups, len(pairs)),
            in_specs=[pl.BlockSpec((1, rep * dh, tqs), lambda b_, g, pr, qi, kj: (b_, g, qi[pr])),
                      pl.BlockSpec((1, kb, dh), lambda b_, g, pr, qi, kj: (b_, kj[pr], g)),
                      pl.BlockSpec((1, dh, kb), lambda b_, g, pr, qi, kj: (b_, n_heads + g, kj[pr])),
                      pl.BlockSpec((1, 1, n_sel, tqs), lambda b_, g, pr, qi, kj: (b_, g, 0, qi[pr]))],
            out_specs=pl.BlockSpec((1, rep * dh, tqs), lambda b_, g, pr, qi, kj: (b_, g, qi[pr])),
            scratch_shapes=flash_scratch(tqs)),
        out_shape=jax.ShapeDtypeStruct((bsz, qw, s_len), F32),
        compiler_params=_params("parallel", "parallel", "arbitrary"),
        name="nsa_select_attn",
    )(qi_of, kj_of, qvt, k_rot, qvt, sel)

    n_steps = -(-NSA_WINDOW // kb) + tqs // kb
    win_blk = lambda i, j: jnp.maximum(i * (tqs // kb) - (n_steps - tqs // kb) + j, 0)
    qt_spec = pl.BlockSpec((1, rep * dh, tqs), lambda b_, g, i, j: (b_, g, i))
    o = pl.pallas_call(
        functools.partial(_nsa_window_t_kernel, tq=tqs, kb=kb, rep=rep, dh=dh, window=NSA_WINDOW, n_steps=n_steps),
        grid=(bsz, groups, s_len // tqs, n_steps),
        in_specs=[qt_spec,
                  pl.BlockSpec((1, kb, dh), lambda b_, g, i, j: (b_, win_blk(i, j), groups + g)),
                  pl.BlockSpec((1, dh, kb), lambda b_, g, i, j: (b_, n_heads + groups + g, win_blk(i, j))),
                  qt_spec, qt_spec,
                  pl.BlockSpec((1, 1, rep * 3, tqs), lambda b_, g, i, j: (b_, g, 0, i))],
        out_specs=pl.BlockSpec((1, tqs, rep * dh), lambda b_, g, i, j: (b_, i, g)),
        out_shape=jax.ShapeDtypeStruct((bsz, s_len, qw), BF16),
        scratch_shapes=flash_scratch(tqs),
        compiler_params=_params("parallel", "parallel", "parallel", "arbitrary"),
        name="nsa_window_attn",
    )(qvt, k_rot, qvt, o_c, o_s, gates)
    return matmul_ws(o.reshape(t, qw), [(w["nsa_w_out"], 0)], d, epilogue=_ep_residual, extras=[(h, "mn")],
                     name="nsa_out")


_MATMUL_WEIGHTS = ("pl_proj", "rw_w1", "rw_w2", "rw_a1", "rw_a2", "rw_g1", "rw_g2")


def kernel(x, p, norm_mix, norm_ffn, norm_pl, pl_proj, pl_gate, norm_final, mb_w_in, mb_conv_w, mb_conv_b, mb_dt_bias, mb_a_log, mb_d_skip, mb_norm_w, mb_w_out, nsa_w_in, nsa_cmp_pos_k, nsa_cmp_pos_v, nsa_cmp_k_w1, nsa_cmp_k_w2, nsa_cmp_v_w1, nsa_cmp_v_w2, nsa_w_out, hg_w_in, hg_lb_logits, hg_norm_w, hg_w_out, rw_mu, rw_w_rkv, rw_w0, rw_w1, rw_w2, rw_a0, rw_a1, rw_a2, rw_g1, rw_g2, rw_k_k, rw_k_a, rw_r_k, rw_ln_w, rw_ln_b, rw_w_out, ffn0_w_in, ffn0_w_out, moe1_router, moe1_w_in, moe1_w_out, ffn2_w_in, ffn2_w_out, moe3_router, moe3_w_in, moe3_w_out):
    w = dict(locals())
    for name in _MATMUL_WEIGHTS:
        w[name] = w[name].astype(BF16)
    bsz, s_len, d = x.shape
    depth = p.shape[0]
    t = bsz * s_len
    lb_all = jax.nn.softmax(hg_lb_logits.astype(F32), axis=0)
    lb_all = jnp.cumsum(lb_all, axis=0) - lb_all[0]
    dense = [(w["ffn0_w_in"], w["ffn0_w_out"]), (w["ffn2_w_in"], w["ffn2_w_out"])]
    moe = [(moe1_router, w["moe1_w_in"], w["moe1_w_out"]), (moe3_router, w["moe3_w_in"], w["moe3_w_out"])]
    p_bf = p.reshape(depth, t, p.shape[-1])
    h = x.reshape(t, d)
    for i in range(depth):
        kind = i % 4
        if kind == 0:
            h = mamba2_mixer(rmsnorm(h, norm_mix[i]), h, w, bsz, s_len)
        elif kind == 1:
            h = nsa_mixer_t(rmsnorm(h, norm_mix[i]), h, w, bsz, s_len)
        elif kind == 2:
            h = hgrn2_mixer(rmsnorm(h, norm_mix[i]), h, w, lb_all[i], bsz, s_len)
        else:
            h = rwkv7_mixer(rmsnorm(h, norm_mix[i], out_dtype=F32), h, w, bsz, s_len)
        v = rmsnorm(h, norm_ffn[i])
        if i % 2 == 0:
            h = dense_ffn(v, h, *dense[i // 2])
        else:
            h = moe_ffn_routed(v, h, *moe[i // 2])
        h = ple_gate(h, p_bf[i], norm_pl[i], w["pl_proj"][i], pl_gate, i)
    return rmsnorm(h, norm_final, out_dtype=F32).reshape(bsz, s_len, d)
```

```python
import functools
import math

import jax
import jax.numpy as jnp
from jax import lax
from jax.experimental import pallas as pl
from jax.experimental.pallas import tpu as pltpu

F32 = jnp.float32
BF16 = jnp.bfloat16

NORM_EPS = 1e-6
ROPE_THETA = 10000.0

V7X_VMEM_BYTES = 64 * 1024 * 1024
VMEM_LIMIT_BYTES = V7X_VMEM_BYTES - 8 * 1024 * 1024
LANES = 128

MB_D_STATE = 128
MB_CHUNK = 128

NSA_HEAD_DIM = 128
NSA_N_KV = 4
NSA_CMP_BLOCK = 32
NSA_CMP_STRIDE = 16
NSA_SEL_BLOCK = 64
NSA_TOPK = 16
NSA_WINDOW = 512
NSA_FORCED_SCORE = 1e9

HG_HEAD_DIM = 128
HG_CHUNK = 32

RW_HEAD_DIM = 64
RW_LN_EPS = 64e-5
RW_CHUNK = 128

MOE_TOPK = 2


def _params(*semantics):
    return pltpu.CompilerParams(dimension_semantics=semantics, vmem_limit_bytes=VMEM_LIMIT_BYTES)


def _pick(n, target):
    if n <= target:
        return n
    for c in range(target, 0, -1):
        if n % c == 0:
            return c
    return n


def _silu(x):
    return x * jax.nn.sigmoid(x)


def _rmsnorm_kernel(x_ref, g_ref, o_ref):
    x = x_ref[...]
    ms = jnp.mean(x * x, axis=-1, keepdims=True)
    o_ref[...] = (x * lax.rsqrt(ms + NORM_EPS) * g_ref[...]).astype(o_ref.dtype)


def rmsnorm(x, gain, out_dtype=BF16, name="rmsnorm"):
    m, d = x.shape
    bm = _pick(m, 256)
    return pl.pallas_call(
        _rmsnorm_kernel,
        grid=(m // bm,),
        in_specs=[pl.BlockSpec((bm, d), lambda i: (i, 0)), pl.BlockSpec((1, d), lambda i: (0, 0))],
        out_specs=pl.BlockSpec((bm, d), lambda i: (i, 0)),
        out_shape=jax.ShapeDtypeStruct((m, d), out_dtype),
        compiler_params=_params("parallel"),
        name=name,
    )(x, gain.reshape(1, d).astype(F32))


def _mm_kernel(*refs, n_w, n_extra, nk, epilogue):
    x_ref = refs[0]
    w_refs = refs[1:1 + n_w]
    e_refs = refs[1 + n_w:1 + n_w + n_extra]
    o_ref = refs[1 + n_w + n_extra]
    acc_refs = refs[2 + n_w + n_extra:]
    x = x_ref[...]
    if nk == 1:
        accs = [jnp.dot(x, w[...], preferred_element_type=F32) for w in w_refs]
        o_ref[...] = epilogue(accs, [e[...] for e in e_refs]).astype(o_ref.dtype)
        return
    k = pl.program_id(2)

    @pl.when(k == 0)
    def _():
        for a in acc_refs:
            a[...] = jnp.zeros_like(a)

    for a, w in zip(acc_refs, w_refs):
        a[...] += jnp.dot(x, w[...], preferred_element_type=F32)

    @pl.when(k == nk - 1)
    def _():
        o_ref[...] = epilogue([a[...] for a in acc_refs], [e[...] for e in e_refs]).astype(o_ref.dtype)


def _first(accs, extras):
    return accs[0]


def matmul(x, ws, n_out, *, epilogue=_first, extras=(), out_dtype=F32, bm=1024, bn=512, bk=None, name="matmul"):
    m, kdim = x.shape
    bm = _pick(m, bm)
    bn = _pick(n_out, bn)
    if bk is None:
        bk = kdim if kdim <= 4096 else _pick(kdim, 4096)
    nk = kdim // bk
    assert kdim % bk == 0 and m % bm == 0 and n_out % bn == 0
    in_specs = [pl.BlockSpec((bm, bk), lambda i, j, k: (i, k))]
    args = [x]
    for w, off in ws:
        assert off % bn == 0 and w.shape[0] == kdim
        in_specs.append(pl.BlockSpec((bk, bn), functools.partial(lambda i, j, k, o: (k, j + o), o=off // bn)))
        args.append(w)
    for arr, kind in extras:
        if kind == "mn":
            in_specs.append(pl.BlockSpec((bm, bn), lambda i, j, k: (i, j)))
        elif kind == "m":
            in_specs.append(pl.BlockSpec((bm, arr.shape[1]), lambda i, j, k: (i, 0)))
        elif kind == "kn":
            in_specs.append(pl.BlockSpec((arr.shape[0], bn), lambda i, j, k: (0, j)))
        else:
            in_specs.append(pl.BlockSpec((1, bn), lambda i, j, k: (0, j)))
        args.append(arr)
    scratch = [pltpu.VMEM((bm, bn), F32) for _ in ws] if nk > 1 else []
    kern = functools.partial(_mm_kernel, n_w=len(ws), n_extra=len(extras), nk=nk, epilogue=epilogue)
    return pl.pallas_call(
        kern,
        grid=(m // bm, n_out // bn, nk),
        in_specs=in_specs,
        out_specs=pl.BlockSpec((bm, bn), lambda i, j, k: (i, j)),
        out_shape=jax.ShapeDtypeStruct((m, n_out), out_dtype),
        scratch_shapes=scratch,
        compiler_params=_params("parallel", "parallel", "arbitrary"),
        name=name,
    )(*args)


WS_CAST_CHUNK = 512


def _mm_ws_kernel(*refs, n_w, n_extra, epilogue):
    x_ref = refs[0]
    w_refs = refs[1:1 + n_w]
    e_refs = refs[1 + n_w:1 + n_w + n_extra]
    o_ref = refs[1 + n_w + n_extra]
    wb_refs = refs[2 + n_w + n_extra:]

    kdim = x_ref.shape[1]
    ck = _pick(kdim, WS_CAST_CHUNK)

    @pl.when(pl.program_id(1) == 0)
    def _():
        accs = [None] * n_w
        for c in range(kdim // ck):
            rows = slice(c * ck, (c + 1) * ck)
            xc = x_ref[:, rows]
            for n, (w, wb) in enumerate(zip(w_refs, wb_refs)):
                wc = (w[0, rows, :] if len(w.shape) == 3 else w[rows, :]).astype(BF16)
                wb[rows, :] = wc
                part = jnp.dot(xc, wc, preferred_element_type=F32)
                accs[n] = part if accs[n] is None else accs[n] + part
        o_ref[...] = epilogue(accs, [e[...] for e in e_refs]).astype(o_ref.dtype)

    @pl.when(pl.program_id(1) != 0)
    def _():
        x = x_ref[...]
        accs = [jnp.dot(x, wb[...], preferred_element_type=F32) for wb in wb_refs]
        o_ref[...] = epilogue(accs, [e[...] for e in e_refs]).astype(o_ref.dtype)


def matmul_ws(x, ws, n_out, *, epilogue=_first, extras=(), out_dtype=F32, bm=1024, bn=512, w_buffers=2,
              name="matmul_ws"):
    m, kdim = x.shape
    bm = _pick(m, bm)
    bn = _pick(n_out, bn)
    assert m % bm == 0 and n_out % bn == 0
    mode = {} if w_buffers == 2 else {"pipeline_mode": pl.Buffered(w_buffers)}
    in_specs = [pl.BlockSpec((bm, kdim), lambda j, i: (i, 0))]
    args = [x]
    for w, off in ws:
        if w.ndim == 3:
            e, o = off
            assert o % bn == 0 and w.shape[1] == kdim
            in_specs.append(pl.BlockSpec((1, kdim, bn), functools.partial(lambda j, i, e_, o_: (e_, 0, j + o_),
                                                                          e_=e, o_=o // bn), **mode))
        else:
            assert off % bn == 0 and w.shape[0] == kdim
            in_specs.append(pl.BlockSpec((kdim, bn), functools.partial(lambda j, i, o_: (0, j + o_), o_=off // bn),
                                         **mode))
        args.append(w)
    for arr, kind in extras:
        if kind == "mn":
            in_specs.append(pl.BlockSpec((bm, bn), lambda j, i: (i, j)))
        elif kind == "m":
            in_specs.append(pl.BlockSpec((bm, arr.shape[1]), lambda j, i: (i, 0)))
        elif kind == "kn":
            in_specs.append(pl.BlockSpec((arr.shape[0], bn), lambda j, i: (0, j)))
        else:
            in_specs.append(pl.BlockSpec((1, bn), lambda j, i: (0, j)))
        args.append(arr)
    kern = functools.partial(_mm_ws_kernel, n_w=len(ws), n_extra=len(extras), epilogue=epilogue)
    return pl.pallas_call(
        kern,
        grid=(n_out // bn, m // bm),
        in_specs=in_specs,
        out_specs=pl.BlockSpec((bm, bn), lambda j, i: (i, j)),
        out_shape=jax.ShapeDtypeStruct((m, n_out), out_dtype),
        scratch_shapes=[pltpu.VMEM((kdim, bn), BF16) for _ in ws],
        compiler_params=_params("parallel", "arbitrary"),
        name=name,
    )(*args)


def _ep_residual(accs, extras):
    return extras[0] + accs[0]


def _ep_swiglu(accs, extras):
    return _silu(accs[0]) * accs[1]


def _ep_tanh(accs, extras):
    return jnp.tanh(accs[0])


def _ep_sigmoid(accs, extras):
    return jax.nn.sigmoid(accs[0])


def _ep_bias_sigmoid(accs, extras):
    return jax.nn.sigmoid(accs[0] + extras[0])


def _ep_rw_logdecay(accs, extras):
    w = -jax.nn.softplus(-(accs[0] + extras[0])) - 0.5
    return -jnp.exp(w)


def _ep_ple_gate(accs, extras):
    pp = jnp.dot(extras[1].astype(BF16), extras[2], preferred_element_type=F32)
    return extras[0] + pp * jax.nn.sigmoid(accs[0])


def _conv_silu_kernel(x_ref, w_ref, b_ref, o_ref, *, k_width):
    x = x_ref[0]
    row = lax.broadcasted_iota(jnp.int32, x.shape, 0)
    y = b_ref[...] + w_ref[k_width - 1:k_width, :] * x
    for j in range(k_width - 1):
        shift = k_width - 1 - j
        xs = jnp.where(row >= shift, pltpu.roll(x, shift, 0), 0.0)
        y = y + w_ref[j:j + 1, :] * xs
    o_ref[0] = _silu(y)


def conv_silu(x, w, b):
    bsz, s_len, c = x.shape
    cb = _pick(c, 256)
    k_width = w.shape[0]
    return pl.pallas_call(
        functools.partial(_conv_silu_kernel, k_width=k_width),
        grid=(bsz, c // cb),
        in_specs=[pl.BlockSpec((1, s_len, cb), lambda b_, j: (b_, 0, j)),
                  pl.BlockSpec((k_width, cb), lambda b_, j: (0, j)),
                  pl.BlockSpec((1, cb), lambda b_, j: (0, j))],
        out_specs=pl.BlockSpec((1, s_len, cb), lambda b_, j: (b_, 0, j)),
        out_shape=jax.ShapeDtypeStruct(x.shape, F32),
        compiler_params=_params("parallel", "parallel"),
        name="mamba_conv_silu",
    )(x, w, b.reshape(1, c))


def _cumsum_rows(x, n):
    row = lax.broadcasted_iota(jnp.int32, x.shape, 0)
    s = 1
    while s < n:
        x = x + jnp.where(row >= s, pltpu.roll(x, s, 0), 0.0)
        s *= 2
    return x


def _cumsum_lanes(x, n):
    col = lax.broadcasted_iota(jnp.int32, x.shape, 1)
    s = 1
    while s < n:
        x = x + jnp.where(col >= s, pltpu.roll(x, s, 1), 0.0)
        s *= 2
    return x


def _dot_nt(a, b):
    return lax.dot_general(a, b, (((1,), (1,)), ((), ())), preferred_element_type=F32)


def _dot_tn(a, b):
    return lax.dot_general(a, b, (((0,), (0,)), ((), ())), preferred_element_type=F32)


def _ssd_kernel(xs_ref, b_ref, c_ref, z_ref, dt_ref, dtt_ref, bias_r_ref, bias_c_ref, alog_r_ref, alog_c_ref,
                dskip_ref, normw_ref, o_ref, state_ref, y_ref, *, chunk, heads, p_dim):
    @pl.when(pl.program_id(2) == 0)
    def _():
        state_ref[...] = jnp.zeros_like(state_ref)

    dt = jax.nn.softplus(dt_ref[0, 0] + bias_r_ref[0])
    dtt = jax.nn.softplus(dtt_ref[0, 0] + bias_c_ref[0])
    a_cum = _cumsum_rows(dt * -jnp.exp(alog_r_ref[0]), chunk)
    a_cum_t = _cumsum_lanes(dtt * -jnp.exp(alog_c_ref[0]), chunk)
    xs = xs_ref[0]
    bmat = b_ref[0]
    cmat = c_ref[0].astype(BF16)
    cb = _dot_nt(cmat, bmat.astype(BF16))
    b_t = bmat.T.astype(BF16)
    li = lax.broadcasted_iota(jnp.int32, (chunk, chunk), 0)
    si = lax.broadcasted_iota(jnp.int32, (chunk, chunk), 1)
    causal = li >= si
    per = LANES // p_dim
    lane_seg = lax.broadcasted_iota(jnp.int32, (1, LANES), 1) // p_dim

    def pick(vals):
        out = vals[-1]
        for i in range(per - 2, -1, -1):
            out = jnp.where(lane_seg == i, vals[i], out)
        return out

    dot = functools.partial(jnp.dot, preferred_element_type=F32)
    es = range(heads)
    tiles = range(heads // per)
    head_row = lax.broadcasted_iota(jnp.int32, (heads, heads * LANES), 0)
    to_tile = jnp.where(lax.broadcasted_iota(jnp.int32, (heads, heads * LANES), 1) // LANES == head_row, 1.0, 0.0)
    cum_t = _dot_hi(a_cum, to_tile)
    of = lambda vals, i: [vals[i * per + j] for j in range(per)]
    tile = lambda x, i: x[:, i * LANES:(i + 1) * LANES]
    cum_c = jnp.concatenate([pick([tile(cum_t, e) for e in of(es, i)]) for i in tiles], axis=-1)
    dt_c = jnp.concatenate([pick([dt[:, e:e + 1] for e in of(es, i)]) for i in tiles], axis=-1)
    last_c = cum_c[chunk - 1:chunk, :]
    m = [(cb * jnp.exp(jnp.where(causal, tile(cum_t, e) - a_cum_t[e:e + 1, :], -jnp.inf))).astype(BF16) for e in es]
    xdt = xs * dt_c
    xdt_b = xdt.astype(BF16)
    xend_b = (xdt * jnp.exp(last_c - cum_c)).astype(BF16)
    grow = jnp.exp(cum_c)
    st_decay = jnp.exp(last_c)
    st = [state_ref[i] for i in tiles]
    y_in = [pick([dot(m[e], tile(xdt_b, i)) for e in of(es, i)]) for i in tiles]
    y_st = [dot(cmat, st[i].astype(BF16)) * tile(grow, i) for i in tiles]
    for i in tiles:
        state_ref[i] = st[i] * tile(st_decay, i) + dot(b_t, tile(xend_b, i))
        y_ref[:, i * LANES:(i + 1) * LANES] = y_in[i] + y_st[i]
    y = y_ref[...] + xs * dskip_ref[...]
    y = y * _silu(z_ref[0])
    ms = jnp.mean(y * y, axis=-1, keepdims=True)
    o_ref[0] = (y * lax.rsqrt(ms + NORM_EPS) * normw_ref[...]).astype(o_ref.dtype)


def ssd_scan(xbc, z, dt, dt_bias, a_log, d_skip, norm_w, *, chunk=MB_CHUNK):
    bsz, s_len, d_inner = z.shape
    n_heads = dt.shape[-1]
    n_state = MB_D_STATE
    groups = (xbc.shape[-1] - d_inner) // (2 * n_state)
    heads = n_heads // groups
    p_dim = d_inner // n_heads
    gw = heads * p_dim
    assert gw % LANES == 0 and d_inner % n_state == 0
    chunk = min(chunk, s_len)
    nc = s_len // chunk
    b_off = d_inner // n_state
    c_off = b_off + groups
    dt_g = jnp.transpose(dt.reshape(bsz, s_len, groups, heads), (0, 2, 1, 3))
    dt_gt = jnp.transpose(dt_g, (0, 1, 3, 2))
    kern = functools.partial(_ssd_kernel, chunk=chunk, heads=heads, p_dim=p_dim)
    per_group = lambda b_, g, c: (g, 0, 0)
    return pl.pallas_call(
        kern,
        grid=(bsz, groups, nc),
        in_specs=[pl.BlockSpec((1, chunk, gw), lambda b_, g, c: (b_, c, g)),
                  pl.BlockSpec((1, chunk, n_state), lambda b_, g, c: (b_, c, b_off + g)),
                  pl.BlockSpec((1, chunk, n_state), lambda b_, g, c: (b_, c, c_off + g)),
                  pl.BlockSpec((1, chunk, gw), lambda b_, g, c: (b_, c, g)),
                  pl.BlockSpec((1, 1, chunk, heads), lambda b_, g, c: (b_, g, c, 0)),
                  pl.BlockSpec((1, 1, heads, chunk), lambda b_, g, c: (b_, g, 0, c)),
                  pl.BlockSpec((1, 1, heads), per_group),
                  pl.BlockSpec((1, heads, 1), per_group),
                  pl.BlockSpec((1, 1, heads), per_group),
                  pl.BlockSpec((1, heads, 1), per_group),
                  pl.BlockSpec((1, gw), lambda b_, g, c: (0, g)),
                  pl.BlockSpec((1, gw), lambda b_, g, c: (0, g))],
        out_specs=pl.BlockSpec((1, chunk, gw), lambda b_, g, c: (b_, c, g)),
        out_shape=jax.ShapeDtypeStruct(z.shape, BF16),
        scratch_shapes=[pltpu.VMEM((gw // LANES, n_state, LANES), F32), pltpu.VMEM((chunk, gw), F32)],
        compiler_params=_params("parallel", "parallel", "arbitrary"),
        name="mamba_ssd",
    )(xbc, xbc, xbc, z, dt_g, dt_gt,
      dt_bias.reshape(groups, 1, heads), dt_bias.reshape(groups, heads, 1),
      a_log.reshape(groups, 1, heads), a_log.reshape(groups, heads, 1),
      jnp.repeat(d_skip, p_dim).reshape(1, d_inner), norm_w.reshape(1, d_inner))


def mamba2_mixer(u, h, w, bsz, s_len):
    d_inner = w["mb_w_out"].shape[0]
    n_heads = w["mb_dt_bias"].shape[0]
    w_in = w["mb_w_in"]
    xbc_w = w_in.shape[1] - d_inner - n_heads
    z = matmul_ws(u, [(w_in, 0)], d_inner, name="mb_in_z")
    xbc = matmul_ws(u, [(w_in, d_inner)], xbc_w, name="mb_in_xbc")
    dt = matmul_ws(u, [(w_in, d_inner + xbc_w)], n_heads, name="mb_in_dt")
    xbc = conv_silu(xbc.reshape(bsz, s_len, xbc_w), w["mb_conv_w"], w["mb_conv_b"])
    y = ssd_scan(xbc, z.reshape(bsz, s_len, d_inner), dt.reshape(bsz, s_len, n_heads),
                 w["mb_dt_bias"], w["mb_a_log"], w["mb_d_skip"], w["mb_norm_w"])
    return matmul_ws(y.reshape(bsz * s_len, d_inner), [(w["mb_w_out"], 0)], h.shape[1],
                     epilogue=_ep_residual, extras=[(h, "mn")], bm=512, w_buffers=1, name="mb_out")


def _seg_cumsum_rows(x, seg, reverse=False):
    n = x.shape[0]
    pos = lax.broadcasted_iota(jnp.int32, x.shape, 0) % seg
    s = 1
    while s < seg:
        if reverse:
            x = x + jnp.where(pos < seg - s, pltpu.roll(x, n - s, 0), 0.0)
        else:
            x = x + jnp.where(pos >= s, pltpu.roll(x, s, 0), 0.0)
        s *= 2
    return x


def _hgrn_kernel(q_ref, f_ref, i_ref, g_ref, lb_ref, nw_ref, o_ref, state_ref, *, sub, n_sub, heads, dk):
    @pl.when(pl.program_id(2) == 0)
    def _():
        state_ref[...] = jnp.zeros_like(state_ref)

    lb = lb_ref[...]
    nw = nw_ref[...]
    ti = lax.broadcasted_iota(jnp.int32, (sub, sub), 0)
    si = lax.broadcasted_iota(jnp.int32, (sub, sub), 1)
    causal = ti >= si
    f = lb + (1.0 - lb) * jax.nn.sigmoid(f_ref[0])
    lf = jnp.log(f)
    k = 1.0 - f
    b = _seg_cumsum_rows(lf, sub)
    to_end = _seg_cumsum_rows(lf, sub, reverse=True) - lf
    q_dec = (_silu(q_ref[0]) * jnp.exp(b)).astype(BF16)
    k_dec = (k * jnp.exp(-b)).astype(BF16)
    k_end = (k * jnp.exp(to_end)).astype(BF16)
    v = i_ref[0].astype(BF16)
    cs = range(n_sub)
    hs = range(heads)
    blk = lambda x, c, h: x[c * sub:(c + 1) * sub, h * dk:(h + 1) * dk]
    scores = [[jnp.where(causal, _dot_nt(blk(q_dec, c, h), blk(k_dec, c, h)), 0.0).astype(BF16) for h in hs]
              for c in cs]
    upd = [[_dot_tn(blk(v, c, h), blk(k_end, c, h)) for h in hs] for c in cs]
    states = []
    st = [state_ref[h] for h in hs]
    for c in cs:
        states.append(st)
        decay = jnp.exp(b[(c + 1) * sub - 1:(c + 1) * sub, :])
        st = [st[h] * decay[:, h * dk:(h + 1) * dk] + upd[c][h] for h in hs]
    for h in hs:
        state_ref[h] = st[h]
    for c in cs:
        rows = slice(c * sub, (c + 1) * sub)
        for h in hs:
            o = (jnp.dot(scores[c][h], blk(v, c, h), preferred_element_type=F32)
                 + _dot_nt(blk(q_dec, c, h), states[c][h].astype(BF16)))
            o = o * lax.rsqrt(jnp.mean(o * o, axis=-1, keepdims=True) + NORM_EPS) * nw
            cols = slice(h * dk, (h + 1) * dk)
            o_ref[0, rows, cols] = (o * _silu(g_ref[0, rows, cols])).astype(o_ref.dtype)


def hgrn2_scan(proj, lower_bound, norm_w, *, dk=HG_HEAD_DIM, sub=HG_CHUNK, tb=256, heads=8):
    bsz, s_len, d4 = proj.shape
    d = d4 // 4
    n_heads = d // dk
    tb = min(tb, s_len)
    heads = min(heads, n_heads)
    hw = heads * dk
    n_hb = n_heads // heads
    kern = functools.partial(_hgrn_kernel, sub=sub, n_sub=tb // sub, heads=heads, dk=dk)
    spec = lambda part: pl.BlockSpec((1, tb, hw), lambda b_, h_, t: (b_, t, part * n_hb + h_))
    return pl.pallas_call(
        kern,
        grid=(bsz, n_hb, s_len // tb),
        in_specs=[spec(0), spec(1), spec(2), spec(3),
                  pl.BlockSpec((1, hw), lambda b_, h_, t: (0, h_)),
                  pl.BlockSpec((1, dk), lambda b_, h_, t: (0, 0))],
        out_specs=pl.BlockSpec((1, tb, hw), lambda b_, h_, t: (b_, t, h_)),
        out_shape=jax.ShapeDtypeStruct((bsz, s_len, d), BF16),
        scratch_shapes=[pltpu.VMEM((heads, dk, dk), F32)],
        compiler_params=_params("parallel", "parallel", "arbitrary"),
        name="hgrn2_scan",
    )(proj, proj, proj, proj, lower_bound.reshape(1, d), norm_w.reshape(1, dk))


def hgrn2_mixer(u, h, w, lower_bound, bsz, s_len):
    d = h.shape[1]
    proj = matmul_ws(u, [(w["hg_w_in"], 0)], 4 * d, bm=512, bn=1024, name="hg_in")
    o = hgrn2_scan(proj.reshape(bsz, s_len, 4 * d), lower_bound, w["hg_norm_w"])
    return matmul_ws(o.reshape(bsz * s_len, d), [(w["hg_w_out"], 0)], d,
                     epilogue=_ep_residual, extras=[(h, "mn")], name="hg_out")


def dense_ffn(v, h, w_in, w_out):
    f = w_out.shape[0]
    hid = matmul_ws(v, [(w_in, 0), (w_in, f)], f, epilogue=_ep_swiglu, out_dtype=BF16, bm=1024, bn=256,
                    name="ffn_in")
    return matmul_ws(hid, [(w_out, 0)], h.shape[1], epilogue=_ep_residual, extras=[(h, "mn")], bm=512, w_buffers=1,
                     name="ffn_out")


def _rmsnorm_router_kernel(x_ref, g_ref, r_ref, o_ref, c_ref, *, n_experts):
    x = x_ref[...]
    ms = jnp.mean(x * x, axis=-1, keepdims=True)
    v = (x * lax.rsqrt(ms + NORM_EPS) * g_ref[...]).astype(BF16)
    o_ref[...] = v
    logits = jnp.dot(v, r_ref[...], preferred_element_type=F32)
    lane = lax.broadcasted_iota(jnp.int32, logits.shape, 1)
    logits = jnp.where(lane < n_experts, logits, -jnp.inf)
    m1 = jnp.max(logits, axis=-1, keepdims=True)
    i1 = jnp.min(jnp.where(logits == m1, lane, LANES), axis=-1, keepdims=True)
    rest = jnp.where(lane == i1, -jnp.inf, logits)
    m2 = jnp.max(rest, axis=-1, keepdims=True)
    i2 = jnp.min(jnp.where(rest == m2, lane, LANES), axis=-1, keepdims=True)
    e2 = jnp.exp(m2 - m1)
    w1 = 1.0 / (1.0 + e2)
    c_ref[...] = jnp.where(lane == i1, w1, 0.0) + jnp.where(lane == i2, e2 * w1, 0.0)


def rmsnorm_router(x, gain, router):
    m, d = x.shape
    n_experts = router.shape[1]
    r_pad = jnp.zeros((d, LANES), BF16).at[:, :n_experts].set(router.astype(BF16))
    bm = _pick(m, 256)
    return pl.pallas_call(
        functools.partial(_rmsnorm_router_kernel, n_experts=n_experts),
        grid=(m // bm,),
        in_specs=[pl.BlockSpec((bm, d), lambda i: (i, 0)), pl.BlockSpec((1, d), lambda i: (0, 0)),
                  pl.BlockSpec((d, LANES), lambda i: (0, 0))],
        out_specs=[pl.BlockSpec((bm, d), lambda i: (i, 0)), pl.BlockSpec((bm, LANES), lambda i: (i, 0))],
        out_shape=[jax.ShapeDtypeStruct((m, d), BF16), jax.ShapeDtypeStruct((m, LANES), F32)],
        compiler_params=_params("parallel"),
        name="rmsnorm_router",
    )(x, gain.reshape(1, d).astype(F32), r_pad)


MOE_BLOCK = 1024
MOE_UNIT = 32
MOE_TILE = 512


def _moe_gather_kernel(x_ref, tok_ref, o_ref):
    tok = tok_ref[0]
    lane = lax.broadcasted_iota(jnp.int32, (tok.shape[0], x_ref.shape[0]), 1)
    onehot = jnp.where(tok == lane, 1.0, 0.0).astype(BF16)
    o_ref[...] = jnp.dot(onehot, x_ref[...], preferred_element_type=F32).astype(o_ref.dtype)


def _moe_expert_in_kernel(src_ref, exp_ref, first_ref, used_ref, *refs, per):
    x_refs = refs[:per]
    wg_ref, wu_ref, rw_ref, o_ref, wgb_ref, wub_ref, x_scr = refs[per:]
    t = pl.program_id(1)

    @pl.when(t < used_ref[0])
    def _():
        @pl.when(first_ref[t] == 1)
        def _():
            wgb_ref[...] = wg_ref[0].astype(BF16)
            wub_ref[...] = wu_ref[0].astype(BF16)

        unit = x_refs[0].shape[0]
        for i in range(per):
            x_scr[i * unit:(i + 1) * unit, :] = x_refs[i][...]
        x = x_scr[...]
        g = jnp.dot(x, wgb_ref[...], preferred_element_type=F32)
        u = jnp.dot(x, wub_ref[...], preferred_element_type=F32)
        o_ref[...] = (_silu(g) * u * rw_ref[...]).astype(o_ref.dtype)

    @pl.when(t >= used_ref[0])
    def _():
        o_ref[...] = jnp.zeros_like(o_ref)


def _moe_expert_out_kernel(exp_ref, first_ref, used_ref, hid_ref, w_ref, o_ref, wb_ref):
    t = pl.program_id(1)

    @pl.when(t < used_ref[0])
    def _():
        @pl.when(first_ref[t] == 1)
        def _():
            wb_ref[...] = w_ref[0].astype(BF16)

        o_ref[...] = jnp.dot(hid_ref[...], wb_ref[...], preferred_element_type=F32).astype(o_ref.dtype)

    @pl.when(pl.program_id(1) >= used_ref[0])
    def _():
        o_ref[...] = jnp.zeros_like(o_ref)


def _moe_scatter_kernel(dst_ref, h_ref, tok_ref, *refs, per):
    y_refs = refs[:per]
    o_ref, y_scr = refs[per:]

    @pl.when(pl.program_id(2) == 0)
    def _():
        o_ref[...] = h_ref[...]

    tok = tok_ref[0]
    row = lax.broadcasted_iota(jnp.int32, (o_ref.shape[0], tok.shape[1]), 0)
    onehot_t = jnp.where(tok == row, 1.0, 0.0).astype(BF16)
    unit = y_refs[0].shape[0]
    for i in range(per):
        y_scr[i * unit:(i + 1) * unit, :] = y_refs[i][...]
    o_ref[...] += jnp.dot(onehot_t, y_scr[...], preferred_element_type=F32)


def moe_ffn_routed(v, h, comb, w_in, w_out, *, tb=MOE_BLOCK, unit=MOE_UNIT, tile=MOE_TILE, bn=512, bo=1024):
    m, d = v.shape
    n_experts, _, two_de = w_in.shape
    de = two_de // 2
    tb = min(tb, m)
    nb = m // tb
    per = tile // unit
    bn = _pick(de, bn)
    bo = _pick(d, bo)
    n_assign = MOE_TOPK * tb
    n_slots = -(-(n_assign // unit + n_experts + 1) // per) * per
    groups = n_slots // per
    n_units = nb * (n_assign // unit + n_experts) + n_experts * (per - 1)
    n_tiles = -(-n_units // per)
    n_units = n_tiles * per

    wts, ids = lax.top_k(comb[:, :n_experts], MOE_TOPK)
    ea = ids.reshape(nb, n_assign)
    wa = wts.reshape(nb, n_assign)
    ta = jnp.broadcast_to(jnp.repeat(jnp.arange(tb, dtype=jnp.int32), MOE_TOPK)[None], (nb, n_assign))
    se, st, sw = lax.sort((ea, ta, wa), dimension=1, num_keys=1, is_stable=True)
    counts = jnp.sum(jax.nn.one_hot(ea, n_experts, dtype=jnp.int32), axis=1)
    units = -(-counts // unit)
    excl = lambda x, axis: jnp.cumsum(x, axis=axis) - x
    slot_start = excl(units, 1)
    row_start = excl(counts, 1)
    is_e = se[..., None] == jnp.arange(n_experts, dtype=jnp.int32)
    lookup = lambda table: jnp.sum(jnp.where(is_e, table[:, None, :], 0), axis=-1)
    pos = lookup(slot_start) * unit + jnp.arange(n_assign, dtype=jnp.int32)[None] - lookup(row_start)
    bidx = jnp.arange(nb, dtype=jnp.int32)[:, None]
    hit = pos[:, None, :] == jnp.arange(n_slots * unit, dtype=jnp.int32)[None, :, None]
    row_token = jnp.sum(jnp.where(hit, st[:, None, :] + 1, 0), axis=-1) - 1
    row_weight = jnp.sum(jnp.where(hit, sw[:, None, :], 0.0), axis=-1)
    per_expert = jnp.sum(units, axis=0)
    per_expert_pad = -(-per_expert // per) * per
    e_off = excl(per_expert_pad, 0)
    before = excl(units, 0)
    slot = jnp.arange(n_slots, dtype=jnp.int32)
    slot_end = jnp.cumsum(units, axis=1)
    e_of_slot = jnp.sum(slot[None, :, None] >= slot_end[:, None, :], axis=-1)
    used_slot = e_of_slot < n_experts
    e_clip = jnp.minimum(e_of_slot, n_experts - 1)
    dst_unit = (e_off[e_clip] + jnp.take_along_axis(before, e_clip, axis=1)
                + slot[None] - jnp.take_along_axis(slot_start, e_clip, axis=1))
    dst_unit = jnp.where(used_slot, dst_unit, 0).astype(jnp.int32)
    flat_slot = (bidx * n_slots + slot[None]).astype(jnp.int32)
    zero_slot = n_slots - 1
    src_unit = jnp.full((n_units,), zero_slot, jnp.int32).at[
        jnp.where(used_slot, dst_unit, n_units).reshape(-1)].set(flat_slot.reshape(-1), mode="drop")
    tile_end = jnp.cumsum(per_expert_pad) // per
    tile_ids = jnp.arange(n_tiles, dtype=jnp.int32)
    tile_expert = jnp.minimum(jnp.sum(tile_ids[:, None] >= tile_end[None, :], axis=-1), n_experts - 1).astype(jnp.int32)
    tiles_used = tile_end[-1:].astype(jnp.int32)
    first = jnp.concatenate([jnp.ones((1,), jnp.int32),
                             (tile_expert[1:] != tile_expert[:-1]).astype(jnp.int32)])
    rw_em = row_weight.reshape(nb * n_slots, unit)[src_unit].reshape(n_units * unit, 1)

    xs = pl.pallas_call(
        _moe_gather_kernel,
        grid=(nb, groups),
        in_specs=[pl.BlockSpec((tb, d), lambda b_, g: (b_, 0)),
                  pl.BlockSpec((1, tile, 1), lambda b_, g: (b_, g, 0))],
        out_specs=pl.BlockSpec((tile, d), lambda b_, g: (b_ * groups + g, 0)),
        out_shape=jax.ShapeDtypeStruct((nb * n_slots * unit, d), BF16),
        compiler_params=_params("parallel", "arbitrary"),
        name="moe_gather",
    )(v, row_token.reshape(nb, n_slots * unit, 1))

    bpe = de // bn
    unit_spec = lambda i: pl.BlockSpec((unit, d), lambda j, t, src, ex, fi, us: (src[per * t + i], 0))
    hid = pl.pallas_call(
        functools.partial(_moe_expert_in_kernel, per=per),
        grid_spec=pltpu.PrefetchScalarGridSpec(
            num_scalar_prefetch=4,
            grid=(bpe, n_tiles),
            in_specs=[unit_spec(i) for i in range(per)] + [
                pl.BlockSpec((1, d, bn), lambda j, t, src, ex, fi, us: (ex[t], 0, j),
                             pipeline_mode=pl.Buffered(1)),
                pl.BlockSpec((1, d, bn), lambda j, t, src, ex, fi, us: (ex[t], 0, j + bpe),
                             pipeline_mode=pl.Buffered(1)),
                pl.BlockSpec((tile, 1), lambda j, t, src, ex, fi, us: (t, 0))],
            out_specs=pl.BlockSpec((tile, bn), lambda j, t, src, ex, fi, us: (t, j)),
            scratch_shapes=[pltpu.VMEM((d, bn), BF16), pltpu.VMEM((d, bn), BF16), pltpu.VMEM((tile, d), BF16)]),
        out_shape=jax.ShapeDtypeStruct((n_tiles * tile, de), BF16),
        compiler_params=_params("arbitrary", "arbitrary"),
        name="moe_expert_in",
    )(src_unit, tile_expert, first, tiles_used, *([xs] * per), w_in, w_in, rw_em)

    bo2 = _pick(d, 2 * bo)
    ys = pl.pallas_call(
        _moe_expert_out_kernel,
        grid_spec=pltpu.PrefetchScalarGridSpec(
            num_scalar_prefetch=3,
            grid=(d // bo2, n_tiles),
            in_specs=[pl.BlockSpec((tile, de), lambda n, t, ex, fi, us: (t, 0)),
                      pl.BlockSpec((1, de, bo2), lambda n, t, ex, fi, us: (ex[t], 0, n))],
            out_specs=pl.BlockSpec((tile, bo2), lambda n, t, ex, fi, us: (t, n)),
            scratch_shapes=[pltpu.VMEM((de, bo2), BF16)]),
        out_shape=jax.ShapeDtypeStruct((n_tiles * tile, d), BF16),
        compiler_params=_params("arbitrary", "arbitrary"),
        name="moe_expert_out",
    )(tile_expert, first, tiles_used, hid, w_out)

    y_spec = lambda i: pl.BlockSpec((unit, bo2), lambda b_, n, g, dst: (dst[(b_ * groups + g) * per + i], n))
    return pl.pallas_call(
        functools.partial(_moe_scatter_kernel, per=per),
        grid_spec=pltpu.PrefetchScalarGridSpec(
            num_scalar_prefetch=1,
            grid=(nb, d // bo2, groups),
            in_specs=[pl.BlockSpec((tb, bo2), lambda b_, n, g, dst: (b_, n)),
                      pl.BlockSpec((1, 1, tile), lambda b_, n, g, dst: (b_ * groups + g, 0, 0))]
                     + [y_spec(i) for i in range(per)],
            out_specs=pl.BlockSpec((tb, bo2), lambda b_, n, g, dst: (b_, n)),
            scratch_shapes=[pltpu.VMEM((tile, bo2), BF16)]),
        out_shape=jax.ShapeDtypeStruct((m, d), F32),
        compiler_params=_params("parallel", "parallel", "arbitrary"),
        name="moe_scatter",
    )(dst_unit.reshape(-1), h, row_token.reshape(nb * groups, 1, tile), *([ys] * per))


def ple_gate(h, p_i, norm_pl, pl_proj, pl_gate, layer):
    d = h.shape[1]
    n = rmsnorm(h, norm_pl, name="rmsnorm_ple")
    return matmul_ws(n, [(pl_gate, (layer, 0))], d, epilogue=_ep_ple_gate,
                     extras=[(h, "mn"), (p_i, "m"), (pl_proj, "kn")], name="ple_gate")


def _rw_mix_kernel(u_ref, mu_ref, *o_refs):
    u = u_ref[0]
    row = lax.broadcasted_iota(jnp.int32, u.shape, 0)
    dx = jnp.where(row >= 1, pltpu.roll(u, 1, 0), 0.0) - u
    for j, o_ref in enumerate(o_refs):
        o_ref[0] = (u + dx * mu_ref[j:j + 1, :]).astype(o_ref.dtype)


def rw_token_mix(u, mu):
    bsz, s_len, d = u.shape
    cb = _pick(d, LANES)
    n_mix = mu.shape[0]
    spec = pl.BlockSpec((1, s_len, cb), lambda b_, j: (b_, 0, j))
    return pl.pallas_call(
        _rw_mix_kernel,
        grid=(bsz, d // cb),
        in_specs=[spec, pl.BlockSpec((n_mix, cb), lambda b_, j: (0, j))],
        out_specs=[spec] * n_mix,
        out_shape=[jax.ShapeDtypeStruct(u.shape, BF16)] * n_mix,
        compiler_params=_params("parallel", "parallel"),
        name="rwkv_token_mix",
    )(u, mu)


def _dot_hi(a, b):
    return jnp.dot(a, b, preferred_element_type=F32, precision=lax.Precision.HIGHEST)


def _rw_scan_tile_kernel(r_ref, k_ref, v_ref, a_ref, lw_ref, g_ref, kk_ref, ka_ref, rk_ref, lnw_ref, lnb_ref,
                         o_ref, state_ref, *, chunk, heads, n):
    @pl.when(pl.program_id(2) == 0)
    def _():
        state_ref[...] = jnp.zeros_like(state_ref)

    per = LANES // n
    tiles = range(heads // per)
    sub = range(per)
    ti = lax.broadcasted_iota(jnp.int32, (chunk, chunk), 0)
    si = lax.broadcasted_iota(jnp.int32, (chunk, chunk), 1)
    strict = ti > si
    incl = ti >= si
    lane_seg = lax.broadcasted_iota(jnp.int32, (1, LANES), 1) // n
    seg_is = [lane_seg == j for j in sub]
    same_head = (lax.broadcasted_iota(jnp.int32, (LANES, LANES), 0) // n
                 == lax.broadcasted_iota(jnp.int32, (LANES, LANES), 1) // n)
    dot = functools.partial(jnp.dot, preferred_element_type=F32)
    tile = lambda x, i: x[:, i * LANES:(i + 1) * LANES]

    def pick(vals):
        out = vals[-1]
        for j in range(per - 2, -1, -1):
            out = jnp.where(seg_is[j], vals[j], out)
        return out

    def seg_sum(x):
        return pick([jnp.sum(jnp.where(seg_is[j], x, 0.0), axis=-1, keepdims=True) for j in sub])

    r, k, v, a, lw = r_ref[0], k_ref[0], v_ref[0], a_ref[0], lw_ref[0]
    kk = k * kk_ref[...]
    kmod = k * (1.0 + (a - 1.0) * ka_ref[...])
    cum = _cumsum_rows(lw, chunk)
    cum_end = cum[chunk - 1:chunk, :]
    mid = cum[chunk // 2 - 1:chunk // 2, :]
    bonus_in = r * kmod * rk_ref[...]
    kk_t, bonus_t = [], []
    for i in tiles:
        kki = tile(kk, i)
        kk_t.append(kki / jnp.maximum(jnp.sqrt(seg_sum(kki * kki)), 1e-12))
        bonus_t.append(seg_sum(tile(bonus_in, i)) * tile(v, i))
    kk = jnp.concatenate(kk_t, axis=-1) if len(kk_t) > 1 else kk_t[0]
    kka = kk * a
    e_neg = jnp.exp(mid - cum)
    to_end = jnp.exp(cum_end - cum)
    am = (kk * jnp.exp(cum - lw - mid)).astype(BF16)
    bm = (kka * e_neg).astype(BF16)
    km = (kmod * e_neg).astype(BF16)
    rm = (r * jnp.exp(cum - mid)).astype(BF16)
    a_abs = (kk * jnp.exp(cum - lw)).astype(BF16)
    r_abs = (r * jnp.exp(cum)).astype(BF16)
    k_end = (kmod * to_end).astype(BF16)
    b_end = (kka * to_end).astype(BF16)
    vb = v.astype(BF16)
    st_decay = jnp.exp(cum_end)
    zero = jnp.zeros((), BF16)

    st = [state_ref[i] for i in tiles]
    stb = [s.astype(BF16) for s in st]
    am_h = [[jnp.where(seg_is[j], tile(am, i), zero) for j in sub] for i in tiles]
    rm_h = [[jnp.where(seg_is[j], tile(rm, i), zero) for j in sub] for i in tiles]
    nb = [[(-jnp.where(strict, _dot_nt(am_h[i][j], tile(bm, i)), 0.0)).astype(BF16) for j in sub] for i in tiles]
    lk = [[jnp.where(strict, _dot_nt(am_h[i][j], tile(km, i)), 0.0).astype(BF16) for j in sub] for i in tiles]
    x = [_dot_nt(tile(a_abs, i), stb[i]) + pick([dot(lk[i][j], tile(vb, i)) for j in sub]) for i in tiles]
    xb = [xi.astype(BF16) for xi in x]
    x = [x[i] + pick([dot(nb[i][j], xb[i]) for j in sub]) for i in tiles]
    p = 2
    while p < chunk:
        nb = [[dot(nb[i][j], nb[i][j]).astype(BF16) for j in sub] for i in tiles]
        xb = [xi.astype(BF16) for xi in x]
        x = [x[i] + pick([dot(nb[i][j], xb[i]) for j in sub]) for i in tiles]
        p *= 2
    pb = [xi.astype(BF16) for xi in x]
    mk = [[jnp.where(incl, _dot_nt(rm_h[i][j], tile(km, i)), 0.0).astype(BF16) for j in sub] for i in tiles]
    mb = [[jnp.where(incl, _dot_nt(rm_h[i][j], tile(bm, i)), 0.0).astype(BF16) for j in sub] for i in tiles]
    y = [_dot_nt(tile(r_abs, i), stb[i])
         + pick([dot(mk[i][j], tile(vb, i)) - dot(mb[i][j], pb[i]) for j in sub]) for i in tiles]
    for i in tiles:
        upd = _dot_tn(tile(vb, i), tile(k_end, i)) - _dot_tn(pb[i], tile(b_end, i))
        state_ref[i] = st[i] * tile(st_decay, i) + jnp.where(same_head, upd, 0.0)
    inv_n = 1.0 / n
    for i in tiles:
        cols = slice(i * LANES, (i + 1) * LANES)
        mean = seg_sum(y[i]) * inv_n
        yc = y[i] - mean
        var = seg_sum(yc * yc) * inv_n
        yn = yc * lax.rsqrt(var + RW_LN_EPS) * lnw_ref[:, cols] + lnb_ref[:, cols]
        o_ref[0, :, cols] = ((yn + bonus_t[i]) * g_ref[0, :, cols]).astype(o_ref.dtype)


def rw_scan(r, k, v, a, lw, g, k_k, k_a, r_k, ln_w, ln_b, *, n=RW_HEAD_DIM, chunk=RW_CHUNK, heads=8):
    bsz, s_len, d = r.shape
    chunk = min(chunk, s_len)
    heads = min(heads, d // n)
    hw = heads * n
    seq = pl.BlockSpec((1, chunk, hw), lambda b_, h_, c: (b_, c, h_))
    par = pl.BlockSpec((1, hw), lambda b_, h_, c: (0, h_))
    row = lambda t: t.reshape(1, d)
    assert hw % LANES == 0 and LANES % n == 0
    kern = functools.partial(_rw_scan_tile_kernel, chunk=chunk, heads=heads, n=n)
    return pl.pallas_call(
        kern,
        grid=(bsz, d // hw, s_len // chunk),
        in_specs=[seq] * 6 + [par] * 5,
        out_specs=seq,
        out_shape=jax.ShapeDtypeStruct(r.shape, BF16),
        scratch_shapes=[pltpu.VMEM((hw // LANES, LANES, LANES), F32)],
        compiler_params=_params("parallel", "parallel", "arbitrary"),
        name="rwkv7_scan",
    )(r, k, v, a, lw, g, row(k_k), row(k_a), row(r_k), row(ln_w), row(ln_b))


def rwkv7_mixer(u, h, w, bsz, s_len):
    t, d = u.shape
    xr, xw, xk, xv, xa, xg = [x.reshape(t, d) for x in rw_token_mix(u.reshape(bsz, s_len, d), w["rw_mu"])]
    r = matmul_ws(xr, [(w["rw_w_rkv"], (0, 0))], d, name="rw_r")
    k = matmul_ws(xk, [(w["rw_w_rkv"], (1, 0))], d, name="rw_k")
    v = matmul_ws(xv, [(w["rw_w_rkv"], (2, 0))], d, name="rw_v")
    row = lambda x: x.reshape(1, d)
    w_lo = matmul(xw, [(w["rw_w1"], 0)], w["rw_w1"].shape[1], epilogue=_ep_tanh, out_dtype=BF16, name="rw_w1")
    wide = 2048
    lw = matmul(w_lo, [(w["rw_w2"], 0)], d, epilogue=_ep_rw_logdecay, extras=[(row(w["rw_w0"]), "n")], bn=wide,
                name="rw_w2")
    a_lo = matmul(xa, [(w["rw_a1"], 0)], w["rw_a1"].shape[1], out_dtype=BF16, name="rw_a1")
    a = matmul(a_lo, [(w["rw_a2"], 0)], d, epilogue=_ep_bias_sigmoid, extras=[(row(w["rw_a0"]), "n")], bn=wide,
               name="rw_a2")
    g_lo = matmul(xg, [(w["rw_g1"], 0)], w["rw_g1"].shape[1], epilogue=_ep_sigmoid, out_dtype=BF16, name="rw_g1")
    g = matmul(g_lo, [(w["rw_g2"], 0)], d, bn=wide, name="rw_g2")
    shp = (bsz, s_len, d)
    y = rw_scan(r.reshape(shp), k.reshape(shp), v.reshape(shp), a.reshape(shp), lw.reshape(shp), g.reshape(shp),
                w["rw_k_k"], w["rw_k_a"], w["rw_r_k"], w["rw_ln_w"], w["rw_ln_b"])
    return matmul_ws(y.reshape(t, d), [(w["rw_w_out"], 0)], d, epilogue=_ep_residual, extras=[(h, "mn")],
                     name="rw_out")


NEG_BIG = -1e30


def _rope_kernel(x_ref, cc_ref, ss_ref, o_ref, *, n_q_slots, scale):
    x = x_ref[0]
    out = x * cc_ref[...] + pltpu.roll(x, x.shape[-1] // 2, 1) * ss_ref[...]
    out = out * jnp.where(pl.program_id(2) < n_q_slots, scale, 1.0)
    o_ref[0] = out.astype(o_ref.dtype)


def _rope_tables(pos, dim):
    inv = ROPE_THETA ** (-(jnp.arange(0, dim, 2, dtype=F32) / dim))
    ang = pos.astype(F32)[:, None] * inv[None, :]
    cos, sin = jnp.cos(ang), jnp.sin(ang)
    return jnp.concatenate([cos, cos], axis=-1), jnp.concatenate([-sin, sin], axis=-1)


def nsa_rope(proj, n_q_slots, k_slots, dh, scale, tb=512):
    bsz, s_len, _ = proj.shape
    tb = min(tb, s_len)
    cc, ss = _rope_tables(jnp.arange(s_len), dh)
    n_out = n_q_slots + len(k_slots)

    def in_slot(j):
        slot = j
        for idx, ks in enumerate(k_slots):
            slot = jnp.where(j == n_q_slots + idx, ks, slot)
        return slot

    return pl.pallas_call(
        functools.partial(_rope_kernel, n_q_slots=n_q_slots, scale=scale),
        grid=(bsz, s_len // tb, n_out),
        in_specs=[pl.BlockSpec((1, tb, dh), lambda b_, t, j: (b_, t, in_slot(j))),
                  pl.BlockSpec((tb, dh), lambda b_, t, j: (t, 0)),
                  pl.BlockSpec((tb, dh), lambda b_, t, j: (t, 0))],
        out_specs=pl.BlockSpec((1, tb, dh), lambda b_, t, j: (b_, t, j)),
        out_shape=jax.ShapeDtypeStruct((bsz, s_len, n_out * dh), BF16),
        compiler_params=_params("parallel", "parallel", "arbitrary"),
        name="nsa_rope",
    )(proj, cc, ss)


def _cmp_finish_kernel(z_ref, bias_ref, w2_ref, cc_ref, ss_ref, o_ref, *, hidden, rope):
    z = z_ref[0]
    nc = z.shape[0]
    nxt = pltpu.roll(z[:, hidden:], nc - 1, 0)
    hid = _silu(z[:, :hidden] + nxt + bias_ref[...])
    out = jnp.dot(hid.astype(BF16), w2_ref[...], preferred_element_type=F32)
    if rope:
        out = out * cc_ref[...] + pltpu.roll(out, out.shape[-1] // 2, 1) * ss_ref[...]
    o_ref[0] = out.astype(o_ref.dtype)


def nsa_compress(x, pos_emb, w1, w2, bsz, s_len, groups, dh, rope, transpose_out=False):
    stride, blk = NSA_CMP_STRIDE, NSA_CMP_BLOCK
    nc = s_len // stride
    hidden = w1.shape[-1]
    half = stride * dh
    x16 = jnp.transpose(x.reshape(bsz, nc, stride, groups, dh), (0, 3, 1, 2, 4)).reshape(bsz * groups * nc, half)
    w1f = w1.reshape(blk * dh, hidden)
    wcat = jnp.concatenate([w1f[:half], w1f[half:]], axis=1).astype(BF16)
    z = matmul(x16.astype(BF16), [(wcat, 0)], 2 * hidden, name="nsa_cmp_w1")
    bias = matmul(pos_emb.reshape(1, blk * dh).astype(BF16), [(w1f.astype(BF16), 0)], hidden, name="nsa_cmp_pos")
    cc, ss = _rope_tables(jnp.arange(nc) * stride + blk - 1, dh)
    if transpose_out:
        assert not rope
        return pl.pallas_call(
            functools.partial(_cmp_finish_t_kernel, hidden=hidden),
            grid=(bsz * groups,),
            in_specs=[pl.BlockSpec((1, nc, 2 * hidden), lambda i: (i, 0, 0)),
                      pl.BlockSpec((1, hidden), lambda i: (0, 0)),
                      pl.BlockSpec((dh, hidden), lambda i: (0, 0))],
            out_specs=pl.BlockSpec((1, dh, nc), lambda i: (i, 0, 0)),
            out_shape=jax.ShapeDtypeStruct((bsz * groups, dh, nc), BF16),
            compiler_params=_params("parallel"),
            name="nsa_cmp_finish_t",
        )(z.reshape(bsz * groups, nc, 2 * hidden), bias, w2.T.astype(BF16))
    return pl.pallas_call(
        functools.partial(_cmp_finish_kernel, hidden=hidden, rope=rope),
        grid=(bsz * groups,),
        in_specs=[pl.BlockSpec((1, nc, 2 * hidden), lambda i: (i, 0, 0)),
                  pl.BlockSpec((1, hidden), lambda i: (0, 0)),
                  pl.BlockSpec((hidden, dh), lambda i: (0, 0)),
                  pl.BlockSpec((nc, dh), lambda i: (0, 0)),
                  pl.BlockSpec((nc, dh), lambda i: (0, 0))],
        out_specs=pl.BlockSpec((1, nc, dh), lambda i: (i, 0, 0)),
        out_shape=jax.ShapeDtypeStruct((bsz * groups, nc, dh), BF16),
        compiler_params=_params("parallel"),
        name="nsa_cmp_finish",
    )(z.reshape(bsz * groups, nc, 2 * hidden), bias, w2.astype(BF16), cc, ss)


def _rope_t_kernel(x_ref, cc_ref, ss_ref, o_ref, *, n_rope, scale, group, dh):
    first_slot = pl.program_id(2) * group
    for i in range(group):
        x = x_ref[0, :, i * dh:(i + 1) * dh]
        roped = (x * cc_ref[...] + pltpu.roll(x, dh // 2, 1) * ss_ref[...]) * scale
        out = jnp.where(first_slot + i < n_rope, roped, x)
        o_ref[0, i * dh:(i + 1) * dh, :] = out.T.astype(o_ref.dtype)


def nsa_rope_t(proj, slots, n_rope, dh, scale, tb=512, group=4):
    bsz, s_len, _ = proj.shape
    tb = min(tb, s_len)
    cc, ss = _rope_tables(jnp.arange(s_len), dh)
    assert len(slots) % group == 0
    firsts = slots[::group]
    assert all(f % group == 0 and slots[i * group:(i + 1) * group] == list(range(f, f + group))
               for i, f in enumerate(firsts))
    table = jnp.asarray([f // group for f in firsts], jnp.int32)
    grid_spec = pltpu.PrefetchScalarGridSpec(
        num_scalar_prefetch=1,
        grid=(bsz, s_len // tb, len(firsts)),
        in_specs=[pl.BlockSpec((1, tb, group * dh), lambda b_, t, j, tab: (b_, t, tab[j])),
                  pl.BlockSpec((tb, dh), lambda b_, t, j, tab: (t, 0)),
                  pl.BlockSpec((tb, dh), lambda b_, t, j, tab: (t, 0))],
        out_specs=pl.BlockSpec((1, group * dh, tb), lambda b_, t, j, tab: (b_, j, t)),
    )
    kern = lambda tab, x_ref, cc_ref, ss_ref, o_ref: _rope_t_kernel(x_ref, cc_ref, ss_ref, o_ref, n_rope=n_rope,
                                                                   scale=scale, group=group, dh=dh)
    return pl.pallas_call(
        kern,
        grid_spec=grid_spec,
        out_shape=jax.ShapeDtypeStruct((bsz, len(slots) * dh, s_len), BF16),
        compiler_params=_params("parallel", "parallel", "arbitrary"),
        name="nsa_rope_t",
    )(table, proj, cc, ss)


def _cmp_finish_t_kernel(z_ref, bias_ref, w2_ref, o_ref, *, hidden):
    z = z_ref[0]
    nc = z.shape[0]
    nxt = pltpu.roll(z[:, hidden:], nc - 1, 0)
    hid = _silu(z[:, :hidden] + nxt + bias_ref[...])
    o_ref[0] = _dot_nt(w2_ref[...], hid.astype(BF16)).astype(o_ref.dtype)


def _nsa_cmp_select_t_kernel(q_ref, kc_ref, vc_ref, ov_ref, oc_ref, sel_ref, *, tq, rep, dh, topn):
    qi = pl.program_id(2)
    kc = kc_ref[0]
    vct = vc_ref[0]
    nc = kc.shape[0]
    n_sel = sel_ref.shape[2]
    t = qi * tq + lax.broadcasted_iota(jnp.int32, (nc, tq), 1)
    cmp_end = lax.broadcasted_iota(jnp.int32, (nc, tq), 0) * NSA_CMP_STRIDE + (NSA_CMP_BLOCK - 1)
    visible = cmp_end <= t
    s = [jnp.where(visible, jnp.dot(kc, q_ref[0, r * dh:(r + 1) * dh, :], preferred_element_type=F32), NEG_BIG)
         for r in range(rep)]
    e = [jnp.where(visible, jnp.exp2(x - jnp.max(x, axis=0, keepdims=True)), 0.0) for x in s]
    den = [jnp.sum(x, axis=0, keepdims=True) for x in e]
    p = [e[r] / jnp.where(den[r] > 0, den[r], 1.0) for r in range(rep)]
    for r in range(rep):
        oc_ref[0, r * dh:(r + 1) * dh, :] = jnp.dot(vct, p[r].astype(BF16), preferred_element_type=F32)
    psum = p[0]
    for r in range(1, rep):
        psum = psum + p[r]
    imp = _dot_hi(ov_ref[...], psum)
    blk = lax.broadcasted_iota(jnp.int32, (n_sel, tq), 0)
    cur = (qi * tq + lax.broadcasted_iota(jnp.int32, (n_sel, tq), 1)) // NSA_SEL_BLOCK
    forced = (blk == 0) | (blk == cur) | (blk == cur - 1)
    imp = jnp.where(forced, NSA_FORCED_SCORE, imp)
    imp = jnp.where(blk > cur, -jnp.inf, imp)
    sel = jnp.zeros((n_sel, tq), F32)
    for _ in range(topn):
        m = jnp.max(imp, axis=0, keepdims=True)
        first = jnp.min(jnp.where(imp == m, blk, n_sel), axis=0, keepdims=True)
        hit = blk == first
        sel = jnp.where(hit, 1.0, sel)
        imp = jnp.where(hit, -jnp.inf, imp)
    sel_ref[0, 0] = sel


def _flash_t_init(m_ref, l_ref, acc_ref):
    m_ref[...] = jnp.full_like(m_ref, NEG_BIG)
    l_ref[...] = jnp.zeros_like(l_ref)
    acc_ref[...] = jnp.zeros_like(acc_ref)


def _flash_t_step(q_ref, k, vt, mask, m_ref, l_ref, acc_ref, rep, dh):
    hs = range(rep)
    s = [jnp.where(mask, jnp.dot(k, q_ref[0, r * dh:(r + 1) * dh, :], preferred_element_type=F32), NEG_BIG)
         for r in hs]
    m_old = [m_ref[r] for r in hs]
    m_new = [jnp.maximum(m_old[r], jnp.max(s[r], axis=0, keepdims=True)) for r in hs]
    p = [jnp.exp2(s[r] - m_new[r]).astype(BF16) for r in hs]
    alpha = [jnp.exp2(m_old[r] - m_new[r]) for r in hs]
    pv = [jnp.dot(vt, p[r], preferred_element_type=F32) for r in hs]
    ones = jnp.ones((8, k.shape[0]), BF16)
    psum = [jnp.dot(ones, p[r], preferred_element_type=F32)[0:1] for r in hs]
    for r in hs:
        m_ref[r] = m_new[r]
        l_ref[r] = alpha[r] * l_ref[r] + psum[r]
        acc_ref[r] = acc_ref[r] * alpha[r] + pv[r]


def _nsa_select_t_kernel(qi_ref, kj_ref, q_ref, k_ref, vt_ref, sel_ref, o_ref, m_ref, l_ref, acc_ref,
                         *, tq, kb, rep, dh):
    pair = pl.program_id(2)
    qi = qi_ref[pair]
    kj = kj_ref[pair]

    @pl.when(kj == 0)
    def _():
        _flash_t_init(m_ref, l_ref, acc_ref)

    kpos = kj * kb + lax.broadcasted_iota(jnp.int32, (kb, tq), 0)
    t = qi * tq + lax.broadcasted_iota(jnp.int32, (kb, tq), 1)
    per = kb // NSA_SEL_BLOCK
    chosen = jnp.zeros((kb, tq), F32)
    for i in range(per):
        row = sel_ref[0, 0, pl.ds(kj * per + i, 1), :]
        chosen = jnp.where((kpos - kj * kb) // NSA_SEL_BLOCK == i, row, chosen)
    mask = (chosen > 0) & (kpos <= t)
    _flash_t_step(q_ref, k_ref[0], vt_ref[0], mask, m_ref, l_ref, acc_ref, rep, dh)

    @pl.when(kj * kb + kb > qi * tq + tq - 1)
    def _():
        for r in range(rep):
            l = l_ref[r]
            o_ref[0, r * dh:(r + 1) * dh, :] = acc_ref[r] / jnp.where(l > 0, l, 1.0)


def _nsa_window_t_kernel(q_ref, k_ref, vt_ref, oc_ref, os_ref, g_ref, o_ref, m_ref, l_ref, acc_ref,
                         *, tq, kb, rep, dh, window, n_steps):
    qi = pl.program_id(2)
    w = pl.program_id(3)
    kblk = qi * (tq // kb) - (n_steps - tq // kb) + w

    @pl.when(w == 0)
    def _():
        _flash_t_init(m_ref, l_ref, acc_ref)

    @pl.when(kblk >= 0)
    def _():
        kpos = kblk * kb + lax.broadcasted_iota(jnp.int32, (kb, tq), 0)
        t = qi * tq + lax.broadcasted_iota(jnp.int32, (kb, tq), 1)
        mask = (kpos <= t) & (kpos > t - window)
        _flash_t_step(q_ref, k_ref[0], vt_ref[0], mask, m_ref, l_ref, acc_ref, rep, dh)

    @pl.when(w == n_steps - 1)
    def _():
        gates = g_ref[0, 0]
        for r in range(rep):
            rows = slice(r * dh, (r + 1) * dh)
            l = l_ref[r]
            o_w = acc_ref[r] / jnp.where(l > 0, l, 1.0)
            o = (gates[3 * r:3 * r + 1, :] * oc_ref[0, rows, :] + gates[3 * r + 1:3 * r + 2, :] * os_ref[0, rows, :]
                 + gates[3 * r + 2:3 * r + 3, :] * o_w)
            o_ref[0, :, rows] = o.T.astype(o_ref.dtype)


def nsa_mixer_t(u, h, w, bsz, s_len):
    t, d = u.shape
    dh, groups = NSA_HEAD_DIM, NSA_N_KV
    n_heads = d // dh
    rep = n_heads // groups
    kvw = groups * dh
    qw = n_heads * dh
    main_w = qw + 6 * kvw
    scale = dh ** -0.5
    tq = kb = min(256, s_len)
    nq = s_len // tq
    n_sel = s_len // NSA_SEL_BLOCK
    topn = min(NSA_TOPK, n_sel)
    w_in = w["nsa_w_in"]
    proj = matmul_ws(u, [(w_in, 0)], main_w, name="nsa_in").reshape(bsz, s_len, main_w)
    gates = matmul(u, [(w_in[:, main_w:].astype(BF16), 0)], w_in.shape[1] - main_w, epilogue=_ep_sigmoid,
                   name="nsa_gates")
    gates = jnp.transpose(gates.reshape(bsz, s_len, groups, rep * 3), (0, 2, 3, 1))
    slot = lambda j: (qw + j * kvw) // dh
    qvt = nsa_rope_t(proj, list(range(n_heads)) + [slot(3) + g for g in range(groups)]
                     + [slot(5) + g for g in range(groups)], n_heads, dh, scale * math.log2(math.e), group=groups)
    k_rot = nsa_rope(proj, 0, [slot(2) + g for g in range(groups)] + [slot(4) + g for g in range(groups)], dh, 1.0)
    kc = nsa_compress(proj[..., qw:qw + kvw], w["nsa_cmp_pos_k"], w["nsa_cmp_k_w1"], w["nsa_cmp_k_w2"],
                      bsz, s_len, groups, dh, True)
    vct = nsa_compress(proj[..., qw + kvw:qw + 2 * kvw], w["nsa_cmp_pos_v"], w["nsa_cmp_v_w1"], w["nsa_cmp_v_w2"],
                       bsz, s_len, groups, dh, False, transpose_out=True)
    nc = kc.shape[1]
    cs = jnp.arange(nc)[None, :] * NSA_CMP_STRIDE
    ss = jnp.arange(n_sel)[:, None] * NSA_SEL_BLOCK
    overlap_t = jnp.clip(jnp.minimum(cs + NSA_CMP_BLOCK, ss + NSA_SEL_BLOCK) - jnp.maximum(cs, ss), 0, None)
    overlap_t = overlap_t.astype(F32) / NSA_CMP_BLOCK

    qt_spec3 = pl.BlockSpec((1, rep * dh, tq), lambda b_, g, i: (b_, g, i))
    o_c, sel = pl.pallas_call(
        functools.partial(_nsa_cmp_select_t_kernel, tq=tq, rep=rep, dh=dh, topn=topn),
        grid=(bsz, groups, nq),
        in_specs=[qt_spec3,
                  pl.BlockSpec((1, nc, dh), lambda b_, g, i: (b_ * groups + g, 0, 0)),
                  pl.BlockSpec((1, dh, nc), lambda b_, g, i: (b_ * groups + g, 0, 0)),
                  pl.BlockSpec((n_sel, nc), lambda b_, g, i: (0, 0))],
        out_specs=[qt_spec3, pl.BlockSpec((1, 1, n_sel, tq), lambda b_, g, i: (b_, g, 0, i))],
        out_shape=[jax.ShapeDtypeStruct((bsz, qw, s_len), F32),
                   jax.ShapeDtypeStruct((bsz, groups, n_sel, s_len), F32)],
        compiler_params=_params("parallel", "parallel", "parallel"),
        name="nsa_cmp_select",
    )(qvt, kc, vct, overlap_t)

    flash_scratch = lambda n: [pltpu.VMEM((rep, 1, n), F32), pltpu.VMEM((rep, 1, n), F32),
                               pltpu.VMEM((rep, dh, n), F32)]
    tqs = tq
    pairs = [(i, j) for i in range(s_len // tqs) for j in range((i * tqs + tqs - 1) // kb + 1)]
    qi_of = jnp.asarray([pr[0] for pr in pairs], jnp.int32)
    kj_of = jnp.asarray([pr[1] for pr in pairs], jnp.int32)
    o_s = pl.pallas_call(
        functools.partial(_nsa_select_t_kernel, tq=tqs, kb=kb, rep=rep, dh=dh),
        grid_spec=pltpu.PrefetchScalarGridSpec(
            num_scalar_prefetch=2,
            grid=(bsz, groups, len(pairs)),
            in_specs=[pl.BlockSpec((1, rep * dh, tqs), lambda b_, g, pr, qi, kj: (b_, g, qi[pr])),
                      pl.BlockSpec((1, kb, dh), lambda b_, g, pr, qi, kj: (b_, kj[pr], g)),
                      pl.BlockSpec((1, dh, kb), lambda b_, g, pr, qi, kj: (b_, n_heads + g, kj[pr])),
                      pl.BlockSpec((1, 1, n_sel, tqs), lambda b_, g, pr, qi, kj: (b_, g, 0, qi[pr]))],
            out_specs=pl.BlockSpec((1, rep * dh, tqs), lambda b_, g, pr, qi, kj: (b_, g, qi[pr])),
            scratch_shapes=flash_scratch(tqs)),
        out_shape=jax.ShapeDtypeStruct((bsz, qw, s_len), F32),
        compiler_params=_params("parallel", "parallel", "arbitrary"),
        name="nsa_select_attn",
    )(qi_of, kj_of, qvt, k_rot, qvt, sel)

    n_steps = -(-NSA_WINDOW // kb) + tqs // kb
    win_blk = lambda i, j: jnp.maximum(i * (tqs // kb) - (n_steps - tqs // kb) + j, 0)
    qt_spec = pl.BlockSpec((1, rep * dh, tqs), lambda b_, g, i, j: (b_, g, i))
    o = pl.pallas_call(
        functools.partial(_nsa_window_t_kernel, tq=tqs, kb=kb, rep=rep, dh=dh, window=NSA_WINDOW, n_steps=n_steps),
        grid=(bsz, groups, s_len // tqs, n_steps),
        in_specs=[qt_spec,
                  pl.BlockSpec((1, kb, dh), lambda b_, g, i, j: (b_, win_blk(i, j), groups + g)),
                  pl.BlockSpec((1, dh, kb), lambda b_, g, i, j: (b_, n_heads + groups + g, win_blk(i, j))),
                  qt_spec, qt_spec,
                  pl.BlockSpec((1, 1, rep * 3, tqs), lambda b_, g, i, j: (b_, g, 0, i))],
        out_specs=pl.BlockSpec((1, tqs, rep * dh), lambda b_, g, i, j: (b_, i, g)),
        out_shape=jax.ShapeDtypeStruct((bsz, s_len, qw), BF16),
        scratch_shapes=flash_scratch(tqs),
        compiler_params=_params("parallel", "parallel", "parallel", "arbitrary"),
        name="nsa_window_attn",
    )(qvt, k_rot, qvt, o_c, o_s, gates)
    return matmul_ws(o.reshape(t, qw), [(w["nsa_w_out"], 0)], d, epilogue=_ep_residual, extras=[(h, "mn")],
                     name="nsa_out")


_MATMUL_WEIGHTS = ("pl_proj", "rw_w1", "rw_w2", "rw_a1", "rw_a2", "rw_g1", "rw_g2")


def kernel(x, p, norm_mix, norm_ffn, norm_pl, pl_proj, pl_gate, norm_final, mb_w_in, mb_conv_w, mb_conv_b, mb_dt_bias, mb_a_log, mb_d_skip, mb_norm_w, mb_w_out, nsa_w_in, nsa_cmp_pos_k, nsa_cmp_pos_v, nsa_cmp_k_w1, nsa_cmp_k_w2, nsa_cmp_v_w1, nsa_cmp_v_w2, nsa_w_out, hg_w_in, hg_lb_logits, hg_norm_w, hg_w_out, rw_mu, rw_w_rkv, rw_w0, rw_w1, rw_w2, rw_a0, rw_a1, rw_a2, rw_g1, rw_g2, rw_k_k, rw_k_a, rw_r_k, rw_ln_w, rw_ln_b, rw_w_out, ffn0_w_in, ffn0_w_out, moe1_router, moe1_w_in, moe1_w_out, ffn2_w_in, ffn2_w_out, moe3_router, moe3_w_in, moe3_w_out):
    w = dict(locals())
    for name in _MATMUL_WEIGHTS:
        w[name] = w[name].astype(BF16)
    bsz, s_len, d = x.shape
    depth = p.shape[0]
    t = bsz * s_len
    lb_all = jax.nn.softmax(hg_lb_logits.astype(F32), axis=0)
    lb_all = jnp.cumsum(lb_all, axis=0) - lb_all[0]
    dense = [(w["ffn0_w_in"], w["ffn0_w_out"]), (w["ffn2_w_in"], w["ffn2_w_out"])]
    moe = [(moe1_router, w["moe1_w_in"], w["moe1_w_out"]), (moe3_router, w["moe3_w_in"], w["moe3_w_out"])]
    p_bf = p.reshape(depth, t, p.shape[-1])
    h = x.reshape(t, d)
    for i in range(depth):
        kind = i % 4
        if kind == 0:
            h = mamba2_mixer(rmsnorm(h, norm_mix[i]), h, w, bsz, s_len)
        elif kind == 1:
            h = nsa_mixer_t(rmsnorm(h, norm_mix[i]), h, w, bsz, s_len)
        elif kind == 2:
            h = hgrn2_mixer(rmsnorm(h, norm_mix[i]), h, w, lb_all[i], bsz, s_len)
        else:
            h = rwkv7_mixer(rmsnorm(h, norm_mix[i], out_dtype=F32), h, w, bsz, s_len)
        if i % 2 == 0:
            h = dense_ffn(rmsnorm(h, norm_ffn[i]), h, *dense[i // 2])
        else:
            router, w_in, w_out = moe[i // 2]
            v, comb = rmsnorm_router(h, norm_ffn[i], router)
            h = moe_ffn_routed(v, h, comb, w_in, w_out)
        h = ple_gate(h, p_bf[i], norm_pl[i], w["pl_proj"][i], pl_gate, i)
    return rmsnorm(h, norm_final, out_dtype=F32).reshape(bsz, s_len, d)
```

```python
import functools
import math

import jax
import jax.numpy as jnp
from jax import lax
from jax.experimental import pallas as pl
from jax.experimental.pallas import tpu as pltpu

F32 = jnp.float32
BF16 = jnp.bfloat16

NORM_EPS = 1e-6
ROPE_THETA = 10000.0

V7X_VMEM_BYTES = 64 * 1024 * 1024
VMEM_LIMIT_BYTES = V7X_VMEM_BYTES - 8 * 1024 * 1024
LANES = 128

MB_D_STATE = 128
MB_CHUNK = 128

NSA_HEAD_DIM = 128
NSA_N_KV = 4
NSA_CMP_BLOCK = 32
NSA_CMP_STRIDE = 16
NSA_SEL_BLOCK = 64
NSA_TOPK = 16
NSA_WINDOW = 512
NSA_FORCED_SCORE = 1e9

HG_HEAD_DIM = 128
HG_CHUNK = 32

RW_HEAD_DIM = 64
RW_LN_EPS = 64e-5
RW_CHUNK = 128

MOE_TOPK = 2


def _params(*semantics):
    return pltpu.CompilerParams(dimension_semantics=semantics, vmem_limit_bytes=VMEM_LIMIT_BYTES)


def _pick(n, target):
    if n <= target:
        return n
    for c in range(target, 0, -1):
        if n % c == 0:
            return c
    return n


def _silu(x):
    return x * jax.nn.sigmoid(x)


def _rmsnorm_kernel(x_ref, g_ref, o_ref):
    x = x_ref[...]
    ms = jnp.mean(x * x, axis=-1, keepdims=True)
    o_ref[...] = (x * lax.rsqrt(ms + NORM_EPS) * g_ref[...]).astype(o_ref.dtype)


def rmsnorm(x, gain, out_dtype=BF16, name="rmsnorm"):
    m, d = x.shape
    bm = _pick(m, 256)
    return pl.pallas_call(
        _rmsnorm_kernel,
        grid=(m // bm,),
        in_specs=[pl.BlockSpec((bm, d), lambda i: (i, 0)), pl.BlockSpec((1, d), lambda i: (0, 0))],
        out_specs=pl.BlockSpec((bm, d), lambda i: (i, 0)),
        out_shape=jax.ShapeDtypeStruct((m, d), out_dtype),
        compiler_params=_params("parallel"),
        name=name,
    )(x, gain.reshape(1, d).astype(F32))


def _mm_kernel(*refs, n_w, n_extra, nk, epilogue):
    x_ref = refs[0]
    w_refs = refs[1:1 + n_w]
    e_refs = refs[1 + n_w:1 + n_w + n_extra]
    o_ref = refs[1 + n_w + n_extra]
    acc_refs = refs[2 + n_w + n_extra:]
    x = x_ref[...]
    if nk == 1:
        accs = [jnp.dot(x, w[...], preferred_element_type=F32) for w in w_refs]
        o_ref[...] = epilogue(accs, [e[...] for e in e_refs]).astype(o_ref.dtype)
        return
    k = pl.program_id(2)

    @pl.when(k == 0)
    def _():
        for a in acc_refs:
            a[...] = jnp.zeros_like(a)

    for a, w in zip(acc_refs, w_refs):
        a[...] += jnp.dot(x, w[...], preferred_element_type=F32)

    @pl.when(k == nk - 1)
    def _():
        o_ref[...] = epilogue([a[...] for a in acc_refs], [e[...] for e in e_refs]).astype(o_ref.dtype)


def _first(accs, extras):
    return accs[0]


def matmul(x, ws, n_out, *, epilogue=_first, extras=(), out_dtype=F32, bm=1024, bn=512, bk=None, name="matmul"):
    m, kdim = x.shape
    bm = _pick(m, bm)
    bn = _pick(n_out, bn)
    if bk is None:
        bk = kdim if kdim <= 4096 else _pick(kdim, 4096)
    nk = kdim // bk
    assert kdim % bk == 0 and m % bm == 0 and n_out % bn == 0
    in_specs = [pl.BlockSpec((bm, bk), lambda i, j, k: (i, k))]
    args = [x]
    for w, off in ws:
        assert off % bn == 0 and w.shape[0] == kdim
        in_specs.append(pl.BlockSpec((bk, bn), functools.partial(lambda i, j, k, o: (k, j + o), o=off // bn)))
        args.append(w)
    for arr, kind in extras:
        if kind == "mn":
            in_specs.append(pl.BlockSpec((bm, bn), lambda i, j, k: (i, j)))
        elif kind == "m":
            in_specs.append(pl.BlockSpec((bm, arr.shape[1]), lambda i, j, k: (i, 0)))
        elif kind == "kn":
            in_specs.append(pl.BlockSpec((arr.shape[0], bn), lambda i, j, k: (0, j)))
        else:
            in_specs.append(pl.BlockSpec((1, bn), lambda i, j, k: (0, j)))
        args.append(arr)
    scratch = [pltpu.VMEM((bm, bn), F32) for _ in ws] if nk > 1 else []
    kern = functools.partial(_mm_kernel, n_w=len(ws), n_extra=len(extras), nk=nk, epilogue=epilogue)
    return pl.pallas_call(
        kern,
        grid=(m // bm, n_out // bn, nk),
        in_specs=in_specs,
        out_specs=pl.BlockSpec((bm, bn), lambda i, j, k: (i, j)),
        out_shape=jax.ShapeDtypeStruct((m, n_out), out_dtype),
        scratch_shapes=scratch,
        compiler_params=_params("parallel", "parallel", "arbitrary"),
        name=name,
    )(*args)


WS_CAST_CHUNK = 512


def _mm_ws_kernel(*refs, n_w, n_extra, epilogue):
    x_ref = refs[0]
    w_refs = refs[1:1 + n_w]
    e_refs = refs[1 + n_w:1 + n_w + n_extra]
    o_ref = refs[1 + n_w + n_extra]
    wb_refs = refs[2 + n_w + n_extra:]

    kdim = x_ref.shape[1]
    ck = _pick(kdim, WS_CAST_CHUNK)

    @pl.when(pl.program_id(1) == 0)
    def _():
        accs = [None] * n_w
        for c in range(kdim // ck):
            rows = slice(c * ck, (c + 1) * ck)
            xc = x_ref[:, rows]
            for n, (w, wb) in enumerate(zip(w_refs, wb_refs)):
                wc = (w[0, rows, :] if len(w.shape) == 3 else w[rows, :]).astype(BF16)
                wb[rows, :] = wc
                part = jnp.dot(xc, wc, preferred_element_type=F32)
                accs[n] = part if accs[n] is None else accs[n] + part
        o_ref[...] = epilogue(accs, [e[...] for e in e_refs]).astype(o_ref.dtype)

    @pl.when(pl.program_id(1) != 0)
    def _():
        x = x_ref[...]
        accs = [jnp.dot(x, wb[...], preferred_element_type=F32) for wb in wb_refs]
        o_ref[...] = epilogue(accs, [e[...] for e in e_refs]).astype(o_ref.dtype)


def matmul_ws(x, ws, n_out, *, epilogue=_first, extras=(), out_dtype=F32, bm=1024, bn=512, w_buffers=2,
              name="matmul_ws"):
    m, kdim = x.shape
    bm = _pick(m, bm)
    bn = _pick(n_out, bn)
    assert m % bm == 0 and n_out % bn == 0
    mode = {} if w_buffers == 2 else {"pipeline_mode": pl.Buffered(w_buffers)}
    in_specs = [pl.BlockSpec((bm, kdim), lambda j, i: (i, 0))]
    args = [x]
    for w, off in ws:
        if w.ndim == 3:
            e, o = off
            assert o % bn == 0 and w.shape[1] == kdim
            in_specs.append(pl.BlockSpec((1, kdim, bn), functools.partial(lambda j, i, e_, o_: (e_, 0, j + o_),
                                                                          e_=e, o_=o // bn), **mode))
        else:
            assert off % bn == 0 and w.shape[0] == kdim
            in_specs.append(pl.BlockSpec((kdim, bn), functools.partial(lambda j, i, o_: (0, j + o_), o_=off // bn),
                                         **mode))
        args.append(w)
    for arr, kind in extras:
        if kind == "mn":
            in_specs.append(pl.BlockSpec((bm, bn), lambda j, i: (i, j)))
        elif kind == "m":
            in_specs.append(pl.BlockSpec((bm, arr.shape[1]), lambda j, i: (i, 0)))
        elif kind == "kn":
            in_specs.append(pl.BlockSpec((arr.shape[0], bn), lambda j, i: (0, j)))
        else:
            in_specs.append(pl.BlockSpec((1, bn), lambda j, i: (0, j)))
        args.append(arr)
    kern = functools.partial(_mm_ws_kernel, n_w=len(ws), n_extra=len(extras), epilogue=epilogue)
    return pl.pallas_call(
        kern,
        grid=(n_out // bn, m // bm),
        in_specs=in_specs,
        out_specs=pl.BlockSpec((bm, bn), lambda j, i: (i, j)),
        out_shape=jax.ShapeDtypeStruct((m, n_out), out_dtype),
        scratch_shapes=[pltpu.VMEM((kdim, bn), BF16) for _ in ws],
        compiler_params=_params("parallel", "arbitrary"),
        name=name,
    )(*args)


def _ep_residual(accs, extras):
    return extras[0] + accs[0]


def _ep_swiglu(accs, extras):
    return _silu(accs[0]) * accs[1]


def _ep_tanh(accs, extras):
    return jnp.tanh(accs[0])


def _ep_sigmoid(accs, extras):
    return jax.nn.sigmoid(accs[0])


def _ep_bias_sigmoid(accs, extras):
    return jax.nn.sigmoid(accs[0] + extras[0])


def _ep_rw_logdecay(accs, extras):
    w = -jax.nn.softplus(-(accs[0] + extras[0])) - 0.5
    return -jnp.exp(w)


def _ep_ple_gate(accs, extras):
    pp = jnp.dot(extras[1].astype(BF16), extras[2], preferred_element_type=F32)
    return extras[0] + pp * jax.nn.sigmoid(accs[0])


def _conv_silu_kernel(x_ref, w_ref, b_ref, o_ref, *, k_width):
    x = x_ref[0]
    row = lax.broadcasted_iota(jnp.int32, x.shape, 0)
    y = b_ref[...] + w_ref[k_width - 1:k_width, :] * x
    for j in range(k_width - 1):
        shift = k_width - 1 - j
        xs = jnp.where(row >= shift, pltpu.roll(x, shift, 0), 0.0)
        y = y + w_ref[j:j + 1, :] * xs
    o_ref[0] = _silu(y)


def conv_silu(x, w, b):
    bsz, s_len, c = x.shape
    cb = _pick(c, 256)
    k_width = w.shape[0]
    return pl.pallas_call(
        functools.partial(_conv_silu_kernel, k_width=k_width),
        grid=(bsz, c // cb),
        in_specs=[pl.BlockSpec((1, s_len, cb), lambda b_, j: (b_, 0, j)),
                  pl.BlockSpec((k_width, cb), lambda b_, j: (0, j)),
                  pl.BlockSpec((1, cb), lambda b_, j: (0, j))],
        out_specs=pl.BlockSpec((1, s_len, cb), lambda b_, j: (b_, 0, j)),
        out_shape=jax.ShapeDtypeStruct(x.shape, F32),
        compiler_params=_params("parallel", "parallel"),
        name="mamba_conv_silu",
    )(x, w, b.reshape(1, c))


def _cumsum_rows(x, n):
    row = lax.broadcasted_iota(jnp.int32, x.shape, 0)
    s = 1
    while s < n:
        x = x + jnp.where(row >= s, pltpu.roll(x, s, 0), 0.0)
        s *= 2
    return x


def _cumsum_lanes(x, n):
    col = lax.broadcasted_iota(jnp.int32, x.shape, 1)
    s = 1
    while s < n:
        x = x + jnp.where(col >= s, pltpu.roll(x, s, 1), 0.0)
        s *= 2
    return x


def _dot_nt(a, b):
    return lax.dot_general(a, b, (((1,), (1,)), ((), ())), preferred_element_type=F32)


def _dot_tn(a, b):
    return lax.dot_general(a, b, (((0,), (0,)), ((), ())), preferred_element_type=F32)


def _ssd_kernel(xs_ref, b_ref, c_ref, z_ref, dt_ref, dtt_ref, bias_r_ref, bias_c_ref, alog_r_ref, alog_c_ref,
                dskip_ref, normw_ref, o_ref, state_ref, y_ref, *, chunk, heads, p_dim):
    @pl.when(pl.program_id(2) == 0)
    def _():
        state_ref[...] = jnp.zeros_like(state_ref)

    dt = jax.nn.softplus(dt_ref[0, 0] + bias_r_ref[0])
    dtt = jax.nn.softplus(dtt_ref[0, 0] + bias_c_ref[0])
    a_cum = _cumsum_rows(dt * -jnp.exp(alog_r_ref[0]), chunk)
    a_cum_t = _cumsum_lanes(dtt * -jnp.exp(alog_c_ref[0]), chunk)
    xs = xs_ref[0]
    bmat = b_ref[0]
    cmat = c_ref[0].astype(BF16)
    cb = _dot_nt(cmat, bmat.astype(BF16))
    b_t = bmat.T.astype(BF16)
    li = lax.broadcasted_iota(jnp.int32, (chunk, chunk), 0)
    si = lax.broadcasted_iota(jnp.int32, (chunk, chunk), 1)
    causal = li >= si
    per = LANES // p_dim
    lane_seg = lax.broadcasted_iota(jnp.int32, (1, LANES), 1) // p_dim

    def pick(vals):
        out = vals[-1]
        for i in range(per - 2, -1, -1):
            out = jnp.where(lane_seg == i, vals[i], out)
        return out

    dot = functools.partial(jnp.dot, preferred_element_type=F32)
    es = range(heads)
    tiles = range(heads // per)
    head_row = lax.broadcasted_iota(jnp.int32, (heads, heads * LANES), 0)
    to_tile = jnp.where(lax.broadcasted_iota(jnp.int32, (heads, heads * LANES), 1) // LANES == head_row, 1.0, 0.0)
    cum_t = _dot_hi(a_cum, to_tile)
    of = lambda vals, i: [vals[i * per + j] for j in range(per)]
    tile = lambda x, i: x[:, i * LANES:(i + 1) * LANES]
    cum_c = jnp.concatenate([pick([tile(cum_t, e) for e in of(es, i)]) for i in tiles], axis=-1)
    dt_c = jnp.concatenate([pick([dt[:, e:e + 1] for e in of(es, i)]) for i in tiles], axis=-1)
    last_c = cum_c[chunk - 1:chunk, :]
    m = [(cb * jnp.exp(jnp.where(causal, tile(cum_t, e) - a_cum_t[e:e + 1, :], -jnp.inf))).astype(BF16) for e in es]
    xdt = xs * dt_c
    xdt_b = xdt.astype(BF16)
    xend_b = (xdt * jnp.exp(last_c - cum_c)).astype(BF16)
    grow = jnp.exp(cum_c)
    st_decay = jnp.exp(last_c)
    st = [state_ref[i] for i in tiles]
    y_in = [pick([dot(m[e], tile(xdt_b, i)) for e in of(es, i)]) for i in tiles]
    y_st = [dot(cmat, st[i].astype(BF16)) * tile(grow, i) for i in tiles]
    for i in tiles:
        state_ref[i] = st[i] * tile(st_decay, i) + dot(b_t, tile(xend_b, i))
        y_ref[:, i * LANES:(i + 1) * LANES] = y_in[i] + y_st[i]
    y = y_ref[...] + xs * dskip_ref[...]
    y = y * _silu(z_ref[0])
    ms = jnp.mean(y * y, axis=-1, keepdims=True)
    o_ref[0] = (y * lax.rsqrt(ms + NORM_EPS) * normw_ref[...]).astype(o_ref.dtype)


def ssd_scan(xbc, z, dt, dt_bias, a_log, d_skip, norm_w, *, chunk=MB_CHUNK):
    bsz, s_len, d_inner = z.shape
    n_heads = dt.shape[-1]
    n_state = MB_D_STATE
    groups = (xbc.shape[-1] - d_inner) // (2 * n_state)
    heads = n_heads // groups
    p_dim = d_inner // n_heads
    gw = heads * p_dim
    assert gw % LANES == 0 and d_inner % n_state == 0
    chunk = min(chunk, s_len)
    nc = s_len // chunk
    b_off = d_inner // n_state
    c_off = b_off + groups
    dt_g = jnp.transpose(dt.reshape(bsz, s_len, groups, heads), (0, 2, 1, 3))
    dt_gt = jnp.transpose(dt_g, (0, 1, 3, 2))
    kern = functools.partial(_ssd_kernel, chunk=chunk, heads=heads, p_dim=p_dim)
    per_group = lambda b_, g, c: (g, 0, 0)
    return pl.pallas_call(
        kern,
        grid=(bsz, groups, nc),
        in_specs=[pl.BlockSpec((1, chunk, gw), lambda b_, g, c: (b_, c, g)),
                  pl.BlockSpec((1, chunk, n_state), lambda b_, g, c: (b_, c, b_off + g)),
                  pl.BlockSpec((1, chunk, n_state), lambda b_, g, c: (b_, c, c_off + g)),
                  pl.BlockSpec((1, chunk, gw), lambda b_, g, c: (b_, c, g)),
                  pl.BlockSpec((1, 1, chunk, heads), lambda b_, g, c: (b_, g, c, 0)),
                  pl.BlockSpec((1, 1, heads, chunk), lambda b_, g, c: (b_, g, 0, c)),
                  pl.BlockSpec((1, 1, heads), per_group),
                  pl.BlockSpec((1, heads, 1), per_group),
                  pl.BlockSpec((1, 1, heads), per_group),
                  pl.BlockSpec((1, heads, 1), per_group),
                  pl.BlockSpec((1, gw), lambda b_, g, c: (0, g)),
                  pl.BlockSpec((1, gw), lambda b_, g, c: (0, g))],
        out_specs=pl.BlockSpec((1, chunk, gw), lambda b_, g, c: (b_, c, g)),
        out_shape=jax.ShapeDtypeStruct(z.shape, BF16),
        scratch_shapes=[pltpu.VMEM((gw // LANES, n_state, LANES), F32), pltpu.VMEM((chunk, gw), F32)],
        compiler_params=_params("parallel", "parallel", "arbitrary"),
        name="mamba_ssd",
    )(xbc, xbc, xbc, z, dt_g, dt_gt,
      dt_bias.reshape(groups, 1, heads), dt_bias.reshape(groups, heads, 1),
      a_log.reshape(groups, 1, heads), a_log.reshape(groups, heads, 1),
      jnp.repeat(d_skip, p_dim).reshape(1, d_inner), norm_w.reshape(1, d_inner))


def mamba2_mixer(u, h, w, bsz, s_len):
    d_inner = w["mb_w_out"].shape[0]
    n_heads = w["mb_dt_bias"].shape[0]
    w_in = w["mb_w_in"]
    xbc_w = w_in.shape[1] - d_inner - n_heads
    z = matmul_ws(u, [(w_in, 0)], d_inner, name="mb_in_z")
    xbc = matmul_ws(u, [(w_in, d_inner)], xbc_w, name="mb_in_xbc")
    dt = matmul_ws(u, [(w_in, d_inner + xbc_w)], n_heads, name="mb_in_dt")
    xbc = conv_silu(xbc.reshape(bsz, s_len, xbc_w), w["mb_conv_w"], w["mb_conv_b"])
    y = ssd_scan(xbc, z.reshape(bsz, s_len, d_inner), dt.reshape(bsz, s_len, n_heads),
                 w["mb_dt_bias"], w["mb_a_log"], w["mb_d_skip"], w["mb_norm_w"])
    return matmul_ws(y.reshape(bsz * s_len, d_inner), [(w["mb_w_out"], 0)], h.shape[1],
                     epilogue=_ep_residual, extras=[(h, "mn")], bm=512, w_buffers=1, name="mb_out")


def _seg_cumsum_rows(x, seg, reverse=False):
    n = x.shape[0]
    pos = lax.broadcasted_iota(jnp.int32, x.shape, 0) % seg
    s = 1
    while s < seg:
        if reverse:
            x = x + jnp.where(pos < seg - s, pltpu.roll(x, n - s, 0), 0.0)
        else:
            x = x + jnp.where(pos >= s, pltpu.roll(x, s, 0), 0.0)
        s *= 2
    return x


def _hgrn_kernel(q_ref, f_ref, i_ref, g_ref, lb_ref, nw_ref, o_ref, state_ref, *, sub, n_sub, heads, dk):
    @pl.when(pl.program_id(2) == 0)
    def _():
        state_ref[...] = jnp.zeros_like(state_ref)

    lb = lb_ref[...]
    nw = nw_ref[...]
    ti = lax.broadcasted_iota(jnp.int32, (sub, sub), 0)
    si = lax.broadcasted_iota(jnp.int32, (sub, sub), 1)
    causal = ti >= si
    f = lb + (1.0 - lb) * jax.nn.sigmoid(f_ref[0])
    lf = jnp.log(f)
    k = 1.0 - f
    b = _seg_cumsum_rows(lf, sub)
    to_end = _seg_cumsum_rows(lf, sub, reverse=True) - lf
    q_dec = (_silu(q_ref[0]) * jnp.exp(b)).astype(BF16)
    k_dec = (k * jnp.exp(-b)).astype(BF16)
    k_end = (k * jnp.exp(to_end)).astype(BF16)
    v = i_ref[0].astype(BF16)
    cs = range(n_sub)
    hs = range(heads)
    blk = lambda x, c, h: x[c * sub:(c + 1) * sub, h * dk:(h + 1) * dk]
    scores = [[jnp.where(causal, _dot_nt(blk(q_dec, c, h), blk(k_dec, c, h)), 0.0).astype(BF16) for h in hs]
              for c in cs]
    upd = [[_dot_tn(blk(v, c, h), blk(k_end, c, h)) for h in hs] for c in cs]
    states = []
    st = [state_ref[h] for h in hs]
    for c in cs:
        states.append(st)
        decay = jnp.exp(b[(c + 1) * sub - 1:(c + 1) * sub, :])
        st = [st[h] * decay[:, h * dk:(h + 1) * dk] + upd[c][h] for h in hs]
    for h in hs:
        state_ref[h] = st[h]
    for c in cs:
        rows = slice(c * sub, (c + 1) * sub)
        for h in hs:
            o = (jnp.dot(scores[c][h], blk(v, c, h), preferred_element_type=F32)
                 + _dot_nt(blk(q_dec, c, h), states[c][h].astype(BF16)))
            o = o * lax.rsqrt(jnp.mean(o * o, axis=-1, keepdims=True) + NORM_EPS) * nw
            cols = slice(h * dk, (h + 1) * dk)
            o_ref[0, rows, cols] = (o * _silu(g_ref[0, rows, cols])).astype(o_ref.dtype)


def hgrn2_scan(proj, lower_bound, norm_w, *, dk=HG_HEAD_DIM, sub=HG_CHUNK, tb=256, heads=8):
    bsz, s_len, d4 = proj.shape
    d = d4 // 4
    n_heads = d // dk
    tb = min(tb, s_len)
    heads = min(heads, n_heads)
    hw = heads * dk
    n_hb = n_heads // heads
    kern = functools.partial(_hgrn_kernel, sub=sub, n_sub=tb // sub, heads=heads, dk=dk)
    spec = lambda part: pl.BlockSpec((1, tb, hw), lambda b_, h_, t: (b_, t, part * n_hb + h_))
    return pl.pallas_call(
        kern,
        grid=(bsz, n_hb, s_len // tb),
        in_specs=[spec(0), spec(1), spec(2), spec(3),
                  pl.BlockSpec((1, hw), lambda b_, h_, t: (0, h_)),
                  pl.BlockSpec((1, dk), lambda b_, h_, t: (0, 0))],
        out_specs=pl.BlockSpec((1, tb, hw), lambda b_, h_, t: (b_, t, h_)),
        out_shape=jax.ShapeDtypeStruct((bsz, s_len, d), BF16),
        scratch_shapes=[pltpu.VMEM((heads, dk, dk), F32)],
        compiler_params=_params("parallel", "parallel", "arbitrary"),
        name="hgrn2_scan",
    )(proj, proj, proj, proj, lower_bound.reshape(1, d), norm_w.reshape(1, dk))


def hgrn2_mixer(u, h, w, lower_bound, bsz, s_len):
    d = h.shape[1]
    proj = matmul_ws(u, [(w["hg_w_in"], 0)], 4 * d, bm=512, bn=1024, name="hg_in")
    o = hgrn2_scan(proj.reshape(bsz, s_len, 4 * d), lower_bound, w["hg_norm_w"])
    return matmul_ws(o.reshape(bsz * s_len, d), [(w["hg_w_out"], 0)], d,
                     epilogue=_ep_residual, extras=[(h, "mn")], name="hg_out")


def dense_ffn(v, h, w_in, w_out):
    f = w_out.shape[0]
    hid = matmul_ws(v, [(w_in, 0), (w_in, f)], f, epilogue=_ep_swiglu, out_dtype=BF16, bm=1024, bn=256,
                    name="ffn_in")
    return matmul_ws(hid, [(w_out, 0)], h.shape[1], epilogue=_ep_residual, extras=[(h, "mn")], bm=512, w_buffers=1,
                     name="ffn_out")


def _rmsnorm_router_kernel(x_ref, g_ref, r_ref, o_ref, c_ref, *, n_experts):
    x = x_ref[...]
    ms = jnp.mean(x * x, axis=-1, keepdims=True)
    v = (x * lax.rsqrt(ms + NORM_EPS) * g_ref[...]).astype(BF16)
    o_ref[...] = v
    logits = jnp.dot(v, r_ref[...], preferred_element_type=F32)
    lane = lax.broadcasted_iota(jnp.int32, logits.shape, 1)
    logits = jnp.where(lane < n_experts, logits, -jnp.inf)
    m1 = jnp.max(logits, axis=-1, keepdims=True)
    i1 = jnp.min(jnp.where(logits == m1, lane, LANES), axis=-1, keepdims=True)
    rest = jnp.where(lane == i1, -jnp.inf, logits)
    m2 = jnp.max(rest, axis=-1, keepdims=True)
    i2 = jnp.min(jnp.where(rest == m2, lane, LANES), axis=-1, keepdims=True)
    e2 = jnp.exp(m2 - m1)
    w1 = 1.0 / (1.0 + e2)
    c_ref[...] = jnp.where(lane == i1, w1, 0.0) + jnp.where(lane == i2, e2 * w1, 0.0)


def rmsnorm_router(x, gain, router):
    m, d = x.shape
    n_experts = router.shape[1]
    r_pad = jnp.zeros((d, LANES), BF16).at[:, :n_experts].set(router.astype(BF16))
    bm = _pick(m, 256)
    return pl.pallas_call(
        functools.partial(_rmsnorm_router_kernel, n_experts=n_experts),
        grid=(m // bm,),
        in_specs=[pl.BlockSpec((bm, d), lambda i: (i, 0)), pl.BlockSpec((1, d), lambda i: (0, 0)),
                  pl.BlockSpec((d, LANES), lambda i: (0, 0))],
        out_specs=[pl.BlockSpec((bm, d), lambda i: (i, 0)), pl.BlockSpec((bm, LANES), lambda i: (i, 0))],
        out_shape=[jax.ShapeDtypeStruct((m, d), BF16), jax.ShapeDtypeStruct((m, LANES), F32)],
        compiler_params=_params("parallel"),
        name="rmsnorm_router",
    )(x, gain.reshape(1, d).astype(F32), r_pad)


MOE_BLOCK = 1024
MOE_UNIT = 32
MOE_TILE = 512


def _moe_gather_kernel(x_ref, tok_ref, o_ref):
    tok = tok_ref[0]
    lane = lax.broadcasted_iota(jnp.int32, (tok.shape[0], x_ref.shape[0]), 1)
    onehot = jnp.where(tok == lane, 1.0, 0.0).astype(BF16)
    o_ref[...] = jnp.dot(onehot, x_ref[...], preferred_element_type=F32).astype(o_ref.dtype)


def _moe_expert_in_kernel(src_ref, exp_ref, first_ref, used_ref, *refs, per):
    x_refs = refs[:per]
    wg_ref, wu_ref, rw_ref, o_ref, wgb_ref, wub_ref, x_scr = refs[per:]
    t = pl.program_id(1)

    @pl.when(t < used_ref[0])
    def _():
        @pl.when(first_ref[t] == 1)
        def _():
            wgb_ref[...] = wg_ref[0].astype(BF16)
            wub_ref[...] = wu_ref[0].astype(BF16)

        unit = x_refs[0].shape[0]
        for i in range(per):
            x_scr[i * unit:(i + 1) * unit, :] = x_refs[i][...]
        x = x_scr[...]
        g = jnp.dot(x, wgb_ref[...], preferred_element_type=F32)
        u = jnp.dot(x, wub_ref[...], preferred_element_type=F32)
        o_ref[...] = (_silu(g) * u * rw_ref[...]).astype(o_ref.dtype)

    @pl.when(t >= used_ref[0])
    def _():
        o_ref[...] = jnp.zeros_like(o_ref)


def _moe_expert_out_kernel(exp_ref, first_ref, used_ref, hid_ref, w_ref, o_ref, wb_ref):
    t = pl.program_id(1)

    @pl.when(t < used_ref[0])
    def _():
        @pl.when(first_ref[t] == 1)
        def _():
            wb_ref[...] = w_ref[0].astype(BF16)

        o_ref[...] = jnp.dot(hid_ref[...], wb_ref[...], preferred_element_type=F32).astype(o_ref.dtype)

    @pl.when(pl.program_id(1) >= used_ref[0])
    def _():
        o_ref[...] = jnp.zeros_like(o_ref)


def _moe_scatter_kernel(dst_ref, h_ref, tok_ref, *refs, per):
    y_refs = refs[:per]
    o_ref, y_scr = refs[per:]

    @pl.when(pl.program_id(2) == 0)
    def _():
        o_ref[...] = h_ref[...]

    tok = tok_ref[0]
    row = lax.broadcasted_iota(jnp.int32, (o_ref.shape[0], tok.shape[1]), 0)
    onehot_t = jnp.where(tok == row, 1.0, 0.0).astype(BF16)
    unit = y_refs[0].shape[0]
    for i in range(per):
        y_scr[i * unit:(i + 1) * unit, :] = y_refs[i][...]
    o_ref[...] += jnp.dot(onehot_t, y_scr[...], preferred_element_type=F32)


def moe_ffn_routed(v, h, comb, w_in, w_out, *, tb=MOE_BLOCK, unit=MOE_UNIT, tile=MOE_TILE, bn=512, bo=1024):
    m, d = v.shape
    n_experts, _, two_de = w_in.shape
    de = two_de // 2
    tb = min(tb, m)
    nb = m // tb
    per = tile // unit
    bn = _pick(de, bn)
    bo = _pick(d, bo)
    n_assign = MOE_TOPK * tb
    n_slots = -(-(n_assign // unit + n_experts + 1) // per) * per
    groups = n_slots // per
    n_units = nb * (n_assign // unit + n_experts) + n_experts * (per - 1)
    n_tiles = -(-n_units // per)
    n_units = n_tiles * per

    wts, ids = lax.top_k(comb[:, :n_experts], MOE_TOPK)
    ea = ids.reshape(nb, n_assign)
    wa = wts.reshape(nb, n_assign)
    ta = jnp.broadcast_to(jnp.repeat(jnp.arange(tb, dtype=jnp.int32), MOE_TOPK)[None], (nb, n_assign))
    se, st, sw = lax.sort((ea, ta, wa), dimension=1, num_keys=1, is_stable=True)
    counts = jnp.sum(jax.nn.one_hot(ea, n_experts, dtype=jnp.int32), axis=1)
    units = -(-counts // unit)
    excl = lambda x, axis: jnp.cumsum(x, axis=axis) - x
    slot_start = excl(units, 1)
    row_start = excl(counts, 1)
    is_e = se[..., None] == jnp.arange(n_experts, dtype=jnp.int32)
    lookup = lambda table: jnp.sum(jnp.where(is_e, table[:, None, :], 0), axis=-1)
    pos = lookup(slot_start) * unit + jnp.arange(n_assign, dtype=jnp.int32)[None] - lookup(row_start)
    bidx = jnp.arange(nb, dtype=jnp.int32)[:, None]
    n_rows = n_slots * unit
    rr = jnp.arange(2 * n_rows, dtype=jnp.int32)[None, :, None]
    hit = pos[:, None, :] == rr % n_rows
    val = jnp.where(rr < n_rows, (st + 1).astype(F32)[:, None, :], sw[:, None, :])
    tab = jnp.sum(jnp.where(hit, val, 0.0), axis=-1)
    row_token = tab[:, :n_rows].astype(jnp.int32) - 1
    row_weight = tab[:, n_rows:]
    per_expert = jnp.sum(units, axis=0)
    per_expert_pad = -(-per_expert // per) * per
    e_off = excl(per_expert_pad, 0)
    before = excl(units, 0)
    slot = jnp.arange(n_slots, dtype=jnp.int32)
    slot_end = jnp.cumsum(units, axis=1)
    e_of_slot = jnp.sum(slot[None, :, None] >= slot_end[:, None, :], axis=-1)
    used_slot = e_of_slot < n_experts
    e_clip = jnp.minimum(e_of_slot, n_experts - 1)
    dst_unit = (e_off[e_clip] + jnp.take_along_axis(before, e_clip, axis=1)
                + slot[None] - jnp.take_along_axis(slot_start, e_clip, axis=1))
    dst_unit = jnp.where(used_slot, dst_unit, 0).astype(jnp.int32)
    flat_slot = (bidx * n_slots + slot[None]).astype(jnp.int32)
    zero_slot = n_slots - 1
    src_unit = jnp.full((n_units,), zero_slot, jnp.int32).at[
        jnp.where(used_slot, dst_unit, n_units).reshape(-1)].set(flat_slot.reshape(-1), mode="drop")
    tile_end = jnp.cumsum(per_expert_pad) // per
    tile_ids = jnp.arange(n_tiles, dtype=jnp.int32)
    tile_expert = jnp.minimum(jnp.sum(tile_ids[:, None] >= tile_end[None, :], axis=-1), n_experts - 1).astype(jnp.int32)
    tiles_used = tile_end[-1:].astype(jnp.int32)
    first = jnp.concatenate([jnp.ones((1,), jnp.int32),
                             (tile_expert[1:] != tile_expert[:-1]).astype(jnp.int32)])
    rw_em = row_weight.reshape(nb * n_slots, unit)[src_unit].reshape(n_units * unit, 1)

    xs = pl.pallas_call(
        _moe_gather_kernel,
        grid=(nb, groups),
        in_specs=[pl.BlockSpec((tb, d), lambda b_, g: (b_, 0)),
                  pl.BlockSpec((1, tile, 1), lambda b_, g: (b_, g, 0))],
        out_specs=pl.BlockSpec((tile, d), lambda b_, g: (b_ * groups + g, 0)),
        out_shape=jax.ShapeDtypeStruct((nb * n_slots * unit, d), BF16),
        compiler_params=_params("parallel", "arbitrary"),
        name="moe_gather",
    )(v, row_token.reshape(nb, n_slots * unit, 1))

    bpe = de // bn
    unit_spec = lambda i: pl.BlockSpec((unit, d), lambda j, t, src, ex, fi, us: (src[per * t + i], 0))
    hid = pl.pallas_call(
        functools.partial(_moe_expert_in_kernel, per=per),
        grid_spec=pltpu.PrefetchScalarGridSpec(
            num_scalar_prefetch=4,
            grid=(bpe, n_tiles),
            in_specs=[unit_spec(i) for i in range(per)] + [
                pl.BlockSpec((1, d, bn), lambda j, t, src, ex, fi, us: (ex[t], 0, j),
                             pipeline_mode=pl.Buffered(1)),
                pl.BlockSpec((1, d, bn), lambda j, t, src, ex, fi, us: (ex[t], 0, j + bpe),
                             pipeline_mode=pl.Buffered(1)),
                pl.BlockSpec((tile, 1), lambda j, t, src, ex, fi, us: (t, 0))],
            out_specs=pl.BlockSpec((tile, bn), lambda j, t, src, ex, fi, us: (t, j)),
            scratch_shapes=[pltpu.VMEM((d, bn), BF16), pltpu.VMEM((d, bn), BF16), pltpu.VMEM((tile, d), BF16)]),
        out_shape=jax.ShapeDtypeStruct((n_tiles * tile, de), BF16),
        compiler_params=_params("arbitrary", "arbitrary"),
        name="moe_expert_in",
    )(src_unit, tile_expert, first, tiles_used, *([xs] * per), w_in, w_in, rw_em)

    bo2 = _pick(d, 2 * bo)
    ys = pl.pallas_call(
        _moe_expert_out_kernel,
        grid_spec=pltpu.PrefetchScalarGridSpec(
            num_scalar_prefetch=3,
            grid=(d // bo2, n_tiles),
            in_specs=[pl.BlockSpec((tile, de), lambda n, t, ex, fi, us: (t, 0)),
                      pl.BlockSpec((1, de, bo2), lambda n, t, ex, fi, us: (ex[t], 0, n))],
            out_specs=pl.BlockSpec((tile, bo2), lambda n, t, ex, fi, us: (t, n)),
            scratch_shapes=[pltpu.VMEM((de, bo2), BF16)]),
        out_shape=jax.ShapeDtypeStruct((n_tiles * tile, d), BF16),
        compiler_params=_params("arbitrary", "arbitrary"),
        name="moe_expert_out",
    )(tile_expert, first, tiles_used, hid, w_out)

    y_spec = lambda i: pl.BlockSpec((unit, bo2), lambda b_, n, g, dst: (dst[(b_ * groups + g) * per + i], n))
    return pl.pallas_call(
        functools.partial(_moe_scatter_kernel, per=per),
        grid_spec=pltpu.PrefetchScalarGridSpec(
            num_scalar_prefetch=1,
            grid=(nb, d // bo2, groups),
            in_specs=[pl.BlockSpec((tb, bo2), lambda b_, n, g, dst: (b_, n)),
                      pl.BlockSpec((1, 1, tile), lambda b_, n, g, dst: (b_ * groups + g, 0, 0))]
                     + [y_spec(i) for i in range(per)],
            out_specs=pl.BlockSpec((tb, bo2), lambda b_, n, g, dst: (b_, n)),
            scratch_shapes=[pltpu.VMEM((tile, bo2), BF16)]),
        out_shape=jax.ShapeDtypeStruct((m, d), F32),
        compiler_params=_params("parallel", "parallel", "arbitrary"),
        name="moe_scatter",
    )(dst_unit.reshape(-1), h, row_token.reshape(nb * groups, 1, tile), *([ys] * per))


def ple_gate(h, p_i, norm_pl, pl_proj, pl_gate, layer):
    d = h.shape[1]
    n = rmsnorm(h, norm_pl, name="rmsnorm_ple")
    return matmul_ws(n, [(pl_gate, (layer, 0))], d, epilogue=_ep_ple_gate,
                     extras=[(h, "mn"), (p_i, "m"), (pl_proj, "kn")], name="ple_gate")


def _rw_mix_kernel(u_ref, mu_ref, *o_refs):
    u = u_ref[0]
    row = lax.broadcasted_iota(jnp.int32, u.shape, 0)
    dx = jnp.where(row >= 1, pltpu.roll(u, 1, 0), 0.0) - u
    for j, o_ref in enumerate(o_refs):
        o_ref[0] = (u + dx * mu_ref[j:j + 1, :]).astype(o_ref.dtype)


def rw_token_mix(u, mu):
    bsz, s_len, d = u.shape
    cb = _pick(d, LANES)
    n_mix = mu.shape[0]
    spec = pl.BlockSpec((1, s_len, cb), lambda b_, j: (b_, 0, j))
    return pl.pallas_call(
        _rw_mix_kernel,
        grid=(bsz, d // cb),
        in_specs=[spec, pl.BlockSpec((n_mix, cb), lambda b_, j: (0, j))],
        out_specs=[spec] * n_mix,
        out_shape=[jax.ShapeDtypeStruct(u.shape, BF16)] * n_mix,
        compiler_params=_params("parallel", "parallel"),
        name="rwkv_token_mix",
    )(u, mu)


def _dot_hi(a, b):
    return jnp.dot(a, b, preferred_element_type=F32, precision=lax.Precision.HIGHEST)


def _rw_scan_tile_kernel(r_ref, k_ref, v_ref, a_ref, lw_ref, g_ref, kk_ref, ka_ref, rk_ref, lnw_ref, lnb_ref,
                         o_ref, state_ref, *, chunk, heads, n):
    @pl.when(pl.program_id(2) == 0)
    def _():
        state_ref[...] = jnp.zeros_like(state_ref)

    per = LANES // n
    tiles = range(heads // per)
    sub = range(per)
    ti = lax.broadcasted_iota(jnp.int32, (chunk, chunk), 0)
    si = lax.broadcasted_iota(jnp.int32, (chunk, chunk), 1)
    strict = ti > si
    incl = ti >= si
    lane_seg = lax.broadcasted_iota(jnp.int32, (1, LANES), 1) // n
    seg_is = [lane_seg == j for j in sub]
    same_head = (lax.broadcasted_iota(jnp.int32, (LANES, LANES), 0) // n
                 == lax.broadcasted_iota(jnp.int32, (LANES, LANES), 1) // n)
    dot = functools.partial(jnp.dot, preferred_element_type=F32)
    tile = lambda x, i: x[:, i * LANES:(i + 1) * LANES]

    def pick(vals):
        out = vals[-1]
        for j in range(per - 2, -1, -1):
            out = jnp.where(seg_is[j], vals[j], out)
        return out

    def seg_sum(x):
        return pick([jnp.sum(jnp.where(seg_is[j], x, 0.0), axis=-1, keepdims=True) for j in sub])

    r, k, v, a, lw = r_ref[0], k_ref[0], v_ref[0], a_ref[0], lw_ref[0]
    kk = k * kk_ref[...]
    kmod = k * (1.0 + (a - 1.0) * ka_ref[...])
    cum = _cumsum_rows(lw, chunk)
    cum_end = cum[chunk - 1:chunk, :]
    mid = cum[chunk // 2 - 1:chunk // 2, :]
    bonus_in = r * kmod * rk_ref[...]
    kk_t, bonus_t = [], []
    for i in tiles:
        kki = tile(kk, i)
        kk_t.append(kki / jnp.maximum(jnp.sqrt(seg_sum(kki * kki)), 1e-12))
        bonus_t.append(seg_sum(tile(bonus_in, i)) * tile(v, i))
    kk = jnp.concatenate(kk_t, axis=-1) if len(kk_t) > 1 else kk_t[0]
    kka = kk * a
    e_neg = jnp.exp(mid - cum)
    to_end = jnp.exp(cum_end - cum)
    am = (kk * jnp.exp(cum - lw - mid)).astype(BF16)
    bm = (kka * e_neg).astype(BF16)
    km = (kmod * e_neg).astype(BF16)
    rm = (r * jnp.exp(cum - mid)).astype(BF16)
    a_abs = (kk * jnp.exp(cum - lw)).astype(BF16)
    r_abs = (r * jnp.exp(cum)).astype(BF16)
    k_end = (kmod * to_end).astype(BF16)
    b_end = (kka * to_end).astype(BF16)
    vb = v.astype(BF16)
    st_decay = jnp.exp(cum_end)
    zero = jnp.zeros((), BF16)

    st = [state_ref[i] for i in tiles]
    stb = [s.astype(BF16) for s in st]
    am_h = [[jnp.where(seg_is[j], tile(am, i), zero) for j in sub] for i in tiles]
    rm_h = [[jnp.where(seg_is[j], tile(rm, i), zero) for j in sub] for i in tiles]
    nb = [[(-jnp.where(strict, _dot_nt(am_h[i][j], tile(bm, i)), 0.0)).astype(BF16) for j in sub] for i in tiles]
    lk = [[jnp.where(strict, _dot_nt(am_h[i][j], tile(km, i)), 0.0).astype(BF16) for j in sub] for i in tiles]
    x = [_dot_nt(tile(a_abs, i), stb[i]) + pick([dot(lk[i][j], tile(vb, i)) for j in sub]) for i in tiles]
    xb = [xi.astype(BF16) for xi in x]
    x = [x[i] + pick([dot(nb[i][j], xb[i]) for j in sub]) for i in tiles]
    p = 2
    while p < chunk:
        nb = [[dot(nb[i][j], nb[i][j]).astype(BF16) for j in sub] for i in tiles]
        xb = [xi.astype(BF16) for xi in x]
        x = [x[i] + pick([dot(nb[i][j], xb[i]) for j in sub]) for i in tiles]
        p *= 2
    pb = [xi.astype(BF16) for xi in x]
    mk = [[jnp.where(incl, _dot_nt(rm_h[i][j], tile(km, i)), 0.0).astype(BF16) for j in sub] for i in tiles]
    mb = [[jnp.where(incl, _dot_nt(rm_h[i][j], tile(bm, i)), 0.0).astype(BF16) for j in sub] for i in tiles]
    y = [_dot_nt(tile(r_abs, i), stb[i])
         + pick([dot(mk[i][j], tile(vb, i)) - dot(mb[i][j], pb[i]) for j in sub]) for i in tiles]
    for i in tiles:
        upd = _dot_tn(tile(vb, i), tile(k_end, i)) - _dot_tn(pb[i], tile(b_end, i))
        state_ref[i] = st[i] * tile(st_decay, i) + jnp.where(same_head, upd, 0.0)
    inv_n = 1.0 / n
    for i in tiles:
        cols = slice(i * LANES, (i + 1) * LANES)
        mean = seg_sum(y[i]) * inv_n
        yc = y[i] - mean
        var = seg_sum(yc * yc) * inv_n
        yn = yc * lax.rsqrt(var + RW_LN_EPS) * lnw_ref[:, cols] + lnb_ref[:, cols]
        o_ref[0, :, cols] = ((yn + bonus_t[i]) * g_ref[0, :, cols]).astype(o_ref.dtype)


def rw_scan(r, k, v, a, lw, g, k_k, k_a, r_k, ln_w, ln_b, *, n=RW_HEAD_DIM, chunk=RW_CHUNK, heads=16):
    bsz, s_len, d = r.shape
    chunk = min(chunk, s_len)
    heads = min(heads, d // n)
    hw = heads * n
    seq = pl.BlockSpec((1, chunk, hw), lambda b_, h_, c: (b_, c, h_))
    par = pl.BlockSpec((1, hw), lambda b_, h_, c: (0, h_))
    row = lambda t: t.reshape(1, d)
    assert hw % LANES == 0 and LANES % n == 0
    kern = functools.partial(_rw_scan_tile_kernel, chunk=chunk, heads=heads, n=n)
    return pl.pallas_call(
        kern,
        grid=(bsz, d // hw, s_len // chunk),
        in_specs=[seq] * 6 + [par] * 5,
        out_specs=seq,
        out_shape=jax.ShapeDtypeStruct(r.shape, BF16),
        scratch_shapes=[pltpu.VMEM((hw // LANES, LANES, LANES), F32)],
        compiler_params=_params("parallel", "parallel", "arbitrary"),
        name="rwkv7_scan",
    )(r, k, v, a, lw, g, row(k_k), row(k_a), row(r_k), row(ln_w), row(ln_b))


def rwkv7_mixer(u, h, w, bsz, s_len):
    t, d = u.shape
    xr, xw, xk, xv, xa, xg = [x.reshape(t, d) for x in rw_token_mix(u.reshape(bsz, s_len, d), w["rw_mu"])]
    r = matmul_ws(xr, [(w["rw_w_rkv"], (0, 0))], d, name="rw_r")
    k = matmul_ws(xk, [(w["rw_w_rkv"], (1, 0))], d, name="rw_k")
    v = matmul_ws(xv, [(w["rw_w_rkv"], (2, 0))], d, name="rw_v")
    row = lambda x: x.reshape(1, d)
    w_lo = matmul(xw, [(w["rw_w1"], 0)], w["rw_w1"].shape[1], epilogue=_ep_tanh, out_dtype=BF16, name="rw_w1")
    wide = 2048
    lw = matmul(w_lo, [(w["rw_w2"], 0)], d, epilogue=_ep_rw_logdecay, extras=[(row(w["rw_w0"]), "n")], bn=wide,
                name="rw_w2")
    a_lo = matmul(xa, [(w["rw_a1"], 0)], w["rw_a1"].shape[1], out_dtype=BF16, name="rw_a1")
    a = matmul(a_lo, [(w["rw_a2"], 0)], d, epilogue=_ep_bias_sigmoid, extras=[(row(w["rw_a0"]), "n")], bn=wide,
               name="rw_a2")
    g_lo = matmul(xg, [(w["rw_g1"], 0)], w["rw_g1"].shape[1], epilogue=_ep_sigmoid, out_dtype=BF16, name="rw_g1")
    g = matmul(g_lo, [(w["rw_g2"], 0)], d, bn=wide, name="rw_g2")
    shp = (bsz, s_len, d)
    y = rw_scan(r.reshape(shp), k.reshape(shp), v.reshape(shp), a.reshape(shp), lw.reshape(shp), g.reshape(shp),
                w["rw_k_k"], w["rw_k_a"], w["rw_r_k"], w["rw_ln_w"], w["rw_ln_b"])
    return matmul_ws(y.reshape(t, d), [(w["rw_w_out"], 0)], d, epilogue=_ep_residual, extras=[(h, "mn")],
                     name="rw_out")


NEG_BIG = -1e30


def _rope_kernel(x_ref, cc_ref, ss_ref, o_ref, *, n_q_slots, scale):
    x = x_ref[0]
    out = x * cc_ref[...] + pltpu.roll(x, x.shape[-1] // 2, 1) * ss_ref[...]
    out = out * jnp.where(pl.program_id(2) < n_q_slots, scale, 1.0)
    o_ref[0] = out.astype(o_ref.dtype)


def _rope_tables(pos, dim):
    inv = ROPE_THETA ** (-(jnp.arange(0, dim, 2, dtype=F32) / dim))
    ang = pos.astype(F32)[:, None] * inv[None, :]
    cos, sin = jnp.cos(ang), jnp.sin(ang)
    return jnp.concatenate([cos, cos], axis=-1), jnp.concatenate([-sin, sin], axis=-1)


def nsa_rope(proj, n_q_slots, k_slots, dh, scale, tb=512):
    bsz, s_len, _ = proj.shape
    tb = min(tb, s_len)
    cc, ss = _rope_tables(jnp.arange(s_len), dh)
    n_out = n_q_slots + len(k_slots)

    def in_slot(j):
        slot = j
        for idx, ks in enumerate(k_slots):
            slot = jnp.where(j == n_q_slots + idx, ks, slot)
        return slot

    return pl.pallas_call(
        functools.partial(_rope_kernel, n_q_slots=n_q_slots, scale=scale),
        grid=(bsz, s_len // tb, n_out),
        in_specs=[pl.BlockSpec((1, tb, dh), lambda b_, t, j: (b_, t, in_slot(j))),
                  pl.BlockSpec((tb, dh), lambda b_, t, j: (t, 0)),
                  pl.BlockSpec((tb, dh), lambda b_, t, j: (t, 0))],
        out_specs=pl.BlockSpec((1, tb, dh), lambda b_, t, j: (b_, t, j)),
        out_shape=jax.ShapeDtypeStruct((bsz, s_len, n_out * dh), BF16),
        compiler_params=_params("parallel", "parallel", "arbitrary"),
        name="nsa_rope",
    )(proj, cc, ss)


def _cmp_finish_kernel(z_ref, bias_ref, w2_ref, cc_ref, ss_ref, o_ref, *, hidden, rope):
    z = z_ref[0]
    nc = z.shape[0]
    nxt = pltpu.roll(z[:, hidden:], nc - 1, 0)
    hid = _silu(z[:, :hidden] + nxt + bias_ref[...])
    out = jnp.dot(hid.astype(BF16), w2_ref[...], preferred_element_type=F32)
    if rope:
        out = out * cc_ref[...] + pltpu.roll(out, out.shape[-1] // 2, 1) * ss_ref[...]
    o_ref[0] = out.astype(o_ref.dtype)


def nsa_compress(x, pos_emb, w1, w2, bsz, s_len, groups, dh, rope, transpose_out=False):
    stride, blk = NSA_CMP_STRIDE, NSA_CMP_BLOCK
    nc = s_len // stride
    hidden = w1.shape[-1]
    half = stride * dh
    x16 = jnp.transpose(x.reshape(bsz, nc, stride, groups, dh), (0, 3, 1, 2, 4)).reshape(bsz * groups * nc, half)
    w1f = w1.reshape(blk * dh, hidden)
    wcat = jnp.concatenate([w1f[:half], w1f[half:]], axis=1).astype(BF16)
    z = matmul(x16.astype(BF16), [(wcat, 0)], 2 * hidden, name="nsa_cmp_w1")
    bias = matmul(pos_emb.reshape(1, blk * dh).astype(BF16), [(w1f.astype(BF16), 0)], hidden, name="nsa_cmp_pos")
    cc, ss = _rope_tables(jnp.arange(nc) * stride + blk - 1, dh)
    if transpose_out:
        assert not rope
        return pl.pallas_call(
            functools.partial(_cmp_finish_t_kernel, hidden=hidden),
            grid=(bsz * groups,),
            in_specs=[pl.BlockSpec((1, nc, 2 * hidden), lambda i: (i, 0, 0)),
                      pl.BlockSpec((1, hidden), lambda i: (0, 0)),
                      pl.BlockSpec((dh, hidden), lambda i: (0, 0))],
            out_specs=pl.BlockSpec((1, dh, nc), lambda i: (i, 0, 0)),
            out_shape=jax.ShapeDtypeStruct((bsz * groups, dh, nc), BF16),
            compiler_params=_params("parallel"),
            name="nsa_cmp_finish_t",
        )(z.reshape(bsz * groups, nc, 2 * hidden), bias, w2.T.astype(BF16))
    return pl.pallas_call(
        functools.partial(_cmp_finish_kernel, hidden=hidden, rope=rope),
        grid=(bsz * groups,),
        in_specs=[pl.BlockSpec((1, nc, 2 * hidden), lambda i: (i, 0, 0)),
                  pl.BlockSpec((1, hidden), lambda i: (0, 0)),
                  pl.BlockSpec((hidden, dh), lambda i: (0, 0)),
                  pl.BlockSpec((nc, dh), lambda i: (0, 0)),
                  pl.BlockSpec((nc, dh), lambda i: (0, 0))],
        out_specs=pl.BlockSpec((1, nc, dh), lambda i: (i, 0, 0)),
        out_shape=jax.ShapeDtypeStruct((bsz * groups, nc, dh), BF16),
        compiler_params=_params("parallel"),
        name="nsa_cmp_finish",
    )(z.reshape(bsz * groups, nc, 2 * hidden), bias, w2.astype(BF16), cc, ss)


def _rope_t_kernel(x_ref, cc_ref, ss_ref, o_ref, *, n_rope, scale, group, dh):
    first_slot = pl.program_id(2) * group
    for i in range(group):
        x = x_ref[0, :, i * dh:(i + 1) * dh]
        roped = (x * cc_ref[...] + pltpu.roll(x, dh // 2, 1) * ss_ref[...]) * scale
        out = jnp.where(first_slot + i < n_rope, roped, x)
        o_ref[0, i * dh:(i + 1) * dh, :] = out.T.astype(o_ref.dtype)


def nsa_rope_t(proj, slots, n_rope, dh, scale, tb=512, group=4):
    bsz, s_len, _ = proj.shape
    tb = min(tb, s_len)
    cc, ss = _rope_tables(jnp.arange(s_len), dh)
    assert len(slots) % group == 0
    firsts = slots[::group]
    assert all(f % group == 0 and slots[i * group:(i + 1) * group] == list(range(f, f + group))
               for i, f in enumerate(firsts))
    table = jnp.asarray([f // group for f in firsts], jnp.int32)
    grid_spec = pltpu.PrefetchScalarGridSpec(
        num_scalar_prefetch=1,
        grid=(bsz, s_len // tb, len(firsts)),
        in_specs=[pl.BlockSpec((1, tb, group * dh), lambda b_, t, j, tab: (b_, t, tab[j])),
                  pl.BlockSpec((tb, dh), lambda b_, t, j, tab: (t, 0)),
                  pl.BlockSpec((tb, dh), lambda b_, t, j, tab: (t, 0))],
        out_specs=pl.BlockSpec((1, group * dh, tb), lambda b_, t, j, tab: (b_, j, t)),
    )
    kern = lambda tab, x_ref, cc_ref, ss_ref, o_ref: _rope_t_kernel(x_ref, cc_ref, ss_ref, o_ref, n_rope=n_rope,
                                                                   scale=scale, group=group, dh=dh)
    return pl.pallas_call(
        kern,
        grid_spec=grid_spec,
        out_shape=jax.ShapeDtypeStruct((bsz, len(slots) * dh, s_len), BF16),
        compiler_params=_params("parallel", "parallel", "arbitrary"),
        name="nsa_rope_t",
    )(table, proj, cc, ss)


def _cmp_finish_t_kernel(z_ref, bias_ref, w2_ref, o_ref, *, hidden):
    z = z_ref[0]
    nc = z.shape[0]
    nxt = pltpu.roll(z[:, hidden:], nc - 1, 0)
    hid = _silu(z[:, :hidden] + nxt + bias_ref[...])
    o_ref[0] = _dot_nt(w2_ref[...], hid.astype(BF16)).astype(o_ref.dtype)


def _nsa_cmp_select_t_kernel(q_ref, kc_ref, vc_ref, ov_ref, oc_ref, sel_ref, *, tq, rep, dh, topn):
    qi = pl.program_id(2)
    kc = kc_ref[0]
    vct = vc_ref[0]
    nc = kc.shape[0]
    n_sel = sel_ref.shape[2]
    t = qi * tq + lax.broadcasted_iota(jnp.int32, (nc, tq), 1)
    cmp_end = lax.broadcasted_iota(jnp.int32, (nc, tq), 0) * NSA_CMP_STRIDE + (NSA_CMP_BLOCK - 1)
    visible = cmp_end <= t
    s = [jnp.where(visible, jnp.dot(kc, q_ref[0, r * dh:(r + 1) * dh, :], preferred_element_type=F32), NEG_BIG)
         for r in range(rep)]
    e = [jnp.where(visible, jnp.exp2(x - jnp.max(x, axis=0, keepdims=True)), 0.0) for x in s]
    den = [jnp.sum(x, axis=0, keepdims=True) for x in e]
    p = [e[r] / jnp.where(den[r] > 0, den[r], 1.0) for r in range(rep)]
    for r in range(rep):
        oc_ref[0, r * dh:(r + 1) * dh, :] = jnp.dot(vct, p[r].astype(BF16), preferred_element_type=F32)
    psum = p[0]
    for r in range(1, rep):
        psum = psum + p[r]
    imp = _dot_hi(ov_ref[...], psum)
    blk = lax.broadcasted_iota(jnp.int32, (n_sel, tq), 0)
    cur = (qi * tq + lax.broadcasted_iota(jnp.int32, (n_sel, tq), 1)) // NSA_SEL_BLOCK
    forced = (blk == 0) | (blk == cur) | (blk == cur - 1)
    imp = jnp.where(forced, NSA_FORCED_SCORE, imp)
    imp = jnp.where(blk > cur, -jnp.inf, imp)
    sel = jnp.zeros((n_sel, tq), F32)
    for _ in range(topn):
        m = jnp.max(imp, axis=0, keepdims=True)
        first = jnp.min(jnp.where(imp == m, blk, n_sel), axis=0, keepdims=True)
        hit = blk == first
        sel = jnp.where(hit, 1.0, sel)
        imp = jnp.where(hit, -jnp.inf, imp)
    sel_ref[0, 0] = sel


def _flash_t_init(m_ref, l_ref, acc_ref):
    m_ref[...] = jnp.full_like(m_ref, NEG_BIG)
    l_ref[...] = jnp.zeros_like(l_ref)
    acc_ref[...] = jnp.zeros_like(acc_ref)


def _flash_t_step(q_ref, k, vt, mask, m_ref, l_ref, acc_ref, rep, dh):
    hs = range(rep)
    s = [jnp.where(mask, jnp.dot(k, q_ref[0, r * dh:(r + 1) * dh, :], preferred_element_type=F32), NEG_BIG)
         for r in hs]
    m_old = [m_ref[r] for r in hs]
    m_new = [jnp.maximum(m_old[r], jnp.max(s[r], axis=0, keepdims=True)) for r in hs]
    p = [jnp.exp2(s[r] - m_new[r]).astype(BF16) for r in hs]
    alpha = [jnp.exp2(m_old[r] - m_new[r]) for r in hs]
    pv = [jnp.dot(vt, p[r], preferred_element_type=F32) for r in hs]
    ones = jnp.ones((8, k.shape[0]), BF16)
    psum = [jnp.dot(ones, p[r], preferred_element_type=F32)[0:1] for r in hs]
    for r in hs:
        m_ref[r] = m_new[r]
        l_ref[r] = alpha[r] * l_ref[r] + psum[r]
        acc_ref[r] = acc_ref[r] * alpha[r] + pv[r]


def _nsa_select_t_kernel(qi_ref, kj_ref, q_ref, k_ref, vt_ref, sel_ref, o_ref, m_ref, l_ref, acc_ref,
                         *, tq, kb, rep, dh):
    pair = pl.program_id(2)
    qi = qi_ref[pair]
    kj = kj_ref[pair]

    @pl.when(kj == 0)
    def _():
        _flash_t_init(m_ref, l_ref, acc_ref)

    kpos = kj * kb + lax.broadcasted_iota(jnp.int32, (kb, tq), 0)
    t = qi * tq + lax.broadcasted_iota(jnp.int32, (kb, tq), 1)
    per = kb // NSA_SEL_BLOCK
    chosen = jnp.zeros((kb, tq), F32)
    for i in range(per):
        row = sel_ref[0, 0, pl.ds(kj * per + i, 1), :]
        chosen = jnp.where((kpos - kj * kb) // NSA_SEL_BLOCK == i, row, chosen)
    mask = (chosen > 0) & (kpos <= t)
    _flash_t_step(q_ref, k_ref[0], vt_ref[0], mask, m_ref, l_ref, acc_ref, rep, dh)

    @pl.when(kj * kb + kb > qi * tq + tq - 1)
    def _():
        for r in range(rep):
            l = l_ref[r]
            o_ref[0, r * dh:(r + 1) * dh, :] = acc_ref[r] / jnp.where(l > 0, l, 1.0)


def _nsa_window_t_kernel(q_ref, k_ref, vt_ref, oc_ref, os_ref, g_ref, o_ref, m_ref, l_ref, acc_ref,
                         *, tq, kb, rep, dh, window, n_steps):
    qi = pl.program_id(2)
    w = pl.program_id(3)
    kblk = qi * (tq // kb) - (n_steps - tq // kb) + w

    @pl.when(w == 0)
    def _():
        _flash_t_init(m_ref, l_ref, acc_ref)

    @pl.when(kblk >= 0)
    def _():
        kpos = kblk * kb + lax.broadcasted_iota(jnp.int32, (kb, tq), 0)
        t = qi * tq + lax.broadcasted_iota(jnp.int32, (kb, tq), 1)
        mask = (kpos <= t) & (kpos > t - window)
        _flash_t_step(q_ref, k_ref[0], vt_ref[0], mask, m_ref, l_ref, acc_ref, rep, dh)

    @pl.when(w == n_steps - 1)
    def _():
        gates = g_ref[0, 0]
        for r in range(rep):
            rows = slice(r * dh, (r + 1) * dh)
            l = l_ref[r]
            o_w = acc_ref[r] / jnp.where(l > 0, l, 1.0)
            o = (gates[3 * r:3 * r + 1, :] * oc_ref[0, rows, :] + gates[3 * r + 1:3 * r + 2, :] * os_ref[0, rows, :]
                 + gates[3 * r + 2:3 * r + 3, :] * o_w)
            o_ref[0, :, rows] = o.T.astype(o_ref.dtype)


def nsa_mixer_t(u, h, w, bsz, s_len):
    t, d = u.shape
    dh, groups = NSA_HEAD_DIM, NSA_N_KV
    n_heads = d // dh
    rep = n_heads // groups
    kvw = groups * dh
    qw = n_heads * dh
    main_w = qw + 6 * kvw
    scale = dh ** -0.5
    tq = kb = min(256, s_len)
    nq = s_len // tq
    n_sel = s_len // NSA_SEL_BLOCK
    topn = min(NSA_TOPK, n_sel)
    w_in = w["nsa_w_in"]
    proj = matmul_ws(u, [(w_in, 0)], main_w, name="nsa_in").reshape(bsz, s_len, main_w)
    gates = matmul(u, [(w_in[:, main_w:].astype(BF16), 0)], w_in.shape[1] - main_w, epilogue=_ep_sigmoid,
                   name="nsa_gates")
    gates = jnp.transpose(gates.reshape(bsz, s_len, groups, rep * 3), (0, 2, 3, 1))
    slot = lambda j: (qw + j * kvw) // dh
    qvt = nsa_rope_t(proj, list(range(n_heads)) + [slot(3) + g for g in range(groups)]
                     + [slot(5) + g for g in range(groups)], n_heads, dh, scale * math.log2(math.e), group=groups)
    k_rot = nsa_rope(proj, 0, [slot(2) + g for g in range(groups)] + [slot(4) + g for g in range(groups)], dh, 1.0)
    kc = nsa_compress(proj[..., qw:qw + kvw], w["nsa_cmp_pos_k"], w["nsa_cmp_k_w1"], w["nsa_cmp_k_w2"],
                      bsz, s_len, groups, dh, True)
    vct = nsa_compress(proj[..., qw + kvw:qw + 2 * kvw], w["nsa_cmp_pos_v"], w["nsa_cmp_v_w1"], w["nsa_cmp_v_w2"],
                       bsz, s_len, groups, dh, False, transpose_out=True)
    nc = kc.shape[1]
    cs = jnp.arange(nc)[None, :] * NSA_CMP_STRIDE
    ss = jnp.arange(n_sel)[:, None] * NSA_SEL_BLOCK
    overlap_t = jnp.clip(jnp.minimum(cs + NSA_CMP_BLOCK, ss + NSA_SEL_BLOCK) - jnp.maximum(cs, ss), 0, None)
    overlap_t = overlap_t.astype(F32) / NSA_CMP_BLOCK

    qt_spec3 = pl.BlockSpec((1, rep * dh, tq), lambda b_, g, i: (b_, g, i))
    o_c, sel = pl.pallas_call(
        functools.partial(_nsa_cmp_select_t_kernel, tq=tq, rep=rep, dh=dh, topn=topn),
        grid=(bsz, groups, nq),
        in_specs=[qt_spec3,
                  pl.BlockSpec((1, nc, dh), lambda b_, g, i: (b_ * groups + g, 0, 0)),
                  pl.BlockSpec((1, dh, nc), lambda b_, g, i: (b_ * groups + g, 0, 0)),
                  pl.BlockSpec((n_sel, nc), lambda b_, g, i: (0, 0))],
        out_specs=[qt_spec3, pl.BlockSpec((1, 1, n_sel, tq), lambda b_, g, i: (b_, g, 0, i))],
        out_shape=[jax.ShapeDtypeStruct((bsz, qw, s_len), F32),
                   jax.ShapeDtypeStruct((bsz, groups, n_sel, s_len), F32)],
        compiler_params=_params("parallel", "parallel", "parallel"),
        name="nsa_cmp_select",
    )(qvt, kc, vct, overlap_t)

    flash_scratch = lambda n: [pltpu.VMEM((rep, 1, n), F32), pltpu.VMEM((rep, 1, n), F32),
                               pltpu.VMEM((rep, dh, n), F32)]
    tqs = tq
    pairs = [(i, j) for i in range(s_len // tqs) for j in range((i * tqs + tqs - 1) // kb + 1)]
    qi_of = jnp.asarray([pr[0] for pr in pairs], jnp.int32)
    kj_of = jnp.asarray([pr[1] for pr in pairs], jnp.int32)
    o_s = pl.pallas_call(
        functools.partial(_nsa_select_t_kernel, tq=tqs, kb=kb, rep=rep, dh=dh),
        grid_spec=pltpu.PrefetchScalarGridSpec(
            num_scalar_prefetch=2,
            grid=(bsz, groups, len(pairs)),
            in_specs=[pl.BlockSpec((1, rep * dh, tqs), lambda b_, g, pr, qi, kj: (b_, g, qi[pr])),
                      pl.BlockSpec((1, kb, dh), lambda b_, g, pr, qi, kj: (b_, kj[pr], g)),
                      pl.BlockSpec((1, dh, kb), lambda b_, g, pr, qi, kj: (b_, n_heads + g, kj[pr])),
                      pl.BlockSpec((1, 1, n_sel, tqs), lambda b_, g, pr, qi, kj: (b_, g, 0, qi[pr]))],
            out_specs=pl.BlockSpec((1, rep * dh, tqs), lambda b_, g, pr, qi, kj: (b_, g, qi[pr])),
            scratch_shapes=flash_scratch(tqs)),
        out_shape=jax.ShapeDtypeStruct((bsz, qw, s_len), F32),
        compiler_params=_params("parallel", "parallel", "arbitrary"),
        name="nsa_select_attn",
    )(qi_of, kj_of, qvt, k_rot, qvt, sel)

    n_steps = -(-NSA_WINDOW // kb) + tqs // kb
    win_blk = lambda i, j: jnp.maximum(i * (tqs // kb) - (n_steps - tqs // kb) + j, 0)
    qt_spec = pl.BlockSpec((1, rep * dh, tqs), lambda b_, g, i, j: (b_, g, i))
    o = pl.pallas_call(
        functools.partial(_nsa_window_t_kernel, tq=tqs, kb=kb, rep=rep, dh=dh, window=NSA_WINDOW, n_steps=n_steps),
        grid=(bsz, groups, s_len // tqs, n_steps),
        in_specs=[qt_spec,
                  pl.BlockSpec((1, kb, dh), lambda b_, g, i, j: (b_, win_blk(i, j), groups + g)),
                  pl.BlockSpec((1, dh, kb), lambda b_, g, i, j: (b_, n_heads + groups + g, win_blk(i, j))),
                  qt_spec, qt_spec,
                  pl.BlockSpec((1, 1, rep * 3, tqs), lambda b_, g, i, j: (b_, g, 0, i))],
        out_specs=pl.BlockSpec((1, tqs, rep * dh), lambda b_, g, i, j: (b_, i, g)),
        out_shape=jax.ShapeDtypeStruct((bsz, s_len, qw), BF16),
        scratch_shapes=flash_scratch(tqs),
        compiler_params=_params("parallel", "parallel", "parallel", "arbitrary"),
        name="nsa_window_attn",
    )(qvt, k_rot, qvt, o_c, o_s, gates)
    return matmul_ws(o.reshape(t, qw), [(w["nsa_w_out"], 0)], d, epilogue=_ep_residual, extras=[(h, "mn")],
                     name="nsa_out")


_MATMUL_WEIGHTS = ("pl_proj", "rw_w1", "rw_w2", "rw_a1", "rw_a2", "rw_g1", "rw_g2")


def kernel(x, p, norm_mix, norm_ffn, norm_pl, pl_proj, pl_gate, norm_final, mb_w_in, mb_conv_w, mb_conv_b, mb_dt_bias, mb_a_log, mb_d_skip, mb_norm_w, mb_w_out, nsa_w_in, nsa_cmp_pos_k, nsa_cmp_pos_v, nsa_cmp_k_w1, nsa_cmp_k_w2, nsa_cmp_v_w1, nsa_cmp_v_w2, nsa_w_out, hg_w_in, hg_lb_logits, hg_norm_w, hg_w_out, rw_mu, rw_w_rkv, rw_w0, rw_w1, rw_w2, rw_a0, rw_a1, rw_a2, rw_g1, rw_g2, rw_k_k, rw_k_a, rw_r_k, rw_ln_w, rw_ln_b, rw_w_out, ffn0_w_in, ffn0_w_out, moe1_router, moe1_w_in, moe1_w_out, ffn2_w_in, ffn2_w_out, moe3_router, moe3_w_in, moe3_w_out):
    w = dict(locals())
    for name in _MATMUL_WEIGHTS:
        w[name] = w[name].astype(BF16)
    bsz, s_len, d = x.shape
    depth = p.shape[0]
    t = bsz * s_len
    lb_all = jax.nn.softmax(hg_lb_logits.astype(F32), axis=0)
    lb_all = jnp.cumsum(lb_all, axis=0) - lb_all[0]
    dense = [(w["ffn0_w_in"], w["ffn0_w_out"]), (w["ffn2_w_in"], w["ffn2_w_out"])]
    moe = [(moe1_router, w["moe1_w_in"], w["moe1_w_out"]), (moe3_router, w["moe3_w_in"], w["moe3_w_out"])]
    p_bf = p.reshape(depth, t, p.shape[-1])
    h = x.reshape(t, d)
    for i in range(depth):
        kind = i % 4
        if kind == 0:
            h = mamba2_mixer(rmsnorm(h, norm_mix[i]), h, w, bsz, s_len)
        elif kind == 1:
            h = nsa_mixer_t(rmsnorm(h, norm_mix[i]), h, w, bsz, s_len)
        elif kind == 2:
            h = hgrn2_mixer(rmsnorm(h, norm_mix[i]), h, w, lb_all[i], bsz, s_len)
        else:
            h = rwkv7_mixer(rmsnorm(h, norm_mix[i], out_dtype=F32), h, w, bsz, s_len)
        if i % 2 == 0:
            h = dense_ffn(rmsnorm(h, norm_ffn[i]), h, *dense[i // 2])
        else:
            router, w_in, w_out = moe[i // 2]
            v, comb = rmsnorm_router(h, norm_ffn[i], router)
            h = moe_ffn_routed(v, h, comb, w_in, w_out)
        h = ple_gate(h, p_bf[i], norm_pl[i], w["pl_proj"][i], pl_gate, i)
    return rmsnorm(h, norm_final, out_dtype=F32).reshape(bsz, s_len, d)
```

```python
import functools
import math

import jax
import jax.numpy as jnp
from jax import lax
from jax.experimental import pallas as pl
from jax.experimental.pallas import tpu as pltpu

F32 = jnp.float32
BF16 = jnp.bfloat16

NORM_EPS = 1e-6
ROPE_THETA = 10000.0

V7X_VMEM_BYTES = 64 * 1024 * 1024
VMEM_LIMIT_BYTES = V7X_VMEM_BYTES - 8 * 1024 * 1024
LANES = 128

MB_D_STATE = 128
MB_CHUNK = 128

NSA_HEAD_DIM = 128
NSA_N_KV = 4
NSA_CMP_BLOCK = 32
NSA_CMP_STRIDE = 16
NSA_SEL_BLOCK = 64
NSA_TOPK = 16
NSA_WINDOW = 512
NSA_FORCED_SCORE = 1e9

HG_HEAD_DIM = 128
HG_CHUNK = 32

RW_HEAD_DIM = 64
RW_LN_EPS = 64e-5
RW_CHUNK = 128

MOE_TOPK = 2


def _params(*semantics):
    return pltpu.CompilerParams(dimension_semantics=semantics, vmem_limit_bytes=VMEM_LIMIT_BYTES)


def _pick(n, target):
    if n <= target:
        return n
    for c in range(target, 0, -1):
        if n % c == 0:
            return c
    return n


def _silu(x):
    return x * jax.nn.sigmoid(x)


def _rmsnorm_kernel(x_ref, g_ref, o_ref):
    x = x_ref[...]
    ms = jnp.mean(x * x, axis=-1, keepdims=True)
    o_ref[...] = (x * lax.rsqrt(ms + NORM_EPS) * g_ref[...]).astype(o_ref.dtype)


def rmsnorm(x, gain, out_dtype=BF16, name="rmsnorm"):
    m, d = x.shape
    bm = _pick(m, 256)
    return pl.pallas_call(
        _rmsnorm_kernel,
        grid=(m // bm,),
        in_specs=[pl.BlockSpec((bm, d), lambda i: (i, 0)), pl.BlockSpec((1, d), lambda i: (0, 0))],
        out_specs=pl.BlockSpec((bm, d), lambda i: (i, 0)),
        out_shape=jax.ShapeDtypeStruct((m, d), out_dtype),
        compiler_params=_params("parallel"),
        name=name,
    )(x, gain.reshape(1, d).astype(F32))


def _mm_kernel(*refs, n_w, n_extra, nk, epilogue):
    x_ref = refs[0]
    w_refs = refs[1:1 + n_w]
    e_refs = refs[1 + n_w:1 + n_w + n_extra]
    o_ref = refs[1 + n_w + n_extra]
    acc_refs = refs[2 + n_w + n_extra:]
    x = x_ref[...]
    if nk == 1:
        accs = [jnp.dot(x, w[...], preferred_element_type=F32) for w in w_refs]
        o_ref[...] = epilogue(accs, [e[...] for e in e_refs]).astype(o_ref.dtype)
        return
    k = pl.program_id(2)

    @pl.when(k == 0)
    def _():
        for a in acc_refs:
            a[...] = jnp.zeros_like(a)

    for a, w in zip(acc_refs, w_refs):
        a[...] += jnp.dot(x, w[...], preferred_element_type=F32)

    @pl.when(k == nk - 1)
    def _():
        o_ref[...] = epilogue([a[...] for a in acc_refs], [e[...] for e in e_refs]).astype(o_ref.dtype)


def _first(accs, extras):
    return accs[0]


def matmul(x, ws, n_out, *, epilogue=_first, extras=(), out_dtype=F32, bm=1024, bn=512, bk=None, name="matmul"):
    m, kdim = x.shape
    bm = _pick(m, bm)
    bn = _pick(n_out, bn)
    if bk is None:
        bk = kdim if kdim <= 4096 else _pick(kdim, 4096)
    nk = kdim // bk
    assert kdim % bk == 0 and m % bm == 0 and n_out % bn == 0
    in_specs = [pl.BlockSpec((bm, bk), lambda i, j, k: (i, k))]
    args = [x]
    for w, off in ws:
        assert off % bn == 0 and w.shape[0] == kdim
        in_specs.append(pl.BlockSpec((bk, bn), functools.partial(lambda i, j, k, o: (k, j + o), o=off // bn)))
        args.append(w)
    for arr, kind in extras:
        if kind == "mn":
            in_specs.append(pl.BlockSpec((bm, bn), lambda i, j, k: (i, j)))
        elif kind == "m":
            in_specs.append(pl.BlockSpec((bm, arr.shape[1]), lambda i, j, k: (i, 0)))
        elif kind == "kn":
            in_specs.append(pl.BlockSpec((arr.shape[0], bn), lambda i, j, k: (0, j)))
        else:
            in_specs.append(pl.BlockSpec((1, bn), lambda i, j, k: (0, j)))
        args.append(arr)
    scratch = [pltpu.VMEM((bm, bn), F32) for _ in ws] if nk > 1 else []
    kern = functools.partial(_mm_kernel, n_w=len(ws), n_extra=len(extras), nk=nk, epilogue=epilogue)
    return pl.pallas_call(
        kern,
        grid=(m // bm, n_out // bn, nk),
        in_specs=in_specs,
        out_specs=pl.BlockSpec((bm, bn), lambda i, j, k: (i, j)),
        out_shape=jax.ShapeDtypeStruct((m, n_out), out_dtype),
        scratch_shapes=scratch,
        compiler_params=_params("parallel", "parallel", "arbitrary"),
        name=name,
    )(*args)


WS_CAST_CHUNK = 512


def _mm_ws_kernel(*refs, n_w, n_extra, epilogue):
    x_ref = refs[0]
    w_refs = refs[1:1 + n_w]
    e_refs = refs[1 + n_w:1 + n_w + n_extra]
    o_ref = refs[1 + n_w + n_extra]
    wb_refs = refs[2 + n_w + n_extra:]

    kdim = x_ref.shape[1]
    ck = _pick(kdim, WS_CAST_CHUNK)

    @pl.when(pl.program_id(1) == 0)
    def _():
        accs = [None] * n_w
        for c in range(kdim // ck):
            rows = slice(c * ck, (c + 1) * ck)
            xc = x_ref[:, rows]
            for n, (w, wb) in enumerate(zip(w_refs, wb_refs)):
                wc = (w[0, rows, :] if len(w.shape) == 3 else w[rows, :]).astype(BF16)
                wb[rows, :] = wc
                part = jnp.dot(xc, wc, preferred_element_type=F32)
                accs[n] = part if accs[n] is None else accs[n] + part
        o_ref[...] = epilogue(accs, [e[...] for e in e_refs]).astype(o_ref.dtype)

    @pl.when(pl.program_id(1) != 0)
    def _():
        x = x_ref[...]
        accs = [jnp.dot(x, wb[...], preferred_element_type=F32) for wb in wb_refs]
        o_ref[...] = epilogue(accs, [e[...] for e in e_refs]).astype(o_ref.dtype)


def matmul_ws(x, ws, n_out, *, epilogue=_first, extras=(), out_dtype=F32, bm=1024, bn=512, w_buffers=2,
              name="matmul_ws"):
    m, kdim = x.shape
    bm = _pick(m, bm)
    bn = _pick(n_out, bn)
    assert m % bm == 0 and n_out % bn == 0
    mode = {} if w_buffers == 2 else {"pipeline_mode": pl.Buffered(w_buffers)}
    in_specs = [pl.BlockSpec((bm, kdim), lambda j, i: (i, 0))]
    args = [x]
    for w, off in ws:
        if w.ndim == 3:
            e, o = off
            assert o % bn == 0 and w.shape[1] == kdim
            in_specs.append(pl.BlockSpec((1, kdim, bn), functools.partial(lambda j, i, e_, o_: (e_, 0, j + o_),
                                                                          e_=e, o_=o // bn), **mode))
        else:
            assert off % bn == 0 and w.shape[0] == kdim
            in_specs.append(pl.BlockSpec((kdim, bn), functools.partial(lambda j, i, o_: (0, j + o_), o_=off // bn),
                                         **mode))
        args.append(w)
    for arr, kind in extras:
        if kind == "mn":
            in_specs.append(pl.BlockSpec((bm, bn), lambda j, i: (i, j)))
        elif kind == "m":
            in_specs.append(pl.BlockSpec((bm, arr.shape[1]), lambda j, i: (i, 0)))
        elif kind == "kn":
            in_specs.append(pl.BlockSpec((arr.shape[0], bn), lambda j, i: (0, j)))
        else:
            in_specs.append(pl.BlockSpec((1, bn), lambda j, i: (0, j)))
        args.append(arr)
    kern = functools.partial(_mm_ws_kernel, n_w=len(ws), n_extra=len(extras), epilogue=epilogue)
    return pl.pallas_call(
        kern,
        grid=(n_out // bn, m // bm),
        in_specs=in_specs,
        out_specs=pl.BlockSpec((bm, bn), lambda j, i: (i, j)),
        out_shape=jax.ShapeDtypeStruct((m, n_out), out_dtype),
        scratch_shapes=[pltpu.VMEM((kdim, bn), BF16) for _ in ws],
        compiler_params=_params("parallel", "arbitrary"),
        name=name,
    )(*args)


def _ep_residual(accs, extras):
    return extras[0] + accs[0]


def _ep_swiglu(accs, extras):
    return _silu(accs[0]) * accs[1]


def _ep_tanh(accs, extras):
    return jnp.tanh(accs[0])


def _ep_sigmoid(accs, extras):
    return jax.nn.sigmoid(accs[0])


def _ep_bias_sigmoid(accs, extras):
    return jax.nn.sigmoid(accs[0] + extras[0])


def _ep_rw_logdecay(accs, extras):
    w = -jax.nn.softplus(-(accs[0] + extras[0])) - 0.5
    return -jnp.exp(w)


def _ep_ple_gate(accs, extras):
    pp = jnp.dot(extras[1].astype(BF16), extras[2], preferred_element_type=F32)
    return extras[0] + pp * jax.nn.sigmoid(accs[0])


HALO = 8


def _conv_silu_chunk(x_ref, tail_ref, w_ref, b_ref):
    x = x_ref[0]
    tail = tail_ref[...]
    k_width = w_ref.shape[0]
    row = lax.broadcasted_iota(jnp.int32, tail.shape, 0)
    y = b_ref[...] + w_ref[k_width - 1:k_width, :] * x
    for j in range(k_width - 1):
        shift = k_width - 1 - j
        xr = pltpu.roll(x, shift, 0)
        top = jnp.where(row < shift, pltpu.roll(tail, shift, 0), xr[0:HALO])
        y = y + w_ref[j:j + 1, :] * jnp.concatenate([top, xr[HALO:]], axis=0)
    tail_ref[...] = x[x.shape[0] - HALO:, :]
    return _silu(y)


def _cumsum_rows(x, n):
    row = lax.broadcasted_iota(jnp.int32, x.shape, 0)
    s = 1
    while s < n:
        x = x + jnp.where(row >= s, pltpu.roll(x, s, 0), 0.0)
        s *= 2
    return x


def _cumsum_lanes(x, n):
    col = lax.broadcasted_iota(jnp.int32, x.shape, 1)
    s = 1
    while s < n:
        x = x + jnp.where(col >= s, pltpu.roll(x, s, 1), 0.0)
        s *= 2
    return x


def _dot_nt(a, b):
    return lax.dot_general(a, b, (((1,), (1,)), ((), ())), preferred_element_type=F32)


def _dot_tn(a, b):
    return lax.dot_general(a, b, (((0,), (0,)), ((), ())), preferred_element_type=F32)


def _ssd_kernel(xs_ref, b_ref, c_ref, z_ref, dt_ref, dtt_ref, bias_r_ref, bias_c_ref, alog_r_ref, alog_c_ref,
                dskip_ref, normw_ref, wx_ref, wb_ref, wc_ref, bx_ref, bb_ref, bc_ref, o_ref,
                state_ref, y_ref, tx_ref, tb_ref, tc_ref, *, chunk, heads, p_dim):
    @pl.when(pl.program_id(2) == 0)
    def _():
        state_ref[...] = jnp.zeros_like(state_ref)
        tx_ref[...] = jnp.zeros_like(tx_ref)
        tb_ref[...] = jnp.zeros_like(tb_ref)
        tc_ref[...] = jnp.zeros_like(tc_ref)

    dt = jax.nn.softplus(dt_ref[0, 0] + bias_r_ref[0])
    dtt = jax.nn.softplus(dtt_ref[0, 0] + bias_c_ref[0])
    a_cum = _cumsum_rows(dt * -jnp.exp(alog_r_ref[0]), chunk)
    a_cum_t = _cumsum_lanes(dtt * -jnp.exp(alog_c_ref[0]), chunk)
    xs = _conv_silu_chunk(xs_ref, tx_ref, wx_ref, bx_ref)
    bmat = _conv_silu_chunk(b_ref, tb_ref, wb_ref, bb_ref)
    cmat = _conv_silu_chunk(c_ref, tc_ref, wc_ref, bc_ref).astype(BF16)
    cb = _dot_nt(cmat, bmat.astype(BF16))
    b_t = bmat.T.astype(BF16)
    li = lax.broadcasted_iota(jnp.int32, (chunk, chunk), 0)
    si = lax.broadcasted_iota(jnp.int32, (chunk, chunk), 1)
    causal = li >= si
    per = LANES // p_dim
    lane_seg = lax.broadcasted_iota(jnp.int32, (1, LANES), 1) // p_dim

    def pick(vals):
        out = vals[-1]
        for i in range(per - 2, -1, -1):
            out = jnp.where(lane_seg == i, vals[i], out)
        return out

    dot = functools.partial(jnp.dot, preferred_element_type=F32)
    es = range(heads)
    tiles = range(heads // per)
    head_row = lax.broadcasted_iota(jnp.int32, (heads, heads * LANES), 0)
    to_tile = jnp.where(lax.broadcasted_iota(jnp.int32, (heads, heads * LANES), 1) // LANES == head_row, 1.0, 0.0)
    cum_t = _dot_hi(a_cum, to_tile)
    of = lambda vals, i: [vals[i * per + j] for j in range(per)]
    tile = lambda x, i: x[:, i * LANES:(i + 1) * LANES]
    cum_c = jnp.concatenate([pick([tile(cum_t, e) for e in of(es, i)]) for i in tiles], axis=-1)
    dt_c = jnp.concatenate([pick([dt[:, e:e + 1] for e in of(es, i)]) for i in tiles], axis=-1)
    last_c = cum_c[chunk - 1:chunk, :]
    m = [(cb * jnp.exp(jnp.where(causal, tile(cum_t, e) - a_cum_t[e:e + 1, :], -jnp.inf))).astype(BF16) for e in es]
    xdt = xs * dt_c
    xdt_b = xdt.astype(BF16)
    xend_b = (xdt * jnp.exp(last_c - cum_c)).astype(BF16)
    grow = jnp.exp(cum_c)
    st_decay = jnp.exp(last_c)
    st = [state_ref[i] for i in tiles]
    y_in = [pick([dot(m[e], tile(xdt_b, i)) for e in of(es, i)]) for i in tiles]
    y_st = [dot(cmat, st[i].astype(BF16)) * tile(grow, i) for i in tiles]
    for i in tiles:
        state_ref[i] = st[i] * tile(st_decay, i) + dot(b_t, tile(xend_b, i))
        y_ref[:, i * LANES:(i + 1) * LANES] = y_in[i] + y_st[i]
    y = y_ref[...] + xs * dskip_ref[...]
    y = y * _silu(z_ref[0])
    ms = jnp.mean(y * y, axis=-1, keepdims=True)
    o_ref[0] = (y * lax.rsqrt(ms + NORM_EPS) * normw_ref[...]).astype(o_ref.dtype)


def ssd_scan(xbc, z, dt, conv_w, conv_b, dt_bias, a_log, d_skip, norm_w, *, chunk=MB_CHUNK):
    bsz, s_len, d_inner = z.shape
    k_width = conv_w.shape[0]
    assert k_width - 1 <= HALO <= min(chunk, s_len)
    conv_b2 = conv_b.reshape(1, -1)
    n_heads = dt.shape[-1]
    n_state = MB_D_STATE
    groups = (xbc.shape[-1] - d_inner) // (2 * n_state)
    heads = n_heads // groups
    p_dim = d_inner // n_heads
    gw = heads * p_dim
    assert gw % LANES == 0 and d_inner % n_state == 0
    chunk = min(chunk, s_len)
    nc = s_len // chunk
    b_off = d_inner // n_state
    c_off = b_off + groups
    dt_g = jnp.transpose(dt.reshape(bsz, s_len, groups, heads), (0, 2, 1, 3))
    dt_gt = jnp.transpose(dt_g, (0, 1, 3, 2))
    kern = functools.partial(_ssd_kernel, chunk=chunk, heads=heads, p_dim=p_dim)
    per_group = lambda b_, g, c: (g, 0, 0)
    return pl.pallas_call(
        kern,
        grid=(bsz, groups, nc),
        in_specs=[pl.BlockSpec((1, chunk, gw), lambda b_, g, c: (b_, c, g)),
                  pl.BlockSpec((1, chunk, n_state), lambda b_, g, c: (b_, c, b_off + g)),
                  pl.BlockSpec((1, chunk, n_state), lambda b_, g, c: (b_, c, c_off + g)),
                  pl.BlockSpec((1, chunk, gw), lambda b_, g, c: (b_, c, g)),
                  pl.BlockSpec((1, 1, chunk, heads), lambda b_, g, c: (b_, g, c, 0)),
                  pl.BlockSpec((1, 1, heads, chunk), lambda b_, g, c: (b_, g, 0, c)),
                  pl.BlockSpec((1, 1, heads), per_group),
                  pl.BlockSpec((1, heads, 1), per_group),
                  pl.BlockSpec((1, 1, heads), per_group),
                  pl.BlockSpec((1, heads, 1), per_group),
                  pl.BlockSpec((1, gw), lambda b_, g, c: (0, g)),
                  pl.BlockSpec((1, gw), lambda b_, g, c: (0, g)),
                  pl.BlockSpec((k_width, gw), lambda b_, g, c: (0, g)),
                  pl.BlockSpec((k_width, n_state), lambda b_, g, c: (0, b_off + g)),
                  pl.BlockSpec((k_width, n_state), lambda b_, g, c: (0, c_off + g)),
                  pl.BlockSpec((1, gw), lambda b_, g, c: (0, g)),
                  pl.BlockSpec((1, n_state), lambda b_, g, c: (0, b_off + g)),
                  pl.BlockSpec((1, n_state), lambda b_, g, c: (0, c_off + g))],
        out_specs=pl.BlockSpec((1, chunk, gw), lambda b_, g, c: (b_, c, g)),
        out_shape=jax.ShapeDtypeStruct(z.shape, BF16),
        scratch_shapes=[pltpu.VMEM((gw // LANES, n_state, LANES), F32), pltpu.VMEM((chunk, gw), F32),
                        pltpu.VMEM((HALO, gw), F32), pltpu.VMEM((HALO, n_state), F32),
                        pltpu.VMEM((HALO, n_state), F32)],
        compiler_params=_params("parallel", "parallel", "arbitrary"),
        name="mamba_ssd",
    )(xbc, xbc, xbc, z, dt_g, dt_gt,
      dt_bias.reshape(groups, 1, heads), dt_bias.reshape(groups, heads, 1),
      a_log.reshape(groups, 1, heads), a_log.reshape(groups, heads, 1),
      jnp.repeat(d_skip, p_dim).reshape(1, d_inner), norm_w.reshape(1, d_inner),
      conv_w, conv_w, conv_w, conv_b2, conv_b2, conv_b2)


def mamba2_mixer(u, h, w, bsz, s_len):
    d_inner = w["mb_w_out"].shape[0]
    n_heads = w["mb_dt_bias"].shape[0]
    w_in = w["mb_w_in"]
    xbc_w = w_in.shape[1] - d_inner - n_heads
    z = matmul_ws(u, [(w_in, 0)], d_inner, name="mb_in_z")
    xbc = matmul_ws(u, [(w_in, d_inner)], xbc_w, name="mb_in_xbc")
    dt = matmul_ws(u, [(w_in, d_inner + xbc_w)], n_heads, name="mb_in_dt")
    y = ssd_scan(xbc.reshape(bsz, s_len, xbc_w), z.reshape(bsz, s_len, d_inner), dt.reshape(bsz, s_len, n_heads),
                 w["mb_conv_w"], w["mb_conv_b"], w["mb_dt_bias"], w["mb_a_log"], w["mb_d_skip"], w["mb_norm_w"])
    return matmul_ws(y.reshape(bsz * s_len, d_inner), [(w["mb_w_out"], 0)], h.shape[1],
                     epilogue=_ep_residual, extras=[(h, "mn")], bm=512, w_buffers=1, name="mb_out")


def _seg_cumsum_rows(x, seg, reverse=False):
    n = x.shape[0]
    pos = lax.broadcasted_iota(jnp.int32, x.shape, 0) % seg
    s = 1
    while s < seg:
        if reverse:
            x = x + jnp.where(pos < seg - s, pltpu.roll(x, n - s, 0), 0.0)
        else:
            x = x + jnp.where(pos >= s, pltpu.roll(x, s, 0), 0.0)
        s *= 2
    return x


def _hgrn_kernel(q_ref, f_ref, i_ref, g_ref, lb_ref, nw_ref, o_ref, state_ref, *, sub, n_sub, heads, dk):
    @pl.when(pl.program_id(2) == 0)
    def _():
        state_ref[...] = jnp.zeros_like(state_ref)

    lb = lb_ref[...]
    nw = nw_ref[...]
    ti = lax.broadcasted_iota(jnp.int32, (sub, sub), 0)
    si = lax.broadcasted_iota(jnp.int32, (sub, sub), 1)
    causal = ti >= si
    f = lb + (1.0 - lb) * jax.nn.sigmoid(f_ref[0])
    lf = jnp.log(f)
    k = 1.0 - f
    b = _seg_cumsum_rows(lf, sub)
    to_end = _seg_cumsum_rows(lf, sub, reverse=True) - lf
    q_dec = (_silu(q_ref[0]) * jnp.exp(b)).astype(BF16)
    k_dec = (k * jnp.exp(-b)).astype(BF16)
    k_end = (k * jnp.exp(to_end)).astype(BF16)
    v = i_ref[0].astype(BF16)
    cs = range(n_sub)
    hs = range(heads)
    blk = lambda x, c, h: x[c * sub:(c + 1) * sub, h * dk:(h + 1) * dk]
    scores = [[jnp.where(causal, _dot_nt(blk(q_dec, c, h), blk(k_dec, c, h)), 0.0).astype(BF16) for h in hs]
              for c in cs]
    upd = [[_dot_tn(blk(v, c, h), blk(k_end, c, h)) for h in hs] for c in cs]
    states = []
    st = [state_ref[h] for h in hs]
    for c in cs:
        states.append(st)
        decay = jnp.exp(b[(c + 1) * sub - 1:(c + 1) * sub, :])
        st = [st[h] * decay[:, h * dk:(h + 1) * dk] + upd[c][h] for h in hs]
    for h in hs:
        state_ref[h] = st[h]
    for c in cs:
        rows = slice(c * sub, (c + 1) * sub)
        for h in hs:
            o = (jnp.dot(scores[c][h], blk(v, c, h), preferred_element_type=F32)
                 + _dot_nt(blk(q_dec, c, h), states[c][h].astype(BF16)))
            o = o * lax.rsqrt(jnp.mean(o * o, axis=-1, keepdims=True) + NORM_EPS) * nw
            cols = slice(h * dk, (h + 1) * dk)
            o_ref[0, rows, cols] = (o * _silu(g_ref[0, rows, cols])).astype(o_ref.dtype)


def hgrn2_scan(proj, lower_bound, norm_w, *, dk=HG_HEAD_DIM, sub=HG_CHUNK, tb=256, heads=8):
    bsz, s_len, d4 = proj.shape
    d = d4 // 4
    n_heads = d // dk
    tb = min(tb, s_len)
    heads = min(heads, n_heads)
    hw = heads * dk
    n_hb = n_heads // heads
    kern = functools.partial(_hgrn_kernel, sub=sub, n_sub=tb // sub, heads=heads, dk=dk)
    spec = lambda part: pl.BlockSpec((1, tb, hw), lambda b_, h_, t: (b_, t, part * n_hb + h_))
    return pl.pallas_call(
        kern,
        grid=(bsz, n_hb, s_len // tb),
        in_specs=[spec(0), spec(1), spec(2), spec(3),
                  pl.BlockSpec((1, hw), lambda b_, h_, t: (0, h_)),
                  pl.BlockSpec((1, dk), lambda b_, h_, t: (0, 0))],
        out_specs=pl.BlockSpec((1, tb, hw), lambda b_, h_, t: (b_, t, h_)),
        out_shape=jax.ShapeDtypeStruct((bsz, s_len, d), BF16),
        scratch_shapes=[pltpu.VMEM((heads, dk, dk), F32)],
        compiler_params=_params("parallel", "parallel", "arbitrary"),
        name="hgrn2_scan",
    )(proj, proj, proj, proj, lower_bound.reshape(1, d), norm_w.reshape(1, dk))


def hgrn2_mixer(u, h, w, lower_bound, bsz, s_len):
    d = h.shape[1]
    proj = matmul_ws(u, [(w["hg_w_in"], 0)], 4 * d, bm=512, bn=1024, name="hg_in")
    o = hgrn2_scan(proj.reshape(bsz, s_len, 4 * d), lower_bound, w["hg_norm_w"])
    return matmul_ws(o.reshape(bsz * s_len, d), [(w["hg_w_out"], 0)], d,
                     epilogue=_ep_residual, extras=[(h, "mn")], name="hg_out")


def dense_ffn(v, h, w_in, w_out):
    f = w_out.shape[0]
    hid = matmul_ws(v, [(w_in, 0), (w_in, f)], f, epilogue=_ep_swiglu, out_dtype=BF16, bm=1024, bn=256,
                    name="ffn_in")
    return matmul_ws(hid, [(w_out, 0)], h.shape[1], epilogue=_ep_residual, extras=[(h, "mn")], bm=512, w_buffers=1,
                     name="ffn_out")


def _rmsnorm_router_kernel(x_ref, g_ref, r_ref, o_ref, c_ref, *, n_experts):
    x = x_ref[...]
    ms = jnp.mean(x * x, axis=-1, keepdims=True)
    v = (x * lax.rsqrt(ms + NORM_EPS) * g_ref[...]).astype(BF16)
    o_ref[...] = v
    logits = jnp.dot(v, r_ref[...], preferred_element_type=F32)
    lane = lax.broadcasted_iota(jnp.int32, logits.shape, 1)
    logits = jnp.where(lane < n_experts, logits, -jnp.inf)
    m1 = jnp.max(logits, axis=-1, keepdims=True)
    i1 = jnp.min(jnp.where(logits == m1, lane, LANES), axis=-1, keepdims=True)
    rest = jnp.where(lane == i1, -jnp.inf, logits)
    m2 = jnp.max(rest, axis=-1, keepdims=True)
    i2 = jnp.min(jnp.where(rest == m2, lane, LANES), axis=-1, keepdims=True)
    e2 = jnp.exp(m2 - m1)
    w1 = 1.0 / (1.0 + e2)
    c_ref[...] = jnp.where(lane == i1, w1, 0.0) + jnp.where(lane == i2, e2 * w1, 0.0)


def rmsnorm_router(x, gain, router):
    m, d = x.shape
    n_experts = router.shape[1]
    r_pad = jnp.zeros((d, LANES), BF16).at[:, :n_experts].set(router.astype(BF16))
    bm = _pick(m, 256)
    return pl.pallas_call(
        functools.partial(_rmsnorm_router_kernel, n_experts=n_experts),
        grid=(m // bm,),
        in_specs=[pl.BlockSpec((bm, d), lambda i: (i, 0)), pl.BlockSpec((1, d), lambda i: (0, 0)),
                  pl.BlockSpec((d, LANES), lambda i: (0, 0))],
        out_specs=[pl.BlockSpec((bm, d), lambda i: (i, 0)), pl.BlockSpec((bm, LANES), lambda i: (i, 0))],
        out_shape=[jax.ShapeDtypeStruct((m, d), BF16), jax.ShapeDtypeStruct((m, LANES), F32)],
        compiler_params=_params("parallel"),
        name="rmsnorm_router",
    )(x, gain.reshape(1, d).astype(F32), r_pad)


MOE_BLOCK = 1024
MOE_UNIT = 32
MOE_TILE = 512


def _moe_gather_kernel(x_ref, tok_ref, o_ref):
    tok = tok_ref[0]
    lane = lax.broadcasted_iota(jnp.int32, (tok.shape[0], x_ref.shape[0]), 1)
    onehot = jnp.where(tok == lane, 1.0, 0.0).astype(BF16)
    o_ref[...] = jnp.dot(onehot, x_ref[...], preferred_element_type=F32).astype(o_ref.dtype)


def _moe_expert_in_kernel(src_ref, exp_ref, first_ref, used_ref, *refs, per):
    x_refs = refs[:per]
    wg_ref, wu_ref, rw_ref, o_ref, wgb_ref, wub_ref, x_scr = refs[per:]
    t = pl.program_id(1)

    @pl.when(t < used_ref[0])
    def _():
        @pl.when(first_ref[t] == 1)
        def _():
            wgb_ref[...] = wg_ref[0].astype(BF16)
            wub_ref[...] = wu_ref[0].astype(BF16)

        unit = x_refs[0].shape[0]
        for i in range(per):
            x_scr[i * unit:(i + 1) * unit, :] = x_refs[i][...]
        x = x_scr[...]
        g = jnp.dot(x, wgb_ref[...], preferred_element_type=F32)
        u = jnp.dot(x, wub_ref[...], preferred_element_type=F32)
        o_ref[...] = (_silu(g) * u * rw_ref[...]).astype(o_ref.dtype)

    @pl.when(t >= used_ref[0])
    def _():
        o_ref[...] = jnp.zeros_like(o_ref)


def _moe_expert_out_kernel(exp_ref, first_ref, used_ref, hid_ref, w_ref, o_ref, wb_ref):
    t = pl.program_id(1)

    @pl.when(t < used_ref[0])
    def _():
        @pl.when(first_ref[t] == 1)
        def _():
            wb_ref[...] = w_ref[0].astype(BF16)

        o_ref[...] = jnp.dot(hid_ref[...], wb_ref[...], preferred_element_type=F32).astype(o_ref.dtype)

    @pl.when(pl.program_id(1) >= used_ref[0])
    def _():
        o_ref[...] = jnp.zeros_like(o_ref)


def _moe_scatter_kernel(dst_ref, h_ref, tok_ref, *refs, per):
    y_refs = refs[:per]
    o_ref, y_scr = refs[per:]

    @pl.when(pl.program_id(2) == 0)
    def _():
        o_ref[...] = h_ref[...]

    tok = tok_ref[0]
    row = lax.broadcasted_iota(jnp.int32, (o_ref.shape[0], tok.shape[1]), 0)
    onehot_t = jnp.where(tok == row, 1.0, 0.0).astype(BF16)
    unit = y_refs[0].shape[0]
    for i in range(per):
        y_scr[i * unit:(i + 1) * unit, :] = y_refs[i][...]
    o_ref[...] += jnp.dot(onehot_t, y_scr[...], preferred_element_type=F32)


def moe_ffn_routed(v, h, comb, w_in, w_out, *, tb=MOE_BLOCK, unit=MOE_UNIT, tile=MOE_TILE, bn=512, bo=1024):
    m, d = v.shape
    n_experts, _, two_de = w_in.shape
    de = two_de // 2
    tb = min(tb, m)
    nb = m // tb
    per = tile // unit
    bn = _pick(de, bn)
    bo = _pick(d, bo)
    n_assign = MOE_TOPK * tb
    n_slots = -(-(n_assign // unit + n_experts + 1) // per) * per
    groups = n_slots // per
    n_units = nb * (n_assign // unit + n_experts) + n_experts * (per - 1)
    n_tiles = -(-n_units // per)
    n_units = n_tiles * per

    wts, ids = lax.top_k(comb[:, :n_experts], MOE_TOPK)
    ea = ids.reshape(nb, n_assign)
    wa = wts.reshape(nb, n_assign)
    ta = jnp.broadcast_to(jnp.repeat(jnp.arange(tb, dtype=jnp.int32), MOE_TOPK)[None], (nb, n_assign))
    se, st, sw = lax.sort((ea, ta, wa), dimension=1, num_keys=1, is_stable=True)
    counts = jnp.sum(jax.nn.one_hot(ea, n_experts, dtype=jnp.int32), axis=1)
    units = -(-counts // unit)
    excl = lambda x, axis: jnp.cumsum(x, axis=axis) - x
    slot_start = excl(units, 1)
    row_start = excl(counts, 1)
    is_e = se[..., None] == jnp.arange(n_experts, dtype=jnp.int32)
    lookup = lambda table: jnp.sum(jnp.where(is_e, table[:, None, :], 0), axis=-1)
    pos = lookup(slot_start) * unit + jnp.arange(n_assign, dtype=jnp.int32)[None] - lookup(row_start)
    bidx = jnp.arange(nb, dtype=jnp.int32)[:, None]
    n_rows = n_slots * unit
    rr = jnp.arange(2 * n_rows, dtype=jnp.int32)[None, :, None]
    hit = pos[:, None, :] == rr % n_rows
    val = jnp.where(rr < n_rows, (st + 1).astype(F32)[:, None, :], sw[:, None, :])
    tab = jnp.sum(jnp.where(hit, val, 0.0), axis=-1)
    row_token = tab[:, :n_rows].astype(jnp.int32) - 1
    row_weight = tab[:, n_rows:]
    per_expert = jnp.sum(units, axis=0)
    per_expert_pad = -(-per_expert // per) * per
    e_off = excl(per_expert_pad, 0)
    before = excl(units, 0)
    slot = jnp.arange(n_slots, dtype=jnp.int32)
    slot_end = jnp.cumsum(units, axis=1)
    e_of_slot = jnp.sum(slot[None, :, None] >= slot_end[:, None, :], axis=-1)
    used_slot = e_of_slot < n_experts
    e_clip = jnp.minimum(e_of_slot, n_experts - 1)
    dst_unit = (e_off[e_clip] + jnp.take_along_axis(before, e_clip, axis=1)
                + slot[None] - jnp.take_along_axis(slot_start, e_clip, axis=1))
    dst_unit = jnp.where(used_slot, dst_unit, 0).astype(jnp.int32)
    flat_slot = (bidx * n_slots + slot[None]).astype(jnp.int32)
    zero_slot = n_slots - 1
    src_unit = jnp.full((n_units,), zero_slot, jnp.int32).at[
        jnp.where(used_slot, dst_unit, n_units).reshape(-1)].set(flat_slot.reshape(-1), mode="drop")
    tile_end = jnp.cumsum(per_expert_pad) // per
    tile_ids = jnp.arange(n_tiles, dtype=jnp.int32)
    tile_expert = jnp.minimum(jnp.sum(tile_ids[:, None] >= tile_end[None, :], axis=-1), n_experts - 1).astype(jnp.int32)
    tiles_used = tile_end[-1:].astype(jnp.int32)
    first = jnp.concatenate([jnp.ones((1,), jnp.int32),
                             (tile_expert[1:] != tile_expert[:-1]).astype(jnp.int32)])
    rw_em = row_weight.reshape(nb * n_slots, unit)[src_unit].reshape(n_units * unit, 1)

    xs = pl.pallas_call(
        _moe_gather_kernel,
        grid=(nb, groups),
        in_specs=[pl.BlockSpec((tb, d), lambda b_, g: (b_, 0)),
                  pl.BlockSpec((1, tile, 1), lambda b_, g: (b_, g, 0))],
        out_specs=pl.BlockSpec((tile, d), lambda b_, g: (b_ * groups + g, 0)),
        out_shape=jax.ShapeDtypeStruct((nb * n_slots * unit, d), BF16),
        compiler_params=_params("parallel", "arbitrary"),
        name="moe_gather",
    )(v, row_token.reshape(nb, n_slots * unit, 1))

    bpe = de // bn
    unit_spec = lambda i: pl.BlockSpec((unit, d), lambda j, t, src, ex, fi, us: (src[per * t + i], 0))
    hid = pl.pallas_call(
        functools.partial(_moe_expert_in_kernel, per=per),
        grid_spec=pltpu.PrefetchScalarGridSpec(
            num_scalar_prefetch=4,
            grid=(bpe, n_tiles),
            in_specs=[unit_spec(i) for i in range(per)] + [
                pl.BlockSpec((1, d, bn), lambda j, t, src, ex, fi, us: (ex[t], 0, j),
                             pipeline_mode=pl.Buffered(1)),
                pl.BlockSpec((1, d, bn), lambda j, t, src, ex, fi, us: (ex[t], 0, j + bpe),
                             pipeline_mode=pl.Buffered(1)),
                pl.BlockSpec((tile, 1), lambda j, t, src, ex, fi, us: (t, 0))],
            out_specs=pl.BlockSpec((tile, bn), lambda j, t, src, ex, fi, us: (t, j)),
            scratch_shapes=[pltpu.VMEM((d, bn), BF16), pltpu.VMEM((d, bn), BF16), pltpu.VMEM((tile, d), BF16)]),
        out_shape=jax.ShapeDtypeStruct((n_tiles * tile, de), BF16),
        compiler_params=_params("arbitrary", "arbitrary"),
        name="moe_expert_in",
    )(src_unit, tile_expert, first, tiles_used, *([xs] * per), w_in, w_in, rw_em)

    bo2 = _pick(d, 2 * bo)
    ys = pl.pallas_call(
        _moe_expert_out_kernel,
        grid_spec=pltpu.PrefetchScalarGridSpec(
            num_scalar_prefetch=3,
            grid=(d // bo2, n_tiles),
            in_specs=[pl.BlockSpec((tile, de), lambda n, t, ex, fi, us: (t, 0)),
                      pl.BlockSpec((1, de, bo2), lambda n, t, ex, fi, us: (ex[t], 0, n))],
            out_specs=pl.BlockSpec((tile, bo2), lambda n, t, ex, fi, us: (t, n)),
            scratch_shapes=[pltpu.VMEM((de, bo2), BF16)]),
        out_shape=jax.ShapeDtypeStruct((n_tiles * tile, d), BF16),
        compiler_params=_params("arbitrary", "arbitrary"),
        name="moe_expert_out",
    )(tile_expert, first, tiles_used, hid, w_out)

    y_spec = lambda i: pl.BlockSpec((unit, bo2), lambda b_, n, g, dst: (dst[(b_ * groups + g) * per + i], n))
    return pl.pallas_call(
        functools.partial(_moe_scatter_kernel, per=per),
        grid_spec=pltpu.PrefetchScalarGridSpec(
            num_scalar_prefetch=1,
            grid=(nb, d // bo2, groups),
            in_specs=[pl.BlockSpec((tb, bo2), lambda b_, n, g, dst: (b_, n)),
                      pl.BlockSpec((1, 1, tile), lambda b_, n, g, dst: (b_ * groups + g, 0, 0))]
                     + [y_spec(i) for i in range(per)],
            out_specs=pl.BlockSpec((tb, bo2), lambda b_, n, g, dst: (b_, n)),
            scratch_shapes=[pltpu.VMEM((tile, bo2), BF16)]),
        out_shape=jax.ShapeDtypeStruct((m, d), F32),
        compiler_params=_params("parallel", "parallel", "arbitrary"),
        name="moe_scatter",
    )(dst_unit.reshape(-1), h, row_token.reshape(nb * groups, 1, tile), *([ys] * per))


def ple_gate(h, p_i, norm_pl, pl_proj, pl_gate, layer):
    d = h.shape[1]
    n = rmsnorm(h, norm_pl, name="rmsnorm_ple")
    return matmul_ws(n, [(pl_gate, (layer, 0))], d, epilogue=_ep_ple_gate,
                     extras=[(h, "mn"), (p_i, "m"), (pl_proj, "kn")], name="ple_gate")


def _rw_mix_kernel(u_ref, mu_ref, *o_refs):
    u = u_ref[0]
    row = lax.broadcasted_iota(jnp.int32, u.shape, 0)
    dx = jnp.where(row >= 1, pltpu.roll(u, 1, 0), 0.0) - u
    for j, o_ref in enumerate(o_refs):
        o_ref[0] = (u + dx * mu_ref[j:j + 1, :]).astype(o_ref.dtype)


def rw_token_mix(u, mu):
    bsz, s_len, d = u.shape
    cb = _pick(d, LANES)
    n_mix = mu.shape[0]
    spec = pl.BlockSpec((1, s_len, cb), lambda b_, j: (b_, 0, j))
    return pl.pallas_call(
        _rw_mix_kernel,
        grid=(bsz, d // cb),
        in_specs=[spec, pl.BlockSpec((n_mix, cb), lambda b_, j: (0, j))],
        out_specs=[spec] * n_mix,
        out_shape=[jax.ShapeDtypeStruct(u.shape, BF16)] * n_mix,
        compiler_params=_params("parallel", "parallel"),
        name="rwkv_token_mix",
    )(u, mu)


def _dot_hi(a, b):
    return jnp.dot(a, b, preferred_element_type=F32, precision=lax.Precision.HIGHEST)


def _rw_scan_tile_kernel(r_ref, k_ref, v_ref, a_ref, lw_ref, g_ref, kk_ref, ka_ref, rk_ref, lnw_ref, lnb_ref,
                         o_ref, state_ref, *, chunk, heads, n):
    @pl.when(pl.program_id(2) == 0)
    def _():
        state_ref[...] = jnp.zeros_like(state_ref)

    per = LANES // n
    tiles = range(heads // per)
    sub = range(per)
    ti = lax.broadcasted_iota(jnp.int32, (chunk, chunk), 0)
    si = lax.broadcasted_iota(jnp.int32, (chunk, chunk), 1)
    strict = ti > si
    incl = ti >= si
    lane_seg = lax.broadcasted_iota(jnp.int32, (1, LANES), 1) // n
    seg_is = [lane_seg == j for j in sub]
    same_head = (lax.broadcasted_iota(jnp.int32, (LANES, LANES), 0) // n
                 == lax.broadcasted_iota(jnp.int32, (LANES, LANES), 1) // n)
    dot = functools.partial(jnp.dot, preferred_element_type=F32)
    tile = lambda x, i: x[:, i * LANES:(i + 1) * LANES]

    def pick(vals):
        out = vals[-1]
        for j in range(per - 2, -1, -1):
            out = jnp.where(seg_is[j], vals[j], out)
        return out

    def seg_sum(x):
        return pick([jnp.sum(jnp.where(seg_is[j], x, 0.0), axis=-1, keepdims=True) for j in sub])

    r, k, v, a, lw = r_ref[0], k_ref[0], v_ref[0], a_ref[0], lw_ref[0]
    kk = k * kk_ref[...]
    kmod = k * (1.0 + (a - 1.0) * ka_ref[...])
    cum = _cumsum_rows(lw, chunk)
    cum_end = cum[chunk - 1:chunk, :]
    mid = cum[chunk // 2 - 1:chunk // 2, :]
    bonus_in = r * kmod * rk_ref[...]
    kk_t, bonus_t = [], []
    for i in tiles:
        kki = tile(kk, i)
        kk_t.append(kki / jnp.maximum(jnp.sqrt(seg_sum(kki * kki)), 1e-12))
        bonus_t.append(seg_sum(tile(bonus_in, i)) * tile(v, i))
    kk = jnp.concatenate(kk_t, axis=-1) if len(kk_t) > 1 else kk_t[0]
    kka = kk * a
    e_neg = jnp.exp(mid - cum)
    to_end = jnp.exp(cum_end - cum)
    am = (kk * jnp.exp(cum - lw - mid)).astype(BF16)
    bm = (kka * e_neg).astype(BF16)
    km = (kmod * e_neg).astype(BF16)
    rm = (r * jnp.exp(cum - mid)).astype(BF16)
    a_abs = (kk * jnp.exp(cum - lw)).astype(BF16)
    r_abs = (r * jnp.exp(cum)).astype(BF16)
    k_end = (kmod * to_end).astype(BF16)
    b_end = (kka * to_end).astype(BF16)
    vb = v.astype(BF16)
    st_decay = jnp.exp(cum_end)
    zero = jnp.zeros((), BF16)

    st = [state_ref[i] for i in tiles]
    stb = [s.astype(BF16) for s in st]
    am_h = [[jnp.where(seg_is[j], tile(am, i), zero) for j in sub] for i in tiles]
    rm_h = [[jnp.where(seg_is[j], tile(rm, i), zero) for j in sub] for i in tiles]
    nb = [[(-jnp.where(strict, _dot_nt(am_h[i][j], tile(bm, i)), 0.0)).astype(BF16) for j in sub] for i in tiles]
    lk = [[jnp.where(strict, _dot_nt(am_h[i][j], tile(km, i)), 0.0).astype(BF16) for j in sub] for i in tiles]
    x = [_dot_nt(tile(a_abs, i), stb[i]) + pick([dot(lk[i][j], tile(vb, i)) for j in sub]) for i in tiles]
    xb = [xi.astype(BF16) for xi in x]
    x = [x[i] + pick([dot(nb[i][j], xb[i]) for j in sub]) for i in tiles]
    p = 2
    while p < chunk:
        nb = [[dot(nb[i][j], nb[i][j]).astype(BF16) for j in sub] for i in tiles]
        xb = [xi.astype(BF16) for xi in x]
        x = [x[i] + pick([dot(nb[i][j], xb[i]) for j in sub]) for i in tiles]
        p *= 2
    pb = [xi.astype(BF16) for xi in x]
    mk = [[jnp.where(incl, _dot_nt(rm_h[i][j], tile(km, i)), 0.0).astype(BF16) for j in sub] for i in tiles]
    mb = [[jnp.where(incl, _dot_nt(rm_h[i][j], tile(bm, i)), 0.0).astype(BF16) for j in sub] for i in tiles]
    y = [_dot_nt(tile(r_abs, i), stb[i])
         + pick([dot(mk[i][j], tile(vb, i)) - dot(mb[i][j], pb[i]) for j in sub]) for i in tiles]
    for i in tiles:
        upd = _dot_tn(tile(vb, i), tile(k_end, i)) - _dot_tn(pb[i], tile(b_end, i))
        state_ref[i] = st[i] * tile(st_decay, i) + jnp.where(same_head, upd, 0.0)
    inv_n = 1.0 / n
    for i in tiles:
        cols = slice(i * LANES, (i + 1) * LANES)
        mean = seg_sum(y[i]) * inv_n
        yc = y[i] - mean
        var = seg_sum(yc * yc) * inv_n
        yn = yc * lax.rsqrt(var + RW_LN_EPS) * lnw_ref[:, cols] + lnb_ref[:, cols]
        o_ref[0, :, cols] = ((yn + bonus_t[i]) * g_ref[0, :, cols]).astype(o_ref.dtype)


def rw_scan(r, k, v, a, lw, g, k_k, k_a, r_k, ln_w, ln_b, *, n=RW_HEAD_DIM, chunk=RW_CHUNK, heads=16):
    bsz, s_len, d = r.shape
    chunk = min(chunk, s_len)
    heads = min(heads, d // n)
    hw = heads * n
    seq = pl.BlockSpec((1, chunk, hw), lambda b_, h_, c: (b_, c, h_))
    par = pl.BlockSpec((1, hw), lambda b_, h_, c: (0, h_))
    row = lambda t: t.reshape(1, d)
    assert hw % LANES == 0 and LANES % n == 0
    kern = functools.partial(_rw_scan_tile_kernel, chunk=chunk, heads=heads, n=n)
    return pl.pallas_call(
        kern,
        grid=(bsz, d // hw, s_len // chunk),
        in_specs=[seq] * 6 + [par] * 5,
        out_specs=seq,
        out_shape=jax.ShapeDtypeStruct(r.shape, BF16),
        scratch_shapes=[pltpu.VMEM((hw // LANES, LANES, LANES), F32)],
        compiler_params=_params("parallel", "parallel", "arbitrary"),
        name="rwkv7_scan",
    )(r, k, v, a, lw, g, row(k_k), row(k_a), row(r_k), row(ln_w), row(ln_b))


def rwkv7_mixer(u, h, w, bsz, s_len):
    t, d = u.shape
    xr, xw, xk, xv, xa, xg = [x.reshape(t, d) for x in rw_token_mix(u.reshape(bsz, s_len, d), w["rw_mu"])]
    r = matmul_ws(xr, [(w["rw_w_rkv"], (0, 0))], d, name="rw_r")
    k = matmul_ws(xk, [(w["rw_w_rkv"], (1, 0))], d, name="rw_k")
    v = matmul_ws(xv, [(w["rw_w_rkv"], (2, 0))], d, name="rw_v")
    row = lambda x: x.reshape(1, d)
    w_lo = matmul(xw, [(w["rw_w1"], 0)], w["rw_w1"].shape[1], epilogue=_ep_tanh, out_dtype=BF16, name="rw_w1")
    wide = 2048
    lw = matmul(w_lo, [(w["rw_w2"], 0)], d, epilogue=_ep_rw_logdecay, extras=[(row(w["rw_w0"]), "n")], bn=wide,
                name="rw_w2")
    a_lo = matmul(xa, [(w["rw_a1"], 0)], w["rw_a1"].shape[1], out_dtype=BF16, name="rw_a1")
    a = matmul(a_lo, [(w["rw_a2"], 0)], d, epilogue=_ep_bias_sigmoid, extras=[(row(w["rw_a0"]), "n")], bn=wide,
               name="rw_a2")
    g_lo = matmul(xg, [(w["rw_g1"], 0)], w["rw_g1"].shape[1], epilogue=_ep_sigmoid, out_dtype=BF16, name="rw_g1")
    g = matmul(g_lo, [(w["rw_g2"], 0)], d, bn=wide, name="rw_g2")
    shp = (bsz, s_len, d)
    y = rw_scan(r.reshape(shp), k.reshape(shp), v.reshape(shp), a.reshape(shp), lw.reshape(shp), g.reshape(shp),
                w["rw_k_k"], w["rw_k_a"], w["rw_r_k"], w["rw_ln_w"], w["rw_ln_b"])
    return matmul_ws(y.reshape(t, d), [(w["rw_w_out"], 0)], d, epilogue=_ep_residual, extras=[(h, "mn")],
                     name="rw_out")


NEG_BIG = -1e30


def _rope_kernel(x_ref, cc_ref, ss_ref, o_ref, *, n_q_slots, scale):
    x = x_ref[0]
    out = x * cc_ref[...] + pltpu.roll(x, x.shape[-1] // 2, 1) * ss_ref[...]
    out = out * jnp.where(pl.program_id(2) < n_q_slots, scale, 1.0)
    o_ref[0] = out.astype(o_ref.dtype)


def _rope_tables(pos, dim):
    inv = ROPE_THETA ** (-(jnp.arange(0, dim, 2, dtype=F32) / dim))
    ang = pos.astype(F32)[:, None] * inv[None, :]
    cos, sin = jnp.cos(ang), jnp.sin(ang)
    return jnp.concatenate([cos, cos], axis=-1), jnp.concatenate([-sin, sin], axis=-1)


def nsa_rope(proj, n_q_slots, k_slots, dh, scale, tb=512):
    bsz, s_len, _ = proj.shape
    tb = min(tb, s_len)
    cc, ss = _rope_tables(jnp.arange(s_len), dh)
    n_out = n_q_slots + len(k_slots)

    def in_slot(j):
        slot = j
        for idx, ks in enumerate(k_slots):
            slot = jnp.where(j == n_q_slots + idx, ks, slot)
        return slot

    return pl.pallas_call(
        functools.partial(_rope_kernel, n_q_slots=n_q_slots, scale=scale),
        grid=(bsz, s_len // tb, n_out),
        in_specs=[pl.BlockSpec((1, tb, dh), lambda b_, t, j: (b_, t, in_slot(j))),
                  pl.BlockSpec((tb, dh), lambda b_, t, j: (t, 0)),
                  pl.BlockSpec((tb, dh), lambda b_, t, j: (t, 0))],
        out_specs=pl.BlockSpec((1, tb, dh), lambda b_, t, j: (b_, t, j)),
        out_shape=jax.ShapeDtypeStruct((bsz, s_len, n_out * dh), BF16),
        compiler_params=_params("parallel", "parallel", "arbitrary"),
        name="nsa_rope",
    )(proj, cc, ss)


def _cmp_finish_kernel(z_ref, bias_ref, w2_ref, cc_ref, ss_ref, o_ref, *, hidden, rope):
    z = z_ref[0]
    nc = z.shape[0]
    nxt = pltpu.roll(z[:, hidden:], nc - 1, 0)
    hid = _silu(z[:, :hidden] + nxt + bias_ref[...])
    out = jnp.dot(hid.astype(BF16), w2_ref[...], preferred_element_type=F32)
    if rope:
        out = out * cc_ref[...] + pltpu.roll(out, out.shape[-1] // 2, 1) * ss_ref[...]
    o_ref[0] = out.astype(o_ref.dtype)


def nsa_compress(x, pos_emb, w1, w2, bsz, s_len, groups, dh, rope, transpose_out=False):
    stride, blk = NSA_CMP_STRIDE, NSA_CMP_BLOCK
    nc = s_len // stride
    hidden = w1.shape[-1]
    half = stride * dh
    x16 = jnp.transpose(x.reshape(bsz, nc, stride, groups, dh), (0, 3, 1, 2, 4)).reshape(bsz * groups * nc, half)
    w1f = w1.reshape(blk * dh, hidden)
    wcat = jnp.concatenate([w1f[:half], w1f[half:]], axis=1).astype(BF16)
    z = matmul(x16.astype(BF16), [(wcat, 0)], 2 * hidden, name="nsa_cmp_w1")
    bias = matmul(pos_emb.reshape(1, blk * dh).astype(BF16), [(w1f.astype(BF16), 0)], hidden, name="nsa_cmp_pos")
    cc, ss = _rope_tables(jnp.arange(nc) * stride + blk - 1, dh)
    if transpose_out:
        assert not rope
        return pl.pallas_call(
            functools.partial(_cmp_finish_t_kernel, hidden=hidden),
            grid=(bsz * groups,),
            in_specs=[pl.BlockSpec((1, nc, 2 * hidden), lambda i: (i, 0, 0)),
                      pl.BlockSpec((1, hidden), lambda i: (0, 0)),
                      pl.BlockSpec((dh, hidden), lambda i: (0, 0))],
            out_specs=pl.BlockSpec((1, dh, nc), lambda i: (i, 0, 0)),
            out_shape=jax.ShapeDtypeStruct((bsz * groups, dh, nc), BF16),
            compiler_params=_params("parallel"),
            name="nsa_cmp_finish_t",
        )(z.reshape(bsz * groups, nc, 2 * hidden), bias, w2.T.astype(BF16))
    return pl.pallas_call(
        functools.partial(_cmp_finish_kernel, hidden=hidden, rope=rope),
        grid=(bsz * groups,),
        in_specs=[pl.BlockSpec((1, nc, 2 * hidden), lambda i: (i, 0, 0)),
                  pl.BlockSpec((1, hidden), lambda i: (0, 0)),
                  pl.BlockSpec((hidden, dh), lambda i: (0, 0)),
                  pl.BlockSpec((nc, dh), lambda i: (0, 0)),
                  pl.BlockSpec((nc, dh), lambda i: (0, 0))],
        out_specs=pl.BlockSpec((1, nc, dh), lambda i: (i, 0, 0)),
        out_shape=jax.ShapeDtypeStruct((bsz * groups, nc, dh), BF16),
        compiler_params=_params("parallel"),
        name="nsa_cmp_finish",
    )(z.reshape(bsz * groups, nc, 2 * hidden), bias, w2.astype(BF16), cc, ss)


def _rope_t_kernel(x_ref, cc_ref, ss_ref, o_ref, *, n_rope, scale, group, dh):
    first_slot = pl.program_id(2) * group
    for i in range(group):
        x = x_ref[0, :, i * dh:(i + 1) * dh]
        roped = (x * cc_ref[...] + pltpu.roll(x, dh // 2, 1) * ss_ref[...]) * scale
        out = jnp.where(first_slot + i < n_rope, roped, x)
        o_ref[0, i * dh:(i + 1) * dh, :] = out.T.astype(o_ref.dtype)


def nsa_rope_t(proj, slots, n_rope, dh, scale, tb=512, group=4):
    bsz, s_len, _ = proj.shape
    tb = min(tb, s_len)
    cc, ss = _rope_tables(jnp.arange(s_len), dh)
    assert len(slots) % group == 0
    firsts = slots[::group]
    assert all(f % group == 0 and slots[i * group:(i + 1) * group] == list(range(f, f + group))
               for i, f in enumerate(firsts))
    table = jnp.asarray([f // group for f in firsts], jnp.int32)
    grid_spec = pltpu.PrefetchScalarGridSpec(
        num_scalar_prefetch=1,
        grid=(bsz, s_len // tb, len(firsts)),
        in_specs=[pl.BlockSpec((1, tb, group * dh), lambda b_, t, j, tab: (b_, t, tab[j])),
                  pl.BlockSpec((tb, dh), lambda b_, t, j, tab: (t, 0)),
                  pl.BlockSpec((tb, dh), lambda b_, t, j, tab: (t, 0))],
        out_specs=pl.BlockSpec((1, group * dh, tb), lambda b_, t, j, tab: (b_, j, t)),
    )
    kern = lambda tab, x_ref, cc_ref, ss_ref, o_ref: _rope_t_kernel(x_ref, cc_ref, ss_ref, o_ref, n_rope=n_rope,
                                                                   scale=scale, group=group, dh=dh)
    return pl.pallas_call(
        kern,
        grid_spec=grid_spec,
        out_shape=jax.ShapeDtypeStruct((bsz, len(slots) * dh, s_len), BF16),
        compiler_params=_params("parallel", "parallel", "arbitrary"),
        name="nsa_rope_t",
    )(table, proj, cc, ss)


def _cmp_finish_t_kernel(z_ref, bias_ref, w2_ref, o_ref, *, hidden):
    z = z_ref[0]
    nc = z.shape[0]
    nxt = pltpu.roll(z[:, hidden:], nc - 1, 0)
    hid = _silu(z[:, :hidden] + nxt + bias_ref[...])
    o_ref[0] = _dot_nt(w2_ref[...], hid.astype(BF16)).astype(o_ref.dtype)


def _nsa_cmp_select_t_kernel(q_ref, kc_ref, vc_ref, ov_ref, oc_ref, sel_ref, *, tq, rep, dh, topn):
    qi = pl.program_id(2)
    kc = kc_ref[0]
    vct = vc_ref[0]
    nc = kc.shape[0]
    n_sel = sel_ref.shape[2]
    t = qi * tq + lax.broadcasted_iota(jnp.int32, (nc, tq), 1)
    cmp_end = lax.broadcasted_iota(jnp.int32, (nc, tq), 0) * NSA_CMP_STRIDE + (NSA_CMP_BLOCK - 1)
    visible = cmp_end <= t
    s = [jnp.where(visible, jnp.dot(kc, q_ref[0, r * dh:(r + 1) * dh, :], preferred_element_type=F32), NEG_BIG)
         for r in range(rep)]
    e = [jnp.where(visible, jnp.exp2(x - jnp.max(x, axis=0, keepdims=True)), 0.0) for x in s]
    den = [jnp.sum(x, axis=0, keepdims=True) for x in e]
    p = [e[r] / jnp.where(den[r] > 0, den[r], 1.0) for r in range(rep)]
    for r in range(rep):
        oc_ref[0, r * dh:(r + 1) * dh, :] = jnp.dot(vct, p[r].astype(BF16), preferred_element_type=F32)
    psum = p[0]
    for r in range(1, rep):
        psum = psum + p[r]
    imp = _dot_hi(ov_ref[...], psum)
    blk = lax.broadcasted_iota(jnp.int32, (n_sel, tq), 0)
    cur = (qi * tq + lax.broadcasted_iota(jnp.int32, (n_sel, tq), 1)) // NSA_SEL_BLOCK
    forced = (blk == 0) | (blk == cur) | (blk == cur - 1)
    imp = jnp.where(forced, NSA_FORCED_SCORE, imp)
    imp = jnp.where(blk > cur, -jnp.inf, imp)
    sel = jnp.zeros((n_sel, tq), F32)
    for _ in range(topn):
        m = jnp.max(imp, axis=0, keepdims=True)
        first = jnp.min(jnp.where(imp == m, blk, n_sel), axis=0, keepdims=True)
        hit = blk == first
        sel = jnp.where(hit, 1.0, sel)
        imp = jnp.where(hit, -jnp.inf, imp)
    sel_ref[0, 0] = sel


def _flash_t_init(m_ref, l_ref, acc_ref):
    m_ref[...] = jnp.full_like(m_ref, NEG_BIG)
    l_ref[...] = jnp.zeros_like(l_ref)
    acc_ref[...] = jnp.zeros_like(acc_ref)


def _flash_t_step(q_ref, k, vt, mask, m_ref, l_ref, acc_ref, rep, dh):
    hs = range(rep)
    s = [jnp.where(mask, jnp.dot(k, q_ref[0, r * dh:(r + 1) * dh, :], preferred_element_type=F32), NEG_BIG)
         for r in hs]
    m_old = [m_ref[r] for r in hs]
    m_new = [jnp.maximum(m_old[r], jnp.max(s[r], axis=0, keepdims=True)) for r in hs]
    p = [jnp.exp2(s[r] - m_new[r]).astype(BF16) for r in hs]
    alpha = [jnp.exp2(m_old[r] - m_new[r]) for r in hs]
    pv = [jnp.dot(vt, p[r], preferred_element_type=F32) for r in hs]
    ones = jnp.ones((8, k.shape[0]), BF16)
    psum = [jnp.dot(ones, p[r], preferred_element_type=F32)[0:1] for r in hs]
    for r in hs:
        m_ref[r] = m_new[r]
        l_ref[r] = alpha[r] * l_ref[r] + psum[r]
        acc_ref[r] = acc_ref[r] * alpha[r] + pv[r]


def _nsa_select_t_kernel(qi_ref, kj_ref, q_ref, k_ref, vt_ref, sel_ref, o_ref, m_ref, l_ref, acc_ref,
                         *, tq, kb, rep, dh):
    pair = pl.program_id(2)
    qi = qi_ref[pair]
    kj = kj_ref[pair]

    @pl.when(kj == 0)
    def _():
        _flash_t_init(m_ref, l_ref, acc_ref)

    kpos = kj * kb + lax.broadcasted_iota(jnp.int32, (kb, tq), 0)
    t = qi * tq + lax.broadcasted_iota(jnp.int32, (kb, tq), 1)
    per = kb // NSA_SEL_BLOCK
    chosen = jnp.zeros((kb, tq), F32)
    for i in range(per):
        row = sel_ref[0, 0, pl.ds(kj * per + i, 1), :]
        chosen = jnp.where((kpos - kj * kb) // NSA_SEL_BLOCK == i, row, chosen)
    mask = (chosen > 0) & (kpos <= t)
    _flash_t_step(q_ref, k_ref[0], vt_ref[0], mask, m_ref, l_ref, acc_ref, rep, dh)

    @pl.when(kj * kb + kb > qi * tq + tq - 1)
    def _():
        for r in range(rep):
            l = l_ref[r]
            o_ref[0, r * dh:(r + 1) * dh, :] = acc_ref[r] / jnp.where(l > 0, l, 1.0)


def _nsa_window_t_kernel(q_ref, k_ref, vt_ref, oc_ref, os_ref, g_ref, o_ref, m_ref, l_ref, acc_ref,
                         *, tq, kb, rep, dh, window, n_steps):
    qi = pl.program_id(2)
    w = pl.program_id(3)
    kblk = qi * (tq // kb) - (n_steps - tq // kb) + w

    @pl.when(w == 0)
    def _():
        _flash_t_init(m_ref, l_ref, acc_ref)

    @pl.when(kblk >= 0)
    def _():
        kpos = kblk * kb + lax.broadcasted_iota(jnp.int32, (kb, tq), 0)
        t = qi * tq + lax.broadcasted_iota(jnp.int32, (kb, tq), 1)
        mask = (kpos <= t) & (kpos > t - window)
        _flash_t_step(q_ref, k_ref[0], vt_ref[0], mask, m_ref, l_ref, acc_ref, rep, dh)

    @pl.when(w == n_steps - 1)
    def _():
        gates = g_ref[0, 0]
        for r in range(rep):
            rows = slice(r * dh, (r + 1) * dh)
            l = l_ref[r]
            o_w = acc_ref[r] / jnp.where(l > 0, l, 1.0)
            o = (gates[3 * r:3 * r + 1, :] * oc_ref[0, rows, :] + gates[3 * r + 1:3 * r + 2, :] * os_ref[0, rows, :]
                 + gates[3 * r + 2:3 * r + 3, :] * o_w)
            o_ref[0, :, rows] = o.T.astype(o_ref.dtype)


def nsa_mixer_t(u, h, w, bsz, s_len):
    t, d = u.shape
    dh, groups = NSA_HEAD_DIM, NSA_N_KV
    n_heads = d // dh
    rep = n_heads // groups
    kvw = groups * dh
    qw = n_heads * dh
    main_w = qw + 6 * kvw
    scale = dh ** -0.5
    tq = kb = min(256, s_len)
    nq = s_len // tq
    n_sel = s_len // NSA_SEL_BLOCK
    topn = min(NSA_TOPK, n_sel)
    w_in = w["nsa_w_in"]
    proj = matmul_ws(u, [(w_in, 0)], main_w, name="nsa_in").reshape(bsz, s_len, main_w)
    gates = matmul(u, [(w_in[:, main_w:].astype(BF16), 0)], w_in.shape[1] - main_w, epilogue=_ep_sigmoid,
                   name="nsa_gates")
    gates = jnp.transpose(gates.reshape(bsz, s_len, groups, rep * 3), (0, 2, 3, 1))
    slot = lambda j: (qw + j * kvw) // dh
    qvt = nsa_rope_t(proj, list(range(n_heads)) + [slot(3) + g for g in range(groups)]
                     + [slot(5) + g for g in range(groups)], n_heads, dh, scale * math.log2(math.e), group=groups)
    k_rot = nsa_rope(proj, 0, [slot(2) + g for g in range(groups)] + [slot(4) + g for g in range(groups)], dh, 1.0)
    kc = nsa_compress(proj[..., qw:qw + kvw], w["nsa_cmp_pos_k"], w["nsa_cmp_k_w1"], w["nsa_cmp_k_w2"],
                      bsz, s_len, groups, dh, True)
    vct = nsa_compress(proj[..., qw + kvw:qw + 2 * kvw], w["nsa_cmp_pos_v"], w["nsa_cmp_v_w1"], w["nsa_cmp_v_w2"],
                       bsz, s_len, groups, dh, False, transpose_out=True)
    nc = kc.shape[1]
    cs = jnp.arange(nc)[None, :] * NSA_CMP_STRIDE
    ss = jnp.arange(n_sel)[:, None] * NSA_SEL_BLOCK
    overlap_t = jnp.clip(jnp.minimum(cs + NSA_CMP_BLOCK, ss + NSA_SEL_BLOCK) - jnp.maximum(cs, ss), 0, None)
    overlap_t = overlap_t.astype(F32) / NSA_CMP_BLOCK

    qt_spec3 = pl.BlockSpec((1, rep * dh, tq), lambda b_, g, i: (b_, g, i))
    o_c, sel = pl.pallas_call(
        functools.partial(_nsa_cmp_select_t_kernel, tq=tq, rep=rep, dh=dh, topn=topn),
        grid=(bsz, groups, nq),
        in_specs=[qt_spec3,
                  pl.BlockSpec((1, nc, dh), lambda b_, g, i: (b_ * groups + g, 0, 0)),
                  pl.BlockSpec((1, dh, nc), lambda b_, g, i: (b_ * groups + g, 0, 0)),
                  pl.BlockSpec((n_sel, nc), lambda b_, g, i: (0, 0))],
        out_specs=[qt_spec3, pl.BlockSpec((1, 1, n_sel, tq), lambda b_, g, i: (b_, g, 0, i))],
        out_shape=[jax.ShapeDtypeStruct((bsz, qw, s_len), F32),
                   jax.ShapeDtypeStruct((bsz, groups, n_sel, s_len), F32)],
        compiler_params=_params("parallel", "parallel", "parallel"),
        name="nsa_cmp_select",
    )(qvt, kc, vct, overlap_t)

    flash_scratch = lambda n: [pltpu.VMEM((rep, 1, n), F32), pltpu.VMEM((rep, 1, n), F32),
                               pltpu.VMEM((rep, dh, n), F32)]
    tqs = tq
    pairs = [(i, j) for i in range(s_len // tqs) for j in range((i * tqs + tqs - 1) // kb + 1)]
    qi_of = jnp.asarray([pr[0] for pr in pairs], jnp.int32)
    kj_of = jnp.asarray([pr[1] for pr in pairs], jnp.int32)
    o_s = pl.pallas_call(
        functools.partial(_nsa_select_t_kernel, tq=tqs, kb=kb, rep=rep, dh=dh),
        grid_spec=pltpu.PrefetchScalarGridSpec(
            num_scalar_prefetch=2,
            grid=(bsz, groups, len(pairs)),
            in_specs=[pl.BlockSpec((1, rep * dh, tqs), lambda b_, g, pr, qi, kj: (b_, g, qi[pr])),
                      pl.BlockSpec((1, kb, dh), lambda b_, g, pr, qi, kj: (b_, kj[pr], g)),
                      pl.BlockSpec((1, dh, kb), lambda b_, g, pr, qi, kj: (b_, n_heads + g, kj[pr])),
                      pl.BlockSpec((1, 1, n_sel, tqs), lambda b_, g, pr, qi, kj: (b_, g, 0, qi[pr]))],
            out_specs=pl.BlockSpec((1, rep * dh, tqs), lambda b_, g, pr, qi, kj: (b_, g, qi[pr])),
            scratch_shapes=flash_scratch(tqs)),
        out_shape=jax.ShapeDtypeStruct((bsz, qw, s_len), F32),
        compiler_params=_params("parallel", "parallel", "arbitrary"),
        name="nsa_select_attn",
    )(qi_of, kj_of, qvt, k_rot, qvt, sel)

    n_steps = -(-NSA_WINDOW // kb) + tqs // kb
    win_blk = lambda i, j: jnp.maximum(i * (tqs // kb) - (n_steps - tqs // kb) + j, 0)
    qt_spec = pl.BlockSpec((1, rep * dh, tqs), lambda b_, g, i, j: (b_, g, i))
    o = pl.pallas_call(
        functools.partial(_nsa_window_t_kernel, tq=tqs, kb=kb, rep=rep, dh=dh, window=NSA_WINDOW, n_steps=n_steps),
        grid=(bsz, groups, s_len // tqs, n_steps),
        in_specs=[qt_spec,
                  pl.BlockSpec((1, kb, dh), lambda b_, g, i, j: (b_, win_blk(i, j), groups + g)),
                  pl.BlockSpec((1, dh, kb), lambda b_, g, i, j: (b_, n_heads + groups + g, win_blk(i, j))),
                  qt_spec, qt_spec,
                  pl.BlockSpec((1, 1, rep * 3, tqs), lambda b_, g, i, j: (b_, g, 0, i))],
        out_specs=pl.BlockSpec((1, tqs, rep * dh), lambda b_, g, i, j: (b_, i, g)),
        out_shape=jax.ShapeDtypeStruct((bsz, s_len, qw), BF16),
        scratch_shapes=flash_scratch(tqs),
        compiler_params=_params("parallel", "parallel", "parallel", "arbitrary"),
        name="nsa_window_attn",
    )(qvt, k_rot, qvt, o_c, o_s, gates)
    return matmul_ws(o.reshape(t, qw), [(w["nsa_w_out"], 0)], d, epilogue=_ep_residual, extras=[(h, "mn")],
                     name="nsa_out")


_MATMUL_WEIGHTS = ("pl_proj", "rw_w1", "rw_w2", "rw_a1", "rw_a2", "rw_g1", "rw_g2")


def kernel(x, p, norm_mix, norm_ffn, norm_pl, pl_proj, pl_gate, norm_final, mb_w_in, mb_conv_w, mb_conv_b, mb_dt_bias, mb_a_log, mb_d_skip, mb_norm_w, mb_w_out, nsa_w_in, nsa_cmp_pos_k, nsa_cmp_pos_v, nsa_cmp_k_w1, nsa_cmp_k_w2, nsa_cmp_v_w1, nsa_cmp_v_w2, nsa_w_out, hg_w_in, hg_lb_logits, hg_norm_w, hg_w_out, rw_mu, rw_w_rkv, rw_w0, rw_w1, rw_w2, rw_a0, rw_a1, rw_a2, rw_g1, rw_g2, rw_k_k, rw_k_a, rw_r_k, rw_ln_w, rw_ln_b, rw_w_out, ffn0_w_in, ffn0_w_out, moe1_router, moe1_w_in, moe1_w_out, ffn2_w_in, ffn2_w_out, moe3_router, moe3_w_in, moe3_w_out):
    w = dict(locals())
    for name in _MATMUL_WEIGHTS:
        w[name] = w[name].astype(BF16)
    bsz, s_len, d = x.shape
    depth = p.shape[0]
    t = bsz * s_len
    lb_all = jax.nn.softmax(hg_lb_logits.astype(F32), axis=0)
    lb_all = jnp.cumsum(lb_all, axis=0) - lb_all[0]
    dense = [(w["ffn0_w_in"], w["ffn0_w_out"]), (w["ffn2_w_in"], w["ffn2_w_out"])]
    moe = [(moe1_router, w["moe1_w_in"], w["moe1_w_out"]), (moe3_router, w["moe3_w_in"], w["moe3_w_out"])]
    p_bf = p.reshape(depth, t, p.shape[-1])
    h = x.reshape(t, d)
    for i in range(depth):
        kind = i % 4
        if kind == 0:
            h = mamba2_mixer(rmsnorm(h, norm_mix[i]), h, w, bsz, s_len)
        elif kind == 1:
            h = nsa_mixer_t(rmsnorm(h, norm_mix[i]), h, w, bsz, s_len)
        elif kind == 2:
            h = hgrn2_mixer(rmsnorm(h, norm_mix[i]), h, w, lb_all[i], bsz, s_len)
        else:
            h = rwkv7_mixer(rmsnorm(h, norm_mix[i], out_dtype=F32), h, w, bsz, s_len)
        if i % 2 == 0:
            h = dense_ffn(rmsnorm(h, norm_ffn[i]), h, *dense[i // 2])
        else:
            router, w_in, w_out = moe[i // 2]
            v, comb = rmsnorm_router(h, norm_ffn[i], router)
            h = moe_ffn_routed(v, h, comb, w_in, w_out)
        h = ple_gate(h, p_bf[i], norm_pl[i], w["pl_proj"][i], pl_gate, i)
    return rmsnorm(h, norm_final, out_dtype=F32).reshape(bsz, s_len, d)
```

```python
import functools
import math

import jax
import jax.numpy as jnp
from jax import lax
from jax.experimental import pallas as pl
from jax.experimental.pallas import tpu as pltpu

F32 = jnp.float32
BF16 = jnp.bfloat16

NORM_EPS = 1e-6
ROPE_THETA = 10000.0

V7X_VMEM_BYTES = 64 * 1024 * 1024
VMEM_LIMIT_BYTES = V7X_VMEM_BYTES - 8 * 1024 * 1024
LANES = 128

MB_D_STATE = 128
MB_CHUNK = 128

NSA_HEAD_DIM = 128
NSA_N_KV = 4
NSA_CMP_BLOCK = 32
NSA_CMP_STRIDE = 16
NSA_SEL_BLOCK = 64
NSA_TOPK = 16
NSA_WINDOW = 512
NSA_FORCED_SCORE = 1e9

HG_HEAD_DIM = 128
HG_CHUNK = 32

RW_HEAD_DIM = 64
RW_LN_EPS = 64e-5
RW_CHUNK = 128

MOE_TOPK = 2


def _params(*semantics):
    return pltpu.CompilerParams(dimension_semantics=semantics, vmem_limit_bytes=VMEM_LIMIT_BYTES)


def _pick(n, target):
    if n <= target:
        return n
    for c in range(target, 0, -1):
        if n % c == 0:
            return c
    return n


def _silu(x):
    return x * jax.nn.sigmoid(x)


def _rmsnorm_kernel(x_ref, g_ref, o_ref):
    x = x_ref[...]
    ms = jnp.mean(x * x, axis=-1, keepdims=True)
    o_ref[...] = (x * lax.rsqrt(ms + NORM_EPS) * g_ref[...]).astype(o_ref.dtype)


def rmsnorm(x, gain, out_dtype=BF16, name="rmsnorm"):
    m, d = x.shape
    bm = _pick(m, 256)
    return pl.pallas_call(
        _rmsnorm_kernel,
        grid=(m // bm,),
        in_specs=[pl.BlockSpec((bm, d), lambda i: (i, 0)), pl.BlockSpec((1, d), lambda i: (0, 0))],
        out_specs=pl.BlockSpec((bm, d), lambda i: (i, 0)),
        out_shape=jax.ShapeDtypeStruct((m, d), out_dtype),
        compiler_params=_params("parallel"),
        name=name,
    )(x, gain.reshape(1, d).astype(F32))


def _mm_kernel(*refs, n_w, n_extra, nk, epilogue):
    x_ref = refs[0]
    w_refs = refs[1:1 + n_w]
    e_refs = refs[1 + n_w:1 + n_w + n_extra]
    o_ref = refs[1 + n_w + n_extra]
    acc_refs = refs[2 + n_w + n_extra:]
    x = x_ref[...]
    if nk == 1:
        accs = [jnp.dot(x, w[...], preferred_element_type=F32) for w in w_refs]
        o_ref[...] = epilogue(accs, [e[...] for e in e_refs]).astype(o_ref.dtype)
        return
    k = pl.program_id(2)

    @pl.when(k == 0)
    def _():
        for a in acc_refs:
            a[...] = jnp.zeros_like(a)

    for a, w in zip(acc_refs, w_refs):
        a[...] += jnp.dot(x, w[...], preferred_element_type=F32)

    @pl.when(k == nk - 1)
    def _():
        o_ref[...] = epilogue([a[...] for a in acc_refs], [e[...] for e in e_refs]).astype(o_ref.dtype)


def _first(accs, extras):
    return accs[0]


def matmul(x, ws, n_out, *, epilogue=_first, extras=(), out_dtype=F32, bm=1024, bn=512, bk=None, name="matmul"):
    m, kdim = x.shape
    bm = _pick(m, bm)
    bn = _pick(n_out, bn)
    if bk is None:
        bk = kdim if kdim <= 4096 else _pick(kdim, 4096)
    nk = kdim // bk
    assert kdim % bk == 0 and m % bm == 0 and n_out % bn == 0
    in_specs = [pl.BlockSpec((bm, bk), lambda i, j, k: (i, k))]
    args = [x]
    for w, off in ws:
        assert off % bn == 0 and w.shape[0] == kdim
        in_specs.append(pl.BlockSpec((bk, bn), functools.partial(lambda i, j, k, o: (k, j + o), o=off // bn)))
        args.append(w)
    for arr, kind in extras:
        if kind == "mn":
            in_specs.append(pl.BlockSpec((bm, bn), lambda i, j, k: (i, j)))
        elif kind == "m":
            in_specs.append(pl.BlockSpec((bm, arr.shape[1]), lambda i, j, k: (i, 0)))
        elif kind == "kn":
            in_specs.append(pl.BlockSpec((arr.shape[0], bn), lambda i, j, k: (0, j)))
        else:
            in_specs.append(pl.BlockSpec((1, bn), lambda i, j, k: (0, j)))
        args.append(arr)
    scratch = [pltpu.VMEM((bm, bn), F32) for _ in ws] if nk > 1 else []
    kern = functools.partial(_mm_kernel, n_w=len(ws), n_extra=len(extras), nk=nk, epilogue=epilogue)
    return pl.pallas_call(
        kern,
        grid=(m // bm, n_out // bn, nk),
        in_specs=in_specs,
        out_specs=pl.BlockSpec((bm, bn), lambda i, j, k: (i, j)),
        out_shape=jax.ShapeDtypeStruct((m, n_out), out_dtype),
        scratch_shapes=scratch,
        compiler_params=_params("parallel", "parallel", "arbitrary"),
        name=name,
    )(*args)


WS_CAST_CHUNK = 512


def _mm_ws_kernel(*refs, n_w, n_extra, epilogue):
    x_ref = refs[0]
    w_refs = refs[1:1 + n_w]
    e_refs = refs[1 + n_w:1 + n_w + n_extra]
    o_ref = refs[1 + n_w + n_extra]
    wb_refs = refs[2 + n_w + n_extra:]

    kdim = x_ref.shape[1]
    ck = _pick(kdim, WS_CAST_CHUNK)

    @pl.when(pl.program_id(1) == 0)
    def _():
        accs = [None] * n_w
        for c in range(kdim // ck):
            rows = slice(c * ck, (c + 1) * ck)
            xc = x_ref[:, rows]
            for n, (w, wb) in enumerate(zip(w_refs, wb_refs)):
                wc = (w[0, rows, :] if len(w.shape) == 3 else w[rows, :]).astype(BF16)
                wb[rows, :] = wc
                part = jnp.dot(xc, wc, preferred_element_type=F32)
                accs[n] = part if accs[n] is None else accs[n] + part
        o_ref[...] = epilogue(accs, [e[...] for e in e_refs]).astype(o_ref.dtype)

    @pl.when(pl.program_id(1) != 0)
    def _():
        x = x_ref[...]
        accs = [jnp.dot(x, wb[...], preferred_element_type=F32) for wb in wb_refs]
        o_ref[...] = epilogue(accs, [e[...] for e in e_refs]).astype(o_ref.dtype)


def matmul_ws(x, ws, n_out, *, epilogue=_first, extras=(), out_dtype=F32, bm=1024, bn=512, w_buffers=2,
              name="matmul_ws"):
    m, kdim = x.shape
    bm = _pick(m, bm)
    bn = _pick(n_out, bn)
    assert m % bm == 0 and n_out % bn == 0
    mode = {} if w_buffers == 2 else {"pipeline_mode": pl.Buffered(w_buffers)}
    in_specs = [pl.BlockSpec((bm, kdim), lambda j, i: (i, 0))]
    args = [x]
    for w, off in ws:
        if w.ndim == 3:
            e, o = off
            assert o % bn == 0 and w.shape[1] == kdim
            in_specs.append(pl.BlockSpec((1, kdim, bn), functools.partial(lambda j, i, e_, o_: (e_, 0, j + o_),
                                                                          e_=e, o_=o // bn), **mode))
        else:
            assert off % bn == 0 and w.shape[0] == kdim
            in_specs.append(pl.BlockSpec((kdim, bn), functools.partial(lambda j, i, o_: (0, j + o_), o_=off // bn),
                                         **mode))
        args.append(w)
    for arr, kind in extras:
        if kind == "mn":
            in_specs.append(pl.BlockSpec((bm, bn), lambda j, i: (i, j)))
        elif kind == "m":
            in_specs.append(pl.BlockSpec((bm, arr.shape[1]), lambda j, i: (i, 0)))
        elif kind == "kn":
            in_specs.append(pl.BlockSpec((arr.shape[0], bn), lambda j, i: (0, j)))
        else:
            in_specs.append(pl.BlockSpec((1, bn), lambda j, i: (0, j)))
        args.append(arr)
    kern = functools.partial(_mm_ws_kernel, n_w=len(ws), n_extra=len(extras), epilogue=epilogue)
    return pl.pallas_call(
        kern,
        grid=(n_out // bn, m // bm),
        in_specs=in_specs,
        out_specs=pl.BlockSpec((bm, bn), lambda j, i: (i, j)),
        out_shape=jax.ShapeDtypeStruct((m, n_out), out_dtype),
        scratch_shapes=[pltpu.VMEM((kdim, bn), BF16) for _ in ws],
        compiler_params=_params("parallel", "arbitrary"),
        name=name,
    )(*args)


def _ep_residual(accs, extras):
    return extras[0] + accs[0]


def _ep_swiglu(accs, extras):
    return _silu(accs[0]) * accs[1]


def _ep_tanh(accs, extras):
    return jnp.tanh(accs[0])


def _ep_sigmoid(accs, extras):
    return jax.nn.sigmoid(accs[0])


def _ep_bias_sigmoid(accs, extras):
    return jax.nn.sigmoid(accs[0] + extras[0])


def _ep_rw_logdecay(accs, extras):
    w = -jax.nn.softplus(-(accs[0] + extras[0])) - 0.5
    return -jnp.exp(w)


def _ep_ple_gate(accs, extras):
    pp = jnp.dot(extras[1].astype(BF16), extras[2], preferred_element_type=F32)
    return extras[0] + pp * jax.nn.sigmoid(accs[0])


HALO = 8


def _conv_silu_chunk(x_ref, tail_ref, w_ref, b_ref):
    x = x_ref[0]
    tail = tail_ref[...]
    k_width = w_ref.shape[0]
    row = lax.broadcasted_iota(jnp.int32, tail.shape, 0)
    y = b_ref[...] + w_ref[k_width - 1:k_width, :] * x
    for j in range(k_width - 1):
        shift = k_width - 1 - j
        xr = pltpu.roll(x, shift, 0)
        top = jnp.where(row < shift, pltpu.roll(tail, shift, 0), xr[0:HALO])
        y = y + w_ref[j:j + 1, :] * jnp.concatenate([top, xr[HALO:]], axis=0)
    tail_ref[...] = x[x.shape[0] - HALO:, :]
    return _silu(y)


def _cumsum_rows(x, n):
    row = lax.broadcasted_iota(jnp.int32, x.shape, 0)
    s = 1
    while s < n:
        x = x + jnp.where(row >= s, pltpu.roll(x, s, 0), 0.0)
        s *= 2
    return x


def _cumsum_lanes(x, n):
    col = lax.broadcasted_iota(jnp.int32, x.shape, 1)
    s = 1
    while s < n:
        x = x + jnp.where(col >= s, pltpu.roll(x, s, 1), 0.0)
        s *= 2
    return x


def _dot_nt(a, b):
    return lax.dot_general(a, b, (((1,), (1,)), ((), ())), preferred_element_type=F32)


def _dot_tn(a, b):
    return lax.dot_general(a, b, (((0,), (0,)), ((), ())), preferred_element_type=F32)


def _ssd_kernel(xs_ref, b_ref, c_ref, z_ref, dt_ref, dtt_ref, bias_r_ref, bias_c_ref, alog_r_ref, alog_c_ref,
                dskip_ref, normw_ref, wx_ref, wb_ref, wc_ref, bx_ref, bb_ref, bc_ref, o_ref,
                state_ref, y_ref, tx_ref, tb_ref, tc_ref, *, chunk, heads, p_dim):
    @pl.when(pl.program_id(2) == 0)
    def _():
        state_ref[...] = jnp.zeros_like(state_ref)
        tx_ref[...] = jnp.zeros_like(tx_ref)
        tb_ref[...] = jnp.zeros_like(tb_ref)
        tc_ref[...] = jnp.zeros_like(tc_ref)

    dt = jax.nn.softplus(dt_ref[0, 0] + bias_r_ref[0])
    dtt = jax.nn.softplus(dtt_ref[0, 0] + bias_c_ref[0])
    a_cum = _cumsum_rows(dt * -jnp.exp(alog_r_ref[0]), chunk)
    a_cum_t = _cumsum_lanes(dtt * -jnp.exp(alog_c_ref[0]), chunk)
    xs = _conv_silu_chunk(xs_ref, tx_ref, wx_ref, bx_ref)
    bmat = _conv_silu_chunk(b_ref, tb_ref, wb_ref, bb_ref)
    cmat = _conv_silu_chunk(c_ref, tc_ref, wc_ref, bc_ref).astype(BF16)
    cb = _dot_nt(cmat, bmat.astype(BF16))
    b_t = bmat.T.astype(BF16)
    li = lax.broadcasted_iota(jnp.int32, (chunk, chunk), 0)
    si = lax.broadcasted_iota(jnp.int32, (chunk, chunk), 1)
    causal = li >= si
    per = LANES // p_dim
    lane_seg = lax.broadcasted_iota(jnp.int32, (1, LANES), 1) // p_dim

    def pick(vals):
        out = vals[-1]
        for i in range(per - 2, -1, -1):
            out = jnp.where(lane_seg == i, vals[i], out)
        return out

    dot = functools.partial(jnp.dot, preferred_element_type=F32)
    es = range(heads)
    tiles = range(heads // per)
    head_row = lax.broadcasted_iota(jnp.int32, (heads, heads * LANES), 0)
    to_tile = jnp.where(lax.broadcasted_iota(jnp.int32, (heads, heads * LANES), 1) // LANES == head_row, 1.0, 0.0)
    cum_t = _dot_hi(a_cum, to_tile)
    of = lambda vals, i: [vals[i * per + j] for j in range(per)]
    tile = lambda x, i: x[:, i * LANES:(i + 1) * LANES]
    cum_c = jnp.concatenate([pick([tile(cum_t, e) for e in of(es, i)]) for i in tiles], axis=-1)
    dt_c = jnp.concatenate([pick([dt[:, e:e + 1] for e in of(es, i)]) for i in tiles], axis=-1)
    last_c = cum_c[chunk - 1:chunk, :]
    m = [(cb * jnp.exp(jnp.where(causal, tile(cum_t, e) - a_cum_t[e:e + 1, :], -jnp.inf))).astype(BF16) for e in es]
    xdt = xs * dt_c
    xdt_b = xdt.astype(BF16)
    xend_b = (xdt * jnp.exp(last_c - cum_c)).astype(BF16)
    grow = jnp.exp(cum_c)
    st_decay = jnp.exp(last_c)
    st = [state_ref[i] for i in tiles]
    y_in = [pick([dot(m[e], tile(xdt_b, i)) for e in of(es, i)]) for i in tiles]
    y_st = [dot(cmat, st[i].astype(BF16)) * tile(grow, i) for i in tiles]
    for i in tiles:
        state_ref[i] = st[i] * tile(st_decay, i) + dot(b_t, tile(xend_b, i))
        y_ref[:, i * LANES:(i + 1) * LANES] = y_in[i] + y_st[i]
    y = y_ref[...] + xs * dskip_ref[...]
    y = y * _silu(z_ref[0].astype(F32))
    ms = jnp.mean(y * y, axis=-1, keepdims=True)
    o_ref[0] = (y * lax.rsqrt(ms + NORM_EPS) * normw_ref[...]).astype(o_ref.dtype)


def ssd_scan(xbc, z, dt, conv_w, conv_b, dt_bias, a_log, d_skip, norm_w, *, chunk=MB_CHUNK):
    bsz, s_len, d_inner = z.shape
    k_width = conv_w.shape[0]
    assert k_width - 1 <= HALO <= min(chunk, s_len)
    conv_b2 = conv_b.reshape(1, -1)
    n_heads = dt.shape[-1]
    n_state = MB_D_STATE
    groups = (xbc.shape[-1] - d_inner) // (2 * n_state)
    heads = n_heads // groups
    p_dim = d_inner // n_heads
    gw = heads * p_dim
    assert gw % LANES == 0 and d_inner % n_state == 0
    chunk = min(chunk, s_len)
    nc = s_len // chunk
    b_off = d_inner // n_state
    c_off = b_off + groups
    dt_g = jnp.transpose(dt.reshape(bsz, s_len, groups, heads), (0, 2, 1, 3))
    dt_gt = jnp.transpose(dt_g, (0, 1, 3, 2))
    kern = functools.partial(_ssd_kernel, chunk=chunk, heads=heads, p_dim=p_dim)
    per_group = lambda b_, g, c: (g, 0, 0)
    return pl.pallas_call(
        kern,
        grid=(bsz, groups, nc),
        in_specs=[pl.BlockSpec((1, chunk, gw), lambda b_, g, c: (b_, c, g)),
                  pl.BlockSpec((1, chunk, n_state), lambda b_, g, c: (b_, c, b_off + g)),
                  pl.BlockSpec((1, chunk, n_state), lambda b_, g, c: (b_, c, c_off + g)),
                  pl.BlockSpec((1, chunk, gw), lambda b_, g, c: (b_, c, g)),
                  pl.BlockSpec((1, 1, chunk, heads), lambda b_, g, c: (b_, g, c, 0)),
                  pl.BlockSpec((1, 1, heads, chunk), lambda b_, g, c: (b_, g, 0, c)),
                  pl.BlockSpec((1, 1, heads), per_group),
                  pl.BlockSpec((1, heads, 1), per_group),
                  pl.BlockSpec((1, 1, heads), per_group),
                  pl.BlockSpec((1, heads, 1), per_group),
                  pl.BlockSpec((1, gw), lambda b_, g, c: (0, g)),
                  pl.BlockSpec((1, gw), lambda b_, g, c: (0, g)),
                  pl.BlockSpec((k_width, gw), lambda b_, g, c: (0, g)),
                  pl.BlockSpec((k_width, n_state), lambda b_, g, c: (0, b_off + g)),
                  pl.BlockSpec((k_width, n_state), lambda b_, g, c: (0, c_off + g)),
                  pl.BlockSpec((1, gw), lambda b_, g, c: (0, g)),
                  pl.BlockSpec((1, n_state), lambda b_, g, c: (0, b_off + g)),
                  pl.BlockSpec((1, n_state), lambda b_, g, c: (0, c_off + g))],
        out_specs=pl.BlockSpec((1, chunk, gw), lambda b_, g, c: (b_, c, g)),
        out_shape=jax.ShapeDtypeStruct(z.shape, BF16),
        scratch_shapes=[pltpu.VMEM((gw // LANES, n_state, LANES), F32), pltpu.VMEM((chunk, gw), F32),
                        pltpu.VMEM((HALO, gw), F32), pltpu.VMEM((HALO, n_state), F32),
                        pltpu.VMEM((HALO, n_state), F32)],
        compiler_params=_params("parallel", "parallel", "arbitrary"),
        name="mamba_ssd",
    )(xbc, xbc, xbc, z, dt_g, dt_gt,
      dt_bias.reshape(groups, 1, heads), dt_bias.reshape(groups, heads, 1),
      a_log.reshape(groups, 1, heads), a_log.reshape(groups, heads, 1),
      jnp.repeat(d_skip, p_dim).reshape(1, d_inner), norm_w.reshape(1, d_inner),
      conv_w, conv_w, conv_w, conv_b2, conv_b2, conv_b2)


def mamba2_mixer(u, h, w, bsz, s_len):
    d_inner = w["mb_w_out"].shape[0]
    n_heads = w["mb_dt_bias"].shape[0]
    w_in = w["mb_w_in"]
    xbc_w = w_in.shape[1] - d_inner - n_heads
    z = matmul_ws(u, [(w_in, 0)], d_inner, out_dtype=BF16, name="mb_in_z")
    xbc = matmul_ws(u, [(w_in, d_inner)], xbc_w, name="mb_in_xbc")
    dt = matmul_ws(u, [(w_in, d_inner + xbc_w)], n_heads, name="mb_in_dt")
    y = ssd_scan(xbc.reshape(bsz, s_len, xbc_w), z.reshape(bsz, s_len, d_inner), dt.reshape(bsz, s_len, n_heads),
                 w["mb_conv_w"], w["mb_conv_b"], w["mb_dt_bias"], w["mb_a_log"], w["mb_d_skip"], w["mb_norm_w"])
    return matmul_ws(y.reshape(bsz * s_len, d_inner), [(w["mb_w_out"], 0)], h.shape[1],
                     epilogue=_ep_residual, extras=[(h, "mn")], bm=512, w_buffers=1, name="mb_out")


def _seg_cumsum_rows(x, seg, reverse=False):
    n = x.shape[0]
    pos = lax.broadcasted_iota(jnp.int32, x.shape, 0) % seg
    s = 1
    while s < seg:
        if reverse:
            x = x + jnp.where(pos < seg - s, pltpu.roll(x, n - s, 0), 0.0)
        else:
            x = x + jnp.where(pos >= s, pltpu.roll(x, s, 0), 0.0)
        s *= 2
    return x


def _hgrn_kernel(q_ref, f_ref, i_ref, g_ref, lb_ref, nw_ref, o_ref, state_ref, *, sub, n_sub, heads, dk):
    @pl.when(pl.program_id(2) == 0)
    def _():
        state_ref[...] = jnp.zeros_like(state_ref)

    lb = lb_ref[...]
    nw = nw_ref[...]
    ti = lax.broadcasted_iota(jnp.int32, (sub, sub), 0)
    si = lax.broadcasted_iota(jnp.int32, (sub, sub), 1)
    causal = ti >= si
    f = lb + (1.0 - lb) * jax.nn.sigmoid(f_ref[0])
    lf = jnp.log(f)
    k = 1.0 - f
    b = _seg_cumsum_rows(lf, sub)
    to_end = _seg_cumsum_rows(lf, sub, reverse=True) - lf
    q_dec = (_silu(q_ref[0]) * jnp.exp(b)).astype(BF16)
    k_dec = (k * jnp.exp(-b)).astype(BF16)
    k_end = (k * jnp.exp(to_end)).astype(BF16)
    v = i_ref[0].astype(BF16)
    cs = range(n_sub)
    hs = range(heads)
    blk = lambda x, c, h: x[c * sub:(c + 1) * sub, h * dk:(h + 1) * dk]
    scores = [[jnp.where(causal, _dot_nt(blk(q_dec, c, h), blk(k_dec, c, h)), 0.0).astype(BF16) for h in hs]
              for c in cs]
    upd = [[_dot_tn(blk(v, c, h), blk(k_end, c, h)) for h in hs] for c in cs]
    states = []
    st = [state_ref[h] for h in hs]
    for c in cs:
        states.append(st)
        decay = jnp.exp(b[(c + 1) * sub - 1:(c + 1) * sub, :])
        st = [st[h] * decay[:, h * dk:(h + 1) * dk] + upd[c][h] for h in hs]
    for h in hs:
        state_ref[h] = st[h]
    for c in cs:
        rows = slice(c * sub, (c + 1) * sub)
        for h in hs:
            o = (jnp.dot(scores[c][h], blk(v, c, h), preferred_element_type=F32)
                 + _dot_nt(blk(q_dec, c, h), states[c][h].astype(BF16)))
            o = o * lax.rsqrt(jnp.mean(o * o, axis=-1, keepdims=True) + NORM_EPS) * nw
            cols = slice(h * dk, (h + 1) * dk)
            o_ref[0, rows, cols] = (o * _silu(g_ref[0, rows, cols])).astype(o_ref.dtype)


def hgrn2_scan(proj, lower_bound, norm_w, *, dk=HG_HEAD_DIM, sub=HG_CHUNK, tb=256, heads=8):
    bsz, s_len, d4 = proj.shape
    d = d4 // 4
    n_heads = d // dk
    tb = min(tb, s_len)
    heads = min(heads, n_heads)
    hw = heads * dk
    n_hb = n_heads // heads
    kern = functools.partial(_hgrn_kernel, sub=sub, n_sub=tb // sub, heads=heads, dk=dk)
    spec = lambda part: pl.BlockSpec((1, tb, hw), lambda b_, h_, t: (b_, t, part * n_hb + h_))
    return pl.pallas_call(
        kern,
        grid=(bsz, n_hb, s_len // tb),
        in_specs=[spec(0), spec(1), spec(2), spec(3),
                  pl.BlockSpec((1, hw), lambda b_, h_, t: (0, h_)),
                  pl.BlockSpec((1, dk), lambda b_, h_, t: (0, 0))],
        out_specs=pl.BlockSpec((1, tb, hw), lambda b_, h_, t: (b_, t, h_)),
        out_shape=jax.ShapeDtypeStruct((bsz, s_len, d), BF16),
        scratch_shapes=[pltpu.VMEM((heads, dk, dk), F32)],
        compiler_params=_params("parallel", "parallel", "arbitrary"),
        name="hgrn2_scan",
    )(proj, proj, proj, proj, lower_bound.reshape(1, d), norm_w.reshape(1, dk))


def hgrn2_mixer(u, h, w, lower_bound, bsz, s_len):
    d = h.shape[1]
    proj = matmul_ws(u, [(w["hg_w_in"], 0)], 4 * d, bm=512, bn=1024, name="hg_in")
    o = hgrn2_scan(proj.reshape(bsz, s_len, 4 * d), lower_bound, w["hg_norm_w"])
    return matmul_ws(o.reshape(bsz * s_len, d), [(w["hg_w_out"], 0)], d,
                     epilogue=_ep_residual, extras=[(h, "mn")], name="hg_out")


def dense_ffn(v, h, w_in, w_out):
    f = w_out.shape[0]
    hid = matmul_ws(v, [(w_in, 0), (w_in, f)], f, epilogue=_ep_swiglu, out_dtype=BF16, bm=1024, bn=256,
                    name="ffn_in")
    return matmul_ws(hid, [(w_out, 0)], h.shape[1], epilogue=_ep_residual, extras=[(h, "mn")], bm=512, w_buffers=1,
                     name="ffn_out")


def _rmsnorm_router_kernel(x_ref, g_ref, r_ref, o_ref, c_ref, *, n_experts):
    x = x_ref[...]
    ms = jnp.mean(x * x, axis=-1, keepdims=True)
    v = (x * lax.rsqrt(ms + NORM_EPS) * g_ref[...]).astype(BF16)
    o_ref[...] = v
    logits = jnp.dot(v, r_ref[...], preferred_element_type=F32)
    lane = lax.broadcasted_iota(jnp.int32, logits.shape, 1)
    logits = jnp.where(lane < n_experts, logits, -jnp.inf)
    m1 = jnp.max(logits, axis=-1, keepdims=True)
    i1 = jnp.min(jnp.where(logits == m1, lane, LANES), axis=-1, keepdims=True)
    rest = jnp.where(lane == i1, -jnp.inf, logits)
    m2 = jnp.max(rest, axis=-1, keepdims=True)
    i2 = jnp.min(jnp.where(rest == m2, lane, LANES), axis=-1, keepdims=True)
    e2 = jnp.exp(m2 - m1)
    w1 = 1.0 / (1.0 + e2)
    c_ref[...] = jnp.where(lane == i1, w1, 0.0) + jnp.where(lane == i2, e2 * w1, 0.0)


def rmsnorm_router(x, gain, router):
    m, d = x.shape
    n_experts = router.shape[1]
    r_pad = jnp.zeros((d, LANES), BF16).at[:, :n_experts].set(router.astype(BF16))
    bm = _pick(m, 256)
    return pl.pallas_call(
        functools.partial(_rmsnorm_router_kernel, n_experts=n_experts),
        grid=(m // bm,),
        in_specs=[pl.BlockSpec((bm, d), lambda i: (i, 0)), pl.BlockSpec((1, d), lambda i: (0, 0)),
                  pl.BlockSpec((d, LANES), lambda i: (0, 0))],
        out_specs=[pl.BlockSpec((bm, d), lambda i: (i, 0)), pl.BlockSpec((bm, LANES), lambda i: (i, 0))],
        out_shape=[jax.ShapeDtypeStruct((m, d), BF16), jax.ShapeDtypeStruct((m, LANES), F32)],
        compiler_params=_params("parallel"),
        name="rmsnorm_router",
    )(x, gain.reshape(1, d).astype(F32), r_pad)


MOE_BLOCK = 1024
MOE_UNIT = 32
MOE_TILE = 512


def _moe_gather_kernel(x_ref, tok_ref, o_ref):
    tok = tok_ref[0]
    lane = lax.broadcasted_iota(jnp.int32, (tok.shape[0], x_ref.shape[0]), 1)
    onehot = jnp.where(tok == lane, 1.0, 0.0).astype(BF16)
    o_ref[...] = jnp.dot(onehot, x_ref[...], preferred_element_type=F32).astype(o_ref.dtype)


def _moe_expert_in_kernel(src_ref, exp_ref, first_ref, used_ref, *refs, per):
    x_refs = refs[:per]
    wg_ref, wu_ref, rw_ref, o_ref, wgb_ref, wub_ref, x_scr = refs[per:]
    t = pl.program_id(1)

    @pl.when(t < used_ref[0])
    def _():
        @pl.when(first_ref[t] == 1)
        def _():
            wgb_ref[...] = wg_ref[0].astype(BF16)
            wub_ref[...] = wu_ref[0].astype(BF16)

        unit = x_refs[0].shape[0]
        for i in range(per):
            x_scr[i * unit:(i + 1) * unit, :] = x_refs[i][...]
        x = x_scr[...]
        g = jnp.dot(x, wgb_ref[...], preferred_element_type=F32)
        u = jnp.dot(x, wub_ref[...], preferred_element_type=F32)
        o_ref[...] = (_silu(g) * u * rw_ref[...]).astype(o_ref.dtype)

    @pl.when(t >= used_ref[0])
    def _():
        o_ref[...] = jnp.zeros_like(o_ref)


def _moe_expert_out_kernel(exp_ref, first_ref, used_ref, hid_ref, w_ref, o_ref, wb_ref):
    t = pl.program_id(1)

    @pl.when(t < used_ref[0])
    def _():
        @pl.when(first_ref[t] == 1)
        def _():
            wb_ref[...] = w_ref[0].astype(BF16)

        o_ref[...] = jnp.dot(hid_ref[...], wb_ref[...], preferred_element_type=F32).astype(o_ref.dtype)

    @pl.when(pl.program_id(1) >= used_ref[0])
    def _():
        o_ref[...] = jnp.zeros_like(o_ref)


def _moe_scatter_kernel(dst_ref, h_ref, tok_ref, *refs, per):
    y_refs = refs[:per]
    o_ref, y_scr = refs[per:]

    @pl.when(pl.program_id(2) == 0)
    def _():
        o_ref[...] = h_ref[...]

    tok = tok_ref[0]
    row = lax.broadcasted_iota(jnp.int32, (o_ref.shape[0], tok.shape[1]), 0)
    onehot_t = jnp.where(tok == row, 1.0, 0.0).astype(BF16)
    unit = y_refs[0].shape[0]
    for i in range(per):
        y_scr[i * unit:(i + 1) * unit, :] = y_refs[i][...]
    o_ref[...] += jnp.dot(onehot_t, y_scr[...], preferred_element_type=F32)


def moe_ffn_routed(v, h, comb, w_in, w_out, *, tb=MOE_BLOCK, unit=MOE_UNIT, tile=MOE_TILE, bn=512, bo=1024):
    m, d = v.shape
    n_experts, _, two_de = w_in.shape
    de = two_de // 2
    tb = min(tb, m)
    nb = m // tb
    per = tile // unit
    bn = _pick(de, bn)
    bo = _pick(d, bo)
    n_assign = MOE_TOPK * tb
    n_slots = -(-(n_assign // unit + n_experts + 1) // per) * per
    groups = n_slots // per
    n_units = nb * (n_assign // unit + n_experts) + n_experts * (per - 1)
    n_tiles = -(-n_units // per)
    n_units = n_tiles * per

    wts, ids = lax.top_k(comb[:, :n_experts], MOE_TOPK)
    ea = ids.reshape(nb, n_assign)
    wa = wts.reshape(nb, n_assign)
    ta = jnp.broadcast_to(jnp.repeat(jnp.arange(tb, dtype=jnp.int32), MOE_TOPK)[None], (nb, n_assign))
    se, st, sw = lax.sort((ea, ta, wa), dimension=1, num_keys=1, is_stable=True)
    counts = jnp.sum(jax.nn.one_hot(ea, n_experts, dtype=jnp.int32), axis=1)
    units = -(-counts // unit)
    excl = lambda x, axis: jnp.cumsum(x, axis=axis) - x
    slot_start = excl(units, 1)
    row_start = excl(counts, 1)
    is_e = se[..., None] == jnp.arange(n_experts, dtype=jnp.int32)
    lookup = lambda table: jnp.sum(jnp.where(is_e, table[:, None, :], 0), axis=-1)
    pos = lookup(slot_start) * unit + jnp.arange(n_assign, dtype=jnp.int32)[None] - lookup(row_start)
    bidx = jnp.arange(nb, dtype=jnp.int32)[:, None]
    n_rows = n_slots * unit
    rr = jnp.arange(2 * n_rows, dtype=jnp.int32)[None, :, None]
    hit = pos[:, None, :] == rr % n_rows
    val = jnp.where(rr < n_rows, (st + 1).astype(F32)[:, None, :], sw[:, None, :])
    tab = jnp.sum(jnp.where(hit, val, 0.0), axis=-1)
    row_token = tab[:, :n_rows].astype(jnp.int32) - 1
    row_weight = tab[:, n_rows:]
    per_expert = jnp.sum(units, axis=0)
    per_expert_pad = -(-per_expert // per) * per
    e_off = excl(per_expert_pad, 0)
    before = excl(units, 0)
    slot = jnp.arange(n_slots, dtype=jnp.int32)
    slot_end = jnp.cumsum(units, axis=1)
    e_of_slot = jnp.sum(slot[None, :, None] >= slot_end[:, None, :], axis=-1)
    used_slot = e_of_slot < n_experts
    e_clip = jnp.minimum(e_of_slot, n_experts - 1)
    dst_unit = (e_off[e_clip] + jnp.take_along_axis(before, e_clip, axis=1)
                + slot[None] - jnp.take_along_axis(slot_start, e_clip, axis=1))
    dst_unit = jnp.where(used_slot, dst_unit, 0).astype(jnp.int32)
    flat_slot = (bidx * n_slots + slot[None]).astype(jnp.int32)
    zero_slot = n_slots - 1
    src_unit = jnp.full((n_units,), zero_slot, jnp.int32).at[
        jnp.where(used_slot, dst_unit, n_units).reshape(-1)].set(flat_slot.reshape(-1), mode="drop")
    tile_end = jnp.cumsum(per_expert_pad) // per
    tile_ids = jnp.arange(n_tiles, dtype=jnp.int32)
    tile_expert = jnp.minimum(jnp.sum(tile_ids[:, None] >= tile_end[None, :], axis=-1), n_experts - 1).astype(jnp.int32)
    tiles_used = tile_end[-1:].astype(jnp.int32)
    first = jnp.concatenate([jnp.ones((1,), jnp.int32),
                             (tile_expert[1:] != tile_expert[:-1]).astype(jnp.int32)])
    rw_em = row_weight.reshape(nb * n_slots, unit)[src_unit].reshape(n_units * unit, 1)

    xs = pl.pallas_call(
        _moe_gather_kernel,
        grid=(nb, groups),
        in_specs=[pl.BlockSpec((tb, d), lambda b_, g: (b_, 0)),
                  pl.BlockSpec((1, tile, 1), lambda b_, g: (b_, g, 0))],
        out_specs=pl.BlockSpec((tile, d), lambda b_, g: (b_ * groups + g, 0)),
        out_shape=jax.ShapeDtypeStruct((nb * n_slots * unit, d), BF16),
        compiler_params=_params("parallel", "arbitrary"),
        name="moe_gather",
    )(v, row_token.reshape(nb, n_slots * unit, 1))

    bpe = de // bn
    unit_spec = lambda i: pl.BlockSpec((unit, d), lambda j, t, src, ex, fi, us: (src[per * t + i], 0))
    hid = pl.pallas_call(
        functools.partial(_moe_expert_in_kernel, per=per),
        grid_spec=pltpu.PrefetchScalarGridSpec(
            num_scalar_prefetch=4,
            grid=(bpe, n_tiles),
            in_specs=[unit_spec(i) for i in range(per)] + [
                pl.BlockSpec((1, d, bn), lambda j, t, src, ex, fi, us: (ex[t], 0, j),
                             pipeline_mode=pl.Buffered(1)),
                pl.BlockSpec((1, d, bn), lambda j, t, src, ex, fi, us: (ex[t], 0, j + bpe),
                             pipeline_mode=pl.Buffered(1)),
                pl.BlockSpec((tile, 1), lambda j, t, src, ex, fi, us: (t, 0))],
            out_specs=pl.BlockSpec((tile, bn), lambda j, t, src, ex, fi, us: (t, j)),
            scratch_shapes=[pltpu.VMEM((d, bn), BF16), pltpu.VMEM((d, bn), BF16), pltpu.VMEM((tile, d), BF16)]),
        out_shape=jax.ShapeDtypeStruct((n_tiles * tile, de), BF16),
        compiler_params=_params("arbitrary", "arbitrary"),
        name="moe_expert_in",
    )(src_unit, tile_expert, first, tiles_used, *([xs] * per), w_in, w_in, rw_em)

    bo2 = _pick(d, 2 * bo)
    ys = pl.pallas_call(
        _moe_expert_out_kernel,
        grid_spec=pltpu.PrefetchScalarGridSpec(
            num_scalar_prefetch=3,
            grid=(d // bo2, n_tiles),
            in_specs=[pl.BlockSpec((tile, de), lambda n, t, ex, fi, us: (t, 0)),
                      pl.BlockSpec((1, de, bo2), lambda n, t, ex, fi, us: (ex[t], 0, n))],
            out_specs=pl.BlockSpec((tile, bo2), lambda n, t, ex, fi, us: (t, n)),
            scratch_shapes=[pltpu.VMEM((de, bo2), BF16)]),
        out_shape=jax.ShapeDtypeStruct((n_tiles * tile, d), BF16),
        compiler_params=_params("arbitrary", "arbitrary"),
        name="moe_expert_out",
    )(tile_expert, first, tiles_used, hid, w_out)

    y_spec = lambda i: pl.BlockSpec((unit, bo2), lambda b_, n, g, dst: (dst[(b_ * groups + g) * per + i], n))
    return pl.pallas_call(
        functools.partial(_moe_scatter_kernel, per=per),
        grid_spec=pltpu.PrefetchScalarGridSpec(
            num_scalar_prefetch=1,
            grid=(nb, d // bo2, groups),
            in_specs=[pl.BlockSpec((tb, bo2), lambda b_, n, g, dst: (b_, n)),
                      pl.BlockSpec((1, 1, tile), lambda b_, n, g, dst: (b_ * groups + g, 0, 0))]
                     + [y_spec(i) for i in range(per)],
            out_specs=pl.BlockSpec((tb, bo2), lambda b_, n, g, dst: (b_, n)),
            scratch_shapes=[pltpu.VMEM((tile, bo2), BF16)]),
        out_shape=jax.ShapeDtypeStruct((m, d), F32),
        compiler_params=_params("parallel", "parallel", "arbitrary"),
        name="moe_scatter",
    )(dst_unit.reshape(-1), h, row_token.reshape(nb * groups, 1, tile), *([ys] * per))


def ple_gate(h, p_i, norm_pl, pl_proj, pl_gate, layer):
    d = h.shape[1]
    n = rmsnorm(h, norm_pl, name="rmsnorm_ple")
    return matmul_ws(n, [(pl_gate, (layer, 0))], d, epilogue=_ep_ple_gate,
                     extras=[(h, "mn"), (p_i, "m"), (pl_proj, "kn")], name="ple_gate")


def _rw_mix_kernel(u_ref, mu_ref, *o_refs):
    u = u_ref[0]
    row = lax.broadcasted_iota(jnp.int32, u.shape, 0)
    dx = jnp.where(row >= 1, pltpu.roll(u, 1, 0), 0.0) - u
    for j, o_ref in enumerate(o_refs):
        o_ref[0] = (u + dx * mu_ref[j:j + 1, :]).astype(o_ref.dtype)


def rw_token_mix(u, mu):
    bsz, s_len, d = u.shape
    cb = _pick(d, LANES)
    n_mix = mu.shape[0]
    spec = pl.BlockSpec((1, s_len, cb), lambda b_, j: (b_, 0, j))
    return pl.pallas_call(
        _rw_mix_kernel,
        grid=(bsz, d // cb),
        in_specs=[spec, pl.BlockSpec((n_mix, cb), lambda b_, j: (0, j))],
        out_specs=[spec] * n_mix,
        out_shape=[jax.ShapeDtypeStruct(u.shape, BF16)] * n_mix,
        compiler_params=_params("parallel", "parallel"),
        name="rwkv_token_mix",
    )(u, mu)


def _dot_hi(a, b):
    return jnp.dot(a, b, preferred_element_type=F32, precision=lax.Precision.HIGHEST)


def _rw_scan_tile_kernel(r_ref, k_ref, v_ref, a_ref, lw_ref, g_ref, kk_ref, ka_ref, rk_ref, lnw_ref, lnb_ref,
                         o_ref, state_ref, *, chunk, heads, n):
    @pl.when(pl.program_id(2) == 0)
    def _():
        state_ref[...] = jnp.zeros_like(state_ref)

    per = LANES // n
    tiles = range(heads // per)
    sub = range(per)
    ti = lax.broadcasted_iota(jnp.int32, (chunk, chunk), 0)
    si = lax.broadcasted_iota(jnp.int32, (chunk, chunk), 1)
    strict = ti > si
    incl = ti >= si
    lane_seg = lax.broadcasted_iota(jnp.int32, (1, LANES), 1) // n
    seg_is = [lane_seg == j for j in sub]
    same_head = (lax.broadcasted_iota(jnp.int32, (LANES, LANES), 0) // n
                 == lax.broadcasted_iota(jnp.int32, (LANES, LANES), 1) // n)
    dot = functools.partial(jnp.dot, preferred_element_type=F32)
    tile = lambda x, i: x[:, i * LANES:(i + 1) * LANES]

    def pick(vals):
        out = vals[-1]
        for j in range(per - 2, -1, -1):
            out = jnp.where(seg_is[j], vals[j], out)
        return out

    def seg_sum(x):
        return pick([jnp.sum(jnp.where(seg_is[j], x, 0.0), axis=-1, keepdims=True) for j in sub])

    r, k, v, a, lw = r_ref[0], k_ref[0], v_ref[0].astype(F32), a_ref[0], lw_ref[0]
    kk = k * kk_ref[...]
    kmod = k * (1.0 + (a - 1.0) * ka_ref[...])
    cum = _cumsum_rows(lw, chunk)
    cum_end = cum[chunk - 1:chunk, :]
    mid = cum[chunk // 2 - 1:chunk // 2, :]
    bonus_in = r * kmod * rk_ref[...]
    kk_t, bonus_t = [], []
    for i in tiles:
        kki = tile(kk, i)
        kk_t.append(kki / jnp.maximum(jnp.sqrt(seg_sum(kki * kki)), 1e-12))
        bonus_t.append(seg_sum(tile(bonus_in, i)) * tile(v, i))
    kk = jnp.concatenate(kk_t, axis=-1) if len(kk_t) > 1 else kk_t[0]
    kka = kk * a
    e_neg = jnp.exp(mid - cum)
    to_end = jnp.exp(cum_end - cum)
    am = (kk * jnp.exp(cum - lw - mid)).astype(BF16)
    bm = (kka * e_neg).astype(BF16)
    km = (kmod * e_neg).astype(BF16)
    rm = (r * jnp.exp(cum - mid)).astype(BF16)
    a_abs = (kk * jnp.exp(cum - lw)).astype(BF16)
    r_abs = (r * jnp.exp(cum)).astype(BF16)
    k_end = (kmod * to_end).astype(BF16)
    b_end = (kka * to_end).astype(BF16)
    vb = v.astype(BF16)
    st_decay = jnp.exp(cum_end)
    zero = jnp.zeros((), BF16)

    st = [state_ref[i] for i in tiles]
    stb = [s.astype(BF16) for s in st]
    am_h = [[jnp.where(seg_is[j], tile(am, i), zero) for j in sub] for i in tiles]
    rm_h = [[jnp.where(seg_is[j], tile(rm, i), zero) for j in sub] for i in tiles]
    nb = [[(-jnp.where(strict, _dot_nt(am_h[i][j], tile(bm, i)), 0.0)).astype(BF16) for j in sub] for i in tiles]
    lk = [[jnp.where(strict, _dot_nt(am_h[i][j], tile(km, i)), 0.0).astype(BF16) for j in sub] for i in tiles]
    x = [_dot_nt(tile(a_abs, i), stb[i]) + pick([dot(lk[i][j], tile(vb, i)) for j in sub]) for i in tiles]
    xb = [xi.astype(BF16) for xi in x]
    x = [x[i] + pick([dot(nb[i][j], xb[i]) for j in sub]) for i in tiles]
    p = 2
    while p < chunk:
        nb = [[dot(nb[i][j], nb[i][j]).astype(BF16) for j in sub] for i in tiles]
        xb = [xi.astype(BF16) for xi in x]
        x = [x[i] + pick([dot(nb[i][j], xb[i]) for j in sub]) for i in tiles]
        p *= 2
    pb = [xi.astype(BF16) for xi in x]
    mk = [[jnp.where(incl, _dot_nt(rm_h[i][j], tile(km, i)), 0.0).astype(BF16) for j in sub] for i in tiles]
    mb = [[jnp.where(incl, _dot_nt(rm_h[i][j], tile(bm, i)), 0.0).astype(BF16) for j in sub] for i in tiles]
    y = [_dot_nt(tile(r_abs, i), stb[i])
         + pick([dot(mk[i][j], tile(vb, i)) - dot(mb[i][j], pb[i]) for j in sub]) for i in tiles]
    for i in tiles:
        upd = _dot_tn(tile(vb, i), tile(k_end, i)) - _dot_tn(pb[i], tile(b_end, i))
        state_ref[i] = st[i] * tile(st_decay, i) + jnp.where(same_head, upd, 0.0)
    inv_n = 1.0 / n
    for i in tiles:
        cols = slice(i * LANES, (i + 1) * LANES)
        mean = seg_sum(y[i]) * inv_n
        yc = y[i] - mean
        var = seg_sum(yc * yc) * inv_n
        yn = yc * lax.rsqrt(var + RW_LN_EPS) * lnw_ref[:, cols] + lnb_ref[:, cols]
        o_ref[0, :, cols] = ((yn + bonus_t[i]) * g_ref[0, :, cols].astype(F32)).astype(o_ref.dtype)


def rw_scan(r, k, v, a, lw, g, k_k, k_a, r_k, ln_w, ln_b, *, n=RW_HEAD_DIM, chunk=RW_CHUNK, heads=16):
    bsz, s_len, d = r.shape
    chunk = min(chunk, s_len)
    heads = min(heads, d // n)
    hw = heads * n
    seq = pl.BlockSpec((1, chunk, hw), lambda b_, h_, c: (b_, c, h_))
    par = pl.BlockSpec((1, hw), lambda b_, h_, c: (0, h_))
    row = lambda t: t.reshape(1, d)
    assert hw % LANES == 0 and LANES % n == 0
    kern = functools.partial(_rw_scan_tile_kernel, chunk=chunk, heads=heads, n=n)
    return pl.pallas_call(
        kern,
        grid=(bsz, d // hw, s_len // chunk),
        in_specs=[seq] * 6 + [par] * 5,
        out_specs=seq,
        out_shape=jax.ShapeDtypeStruct(r.shape, BF16),
        scratch_shapes=[pltpu.VMEM((hw // LANES, LANES, LANES), F32)],
        compiler_params=_params("parallel", "parallel", "arbitrary"),
        name="rwkv7_scan",
    )(r, k, v, a, lw, g, row(k_k), row(k_a), row(r_k), row(ln_w), row(ln_b))


def rwkv7_mixer(u, h, w, bsz, s_len):
    t, d = u.shape
    xr, xw, xk, xv, xa, xg = [x.reshape(t, d) for x in rw_token_mix(u.reshape(bsz, s_len, d), w["rw_mu"])]
    r = matmul_ws(xr, [(w["rw_w_rkv"], (0, 0))], d, name="rw_r")
    k = matmul_ws(xk, [(w["rw_w_rkv"], (1, 0))], d, name="rw_k")
    v = matmul_ws(xv, [(w["rw_w_rkv"], (2, 0))], d, out_dtype=BF16, name="rw_v")
    row = lambda x: x.reshape(1, d)
    w_lo = matmul(xw, [(w["rw_w1"], 0)], w["rw_w1"].shape[1], epilogue=_ep_tanh, out_dtype=BF16, name="rw_w1")
    wide = 2048
    lw = matmul(w_lo, [(w["rw_w2"], 0)], d, epilogue=_ep_rw_logdecay, extras=[(row(w["rw_w0"]), "n")], bn=wide,
                name="rw_w2")
    a_lo = matmul(xa, [(w["rw_a1"], 0)], w["rw_a1"].shape[1], out_dtype=BF16, name="rw_a1")
    a = matmul(a_lo, [(w["rw_a2"], 0)], d, epilogue=_ep_bias_sigmoid, extras=[(row(w["rw_a0"]), "n")], bn=wide,
               name="rw_a2")
    g_lo = matmul(xg, [(w["rw_g1"], 0)], w["rw_g1"].shape[1], epilogue=_ep_sigmoid, out_dtype=BF16, name="rw_g1")
    g = matmul(g_lo, [(w["rw_g2"], 0)], d, out_dtype=BF16, bn=wide, name="rw_g2")
    shp = (bsz, s_len, d)
    y = rw_scan(r.reshape(shp), k.reshape(shp), v.reshape(shp), a.reshape(shp), lw.reshape(shp), g.reshape(shp),
                w["rw_k_k"], w["rw_k_a"], w["rw_r_k"], w["rw_ln_w"], w["rw_ln_b"])
    return matmul_ws(y.reshape(t, d), [(w["rw_w_out"], 0)], d, epilogue=_ep_residual, extras=[(h, "mn")],
                     name="rw_out")


NEG_BIG = -1e30


def _rope_kernel(x_ref, cc_ref, ss_ref, o_ref, *, n_q_slots, scale):
    x = x_ref[0]
    out = x * cc_ref[...] + pltpu.roll(x, x.shape[-1] // 2, 1) * ss_ref[...]
    out = out * jnp.where(pl.program_id(2) < n_q_slots, scale, 1.0)
    o_ref[0] = out.astype(o_ref.dtype)


def _rope_tables(pos, dim):
    inv = ROPE_THETA ** (-(jnp.arange(0, dim, 2, dtype=F32) / dim))
    ang = pos.astype(F32)[:, None] * inv[None, :]
    cos, sin = jnp.cos(ang), jnp.sin(ang)
    return jnp.concatenate([cos, cos], axis=-1), jnp.concatenate([-sin, sin], axis=-1)


def nsa_rope(proj, n_q_slots, k_slots, dh, scale, tb=512):
    bsz, s_len, _ = proj.shape
    tb = min(tb, s_len)
    cc, ss = _rope_tables(jnp.arange(s_len), dh)
    n_out = n_q_slots + len(k_slots)

    def in_slot(j):
        slot = j
        for idx, ks in enumerate(k_slots):
            slot = jnp.where(j == n_q_slots + idx, ks, slot)
        return slot

    return pl.pallas_call(
        functools.partial(_rope_kernel, n_q_slots=n_q_slots, scale=scale),
        grid=(bsz, s_len // tb, n_out),
        in_specs=[pl.BlockSpec((1, tb, dh), lambda b_, t, j: (b_, t, in_slot(j))),
                  pl.BlockSpec((tb, dh), lambda b_, t, j: (t, 0)),
                  pl.BlockSpec((tb, dh), lambda b_, t, j: (t, 0))],
        out_specs=pl.BlockSpec((1, tb, dh), lambda b_, t, j: (b_, t, j)),
        out_shape=jax.ShapeDtypeStruct((bsz, s_len, n_out * dh), BF16),
        compiler_params=_params("parallel", "parallel", "arbitrary"),
        name="nsa_rope",
    )(proj, cc, ss)


def _cmp_finish_kernel(z_ref, bias_ref, w2_ref, cc_ref, ss_ref, o_ref, *, hidden, rope):
    z = z_ref[0]
    nc = z.shape[0]
    nxt = pltpu.roll(z[:, hidden:], nc - 1, 0)
    hid = _silu(z[:, :hidden] + nxt + bias_ref[...])
    out = jnp.dot(hid.astype(BF16), w2_ref[...], preferred_element_type=F32)
    if rope:
        out = out * cc_ref[...] + pltpu.roll(out, out.shape[-1] // 2, 1) * ss_ref[...]
    o_ref[0] = out.astype(o_ref.dtype)


def nsa_compress(x, pos_emb, w1, w2, bsz, s_len, groups, dh, rope, transpose_out=False):
    stride, blk = NSA_CMP_STRIDE, NSA_CMP_BLOCK
    nc = s_len // stride
    hidden = w1.shape[-1]
    half = stride * dh
    x16 = jnp.transpose(x.reshape(bsz, nc, stride, groups, dh), (0, 3, 1, 2, 4)).reshape(bsz * groups * nc, half)
    w1f = w1.reshape(blk * dh, hidden)
    wcat = jnp.concatenate([w1f[:half], w1f[half:]], axis=1).astype(BF16)
    z = matmul(x16.astype(BF16), [(wcat, 0)], 2 * hidden, name="nsa_cmp_w1")
    bias = matmul(pos_emb.reshape(1, blk * dh).astype(BF16), [(w1f.astype(BF16), 0)], hidden, name="nsa_cmp_pos")
    cc, ss = _rope_tables(jnp.arange(nc) * stride + blk - 1, dh)
    if transpose_out:
        assert not rope
        return pl.pallas_call(
            functools.partial(_cmp_finish_t_kernel, hidden=hidden),
            grid=(bsz * groups,),
            in_specs=[pl.BlockSpec((1, nc, 2 * hidden), lambda i: (i, 0, 0)),
                      pl.BlockSpec((1, hidden), lambda i: (0, 0)),
                      pl.BlockSpec((dh, hidden), lambda i: (0, 0))],
            out_specs=pl.BlockSpec((1, dh, nc), lambda i: (i, 0, 0)),
            out_shape=jax.ShapeDtypeStruct((bsz * groups, dh, nc), BF16),
            compiler_params=_params("parallel"),
            name="nsa_cmp_finish_t",
        )(z.reshape(bsz * groups, nc, 2 * hidden), bias, w2.T.astype(BF16))
    return pl.pallas_call(
        functools.partial(_cmp_finish_kernel, hidden=hidden, rope=rope),
        grid=(bsz * groups,),
        in_specs=[pl.BlockSpec((1, nc, 2 * hidden), lambda i: (i, 0, 0)),
                  pl.BlockSpec((1, hidden), lambda i: (0, 0)),
                  pl.BlockSpec((hidden, dh), lambda i: (0, 0)),
                  pl.BlockSpec((nc, dh), lambda i: (0, 0)),
                  pl.BlockSpec((nc, dh), lambda i: (0, 0))],
        out_specs=pl.BlockSpec((1, nc, dh), lambda i: (i, 0, 0)),
        out_shape=jax.ShapeDtypeStruct((bsz * groups, nc, dh), BF16),
        compiler_params=_params("parallel"),
        name="nsa_cmp_finish",
    )(z.reshape(bsz * groups, nc, 2 * hidden), bias, w2.astype(BF16), cc, ss)


def _rope_t_kernel(x_ref, cc_ref, ss_ref, o_ref, *, n_rope, scale, group, dh):
    first_slot = pl.program_id(2) * group
    for i in range(group):
        x = x_ref[0, :, i * dh:(i + 1) * dh]
        roped = (x * cc_ref[...] + pltpu.roll(x, dh // 2, 1) * ss_ref[...]) * scale
        out = jnp.where(first_slot + i < n_rope, roped, x)
        o_ref[0, i * dh:(i + 1) * dh, :] = out.T.astype(o_ref.dtype)


def nsa_rope_t(proj, slots, n_rope, dh, scale, tb=512, group=4):
    bsz, s_len, _ = proj.shape
    tb = min(tb, s_len)
    cc, ss = _rope_tables(jnp.arange(s_len), dh)
    assert len(slots) % group == 0
    firsts = slots[::group]
    assert all(f % group == 0 and slots[i * group:(i + 1) * group] == list(range(f, f + group))
               for i, f in enumerate(firsts))
    table = jnp.asarray([f // group for f in firsts], jnp.int32)
    grid_spec = pltpu.PrefetchScalarGridSpec(
        num_scalar_prefetch=1,
        grid=(bsz, s_len // tb, len(firsts)),
        in_specs=[pl.BlockSpec((1, tb, group * dh), lambda b_, t, j, tab: (b_, t, tab[j])),
                  pl.BlockSpec((tb, dh), lambda b_, t, j, tab: (t, 0)),
                  pl.BlockSpec((tb, dh), lambda b_, t, j, tab: (t, 0))],
        out_specs=pl.BlockSpec((1, group * dh, tb), lambda b_, t, j, tab: (b_, j, t)),
    )
    kern = lambda tab, x_ref, cc_ref, ss_ref, o_ref: _rope_t_kernel(x_ref, cc_ref, ss_ref, o_ref, n_rope=n_rope,
                                                                   scale=scale, group=group, dh=dh)
    return pl.pallas_call(
        kern,
        grid_spec=grid_spec,
        out_shape=jax.ShapeDtypeStruct((bsz, len(slots) * dh, s_len), BF16),
        compiler_params=_params("parallel", "parallel", "arbitrary"),
        name="nsa_rope_t",
    )(table, proj, cc, ss)


def _cmp_finish_t_kernel(z_ref, bias_ref, w2_ref, o_ref, *, hidden):
    z = z_ref[0]
    nc = z.shape[0]
    nxt = pltpu.roll(z[:, hidden:], nc - 1, 0)
    hid = _silu(z[:, :hidden] + nxt + bias_ref[...])
    o_ref[0] = _dot_nt(w2_ref[...], hid.astype(BF16)).astype(o_ref.dtype)


def _nsa_cmp_select_t_kernel(q_ref, kc_ref, vc_ref, ov_ref, oc_ref, sel_ref, *, tq, rep, dh, topn):
    qi = pl.program_id(2)
    kc = kc_ref[0]
    vct = vc_ref[0]
    nc = kc.shape[0]
    n_sel = sel_ref.shape[2]
    t = qi * tq + lax.broadcasted_iota(jnp.int32, (nc, tq), 1)
    cmp_end = lax.broadcasted_iota(jnp.int32, (nc, tq), 0) * NSA_CMP_STRIDE + (NSA_CMP_BLOCK - 1)
    visible = cmp_end <= t
    s = [jnp.where(visible, jnp.dot(kc, q_ref[0, r * dh:(r + 1) * dh, :], preferred_element_type=F32), NEG_BIG)
         for r in range(rep)]
    e = [jnp.where(visible, jnp.exp2(x - jnp.max(x, axis=0, keepdims=True)), 0.0) for x in s]
    den = [jnp.sum(x, axis=0, keepdims=True) for x in e]
    p = [e[r] / jnp.where(den[r] > 0, den[r], 1.0) for r in range(rep)]
    for r in range(rep):
        oc_ref[0, r * dh:(r + 1) * dh, :] = jnp.dot(vct, p[r].astype(BF16), preferred_element_type=F32)
    psum = p[0]
    for r in range(1, rep):
        psum = psum + p[r]
    imp = _dot_hi(ov_ref[...], psum)
    blk = lax.broadcasted_iota(jnp.int32, (n_sel, tq), 0)
    cur = (qi * tq + lax.broadcasted_iota(jnp.int32, (n_sel, tq), 1)) // NSA_SEL_BLOCK
    forced = (blk == 0) | (blk == cur) | (blk == cur - 1)
    imp = jnp.where(forced, NSA_FORCED_SCORE, imp)
    imp = jnp.where(blk > cur, -jnp.inf, imp)
    sel = jnp.zeros((n_sel, tq), F32)
    for _ in range(topn):
        m = jnp.max(imp, axis=0, keepdims=True)
        first = jnp.min(jnp.where(imp == m, blk, n_sel), axis=0, keepdims=True)
        hit = blk == first
        sel = jnp.where(hit, 1.0, sel)
        imp = jnp.where(hit, -jnp.inf, imp)
    sel_ref[0, 0] = sel


def _flash_t_init(m_ref, l_ref, acc_ref):
    m_ref[...] = jnp.full_like(m_ref, NEG_BIG)
    l_ref[...] = jnp.zeros_like(l_ref)
    acc_ref[...] = jnp.zeros_like(acc_ref)


def _flash_t_step(q_ref, k, vt, mask, m_ref, l_ref, acc_ref, rep, dh):
    hs = range(rep)
    s = [jnp.where(mask, jnp.dot(k, q_ref[0, r * dh:(r + 1) * dh, :], preferred_element_type=F32), NEG_BIG)
         for r in hs]
    m_old = [m_ref[r] for r in hs]
    m_new = [jnp.maximum(m_old[r], jnp.max(s[r], axis=0, keepdims=True)) for r in hs]
    p = [jnp.exp2(s[r] - m_new[r]).astype(BF16) for r in hs]
    alpha = [jnp.exp2(m_old[r] - m_new[r]) for r in hs]
    pv = [jnp.dot(vt, p[r], preferred_element_type=F32) for r in hs]
    ones = jnp.ones((8, k.shape[0]), BF16)
    psum = [jnp.dot(ones, p[r], preferred_element_type=F32)[0:1] for r in hs]
    for r in hs:
        m_ref[r] = m_new[r]
        l_ref[r] = alpha[r] * l_ref[r] + psum[r]
        acc_ref[r] = acc_ref[r] * alpha[r] + pv[r]


def _nsa_select_t_kernel(qi_ref, kj_ref, q_ref, k_ref, vt_ref, sel_ref, o_ref, m_ref, l_ref, acc_ref,
                         *, tq, kb, rep, dh):
    pair = pl.program_id(2)
    qi = qi_ref[pair]
    kj = kj_ref[pair]

    @pl.when(kj == 0)
    def _():
        _flash_t_init(m_ref, l_ref, acc_ref)

    kpos = kj * kb + lax.broadcasted_iota(jnp.int32, (kb, tq), 0)
    t = qi * tq + lax.broadcasted_iota(jnp.int32, (kb, tq), 1)
    per = kb // NSA_SEL_BLOCK
    chosen = jnp.zeros((kb, tq), F32)
    for i in range(per):
        row = sel_ref[0, 0, pl.ds(kj * per + i, 1), :]
        chosen = jnp.where((kpos - kj * kb) // NSA_SEL_BLOCK == i, row, chosen)
    mask = (chosen > 0) & (kpos <= t)
    _flash_t_step(q_ref, k_ref[0], vt_ref[0], mask, m_ref, l_ref, acc_ref, rep, dh)

    @pl.when(kj * kb + kb > qi * tq + tq - 1)
    def _():
        for r in range(rep):
            l = l_ref[r]
            o_ref[0, r * dh:(r + 1) * dh, :] = acc_ref[r] / jnp.where(l > 0, l, 1.0)


def _nsa_window_t_kernel(q_ref, k_ref, vt_ref, oc_ref, os_ref, g_ref, o_ref, m_ref, l_ref, acc_ref,
                         *, tq, kb, rep, dh, window, n_steps):
    qi = pl.program_id(2)
    w = pl.program_id(3)
    kblk = qi * (tq // kb) - (n_steps - tq // kb) + w

    @pl.when(w == 0)
    def _():
        _flash_t_init(m_ref, l_ref, acc_ref)

    @pl.when(kblk >= 0)
    def _():
        kpos = kblk * kb + lax.broadcasted_iota(jnp.int32, (kb, tq), 0)
        t = qi * tq + lax.broadcasted_iota(jnp.int32, (kb, tq), 1)
        mask = (kpos <= t) & (kpos > t - window)
        _flash_t_step(q_ref, k_ref[0], vt_ref[0], mask, m_ref, l_ref, acc_ref, rep, dh)

    @pl.when(w == n_steps - 1)
    def _():
        gates = g_ref[0, 0]
        for r in range(rep):
            rows = slice(r * dh, (r + 1) * dh)
            l = l_ref[r]
            o_w = acc_ref[r] / jnp.where(l > 0, l, 1.0)
            o = (gates[3 * r:3 * r + 1, :] * oc_ref[0, rows, :] + gates[3 * r + 1:3 * r + 2, :] * os_ref[0, rows, :]
                 + gates[3 * r + 2:3 * r + 3, :] * o_w)
            o_ref[0, :, rows] = o.T.astype(o_ref.dtype)


def nsa_mixer_t(u, h, w, bsz, s_len):
    t, d = u.shape
    dh, groups = NSA_HEAD_DIM, NSA_N_KV
    n_heads = d // dh
    rep = n_heads // groups
    kvw = groups * dh
    qw = n_heads * dh
    main_w = qw + 6 * kvw
    scale = dh ** -0.5
    tq = kb = min(256, s_len)
    nq = s_len // tq
    n_sel = s_len // NSA_SEL_BLOCK
    topn = min(NSA_TOPK, n_sel)
    w_in = w["nsa_w_in"]
    proj = matmul_ws(u, [(w_in, 0)], main_w, name="nsa_in").reshape(bsz, s_len, main_w)
    gates = matmul(u, [(w_in[:, main_w:].astype(BF16), 0)], w_in.shape[1] - main_w, epilogue=_ep_sigmoid,
                   name="nsa_gates")
    gates = jnp.transpose(gates.reshape(bsz, s_len, groups, rep * 3), (0, 2, 3, 1))
    slot = lambda j: (qw + j * kvw) // dh
    qvt = nsa_rope_t(proj, list(range(n_heads)) + [slot(3) + g for g in range(groups)]
                     + [slot(5) + g for g in range(groups)], n_heads, dh, scale * math.log2(math.e), group=groups)
    k_rot = nsa_rope(proj, 0, [slot(2) + g for g in range(groups)] + [slot(4) + g for g in range(groups)], dh, 1.0)
    kc = nsa_compress(proj[..., qw:qw + kvw], w["nsa_cmp_pos_k"], w["nsa_cmp_k_w1"], w["nsa_cmp_k_w2"],
                      bsz, s_len, groups, dh, True)
    vct = nsa_compress(proj[..., qw + kvw:qw + 2 * kvw], w["nsa_cmp_pos_v"], w["nsa_cmp_v_w1"], w["nsa_cmp_v_w2"],
                       bsz, s_len, groups, dh, False, transpose_out=True)
    nc = kc.shape[1]
    cs = jnp.arange(nc)[None, :] * NSA_CMP_STRIDE
    ss = jnp.arange(n_sel)[:, None] * NSA_SEL_BLOCK
    overlap_t = jnp.clip(jnp.minimum(cs + NSA_CMP_BLOCK, ss + NSA_SEL_BLOCK) - jnp.maximum(cs, ss), 0, None)
    overlap_t = overlap_t.astype(F32) / NSA_CMP_BLOCK

    qt_spec3 = pl.BlockSpec((1, rep * dh, tq), lambda b_, g, i: (b_, g, i))
    o_c, sel = pl.pallas_call(
        functools.partial(_nsa_cmp_select_t_kernel, tq=tq, rep=rep, dh=dh, topn=topn),
        grid=(bsz, groups, nq),
        in_specs=[qt_spec3,
                  pl.BlockSpec((1, nc, dh), lambda b_, g, i: (b_ * groups + g, 0, 0)),
                  pl.BlockSpec((1, dh, nc), lambda b_, g, i: (b_ * groups + g, 0, 0)),
                  pl.BlockSpec((n_sel, nc), lambda b_, g, i: (0, 0))],
        out_specs=[qt_spec3, pl.BlockSpec((1, 1, n_sel, tq), lambda b_, g, i: (b_, g, 0, i))],
        out_shape=[jax.ShapeDtypeStruct((bsz, qw, s_len), F32),
                   jax.ShapeDtypeStruct((bsz, groups, n_sel, s_len), F32)],
        compiler_params=_params("parallel", "parallel", "parallel"),
        name="nsa_cmp_select",
    )(qvt, kc, vct, overlap_t)

    flash_scratch = lambda n: [pltpu.VMEM((rep, 1, n), F32), pltpu.VMEM((rep, 1, n), F32),
                               pltpu.VMEM((rep, dh, n), F32)]
    tqs = tq
    pairs = [(i, j) for i in range(s_len // tqs) for j in range((i * tqs + tqs - 1) // kb + 1)]
    qi_of = jnp.asarray([pr[0] for pr in pairs], jnp.int32)
    kj_of = jnp.asarray([pr[1] for pr in pairs], jnp.int32)
    o_s = pl.pallas_call(
        functools.partial(_nsa_select_t_kernel, tq=tqs, kb=kb, rep=rep, dh=dh),
        grid_spec=pltpu.PrefetchScalarGridSpec(
            num_scalar_prefetch=2,
            grid=(bsz, groups, len(pairs)),
            in_specs=[pl.BlockSpec((1, rep * dh, tqs), lambda b_, g, pr, qi, kj: (b_, g, qi[pr])),
                      pl.BlockSpec((1, kb, dh), lambda b_, g, pr, qi, kj: (b_, kj[pr], g)),
                      pl.BlockSpec((1, dh, kb), lambda b_, g, pr, qi, kj: (b_, n_heads + g, kj[pr])),
                      pl.BlockSpec((1, 1, n_sel, tqs), lambda b_, g, pr, qi, kj: (b_, g, 0, qi[pr]))],
            out_specs=pl.BlockSpec((1, rep * dh, tqs), lambda b_, g, pr, qi, kj: (b_, g, qi[pr])),
            scratch_shapes=flash_scratch(tqs)),
        out_shape=jax.ShapeDtypeStruct((bsz, qw, s_len), F32),
        compiler_params=_params("parallel", "parallel", "arbitrary"),
        name="nsa_select_attn",
    )(qi_of, kj_of, qvt, k_rot, qvt, sel)

    n_steps = -(-NSA_WINDOW // kb) + tqs // kb
    win_blk = lambda i, j: jnp.maximum(i * (tqs // kb) - (n_steps - tqs // kb) + j, 0)
    qt_spec = pl.BlockSpec((1, rep * dh, tqs), lambda b_, g, i, j: (b_, g, i))
    o = pl.pallas_call(
        functools.partial(_nsa_window_t_kernel, tq=tqs, kb=kb, rep=rep, dh=dh, window=NSA_WINDOW, n_steps=n_steps),
        grid=(bsz, groups, s_len // tqs, n_steps),
        in_specs=[qt_spec,
                  pl.BlockSpec((1, kb, dh), lambda b_, g, i, j: (b_, win_blk(i, j), groups + g)),
                  pl.BlockSpec((1, dh, kb), lambda b_, g, i, j: (b_, n_heads + groups + g, win_blk(i, j))),
                  qt_spec, qt_spec,
                  pl.BlockSpec((1, 1, rep * 3, tqs), lambda b_, g, i, j: (b_, g, 0, i))],
        out_specs=pl.BlockSpec((1, tqs, rep * dh), lambda b_, g, i, j: (b_, i, g)),
        out_shape=jax.ShapeDtypeStruct((bsz, s_len, qw), BF16),
        scratch_shapes=flash_scratch(tqs),
        compiler_params=_params("parallel", "parallel", "parallel", "arbitrary"),
        name="nsa_window_attn",
    )(qvt, k_rot, qvt, o_c, o_s, gates)
    return matmul_ws(o.reshape(t, qw), [(w["nsa_w_out"], 0)], d, epilogue=_ep_residual, extras=[(h, "mn")],
                     name="nsa_out")


_MATMUL_WEIGHTS = ("pl_proj", "rw_w1", "rw_w2", "rw_a1", "rw_a2", "rw_g1", "rw_g2")


def kernel(x, p, norm_mix, norm_ffn, norm_pl, pl_proj, pl_gate, norm_final, mb_w_in, mb_conv_w, mb_conv_b, mb_dt_bias, mb_a_log, mb_d_skip, mb_norm_w, mb_w_out, nsa_w_in, nsa_cmp_pos_k, nsa_cmp_pos_v, nsa_cmp_k_w1, nsa_cmp_k_w2, nsa_cmp_v_w1, nsa_cmp_v_w2, nsa_w_out, hg_w_in, hg_lb_logits, hg_norm_w, hg_w_out, rw_mu, rw_w_rkv, rw_w0, rw_w1, rw_w2, rw_a0, rw_a1, rw_a2, rw_g1, rw_g2, rw_k_k, rw_k_a, rw_r_k, rw_ln_w, rw_ln_b, rw_w_out, ffn0_w_in, ffn0_w_out, moe1_router, moe1_w_in, moe1_w_out, ffn2_w_in, ffn2_w_out, moe3_router, moe3_w_in, moe3_w_out):
    w = dict(locals())
    for name in _MATMUL_WEIGHTS:
        w[name] = w[name].astype(BF16)
    bsz, s_len, d = x.shape
    depth = p.shape[0]
    t = bsz * s_len
    lb_all = jax.nn.softmax(hg_lb_logits.astype(F32), axis=0)
    lb_all = jnp.cumsum(lb_all, axis=0) - lb_all[0]
    dense = [(w["ffn0_w_in"], w["ffn0_w_out"]), (w["ffn2_w_in"], w["ffn2_w_out"])]
    moe = [(moe1_router, w["moe1_w_in"], w["moe1_w_out"]), (moe3_router, w["moe3_w_in"], w["moe3_w_out"])]
    p_bf = p.reshape(depth, t, p.shape[-1])
    h = x.reshape(t, d)
    for i in range(depth):
        kind = i % 4
        if kind == 0:
            h = mamba2_mixer(rmsnorm(h, norm_mix[i]), h, w, bsz, s_len)
        elif kind == 1:
            h = nsa_mixer_t(rmsnorm(h, norm_mix[i]), h, w, bsz, s_len)
        elif kind == 2:
            h = hgrn2_mixer(rmsnorm(h, norm_mix[i]), h, w, lb_all[i], bsz, s_len)
        else:
            h = rwkv7_mixer(rmsnorm(h, norm_mix[i], out_dtype=F32), h, w, bsz, s_len)
        if i % 2 == 0:
            h = dense_ffn(rmsnorm(h, norm_ffn[i]), h, *dense[i // 2])
        else:
            router, w_in, w_out = moe[i // 2]
            v, comb = rmsnorm_router(h, norm_ffn[i], router)
            h = moe_ffn_routed(v, h, comb, w_in, w_out)
        h = ple_gate(h, p_bf[i], norm_pl[i], w["pl_proj"][i], pl_gate, i)
    return rmsnorm(h, norm_final, out_dtype=F32).reshape(bsz, s_len, d)
```
